```python
import math
import jax, jax.numpy as jnp
from jax import lax
import numpy as np

D_MODEL = 1024
BATCH = 8
SEQ = 4096
DEPTH = 4

HEAD_DIM = 64
BLOCK = 128
A_HEADS = 8
A_KV = 2
WINDOW = 128
B_HEADS = 8
B_KV = 2
GRID_W = 64
C_WIDTH = 512
C_GROUPS = 4
C_GROUP_DIM = C_WIDTH // C_GROUPS
CHUNK = 128
N_BRANCH = 3
BRANCH_WIDTH = 512
ROPE_THETA = 10000.0
MEM_LEN = 256
X_HEADS = 4
X_HEAD_DIM = 128
X_WIDTH = X_HEADS * X_HEAD_DIM
D_FF = 2816
CONV_W = 3
ALPHA = (2 * DEPTH) ** 0.25
BETA = (8 * DEPTH) ** -0.25
LN_EPS = 1e-5
RMS_EPS = 1e-6

A_Q_W = A_HEADS * HEAD_DIM
A_KV_W = A_KV * HEAD_DIM
B_Q_W = B_HEADS * HEAD_DIM
B_KV_W = B_KV * HEAD_DIM
IN_SIZES = (A_Q_W, A_KV_W, A_KV_W, B_Q_W, B_KV_W, B_KV_W, 2 * C_WIDTH, N_BRANCH * D_MODEL)
D_IN = sum(IN_SIZES)

kernel_name = "hybrid_gated_window_axial_gmlp_encoder"


def _split_points(sizes):
    pts, acc = [], 0
    for s in sizes[:-1]:
        acc += s
        pts.append(acc)
    return pts


def layer_norm(x, g, b):
    xf = x.astype(jnp.float32)
    mu = jnp.mean(xf, axis=-1, keepdims=True)
    var = jnp.mean(jnp.square(xf - mu), axis=-1, keepdims=True)
    y = (xf - mu) * lax.rsqrt(var + LN_EPS) * g.astype(jnp.float32) + b.astype(jnp.float32)
    return y.astype(x.dtype)


def rms_norm(x, g):
    xf = x.astype(jnp.float32)
    y = xf * lax.rsqrt(jnp.mean(jnp.square(xf), axis=-1, keepdims=True) + RMS_EPS)
    return (y * g.astype(jnp.float32)).astype(x.dtype)


def rope(x, pos, theta):
    d = x.shape[-1]
    half = d // 2
    inv = theta ** (-jnp.arange(half, dtype=jnp.float32) * (2.0 / d))
    ang = pos.astype(jnp.float32)[:, None] * inv[None, :]
    cos = jnp.cos(ang)[:, None, :]
    sin = jnp.sin(ang)[:, None, :]
    x1 = x[..., :half].astype(jnp.float32)
    x2 = x[..., half:].astype(jnp.float32)
    out = jnp.concatenate([x1 * cos - x2 * sin, x2 * cos + x1 * sin], axis=-1)
    return out.astype(x.dtype)


def window_attention(q, k, v, sink):
    B, S, H, D = q.shape
    KV = k.shape[2]
    G = H // KV
    nb = S // BLOCK
    qb = q.reshape(B, nb, BLOCK, KV, G, D)
    pad = ((0, 0), (BLOCK, BLOCK), (0, 0), (0, 0))
    kp = jnp.pad(k, pad).reshape(B, nb + 2, BLOCK, KV, D)
    vp = jnp.pad(v, pad).reshape(B, nb + 2, BLOCK, KV, D)
    kw = jnp.concatenate([kp[:, :-2], kp[:, 1:-1], kp[:, 2:]], axis=2)
    vw = jnp.concatenate([vp[:, :-2], vp[:, 1:-1], vp[:, 2:]], axis=2)
    s = jnp.einsum('bnqkgd,bnskd->bnkgqs', qb, kw).astype(jnp.float32) / math.sqrt(D)
    blk = jnp.arange(nb, dtype=jnp.int32)[:, None]
    qpos = blk * BLOCK + jnp.arange(BLOCK, dtype=jnp.int32)[None, :]
    kpos = (blk - 1) * BLOCK + jnp.arange(3 * BLOCK, dtype=jnp.int32)[None, :]
    dist = jnp.abs(qpos[:, :, None] - kpos[:, None, :])
    valid = (dist <= WINDOW) & (kpos[:, None, :] >= 0) & (kpos[:, None, :] < S)
    s = jnp.where(valid[None, :, None, None], s, -jnp.inf)
    sink_l = jnp.broadcast_to(sink.astype(jnp.float32).reshape(KV, G)[None, None, :, :, None, None],
                              s.shape[:-1] + (1,))
    p = jax.nn.softmax(jnp.concatenate([s, sink_l], axis=-1), axis=-1)[..., :-1]
    o = jnp.einsum('bnkgqs,bnskd->bnqkgd', p.astype(v.dtype), vw)
    return o.reshape(B, S, H * D)


def dense_block_attention(q, k, v):
    B, S, H, D = q.shape
    KV = k.shape[2]
    G = H // KV
    nb = S // BLOCK
    scale = 1.0 / math.sqrt(D)
    qb = q.reshape(B, nb, BLOCK, KV, G, D).transpose(1, 0, 2, 3, 4, 5)

    def one_block(qblk):
        s = jnp.einsum('bqkgd,bskd->bkgqs', qblk, k).astype(jnp.float32) * scale
        p = jax.nn.softmax(s, axis=-1)
        return jnp.einsum('bkgqs,bskd->bqkgd', p.astype(v.dtype), v)

    o = lax.map(one_block, qb)
    return o.transpose(1, 0, 2, 3, 4, 5).reshape(B, S, H * D)


def spatial_gating(u, v, ln_g, ln_b, w_s, b_s):
    B, S, _ = v.shape
    nc = S // CHUNK
    vc = layer_norm(v, ln_g, ln_b).reshape(B, nc, CHUNK, C_GROUPS, C_GROUP_DIM)
    mixed = jnp.einsum('gij,bnjgc->bnigc', w_s, vc) + b_s.T[None, None, :, :, None]
    return u * mixed.reshape(B, S, C_WIDTH)


def hybrid_mixer(x, pos, row, col, w_in, b_gate, a_sink, b_q_gain, b_k_gain,
                 c_ln_g, c_ln_b, c_ws, c_bs, w_branch, w_mix_out):
    B, S, _ = x.shape
    proj = x @ w_in
    aq, ak, av, bq, bk, bv, cz, gate = jnp.split(proj, _split_points(IN_SIZES), axis=-1)
    aq = rope(aq.reshape(B, S, A_HEADS, HEAD_DIM), pos, ROPE_THETA)
    ak = rope(ak.reshape(B, S, A_KV, HEAD_DIM), pos, ROPE_THETA)
    av = av.reshape(B, S, A_KV, HEAD_DIM)
    out_a = window_attention(aq, ak, av, a_sink)
    half = HEAD_DIM // 2
    bq = rms_norm(bq.reshape(B, S, B_HEADS, HEAD_DIM), b_q_gain)
    bk = rms_norm(bk.reshape(B, S, B_KV, HEAD_DIM), b_k_gain)
    bq = jnp.concatenate([rope(bq[..., :half], row, ROPE_THETA), rope(bq[..., half:], col, ROPE_THETA)], axis=-1)
    bk = jnp.concatenate([rope(bk[..., :half], row, ROPE_THETA), rope(bk[..., half:], col, ROPE_THETA)], axis=-1)
    bv = bv.reshape(B, S, B_KV, HEAD_DIM)
    out_b = dense_block_attention(bq, bk, bv)
    u, v = jnp.split(jax.nn.gelu(cz, approximate=False), 2, axis=-1)
    out_c = spatial_gating(u, v, c_ln_g, c_ln_b, c_ws, c_bs)
    branches = jnp.stack([out_a, out_b, out_c], axis=2)
    gates = jax.nn.sigmoid(gate + b_gate).reshape(B, S, N_BRANCH, D_MODEL)
    merged = (jnp.einsum('bsnc,ncd->bsnd', branches, w_branch) * gates).sum(axis=2)
    return merged @ w_mix_out


def memory_cross_attention(x, mem, wq, wkv, wo):
    B, S, _ = x.shape
    M = mem.shape[1]
    q = (x @ wq).reshape(B, S, X_HEADS, X_HEAD_DIM)
    k, v = jnp.split(mem @ wkv, 2, axis=-1)
    k = k.reshape(B, M, X_HEADS, X_HEAD_DIM)
    v = v.reshape(B, M, X_HEADS, X_HEAD_DIM)
    s = jnp.einsum('bqhd,bmhd->bhqm', q, k).astype(jnp.float32) / math.sqrt(X_HEAD_DIM)
    p = jax.nn.softmax(s, axis=-1)
    o = jnp.einsum('bhqm,bmhd->bqhd', p.astype(v.dtype), v)
    return o.reshape(B, S, X_WIDTH) @ wo


def conv_ffn(x, w_up, conv_k, conv_b, w_down):
    S = x.shape[1]
    h = x @ w_up
    r = CONV_W // 2
    hp = jnp.pad(h, ((0, 0), (r, r), (0, 0)))
    h = sum(hp[:, j:j + S] * conv_k[j] for j in range(CONV_W)) + conv_b
    a, b = jnp.split(h, 2, axis=-1)
    return (jax.nn.gelu(a, approximate=False) * b) @ w_down


def _fwd_setup_inputs(seed: int = 0) -> dict:
    key = jax.random.key(seed)
    ks = jax.random.split(key, 26)

    def nrm(k, shape, scale):
        return jax.random.normal(k, shape, dtype=jnp.float32) * scale

    L = DEPTH
    return {
        "x": nrm(ks[0], (BATCH, SEQ, D_MODEL), 1.0),
        "mem": nrm(ks[1], (BATCH, MEM_LEN, D_MODEL), 1.0),
        "w_in": nrm(ks[2], (L, D_MODEL, D_IN), D_MODEL ** -0.5),
        "b_gate": nrm(ks[3], (L, N_BRANCH * D_MODEL), 0.1),
        "a_sink": nrm(ks[4], (L, A_HEADS), 1.0),
        "b_q_gain": 1.0 + nrm(ks[5], (L, HEAD_DIM), 0.02),
        "b_k_gain": 1.0 + nrm(ks[6], (L, HEAD_DIM), 0.02),
        "c_ln_g": 1.0 + nrm(ks[7], (L, C_WIDTH), 0.02),
        "c_ln_b": nrm(ks[8], (L, C_WIDTH), 0.02),
        "c_ws": nrm(ks[9], (L, C_GROUPS, CHUNK, CHUNK), CHUNK ** -0.5),
        "c_bs": 1.0 + nrm(ks[10], (L, C_GROUPS, CHUNK), 0.02),
        "w_branch": nrm(ks[11], (L, N_BRANCH, BRANCH_WIDTH, D_MODEL), BRANCH_WIDTH ** -0.5),
        "w_mix_out": nrm(ks[12], (L, D_MODEL, D_MODEL), BETA * D_MODEL ** -0.5),
        "ln1_g": 1.0 + nrm(ks[13], (L, D_MODEL), 0.02),
        "ln1_b": nrm(ks[14], (L, D_MODEL), 0.02),
        "x_wq": nrm(ks[15], (L, D_MODEL, X_WIDTH), D_MODEL ** -0.5),
        "x_wkv": nrm(ks[16], (L, D_MODEL, 2 * X_WIDTH), D_MODEL ** -0.5),
        "x_wo": nrm(ks[17], (L, X_WIDTH, D_MODEL), BETA * X_WIDTH ** -0.5),
        "ln2_g": 1.0 + nrm(ks[18], (L, D_MODEL), 0.02),
        "ln2_b": nrm(ks[19], (L, D_MODEL), 0.02),
        "f_w_up": nrm(ks[20], (L, D_MODEL, 2 * D_FF), D_MODEL ** -0.5),
        "f_conv_k": nrm(ks[21], (L, CONV_W, 2 * D_FF), CONV_W ** -0.5),
        "f_conv_b": nrm(ks[22], (L, 2 * D_FF), 0.02),
        "f_w_down": nrm(ks[23], (L, D_FF, D_MODEL), BETA * D_FF ** -0.5),
        "ln3_g": 1.0 + nrm(ks[24], (L, D_MODEL), 0.02),
        "ln3_b": nrm(ks[25], (L, D_MODEL), 0.02),
    }


def _fwd_reference(x, mem, w_in, b_gate, a_sink, b_q_gain, b_k_gain, c_ln_g, c_ln_b, c_ws, c_bs,
              w_branch, w_mix_out, ln1_g, ln1_b, x_wq, x_wkv, x_wo, ln2_g, ln2_b,
              f_w_up, f_conv_k, f_conv_b, f_w_down, ln3_g, ln3_b):
    seq = x.shape[1]
    rows = seq // GRID_W
    pos = jnp.arange(seq, dtype=jnp.int32)
    row = jnp.repeat(jnp.arange(rows, dtype=jnp.int32), GRID_W)
    col = jnp.tile(jnp.arange(GRID_W, dtype=jnp.int32), rows)
    for l in range(DEPTH):
        h = hybrid_mixer(x, pos, row, col, w_in[l], b_gate[l], a_sink[l], b_q_gain[l], b_k_gain[l],
                         c_ln_g[l], c_ln_b[l], c_ws[l], c_bs[l], w_branch[l], w_mix_out[l])
        x = layer_norm(ALPHA * x + h, ln1_g[l], ln1_b[l])
        h = memory_cross_attention(x, mem, x_wq[l], x_wkv[l], x_wo[l])
        x = layer_norm(ALPHA * x + h, ln2_g[l], ln2_b[l])
        h = conv_ffn(x, f_w_up[l], f_conv_k[l], f_conv_b[l], f_w_down[l])
        x = layer_norm(ALPHA * x + h, ln3_g[l], ln3_b[l])
    return x


import jax as _jax
import jax.numpy as _jnp

TWIN_FORMAT = 'train_step'
FWD_PARAMS = ['x', 'mem', 'w_in', 'b_gate', 'a_sink', 'b_q_gain', 'b_k_gain', 'c_ln_g', 'c_ln_b', 'c_ws', 'c_bs', 'w_branch', 'w_mix_out', 'ln1_g', 'ln1_b', 'x_wq', 'x_wkv', 'x_wo', 'ln2_g', 'ln2_b', 'f_w_up', 'f_conv_k', 'f_conv_b', 'f_w_down', 'ln3_g', 'ln3_b']
TWIN_WEIGHTS = ['w_in', 'b_gate', 'a_sink', 'b_q_gain', 'b_k_gain', 'c_ln_g', 'c_ln_b', 'c_ws', 'c_bs', 'w_branch', 'w_mix_out', 'ln1_g', 'ln1_b', 'x_wq', 'x_wkv', 'x_wo', 'ln2_g', 'ln2_b', 'f_w_up', 'f_conv_k', 'f_conv_b', 'f_w_down', 'ln3_g', 'ln3_b']
TWIN_DIFF_INPUT = 'x'
TWIN_INPUTS = ['x', 'mem', 'w_in', 'b_gate', 'a_sink', 'b_q_gain', 'b_k_gain', 'c_ln_g', 'c_ln_b', 'c_ws', 'c_bs', 'w_branch', 'w_mix_out', 'ln1_g', 'ln1_b', 'x_wq', 'x_wkv', 'x_wo', 'ln2_g', 'ln2_b', 'f_w_up', 'f_conv_k', 'f_conv_b', 'f_w_down', 'ln3_g', 'ln3_b', 'loss_target', 'm_w_in', 'm_b_gate', 'm_a_sink', 'm_b_q_gain', 'm_b_k_gain', 'm_c_ln_g', 'm_c_ln_b', 'm_c_ws', 'm_c_bs', 'm_w_branch', 'm_w_mix_out', 'm_ln1_g', 'm_ln1_b', 'm_x_wq', 'm_x_wkv', 'm_x_wo', 'm_ln2_g', 'm_ln2_b', 'm_f_w_up', 'm_f_conv_k', 'm_f_conv_b', 'm_f_w_down', 'm_ln3_g', 'm_ln3_b', 'v_w_in', 'v_b_gate', 'v_a_sink', 'v_b_q_gain', 'v_b_k_gain', 'v_c_ln_g', 'v_c_ln_b', 'v_c_ws', 'v_c_bs', 'v_w_branch', 'v_w_mix_out', 'v_ln1_g', 'v_ln1_b', 'v_x_wq', 'v_x_wkv', 'v_x_wo', 'v_ln2_g', 'v_ln2_b', 'v_f_w_up', 'v_f_conv_k', 'v_f_conv_b', 'v_f_w_down', 'v_ln3_g', 'v_ln3_b']
TWIN_OUTPUTS = ['loss', 'grad_x', 'grad_w_in', 'grad_b_gate', 'grad_a_sink', 'grad_b_q_gain', 'grad_b_k_gain', 'grad_c_ln_g', 'grad_c_ln_b', 'grad_c_ws', 'grad_c_bs', 'grad_w_branch', 'grad_w_mix_out', 'grad_ln1_g', 'grad_ln1_b', 'grad_x_wq', 'grad_x_wkv', 'grad_x_wo', 'grad_ln2_g', 'grad_ln2_b', 'grad_f_w_up', 'grad_f_conv_k', 'grad_f_conv_b', 'grad_f_w_down', 'grad_ln3_g', 'grad_ln3_b', 'delta_w_in', 'delta_b_gate', 'delta_a_sink', 'delta_b_q_gain', 'delta_b_k_gain', 'delta_c_ln_g', 'delta_c_ln_b', 'delta_c_ws', 'delta_c_bs', 'delta_w_branch', 'delta_w_mix_out', 'delta_ln1_g', 'delta_ln1_b', 'delta_x_wq', 'delta_x_wkv', 'delta_x_wo', 'delta_ln2_g', 'delta_ln2_b', 'delta_f_w_up', 'delta_f_conv_k', 'delta_f_conv_b', 'delta_f_w_down', 'delta_ln3_g', 'delta_ln3_b', 'new_m_w_in', 'new_m_b_gate', 'new_m_a_sink', 'new_m_b_q_gain', 'new_m_b_k_gain', 'new_m_c_ln_g', 'new_m_c_ln_b', 'new_m_c_ws', 'new_m_c_bs', 'new_m_w_branch', 'new_m_w_mix_out', 'new_m_ln1_g', 'new_m_ln1_b', 'new_m_x_wq', 'new_m_x_wkv', 'new_m_x_wo', 'new_m_ln2_g', 'new_m_ln2_b', 'new_m_f_w_up', 'new_m_f_conv_k', 'new_m_f_conv_b', 'new_m_f_w_down', 'new_m_ln3_g', 'new_m_ln3_b', 'new_v_w_in', 'new_v_b_gate', 'new_v_a_sink', 'new_v_b_q_gain', 'new_v_b_k_gain', 'new_v_c_ln_g', 'new_v_c_ln_b', 'new_v_c_ws', 'new_v_c_bs', 'new_v_w_branch', 'new_v_w_mix_out', 'new_v_ln1_g', 'new_v_ln1_b', 'new_v_x_wq', 'new_v_x_wkv', 'new_v_x_wo', 'new_v_ln2_g', 'new_v_ln2_b', 'new_v_f_w_up', 'new_v_f_conv_k', 'new_v_f_conv_b', 'new_v_f_w_down', 'new_v_ln3_g', 'new_v_ln3_b']
TWIN_LEAF_KINDS = {'loss': 'loss', 'grad_x': 'grad_x', 'grad_w_in': 'grad_w', 'grad_b_gate': 'grad_w', 'grad_a_sink': 'grad_w', 'grad_b_q_gain': 'grad_w', 'grad_b_k_gain': 'grad_w', 'grad_c_ln_g': 'grad_w', 'grad_c_ln_b': 'grad_w', 'grad_c_ws': 'grad_w', 'grad_c_bs': 'grad_w', 'grad_w_branch': 'grad_w', 'grad_w_mix_out': 'grad_w', 'grad_ln1_g': 'grad_w', 'grad_ln1_b': 'grad_w', 'grad_x_wq': 'grad_w', 'grad_x_wkv': 'grad_w', 'grad_x_wo': 'grad_w', 'grad_ln2_g': 'grad_w', 'grad_ln2_b': 'grad_w', 'grad_f_w_up': 'grad_w', 'grad_f_conv_k': 'grad_w', 'grad_f_conv_b': 'grad_w', 'grad_f_w_down': 'grad_w', 'grad_ln3_g': 'grad_w', 'grad_ln3_b': 'grad_w', 'delta_w_in': 'delta_w', 'delta_b_gate': 'delta_w', 'delta_a_sink': 'delta_w', 'delta_b_q_gain': 'delta_w', 'delta_b_k_gain': 'delta_w', 'delta_c_ln_g': 'delta_w', 'delta_c_ln_b': 'delta_w', 'delta_c_ws': 'delta_w', 'delta_c_bs': 'delta_w', 'delta_w_branch': 'delta_w', 'delta_w_mix_out': 'delta_w', 'delta_ln1_g': 'delta_w', 'delta_ln1_b': 'delta_w', 'delta_x_wq': 'delta_w', 'delta_x_wkv': 'delta_w', 'delta_x_wo': 'delta_w', 'delta_ln2_g': 'delta_w', 'delta_ln2_b': 'delta_w', 'delta_f_w_up': 'delta_w', 'delta_f_conv_k': 'delta_w', 'delta_f_conv_b': 'delta_w', 'delta_f_w_down': 'delta_w', 'delta_ln3_g': 'delta_w', 'delta_ln3_b': 'delta_w', 'new_m_w_in': 'new_m', 'new_m_b_gate': 'new_m', 'new_m_a_sink': 'new_m', 'new_m_b_q_gain': 'new_m', 'new_m_b_k_gain': 'new_m', 'new_m_c_ln_g': 'new_m', 'new_m_c_ln_b': 'new_m', 'new_m_c_ws': 'new_m', 'new_m_c_bs': 'new_m', 'new_m_w_branch': 'new_m', 'new_m_w_mix_out': 'new_m', 'new_m_ln1_g': 'new_m', 'new_m_ln1_b': 'new_m', 'new_m_x_wq': 'new_m', 'new_m_x_wkv': 'new_m', 'new_m_x_wo': 'new_m', 'new_m_ln2_g': 'new_m', 'new_m_ln2_b': 'new_m', 'new_m_f_w_up': 'new_m', 'new_m_f_conv_k': 'new_m', 'new_m_f_conv_b': 'new_m', 'new_m_f_w_down': 'new_m', 'new_m_ln3_g': 'new_m', 'new_m_ln3_b': 'new_m', 'new_v_w_in': 'new_v', 'new_v_b_gate': 'new_v', 'new_v_a_sink': 'new_v', 'new_v_b_q_gain': 'new_v', 'new_v_b_k_gain': 'new_v', 'new_v_c_ln_g': 'new_v', 'new_v_c_ln_b': 'new_v', 'new_v_c_ws': 'new_v', 'new_v_c_bs': 'new_v', 'new_v_w_branch': 'new_v', 'new_v_w_mix_out': 'new_v', 'new_v_ln1_g': 'new_v', 'new_v_ln1_b': 'new_v', 'new_v_x_wq': 'new_v', 'new_v_x_wkv': 'new_v', 'new_v_x_wo': 'new_v', 'new_v_ln2_g': 'new_v', 'new_v_ln2_b': 'new_v', 'new_v_f_w_up': 'new_v', 'new_v_f_conv_k': 'new_v', 'new_v_f_conv_b': 'new_v', 'new_v_f_w_down': 'new_v', 'new_v_ln3_g': 'new_v', 'new_v_ln3_b': 'new_v'}


def _forward(args):
    return _fwd_reference(*[args[k] for k in FWD_PARAMS])


def _output_shape():
    def fwd():
        inp = _fwd_setup_inputs(0)
        return _fwd_reference(*[inp[k] for k in FWD_PARAMS])
    out = _jax.eval_shape(fwd)
    return out.shape, out.dtype

N_MICROBATCH = 1
ADAM_LR = 0.001
ADAM_B1 = 0.9
ADAM_B2 = 0.999
ADAM_EPS = 1e-08
ADAM_WD = 0.01
ADAM_STEP = 10
PER_EXAMPLE_BATCH_AXIS = {'x': 0, 'mem': 0, 'loss_target': 0}
SHARED_INPUTS = []
_WEIGHT_DTYPES = {'w_in': _jnp.float32, 'b_gate': _jnp.float32, 'a_sink': _jnp.float32, 'b_q_gain': _jnp.float32, 'b_k_gain': _jnp.float32, 'c_ln_g': _jnp.float32, 'c_ln_b': _jnp.float32, 'c_ws': _jnp.float32, 'c_bs': _jnp.float32, 'w_branch': _jnp.float32, 'w_mix_out': _jnp.float32, 'ln1_g': _jnp.float32, 'ln1_b': _jnp.float32, 'x_wq': _jnp.float32, 'x_wkv': _jnp.float32, 'x_wo': _jnp.float32, 'ln2_g': _jnp.float32, 'ln2_b': _jnp.float32, 'f_w_up': _jnp.float32, 'f_conv_k': _jnp.float32, 'f_conv_b': _jnp.float32, 'f_w_down': _jnp.float32, 'ln3_g': _jnp.float32, 'ln3_b': _jnp.float32}
MOMENT_SCALE = {'w_in': 1.380022e-02, 'b_gate': 6.406085e-03, 'a_sink': 2.204700e-04, 'b_q_gain': 9.121952e-03, 'b_k_gain': 9.301044e-03, 'c_ln_g': 2.351070e-02, 'c_ln_b': 2.234686e-02, 'c_ws': 2.291877e-02, 'c_bs': 2.317526e-02, 'w_branch': 1.648541e-02, 'w_mix_out': 6.855101e-02, 'ln1_g': 1.130674e+00, 'ln1_b': 5.816819e-01, 'x_wq': 6.882596e-03, 'x_wkv': 7.348069e-03, 'x_wo': 1.317512e-02, 'ln2_g': 1.131493e+00, 'ln2_b': 5.821397e-01, 'f_w_up': 1.840792e-02, 'f_conv_k': 1.840713e-02, 'f_conv_b': 2.407148e-02, 'f_w_down': 7.187534e-02, 'ln3_g': 1.615727e+01, 'ln3_b': 1.564161e+00}


def _to_microbatches(a, axis):
    t = _jnp.moveaxis(a, axis, 0)
    t = t.reshape((N_MICROBATCH, t.shape[0] // N_MICROBATCH) + t.shape[1:])
    return _jnp.moveaxis(t, 1, axis + 1)


def setup_inputs(seed: int = 0) -> dict:
    inp = _fwd_setup_inputs(seed)
    key = _jax.random.fold_in(_jax.random.key(seed), 7919)
    shape, _ = _output_shape()
    out = dict(inp)
    out["loss_target"] = _jax.random.normal(_jax.random.fold_in(key, 0), shape, _jnp.float32)
    for i, name in enumerate(TWIN_WEIGHTS):
        w = inp[name].astype(_jnp.float32)
        if MOMENT_SCALE is None:
            s = _jnp.sqrt(_jnp.mean(_jnp.square(w)) + 1e-30)
        else:
            s = MOMENT_SCALE[name]
        km, kv = _jax.random.split(_jax.random.fold_in(key, i + 1))
        out[name] = w
        out["m_" + name] = s * _jax.random.normal(km, w.shape, _jnp.float32)
        out["v_" + name] = (s * s) * _jax.random.uniform(kv, w.shape, _jnp.float32, 0.5, 1.5)
    if N_MICROBATCH > 1:
        for name, axis in PER_EXAMPLE_BATCH_AXIS.items():
            out[name] = _to_microbatches(out[name], axis)
    return {'x': out['x'], 'mem': out['mem'], 'w_in': out['w_in'], 'b_gate': out['b_gate'], 'a_sink': out['a_sink'], 'b_q_gain': out['b_q_gain'], 'b_k_gain': out['b_k_gain'], 'c_ln_g': out['c_ln_g'], 'c_ln_b': out['c_ln_b'], 'c_ws': out['c_ws'], 'c_bs': out['c_bs'], 'w_branch': out['w_branch'], 'w_mix_out': out['w_mix_out'], 'ln1_g': out['ln1_g'], 'ln1_b': out['ln1_b'], 'x_wq': out['x_wq'], 'x_wkv': out['x_wkv'], 'x_wo': out['x_wo'], 'ln2_g': out['ln2_g'], 'ln2_b': out['ln2_b'], 'f_w_up': out['f_w_up'], 'f_conv_k': out['f_conv_k'], 'f_conv_b': out['f_conv_b'], 'f_w_down': out['f_w_down'], 'ln3_g': out['ln3_g'], 'ln3_b': out['ln3_b'], 'loss_target': out['loss_target'], 'm_w_in': out['m_w_in'], 'm_b_gate': out['m_b_gate'], 'm_a_sink': out['m_a_sink'], 'm_b_q_gain': out['m_b_q_gain'], 'm_b_k_gain': out['m_b_k_gain'], 'm_c_ln_g': out['m_c_ln_g'], 'm_c_ln_b': out['m_c_ln_b'], 'm_c_ws': out['m_c_ws'], 'm_c_bs': out['m_c_bs'], 'm_w_branch': out['m_w_branch'], 'm_w_mix_out': out['m_w_mix_out'], 'm_ln1_g': out['m_ln1_g'], 'm_ln1_b': out['m_ln1_b'], 'm_x_wq': out['m_x_wq'], 'm_x_wkv': out['m_x_wkv'], 'm_x_wo': out['m_x_wo'], 'm_ln2_g': out['m_ln2_g'], 'm_ln2_b': out['m_ln2_b'], 'm_f_w_up': out['m_f_w_up'], 'm_f_conv_k': out['m_f_conv_k'], 'm_f_conv_b': out['m_f_conv_b'], 'm_f_w_down': out['m_f_w_down'], 'm_ln3_g': out['m_ln3_g'], 'm_ln3_b': out['m_ln3_b'], 'v_w_in': out['v_w_in'], 'v_b_gate': out['v_b_gate'], 'v_a_sink': out['v_a_sink'], 'v_b_q_gain': out['v_b_q_gain'], 'v_b_k_gain': out['v_b_k_gain'], 'v_c_ln_g': out['v_c_ln_g'], 'v_c_ln_b': out['v_c_ln_b'], 'v_c_ws': out['v_c_ws'], 'v_c_bs': out['v_c_bs'], 'v_w_branch': out['v_w_branch'], 'v_w_mix_out': out['v_w_mix_out'], 'v_ln1_g': out['v_ln1_g'], 'v_ln1_b': out['v_ln1_b'], 'v_x_wq': out['v_x_wq'], 'v_x_wkv': out['v_x_wkv'], 'v_x_wo': out['v_x_wo'], 'v_ln2_g': out['v_ln2_g'], 'v_ln2_b': out['v_ln2_b'], 'v_f_w_up': out['v_f_w_up'], 'v_f_conv_k': out['v_f_conv_k'], 'v_f_conv_b': out['v_f_conv_b'], 'v_f_w_down': out['v_f_w_down'], 'v_ln3_g': out['v_ln3_g'], 'v_ln3_b': out['v_ln3_b']}


def _loss(weights, diff, rest, loss_target):
    with _jax.named_scope("forward"):
        args = {**rest, TWIN_DIFF_INPUT: diff, **{k: w.astype(_WEIGHT_DTYPES[k]) for k, w in weights.items()}}
        y = _forward(args)
    with _jax.named_scope("loss_head"):
        err = _jnp.square(y.astype(_jnp.float32) - loss_target)
        return 0.5 * _jnp.sum(_jnp.mean(err, axis=-1)) if err.ndim else 0.5 * err


def _adamw(w, g, m, v):
    m = ADAM_B1 * m + (1.0 - ADAM_B1) * g
    v = ADAM_B2 * v + (1.0 - ADAM_B2) * _jnp.square(g)
    m_hat = m / (1.0 - ADAM_B1 ** ADAM_STEP)
    v_hat = v / (1.0 - ADAM_B2 ** ADAM_STEP)
    delta = -ADAM_LR * (m_hat / (_jnp.sqrt(v_hat) + ADAM_EPS) + ADAM_WD * w)
    return delta, m, v


def reference(x, mem, w_in, b_gate, a_sink, b_q_gain, b_k_gain, c_ln_g, c_ln_b, c_ws, c_bs, w_branch, w_mix_out, ln1_g, ln1_b, x_wq, x_wkv, x_wo, ln2_g, ln2_b, f_w_up, f_conv_k, f_conv_b, f_w_down, ln3_g, ln3_b, loss_target, m_w_in, m_b_gate, m_a_sink, m_b_q_gain, m_b_k_gain, m_c_ln_g, m_c_ln_b, m_c_ws, m_c_bs, m_w_branch, m_w_mix_out, m_ln1_g, m_ln1_b, m_x_wq, m_x_wkv, m_x_wo, m_ln2_g, m_ln2_b, m_f_w_up, m_f_conv_k, m_f_conv_b, m_f_w_down, m_ln3_g, m_ln3_b, v_w_in, v_b_gate, v_a_sink, v_b_q_gain, v_b_k_gain, v_c_ln_g, v_c_ln_b, v_c_ws, v_c_bs, v_w_branch, v_w_mix_out, v_ln1_g, v_ln1_b, v_x_wq, v_x_wkv, v_x_wo, v_ln2_g, v_ln2_b, v_f_w_up, v_f_conv_k, v_f_conv_b, v_f_w_down, v_ln3_g, v_ln3_b):
    given = dict(x=x, mem=mem, w_in=w_in, b_gate=b_gate, a_sink=a_sink, b_q_gain=b_q_gain, b_k_gain=b_k_gain, c_ln_g=c_ln_g, c_ln_b=c_ln_b, c_ws=c_ws, c_bs=c_bs, w_branch=w_branch, w_mix_out=w_mix_out, ln1_g=ln1_g, ln1_b=ln1_b, x_wq=x_wq, x_wkv=x_wkv, x_wo=x_wo, ln2_g=ln2_g, ln2_b=ln2_b, f_w_up=f_w_up, f_conv_k=f_conv_k, f_conv_b=f_conv_b, f_w_down=f_w_down, ln3_g=ln3_g, ln3_b=ln3_b, loss_target=loss_target, m_w_in=m_w_in, m_b_gate=m_b_gate, m_a_sink=m_a_sink, m_b_q_gain=m_b_q_gain, m_b_k_gain=m_b_k_gain, m_c_ln_g=m_c_ln_g, m_c_ln_b=m_c_ln_b, m_c_ws=m_c_ws, m_c_bs=m_c_bs, m_w_branch=m_w_branch, m_w_mix_out=m_w_mix_out, m_ln1_g=m_ln1_g, m_ln1_b=m_ln1_b, m_x_wq=m_x_wq, m_x_wkv=m_x_wkv, m_x_wo=m_x_wo, m_ln2_g=m_ln2_g, m_ln2_b=m_ln2_b, m_f_w_up=m_f_w_up, m_f_conv_k=m_f_conv_k, m_f_conv_b=m_f_conv_b, m_f_w_down=m_f_w_down, m_ln3_g=m_ln3_g, m_ln3_b=m_ln3_b, v_w_in=v_w_in, v_b_gate=v_b_gate, v_a_sink=v_a_sink, v_b_q_gain=v_b_q_gain, v_b_k_gain=v_b_k_gain, v_c_ln_g=v_c_ln_g, v_c_ln_b=v_c_ln_b, v_c_ws=v_c_ws, v_c_bs=v_c_bs, v_w_branch=v_w_branch, v_w_mix_out=v_w_mix_out, v_ln1_g=v_ln1_g, v_ln1_b=v_ln1_b, v_x_wq=v_x_wq, v_x_wkv=v_x_wkv, v_x_wo=v_x_wo, v_ln2_g=v_ln2_g, v_ln2_b=v_ln2_b, v_f_w_up=v_f_w_up, v_f_conv_k=v_f_conv_k, v_f_conv_b=v_f_conv_b, v_f_w_down=v_f_w_down, v_ln3_g=v_ln3_g, v_ln3_b=v_ln3_b)
    weights = {n: given[n] for n in TWIN_WEIGHTS}
    shared = {n: given[n] for n in SHARED_INPUTS}
    per_example = {n: given[n] for n in ['x', 'mem']}
    grad_fn = _jax.value_and_grad(_loss, argnums=(0, 1))

    def one_microbatch(ex, loss_target):
        ex = dict(ex)
        diff = ex.pop(TWIN_DIFF_INPUT)
        return grad_fn(weights, diff, {**shared, **ex}, loss_target)

    if N_MICROBATCH == 1:
        loss, (grad_w, grad_x) = one_microbatch(per_example, given["loss_target"])
    else:
        def body(carry, xs):
            loss_sum, grad_sum = carry
            l_k, (gw_k, gx_k) = one_microbatch(xs[0], xs[1])
            with _jax.named_scope("update"):
                return (loss_sum + l_k, _jax.tree.map(_jnp.add, grad_sum, gw_k)), gx_k

        init = (_jnp.zeros((), _jnp.float32), _jax.tree.map(_jnp.zeros_like, weights))
        (loss, grad_w), grad_x = _jax.lax.scan(body, init, (per_example, given["loss_target"]))
    with _jax.named_scope("update"):
        delta_w, new_m, new_v = {}, {}, {}
        for n in TWIN_WEIGHTS:
            delta_w[n], new_m[n], new_v[n] = _adamw(weights[n], grad_w[n], given["m_" + n], given["v_" + n])
    return (loss, grad_x, *[grad_w[n] for n in TWIN_WEIGHTS], *[delta_w[n] for n in TWIN_WEIGHTS],
            *[new_m[n] for n in TWIN_WEIGHTS], *[new_v[n] for n in TWIN_WEIGHTS])
```

```python
import functools
import math

import jax
import jax.numpy as jnp
from jax import lax
from jax.experimental import pallas as pl
from jax.experimental.pallas import tpu as pltpu

F32 = jnp.float32
BF16 = jnp.bfloat16

DEPTH = 4
HEAD_DIM = 64
BLOCK = 128
WINDOW = 128
GRID_W = 64
C_WIDTH = 512
C_GROUPS = 4
CHUNK = 128
N_BRANCH = 3
BRANCH_WIDTH = 512
ROPE_THETA = 10000.0
X_HEADS = 4
X_HEAD_DIM = 128
ALPHA = (2 * DEPTH) ** 0.25
LN_EPS = 1e-5
RMS_EPS = 1e-6
ADAM_LR = 0.001
ADAM_B1 = 0.9
ADAM_B2 = 0.999
ADAM_EPS = 1e-08
ADAM_WD = 0.01
ADAM_STEP = 10
N_DEV = 8

COL_A = 0
COL_B = 768
COL_C = 1536
COL_GATE = 2560
QKV_W = 768

LANES = 128
V7X_VMEM_BYTES = 64 * 1024 * 1024
VMEM_LIMIT = V7X_VMEM_BYTES - 8 * 1024 * 1024
NEG_BIG = -1e30

_NT = (((1,), (1,)), ((), ()))
_TN = (((0,), (0,)), ((), ()))
_NN = (((1,), (0,)), ((), ()))


def _cp(sem=None):
    return pltpu.CompilerParams(dimension_semantics=sem, vmem_limit_bytes=VMEM_LIMIT)


def _tile(n, target, align=LANES):
    if n <= target:
        return n
    best = None
    for t in range(align, target + 1, align):
        if n % t == 0:
            best = t
    assert best is not None, (n, target)
    return best


def _big_tile(n):
    t = _tile(n, 1024)
    return t if t >= 512 else _tile(n, 1536)


def _dot(a, b, dims=_NN):
    return lax.dot_general(a, b, dims, preferred_element_type=F32)


def _gelu(x):
    return 0.5 * x * (1.0 + lax.erf(x * 0.7071067811865476))


def _gelu_grad(x):
    return 0.5 * (1.0 + lax.erf(x * 0.7071067811865476)) + x * jnp.exp(-0.5 * x * x) * 0.3989422804014327


def _sigmoid(x):
    return 1.0 / (1.0 + jnp.exp(-x))


def _mm(a, b, *, ta=False, tb=False, out_dtype=F32, res=None, res_scale=1.0, name):
    if ta:
        K, M = a.shape
    else:
        M, K = a.shape
    if tb:
        N, Kb = b.shape
    else:
        Kb, N = b.shape
    assert K == Kb, (a.shape, b.shape, ta, tb)
    tm, tn, tk = _big_tile(M), _big_tile(N), _tile(K, 512)
    nk = K // tk
    dims = (((0 if ta else 1,), (1 if tb else 0,)), ((), ()))

    def body(*refs):
        if res is None:
            a_ref, b_ref, o_ref, acc = refs
            r_ref = None
        else:
            a_ref, b_ref, r_ref, o_ref, acc = refs
        k = pl.program_id(2)

        @pl.when(k == 0)
        def _():
            acc[...] = jnp.zeros_like(acc)

        acc[...] += _dot(a_ref[...].astype(BF16), b_ref[...].astype(BF16), dims)

        @pl.when(k == nk - 1)
        def _():
            out = acc[...]
            if r_ref is not None:
                out = out + res_scale * r_ref[...]
            o_ref[...] = out.astype(out_dtype)

    a_spec = pl.BlockSpec((tk, tm), lambda i, j, k: (k, i)) if ta else pl.BlockSpec((tm, tk), lambda i, j, k: (i, k))
    b_spec = pl.BlockSpec((tn, tk), lambda i, j, k: (j, k)) if tb else pl.BlockSpec((tk, tn), lambda i, j, k: (k, j))
    in_specs = [a_spec, b_spec]
    args = [a, b]
    if res is not None:
        in_specs.append(pl.BlockSpec((tm, tn), lambda i, j, k: (i, j)))
        args.append(res)
    return pl.pallas_call(
        body, name=name,
        out_shape=jax.ShapeDtypeStruct((M, N), out_dtype),
        grid=(M // tm, N // tn, nk),
        in_specs=in_specs,
        out_specs=pl.BlockSpec((tm, tn), lambda i, j, k: (i, j)),
        scratch_shapes=[pltpu.VMEM((tm, tn), F32)],
        compiler_params=_cp(("parallel", "parallel", "arbitrary")),
    )(*args)


def _mm_res_ln(a, w, x, g, b, *, name):
    S, K = a.shape
    D = w.shape[1]
    tm = _tile(S, 256)

    def body(a_ref, w_ref, x_ref, g_ref, b_ref, y_ref, yb_ref, xh_ref, rs_ref):
        h = _dot(a_ref[...], w_ref[...])
        z = ALPHA * x_ref[...] + h
        mu = jnp.mean(z, axis=-1, keepdims=True)
        zc = z - mu
        var = jnp.mean(zc * zc, axis=-1, keepdims=True)
        r = lax.rsqrt(var + LN_EPS)
        xh = zc * r
        y = xh * g_ref[...] + b_ref[...]
        y_ref[...] = y
        yb_ref[...] = y.astype(BF16)
        xh_ref[...] = xh
        rs_ref[...] = r

    row = lambda i: (i, 0)
    full = lambda i: (0, 0)
    return pl.pallas_call(
        body, name=name,
        out_shape=(jax.ShapeDtypeStruct((S, D), F32), jax.ShapeDtypeStruct((S, D), BF16),
                   jax.ShapeDtypeStruct((S, D), F32), jax.ShapeDtypeStruct((S, 1), F32)),
        grid=(S // tm,),
        in_specs=[pl.BlockSpec((tm, K), row), pl.BlockSpec((K, D), full), pl.BlockSpec((tm, D), row),
                  pl.BlockSpec((1, D), full), pl.BlockSpec((1, D), full)],
        out_specs=(pl.BlockSpec((tm, D), row), pl.BlockSpec((tm, D), row), pl.BlockSpec((tm, D), row),
                   pl.BlockSpec((tm, 1), row)),
        compiler_params=_cp(("parallel",)),
    )(a, w, x, g, b)


def _ln_bwd(dy, xh, rs, g, *, name):
    S, D = dy.shape
    tm = _tile(S, 256)

    def body(dy_ref, xh_ref, rs_ref, g_ref, dz_ref, dzb_ref, dg_ref, db_ref):
        @pl.when(pl.program_id(0) == 0)
        def _():
            dg_ref[...] = jnp.zeros_like(dg_ref)
            db_ref[...] = jnp.zeros_like(db_ref)

        dy = dy_ref[...]
        xh = xh_ref[...]
        dxh = dy * g_ref[...]
        m1 = jnp.mean(dxh, axis=-1, keepdims=True)
        m2 = jnp.mean(dxh * xh, axis=-1, keepdims=True)
        dz = rs_ref[...] * (dxh - m1 - xh * m2)
        dz_ref[...] = dz
        dzb_ref[...] = dz.astype(BF16)
        dg_ref[...] += jnp.sum(dy * xh, axis=0, keepdims=True)
        db_ref[...] += jnp.sum(dy, axis=0, keepdims=True)

    row = lambda i: (i, 0)
    full = lambda i: (0, 0)
    return pl.pallas_call(
        body, name=name,
        out_shape=(jax.ShapeDtypeStruct((S, D), F32), jax.ShapeDtypeStruct((S, D), BF16),
                   jax.ShapeDtypeStruct((1, D), F32), jax.ShapeDtypeStruct((1, D), F32)),
        grid=(S // tm,),
        in_specs=[pl.BlockSpec((tm, D), row), pl.BlockSpec((tm, D), row), pl.BlockSpec((tm, 1), row),
                  pl.BlockSpec((1, D), full)],
        out_specs=(pl.BlockSpec((tm, D), row), pl.BlockSpec((tm, D), row), pl.BlockSpec((1, D), full),
                   pl.BlockSpec((1, D), full)),
        compiler_params=_cp(("arbitrary",)),
    )(dy, xh, rs, g)


def _loss_head(y, t, *, name):
    S, D = y.shape
    tm = _tile(S, 512)

    def body(y_ref, t_ref, dy_ref, l_ref):
        @pl.when(pl.program_id(0) == 0)
        def _():
            l_ref[...] = jnp.zeros_like(l_ref)

        e = y_ref[...] - t_ref[...]
        dy_ref[...] = e / D
        l_ref[...] += 0.5 * jnp.sum(jnp.mean(e * e, axis=-1, keepdims=True), axis=0, keepdims=True)

    row = lambda i: (i, 0)
    return pl.pallas_call(
        body, name=name,
        out_shape=(jax.ShapeDtypeStruct((S, D), F32), jax.ShapeDtypeStruct((1, 1), F32)),
        grid=(S // tm,),
        in_specs=[pl.BlockSpec((tm, D), row), pl.BlockSpec((tm, D), row)],
        out_specs=(pl.BlockSpec((tm, D), row), pl.BlockSpec((1, 1), lambda i: (0, 0))),
        compiler_params=_cp(("arbitrary",)),
    )(y, t)


def _rope_tables(S):
    pos = jnp.arange(S, dtype=jnp.int32)
    row = pos // GRID_W
    col = pos % GRID_W

    def cs(p, d):
        half = d // 2
        inv = ROPE_THETA ** (-jnp.arange(half, dtype=F32) * (2.0 / d))
        ang = p.astype(F32)[:, None] * inv[None, :]
        c, s = jnp.cos(ang), jnp.sin(ang)
        return jnp.concatenate([c, c], -1), jnp.concatenate([-s, s], -1)

    ca, sa = cs(pos, HEAD_DIM)
    cr, sr = cs(row, HEAD_DIM // 2)
    cc, sc = cs(col, HEAD_DIM // 2)
    cb, sb = jnp.concatenate([cr, cc], -1), jnp.concatenate([sr, sc], -1)
    two = lambda t: jnp.concatenate([t, t], -1)
    return two(ca), two(sa), two(cb), two(sb)


def _partner(x, lane, width):
    h = width // 2
    return jnp.where(lane % width < h, pltpu.roll(x, LANES - h, 1), pltpu.roll(x, h, 1))


def _rope_fwd(x, c, s, lane, width):
    return x * c + _partner(x, lane, width) * s


def _rope_bwd(dy, c, s, lane, width):
    return dy * c + _partner(dy * s, lane, width)


def _head_sum(x, seg):
    return lax.dot_general(x, seg, _NN, precision=lax.Precision.HIGHEST, preferred_element_type=F32)


def _split_heads(x, lane):
    lo = lane < HEAD_DIM
    r = pltpu.roll(x, HEAD_DIM, 1)
    z = jnp.zeros_like(x)
    return jnp.where(lo, x, z), jnp.where(lo, z, r), jnp.where(lo, r, z), jnp.where(lo, z, x)


def _fold_heads(d0, d1, lane):
    t0 = d0 + pltpu.roll(d0, HEAD_DIM, 1)
    t1 = d1 + pltpu.roll(d1, HEAD_DIM, 1)
    return jnp.where(lane < HEAD_DIM, t0, t1)


def _seg_matrix():
    i = jnp.arange(LANES)
    return (i[:, None] // HEAD_DIM == i[None, :] // HEAD_DIM).astype(F32)


def _prep(proj, tabs, qg2, kg2, seg, *, name):
    S = proj.shape[0]
    ts = _tile(S, 256)
    ca, sa, cb, sb = tabs

    def body(pa_ref, pb_ref, ca_ref, sa_ref, cb_ref, sb_ref, qg_ref, kg_ref, seg_ref,
             aq_ref, ak_ref, av_ref, bq_ref, bk_ref, bv_ref):
        lane = lax.broadcasted_iota(jnp.int32, (ts, LANES), 1)
        ca, sa, cb, sb = ca_ref[...], sa_ref[...], cb_ref[...], sb_ref[...]
        seg = seg_ref[...]

        def norm(x, gain):
            r = lax.rsqrt(_head_sum(x * x, seg) * (1.0 / HEAD_DIM) + RMS_EPS)
            return x * r * gain

        def put(ref, x):
            for i, part in enumerate(_split_heads(x, lane)):
                ref[i] = part.astype(BF16)

        for gidx in range(4):
            cols = slice(gidx * LANES, (gidx + 1) * LANES)
            aq_ref[:, cols] = (_rope_fwd(pa_ref[:, cols], ca, sa, lane, HEAD_DIM) * 0.125).astype(BF16)
            bq = norm(pb_ref[:, cols], qg_ref[...])
            bq_ref[:, cols] = (_rope_fwd(bq, cb, sb, lane, HEAD_DIM // 2) * 0.125).astype(BF16)
        put(ak_ref, _rope_fwd(pa_ref[:, 512:640], ca, sa, lane, HEAD_DIM))
        put(av_ref, pa_ref[:, 640:768])
        bk = norm(pb_ref[:, 512:640], kg_ref[...])
        put(bk_ref, _rope_fwd(bk, cb, sb, lane, HEAD_DIM // 2))
        put(bv_ref, pb_ref[:, 640:768])

    row = lambda i: (i, 0)
    full = lambda i: (0, 0)
    tab = pl.BlockSpec((ts, LANES), row)
    kv_shape = jax.ShapeDtypeStruct((4, S, LANES), BF16)
    kv_spec = pl.BlockSpec((4, ts, LANES), lambda i: (0, i, 0))
    q_shape = jax.ShapeDtypeStruct((S, 512), BF16)
    q_spec = pl.BlockSpec((ts, 512), row)
    return pl.pallas_call(
        body, name=name,
        out_shape=(q_shape, kv_shape, kv_shape, q_shape, kv_shape, kv_shape),
        grid=(S // ts,),
        in_specs=[pl.BlockSpec((ts, QKV_W), lambda i: (i, 0)), pl.BlockSpec((ts, QKV_W), lambda i: (i, 1)),
                  tab, tab, tab, tab, pl.BlockSpec((1, LANES), full), pl.BlockSpec((1, LANES), full),
                  pl.BlockSpec((LANES, LANES), full)],
        out_specs=(q_spec, kv_spec, kv_spec, q_spec, kv_spec, kv_spec),
        compiler_params=_cp(("parallel",)),
    )(proj, proj, ca, sa, cb, sb, qg2, kg2, seg)


def _unprep(dqa, dka, dva, dqb, dkb, dvb, proj, tabs, qg2, kg2, seg, *, name):
    S = proj.shape[0]
    ts = _tile(S, 256)
    ca, sa, cb, sb = tabs

    def body(dqa_ref, dka_ref, dva_ref, dqb_ref, dkb_ref, dvb_ref, pb_ref, ca_ref, sa_ref, cb_ref, sb_ref,
             qg_ref, kg_ref, seg_ref, dp_ref, dqg_ref, dkg_ref):
        @pl.when(pl.program_id(0) == 0)
        def _():
            dqg_ref[...] = jnp.zeros_like(dqg_ref)
            dkg_ref[...] = jnp.zeros_like(dkg_ref)

        lane = lax.broadcasted_iota(jnp.int32, (ts, LANES), 1)
        ca, sa, cb, sb = ca_ref[...], sa_ref[...], cb_ref[...], sb_ref[...]
        seg = seg_ref[...]

        def norm_bwd(dy, x, gain):
            r = lax.rsqrt(_head_sum(x * x, seg) * (1.0 / HEAD_DIM) + RMS_EPS)
            gdy = gain * dy
            dot = _head_sum(gdy * x, seg) * (1.0 / HEAD_DIM)
            dx = r * gdy - x * (r * r * r) * dot
            return dx, jnp.sum(dy * x * r, axis=0, keepdims=True)

        for gidx in range(4):
            cols = slice(gidx * LANES, (gidx + 1) * LANES)
            dp_ref[:, cols] = _rope_bwd(dqa_ref[:, cols] * 0.125, ca, sa, lane, HEAD_DIM).astype(BF16)
            dbq = _rope_bwd(dqb_ref[:, cols] * 0.125, cb, sb, lane, HEAD_DIM // 2)
            dx, dg = norm_bwd(dbq, pb_ref[:, cols], qg_ref[...])
            dp_ref[:, COL_B + gidx * LANES:COL_B + (gidx + 1) * LANES] = dx.astype(BF16)
            dqg_ref[...] += dg
        dak = _fold_heads(dka_ref[0] + dka_ref[1], dka_ref[2] + dka_ref[3], lane)
        dp_ref[:, 512:640] = _rope_bwd(dak, ca, sa, lane, HEAD_DIM).astype(BF16)
        dp_ref[:, 640:768] = _fold_heads(dva_ref[0] + dva_ref[1], dva_ref[2] + dva_ref[3], lane).astype(BF16)
        dbk = _fold_heads(dkb_ref[0] + dkb_ref[1], dkb_ref[2] + dkb_ref[3], lane)
        dbk = _rope_bwd(dbk, cb, sb, lane, HEAD_DIM // 2)
        dx, dg = norm_bwd(dbk, pb_ref[:, 512:640], kg_ref[...])
        dp_ref[:, COL_B + 512:COL_B + 640] = dx.astype(BF16)
        dkg_ref[...] += dg
        dp_ref[:, COL_B + 640:COL_B + 768] = _fold_heads(dvb_ref[0] + dvb_ref[1], dvb_ref[2] + dvb_ref[3],
                                                         lane).astype(BF16)

    row = lambda i: (i, 0)
    full = lambda i: (0, 0)
    tab = pl.BlockSpec((ts, LANES), row)
    q_spec = pl.BlockSpec((ts, 512), row)
    kv_spec = pl.BlockSpec((4, ts, LANES), lambda i: (0, i, 0))
    return pl.pallas_call(
        body, name=name,
        out_shape=(jax.ShapeDtypeStruct((S, 2 * QKV_W), BF16), jax.ShapeDtypeStruct((1, LANES), F32),
                   jax.ShapeDtypeStruct((1, LANES), F32)),
        grid=(S // ts,),
        in_specs=[q_spec, kv_spec, kv_spec, q_spec, kv_spec, kv_spec,
                  pl.BlockSpec((ts, QKV_W), lambda i: (i, 1)), tab, tab, tab, tab,
                  pl.BlockSpec((1, LANES), full), pl.BlockSpec((1, LANES), full), pl.BlockSpec((LANES, LANES), full)],
        out_specs=(pl.BlockSpec((ts, 2 * QKV_W), row), pl.BlockSpec((1, LANES), full),
                   pl.BlockSpec((1, LANES), full)),
        compiler_params=_cp(("arbitrary",)),
    )(dqa, dka, dva, dqb, dkb, dvb, proj, ca, sa, cb, sb, qg2, kg2, seg)


def _attn_dense_fwd(q, k4, v4, *, name):
    S = q.shape[0]
    tq = _tile(S, 256)

    def body(q_ref, k_ref, v_ref, o_ref, lse_ref):
        for pr in range(2):
            qp = q_ref[:, pr * LANES:(pr + 1) * LANES]
            acc = None
            for half in range(2):
                s = _dot(qp, k_ref[half], _NT)
                m = jnp.max(s, axis=-1, keepdims=True)
                e = jnp.exp(s - m)
                l = jnp.sum(e, axis=-1, keepdims=True)
                p = (e * (1.0 / l)).astype(BF16)
                pv = _dot(p, v_ref[half])
                acc = pv if acc is None else acc + pv
                lse_ref[pr * 2 + half] = m + jnp.log(l)
            o_ref[:, pr * LANES:(pr + 1) * LANES] = acc.astype(BF16)

    kv_spec = pl.BlockSpec((2, S, LANES), lambda kv, i: (kv, 0, 0))
    return pl.pallas_call(
        body, name=name,
        out_shape=(jax.ShapeDtypeStruct((S, 512), BF16), jax.ShapeDtypeStruct((8, S, 1), F32)),
        grid=(2, S // tq),
        in_specs=[pl.BlockSpec((tq, 256), lambda kv, i: (i, kv)), kv_spec, kv_spec],
        out_specs=(pl.BlockSpec((tq, 256), lambda kv, i: (i, kv)),
                   pl.BlockSpec((4, tq, 1), lambda kv, i: (kv, i, 0))),
        compiler_params=_cp(("parallel", "parallel")),
    )(q, k4, v4)


def _attn_dense_bwd(q, k4, v4, lse, do, *, name):
    S = q.shape[0]
    tq = _tile(S, 128)

    def body(q_ref, k_ref, v_ref, lse_ref, do_ref, dq_ref, dk_ref, dv_ref):
        @pl.when(pl.program_id(1) == 0)
        def _():
            dk_ref[...] = jnp.zeros_like(dk_ref)
            dv_ref[...] = jnp.zeros_like(dv_ref)

        lane = lax.broadcasted_iota(jnp.int32, (tq, LANES), 1)
        for pr in range(2):
            qp = q_ref[:, pr * LANES:(pr + 1) * LANES]
            dop = do_ref[:, pr * LANES:(pr + 1) * LANES].astype(BF16)
            dq = None
            for half in range(2):
                mine = (lane < HEAD_DIM) if half == 0 else (lane >= HEAD_DIM)
                s = _dot(qp, k_ref[half], _NT)
                p = jnp.exp(s - lse_ref[pr * 2 + half])
                dp = _dot(dop, v_ref[half], _NT)
                delta = jnp.sum(p * dp, axis=-1, keepdims=True)
                ds = (p * (dp - delta)).astype(BF16)
                pb = p.astype(BF16)
                d = _dot(ds, k_ref[half])
                dq = d if dq is None else dq + d
                dk_ref[half] += _dot(ds, jnp.where(mine, qp, jnp.zeros_like(qp)), _TN)
                dv_ref[half] += _dot(pb, jnp.where(mine, dop, jnp.zeros_like(dop)), _TN)
            dq_ref[:, pr * LANES:(pr + 1) * LANES] = dq

    kv_spec = pl.BlockSpec((2, S, LANES), lambda kv, i: (kv, 0, 0))
    q_spec = pl.BlockSpec((tq, 256), lambda kv, i: (i, kv))
    return pl.pallas_call(
        body, name=name,
        out_shape=(jax.ShapeDtypeStruct((S, 512), F32), jax.ShapeDtypeStruct((4, S, LANES), F32),
                   jax.ShapeDtypeStruct((4, S, LANES), F32)),
        grid=(2, S // tq),
        in_specs=[q_spec, kv_spec, kv_spec, pl.BlockSpec((4, tq, 1), lambda kv, i: (kv, i, 0)), q_spec],
        out_specs=(q_spec, kv_spec, kv_spec),
        compiler_params=_cp(("parallel", "arbitrary")),
    )(q, k4, v4, lse, do)


WIN_KEYS = 3 * BLOCK


def _win_start(n, S):
    return pl.multiple_of(jnp.clip((n - 1) * BLOCK, 0, S - WIN_KEYS), BLOCK)


def _win_valid(n, start):
    qpos = n * BLOCK + lax.broadcasted_iota(jnp.int32, (BLOCK, WIN_KEYS), 0)
    kpos = start + lax.broadcasted_iota(jnp.int32, (BLOCK, WIN_KEYS), 1)
    return jnp.abs(qpos - kpos) <= WINDOW


def _attn_win_fwd(q, k4, v4, sink, *, name):
    S = q.shape[0]
    assert S >= WIN_KEYS

    def body(sink_ref, q_ref, k_ref, v_ref, o_ref, lse_ref):
        n = pl.program_id(0)
        start = _win_start(n, S)
        valid = _win_valid(n, start)
        for kv in range(2):
            for pr in range(2):
                cols = slice((kv * 2 + pr) * LANES, (kv * 2 + pr + 1) * LANES)
                qp = q_ref[:, cols]
                acc = None
                for half in range(2):
                    h = kv * 4 + pr * 2 + half
                    kk = k_ref[kv * 2 + half, pl.ds(start, WIN_KEYS), :]
                    vv = v_ref[kv * 2 + half, pl.ds(start, WIN_KEYS), :]
                    s = jnp.where(valid, _dot(qp, kk, _NT), NEG_BIG)
                    snk = sink_ref[h]
                    m = jnp.maximum(jnp.max(s, axis=-1, keepdims=True), snk)
                    e = jnp.exp(s - m)
                    l = jnp.sum(e, axis=-1, keepdims=True) + jnp.exp(snk - m)
                    p = (e * (1.0 / l)).astype(BF16)
                    pv = _dot(p, vv)
                    acc = pv if acc is None else acc + pv
                    lse_ref[h] = m + jnp.log(l)
                o_ref[:, cols] = acc.astype(BF16)

    kv_spec = pl.BlockSpec((4, S, LANES), lambda n: (0, 0, 0))
    return pl.pallas_call(
        body, name=name,
        out_shape=(jax.ShapeDtypeStruct((S, 512), BF16), jax.ShapeDtypeStruct((8, S, 1), F32)),
        grid=(S // BLOCK,),
        in_specs=[pl.BlockSpec(memory_space=pltpu.SMEM), pl.BlockSpec((BLOCK, 512), lambda n: (n, 0)),
                  kv_spec, kv_spec],
        out_specs=(pl.BlockSpec((BLOCK, 512), lambda n: (n, 0)), pl.BlockSpec((8, BLOCK, 1), lambda n: (0, n, 0))),
        compiler_params=_cp(("parallel",)),
    )(sink, q, k4, v4)


def _attn_win_bwd(q, k4, v4, sink, lse, do, *, name):
    S = q.shape[0]

    def body(sink_ref, q_ref, k_ref, v_ref, lse_ref, do_ref, dq_ref, dk_ref, dv_ref, dsink_ref):
        n = pl.program_id(0)

        @pl.when(n == 0)
        def _():
            dk_ref[...] = jnp.zeros_like(dk_ref)
            dv_ref[...] = jnp.zeros_like(dv_ref)
            dsink_ref[...] = jnp.zeros_like(dsink_ref)

        start = _win_start(n, S)
        valid = _win_valid(n, start)
        lane = lax.broadcasted_iota(jnp.int32, (BLOCK, LANES), 1)
        for kv in range(2):
            for pr in range(2):
                cols = slice((kv * 2 + pr) * LANES, (kv * 2 + pr + 1) * LANES)
                qp = q_ref[:, cols]
                dop = do_ref[:, cols].astype(BF16)
                dq = None
                for half in range(2):
                    h = kv * 4 + pr * 2 + half
                    slot = kv * 2 + half
                    mine = (lane < HEAD_DIM) if half == 0 else (lane >= HEAD_DIM)
                    win = pl.ds(start, WIN_KEYS)
                    kk = k_ref[slot, win, :]
                    vv = v_ref[slot, win, :]
                    lse_h = lse_ref[h]
                    s = jnp.where(valid, _dot(qp, kk, _NT), NEG_BIG)
                    p = jnp.exp(s - lse_h)
                    dp = _dot(dop, vv, _NT)
                    delta = jnp.sum(p * dp, axis=-1, keepdims=True)
                    ds = (p * (dp - delta)).astype(BF16)
                    pb = p.astype(BF16)
                    d = _dot(ds, kk)
                    dq = d if dq is None else dq + d
                    dk_ref[slot, win, :] += _dot(ds, jnp.where(mine, qp, jnp.zeros_like(qp)), _TN)
                    dv_ref[slot, win, :] += _dot(pb, jnp.where(mine, dop, jnp.zeros_like(dop)), _TN)
                    p_sink = jnp.exp(sink_ref[h] - lse_h)
                    dsink_ref[h:h + 1, :] += jnp.broadcast_to(-jnp.sum(p_sink * delta, axis=0, keepdims=True),
                                                              (1, LANES))
                dq_ref[:, cols] = dq

    kv_spec = pl.BlockSpec((4, S, LANES), lambda n: (0, 0, 0))
    q_spec = pl.BlockSpec((BLOCK, 512), lambda n: (n, 0))
    return pl.pallas_call(
        body, name=name,
        out_shape=(jax.ShapeDtypeStruct((S, 512), F32), jax.ShapeDtypeStruct((4, S, LANES), F32),
                   jax.ShapeDtypeStruct((4, S, LANES), F32), jax.ShapeDtypeStruct((8, LANES), F32)),
        grid=(S // BLOCK,),
        in_specs=[pl.BlockSpec(memory_space=pltpu.SMEM), q_spec, kv_spec, kv_spec,
                  pl.BlockSpec((8, BLOCK, 1), lambda n: (0, n, 0)), q_spec],
        out_specs=(q_spec, kv_spec, kv_spec, pl.BlockSpec((8, LANES), lambda n: (0, 0))),
        compiler_params=_cp(("arbitrary",)),
    )(sink, q, k4, v4, lse, do)


def _c_ln(v, g, b):
    mu = jnp.mean(v, axis=-1, keepdims=True)
    vc = v - mu
    r = lax.rsqrt(jnp.mean(vc * vc, axis=-1, keepdims=True) + LN_EPS)
    vh = vc * r
    return vh, r, vh * g + b


def _gmlp_fwd(proj, ws, bs3, lg, lb, *, name):
    S = proj.shape[0]

    def body(u_ref, v_ref, ws_ref, bs_ref, lg_ref, lb_ref, o_ref):
        u = _gelu(u_ref[...])
        _, _, vn = _c_ln(_gelu(v_ref[...]), lg_ref[...], lb_ref[...])
        vn = vn.astype(BF16)
        for gi in range(C_GROUPS):
            cols = slice(gi * LANES, (gi + 1) * LANES)
            mixed = _dot(ws_ref[gi], vn[:, cols]) + bs_ref[gi]
            o_ref[:, cols] = (u[:, cols] * mixed).astype(BF16)

    full2 = lambda n: (0, 0)
    full3 = lambda n: (0, 0, 0)
    return pl.pallas_call(
        body, name=name,
        out_shape=jax.ShapeDtypeStruct((S, C_WIDTH), BF16),
        grid=(S // CHUNK,),
        in_specs=[pl.BlockSpec((CHUNK, C_WIDTH), lambda n: (n, COL_C // C_WIDTH)),
                  pl.BlockSpec((CHUNK, C_WIDTH), lambda n: (n, COL_C // C_WIDTH + 1)),
                  pl.BlockSpec((C_GROUPS, CHUNK, CHUNK), full3), pl.BlockSpec((C_GROUPS, CHUNK, 1), full3),
                  pl.BlockSpec((1, C_WIDTH), full2), pl.BlockSpec((1, C_WIDTH), full2)],
        out_specs=pl.BlockSpec((CHUNK, C_WIDTH), lambda n: (n, 0)),
        compiler_params=_cp(("parallel",)),
    )(proj, proj, ws, bs3, lg, lb)


def _gmlp_bwd(proj, dout, ws, bs3, lg, lb, *, name):
    S = proj.shape[0]

    def body(u_ref, v_ref, d_ref, ws_ref, bs_ref, lg_ref, lb_ref, dz_ref, dws_ref, dbs_ref, dlg_ref, dlb_ref):
        @pl.when(pl.program_id(0) == 0)
        def _():
            dws_ref[...] = jnp.zeros_like(dws_ref)
            dbs_ref[...] = jnp.zeros_like(dbs_ref)
            dlg_ref[...] = jnp.zeros_like(dlg_ref)
            dlb_ref[...] = jnp.zeros_like(dlb_ref)

        u_pre, v_pre, d = u_ref[...], v_ref[...], d_ref[...]
        u = _gelu(u_pre)
        vh, r, vn = _c_ln(_gelu(v_pre), lg_ref[...], lb_ref[...])
        vnb = vn.astype(BF16)
        du_parts, dvn_parts = [], []
        for gi in range(C_GROUPS):
            cols = slice(gi * LANES, (gi + 1) * LANES)
            mixed = _dot(ws_ref[gi], vnb[:, cols]) + bs_ref[gi]
            du_parts.append(d[:, cols] * mixed)
            dm = d[:, cols] * u[:, cols]
            dbs_ref[gi] += jnp.sum(dm, axis=-1, keepdims=True)
            dmb = dm.astype(BF16)
            dws_ref[gi] += _dot(dmb, vnb[:, cols], _NT)
            dvn_parts.append(_dot(ws_ref[gi], dmb, _TN))
        du = jnp.concatenate(du_parts, axis=-1)
        dvn = jnp.concatenate(dvn_parts, axis=-1)
        dlg_ref[...] += jnp.sum(dvn * vh, axis=0, keepdims=True)
        dlb_ref[...] += jnp.sum(dvn, axis=0, keepdims=True)
        dvh = dvn * lg_ref[...]
        m1 = jnp.mean(dvh, axis=-1, keepdims=True)
        m2 = jnp.mean(dvh * vh, axis=-1, keepdims=True)
        dv = r * (dvh - m1 - vh * m2)
        dz_ref[:, :C_WIDTH] = (du * _gelu_grad(u_pre)).astype(BF16)
        dz_ref[:, C_WIDTH:] = (dv * _gelu_grad(v_pre)).astype(BF16)

    full2 = lambda n: (0, 0)
    full3 = lambda n: (0, 0, 0)
    return pl.pallas_call(
        body, name=name,
        out_shape=(jax.ShapeDtypeStruct((S, 2 * C_WIDTH), BF16), jax.ShapeDtypeStruct((C_GROUPS, CHUNK, CHUNK), F32),
                   jax.ShapeDtypeStruct((C_GROUPS, CHUNK, 1), F32), jax.ShapeDtypeStruct((1, C_WIDTH), F32),
                   jax.ShapeDtypeStruct((1, C_WIDTH), F32)),
        grid=(S // CHUNK,),
        in_specs=[pl.BlockSpec((CHUNK, C_WIDTH), lambda n: (n, COL_C // C_WIDTH)),
                  pl.BlockSpec((CHUNK, C_WIDTH), lambda n: (n, COL_C // C_WIDTH + 1)),
                  pl.BlockSpec((CHUNK, C_WIDTH), lambda n: (n, 0)),
                  pl.BlockSpec((C_GROUPS, CHUNK, CHUNK), full3), pl.BlockSpec((C_GROUPS, CHUNK, 1), full3),
                  pl.BlockSpec((1, C_WIDTH), full2), pl.BlockSpec((1, C_WIDTH), full2)],
        out_specs=(pl.BlockSpec((CHUNK, 2 * C_WIDTH), lambda n: (n, 0)), pl.BlockSpec((C_GROUPS, CHUNK, CHUNK), full3),
                   pl.BlockSpec((C_GROUPS, CHUNK, 1), full3), pl.BlockSpec((1, C_WIDTH), full2),
                   pl.BlockSpec((1, C_WIDTH), full2)),
        compiler_params=_cp(("arbitrary",)),
    )(proj, proj, dout, ws, bs3, lg, lb)


GATE_BLK = 512


def _gate_specs(tm, D):
    nh = D // GATE_BLK
    first = COL_GATE // GATE_BLK
    return [pl.BlockSpec((tm, GATE_BLK), functools.partial(lambda i, c: (i, c), c=first + b))
            for b in range(N_BRANCH * nh)]


def _merge_fwd(oa, ob, oc, wb, proj, bg, *, name):
    S = oa.shape[0]
    D = wb.shape[2]
    assert D % GATE_BLK == 0
    nh = D // GATE_BLK
    tm = _tile(S, 256)

    def body(oa_ref, ob_ref, oc_ref, wb_ref, *rest):
        gate_refs, bg_ref, o_ref = rest[:N_BRANCH * nh], rest[N_BRANCH * nh], rest[N_BRANCH * nh + 1]
        brs = (oa_ref[...], ob_ref[...], oc_ref[...])
        for j in range(nh):
            cols = slice(j * GATE_BLK, (j + 1) * GATE_BLK)
            acc = None
            for n in range(N_BRANCH):
                b = n * nh + j
                t = _dot(brs[n], wb_ref[n, :, cols])
                g = _sigmoid(gate_refs[b][...] + bg_ref[:, b * GATE_BLK:(b + 1) * GATE_BLK])
                acc = t * g if acc is None else acc + t * g
            o_ref[:, cols] = acc.astype(BF16)

    row = lambda i: (i, 0)
    br = pl.BlockSpec((tm, BRANCH_WIDTH), row)
    return pl.pallas_call(
        body, name=name,
        out_shape=jax.ShapeDtypeStruct((S, D), BF16),
        grid=(S // tm,),
        in_specs=[br, br, br, pl.BlockSpec((N_BRANCH, BRANCH_WIDTH, D), lambda i: (0, 0, 0))]
                 + _gate_specs(tm, D) + [pl.BlockSpec((1, N_BRANCH * D), lambda i: (0, 0))],
        out_specs=pl.BlockSpec((tm, D), row),
        compiler_params=_cp(("parallel",)),
    )(oa, ob, oc, wb, *([proj] * (N_BRANCH * nh)), bg)


def _merge_bwd(oa, ob, oc, wb, proj, bg, dmerged, *, name):
    S = oa.shape[0]
    D = wb.shape[2]
    nh = D // GATE_BLK
    tm = _tile(S, 256)

    def body(oa_ref, ob_ref, oc_ref, wb_ref, *rest):
        gate_refs = rest[:N_BRANCH * nh]
        bg_ref, dm_ref, dgl_ref, dt_ref, dbr_ref, dbg_ref = rest[N_BRANCH * nh:]

        @pl.when(pl.program_id(0) == 0)
        def _():
            dbg_ref[...] = jnp.zeros_like(dbg_ref)

        brs = (oa_ref[...], ob_ref[...], oc_ref[...])
        for n in range(N_BRANCH):
            dbr = None
            for j in range(nh):
                cols = slice(j * GATE_BLK, (j + 1) * GATE_BLK)
                b = n * nh + j
                gcols = slice(b * GATE_BLK, (b + 1) * GATE_BLK)
                w = wb_ref[n, :, cols]
                t = _dot(brs[n], w)
                g = _sigmoid(gate_refs[b][...] + bg_ref[:, gcols])
                dm = dm_ref[:, cols]
                dt = (dm * g).astype(BF16)
                dgl = dm * t * g * (1.0 - g)
                dt_ref[n, :, cols] = dt
                dgl_ref[:, gcols] = dgl.astype(BF16)
                dbg_ref[:, gcols] += jnp.sum(dgl, axis=0, keepdims=True)
                d = _dot(dt, w, _NT)
                dbr = d if dbr is None else dbr + d
            dbr_ref[n] = dbr

    row = lambda i: (i, 0)
    br = pl.BlockSpec((tm, BRANCH_WIDTH), row)
    return pl.pallas_call(
        body, name=name,
        out_shape=(jax.ShapeDtypeStruct((S, N_BRANCH * D), BF16), jax.ShapeDtypeStruct((N_BRANCH, S, D), BF16),
                   jax.ShapeDtypeStruct((N_BRANCH, S, BRANCH_WIDTH), F32), jax.ShapeDtypeStruct((1, N_BRANCH * D), F32)),
        grid=(S // tm,),
        in_specs=[br, br, br, pl.BlockSpec((N_BRANCH, BRANCH_WIDTH, D), lambda i: (0, 0, 0))]
                 + _gate_specs(tm, D)
                 + [pl.BlockSpec((1, N_BRANCH * D), lambda i: (0, 0)), pl.BlockSpec((tm, D), row)],
        out_specs=(pl.BlockSpec((tm, N_BRANCH * D), row), pl.BlockSpec((N_BRANCH, tm, D), lambda i: (0, i, 0)),
                   pl.BlockSpec((N_BRANCH, tm, BRANCH_WIDTH), lambda i: (0, i, 0)),
                   pl.BlockSpec((1, N_BRANCH * D), lambda i: (0, 0))),
        compiler_params=_cp(("arbitrary",)),
    )(oa, ob, oc, wb, *([proj] * (N_BRANCH * nh)), bg, dmerged)


X_SCALE = 1.0 / math.sqrt(X_HEAD_DIM)
X_W = X_HEADS * X_HEAD_DIM


def _xattn_fwd(q, kv, *, name):
    S = q.shape[0]
    M = kv.shape[0]
    tq = _tile(S, 512)

    def body(q_ref, kv_ref, o_ref, lse_ref):
        for h in range(X_HEADS):
            cols = slice(h * LANES, (h + 1) * LANES)
            s = _dot(q_ref[:, cols], kv_ref[:, cols], _NT) * X_SCALE
            m = jnp.max(s, axis=-1, keepdims=True)
            e = jnp.exp(s - m)
            l = jnp.sum(e, axis=-1, keepdims=True)
            p = (e * (1.0 / l)).astype(BF16)
            o_ref[:, cols] = _dot(p, kv_ref[:, X_W + h * LANES:X_W + (h + 1) * LANES]).astype(BF16)
            lse_ref[h] = m + jnp.log(l)

    return pl.pallas_call(
        body, name=name,
        out_shape=(jax.ShapeDtypeStruct((S, X_W), BF16), jax.ShapeDtypeStruct((X_HEADS, S, 1), F32)),
        grid=(S // tq,),
        in_specs=[pl.BlockSpec((tq, X_W), lambda i: (i, 0)), pl.BlockSpec((M, 2 * X_W), lambda i: (0, 0))],
        out_specs=(pl.BlockSpec((tq, X_W), lambda i: (i, 0)), pl.BlockSpec((X_HEADS, tq, 1), lambda i: (0, i, 0))),
        compiler_params=_cp(("parallel",)),
    )(q, kv)


def _xattn_bwd(q, kv, lse, do, *, name):
    S = q.shape[0]
    M = kv.shape[0]
    tq = _tile(S, 512)

    def body(q_ref, kv_ref, lse_ref, do_ref, dq_ref, dkv_ref):
        @pl.when(pl.program_id(0) == 0)
        def _():
            dkv_ref[...] = jnp.zeros_like(dkv_ref)

        for h in range(X_HEADS):
            cols = slice(h * LANES, (h + 1) * LANES)
            vcols = slice(X_W + h * LANES, X_W + (h + 1) * LANES)
            qh, kh, vh = q_ref[:, cols], kv_ref[:, cols], kv_ref[:, vcols]
            doh = do_ref[:, cols].astype(BF16)
            p = jnp.exp(_dot(qh, kh, _NT) * X_SCALE - lse_ref[h])
            dp = _dot(doh, vh, _NT)
            delta = jnp.sum(p * dp, axis=-1, keepdims=True)
            ds = (p * (dp - delta) * X_SCALE).astype(BF16)
            dq_ref[:, cols] = _dot(ds, kh).astype(BF16)
            dkv_ref[:, cols] += _dot(ds, qh, _TN)
            dkv_ref[:, vcols] += _dot(p.astype(BF16), doh, _TN)

    q_spec = pl.BlockSpec((tq, X_W), lambda i: (i, 0))
    return pl.pallas_call(
        body, name=name,
        out_shape=(jax.ShapeDtypeStruct((S, X_W), BF16), jax.ShapeDtypeStruct((M, 2 * X_W), F32)),
        grid=(S // tq,),
        in_specs=[q_spec, pl.BlockSpec((M, 2 * X_W), lambda i: (0, 0)),
                  pl.BlockSpec((X_HEADS, tq, 1), lambda i: (0, i, 0)), q_spec],
        out_specs=(q_spec, pl.BlockSpec((M, 2 * X_W), lambda i: (0, 0))),
        compiler_params=_cp(("arbitrary",)),
    )(q, kv, lse, do)


def _shift_down(h, row):
    return jnp.where(row == 0, 0.0, pltpu.roll(h, 1, 0))


def _shift_up(h, row, S):
    return jnp.where(row == S - 1, 0.0, pltpu.roll(h, S - 1, 0))


def _conv3(h, ck, cb, row, S):
    return _shift_down(h, row) * ck[0:1] + h * ck[1:2] + _shift_up(h, row, S) * ck[2:3] + cb


def _conv_act_fwd(h, ck, cb, *, name):
    S, F2 = h.shape
    F = F2 // 2
    nt = F // LANES

    def body(ha_ref, hb_ref, cka_ref, ckb_ref, cba_ref, cbb_ref, o_ref):
        row = lax.broadcasted_iota(jnp.int32, (S, LANES), 0)
        a = _conv3(ha_ref[...], cka_ref[...], cba_ref[...], row, S)
        b = _conv3(hb_ref[...], ckb_ref[...], cbb_ref[...], row, S)
        o_ref[...] = (_gelu(a) * b).astype(BF16)

    ca = lambda j: (0, j)
    cbi = lambda j: (0, j + nt)
    return pl.pallas_call(
        body, name=name,
        out_shape=jax.ShapeDtypeStruct((S, F), BF16),
        grid=(nt,),
        in_specs=[pl.BlockSpec((S, LANES), ca), pl.BlockSpec((S, LANES), cbi), pl.BlockSpec((3, LANES), ca),
                  pl.BlockSpec((3, LANES), cbi), pl.BlockSpec((1, LANES), ca), pl.BlockSpec((1, LANES), cbi)],
        out_specs=pl.BlockSpec((S, LANES), ca),
        compiler_params=_cp(("parallel",)),
    )(h, h, ck, ck, cb, cb)


def _conv_act_bwd(h, ck, cb, dact, *, name):
    S, F2 = h.shape
    F = F2 // 2
    nt = F // LANES

    def body(ha_ref, hb_ref, cka_ref, ckb_ref, cba_ref, cbb_ref, d_ref,
             dha_ref, dhb_ref, dcka_ref, dckb_ref, dcba_ref, dcbb_ref):
        row = lax.broadcasted_iota(jnp.int32, (S, LANES), 0)
        ha, hb = ha_ref[...], hb_ref[...]
        cka, ckb = cka_ref[...], ckb_ref[...]
        a = _conv3(ha, cka, cba_ref[...], row, S)
        b = _conv3(hb, ckb, cbb_ref[...], row, S)
        d = d_ref[...]
        da = d * b * _gelu_grad(a)
        db = d * _gelu(a)
        for dd, hh, ck_, dh_ref, dck_ref, dcb_ref in ((da, ha, cka, dha_ref, dcka_ref, dcba_ref),
                                                      (db, hb, ckb, dhb_ref, dckb_ref, dcbb_ref)):
            dcb_ref[...] = jnp.sum(dd, axis=0, keepdims=True)
            dck_ref[0:1, :] = jnp.sum(dd * _shift_down(hh, row), axis=0, keepdims=True)
            dck_ref[1:2, :] = jnp.sum(dd * hh, axis=0, keepdims=True)
            dck_ref[2:3, :] = jnp.sum(dd * _shift_up(hh, row, S), axis=0, keepdims=True)
            dh = _shift_up(dd, row, S) * ck_[0:1] + dd * ck_[1:2] + _shift_down(dd, row) * ck_[2:3]
            dh_ref[...] = dh.astype(BF16)

    ca = lambda j: (0, j)
    cbi = lambda j: (0, j + nt)
    col = pl.BlockSpec((S, LANES), ca)
    return pl.pallas_call(
        body, name=name,
        out_shape=(jax.ShapeDtypeStruct((S, F), BF16), jax.ShapeDtypeStruct((S, F), BF16),
                   jax.ShapeDtypeStruct((3, F), F32), jax.ShapeDtypeStruct((3, F), F32),
                   jax.ShapeDtypeStruct((1, F), F32), jax.ShapeDtypeStruct((1, F), F32)),
        grid=(nt,),
        in_specs=[col, pl.BlockSpec((S, LANES), cbi), pl.BlockSpec((3, LANES), ca), pl.BlockSpec((3, LANES), cbi),
                  pl.BlockSpec((1, LANES), ca), pl.BlockSpec((1, LANES), cbi), col],
        out_specs=(col, col, pl.BlockSpec((3, LANES), ca), pl.BlockSpec((3, LANES), ca),
                   pl.BlockSpec((1, LANES), ca), pl.BlockSpec((1, LANES), ca)),
        compiler_params=_cp(("parallel",)),
    )(h, h, ck, ck, cb, cb, dact)


def _layer_fwd(x, xb, memb, w, tabs, seg, l):
    n = lambda s: f"L{l}_{s}"
    qg2 = jnp.tile(w["b_q_gain"], 2)[None, :]
    kg2 = jnp.tile(w["b_k_gain"], 2)[None, :]
    proj = _mm(xb, w["w_in"], name=n("proj"))
    aq, ak4, av4, bq, bk4, bv4 = _prep(proj, tabs, qg2, kg2, seg, name=n("prep"))
    oa, lse_a = _attn_win_fwd(aq, ak4, av4, w["a_sink"], name=n("attn_win"))
    ob, lse_b = _attn_dense_fwd(bq, bk4, bv4, name=n("attn_dense"))
    oc = _gmlp_fwd(proj, w["c_ws"], w["c_bs3"], w["c_ln_g"], w["c_ln_b"], name=n("gmlp"))
    merged = _merge_fwd(oa, ob, oc, w["w_branch"], proj, w["b_gate"], name=n("merge"))
    x1, x1b, xh1, rs1 = _mm_res_ln(merged, w["w_mix_out"], x, w["ln1_g"], w["ln1_b"], name=n("mix_ln1"))
    xq = _mm(x1b, w["x_wq"], out_dtype=BF16, name=n("xq"))
    xkv = _mm(memb, w["x_wkv"], out_dtype=BF16, name=n("xkv"))
    xo, lse_x = _xattn_fwd(xq, xkv, name=n("xattn"))
    x2, x2b, xh2, rs2 = _mm_res_ln(xo, w["x_wo"], x1, w["ln2_g"], w["ln2_b"], name=n("xo_ln2"))
    h = _mm(x2b, w["f_w_up"], name=n("ffn_up"))
    act = _conv_act_fwd(h, w["f_conv_k"], w["f_conv_b"], name=n("conv_act"))
    x3, x3b, xh3, rs3 = _mm_res_ln(act, w["f_w_down"], x2, w["ln3_g"], w["ln3_b"], name=n("down_ln3"))
    saved = dict(xb=xb, proj=proj, aq=aq, ak4=ak4, av4=av4, bq=bq, bk4=bk4, bv4=bv4, lse_a=lse_a, lse_b=lse_b,
                 oa=oa, ob=ob, oc=oc, merged=merged, xh1=xh1, rs1=rs1, x1b=x1b, xq=xq, xkv=xkv, xo=xo, lse_x=lse_x,
                 xh2=xh2, rs2=rs2, x2b=x2b, h=h, act=act, xh3=xh3, rs3=rs3, qg2=qg2, kg2=kg2)
    return x3, x3b, saved


def _layer_bwd(dy, memb, w, sv, tabs, seg, l):
    n = lambda s: f"L{l}_{s}"
    g = {}
    dz3, dz3b, g["ln3_g"], g["ln3_b"] = _ln_bwd(dy, sv["xh3"], sv["rs3"], w["ln3_g"], name=n("ln3_bwd"))
    g["f_w_down"] = _mm(sv["act"], dz3b, ta=True, name=n("dw_down"))
    dact = _mm(dz3b, w["f_w_down"], tb=True, name=n("dact"))
    dha, dhb, dcka, dckb, dcba, dcbb = _conv_act_bwd(sv["h"], w["f_conv_k"], w["f_conv_b"], dact, name=n("conv_act_bwd"))
    dh = jnp.concatenate([dha, dhb], axis=1)
    g["f_conv_k"] = jnp.concatenate([dcka, dckb], axis=1)
    g["f_conv_b"] = jnp.concatenate([dcba, dcbb], axis=1)[0]
    g["f_w_up"] = _mm(sv["x2b"], dh, ta=True, name=n("dw_up"))
    dx2 = _mm(dh, w["f_w_up"], tb=True, res=dz3, res_scale=ALPHA, name=n("dx2"))
    dz2, dz2b, g["ln2_g"], g["ln2_b"] = _ln_bwd(dx2, sv["xh2"], sv["rs2"], w["ln2_g"], name=n("ln2_bwd"))
    g["x_wo"] = _mm(sv["xo"], dz2b, ta=True, name=n("dw_xo"))
    dxo = _mm(dz2b, w["x_wo"], tb=True, out_dtype=BF16, name=n("dxo"))
    dxq, dxkv = _xattn_bwd(sv["xq"], sv["xkv"], sv["lse_x"], dxo, name=n("xattn_bwd"))
    g["x_wq"] = _mm(sv["x1b"], dxq, ta=True, name=n("dw_xq"))
    g["x_wkv"] = _mm(memb, dxkv, ta=True, name=n("dw_xkv"))
    dx1 = _mm(dxq, w["x_wq"], tb=True, res=dz2, res_scale=ALPHA, name=n("dx1"))
    dz1, dz1b, g["ln1_g"], g["ln1_b"] = _ln_bwd(dx1, sv["xh1"], sv["rs1"], w["ln1_g"], name=n("ln1_bwd"))
    g["w_mix_out"] = _mm(sv["merged"], dz1b, ta=True, name=n("dw_mix"))
    dmerged = _mm(dz1b, w["w_mix_out"], tb=True, name=n("dmerged"))
    dgl, dt, dbr, dbg = _merge_bwd(sv["oa"], sv["ob"], sv["oc"], w["w_branch"], sv["proj"], w["b_gate"], dmerged,
                                   name=n("merge_bwd"))
    g["b_gate"] = dbg[0]
    g["w_branch"] = jnp.stack([_mm(sv[k], dt[i], ta=True, name=n(f"dw_branch{i}"))
                               for i, k in enumerate(("oa", "ob", "oc"))])
    dqa, dka, dva, dsink = _attn_win_bwd(sv["aq"], sv["ak4"], sv["av4"], w["a_sink"], sv["lse_a"], dbr[0],
                                         name=n("attn_win_bwd"))
    g["a_sink"] = dsink[:, 0]
    dqb, dkb, dvb = _attn_dense_bwd(sv["bq"], sv["bk4"], sv["bv4"], sv["lse_b"], dbr[1], name=n("attn_dense_bwd"))
    dcz, g["c_ws"], dbs3, dlg, dlb = _gmlp_bwd(sv["proj"], dbr[2], w["c_ws"], w["c_bs3"], w["c_ln_g"], w["c_ln_b"],
                                               name=n("gmlp_bwd"))
    g["c_bs"] = dbs3[:, :, 0]
    g["c_ln_g"], g["c_ln_b"] = dlg[0], dlb[0]
    dqkv, dqg, dkg = _unprep(dqa, dka, dva, dqb, dkb, dvb, sv["proj"], tabs, sv["qg2"], sv["kg2"], seg, name=n("unprep"))
    g["b_q_gain"] = dqg[0, :HEAD_DIM] + dqg[0, HEAD_DIM:]
    g["b_k_gain"] = dkg[0, :HEAD_DIM] + dkg[0, HEAD_DIM:]
    dproj = jnp.concatenate([dqkv, dcz, dgl], axis=1)
    g["w_in"] = _mm(sv["xb"], dproj, ta=True, name=n("dw_in"))
    dx0 = _mm(dproj, w["w_in"], tb=True, res=dz1, res_scale=ALPHA, name=n("dx0"))
    for k in ("ln1_g", "ln1_b", "ln2_g", "ln2_b", "ln3_g", "ln3_b"):
        g[k] = g[k][0]
    return dx0, g


def _local_step(x, mem, target, wl):
    S = x.shape[0]
    tabs = _rope_tables(S)
    seg = _seg_matrix()
    memb = mem.astype(BF16)
    xb = x.astype(BF16)
    saved = []
    for l in range(DEPTH):
        x, xb, sv = _layer_fwd(x, xb, memb, wl[l], tabs, seg, l)
        saved.append(sv)
    dy, loss = _loss_head(x, target, name="loss_head")
    grads = [None] * DEPTH
    for l in reversed(range(DEPTH)):
        dy, grads[l] = _layer_bwd(dy, memb, wl[l], saved[l], tabs, seg, l)
    return loss, dy, grads


MATMUL_WEIGHTS = ("w_in", "w_branch", "w_mix_out", "x_wq", "x_wkv", "x_wo", "f_w_up", "f_w_down", "c_ws")


def _full_weight_dicts(full):
    out = []
    for l in range(DEPTH):
        w = {k: v[l] for k, v in full.items()}
        for k in ("c_ln_g", "c_ln_b", "ln1_g", "ln1_b", "ln2_g", "ln2_b", "ln3_g", "ln3_b", "b_gate", "f_conv_b"):
            w[k] = w[k][None, :]
        w["c_bs3"] = w["c_bs"][:, :, None]
        out.append(w)
    return out


MESH_ID = pl.DeviceIdType.MESH
PACK_W = 1024
_ANY = pl.BlockSpec(memory_space=pl.ANY)


def _all_gather(x, *, name):
    R, C = x.shape

    def body(x_ref, out_ref, send_sems, recv_sems, local_sem):
        mx, my, mc = lax.axis_index("x"), lax.axis_index("y"), lax.axis_index("c")
        me, sibling = (mx, my, mc), (mx, my, 1 - mc)
        chips = [(1 - mx, my), (mx, 1 - my), (1 - mx, 1 - my)]

        def slot(px, py, pc):
            return out_ref.at[4 * px + 2 * py + pc]

        def copy(k, block, to, src=None):
            return pltpu.make_async_remote_copy(
                src_ref=slot(*block) if src is None else src, dst_ref=slot(*block),
                send_sem=send_sems.at[k], recv_sem=recv_sems.at[k], device_id=to, device_id_type=MESH_ID)

        mine = pltpu.make_async_copy(x_ref, slot(*me), local_sem)
        mine.start()
        first = [copy(0, me, sibling, src=x_ref)]
        first += [copy(1 + j, me, (*chip, mc), src=x_ref) for j, chip in enumerate(chips)]
        for cp in first:
            cp.start()
        passed = [copy(4 + j, (*chip, mc), sibling) for j, chip in enumerate(chips)]
        for j, chip in enumerate(chips):
            copy(1 + j, (*chip, mc), me).wait_recv()
            passed[j].start()
        copy(0, sibling, me).wait_recv()
        for j, chip in enumerate(chips):
            copy(4 + j, (*chip, 1 - mc), me).wait_recv()
        for cp in first + passed:
            cp.wait_send()
        mine.wait()

    return pl.pallas_call(
        body, name=name,
        out_shape=jax.ShapeDtypeStruct((N_DEV, R, C), x.dtype),
        in_specs=[_ANY], out_specs=_ANY,
        scratch_shapes=[pltpu.SemaphoreType.DMA((7,)), pltpu.SemaphoreType.DMA((7,)), pltpu.SemaphoreType.DMA],
    )(x)


def _all_to_all(send, *, name):
    _, R, C = send.shape

    def body(send_ref, recv_ref, send_sems, recv_sems, local_sem):
        mx, my, mc = lax.axis_index("x"), lax.axis_index("y"), lax.axis_index("c")
        me = 4 * mx + 2 * my + mc
        mine = pltpu.make_async_copy(send_ref.at[me], recv_ref.at[me], local_sem)
        mine.start()
        copies = []
        for k in range(1, N_DEV):
            px = 1 - mx if k & 4 else mx
            py = 1 - my if k & 2 else my
            pc = 1 - mc if k & 1 else mc
            peer = 4 * px + 2 * py + pc
            out = pltpu.make_async_remote_copy(
                src_ref=send_ref.at[peer], dst_ref=recv_ref.at[me], send_sem=send_sems.at[k - 1],
                recv_sem=recv_sems.at[k - 1], device_id=(px, py, pc), device_id_type=MESH_ID)
            out.start()
            arrives = pltpu.make_async_remote_copy(
                src_ref=send_ref.at[me], dst_ref=recv_ref.at[peer], send_sem=send_sems.at[k - 1],
                recv_sem=recv_sems.at[k - 1], device_id=(px, py, pc), device_id_type=MESH_ID)
            copies.append((out, arrives))
        for out, arrives in copies:
            arrives.wait_recv()
        for out, arrives in copies:
            out.wait_send()
        mine.wait()

    return pl.pallas_call(
        body, name=name,
        out_shape=jax.ShapeDtypeStruct(send.shape, send.dtype),
        in_specs=[_ANY], out_specs=_ANY,
        scratch_shapes=[pltpu.SemaphoreType.DMA((7,)), pltpu.SemaphoreType.DMA((7,)), pltpu.SemaphoreType.DMA],
    )(send)


def _sum_parts(parts, *, name):
    P, R, C = parts.shape
    tr = _tile(R, 56, align=8)

    def body(p_ref, o_ref):
        g = p_ref[0].astype(F32)
        for s in range(1, P):
            g = g + p_ref[s].astype(F32)
        o_ref[...] = g

    return pl.pallas_call(
        body, name=name, out_shape=jax.ShapeDtypeStruct((R, C), F32), grid=(R // tr,),
        in_specs=[pl.BlockSpec((P, tr, C), lambda i: (0, i, 0))], out_specs=pl.BlockSpec((tr, C), lambda i: (i, 0)),
        compiler_params=_cp(("parallel",)),
    )(parts)


def _adamw(parts, row_off, w, m, v, *, tr, name):
    P = parts.shape[0]
    R, C = w.shape
    assert R % tr == 0 and row_off % tr == 0
    off = row_off // tr

    def body(p_ref, w_ref, m_ref, v_ref, g_ref, d_ref, nm_ref, nv_ref):
        g = p_ref[0].astype(F32)
        for s in range(1, P):
            g = g + p_ref[s].astype(F32)
        nm = ADAM_B1 * m_ref[...] + (1.0 - ADAM_B1) * g
        nv = ADAM_B2 * v_ref[...] + (1.0 - ADAM_B2) * (g * g)
        m_hat = nm / (1.0 - ADAM_B1 ** ADAM_STEP)
        v_hat = nv / (1.0 - ADAM_B2 ** ADAM_STEP)
        g_ref[...] = g
        d_ref[...] = -ADAM_LR * (m_hat / (jnp.sqrt(v_hat) + ADAM_EPS) + ADAM_WD * w_ref[...])
        nm_ref[...] = nm
        nv_ref[...] = nv

    blk = pl.BlockSpec((tr, C), lambda i: (i, 0))
    shp = jax.ShapeDtypeStruct((R, C), F32)
    return pl.pallas_call(
        body, name=name, out_shape=(shp, shp, shp, shp), grid=(R // tr,),
        in_specs=[pl.BlockSpec((P, tr, C), lambda i: (0, off + i, 0)), blk, blk, blk],
        out_specs=(blk, blk, blk, blk),
        compiler_params=_cp(("parallel",)),
    )(parts, w, m, v)


WEIGHTS = ("w_in", "b_gate", "a_sink", "b_q_gain", "b_k_gain", "c_ln_g", "c_ln_b", "c_ws", "c_bs", "w_branch",
           "w_mix_out", "ln1_g", "ln1_b", "x_wq", "x_wkv", "x_wo", "ln2_g", "ln2_b", "f_w_up", "f_conv_k",
           "f_conv_b", "f_w_down", "ln3_g", "ln3_b")
BIG_AXIS = {"w_in": 2, "w_branch": 3, "w_mix_out": 1, "x_wq": 1, "x_wkv": 1, "x_wo": 2, "f_w_up": 2, "f_w_down": 1}
BIG = tuple(BIG_AXIS)
SMALL = tuple(k for k in WEIGHTS if k not in BIG_AXIS and k != "f_conv_k")
ADAM_ROWS = 128


def _unshard(g, axis):
    t = jnp.moveaxis(g, 0, axis)
    return t.reshape(t.shape[:axis] + (t.shape[axis] * t.shape[axis + 1],) + t.shape[axis + 2:])


def _reshard(full, axis):
    t = full.reshape(full.shape[:axis] + (N_DEV, full.shape[axis] // N_DEV) + full.shape[axis + 1:])
    return jnp.moveaxis(t, axis, 0)


def _pad_rows(vec, width, row_align):
    n = vec.shape[0]
    rows = -(-n // width)
    rows = -(-rows // row_align) * row_align
    return jnp.pad(vec, (0, rows * width - n)).reshape(rows, width)


def kernel(x, mem, w_in, b_gate, a_sink, b_q_gain, b_k_gain, c_ln_g, c_ln_b, c_ws, c_bs, w_branch, w_mix_out, ln1_g, ln1_b, x_wq, x_wkv, x_wo, ln2_g, ln2_b, f_w_up, f_conv_k, f_conv_b, f_w_down, ln3_g, ln3_b, loss_target, m_w_in, m_b_gate, m_a_sink, m_b_q_gain, m_b_k_gain, m_c_ln_g, m_c_ln_b, m_c_ws, m_c_bs, m_w_branch, m_w_mix_out, m_ln1_g, m_ln1_b, m_x_wq, m_x_wkv, m_x_wo, m_ln2_g, m_ln2_b, m_f_w_up, m_f_conv_k, m_f_conv_b, m_f_w_down, m_ln3_g, m_ln3_b, v_w_in, v_b_gate, v_a_sink, v_b_q_gain, v_b_k_gain, v_c_ln_g, v_c_ln_b, v_c_ws, v_c_bs, v_w_branch, v_w_mix_out, v_ln1_g, v_ln1_b, v_x_wq, v_x_wkv, v_x_wo, v_ln2_g, v_ln2_b, v_f_w_up, v_f_conv_k, v_f_conv_b, v_f_w_down, v_ln3_g, v_ln3_b):
    w = dict(w_in=w_in, b_gate=b_gate, a_sink=a_sink, b_q_gain=b_q_gain, b_k_gain=b_k_gain, c_ln_g=c_ln_g,
             c_ln_b=c_ln_b, c_ws=c_ws, c_bs=c_bs, w_branch=w_branch, w_mix_out=w_mix_out, ln1_g=ln1_g, ln1_b=ln1_b,
             x_wq=x_wq, x_wkv=x_wkv, x_wo=x_wo, ln2_g=ln2_g, ln2_b=ln2_b, f_w_up=f_w_up, f_conv_k=f_conv_k,
             f_conv_b=f_conv_b, f_w_down=f_w_down, ln3_g=ln3_g, ln3_b=ln3_b)
    m = dict(w_in=m_w_in, b_gate=m_b_gate, a_sink=m_a_sink, b_q_gain=m_b_q_gain, b_k_gain=m_b_k_gain,
             c_ln_g=m_c_ln_g, c_ln_b=m_c_ln_b, c_ws=m_c_ws, c_bs=m_c_bs, w_branch=m_w_branch, w_mix_out=m_w_mix_out,
             ln1_g=m_ln1_g, ln1_b=m_ln1_b, x_wq=m_x_wq, x_wkv=m_x_wkv, x_wo=m_x_wo, ln2_g=m_ln2_g, ln2_b=m_ln2_b,
             f_w_up=m_f_w_up, f_conv_k=m_f_conv_k, f_conv_b=m_f_conv_b, f_w_down=m_f_w_down, ln3_g=m_ln3_g,
             ln3_b=m_ln3_b)
    v = dict(w_in=v_w_in, b_gate=v_b_gate, a_sink=v_a_sink, b_q_gain=v_b_q_gain, b_k_gain=v_b_k_gain,
             c_ln_g=v_c_ln_g, c_ln_b=v_c_ln_b, c_ws=v_c_ws, c_bs=v_c_bs, w_branch=v_w_branch, w_mix_out=v_w_mix_out,
             ln1_g=v_ln1_g, ln1_b=v_ln1_b, x_wq=v_x_wq, x_wkv=v_x_wkv, x_wo=v_x_wo, ln2_g=v_ln2_g, ln2_b=v_ln2_b,
             f_w_up=v_f_w_up, f_conv_k=v_f_conv_k, f_conv_b=v_f_conv_b, f_w_down=v_f_w_down, ln3_g=v_ln3_g,
             ln3_b=v_ln3_b)
    me = 4 * lax.axis_index("x") + 2 * lax.axis_index("y") + lax.axis_index("c")

    sizes = {k: w[k].size for k in BIG}
    adam_rows = math.gcd(ADAM_ROWS, *[s // PACK_W for s in sizes.values()])
    assert all(s % PACK_W == 0 for s in sizes.values()) and adam_rows % 16 == 0
    row_off, acc = {}, 0
    for k in BIG:
        row_off[k] = acc
        acc += sizes[k] // PACK_W
    packed = jnp.concatenate([w[k].astype(BF16).reshape(-1, PACK_W) for k in BIG], axis=0)
    gathered = _all_gather(packed, name="gather_weights")
    full = {k: w[k] for k in SMALL}
    for k in BIG:
        seg = gathered[:, row_off[k]:row_off[k] + sizes[k] // PACK_W].reshape((N_DEV,) + w[k].shape)
        full[k] = _unshard(seg, BIG_AXIS[k])
    ck_local = w["f_conv_k"]
    ck_g = _all_gather(_pad_rows(ck_local.reshape(-1), PACK_W, 8), name="gather_conv_taps")
    ck_g = ck_g.reshape(N_DEV, -1)[:, :ck_local.size].reshape((N_DEV,) + ck_local.shape)
    full["f_conv_k"] = _unshard(ck_g, 2)
    full["c_ws"] = full["c_ws"].astype(BF16)

    loss, grad_x, grads = _local_step(x[0], mem[0], loss_target[0], _full_weight_dicts(full))
    loss = lax.psum(loss[0, 0], ("x", "y", "c"))
    gfull = {k: jnp.stack([grads[l][k] for l in range(DEPTH)]) for k in WEIGHTS}

    send = jnp.concatenate([_reshard(gfull[k], BIG_AXIS[k]).astype(BF16).reshape(N_DEV, -1, PACK_W) for k in BIG],
                           axis=1)
    recv = _all_to_all(send, name="scatter_grads")
    out_g, out_d, out_m, out_v = {}, {}, {}, {}
    for k in BIG:
        shp = w[k].shape
        r2 = lambda t: t.reshape(-1, PACK_W)
        g_, d_, m_, v_ = _adamw(recv, row_off[k], r2(w[k]), r2(m[k]), r2(v[k]), tr=adam_rows, name=f"adamw_{k}")
        out_g[k], out_d[k], out_m[k], out_v[k] = (t.reshape(shp) for t in (g_, d_, m_, v_))

    small_all = SMALL + ("f_conv_k",)
    svec = _pad_rows(jnp.concatenate([gfull[k].reshape(-1) for k in small_all]), PACK_W, 8)
    ssum = _sum_parts(_all_gather(svec, name="gather_small_grads"), name="sum_small_grads").reshape(-1)
    sg, acc = {}, 0
    for k in small_all:
        sg[k] = ssum[acc:acc + gfull[k].size].reshape(gfull[k].shape)
        acc += gfull[k].size
    width = ck_local.shape[2]
    sg["f_conv_k"] = lax.dynamic_slice_in_dim(sg["f_conv_k"], me * width, width, axis=2)
    pack = lambda d: _pad_rows(jnp.concatenate([d[k].reshape(-1) for k in small_all]), PACK_W, 8)
    pw = pack(w)
    g_, d_, m_, v_ = _adamw(pack(sg)[None], 0, pw, pack(m), pack(v), tr=pw.shape[0], name="adamw_small")
    acc = 0
    for k in small_all:
        n, shp = w[k].size, w[k].shape
        out_g[k] = sg[k]
        out_d[k], out_m[k], out_v[k] = (t.reshape(-1)[acc:acc + n].reshape(shp) for t in (d_, m_, v_))
        acc += n

    return (loss, grad_x[None], *[out_g[k] for k in WEIGHTS], *[out_d[k] for k in WEIGHTS],
            *[out_m[k] for k in WEIGHTS], *[out_v[k] for k in WEIGHTS])
```

```python
import functools
import math

import jax
import jax.numpy as jnp
from jax import lax
from jax.experimental import pallas as pl
from jax.experimental.pallas import tpu as pltpu

F32 = jnp.float32
BF16 = jnp.bfloat16

DEPTH = 4
HEAD_DIM = 64
BLOCK = 128
WINDOW = 128
GRID_W = 64
C_WIDTH = 512
C_GROUPS = 4
CHUNK = 128
N_BRANCH = 3
BRANCH_WIDTH = 512
ROPE_THETA = 10000.0
X_HEADS = 4
X_HEAD_DIM = 128
ALPHA = (2 * DEPTH) ** 0.25
LN_EPS = 1e-5
RMS_EPS = 1e-6
ADAM_LR = 0.001
ADAM_B1 = 0.9
ADAM_B2 = 0.999
ADAM_EPS = 1e-08
ADAM_WD = 0.01
ADAM_STEP = 10
N_DEV = 8

COL_A = 0
COL_B = 768
COL_C = 1536
COL_GATE = 2560
QKV_W = 768

LANES = 128
V7X_VMEM_BYTES = 64 * 1024 * 1024
VMEM_LIMIT = V7X_VMEM_BYTES - 8 * 1024 * 1024
NEG_BIG = -1e30

_NT = (((1,), (1,)), ((), ()))
_TN = (((0,), (0,)), ((), ()))
_NN = (((1,), (0,)), ((), ()))


def _cp(sem=None):
    return pltpu.CompilerParams(dimension_semantics=sem, vmem_limit_bytes=VMEM_LIMIT)


def _tile(n, target, align=LANES):
    if n <= target:
        return n
    best = None
    for t in range(align, target + 1, align):
        if n % t == 0:
            best = t
    assert best is not None, (n, target)
    return best


def _dot(a, b, dims=_NN):
    return lax.dot_general(a, b, dims, preferred_element_type=F32)


def _gelu(x):
    return 0.5 * x * (1.0 + lax.erf(x * 0.7071067811865476))


def _gelu_grad(x):
    return 0.5 * (1.0 + lax.erf(x * 0.7071067811865476)) + x * jnp.exp(-0.5 * x * x) * 0.3989422804014327


def _sigmoid(x):
    return 1.0 / (1.0 + jnp.exp(-x))


MM_TM, MM_TN, MM_TK = 1024, 1536, 2048


def _mm(a, b, *, ta=False, tb=False, out_dtype=F32, res=None, res_scale=1.0, into=None, name):
    if ta:
        K, M = a.shape
    else:
        M, K = a.shape
    if tb:
        N, Kb = b.shape
    else:
        Kb, N = b.shape
    assert K == Kb, (a.shape, b.shape, ta, tb)
    tm, tn, tk = _tile(M, MM_TM), _tile(N, MM_TN), _tile(K, MM_TK)
    nk = K // tk
    dims = (((0 if ta else 1,), (1 if tb else 0,)), ((), ()))
    n_in = 2 + (res is not None) + (into is not None)

    def body(*refs):
        a_ref, b_ref = refs[0], refs[1]
        r_ref = refs[2] if res is not None else None
        o_ref = refs[n_in]

        def finish(out):
            if r_ref is not None:
                out = out + res_scale * r_ref[...]
            o_ref[...] = out.astype(o_ref.dtype)

        prod = _dot(a_ref[...].astype(BF16), b_ref[...].astype(BF16), dims)
        if nk == 1:
            finish(prod)
        else:
            acc = refs[n_in + 1]
            k = pl.program_id(2)

            @pl.when(k == 0)
            def _():
                acc[...] = prod

            @pl.when(k > 0)
            def _():
                acc[...] += prod

            @pl.when(k == nk - 1)
            def _():
                finish(acc[...])

    a_spec = pl.BlockSpec((tk, tm), lambda i, j, k: (k, i)) if ta else pl.BlockSpec((tm, tk), lambda i, j, k: (i, k))
    b_spec = pl.BlockSpec((tn, tk), lambda i, j, k: (j, k)) if tb else pl.BlockSpec((tk, tn), lambda i, j, k: (k, j))
    in_specs = [a_spec, b_spec]
    args = [a, b]
    if res is not None:
        in_specs.append(pl.BlockSpec((tm, tn), lambda i, j, k: (i, j)))
        args.append(res)
    if into is None:
        out_shape = jax.ShapeDtypeStruct((M, N), out_dtype)
        out_spec = pl.BlockSpec((tm, tn), lambda i, j, k: (i, j))
        aliases = {}
    else:
        buf, l = into
        assert buf.shape[1:] == (M, N), (buf.shape, M, N)
        out_shape = jax.ShapeDtypeStruct(buf.shape, buf.dtype)
        out_spec = pl.BlockSpec((None, tm, tn), lambda i, j, k: (l, i, j))
        in_specs.append(pl.BlockSpec(memory_space=pl.ANY))
        args.append(buf)
        aliases = {n_in - 1: 0}
    return pl.pallas_call(
        body, name=name,
        out_shape=out_shape,
        grid=(M // tm, N // tn, nk),
        in_specs=in_specs,
        out_specs=out_spec,
        scratch_shapes=[pltpu.VMEM((tm, tn), F32)] if nk > 1 else [],
        input_output_aliases=aliases,
        compiler_params=_cp(("parallel", "parallel", "arbitrary")),
    )(*args)


def _mm_res_ln(a, w, x, g, b, *, name):
    S, K = a.shape
    D = w.shape[1]
    tm = _tile(S, 256)

    def body(a_ref, w_ref, x_ref, g_ref, b_ref, y_ref, yb_ref, xh_ref, rs_ref):
        h = _dot(a_ref[...], w_ref[...])
        z = ALPHA * x_ref[...] + h
        mu = jnp.mean(z, axis=-1, keepdims=True)
        zc = z - mu
        var = jnp.mean(zc * zc, axis=-1, keepdims=True)
        r = lax.rsqrt(var + LN_EPS)
        xh = zc * r
        y = xh * g_ref[...] + b_ref[...]
        y_ref[...] = y
        yb_ref[...] = y.astype(BF16)
        xh_ref[...] = xh
        rs_ref[...] = r

    row = lambda i: (i, 0)
    full = lambda i: (0, 0)
    return pl.pallas_call(
        body, name=name,
        out_shape=(jax.ShapeDtypeStruct((S, D), F32), jax.ShapeDtypeStruct((S, D), BF16),
                   jax.ShapeDtypeStruct((S, D), F32), jax.ShapeDtypeStruct((S, 1), F32)),
        grid=(S // tm,),
        in_specs=[pl.BlockSpec((tm, K), row), pl.BlockSpec((K, D), full), pl.BlockSpec((tm, D), row),
                  pl.BlockSpec((1, D), full), pl.BlockSpec((1, D), full)],
        out_specs=(pl.BlockSpec((tm, D), row), pl.BlockSpec((tm, D), row), pl.BlockSpec((tm, D), row),
                   pl.BlockSpec((tm, 1), row)),
        compiler_params=_cp(("parallel",)),
    )(a, w, x, g, b)


def _ln_bwd(dy, xh, rs, g, *, name):
    S, D = dy.shape
    tm = _tile(S, 256)

    def body(dy_ref, xh_ref, rs_ref, g_ref, dz_ref, dzb_ref, dg_ref, db_ref):
        @pl.when(pl.program_id(0) == 0)
        def _():
            dg_ref[...] = jnp.zeros_like(dg_ref)
            db_ref[...] = jnp.zeros_like(db_ref)

        dy = dy_ref[...]
        xh = xh_ref[...]
        dxh = dy * g_ref[...]
        m1 = jnp.mean(dxh, axis=-1, keepdims=True)
        m2 = jnp.mean(dxh * xh, axis=-1, keepdims=True)
        dz = rs_ref[...] * (dxh - m1 - xh * m2)
        dz_ref[...] = dz
        dzb_ref[...] = dz.astype(BF16)
        dg_ref[...] += jnp.sum(dy * xh, axis=0, keepdims=True)
        db_ref[...] += jnp.sum(dy, axis=0, keepdims=True)

    row = lambda i: (i, 0)
    full = lambda i: (0, 0)
    return pl.pallas_call(
        body, name=name,
        out_shape=(jax.ShapeDtypeStruct((S, D), F32), jax.ShapeDtypeStruct((S, D), BF16),
                   jax.ShapeDtypeStruct((1, D), F32), jax.ShapeDtypeStruct((1, D), F32)),
        grid=(S // tm,),
        in_specs=[pl.BlockSpec((tm, D), row), pl.BlockSpec((tm, D), row), pl.BlockSpec((tm, 1), row),
                  pl.BlockSpec((1, D), full)],
        out_specs=(pl.BlockSpec((tm, D), row), pl.BlockSpec((tm, D), row), pl.BlockSpec((1, D), full),
                   pl.BlockSpec((1, D), full)),
        compiler_params=_cp(("arbitrary",)),
    )(dy, xh, rs, g)


def _loss_head(y, t, *, name):
    S, D = y.shape
    tm = _tile(S, 512)

    def body(y_ref, t_ref, dy_ref, l_ref):
        @pl.when(pl.program_id(0) == 0)
        def _():
            l_ref[...] = jnp.zeros_like(l_ref)

        e = y_ref[...] - t_ref[...]
        dy_ref[...] = e / D
        l_ref[...] += 0.5 * jnp.sum(jnp.mean(e * e, axis=-1, keepdims=True), axis=0, keepdims=True)

    row = lambda i: (i, 0)
    return pl.pallas_call(
        body, name=name,
        out_shape=(jax.ShapeDtypeStruct((S, D), F32), jax.ShapeDtypeStruct((1, 1), F32)),
        grid=(S // tm,),
        in_specs=[pl.BlockSpec((tm, D), row), pl.BlockSpec((tm, D), row)],
        out_specs=(pl.BlockSpec((tm, D), row), pl.BlockSpec((1, 1), lambda i: (0, 0))),
        compiler_params=_cp(("arbitrary",)),
    )(y, t)


def _rope_tables(S):
    pos = jnp.arange(S, dtype=jnp.int32)
    row = pos // GRID_W
    col = pos % GRID_W

    def cs(p, d):
        half = d // 2
        inv = ROPE_THETA ** (-jnp.arange(half, dtype=F32) * (2.0 / d))
        ang = p.astype(F32)[:, None] * inv[None, :]
        c, s = jnp.cos(ang), jnp.sin(ang)
        return jnp.concatenate([c, c], -1), jnp.concatenate([-s, s], -1)

    ca, sa = cs(pos, HEAD_DIM)
    cr, sr = cs(row, HEAD_DIM // 2)
    cc, sc = cs(col, HEAD_DIM // 2)
    cb, sb = jnp.concatenate([cr, cc], -1), jnp.concatenate([sr, sc], -1)
    two = lambda t: jnp.concatenate([t, t], -1)
    return two(ca), two(sa), two(cb), two(sb)


def _partner(x, lane, width):
    h = width // 2
    return jnp.where(lane % width < h, pltpu.roll(x, LANES - h, 1), pltpu.roll(x, h, 1))


def _rope_fwd(x, c, s, lane, width):
    return x * c + _partner(x, lane, width) * s


def _rope_bwd(dy, c, s, lane, width):
    return dy * c + _partner(dy * s, lane, width)


def _head_sum(x, seg):
    return lax.dot_general(x, seg, _NN, precision=lax.Precision.HIGHEST, preferred_element_type=F32)


def _split_heads(x, lane):
    lo = lane < HEAD_DIM
    r = pltpu.roll(x, HEAD_DIM, 1)
    z = jnp.zeros_like(x)
    return jnp.where(lo, x, z), jnp.where(lo, z, r), jnp.where(lo, r, z), jnp.where(lo, z, x)


def _fold_heads(d0, d1, lane):
    t0 = d0 + pltpu.roll(d0, HEAD_DIM, 1)
    t1 = d1 + pltpu.roll(d1, HEAD_DIM, 1)
    return jnp.where(lane < HEAD_DIM, t0, t1)


def _seg_matrix():
    i = jnp.arange(LANES)
    return (i[:, None] // HEAD_DIM == i[None, :] // HEAD_DIM).astype(F32)


def _prep(proj, tabs, qg2, kg2, seg, *, name):
    S = proj.shape[0]
    ts = _tile(S, 256)
    ca, sa, cb, sb = tabs

    def body(pa_ref, pb_ref, ca_ref, sa_ref, cb_ref, sb_ref, qg_ref, kg_ref, seg_ref,
             aq_ref, ak_ref, av_ref, bq_ref, bk_ref, bv_ref):
        lane = lax.broadcasted_iota(jnp.int32, (ts, LANES), 1)
        ca, sa, cb, sb = ca_ref[...], sa_ref[...], cb_ref[...], sb_ref[...]
        seg = seg_ref[...]

        def norm(x, gain):
            r = lax.rsqrt(_head_sum(x * x, seg) * (1.0 / HEAD_DIM) + RMS_EPS)
            return x * r * gain

        def put(ref, x):
            for i, part in enumerate(_split_heads(x, lane)):
                ref[i] = part.astype(BF16)

        for gidx in range(4):
            cols = slice(gidx * LANES, (gidx + 1) * LANES)
            aq_ref[:, cols] = (_rope_fwd(pa_ref[:, cols], ca, sa, lane, HEAD_DIM) * 0.125).astype(BF16)
            bq = norm(pb_ref[:, cols], qg_ref[...])
            bq_ref[:, cols] = (_rope_fwd(bq, cb, sb, lane, HEAD_DIM // 2) * 0.125).astype(BF16)
        put(ak_ref, _rope_fwd(pa_ref[:, 512:640], ca, sa, lane, HEAD_DIM))
        put(av_ref, pa_ref[:, 640:768])
        bk = norm(pb_ref[:, 512:640], kg_ref[...])
        put(bk_ref, _rope_fwd(bk, cb, sb, lane, HEAD_DIM // 2))
        put(bv_ref, pb_ref[:, 640:768])

    row = lambda i: (i, 0)
    full = lambda i: (0, 0)
    tab = pl.BlockSpec((ts, LANES), row)
    kv_shape = jax.ShapeDtypeStruct((4, S, LANES), BF16)
    kv_spec = pl.BlockSpec((4, ts, LANES), lambda i: (0, i, 0))
    q_shape = jax.ShapeDtypeStruct((S, 512), BF16)
    q_spec = pl.BlockSpec((ts, 512), row)
    return pl.pallas_call(
        body, name=name,
        out_shape=(q_shape, kv_shape, kv_shape, q_shape, kv_shape, kv_shape),
        grid=(S // ts,),
        in_specs=[pl.BlockSpec((ts, QKV_W), lambda i: (i, 0)), pl.BlockSpec((ts, QKV_W), lambda i: (i, 1)),
                  tab, tab, tab, tab, pl.BlockSpec((1, LANES), full), pl.BlockSpec((1, LANES), full),
                  pl.BlockSpec((LANES, LANES), full)],
        out_specs=(q_spec, kv_spec, kv_spec, q_spec, kv_spec, kv_spec),
        compiler_params=_cp(("parallel",)),
    )(proj, proj, ca, sa, cb, sb, qg2, kg2, seg)


def _unprep(dqa, dka, dva, dqb, dkb, dvb, proj, tabs, qg2, kg2, seg, *, name):
    S = proj.shape[0]
    ts = _tile(S, 256)
    ca, sa, cb, sb = tabs

    def body(dqa_ref, dka_ref, dva_ref, dqb_ref, dkb_ref, dvb_ref, pb_ref, ca_ref, sa_ref, cb_ref, sb_ref,
             qg_ref, kg_ref, seg_ref, dp_ref, dqg_ref, dkg_ref):
        @pl.when(pl.program_id(0) == 0)
        def _():
            dqg_ref[...] = jnp.zeros_like(dqg_ref)
            dkg_ref[...] = jnp.zeros_like(dkg_ref)

        lane = lax.broadcasted_iota(jnp.int32, (ts, LANES), 1)
        ca, sa, cb, sb = ca_ref[...], sa_ref[...], cb_ref[...], sb_ref[...]
        seg = seg_ref[...]

        def norm_bwd(dy, x, gain):
            r = lax.rsqrt(_head_sum(x * x, seg) * (1.0 / HEAD_DIM) + RMS_EPS)
            gdy = gain * dy
            dot = _head_sum(gdy * x, seg) * (1.0 / HEAD_DIM)
            dx = r * gdy - x * (r * r * r) * dot
            return dx, jnp.sum(dy * x * r, axis=0, keepdims=True)

        for gidx in range(4):
            cols = slice(gidx * LANES, (gidx + 1) * LANES)
            dp_ref[:, cols] = _rope_bwd(dqa_ref[:, cols] * 0.125, ca, sa, lane, HEAD_DIM).astype(BF16)
            dbq = _rope_bwd(dqb_ref[:, cols] * 0.125, cb, sb, lane, HEAD_DIM // 2)
            dx, dg = norm_bwd(dbq, pb_ref[:, cols], qg_ref[...])
            dp_ref[:, COL_B + gidx * LANES:COL_B + (gidx + 1) * LANES] = dx.astype(BF16)
            dqg_ref[...] += dg
        dak = _fold_heads(dka_ref[0] + dka_ref[1], dka_ref[2] + dka_ref[3], lane)
        dp_ref[:, 512:640] = _rope_bwd(dak, ca, sa, lane, HEAD_DIM).astype(BF16)
        dp_ref[:, 640:768] = _fold_heads(dva_ref[0] + dva_ref[1], dva_ref[2] + dva_ref[3], lane).astype(BF16)
        dbk = _fold_heads(dkb_ref[0] + dkb_ref[1], dkb_ref[2] + dkb_ref[3], lane)
        dbk = _rope_bwd(dbk, cb, sb, lane, HEAD_DIM // 2)
        dx, dg = norm_bwd(dbk, pb_ref[:, 512:640], kg_ref[...])
        dp_ref[:, COL_B + 512:COL_B + 640] = dx.astype(BF16)
        dkg_ref[...] += dg
        dp_ref[:, COL_B + 640:COL_B + 768] = _fold_heads(dvb_ref[0] + dvb_ref[1], dvb_ref[2] + dvb_ref[3],
                                                         lane).astype(BF16)

    row = lambda i: (i, 0)
    full = lambda i: (0, 0)
    tab = pl.BlockSpec((ts, LANES), row)
    q_spec = pl.BlockSpec((ts, 512), row)
    kv_spec = pl.BlockSpec((4, ts, LANES), lambda i: (0, i, 0))
    return pl.pallas_call(
        body, name=name,
        out_shape=(jax.ShapeDtypeStruct((S, 2 * QKV_W), BF16), jax.ShapeDtypeStruct((1, LANES), F32),
                   jax.ShapeDtypeStruct((1, LANES), F32)),
        grid=(S // ts,),
        in_specs=[q_spec, kv_spec, kv_spec, q_spec, kv_spec, kv_spec,
                  pl.BlockSpec((ts, QKV_W), lambda i: (i, 1)), tab, tab, tab, tab,
                  pl.BlockSpec((1, LANES), full), pl.BlockSpec((1, LANES), full), pl.BlockSpec((LANES, LANES), full)],
        out_specs=(pl.BlockSpec((ts, 2 * QKV_W), row), pl.BlockSpec((1, LANES), full),
                   pl.BlockSpec((1, LANES), full)),
        compiler_params=_cp(("arbitrary",)),
    )(dqa, dka, dva, dqb, dkb, dvb, proj, ca, sa, cb, sb, qg2, kg2, seg)


def _attn_dense_fwd(q, k4, v4, *, name):
    S = q.shape[0]
    tq = _tile(S, 256)

    def body(q_ref, k_ref, v_ref, o_ref, lse_ref):
        for pr in range(2):
            qp = q_ref[:, pr * LANES:(pr + 1) * LANES]
            acc = None
            for half in range(2):
                s = _dot(qp, k_ref[half], _NT)
                m = jnp.max(s, axis=-1, keepdims=True)
                e = jnp.exp(s - m)
                l = jnp.sum(e, axis=-1, keepdims=True)
                pv = _dot(e.astype(BF16), v_ref[half]) * (1.0 / l)
                acc = pv if acc is None else acc + pv
                lse_ref[pr * 2 + half] = m + jnp.log(l)
            o_ref[:, pr * LANES:(pr + 1) * LANES] = acc.astype(BF16)

    kv_spec = pl.BlockSpec((2, S, LANES), lambda kv, i: (kv, 0, 0))
    return pl.pallas_call(
        body, name=name,
        out_shape=(jax.ShapeDtypeStruct((S, 512), BF16), jax.ShapeDtypeStruct((8, S, 1), F32)),
        grid=(2, S // tq),
        in_specs=[pl.BlockSpec((tq, 256), lambda kv, i: (i, kv)), kv_spec, kv_spec],
        out_specs=(pl.BlockSpec((tq, 256), lambda kv, i: (i, kv)),
                   pl.BlockSpec((4, tq, 1), lambda kv, i: (kv, i, 0))),
        compiler_params=_cp(("parallel", "parallel")),
    )(q, k4, v4)


def _attn_dense_bwd(q, k4, v4, lse, do, *, name):
    S = q.shape[0]
    tq = _tile(S, 256)

    def body(q_ref, k_ref, v_ref, lse_ref, do_ref, dq_ref, dk_ref, dv_ref):
        @pl.when(pl.program_id(1) == 0)
        def _():
            dk_ref[...] = jnp.zeros_like(dk_ref)
            dv_ref[...] = jnp.zeros_like(dv_ref)

        lane = lax.broadcasted_iota(jnp.int32, (tq, LANES), 1)
        for pr in range(2):
            qp = q_ref[:, pr * LANES:(pr + 1) * LANES]
            dop = do_ref[:, pr * LANES:(pr + 1) * LANES].astype(BF16)
            dq = None
            for half in range(2):
                mine = (lane < HEAD_DIM) if half == 0 else (lane >= HEAD_DIM)
                s = _dot(qp, k_ref[half], _NT)
                p = jnp.exp(s - lse_ref[pr * 2 + half])
                dp = _dot(dop, v_ref[half], _NT)
                delta = jnp.sum(p * dp, axis=-1, keepdims=True)
                ds = (p * (dp - delta)).astype(BF16)
                pb = p.astype(BF16)
                d = _dot(ds, k_ref[half])
                dq = d if dq is None else dq + d
                dk_ref[half] += _dot(ds, jnp.where(mine, qp, jnp.zeros_like(qp)), _TN)
                dv_ref[half] += _dot(pb, jnp.where(mine, dop, jnp.zeros_like(dop)), _TN)
            dq_ref[:, pr * LANES:(pr + 1) * LANES] = dq

    kv_spec = pl.BlockSpec((2, S, LANES), lambda kv, i: (kv, 0, 0))
    q_spec = pl.BlockSpec((tq, 256), lambda kv, i: (i, kv))
    return pl.pallas_call(
        body, name=name,
        out_shape=(jax.ShapeDtypeStruct((S, 512), F32), jax.ShapeDtypeStruct((4, S, LANES), F32),
                   jax.ShapeDtypeStruct((4, S, LANES), F32)),
        grid=(2, S // tq),
        in_specs=[q_spec, kv_spec, kv_spec, pl.BlockSpec((4, tq, 1), lambda kv, i: (kv, i, 0)), q_spec],
        out_specs=(q_spec, kv_spec, kv_spec),
        compiler_params=_cp(("parallel", "arbitrary")),
    )(q, k4, v4, lse, do)


WIN_KEYS = 3 * BLOCK


def _win_start(n, S):
    return pl.multiple_of(jnp.clip((n - 1) * BLOCK, 0, S - WIN_KEYS), BLOCK)


def _win_valid(n, start):
    qpos = n * BLOCK + lax.broadcasted_iota(jnp.int32, (BLOCK, WIN_KEYS), 0)
    kpos = start + lax.broadcasted_iota(jnp.int32, (BLOCK, WIN_KEYS), 1)
    return jnp.abs(qpos - kpos) <= WINDOW


def _attn_win_fwd(q, k4, v4, sink, *, name):
    S = q.shape[0]
    assert S >= WIN_KEYS

    def body(sink_ref, q_ref, k_ref, v_ref, o_ref, lse_ref):
        n = pl.program_id(0)
        start = _win_start(n, S)
        valid = _win_valid(n, start)
        for kv in range(2):
            for pr in range(2):
                cols = slice((kv * 2 + pr) * LANES, (kv * 2 + pr + 1) * LANES)
                qp = q_ref[:, cols]
                acc = None
                for half in range(2):
                    h = kv * 4 + pr * 2 + half
                    kk = k_ref[kv * 2 + half, pl.ds(start, WIN_KEYS), :]
                    vv = v_ref[kv * 2 + half, pl.ds(start, WIN_KEYS), :]
                    s = jnp.where(valid, _dot(qp, kk, _NT), NEG_BIG)
                    snk = sink_ref[h]
                    m = jnp.maximum(jnp.max(s, axis=-1, keepdims=True), snk)
                    e = jnp.exp(s - m)
                    l = jnp.sum(e, axis=-1, keepdims=True) + jnp.exp(snk - m)
                    p = (e * (1.0 / l)).astype(BF16)
                    pv = _dot(p, vv)
                    acc = pv if acc is None else acc + pv
                    lse_ref[h] = m + jnp.log(l)
                o_ref[:, cols] = acc.astype(BF16)

    kv_spec = pl.BlockSpec((4, S, LANES), lambda n: (0, 0, 0))
    return pl.pallas_call(
        body, name=name,
        out_shape=(jax.ShapeDtypeStruct((S, 512), BF16), jax.ShapeDtypeStruct((8, S, 1), F32)),
        grid=(S // BLOCK,),
        in_specs=[pl.BlockSpec(memory_space=pltpu.SMEM), pl.BlockSpec((BLOCK, 512), lambda n: (n, 0)),
                  kv_spec, kv_spec],
        out_specs=(pl.BlockSpec((BLOCK, 512), lambda n: (n, 0)), pl.BlockSpec((8, BLOCK, 1), lambda n: (0, n, 0))),
        compiler_params=_cp(("parallel",)),
    )(sink, q, k4, v4)


def _attn_win_bwd(q, k4, v4, sink, lse, do, *, name):
    S = q.shape[0]

    def body(sink_ref, q_ref, k_ref, v_ref, lse_ref, do_ref, dq_ref, dk_ref, dv_ref, dsink_ref):
        n = pl.program_id(0)

        @pl.when(n == 0)
        def _():
            dk_ref[...] = jnp.zeros_like(dk_ref)
            dv_ref[...] = jnp.zeros_like(dv_ref)
            dsink_ref[...] = jnp.zeros_like(dsink_ref)

        start = _win_start(n, S)
        valid = _win_valid(n, start)
        lane = lax.broadcasted_iota(jnp.int32, (BLOCK, LANES), 1)
        for kv in range(2):
            for pr in range(2):
                cols = slice((kv * 2 + pr) * LANES, (kv * 2 + pr + 1) * LANES)
                qp = q_ref[:, cols]
                dop = do_ref[:, cols].astype(BF16)
                dq = None
                for half in range(2):
                    h = kv * 4 + pr * 2 + half
                    slot = kv * 2 + half
                    mine = (lane < HEAD_DIM) if half == 0 else (lane >= HEAD_DIM)
                    win = pl.ds(start, WIN_KEYS)
                    kk = k_ref[slot, win, :]
                    vv = v_ref[slot, win, :]
                    lse_h = lse_ref[h]
                    s = jnp.where(valid, _dot(qp, kk, _NT), NEG_BIG)
                    p = jnp.exp(s - lse_h)
                    dp = _dot(dop, vv, _NT)
                    delta = jnp.sum(p * dp, axis=-1, keepdims=True)
                    ds = (p * (dp - delta)).astype(BF16)
                    pb = p.astype(BF16)
                    d = _dot(ds, kk)
                    dq = d if dq is None else dq + d
                    dk_ref[slot, win, :] += _dot(ds, jnp.where(mine, qp, jnp.zeros_like(qp)), _TN)
                    dv_ref[slot, win, :] += _dot(pb, jnp.where(mine, dop, jnp.zeros_like(dop)), _TN)
                    p_sink = jnp.exp(sink_ref[h] - lse_h)
                    dsink_ref[h:h + 1, :] += jnp.broadcast_to(-jnp.sum(p_sink * delta, axis=0, keepdims=True),
                                                              (1, LANES))
                dq_ref[:, cols] = dq

    kv_spec = pl.BlockSpec((4, S, LANES), lambda n: (0, 0, 0))
    q_spec = pl.BlockSpec((BLOCK, 512), lambda n: (n, 0))
    return pl.pallas_call(
        body, name=name,
        out_shape=(jax.ShapeDtypeStruct((S, 512), F32), jax.ShapeDtypeStruct((4, S, LANES), F32),
                   jax.ShapeDtypeStruct((4, S, LANES), F32), jax.ShapeDtypeStruct((8, LANES), F32)),
        grid=(S // BLOCK,),
        in_specs=[pl.BlockSpec(memory_space=pltpu.SMEM), q_spec, kv_spec, kv_spec,
                  pl.BlockSpec((8, BLOCK, 1), lambda n: (0, n, 0)), q_spec],
        out_specs=(q_spec, kv_spec, kv_spec, pl.BlockSpec((8, LANES), lambda n: (0, 0))),
        compiler_params=_cp(("arbitrary",)),
    )(sink, q, k4, v4, lse, do)


def _c_ln(v, g, b):
    mu = jnp.mean(v, axis=-1, keepdims=True)
    vc = v - mu
    r = lax.rsqrt(jnp.mean(vc * vc, axis=-1, keepdims=True) + LN_EPS)
    vh = vc * r
    return vh, r, vh * g + b


def _gmlp_fwd(proj, ws, bs3, lg, lb, *, name):
    S = proj.shape[0]

    def body(u_ref, v_ref, ws_ref, bs_ref, lg_ref, lb_ref, o_ref):
        u = _gelu(u_ref[...])
        _, _, vn = _c_ln(_gelu(v_ref[...]), lg_ref[...], lb_ref[...])
        vn = vn.astype(BF16)
        for gi in range(C_GROUPS):
            cols = slice(gi * LANES, (gi + 1) * LANES)
            mixed = _dot(ws_ref[gi], vn[:, cols]) + bs_ref[gi]
            o_ref[:, cols] = (u[:, cols] * mixed).astype(BF16)

    full2 = lambda n: (0, 0)
    full3 = lambda n: (0, 0, 0)
    return pl.pallas_call(
        body, name=name,
        out_shape=jax.ShapeDtypeStruct((S, C_WIDTH), BF16),
        grid=(S // CHUNK,),
        in_specs=[pl.BlockSpec((CHUNK, C_WIDTH), lambda n: (n, COL_C // C_WIDTH)),
                  pl.BlockSpec((CHUNK, C_WIDTH), lambda n: (n, COL_C // C_WIDTH + 1)),
                  pl.BlockSpec((C_GROUPS, CHUNK, CHUNK), full3), pl.BlockSpec((C_GROUPS, CHUNK, 1), full3),
                  pl.BlockSpec((1, C_WIDTH), full2), pl.BlockSpec((1, C_WIDTH), full2)],
        out_specs=pl.BlockSpec((CHUNK, C_WIDTH), lambda n: (n, 0)),
        compiler_params=_cp(("parallel",)),
    )(proj, proj, ws, bs3, lg, lb)


def _gmlp_bwd(proj, dout, ws, bs3, lg, lb, *, name):
    S = proj.shape[0]

    def body(u_ref, v_ref, d_ref, ws_ref, bs_ref, lg_ref, lb_ref, dz_ref, dws_ref, dbs_ref, dlg_ref, dlb_ref):
        @pl.when(pl.program_id(0) == 0)
        def _():
            dws_ref[...] = jnp.zeros_like(dws_ref)
            dbs_ref[...] = jnp.zeros_like(dbs_ref)
            dlg_ref[...] = jnp.zeros_like(dlg_ref)
            dlb_ref[...] = jnp.zeros_like(dlb_ref)

        u_pre, v_pre, d = u_ref[...], v_ref[...], d_ref[...]
        u = _gelu(u_pre)
        vh, r, vn = _c_ln(_gelu(v_pre), lg_ref[...], lb_ref[...])
        vnb = vn.astype(BF16)
        du_parts, dvn_parts = [], []
        for gi in range(C_GROUPS):
            cols = slice(gi * LANES, (gi + 1) * LANES)
            mixed = _dot(ws_ref[gi], vnb[:, cols]) + bs_ref[gi]
            du_parts.append(d[:, cols] * mixed)
            dm = d[:, cols] * u[:, cols]
            dbs_ref[gi] += jnp.sum(dm, axis=-1, keepdims=True)
            dmb = dm.astype(BF16)
            dws_ref[gi] += _dot(dmb, vnb[:, cols], _NT)
            dvn_parts.append(_dot(ws_ref[gi], dmb, _TN))
        du = jnp.concatenate(du_parts, axis=-1)
        dvn = jnp.concatenate(dvn_parts, axis=-1)
        dlg_ref[...] += jnp.sum(dvn * vh, axis=0, keepdims=True)
        dlb_ref[...] += jnp.sum(dvn, axis=0, keepdims=True)
        dvh = dvn * lg_ref[...]
        m1 = jnp.mean(dvh, axis=-1, keepdims=True)
        m2 = jnp.mean(dvh * vh, axis=-1, keepdims=True)
        dv = r * (dvh - m1 - vh * m2)
        dz_ref[:, :C_WIDTH] = (du * _gelu_grad(u_pre)).astype(BF16)
        dz_ref[:, C_WIDTH:] = (dv * _gelu_grad(v_pre)).astype(BF16)

    full2 = lambda n: (0, 0)
    full3 = lambda n: (0, 0, 0)
    return pl.pallas_call(
        body, name=name,
        out_shape=(jax.ShapeDtypeStruct((S, 2 * C_WIDTH), BF16), jax.ShapeDtypeStruct((C_GROUPS, CHUNK, CHUNK), F32),
                   jax.ShapeDtypeStruct((C_GROUPS, CHUNK, 1), F32), jax.ShapeDtypeStruct((1, C_WIDTH), F32),
                   jax.ShapeDtypeStruct((1, C_WIDTH), F32)),
        grid=(S // CHUNK,),
        in_specs=[pl.BlockSpec((CHUNK, C_WIDTH), lambda n: (n, COL_C // C_WIDTH)),
                  pl.BlockSpec((CHUNK, C_WIDTH), lambda n: (n, COL_C // C_WIDTH + 1)),
                  pl.BlockSpec((CHUNK, C_WIDTH), lambda n: (n, 0)),
                  pl.BlockSpec((C_GROUPS, CHUNK, CHUNK), full3), pl.BlockSpec((C_GROUPS, CHUNK, 1), full3),
                  pl.BlockSpec((1, C_WIDTH), full2), pl.BlockSpec((1, C_WIDTH), full2)],
        out_specs=(pl.BlockSpec((CHUNK, 2 * C_WIDTH), lambda n: (n, 0)), pl.BlockSpec((C_GROUPS, CHUNK, CHUNK), full3),
                   pl.BlockSpec((C_GROUPS, CHUNK, 1), full3), pl.BlockSpec((1, C_WIDTH), full2),
                   pl.BlockSpec((1, C_WIDTH), full2)),
        compiler_params=_cp(("arbitrary",)),
    )(proj, proj, dout, ws, bs3, lg, lb)


GATE_BLK = 512


def _gate_specs(tm, D):
    nh = D // GATE_BLK
    first = COL_GATE // GATE_BLK
    return [pl.BlockSpec((tm, GATE_BLK), functools.partial(lambda i, c: (i, c), c=first + b))
            for b in range(N_BRANCH * nh)]


def _merge_fwd(oa, ob, oc, wb, proj, bg, *, name):
    S = oa.shape[0]
    D = wb.shape[2]
    assert D % GATE_BLK == 0
    nh = D // GATE_BLK
    tm = _tile(S, 256)

    def body(oa_ref, ob_ref, oc_ref, wb_ref, *rest):
        gate_refs, bg_ref, o_ref = rest[:N_BRANCH * nh], rest[N_BRANCH * nh], rest[N_BRANCH * nh + 1]
        brs = (oa_ref[...], ob_ref[...], oc_ref[...])
        for j in range(nh):
            cols = slice(j * GATE_BLK, (j + 1) * GATE_BLK)
            acc = None
            for n in range(N_BRANCH):
                b = n * nh + j
                t = _dot(brs[n], wb_ref[n, :, cols])
                g = _sigmoid(gate_refs[b][...] + bg_ref[:, b * GATE_BLK:(b + 1) * GATE_BLK])
                acc = t * g if acc is None else acc + t * g
            o_ref[:, cols] = acc.astype(BF16)

    row = lambda i: (i, 0)
    br = pl.BlockSpec((tm, BRANCH_WIDTH), row)
    return pl.pallas_call(
        body, name=name,
        out_shape=jax.ShapeDtypeStruct((S, D), BF16),
        grid=(S // tm,),
        in_specs=[br, br, br, pl.BlockSpec((N_BRANCH, BRANCH_WIDTH, D), lambda i: (0, 0, 0))]
                 + _gate_specs(tm, D) + [pl.BlockSpec((1, N_BRANCH * D), lambda i: (0, 0))],
        out_specs=pl.BlockSpec((tm, D), row),
        compiler_params=_cp(("parallel",)),
    )(oa, ob, oc, wb, *([proj] * (N_BRANCH * nh)), bg)


def _merge_bwd(oa, ob, oc, wb, proj, bg, dmerged, *, name):
    S = oa.shape[0]
    D = wb.shape[2]
    nh = D // GATE_BLK
    tm = _tile(S, 256)

    def body(oa_ref, ob_ref, oc_ref, wb_ref, *rest):
        gate_refs = rest[:N_BRANCH * nh]
        bg_ref, dm_ref, dgl_ref, dt_ref, dbr_ref, dbg_ref = rest[N_BRANCH * nh:]

        @pl.when(pl.program_id(0) == 0)
        def _():
            dbg_ref[...] = jnp.zeros_like(dbg_ref)

        brs = (oa_ref[...], ob_ref[...], oc_ref[...])
        for n in range(N_BRANCH):
            dbr = None
            for j in range(nh):
                cols = slice(j * GATE_BLK, (j + 1) * GATE_BLK)
                b = n * nh + j
                gcols = slice(b * GATE_BLK, (b + 1) * GATE_BLK)
                w = wb_ref[n, :, cols]
                t = _dot(brs[n], w)
                g = _sigmoid(gate_refs[b][...] + bg_ref[:, gcols])
                dm = dm_ref[:, cols]
                dt = (dm * g).astype(BF16)
                dgl = dm * t * g * (1.0 - g)
                dt_ref[n, :, cols] = dt
                dgl_ref[:, gcols] = dgl.astype(BF16)
                dbg_ref[:, gcols] += jnp.sum(dgl, axis=0, keepdims=True)
                d = _dot(dt, w, _NT)
                dbr = d if dbr is None else dbr + d
            dbr_ref[n] = dbr

    row = lambda i: (i, 0)
    br = pl.BlockSpec((tm, BRANCH_WIDTH), row)
    return pl.pallas_call(
        body, name=name,
        out_shape=(jax.ShapeDtypeStruct((S, N_BRANCH * D), BF16), jax.ShapeDtypeStruct((N_BRANCH, S, D), BF16),
                   jax.ShapeDtypeStruct((N_BRANCH, S, BRANCH_WIDTH), F32), jax.ShapeDtypeStruct((1, N_BRANCH * D), F32)),
        grid=(S // tm,),
        in_specs=[br, br, br, pl.BlockSpec((N_BRANCH, BRANCH_WIDTH, D), lambda i: (0, 0, 0))]
                 + _gate_specs(tm, D)
                 + [pl.BlockSpec((1, N_BRANCH * D), lambda i: (0, 0)), pl.BlockSpec((tm, D), row)],
        out_specs=(pl.BlockSpec((tm, N_BRANCH * D), row), pl.BlockSpec((N_BRANCH, tm, D), lambda i: (0, i, 0)),
                   pl.BlockSpec((N_BRANCH, tm, BRANCH_WIDTH), lambda i: (0, i, 0)),
                   pl.BlockSpec((1, N_BRANCH * D), lambda i: (0, 0))),
        compiler_params=_cp(("arbitrary",)),
    )(oa, ob, oc, wb, *([proj] * (N_BRANCH * nh)), bg, dmerged)


X_SCALE = 1.0 / math.sqrt(X_HEAD_DIM)
X_W = X_HEADS * X_HEAD_DIM


def _xattn_fwd(q, kv, *, name):
    S = q.shape[0]
    M = kv.shape[0]
    tq = _tile(S, 512)

    def body(q_ref, kv_ref, o_ref, lse_ref):
        for h in range(X_HEADS):
            cols = slice(h * LANES, (h + 1) * LANES)
            s = _dot(q_ref[:, cols], kv_ref[:, cols], _NT) * X_SCALE
            m = jnp.max(s, axis=-1, keepdims=True)
            e = jnp.exp(s - m)
            l = jnp.sum(e, axis=-1, keepdims=True)
            p = (e * (1.0 / l)).astype(BF16)
            o_ref[:, cols] = _dot(p, kv_ref[:, X_W + h * LANES:X_W + (h + 1) * LANES]).astype(BF16)
            lse_ref[h] = m + jnp.log(l)

    return pl.pallas_call(
        body, name=name,
        out_shape=(jax.ShapeDtypeStruct((S, X_W), BF16), jax.ShapeDtypeStruct((X_HEADS, S, 1), F32)),
        grid=(S // tq,),
        in_specs=[pl.BlockSpec((tq, X_W), lambda i: (i, 0)), pl.BlockSpec((M, 2 * X_W), lambda i: (0, 0))],
        out_specs=(pl.BlockSpec((tq, X_W), lambda i: (i, 0)), pl.BlockSpec((X_HEADS, tq, 1), lambda i: (0, i, 0))),
        compiler_params=_cp(("parallel",)),
    )(q, kv)


def _xattn_bwd(q, kv, lse, do, *, name):
    S = q.shape[0]
    M = kv.shape[0]
    tq = _tile(S, 512)

    def body(q_ref, kv_ref, lse_ref, do_ref, dq_ref, dkv_ref):
        @pl.when(pl.program_id(0) == 0)
        def _():
            dkv_ref[...] = jnp.zeros_like(dkv_ref)

        for h in range(X_HEADS):
            cols = slice(h * LANES, (h + 1) * LANES)
            vcols = slice(X_W + h * LANES, X_W + (h + 1) * LANES)
            qh, kh, vh = q_ref[:, cols], kv_ref[:, cols], kv_ref[:, vcols]
            doh = do_ref[:, cols].astype(BF16)
            p = jnp.exp(_dot(qh, kh, _NT) * X_SCALE - lse_ref[h])
            dp = _dot(doh, vh, _NT)
            delta = jnp.sum(p * dp, axis=-1, keepdims=True)
            ds = (p * (dp - delta) * X_SCALE).astype(BF16)
            dq_ref[:, cols] = _dot(ds, kh).astype(BF16)
            dkv_ref[:, cols] += _dot(ds, qh, _TN)
            dkv_ref[:, vcols] += _dot(p.astype(BF16), doh, _TN)

    q_spec = pl.BlockSpec((tq, X_W), lambda i: (i, 0))
    return pl.pallas_call(
        body, name=name,
        out_shape=(jax.ShapeDtypeStruct((S, X_W), BF16), jax.ShapeDtypeStruct((M, 2 * X_W), F32)),
        grid=(S // tq,),
        in_specs=[q_spec, pl.BlockSpec((M, 2 * X_W), lambda i: (0, 0)),
                  pl.BlockSpec((X_HEADS, tq, 1), lambda i: (0, i, 0)), q_spec],
        out_specs=(q_spec, pl.BlockSpec((M, 2 * X_W), lambda i: (0, 0))),
        compiler_params=_cp(("arbitrary",)),
    )(q, kv, lse, do)


def _shift_down(h, row):
    return jnp.where(row == 0, 0.0, pltpu.roll(h, 1, 0))


def _shift_up(h, row, S):
    return jnp.where(row == S - 1, 0.0, pltpu.roll(h, S - 1, 0))


def _conv3(h, ck, cb, row, S):
    return _shift_down(h, row) * ck[0:1] + h * ck[1:2] + _shift_up(h, row, S) * ck[2:3] + cb


def _conv_act_fwd(h, ck, cb, *, name):
    S, F2 = h.shape
    F = F2 // 2
    nt = F // LANES

    def body(ha_ref, hb_ref, cka_ref, ckb_ref, cba_ref, cbb_ref, o_ref):
        row = lax.broadcasted_iota(jnp.int32, (S, LANES), 0)
        a = _conv3(ha_ref[...], cka_ref[...], cba_ref[...], row, S)
        b = _conv3(hb_ref[...], ckb_ref[...], cbb_ref[...], row, S)
        o_ref[...] = (_gelu(a) * b).astype(BF16)

    ca = lambda j: (0, j)
    cbi = lambda j: (0, j + nt)
    return pl.pallas_call(
        body, name=name,
        out_shape=jax.ShapeDtypeStruct((S, F), BF16),
        grid=(nt,),
        in_specs=[pl.BlockSpec((S, LANES), ca), pl.BlockSpec((S, LANES), cbi), pl.BlockSpec((3, LANES), ca),
                  pl.BlockSpec((3, LANES), cbi), pl.BlockSpec((1, LANES), ca), pl.BlockSpec((1, LANES), cbi)],
        out_specs=pl.BlockSpec((S, LANES), ca),
        compiler_params=_cp(("parallel",)),
    )(h, h, ck, ck, cb, cb)


def _conv_act_bwd(h, ck, cb, dact, *, name):
    S, F2 = h.shape
    F = F2 // 2
    nt = F // LANES

    def body(ha_ref, hb_ref, cka_ref, ckb_ref, cba_ref, cbb_ref, d_ref,
             dha_ref, dhb_ref, dcka_ref, dckb_ref, dcba_ref, dcbb_ref):
        row = lax.broadcasted_iota(jnp.int32, (S, LANES), 0)
        ha, hb = ha_ref[...], hb_ref[...]
        cka, ckb = cka_ref[...], ckb_ref[...]
        a = _conv3(ha, cka, cba_ref[...], row, S)
        b = _conv3(hb, ckb, cbb_ref[...], row, S)
        d = d_ref[...]
        da = d * b * _gelu_grad(a)
        db = d * _gelu(a)
        for dd, hh, ck_, dh_ref, dck_ref, dcb_ref in ((da, ha, cka, dha_ref, dcka_ref, dcba_ref),
                                                      (db, hb, ckb, dhb_ref, dckb_ref, dcbb_ref)):
            dcb_ref[...] = jnp.sum(dd, axis=0, keepdims=True)
            dck_ref[0:1, :] = jnp.sum(dd * _shift_down(hh, row), axis=0, keepdims=True)
            dck_ref[1:2, :] = jnp.sum(dd * hh, axis=0, keepdims=True)
            dck_ref[2:3, :] = jnp.sum(dd * _shift_up(hh, row, S), axis=0, keepdims=True)
            dh = _shift_up(dd, row, S) * ck_[0:1] + dd * ck_[1:2] + _shift_down(dd, row) * ck_[2:3]
            dh_ref[...] = dh.astype(BF16)

    ca = lambda j: (0, j)
    cbi = lambda j: (0, j + nt)
    col = pl.BlockSpec((S, LANES), ca)
    return pl.pallas_call(
        body, name=name,
        out_shape=(jax.ShapeDtypeStruct((S, F), BF16), jax.ShapeDtypeStruct((S, F), BF16),
                   jax.ShapeDtypeStruct((3, F), F32), jax.ShapeDtypeStruct((3, F), F32),
                   jax.ShapeDtypeStruct((1, F), F32), jax.ShapeDtypeStruct((1, F), F32)),
        grid=(nt,),
        in_specs=[col, pl.BlockSpec((S, LANES), cbi), pl.BlockSpec((3, LANES), ca), pl.BlockSpec((3, LANES), cbi),
                  pl.BlockSpec((1, LANES), ca), pl.BlockSpec((1, LANES), cbi), col],
        out_specs=(col, col, pl.BlockSpec((3, LANES), ca), pl.BlockSpec((3, LANES), ca),
                   pl.BlockSpec((1, LANES), ca), pl.BlockSpec((1, LANES), ca)),
        compiler_params=_cp(("parallel",)),
    )(h, h, ck, ck, cb, cb, dact)


def _layer_fwd(x, xb, memb, w, tabs, seg, l):
    n = lambda s: f"L{l}_{s}"
    qg2 = jnp.tile(w["b_q_gain"], 2)[None, :]
    kg2 = jnp.tile(w["b_k_gain"], 2)[None, :]
    proj = _mm(xb, w["w_in"], name=n("proj"))
    aq, ak4, av4, bq, bk4, bv4 = _prep(proj, tabs, qg2, kg2, seg, name=n("prep"))
    oa, lse_a = _attn_win_fwd(aq, ak4, av4, w["a_sink"], name=n("attn_win"))
    ob, lse_b = _attn_dense_fwd(bq, bk4, bv4, name=n("attn_dense"))
    oc = _gmlp_fwd(proj, w["c_ws"], w["c_bs3"], w["c_ln_g"], w["c_ln_b"], name=n("gmlp"))
    merged = _merge_fwd(oa, ob, oc, w["w_branch"], proj, w["b_gate"], name=n("merge"))
    x1, x1b, xh1, rs1 = _mm_res_ln(merged, w["w_mix_out"], x, w["ln1_g"], w["ln1_b"], name=n("mix_ln1"))
    xq = _mm(x1b, w["x_wq"], out_dtype=BF16, name=n("xq"))
    xkv = _mm(memb, w["x_wkv"], out_dtype=BF16, name=n("xkv"))
    xo, lse_x = _xattn_fwd(xq, xkv, name=n("xattn"))
    x2, x2b, xh2, rs2 = _mm_res_ln(xo, w["x_wo"], x1, w["ln2_g"], w["ln2_b"], name=n("xo_ln2"))
    h = _mm(x2b, w["f_w_up"], name=n("ffn_up"))
    act = _conv_act_fwd(h, w["f_conv_k"], w["f_conv_b"], name=n("conv_act"))
    x3, x3b, xh3, rs3 = _mm_res_ln(act, w["f_w_down"], x2, w["ln3_g"], w["ln3_b"], name=n("down_ln3"))
    saved = dict(xb=xb, proj=proj, aq=aq, ak4=ak4, av4=av4, bq=bq, bk4=bk4, bv4=bv4, lse_a=lse_a, lse_b=lse_b,
                 oa=oa, ob=ob, oc=oc, merged=merged, xh1=xh1, rs1=rs1, x1b=x1b, xq=xq, xkv=xkv, xo=xo, lse_x=lse_x,
                 xh2=xh2, rs2=rs2, x2b=x2b, h=h, act=act, xh3=xh3, rs3=rs3, qg2=qg2, kg2=kg2)
    return x3, x3b, saved


def _layer_bwd(dy, memb, w, sv, tabs, seg, l, gbuf):
    n = lambda s: f"L{l}_{s}"
    g = {}
    gbuf = dict(gbuf)

    def dw(key, a, b, tag, slot=l):
        gbuf[key] = _mm(a, b, ta=True, into=(gbuf[key], slot), name=n(tag))

    dz3, dz3b, g["ln3_g"], g["ln3_b"] = _ln_bwd(dy, sv["xh3"], sv["rs3"], w["ln3_g"], name=n("ln3_bwd"))
    dw("f_w_down", sv["act"], dz3b, "dw_down")
    dact = _mm(dz3b, w["f_w_down"], tb=True, name=n("dact"))
    dha, dhb, dcka, dckb, dcba, dcbb = _conv_act_bwd(sv["h"], w["f_conv_k"], w["f_conv_b"], dact, name=n("conv_act_bwd"))
    dh = jnp.concatenate([dha, dhb], axis=1)
    g["f_conv_k"] = jnp.concatenate([dcka, dckb], axis=1)
    g["f_conv_b"] = jnp.concatenate([dcba, dcbb], axis=1)[0]
    dw("f_w_up", sv["x2b"], dh, "dw_up")
    dx2 = _mm(dh, w["f_w_up"], tb=True, res=dz3, res_scale=ALPHA, name=n("dx2"))
    dz2, dz2b, g["ln2_g"], g["ln2_b"] = _ln_bwd(dx2, sv["xh2"], sv["rs2"], w["ln2_g"], name=n("ln2_bwd"))
    dw("x_wo", sv["xo"], dz2b, "dw_xo")
    dxo = _mm(dz2b, w["x_wo"], tb=True, out_dtype=BF16, name=n("dxo"))
    dxq, dxkv = _xattn_bwd(sv["xq"], sv["xkv"], sv["lse_x"], dxo, name=n("xattn_bwd"))
    dw("x_wq", sv["x1b"], dxq, "dw_xq")
    dw("x_wkv", memb, dxkv, "dw_xkv")
    dx1 = _mm(dxq, w["x_wq"], tb=True, res=dz2, res_scale=ALPHA, name=n("dx1"))
    dz1, dz1b, g["ln1_g"], g["ln1_b"] = _ln_bwd(dx1, sv["xh1"], sv["rs1"], w["ln1_g"], name=n("ln1_bwd"))
    dw("w_mix_out", sv["merged"], dz1b, "dw_mix")
    dmerged = _mm(dz1b, w["w_mix_out"], tb=True, name=n("dmerged"))
    dgl, dt, dbr, dbg = _merge_bwd(sv["oa"], sv["ob"], sv["oc"], w["w_branch"], sv["proj"], w["b_gate"], dmerged,
                                   name=n("merge_bwd"))
    g["b_gate"] = dbg[0]
    for i, k in enumerate(("oa", "ob", "oc")):
        dw("w_branch", sv[k], dt[i], f"dw_branch{i}", slot=l * N_BRANCH + i)
    dqa, dka, dva, dsink = _attn_win_bwd(sv["aq"], sv["ak4"], sv["av4"], w["a_sink"], sv["lse_a"], dbr[0],
                                         name=n("attn_win_bwd"))
    g["a_sink"] = dsink[:, 0]
    dqb, dkb, dvb = _attn_dense_bwd(sv["bq"], sv["bk4"], sv["bv4"], sv["lse_b"], dbr[1], name=n("attn_dense_bwd"))
    dcz, g["c_ws"], dbs3, dlg, dlb = _gmlp_bwd(sv["proj"], dbr[2], w["c_ws"], w["c_bs3"], w["c_ln_g"], w["c_ln_b"],
                                               name=n("gmlp_bwd"))
    g["c_bs"] = dbs3[:, :, 0]
    g["c_ln_g"], g["c_ln_b"] = dlg[0], dlb[0]
    dqkv, dqg, dkg = _unprep(dqa, dka, dva, dqb, dkb, dvb, sv["proj"], tabs, sv["qg2"], sv["kg2"], seg, name=n("unprep"))
    g["b_q_gain"] = dqg[0, :HEAD_DIM] + dqg[0, HEAD_DIM:]
    g["b_k_gain"] = dkg[0, :HEAD_DIM] + dkg[0, HEAD_DIM:]
    dproj = jnp.concatenate([dqkv, dcz, dgl], axis=1)
    dw("w_in", sv["xb"], dproj, "dw_in")
    dx0 = _mm(dproj, w["w_in"], tb=True, res=dz1, res_scale=ALPHA, name=n("dx0"))
    for k in ("ln1_g", "ln1_b", "ln2_g", "ln2_b", "ln3_g", "ln3_b"):
        g[k] = g[k][0]
    return dx0, g, gbuf


def _local_step(x, mem, target, wl):
    S = x.shape[0]
    w0 = wl[0]
    gbuf = {k: jnp.zeros((DEPTH,) + w0[k].shape, BF16)
            for k in ("w_in", "w_mix_out", "x_wq", "x_wkv", "x_wo", "f_w_up", "f_w_down")}
    gbuf["w_branch"] = jnp.zeros((DEPTH * N_BRANCH,) + w0["w_branch"].shape[1:], BF16)
    tabs = _rope_tables(S)
    seg = _seg_matrix()
    memb = mem.astype(BF16)
    xb = x.astype(BF16)
    saved = []
    for l in range(DEPTH):
        x, xb, sv = _layer_fwd(x, xb, memb, wl[l], tabs, seg, l)
        saved.append(sv)
    dy, loss = _loss_head(x, target, name="loss_head")
    grads = [None] * DEPTH
    for l in reversed(range(DEPTH)):
        dy, grads[l], gbuf = _layer_bwd(dy, memb, wl[l], saved[l], tabs, seg, l, gbuf)
    return loss, dy, grads, gbuf


MATMUL_WEIGHTS = ("w_in", "w_branch", "w_mix_out", "x_wq", "x_wkv", "x_wo", "f_w_up", "f_w_down", "c_ws")


def _full_weight_dicts(full):
    out = []
    for l in range(DEPTH):
        w = {k: v[l] for k, v in full.items()}
        for k in ("c_ln_g", "c_ln_b", "ln1_g", "ln1_b", "ln2_g", "ln2_b", "ln3_g", "ln3_b", "b_gate", "f_conv_b"):
            w[k] = w[k][None, :]
        w["c_bs3"] = w["c_bs"][:, :, None]
        out.append(w)
    return out


MESH_ID = pl.DeviceIdType.MESH
PACK_W = 1024
_ANY = pl.BlockSpec(memory_space=pl.ANY)


def _all_gather(x, *, name):
    def body(x_ref, out_ref, send_sems, recv_sems, local_sem):
        mx, my, mc = lax.axis_index("x"), lax.axis_index("y"), lax.axis_index("c")
        me, sibling = (mx, my, mc), (mx, my, 1 - mc)
        chips = [(1 - mx, my), (mx, 1 - my), (1 - mx, 1 - my)]

        def slot(px, py, pc):
            return out_ref.at[:, 4 * px + 2 * py + pc]

        def copy(k, block, to, src=None):
            return pltpu.make_async_remote_copy(
                src_ref=slot(*block) if src is None else src, dst_ref=slot(*block),
                send_sem=send_sems.at[k], recv_sem=recv_sems.at[k], device_id=to, device_id_type=MESH_ID)

        mine = pltpu.make_async_copy(x_ref, slot(*me), local_sem)
        mine.start()
        first = [copy(0, me, sibling, src=x_ref)]
        first += [copy(1 + j, me, (*chip, mc), src=x_ref) for j, chip in enumerate(chips)]
        for cp in first:
            cp.start()
        passed = [copy(4 + j, (*chip, mc), sibling) for j, chip in enumerate(chips)]
        for j, chip in enumerate(chips):
            copy(1 + j, (*chip, mc), me).wait_recv()
            passed[j].start()
        copy(0, sibling, me).wait_recv()
        for j, chip in enumerate(chips):
            copy(4 + j, (*chip, 1 - mc), me).wait_recv()
        for cp in first + passed:
            cp.wait_send()
        mine.wait()

    return pl.pallas_call(
        body, name=name,
        out_shape=jax.ShapeDtypeStruct((x.shape[0], N_DEV) + x.shape[1:], x.dtype),
        in_specs=[_ANY], out_specs=_ANY,
        scratch_shapes=[pltpu.SemaphoreType.DMA((7,)), pltpu.SemaphoreType.DMA((7,)), pltpu.SemaphoreType.DMA],
    )(x)


def _all_to_all(send, *, name):
    def body(send_ref, recv_ref, send_sems, recv_sems, local_sem):
        mx, my, mc = lax.axis_index("x"), lax.axis_index("y"), lax.axis_index("c")
        me = 4 * mx + 2 * my + mc
        mine = pltpu.make_async_copy(send_ref.at[:, me], recv_ref.at[:, me], local_sem)
        mine.start()
        copies = []
        for k in range(1, N_DEV):
            px = 1 - mx if k & 4 else mx
            py = 1 - my if k & 2 else my
            pc = 1 - mc if k & 1 else mc
            peer = 4 * px + 2 * py + pc
            out = pltpu.make_async_remote_copy(
                src_ref=send_ref.at[:, peer], dst_ref=recv_ref.at[:, me], send_sem=send_sems.at[k - 1],
                recv_sem=recv_sems.at[k - 1], device_id=(px, py, pc), device_id_type=MESH_ID)
            out.start()
            arrives = pltpu.make_async_remote_copy(
                src_ref=send_ref.at[:, me], dst_ref=recv_ref.at[:, peer], send_sem=send_sems.at[k - 1],
                recv_sem=recv_sems.at[k - 1], device_id=(px, py, pc), device_id_type=MESH_ID)
            copies.append((out, arrives))
        for out, arrives in copies:
            arrives.wait_recv()
        for out, arrives in copies:
            out.wait_send()
        mine.wait()

    return pl.pallas_call(
        body, name=name,
        out_shape=jax.ShapeDtypeStruct(send.shape, send.dtype),
        in_specs=[_ANY], out_specs=_ANY,
        scratch_shapes=[pltpu.SemaphoreType.DMA((7,)), pltpu.SemaphoreType.DMA((7,)), pltpu.SemaphoreType.DMA],
    )(send)


def _sum_parts(parts, *, name):
    P, R, C = parts.shape
    tr = _tile(R, 56, align=8)

    def body(p_ref, o_ref):
        g = p_ref[0].astype(F32)
        for s in range(1, P):
            g = g + p_ref[s].astype(F32)
        o_ref[...] = g

    return pl.pallas_call(
        body, name=name, out_shape=jax.ShapeDtypeStruct((R, C), F32), grid=(R // tr,),
        in_specs=[pl.BlockSpec((P, tr, C), lambda i: (0, i, 0))], out_specs=pl.BlockSpec((tr, C), lambda i: (i, 0)),
        compiler_params=_cp(("parallel",)),
    )(parts)


ADAM_BLOCK_ELEMS = 512 * 1024


def _adamw(parts, w, m, v, *, name):
    L, P, R, C = parts.shape
    assert w.shape == (L, R, C), (parts.shape, w.shape)
    tr = _tile(R, max(16, ADAM_BLOCK_ELEMS // C), align=16)

    def body(p_ref, w_ref, m_ref, v_ref, g_ref, d_ref, nm_ref, nv_ref):
        g = p_ref[0].astype(F32)
        for s in range(1, P):
            g = g + p_ref[s].astype(F32)
        nm = ADAM_B1 * m_ref[...] + (1.0 - ADAM_B1) * g
        nv = ADAM_B2 * v_ref[...] + (1.0 - ADAM_B2) * (g * g)
        m_hat = nm / (1.0 - ADAM_B1 ** ADAM_STEP)
        v_hat = nv / (1.0 - ADAM_B2 ** ADAM_STEP)
        g_ref[...] = g
        d_ref[...] = -ADAM_LR * (m_hat / (jnp.sqrt(v_hat) + ADAM_EPS) + ADAM_WD * w_ref[...])
        nm_ref[...] = nm
        nv_ref[...] = nv

    blk = pl.BlockSpec((None, tr, C), lambda l, i: (l, i, 0))
    shp = jax.ShapeDtypeStruct((L, R, C), F32)
    return pl.pallas_call(
        body, name=name, out_shape=(shp, shp, shp, shp), grid=(L, R // tr),
        in_specs=[pl.BlockSpec((None, P, tr, C), lambda l, i: (l, 0, i, 0)), blk, blk, blk],
        out_specs=(blk, blk, blk, blk),
        compiler_params=_cp(("parallel", "parallel")),
    )(parts, w, m, v)


WEIGHTS = ("w_in", "b_gate", "a_sink", "b_q_gain", "b_k_gain", "c_ln_g", "c_ln_b", "c_ws", "c_bs", "w_branch",
           "w_mix_out", "ln1_g", "ln1_b", "x_wq", "x_wkv", "x_wo", "ln2_g", "ln2_b", "f_w_up", "f_conv_k",
           "f_conv_b", "f_w_down", "ln3_g", "ln3_b")
BIG_AXIS = {"w_in": 2, "w_branch": 3, "w_mix_out": 1, "x_wq": 1, "x_wkv": 1, "x_wo": 2, "f_w_up": 2, "f_w_down": 1}
BIG = tuple(BIG_AXIS)
SMALL = tuple(k for k in WEIGHTS if k not in BIG_AXIS and k != "f_conv_k")


def _unshard(g, axis):
    t = jnp.moveaxis(g, 1, axis)
    return t.reshape(t.shape[:axis] + (t.shape[axis] * t.shape[axis + 1],) + t.shape[axis + 2:])


def _reshard(full, axis):
    t = full.reshape(full.shape[:axis] + (N_DEV, full.shape[axis] // N_DEV) + full.shape[axis + 1:])
    return jnp.moveaxis(t, axis, 1)


def _pad_rows(vec, width, row_align):
    n = vec.shape[0]
    rows = -(-n // width)
    rows = -(-rows // row_align) * row_align
    return jnp.pad(vec, (0, rows * width - n)).reshape(rows, width)


def kernel(x, mem, w_in, b_gate, a_sink, b_q_gain, b_k_gain, c_ln_g, c_ln_b, c_ws, c_bs, w_branch, w_mix_out, ln1_g, ln1_b, x_wq, x_wkv, x_wo, ln2_g, ln2_b, f_w_up, f_conv_k, f_conv_b, f_w_down, ln3_g, ln3_b, loss_target, m_w_in, m_b_gate, m_a_sink, m_b_q_gain, m_b_k_gain, m_c_ln_g, m_c_ln_b, m_c_ws, m_c_bs, m_w_branch, m_w_mix_out, m_ln1_g, m_ln1_b, m_x_wq, m_x_wkv, m_x_wo, m_ln2_g, m_ln2_b, m_f_w_up, m_f_conv_k, m_f_conv_b, m_f_w_down, m_ln3_g, m_ln3_b, v_w_in, v_b_gate, v_a_sink, v_b_q_gain, v_b_k_gain, v_c_ln_g, v_c_ln_b, v_c_ws, v_c_bs, v_w_branch, v_w_mix_out, v_ln1_g, v_ln1_b, v_x_wq, v_x_wkv, v_x_wo, v_ln2_g, v_ln2_b, v_f_w_up, v_f_conv_k, v_f_conv_b, v_f_w_down, v_ln3_g, v_ln3_b):
    w = dict(w_in=w_in, b_gate=b_gate, a_sink=a_sink, b_q_gain=b_q_gain, b_k_gain=b_k_gain, c_ln_g=c_ln_g,
             c_ln_b=c_ln_b, c_ws=c_ws, c_bs=c_bs, w_branch=w_branch, w_mix_out=w_mix_out, ln1_g=ln1_g, ln1_b=ln1_b,
             x_wq=x_wq, x_wkv=x_wkv, x_wo=x_wo, ln2_g=ln2_g, ln2_b=ln2_b, f_w_up=f_w_up, f_conv_k=f_conv_k,
             f_conv_b=f_conv_b, f_w_down=f_w_down, ln3_g=ln3_g, ln3_b=ln3_b)
    m = dict(w_in=m_w_in, b_gate=m_b_gate, a_sink=m_a_sink, b_q_gain=m_b_q_gain, b_k_gain=m_b_k_gain,
             c_ln_g=m_c_ln_g, c_ln_b=m_c_ln_b, c_ws=m_c_ws, c_bs=m_c_bs, w_branch=m_w_branch, w_mix_out=m_w_mix_out,
             ln1_g=m_ln1_g, ln1_b=m_ln1_b, x_wq=m_x_wq, x_wkv=m_x_wkv, x_wo=m_x_wo, ln2_g=m_ln2_g, ln2_b=m_ln2_b,
             f_w_up=m_f_w_up, f_conv_k=m_f_conv_k, f_conv_b=m_f_conv_b, f_w_down=m_f_w_down, ln3_g=m_ln3_g,
             ln3_b=m_ln3_b)
    v = dict(w_in=v_w_in, b_gate=v_b_gate, a_sink=v_a_sink, b_q_gain=v_b_q_gain, b_k_gain=v_b_k_gain,
             c_ln_g=v_c_ln_g, c_ln_b=v_c_ln_b, c_ws=v_c_ws, c_bs=v_c_bs, w_branch=v_w_branch, w_mix_out=v_w_mix_out,
             ln1_g=v_ln1_g, ln1_b=v_ln1_b, x_wq=v_x_wq, x_wkv=v_x_wkv, x_wo=v_x_wo, ln2_g=v_ln2_g, ln2_b=v_ln2_b,
             f_w_up=v_f_w_up, f_conv_k=v_f_conv_k, f_conv_b=v_f_conv_b, f_w_down=v_f_w_down, ln3_g=v_ln3_g,
             ln3_b=v_ln3_b)
    me = 4 * lax.axis_index("x") + 2 * lax.axis_index("y") + lax.axis_index("c")

    full = {k: w[k] for k in SMALL}
    for k in BIG:
        full[k] = _unshard(_all_gather(w[k].astype(BF16), name=f"gather_{k}"), BIG_AXIS[k])
    full["f_conv_k"] = _unshard(_all_gather(w["f_conv_k"], name="gather_f_conv_k"), 2)
    full["c_ws"] = full["c_ws"].astype(BF16)

    loss, grad_x, grads, gbuf = _local_step(x[0], mem[0], loss_target[0], _full_weight_dicts(full))
    loss = lax.psum(loss[0, 0], ("x", "y", "c"))

    out_g, out_d, out_m, out_v = {}, {}, {}, {}
    for k in BIG:
        shp = w[k].shape
        rc = (DEPTH, math.prod(shp[1:-1]), shp[-1])
        send = _reshard(gbuf[k].reshape((DEPTH,) + full[k].shape[1:]), BIG_AXIS[k])
        recv = _all_to_all(send, name=f"scatter_{k}")
        parts = recv.reshape((DEPTH, N_DEV) + rc[1:])
        g_, d_, m_, v_ = _adamw(parts, w[k].reshape(rc), m[k].reshape(rc), v[k].reshape(rc), name=f"adamw_{k}")
        out_g[k], out_d[k], out_m[k], out_v[k] = (t.reshape(shp) for t in (g_, d_, m_, v_))

    small_all = SMALL + ("f_conv_k",)
    gfull = {k: jnp.stack([grads[l][k] for l in range(DEPTH)]) for k in small_all}
    svec = _pad_rows(jnp.concatenate([gfull[k].reshape(-1) for k in small_all]), PACK_W, 8)
    ssum = _sum_parts(_all_gather(svec[None], name="gather_small_grads")[0], name="sum_small_grads").reshape(-1)
    sg, acc = {}, 0
    for k in small_all:
        sg[k] = ssum[acc:acc + gfull[k].size].reshape(gfull[k].shape)
        acc += gfull[k].size
    width = w["f_conv_k"].shape[2]
    sg["f_conv_k"] = lax.dynamic_slice_in_dim(sg["f_conv_k"], me * width, width, axis=2)
    pack = lambda d: _pad_rows(jnp.concatenate([d[k].reshape(-1) for k in small_all]), PACK_W, 8)[None]
    g_, d_, m_, v_ = _adamw(pack(sg)[None], pack(w), pack(m), pack(v), name="adamw_small")
    acc = 0
    for k in small_all:
        n, shp = w[k].size, w[k].shape
        out_g[k] = sg[k]
        out_d[k], out_m[k], out_v[k] = (t.reshape(-1)[acc:acc + n].reshape(shp) for t in (d_, m_, v_))
        acc += n

    return (loss, grad_x[None], *[out_g[k] for k in WEIGHTS], *[out_d[k] for k in WEIGHTS],
            *[out_m[k] for k in WEIGHTS], *[out_v[k] for k in WEIGHTS])
```

```python
import functools
import math

import jax
import jax.numpy as jnp
from jax import lax
from jax.experimental import pallas as pl
from jax.experimental.pallas import tpu as pltpu

F32 = jnp.float32
BF16 = jnp.bfloat16

DEPTH = 4
HEAD_DIM = 64
BLOCK = 128
WINDOW = 128
GRID_W = 64
C_WIDTH = 512
C_GROUPS = 4
CHUNK = 128
N_BRANCH = 3
BRANCH_WIDTH = 512
ROPE_THETA = 10000.0
X_HEADS = 4
X_HEAD_DIM = 128
ALPHA = (2 * DEPTH) ** 0.25
LN_EPS = 1e-5
RMS_EPS = 1e-6
ADAM_LR = 0.001
ADAM_B1 = 0.9
ADAM_B2 = 0.999
ADAM_EPS = 1e-08
ADAM_WD = 0.01
ADAM_STEP = 10
N_DEV = 8

COL_A = 0
COL_B = 768
COL_C = 1536
COL_GATE = 2560
QKV_W = 768

LANES = 128
V7X_VMEM_BYTES = 64 * 1024 * 1024
VMEM_LIMIT = V7X_VMEM_BYTES - 8 * 1024 * 1024
NEG_BIG = -1e30

_NT = (((1,), (1,)), ((), ()))
_TN = (((0,), (0,)), ((), ()))
_NN = (((1,), (0,)), ((), ()))


def _cp(sem=None):
    return pltpu.CompilerParams(dimension_semantics=sem, vmem_limit_bytes=VMEM_LIMIT)


def _tile(n, target, align=LANES):
    if n <= target:
        return n
    best = None
    for t in range(align, target + 1, align):
        if n % t == 0:
            best = t
    assert best is not None, (n, target)
    return best


def _dot(a, b, dims=_NN):
    return lax.dot_general(a, b, dims, preferred_element_type=F32)


def _gelu(x):
    return 0.5 * x * (1.0 + lax.erf(x * 0.7071067811865476))


def _gelu_grad(x):
    return 0.5 * (1.0 + lax.erf(x * 0.7071067811865476)) + x * jnp.exp(-0.5 * x * x) * 0.3989422804014327


def _sigmoid(x):
    return 1.0 / (1.0 + jnp.exp(-x))


MESH_ID = pl.DeviceIdType.MESH
_ANY = pl.BlockSpec(memory_space=pl.ANY)
COPIES_PER_ARRAY = N_DEV - 1


def _comm_scratch(n_arrays):
    return [pltpu.SemaphoreType.DMA((COPIES_PER_ARRAY * n_arrays,)),
            pltpu.SemaphoreType.DMA((COPIES_PER_ARRAY * n_arrays,)), pltpu.SemaphoreType.DMA((n_arrays,))]


def _gathered_shape(x):
    return jax.ShapeDtypeStruct((N_DEV,) + x.shape[1:], x.dtype)


def _gather_plan(entries, send_sems, recv_sems, local_sems):
    mx, my, mc = lax.axis_index("x"), lax.axis_index("y"), lax.axis_index("c")
    me, sibling = (mx, my, mc), (mx, my, 1 - mc)
    chips = [(1 - mx, my), (mx, 1 - my), (1 - mx, 1 - my)]

    def copy(a, k, block, to, from_shard=False):
        x_ref, l, out_ref = entries[a]
        dst = out_ref.at[4 * block[0] + 2 * block[1] + block[2]]
        return pltpu.make_async_remote_copy(
            src_ref=x_ref.at[l] if from_shard else dst, dst_ref=dst,
            send_sem=send_sems.at[COPIES_PER_ARRAY * a + k], recv_sem=recv_sems.at[COPIES_PER_ARRAY * a + k],
            device_id=to, device_id_type=MESH_ID)

    def own(a):
        x_ref, l, out_ref = entries[a]
        return pltpu.make_async_copy(x_ref.at[l], out_ref.at[4 * mx + 2 * my + mc], local_sems.at[a])

    def first(a):
        return [copy(a, 0, me, sibling, True)] + [copy(a, 1 + j, me, (*chip, mc), True) for j, chip in enumerate(chips)]

    def passed(a):
        return [copy(a, 4 + j, (*chip, mc), sibling) for j, chip in enumerate(chips)]

    def start():
        for a in range(len(entries)):
            own(a).start()
            for cp in first(a):
                cp.start()

    def finish():
        for a in range(len(entries)):
            fwd = passed(a)
            for j, chip in enumerate(chips):
                copy(a, 1 + j, (*chip, mc), me).wait_recv()
                fwd[j].start()
        for a in range(len(entries)):
            copy(a, 0, sibling, me).wait_recv()
            for j, chip in enumerate(chips):
                copy(a, 4 + j, (*chip, 1 - mc), me).wait_recv()
            for cp in first(a) + passed(a):
                cp.wait_send()
            own(a).wait()

    return start, finish


def _scatter_plan(entries, send_sems, recv_sems, local_sems):
    mx, my, mc = lax.axis_index("x"), lax.axis_index("y"), lax.axis_index("c")
    me = 4 * mx + 2 * my + mc

    def src(a, dev):
        send_ref, ls, _, _ = entries[a]
        return send_ref.at[dev] if ls is None else send_ref.at[ls, dev]

    def copies(a):
        _, _, recv_ref, lr = entries[a]
        out = []
        for k in range(1, N_DEV):
            px = 1 - mx if k & 4 else mx
            py = 1 - my if k & 2 else my
            pc = 1 - mc if k & 1 else mc
            peer = 4 * px + 2 * py + pc
            sems = dict(send_sem=send_sems.at[COPIES_PER_ARRAY * a + k - 1],
                        recv_sem=recv_sems.at[COPIES_PER_ARRAY * a + k - 1],
                        device_id=(px, py, pc), device_id_type=MESH_ID)
            sends = pltpu.make_async_remote_copy(src_ref=src(a, peer), dst_ref=recv_ref.at[lr, me], **sems)
            lands = pltpu.make_async_remote_copy(src_ref=src(a, me), dst_ref=recv_ref.at[lr, peer], **sems)
            out.append((sends, lands))
        return out

    def own(a):
        _, _, recv_ref, lr = entries[a]
        return pltpu.make_async_copy(src(a, me), recv_ref.at[lr, me], local_sems.at[a])

    def start():
        for a in range(len(entries)):
            own(a).start()
            for sends, _ in copies(a):
                sends.start()

    def finish():
        for a in range(len(entries)):
            for _, lands in copies(a):
                lands.wait_recv()
        for a in range(len(entries)):
            for sends, _ in copies(a):
                sends.wait_send()
            own(a).wait()

    return start, finish


MM_TM, MM_TN, MM_TK = 1024, 1536, 2048


def _mm(a, b, *, ta=False, tb=False, out_dtype=F32, res=None, res_scale=1.0, name):
    if ta:
        K, M = a.shape
    else:
        M, K = a.shape
    if tb:
        N, Kb = b.shape
    else:
        Kb, N = b.shape
    assert K == Kb, (a.shape, b.shape, ta, tb)
    tm, tn, tk = _tile(M, MM_TM), _tile(N, MM_TN), _tile(K, MM_TK)
    nk = K // tk
    dims = (((0 if ta else 1,), (1 if tb else 0,)), ((), ()))
    n_in = 2 + (res is not None)

    def body(*refs):
        a_ref, b_ref = refs[0], refs[1]
        r_ref = refs[2] if res is not None else None
        o_ref = refs[n_in]

        def finish(out):
            if r_ref is not None:
                out = out + res_scale * r_ref[...]
            o_ref[...] = out.astype(o_ref.dtype)

        prod = _dot(a_ref[...].astype(BF16), b_ref[...].astype(BF16), dims)
        if nk == 1:
            finish(prod)
        else:
            acc = refs[n_in + 1]
            k = pl.program_id(2)

            @pl.when(k == 0)
            def _():
                acc[...] = prod

            @pl.when(k > 0)
            def _():
                acc[...] += prod

            @pl.when(k == nk - 1)
            def _():
                finish(acc[...])

    a_spec = pl.BlockSpec((tk, tm), lambda i, j, k: (k, i)) if ta else pl.BlockSpec((tm, tk), lambda i, j, k: (i, k))
    b_spec = pl.BlockSpec((tn, tk), lambda i, j, k: (j, k)) if tb else pl.BlockSpec((tk, tn), lambda i, j, k: (k, j))
    in_specs = [a_spec, b_spec]
    args = [a, b]
    if res is not None:
        in_specs.append(pl.BlockSpec((tm, tn), lambda i, j, k: (i, j)))
        args.append(res)
    return pl.pallas_call(
        body, name=name,
        out_shape=jax.ShapeDtypeStruct((M, N), out_dtype),
        grid=(M // tm, N // tn, nk),
        in_specs=in_specs,
        out_specs=pl.BlockSpec((tm, tn), lambda i, j, k: (i, j)),
        scratch_shapes=[pltpu.VMEM((tm, tn), F32)] if nk > 1 else [],
        compiler_params=_cp(("parallel", "parallel", "arbitrary")),
    )(*args)


def _mm_res_ln(a, w, x, g, b, *, name):
    S, K = a.shape
    D = w.shape[1]
    tm = _tile(S, 256)

    def body(a_ref, w_ref, x_ref, g_ref, b_ref, y_ref, yb_ref, xh_ref, rs_ref):
        h = _dot(a_ref[...], w_ref[...])
        z = ALPHA * x_ref[...] + h
        mu = jnp.mean(z, axis=-1, keepdims=True)
        zc = z - mu
        var = jnp.mean(zc * zc, axis=-1, keepdims=True)
        r = lax.rsqrt(var + LN_EPS)
        xh = zc * r
        y = xh * g_ref[...] + b_ref[...]
        y_ref[...] = y
        yb_ref[...] = y.astype(BF16)
        xh_ref[...] = xh
        rs_ref[...] = r

    row = lambda i: (i, 0)
    full = lambda i: (0, 0)
    return pl.pallas_call(
        body, name=name,
        out_shape=(jax.ShapeDtypeStruct((S, D), F32), jax.ShapeDtypeStruct((S, D), BF16),
                   jax.ShapeDtypeStruct((S, D), F32), jax.ShapeDtypeStruct((S, 1), F32)),
        grid=(S // tm,),
        in_specs=[pl.BlockSpec((tm, K), row), pl.BlockSpec((K, D), full), pl.BlockSpec((tm, D), row),
                  pl.BlockSpec((1, D), full), pl.BlockSpec((1, D), full)],
        out_specs=(pl.BlockSpec((tm, D), row), pl.BlockSpec((tm, D), row), pl.BlockSpec((tm, D), row),
                   pl.BlockSpec((tm, 1), row)),
        compiler_params=_cp(("parallel",)),
    )(a, w, x, g, b)


def _ln_bwd(dy, xh, rs, g, *, name):
    S, D = dy.shape
    tm = _tile(S, 256)

    def body(dy_ref, xh_ref, rs_ref, g_ref, dz_ref, dzb_ref, dg_ref, db_ref):
        @pl.when(pl.program_id(0) == 0)
        def _():
            dg_ref[...] = jnp.zeros_like(dg_ref)
            db_ref[...] = jnp.zeros_like(db_ref)

        dy = dy_ref[...]
        xh = xh_ref[...]
        dxh = dy * g_ref[...]
        m1 = jnp.mean(dxh, axis=-1, keepdims=True)
        m2 = jnp.mean(dxh * xh, axis=-1, keepdims=True)
        dz = rs_ref[...] * (dxh - m1 - xh * m2)
        dz_ref[...] = dz
        dzb_ref[...] = dz.astype(BF16)
        dg_ref[...] += jnp.sum(dy * xh, axis=0, keepdims=True)
        db_ref[...] += jnp.sum(dy, axis=0, keepdims=True)

    row = lambda i: (i, 0)
    full = lambda i: (0, 0)
    return pl.pallas_call(
        body, name=name,
        out_shape=(jax.ShapeDtypeStruct((S, D), F32), jax.ShapeDtypeStruct((S, D), BF16),
                   jax.ShapeDtypeStruct((1, D), F32), jax.ShapeDtypeStruct((1, D), F32)),
        grid=(S // tm,),
        in_specs=[pl.BlockSpec((tm, D), row), pl.BlockSpec((tm, D), row), pl.BlockSpec((tm, 1), row),
                  pl.BlockSpec((1, D), full)],
        out_specs=(pl.BlockSpec((tm, D), row), pl.BlockSpec((tm, D), row), pl.BlockSpec((1, D), full),
                   pl.BlockSpec((1, D), full)),
        compiler_params=_cp(("arbitrary",)),
    )(dy, xh, rs, g)


def _loss_head(y, t, *, name):
    S, D = y.shape
    tm = _tile(S, 512)

    def body(y_ref, t_ref, dy_ref, l_ref):
        @pl.when(pl.program_id(0) == 0)
        def _():
            l_ref[...] = jnp.zeros_like(l_ref)

        e = y_ref[...] - t_ref[...]
        dy_ref[...] = e / D
        l_ref[...] += 0.5 * jnp.sum(jnp.mean(e * e, axis=-1, keepdims=True), axis=0, keepdims=True)

    row = lambda i: (i, 0)
    return pl.pallas_call(
        body, name=name,
        out_shape=(jax.ShapeDtypeStruct((S, D), F32), jax.ShapeDtypeStruct((1, 1), F32)),
        grid=(S // tm,),
        in_specs=[pl.BlockSpec((tm, D), row), pl.BlockSpec((tm, D), row)],
        out_specs=(pl.BlockSpec((tm, D), row), pl.BlockSpec((1, 1), lambda i: (0, 0))),
        compiler_params=_cp(("arbitrary",)),
    )(y, t)


def _rope_tables(S):
    pos = jnp.arange(S, dtype=jnp.int32)
    row = pos // GRID_W
    col = pos % GRID_W

    def cs(p, d):
        half = d // 2
        inv = ROPE_THETA ** (-jnp.arange(half, dtype=F32) * (2.0 / d))
        ang = p.astype(F32)[:, None] * inv[None, :]
        c, s = jnp.cos(ang), jnp.sin(ang)
        return jnp.concatenate([c, c], -1), jnp.concatenate([-s, s], -1)

    ca, sa = cs(pos, HEAD_DIM)
    cr, sr = cs(row, HEAD_DIM // 2)
    cc, sc = cs(col, HEAD_DIM // 2)
    cb, sb = jnp.concatenate([cr, cc], -1), jnp.concatenate([sr, sc], -1)
    two = lambda t: jnp.concatenate([t, t], -1)
    return two(ca), two(sa), two(cb), two(sb)


def _partner(x, lane, width):
    h = width // 2
    return jnp.where(lane % width < h, pltpu.roll(x, LANES - h, 1), pltpu.roll(x, h, 1))


def _rope_fwd(x, c, s, lane, width):
    return x * c + _partner(x, lane, width) * s


def _rope_bwd(dy, c, s, lane, width):
    return dy * c + _partner(dy * s, lane, width)


def _head_sum(x, seg):
    return lax.dot_general(x, seg, _NN, precision=lax.Precision.HIGHEST, preferred_element_type=F32)


def _split_heads(x, lane):
    lo = lane < HEAD_DIM
    r = pltpu.roll(x, HEAD_DIM, 1)
    z = jnp.zeros_like(x)
    return jnp.where(lo, x, z), jnp.where(lo, z, r), jnp.where(lo, r, z), jnp.where(lo, z, x)


def _fold_heads(d0, d1, lane):
    t0 = d0 + pltpu.roll(d0, HEAD_DIM, 1)
    t1 = d1 + pltpu.roll(d1, HEAD_DIM, 1)
    return jnp.where(lane < HEAD_DIM, t0, t1)


def _seg_matrix():
    i = jnp.arange(LANES)
    return (i[:, None] // HEAD_DIM == i[None, :] // HEAD_DIM).astype(F32)


def _prep(proj, tabs, qg2, kg2, seg, *, name):
    S = proj.shape[0]
    ts = _tile(S, 256)
    ca, sa, cb, sb = tabs

    def body(pa_ref, pb_ref, ca_ref, sa_ref, cb_ref, sb_ref, qg_ref, kg_ref, seg_ref,
             aq_ref, ak_ref, av_ref, bq_ref, bk_ref, bv_ref):
        lane = lax.broadcasted_iota(jnp.int32, (ts, LANES), 1)
        ca, sa, cb, sb = ca_ref[...], sa_ref[...], cb_ref[...], sb_ref[...]
        seg = seg_ref[...]

        def norm(x, gain):
            r = lax.rsqrt(_head_sum(x * x, seg) * (1.0 / HEAD_DIM) + RMS_EPS)
            return x * r * gain

        def put(ref, x):
            for i, part in enumerate(_split_heads(x, lane)):
                ref[i] = part.astype(BF16)

        for gidx in range(4):
            cols = slice(gidx * LANES, (gidx + 1) * LANES)
            aq_ref[:, cols] = (_rope_fwd(pa_ref[:, cols], ca, sa, lane, HEAD_DIM) * 0.125).astype(BF16)
            bq = norm(pb_ref[:, cols], qg_ref[...])
            bq_ref[:, cols] = (_rope_fwd(bq, cb, sb, lane, HEAD_DIM // 2) * 0.125).astype(BF16)
        put(ak_ref, _rope_fwd(pa_ref[:, 512:640], ca, sa, lane, HEAD_DIM))
        put(av_ref, pa_ref[:, 640:768])
        bk = norm(pb_ref[:, 512:640], kg_ref[...])
        put(bk_ref, _rope_fwd(bk, cb, sb, lane, HEAD_DIM // 2))
        put(bv_ref, pb_ref[:, 640:768])

    row = lambda i: (i, 0)
    full = lambda i: (0, 0)
    tab = pl.BlockSpec((ts, LANES), row)
    kv_shape = jax.ShapeDtypeStruct((4, S, LANES), BF16)
    kv_spec = pl.BlockSpec((4, ts, LANES), lambda i: (0, i, 0))
    q_shape = jax.ShapeDtypeStruct((S, 512), BF16)
    q_spec = pl.BlockSpec((ts, 512), row)
    return pl.pallas_call(
        body, name=name,
        out_shape=(q_shape, kv_shape, kv_shape, q_shape, kv_shape, kv_shape),
        grid=(S // ts,),
        in_specs=[pl.BlockSpec((ts, QKV_W), lambda i: (i, 0)), pl.BlockSpec((ts, QKV_W), lambda i: (i, 1)),
                  tab, tab, tab, tab, pl.BlockSpec((1, LANES), full), pl.BlockSpec((1, LANES), full),
                  pl.BlockSpec((LANES, LANES), full)],
        out_specs=(q_spec, kv_spec, kv_spec, q_spec, kv_spec, kv_spec),
        compiler_params=_cp(("parallel",)),
    )(proj, proj, ca, sa, cb, sb, qg2, kg2, seg)


def _unprep(dqa, dka, dva, dqb, dkb, dvb, proj, tabs, qg2, kg2, seg, *, name):
    S = proj.shape[0]
    ts = _tile(S, 256)
    ca, sa, cb, sb = tabs

    def body(dqa_ref, dka_ref, dva_ref, dqb_ref, dkb_ref, dvb_ref, pb_ref, ca_ref, sa_ref, cb_ref, sb_ref,
             qg_ref, kg_ref, seg_ref, dp_ref, dqg_ref, dkg_ref):
        @pl.when(pl.program_id(0) == 0)
        def _():
            dqg_ref[...] = jnp.zeros_like(dqg_ref)
            dkg_ref[...] = jnp.zeros_like(dkg_ref)

        lane = lax.broadcasted_iota(jnp.int32, (ts, LANES), 1)
        ca, sa, cb, sb = ca_ref[...], sa_ref[...], cb_ref[...], sb_ref[...]
        seg = seg_ref[...]

        def norm_bwd(dy, x, gain):
            r = lax.rsqrt(_head_sum(x * x, seg) * (1.0 / HEAD_DIM) + RMS_EPS)
            gdy = gain * dy
            dot = _head_sum(gdy * x, seg) * (1.0 / HEAD_DIM)
            dx = r * gdy - x * (r * r * r) * dot
            return dx, jnp.sum(dy * x * r, axis=0, keepdims=True)

        for gidx in range(4):
            cols = slice(gidx * LANES, (gidx + 1) * LANES)
            dp_ref[:, cols] = _rope_bwd(dqa_ref[:, cols] * 0.125, ca, sa, lane, HEAD_DIM).astype(BF16)
            dbq = _rope_bwd(dqb_ref[:, cols] * 0.125, cb, sb, lane, HEAD_DIM // 2)
            dx, dg = norm_bwd(dbq, pb_ref[:, cols], qg_ref[...])
            dp_ref[:, COL_B + gidx * LANES:COL_B + (gidx + 1) * LANES] = dx.astype(BF16)
            dqg_ref[...] += dg
        dak = _fold_heads(dka_ref[0] + dka_ref[1], dka_ref[2] + dka_ref[3], lane)
        dp_ref[:, 512:640] = _rope_bwd(dak, ca, sa, lane, HEAD_DIM).astype(BF16)
        dp_ref[:, 640:768] = _fold_heads(dva_ref[0] + dva_ref[1], dva_ref[2] + dva_ref[3], lane).astype(BF16)
        dbk = _fold_heads(dkb_ref[0] + dkb_ref[1], dkb_ref[2] + dkb_ref[3], lane)
        dbk = _rope_bwd(dbk, cb, sb, lane, HEAD_DIM // 2)
        dx, dg = norm_bwd(dbk, pb_ref[:, 512:640], kg_ref[...])
        dp_ref[:, COL_B + 512:COL_B + 640] = dx.astype(BF16)
        dkg_ref[...] += dg
        dp_ref[:, COL_B + 640:COL_B + 768] = _fold_heads(dvb_ref[0] + dvb_ref[1], dvb_ref[2] + dvb_ref[3],
                                                         lane).astype(BF16)

    row = lambda i: (i, 0)
    full = lambda i: (0, 0)
    tab = pl.BlockSpec((ts, LANES), row)
    q_spec = pl.BlockSpec((ts, 512), row)
    kv_spec = pl.BlockSpec((4, ts, LANES), lambda i: (0, i, 0))
    return pl.pallas_call(
        body, name=name,
        out_shape=(jax.ShapeDtypeStruct((S, 2 * QKV_W), BF16), jax.ShapeDtypeStruct((1, LANES), F32),
                   jax.ShapeDtypeStruct((1, LANES), F32)),
        grid=(S // ts,),
        in_specs=[q_spec, kv_spec, kv_spec, q_spec, kv_spec, kv_spec,
                  pl.BlockSpec((ts, QKV_W), lambda i: (i, 1)), tab, tab, tab, tab,
                  pl.BlockSpec((1, LANES), full), pl.BlockSpec((1, LANES), full), pl.BlockSpec((LANES, LANES), full)],
        out_specs=(pl.BlockSpec((ts, 2 * QKV_W), row), pl.BlockSpec((1, LANES), full),
                   pl.BlockSpec((1, LANES), full)),
        compiler_params=_cp(("arbitrary",)),
    )(dqa, dka, dva, dqb, dkb, dvb, proj, ca, sa, cb, sb, qg2, kg2, seg)


def _attn_dense_fwd(q, k4, v4, *, gather=None, name):
    S = q.shape[0]
    tq = _tile(S, 256)
    xs, gl = gather if gather is not None else ([], None)
    na = len(xs)

    def body(q_ref, k_ref, v_ref, *rest):
        o_ref, lse_ref = rest[na], rest[na + 1]
        if na:
            x_refs, out_refs, sems = rest[:na], rest[na + 2:2 * na + 2], rest[2 * na + 2:]
            start, finish = _gather_plan([(x_refs[a], gl, out_refs[a]) for a in range(na)], *sems)
            pl.when((pl.program_id(0) == 0) & (pl.program_id(1) == 0))(start)
        for pr in range(2):
            qp = q_ref[:, pr * LANES:(pr + 1) * LANES]
            acc = None
            for half in range(2):
                s = _dot(qp, k_ref[half], _NT)
                m = jnp.max(s, axis=-1, keepdims=True)
                e = jnp.exp(s - m)
                l = jnp.sum(e, axis=-1, keepdims=True)
                pv = _dot(e.astype(BF16), v_ref[half]) * (1.0 / l)
                acc = pv if acc is None else acc + pv
                lse_ref[pr * 2 + half] = m + jnp.log(l)
            o_ref[:, pr * LANES:(pr + 1) * LANES] = acc.astype(BF16)
        if na:
            pl.when((pl.program_id(0) == 1) & (pl.program_id(1) == S // tq - 1))(finish)

    kv_spec = pl.BlockSpec((2, S, LANES), lambda kv, i: (kv, 0, 0))
    res = pl.pallas_call(
        body, name=name,
        out_shape=(jax.ShapeDtypeStruct((S, 512), BF16), jax.ShapeDtypeStruct((8, S, 1), F32),
                   *[_gathered_shape(x) for x in xs]),
        grid=(2, S // tq),
        in_specs=[pl.BlockSpec((tq, 256), lambda kv, i: (i, kv)), kv_spec, kv_spec] + [_ANY] * na,
        out_specs=(pl.BlockSpec((tq, 256), lambda kv, i: (i, kv)),
                   pl.BlockSpec((4, tq, 1), lambda kv, i: (kv, i, 0)), *([_ANY] * na)),
        scratch_shapes=_comm_scratch(na) if na else [],
        compiler_params=_cp(("arbitrary", "arbitrary") if na else ("parallel", "parallel")),
    )(q, k4, v4, *xs)
    return res[0], res[1], list(res[2:])


def _attn_dense_bwd(q, k4, v4, lse, do, *, scatter=None, name):
    S = q.shape[0]
    tq = _tile(S, 256)
    sends, recvs, lr = scatter if scatter is not None else ([], [], None)
    na = len(sends)
    comm_in, comm_out, held = _scatter_io(sends, recvs)
    n_in = len(comm_in)

    def body(q_ref, k_ref, v_ref, lse_ref, do_ref, *rest):
        dq_ref, dk_ref, dv_ref = rest[n_in:n_in + 3]
        if na:
            s_refs, r_refs, sems = rest[:na], rest[n_in + 3:n_in + 3 + na], rest[n_in + 3 + na:]
            start, finish = _scatter_plan([(s_refs[a], sends[a][1], r_refs[a], lr) for a in range(na)], *sems)
            pl.when((pl.program_id(0) == 0) & (pl.program_id(1) == 0))(start)

        @pl.when(pl.program_id(1) == 0)
        def _():
            dk_ref[...] = jnp.zeros_like(dk_ref)
            dv_ref[...] = jnp.zeros_like(dv_ref)

        lane = lax.broadcasted_iota(jnp.int32, (tq, LANES), 1)
        for pr in range(2):
            qp = q_ref[:, pr * LANES:(pr + 1) * LANES]
            dop = do_ref[:, pr * LANES:(pr + 1) * LANES].astype(BF16)
            dq = None
            for half in range(2):
                mine = (lane < HEAD_DIM) if half == 0 else (lane >= HEAD_DIM)
                s = _dot(qp, k_ref[half], _NT)
                p = jnp.exp(s - lse_ref[pr * 2 + half])
                dp = _dot(dop, v_ref[half], _NT)
                delta = jnp.sum(p * dp, axis=-1, keepdims=True)
                ds = (p * (dp - delta)).astype(BF16)
                pb = p.astype(BF16)
                d = _dot(ds, k_ref[half])
                dq = d if dq is None else dq + d
                dk_ref[half] += _dot(ds, jnp.where(mine, qp, jnp.zeros_like(qp)), _TN)
                dv_ref[half] += _dot(pb, jnp.where(mine, dop, jnp.zeros_like(dop)), _TN)
            dq_ref[:, pr * LANES:(pr + 1) * LANES] = dq
        if na:
            pl.when((pl.program_id(0) == 1) & (pl.program_id(1) == S // tq - 1))(finish)

    kv_spec = pl.BlockSpec((2, S, LANES), lambda kv, i: (kv, 0, 0))
    q_spec = pl.BlockSpec((tq, 256), lambda kv, i: (i, kv))
    res = pl.pallas_call(
        body, name=name,
        out_shape=(jax.ShapeDtypeStruct((S, 512), F32), jax.ShapeDtypeStruct((4, S, LANES), F32),
                   jax.ShapeDtypeStruct((4, S, LANES), F32), *comm_out),
        grid=(2, S // tq),
        in_specs=[q_spec, kv_spec, kv_spec, pl.BlockSpec((4, tq, 1), lambda kv, i: (kv, i, 0)), q_spec]
                 + [_ANY] * n_in,
        out_specs=(q_spec, kv_spec, kv_spec, *([_ANY] * na)),
        scratch_shapes=_comm_scratch(na) if na else [],
        input_output_aliases={5 + na + i: 3 + a for i, a in enumerate(held)},
        compiler_params=_cp(("arbitrary", "arbitrary") if na else ("parallel", "arbitrary")),
    )(q, k4, v4, lse, do, *comm_in)
    return res[0], res[1], res[2], list(res[3:])


WIN_KEYS = 3 * BLOCK


def _win_start(n, S):
    return pl.multiple_of(jnp.clip((n - 1) * BLOCK, 0, S - WIN_KEYS), BLOCK)


def _win_valid(n, start):
    qpos = n * BLOCK + lax.broadcasted_iota(jnp.int32, (BLOCK, WIN_KEYS), 0)
    kpos = start + lax.broadcasted_iota(jnp.int32, (BLOCK, WIN_KEYS), 1)
    return jnp.abs(qpos - kpos) <= WINDOW


def _attn_win_fwd(q, k4, v4, sink, *, name):
    S = q.shape[0]
    assert S >= WIN_KEYS

    def body(sink_ref, q_ref, k_ref, v_ref, o_ref, lse_ref):
        n = pl.program_id(0)
        start = _win_start(n, S)
        valid = _win_valid(n, start)
        for kv in range(2):
            for pr in range(2):
                cols = slice((kv * 2 + pr) * LANES, (kv * 2 + pr + 1) * LANES)
                qp = q_ref[:, cols]
                acc = None
                for half in range(2):
                    h = kv * 4 + pr * 2 + half
                    kk = k_ref[kv * 2 + half, pl.ds(start, WIN_KEYS), :]
                    vv = v_ref[kv * 2 + half, pl.ds(start, WIN_KEYS), :]
                    s = jnp.where(valid, _dot(qp, kk, _NT), NEG_BIG)
                    snk = sink_ref[h]
                    m = jnp.maximum(jnp.max(s, axis=-1, keepdims=True), snk)
                    e = jnp.exp(s - m)
                    l = jnp.sum(e, axis=-1, keepdims=True) + jnp.exp(snk - m)
                    p = (e * (1.0 / l)).astype(BF16)
                    pv = _dot(p, vv)
                    acc = pv if acc is None else acc + pv
                    lse_ref[h] = m + jnp.log(l)
                o_ref[:, cols] = acc.astype(BF16)

    kv_spec = pl.BlockSpec((4, S, LANES), lambda n: (0, 0, 0))
    return pl.pallas_call(
        body, name=name,
        out_shape=(jax.ShapeDtypeStruct((S, 512), BF16), jax.ShapeDtypeStruct((8, S, 1), F32)),
        grid=(S // BLOCK,),
        in_specs=[pl.BlockSpec(memory_space=pltpu.SMEM), pl.BlockSpec((BLOCK, 512), lambda n: (n, 0)),
                  kv_spec, kv_spec],
        out_specs=(pl.BlockSpec((BLOCK, 512), lambda n: (n, 0)), pl.BlockSpec((8, BLOCK, 1), lambda n: (0, n, 0))),
        compiler_params=_cp(("parallel",)),
    )(sink, q, k4, v4)


def _attn_win_bwd(q, k4, v4, sink, lse, do, *, name):
    S = q.shape[0]

    def body(sink_ref, q_ref, k_ref, v_ref, lse_ref, do_ref, dq_ref, dk_ref, dv_ref, dsink_ref):
        n = pl.program_id(0)

        @pl.when(n == 0)
        def _():
            dk_ref[...] = jnp.zeros_like(dk_ref)
            dv_ref[...] = jnp.zeros_like(dv_ref)
            dsink_ref[...] = jnp.zeros_like(dsink_ref)

        start = _win_start(n, S)
        valid = _win_valid(n, start)
        lane = lax.broadcasted_iota(jnp.int32, (BLOCK, LANES), 1)
        for kv in range(2):
            for pr in range(2):
                cols = slice((kv * 2 + pr) * LANES, (kv * 2 + pr + 1) * LANES)
                qp = q_ref[:, cols]
                dop = do_ref[:, cols].astype(BF16)
                dq = None
                for half in range(2):
                    h = kv * 4 + pr * 2 + half
                    slot = kv * 2 + half
                    mine = (lane < HEAD_DIM) if half == 0 else (lane >= HEAD_DIM)
                    win = pl.ds(start, WIN_KEYS)
                    kk = k_ref[slot, win, :]
                    vv = v_ref[slot, win, :]
                    lse_h = lse_ref[h]
                    s = jnp.where(valid, _dot(qp, kk, _NT), NEG_BIG)
                    p = jnp.exp(s - lse_h)
                    dp = _dot(dop, vv, _NT)
                    delta = jnp.sum(p * dp, axis=-1, keepdims=True)
                    ds = (p * (dp - delta)).astype(BF16)
                    pb = p.astype(BF16)
                    d = _dot(ds, kk)
                    dq = d if dq is None else dq + d
                    dk_ref[slot, win, :] += _dot(ds, jnp.where(mine, qp, jnp.zeros_like(qp)), _TN)
                    dv_ref[slot, win, :] += _dot(pb, jnp.where(mine, dop, jnp.zeros_like(dop)), _TN)
                    p_sink = jnp.exp(sink_ref[h] - lse_h)
                    dsink_ref[h:h + 1, :] += jnp.broadcast_to(-jnp.sum(p_sink * delta, axis=0, keepdims=True),
                                                              (1, LANES))
                dq_ref[:, cols] = dq

    kv_spec = pl.BlockSpec((4, S, LANES), lambda n: (0, 0, 0))
    q_spec = pl.BlockSpec((BLOCK, 512), lambda n: (n, 0))
    return pl.pallas_call(
        body, name=name,
        out_shape=(jax.ShapeDtypeStruct((S, 512), F32), jax.ShapeDtypeStruct((4, S, LANES), F32),
                   jax.ShapeDtypeStruct((4, S, LANES), F32), jax.ShapeDtypeStruct((8, LANES), F32)),
        grid=(S // BLOCK,),
        in_specs=[pl.BlockSpec(memory_space=pltpu.SMEM), q_spec, kv_spec, kv_spec,
                  pl.BlockSpec((8, BLOCK, 1), lambda n: (0, n, 0)), q_spec],
        out_specs=(q_spec, kv_spec, kv_spec, pl.BlockSpec((8, LANES), lambda n: (0, 0))),
        compiler_params=_cp(("arbitrary",)),
    )(sink, q, k4, v4, lse, do)


def _c_ln(v, g, b):
    mu = jnp.mean(v, axis=-1, keepdims=True)
    vc = v - mu
    r = lax.rsqrt(jnp.mean(vc * vc, axis=-1, keepdims=True) + LN_EPS)
    vh = vc * r
    return vh, r, vh * g + b


def _gmlp_fwd(proj, ws, bs3, lg, lb, *, name):
    S = proj.shape[0]

    def body(u_ref, v_ref, ws_ref, bs_ref, lg_ref, lb_ref, o_ref):
        u = _gelu(u_ref[...])
        _, _, vn = _c_ln(_gelu(v_ref[...]), lg_ref[...], lb_ref[...])
        vn = vn.astype(BF16)
        for gi in range(C_GROUPS):
            cols = slice(gi * LANES, (gi + 1) * LANES)
            mixed = _dot(ws_ref[gi], vn[:, cols]) + bs_ref[gi]
            o_ref[:, cols] = (u[:, cols] * mixed).astype(BF16)

    full2 = lambda n: (0, 0)
    full3 = lambda n: (0, 0, 0)
    return pl.pallas_call(
        body, name=name,
        out_shape=jax.ShapeDtypeStruct((S, C_WIDTH), BF16),
        grid=(S // CHUNK,),
        in_specs=[pl.BlockSpec((CHUNK, C_WIDTH), lambda n: (n, COL_C // C_WIDTH)),
                  pl.BlockSpec((CHUNK, C_WIDTH), lambda n: (n, COL_C // C_WIDTH + 1)),
                  pl.BlockSpec((C_GROUPS, CHUNK, CHUNK), full3), pl.BlockSpec((C_GROUPS, CHUNK, 1), full3),
                  pl.BlockSpec((1, C_WIDTH), full2), pl.BlockSpec((1, C_WIDTH), full2)],
        out_specs=pl.BlockSpec((CHUNK, C_WIDTH), lambda n: (n, 0)),
        compiler_params=_cp(("parallel",)),
    )(proj, proj, ws, bs3, lg, lb)


def _gmlp_bwd(proj, dout, ws, bs3, lg, lb, *, name):
    S = proj.shape[0]

    def body(u_ref, v_ref, d_ref, ws_ref, bs_ref, lg_ref, lb_ref, dz_ref, dws_ref, dbs_ref, dlg_ref, dlb_ref):
        @pl.when(pl.program_id(0) == 0)
        def _():
            dws_ref[...] = jnp.zeros_like(dws_ref)
            dbs_ref[...] = jnp.zeros_like(dbs_ref)
            dlg_ref[...] = jnp.zeros_like(dlg_ref)
            dlb_ref[...] = jnp.zeros_like(dlb_ref)

        u_pre, v_pre, d = u_ref[...], v_ref[...], d_ref[...]
        u = _gelu(u_pre)
        vh, r, vn = _c_ln(_gelu(v_pre), lg_ref[...], lb_ref[...])
        vnb = vn.astype(BF16)
        du_parts, dvn_parts = [], []
        for gi in range(C_GROUPS):
            cols = slice(gi * LANES, (gi + 1) * LANES)
            mixed = _dot(ws_ref[gi], vnb[:, cols]) + bs_ref[gi]
            du_parts.append(d[:, cols] * mixed)
            dm = d[:, cols] * u[:, cols]
            dbs_ref[gi] += jnp.sum(dm, axis=-1, keepdims=True)
            dmb = dm.astype(BF16)
            dws_ref[gi] += _dot(dmb, vnb[:, cols], _NT)
            dvn_parts.append(_dot(ws_ref[gi], dmb, _TN))
        du = jnp.concatenate(du_parts, axis=-1)
        dvn = jnp.concatenate(dvn_parts, axis=-1)
        dlg_ref[...] += jnp.sum(dvn * vh, axis=0, keepdims=True)
        dlb_ref[...] += jnp.sum(dvn, axis=0, keepdims=True)
        dvh = dvn * lg_ref[...]
        m1 = jnp.mean(dvh, axis=-1, keepdims=True)
        m2 = jnp.mean(dvh * vh, axis=-1, keepdims=True)
        dv = r * (dvh - m1 - vh * m2)
        dz_ref[:, :C_WIDTH] = (du * _gelu_grad(u_pre)).astype(BF16)
        dz_ref[:, C_WIDTH:] = (dv * _gelu_grad(v_pre)).astype(BF16)

    full2 = lambda n: (0, 0)
    full3 = lambda n: (0, 0, 0)
    return pl.pallas_call(
        body, name=name,
        out_shape=(jax.ShapeDtypeStruct((S, 2 * C_WIDTH), BF16), jax.ShapeDtypeStruct((C_GROUPS, CHUNK, CHUNK), F32),
                   jax.ShapeDtypeStruct((C_GROUPS, CHUNK, 1), F32), jax.ShapeDtypeStruct((1, C_WIDTH), F32),
                   jax.ShapeDtypeStruct((1, C_WIDTH), F32)),
        grid=(S // CHUNK,),
        in_specs=[pl.BlockSpec((CHUNK, C_WIDTH), lambda n: (n, COL_C // C_WIDTH)),
                  pl.BlockSpec((CHUNK, C_WIDTH), lambda n: (n, COL_C // C_WIDTH + 1)),
                  pl.BlockSpec((CHUNK, C_WIDTH), lambda n: (n, 0)),
                  pl.BlockSpec((C_GROUPS, CHUNK, CHUNK), full3), pl.BlockSpec((C_GROUPS, CHUNK, 1), full3),
                  pl.BlockSpec((1, C_WIDTH), full2), pl.BlockSpec((1, C_WIDTH), full2)],
        out_specs=(pl.BlockSpec((CHUNK, 2 * C_WIDTH), lambda n: (n, 0)), pl.BlockSpec((C_GROUPS, CHUNK, CHUNK), full3),
                   pl.BlockSpec((C_GROUPS, CHUNK, 1), full3), pl.BlockSpec((1, C_WIDTH), full2),
                   pl.BlockSpec((1, C_WIDTH), full2)),
        compiler_params=_cp(("arbitrary",)),
    )(proj, proj, dout, ws, bs3, lg, lb)


GATE_BLK = 512


def _gate_specs(tm, D):
    nh = D // GATE_BLK
    first = COL_GATE // GATE_BLK
    return [pl.BlockSpec((tm, GATE_BLK), functools.partial(lambda i, c: (i, c), c=first + b))
            for b in range(N_BRANCH * nh)]


def _merge_fwd(oa, ob, oc, wb, proj, bg, *, name):
    S = oa.shape[0]
    D = wb.shape[2]
    assert D % GATE_BLK == 0
    nh = D // GATE_BLK
    tm = _tile(S, 256)

    def body(oa_ref, ob_ref, oc_ref, wb_ref, *rest):
        gate_refs, bg_ref, o_ref = rest[:N_BRANCH * nh], rest[N_BRANCH * nh], rest[N_BRANCH * nh + 1]
        brs = (oa_ref[...], ob_ref[...], oc_ref[...])
        for j in range(nh):
            cols = slice(j * GATE_BLK, (j + 1) * GATE_BLK)
            acc = None
            for n in range(N_BRANCH):
                b = n * nh + j
                t = _dot(brs[n], wb_ref[n, :, cols])
                g = _sigmoid(gate_refs[b][...] + bg_ref[:, b * GATE_BLK:(b + 1) * GATE_BLK])
                acc = t * g if acc is None else acc + t * g
            o_ref[:, cols] = acc.astype(BF16)

    row = lambda i: (i, 0)
    br = pl.BlockSpec((tm, BRANCH_WIDTH), row)
    return pl.pallas_call(
        body, name=name,
        out_shape=jax.ShapeDtypeStruct((S, D), BF16),
        grid=(S // tm,),
        in_specs=[br, br, br, pl.BlockSpec((N_BRANCH, BRANCH_WIDTH, D), lambda i: (0, 0, 0))]
                 + _gate_specs(tm, D) + [pl.BlockSpec((1, N_BRANCH * D), lambda i: (0, 0))],
        out_specs=pl.BlockSpec((tm, D), row),
        compiler_params=_cp(("parallel",)),
    )(oa, ob, oc, wb, *([proj] * (N_BRANCH * nh)), bg)


def _merge_bwd(oa, ob, oc, wb, proj, bg, dmerged, *, name):
    S = oa.shape[0]
    D = wb.shape[2]
    nh = D // GATE_BLK
    tm = _tile(S, 256)

    def body(oa_ref, ob_ref, oc_ref, wb_ref, *rest):
        gate_refs = rest[:N_BRANCH * nh]
        bg_ref, dm_ref, dgl_ref, dt_ref, dbr_ref, dbg_ref = rest[N_BRANCH * nh:]

        @pl.when(pl.program_id(0) == 0)
        def _():
            dbg_ref[...] = jnp.zeros_like(dbg_ref)

        brs = (oa_ref[...], ob_ref[...], oc_ref[...])
        for n in range(N_BRANCH):
            dbr = None
            for j in range(nh):
                cols = slice(j * GATE_BLK, (j + 1) * GATE_BLK)
                b = n * nh + j
                gcols = slice(b * GATE_BLK, (b + 1) * GATE_BLK)
                w = wb_ref[n, :, cols]
                t = _dot(brs[n], w)
                g = _sigmoid(gate_refs[b][...] + bg_ref[:, gcols])
                dm = dm_ref[:, cols]
                dt = (dm * g).astype(BF16)
                dgl = dm * t * g * (1.0 - g)
                dt_ref[n, :, cols] = dt
                dgl_ref[:, gcols] = dgl.astype(BF16)
                dbg_ref[:, gcols] += jnp.sum(dgl, axis=0, keepdims=True)
                d = _dot(dt, w, _NT)
                dbr = d if dbr is None else dbr + d
            dbr_ref[n] = dbr

    row = lambda i: (i, 0)
    br = pl.BlockSpec((tm, BRANCH_WIDTH), row)
    return pl.pallas_call(
        body, name=name,
        out_shape=(jax.ShapeDtypeStruct((S, N_BRANCH * D), BF16), jax.ShapeDtypeStruct((N_BRANCH, S, D), BF16),
                   jax.ShapeDtypeStruct((N_BRANCH, S, BRANCH_WIDTH), F32), jax.ShapeDtypeStruct((1, N_BRANCH * D), F32)),
        grid=(S // tm,),
        in_specs=[br, br, br, pl.BlockSpec((N_BRANCH, BRANCH_WIDTH, D), lambda i: (0, 0, 0))]
                 + _gate_specs(tm, D)
                 + [pl.BlockSpec((1, N_BRANCH * D), lambda i: (0, 0)), pl.BlockSpec((tm, D), row)],
        out_specs=(pl.BlockSpec((tm, N_BRANCH * D), row), pl.BlockSpec((N_BRANCH, tm, D), lambda i: (0, i, 0)),
                   pl.BlockSpec((N_BRANCH, tm, BRANCH_WIDTH), lambda i: (0, i, 0)),
                   pl.BlockSpec((1, N_BRANCH * D), lambda i: (0, 0))),
        compiler_params=_cp(("arbitrary",)),
    )(oa, ob, oc, wb, *([proj] * (N_BRANCH * nh)), bg, dmerged)


X_SCALE = 1.0 / math.sqrt(X_HEAD_DIM)
X_W = X_HEADS * X_HEAD_DIM


def _xattn_fwd(q, kv, *, name):
    S = q.shape[0]
    M = kv.shape[0]
    tq = _tile(S, 512)

    def body(q_ref, kv_ref, o_ref, lse_ref):
        for h in range(X_HEADS):
            cols = slice(h * LANES, (h + 1) * LANES)
            s = _dot(q_ref[:, cols], kv_ref[:, cols], _NT) * X_SCALE
            m = jnp.max(s, axis=-1, keepdims=True)
            e = jnp.exp(s - m)
            l = jnp.sum(e, axis=-1, keepdims=True)
            p = (e * (1.0 / l)).astype(BF16)
            o_ref[:, cols] = _dot(p, kv_ref[:, X_W + h * LANES:X_W + (h + 1) * LANES]).astype(BF16)
            lse_ref[h] = m + jnp.log(l)

    return pl.pallas_call(
        body, name=name,
        out_shape=(jax.ShapeDtypeStruct((S, X_W), BF16), jax.ShapeDtypeStruct((X_HEADS, S, 1), F32)),
        grid=(S // tq,),
        in_specs=[pl.BlockSpec((tq, X_W), lambda i: (i, 0)), pl.BlockSpec((M, 2 * X_W), lambda i: (0, 0))],
        out_specs=(pl.BlockSpec((tq, X_W), lambda i: (i, 0)), pl.BlockSpec((X_HEADS, tq, 1), lambda i: (0, i, 0))),
        compiler_params=_cp(("parallel",)),
    )(q, kv)


def _xattn_bwd(q, kv, lse, do, *, name):
    S = q.shape[0]
    M = kv.shape[0]
    tq = _tile(S, 512)

    def body(q_ref, kv_ref, lse_ref, do_ref, dq_ref, dkv_ref):
        @pl.when(pl.program_id(0) == 0)
        def _():
            dkv_ref[...] = jnp.zeros_like(dkv_ref)

        for h in range(X_HEADS):
            cols = slice(h * LANES, (h + 1) * LANES)
            vcols = slice(X_W + h * LANES, X_W + (h + 1) * LANES)
            qh, kh, vh = q_ref[:, cols], kv_ref[:, cols], kv_ref[:, vcols]
            doh = do_ref[:, cols].astype(BF16)
            p = jnp.exp(_dot(qh, kh, _NT) * X_SCALE - lse_ref[h])
            dp = _dot(doh, vh, _NT)
            delta = jnp.sum(p * dp, axis=-1, keepdims=True)
            ds = (p * (dp - delta) * X_SCALE).astype(BF16)
            dq_ref[:, cols] = _dot(ds, kh).astype(BF16)
            dkv_ref[:, cols] += _dot(ds, qh, _TN)
            dkv_ref[:, vcols] += _dot(p.astype(BF16), doh, _TN)

    q_spec = pl.BlockSpec((tq, X_W), lambda i: (i, 0))
    return pl.pallas_call(
        body, name=name,
        out_shape=(jax.ShapeDtypeStruct((S, X_W), BF16), jax.ShapeDtypeStruct((M, 2 * X_W), F32)),
        grid=(S // tq,),
        in_specs=[q_spec, pl.BlockSpec((M, 2 * X_W), lambda i: (0, 0)),
                  pl.BlockSpec((X_HEADS, tq, 1), lambda i: (0, i, 0)), q_spec],
        out_specs=(q_spec, pl.BlockSpec((M, 2 * X_W), lambda i: (0, 0))),
        compiler_params=_cp(("arbitrary",)),
    )(q, kv, lse, do)


def _shift_down(h, row):
    return jnp.where(row == 0, 0.0, pltpu.roll(h, 1, 0))


def _shift_up(h, row, S):
    return jnp.where(row == S - 1, 0.0, pltpu.roll(h, S - 1, 0))


def _conv3(h, ck, cb, row, S):
    return _shift_down(h, row) * ck[0:1] + h * ck[1:2] + _shift_up(h, row, S) * ck[2:3] + cb


def _conv_act_fwd(h, ck, cb, *, name):
    S, F2 = h.shape
    F = F2 // 2
    nt = F // LANES

    def body(ha_ref, hb_ref, cka_ref, ckb_ref, cba_ref, cbb_ref, o_ref):
        row = lax.broadcasted_iota(jnp.int32, (S, LANES), 0)
        a = _conv3(ha_ref[...], cka_ref[...], cba_ref[...], row, S)
        b = _conv3(hb_ref[...], ckb_ref[...], cbb_ref[...], row, S)
        o_ref[...] = (_gelu(a) * b).astype(BF16)

    ca = lambda j: (0, j)
    cbi = lambda j: (0, j + nt)
    return pl.pallas_call(
        body, name=name,
        out_shape=jax.ShapeDtypeStruct((S, F), BF16),
        grid=(nt,),
        in_specs=[pl.BlockSpec((S, LANES), ca), pl.BlockSpec((S, LANES), cbi), pl.BlockSpec((3, LANES), ca),
                  pl.BlockSpec((3, LANES), cbi), pl.BlockSpec((1, LANES), ca), pl.BlockSpec((1, LANES), cbi)],
        out_specs=pl.BlockSpec((S, LANES), ca),
        compiler_params=_cp(("parallel",)),
    )(h, h, ck, ck, cb, cb)


def _conv_act_bwd(h, ck, cb, dact, *, name):
    S, F2 = h.shape
    F = F2 // 2
    nt = F // LANES

    def body(ha_ref, hb_ref, cka_ref, ckb_ref, cba_ref, cbb_ref, d_ref,
             dha_ref, dhb_ref, dcka_ref, dckb_ref, dcba_ref, dcbb_ref):
        row = lax.broadcasted_iota(jnp.int32, (S, LANES), 0)
        ha, hb = ha_ref[...], hb_ref[...]
        cka, ckb = cka_ref[...], ckb_ref[...]
        a = _conv3(ha, cka, cba_ref[...], row, S)
        b = _conv3(hb, ckb, cbb_ref[...], row, S)
        d = d_ref[...]
        da = d * b * _gelu_grad(a)
        db = d * _gelu(a)
        for dd, hh, ck_, dh_ref, dck_ref, dcb_ref in ((da, ha, cka, dha_ref, dcka_ref, dcba_ref),
                                                      (db, hb, ckb, dhb_ref, dckb_ref, dcbb_ref)):
            dcb_ref[...] = jnp.sum(dd, axis=0, keepdims=True)
            dck_ref[0:1, :] = jnp.sum(dd * _shift_down(hh, row), axis=0, keepdims=True)
            dck_ref[1:2, :] = jnp.sum(dd * hh, axis=0, keepdims=True)
            dck_ref[2:3, :] = jnp.sum(dd * _shift_up(hh, row, S), axis=0, keepdims=True)
            dh = _shift_up(dd, row, S) * ck_[0:1] + dd * ck_[1:2] + _shift_down(dd, row) * ck_[2:3]
            dh_ref[...] = dh.astype(BF16)

    ca = lambda j: (0, j)
    cbi = lambda j: (0, j + nt)
    col = pl.BlockSpec((S, LANES), ca)
    return pl.pallas_call(
        body, name=name,
        out_shape=(jax.ShapeDtypeStruct((S, F), BF16), jax.ShapeDtypeStruct((S, F), BF16),
                   jax.ShapeDtypeStruct((3, F), F32), jax.ShapeDtypeStruct((3, F), F32),
                   jax.ShapeDtypeStruct((1, F), F32), jax.ShapeDtypeStruct((1, F), F32)),
        grid=(nt,),
        in_specs=[col, pl.BlockSpec((S, LANES), cbi), pl.BlockSpec((3, LANES), ca), pl.BlockSpec((3, LANES), cbi),
                  pl.BlockSpec((1, LANES), ca), pl.BlockSpec((1, LANES), cbi), col],
        out_specs=(col, col, pl.BlockSpec((3, LANES), ca), pl.BlockSpec((3, LANES), ca),
                   pl.BlockSpec((1, LANES), ca), pl.BlockSpec((1, LANES), ca)),
        compiler_params=_cp(("parallel",)),
    )(h, h, ck, ck, cb, cb, dact)


def _layer_fwd(x, xb, memb, w, tabs, seg, l, gather):
    n = lambda s: f"L{l}_{s}"
    qg2 = jnp.tile(w["b_q_gain"], 2)[None, :]
    kg2 = jnp.tile(w["b_k_gain"], 2)[None, :]
    proj = _mm(xb, w["w_in"], name=n("proj"))
    aq, ak4, av4, bq, bk4, bv4 = _prep(proj, tabs, qg2, kg2, seg, name=n("prep"))
    oa, lse_a = _attn_win_fwd(aq, ak4, av4, w["a_sink"], name=n("attn_win"))
    ob, lse_b, gathered = _attn_dense_fwd(bq, bk4, bv4, gather=gather, name=n("attn_dense"))
    oc = _gmlp_fwd(proj, w["c_ws"], w["c_bs3"], w["c_ln_g"], w["c_ln_b"], name=n("gmlp"))
    merged = _merge_fwd(oa, ob, oc, w["w_branch"], proj, w["b_gate"], name=n("merge"))
    x1, x1b, xh1, rs1 = _mm_res_ln(merged, w["w_mix_out"], x, w["ln1_g"], w["ln1_b"], name=n("mix_ln1"))
    xq = _mm(x1b, w["x_wq"], out_dtype=BF16, name=n("xq"))
    xkv = _mm(memb, w["x_wkv"], out_dtype=BF16, name=n("xkv"))
    xo, lse_x = _xattn_fwd(xq, xkv, name=n("xattn"))
    x2, x2b, xh2, rs2 = _mm_res_ln(xo, w["x_wo"], x1, w["ln2_g"], w["ln2_b"], name=n("xo_ln2"))
    h = _mm(x2b, w["f_w_up"], name=n("ffn_up"))
    act = _conv_act_fwd(h, w["f_conv_k"], w["f_conv_b"], name=n("conv_act"))
    x3, x3b, xh3, rs3 = _mm_res_ln(act, w["f_w_down"], x2, w["ln3_g"], w["ln3_b"], name=n("down_ln3"))
    saved = dict(xb=xb, proj=proj, aq=aq, ak4=ak4, av4=av4, bq=bq, bk4=bk4, bv4=bv4, lse_a=lse_a, lse_b=lse_b,
                 oa=oa, ob=ob, oc=oc, merged=merged, xh1=xh1, rs1=rs1, x1b=x1b, xq=xq, xkv=xkv, xo=xo, lse_x=lse_x,
                 xh2=xh2, rs2=rs2, x2b=x2b, h=h, act=act, xh3=xh3, rs3=rs3, qg2=qg2, kg2=kg2)
    return x3, x3b, saved, gathered


def _layer_bwd(dy, memb, w, sv, tabs, seg, l, scatter):
    n = lambda s: f"L{l}_{s}"
    g = {}
    big = {}

    def dw(key, a, b, tag):
        big[key] = _mm(a, b, ta=True, out_dtype=BF16, name=n(tag))

    dz3, dz3b, g["ln3_g"], g["ln3_b"] = _ln_bwd(dy, sv["xh3"], sv["rs3"], w["ln3_g"], name=n("ln3_bwd"))
    dw("f_w_down", sv["act"], dz3b, "dw_down")
    dact = _mm(dz3b, w["f_w_down"], tb=True, name=n("dact"))
    dha, dhb, dcka, dckb, dcba, dcbb = _conv_act_bwd(sv["h"], w["f_conv_k"], w["f_conv_b"], dact, name=n("conv_act_bwd"))
    dh = jnp.concatenate([dha, dhb], axis=1)
    g["f_conv_k"] = jnp.concatenate([dcka, dckb], axis=1)
    g["f_conv_b"] = jnp.concatenate([dcba, dcbb], axis=1)[0]
    dw("f_w_up", sv["x2b"], dh, "dw_up")
    dx2 = _mm(dh, w["f_w_up"], tb=True, res=dz3, res_scale=ALPHA, name=n("dx2"))
    dz2, dz2b, g["ln2_g"], g["ln2_b"] = _ln_bwd(dx2, sv["xh2"], sv["rs2"], w["ln2_g"], name=n("ln2_bwd"))
    dw("x_wo", sv["xo"], dz2b, "dw_xo")
    dxo = _mm(dz2b, w["x_wo"], tb=True, out_dtype=BF16, name=n("dxo"))
    dxq, dxkv = _xattn_bwd(sv["xq"], sv["xkv"], sv["lse_x"], dxo, name=n("xattn_bwd"))
    dw("x_wq", sv["x1b"], dxq, "dw_xq")
    dw("x_wkv", memb, dxkv, "dw_xkv")
    dx1 = _mm(dxq, w["x_wq"], tb=True, res=dz2, res_scale=ALPHA, name=n("dx1"))
    dz1, dz1b, g["ln1_g"], g["ln1_b"] = _ln_bwd(dx1, sv["xh1"], sv["rs1"], w["ln1_g"], name=n("ln1_bwd"))
    dw("w_mix_out", sv["merged"], dz1b, "dw_mix")
    dmerged = _mm(dz1b, w["w_mix_out"], tb=True, name=n("dmerged"))
    dgl, dt, dbr, dbg = _merge_bwd(sv["oa"], sv["ob"], sv["oc"], w["w_branch"], sv["proj"], w["b_gate"], dmerged,
                                   name=n("merge_bwd"))
    g["b_gate"] = dbg[0]
    for i, k in enumerate(("oa", "ob", "oc")):
        dw(f"w_branch{i}", sv[k], dt[i], f"dw_branch{i}")
    big["w_branch"] = jnp.stack([big.pop(f"w_branch{i}") for i in range(N_BRANCH)])
    dqa, dka, dva, dsink = _attn_win_bwd(sv["aq"], sv["ak4"], sv["av4"], w["a_sink"], sv["lse_a"], dbr[0],
                                         name=n("attn_win_bwd"))
    g["a_sink"] = dsink[:, 0]
    dqb, dkb, dvb, recvs = _attn_dense_bwd(sv["bq"], sv["bk4"], sv["bv4"], sv["lse_b"], dbr[1], scatter=scatter,
                                           name=n("attn_dense_bwd"))
    dcz, g["c_ws"], dbs3, dlg, dlb = _gmlp_bwd(sv["proj"], dbr[2], w["c_ws"], w["c_bs3"], w["c_ln_g"], w["c_ln_b"],
                                               name=n("gmlp_bwd"))
    g["c_bs"] = dbs3[:, :, 0]
    g["c_ln_g"], g["c_ln_b"] = dlg[0], dlb[0]
    dqkv, dqg, dkg = _unprep(dqa, dka, dva, dqb, dkb, dvb, sv["proj"], tabs, sv["qg2"], sv["kg2"], seg, name=n("unprep"))
    g["b_q_gain"] = dqg[0, :HEAD_DIM] + dqg[0, HEAD_DIM:]
    g["b_k_gain"] = dkg[0, :HEAD_DIM] + dkg[0, HEAD_DIM:]
    dproj = jnp.concatenate([dqkv, dcz, dgl], axis=1)
    dw("w_in", sv["xb"], dproj, "dw_in")
    dx0 = _mm(dproj, w["w_in"], tb=True, res=dz1, res_scale=ALPHA, name=n("dx0"))
    for k in ("ln1_g", "ln1_b", "ln2_g", "ln2_b", "ln3_g", "ln3_b"):
        g[k] = g[k][0]
    return dx0, g, big, recvs


WEIGHTS = ("w_in", "b_gate", "a_sink", "b_q_gain", "b_k_gain", "c_ln_g", "c_ln_b", "c_ws", "c_bs", "w_branch",
           "w_mix_out", "ln1_g", "ln1_b", "x_wq", "x_wkv", "x_wo", "ln2_g", "ln2_b", "f_w_up", "f_conv_k",
           "f_conv_b", "f_w_down", "ln3_g", "ln3_b")
BIG_AXIS = {"w_in": 1, "w_branch": 2, "w_mix_out": 0, "x_wq": 0, "x_wkv": 0, "x_wo": 1, "f_w_up": 1, "f_w_down": 0}
BIG = tuple(BIG_AXIS)
GATHERED = BIG + ("f_conv_k",)
GATHER_AXIS = dict(BIG_AXIS, f_conv_k=1)
SMALL = tuple(k for k in WEIGHTS if k not in GATHERED)


def _unshard(g, axis):
    t = jnp.moveaxis(g, 0, axis)
    return t.reshape(t.shape[:axis] + (t.shape[axis] * t.shape[axis + 1],) + t.shape[axis + 2:])


def _reshard(full, axis):
    t = full.reshape(full.shape[:axis] + (N_DEV, full.shape[axis] // N_DEV) + full.shape[axis + 1:])
    return jnp.moveaxis(t, axis, 0)


def _layer_weights(small, gathered, l):
    w = {k: v[l] for k, v in small.items()}
    for k in ("c_ln_g", "c_ln_b", "ln1_g", "ln1_b", "ln2_g", "ln2_b", "ln3_g", "ln3_b", "b_gate", "f_conv_b"):
        w[k] = w[k][None, :]
    w["c_bs3"] = w["c_bs"][:, :, None]
    w["c_ws"] = w["c_ws"].astype(BF16)
    for k, g in zip(GATHERED, gathered):
        w[k] = _unshard(g, GATHER_AXIS[k])
    return w


def _local_step(x, mem, target, small, shards):
    S = x.shape[0]
    tabs = _rope_tables(S)
    seg = _seg_matrix()
    memb = mem.astype(BF16)
    xb = x.astype(BF16)
    saved, weights = [], []
    gathered = _gather_call(shards, 0, name="gather_L0")
    for l in range(DEPTH):
        w = _layer_weights(small, gathered, l)
        gather = (shards, l + 1) if l + 1 < DEPTH else None
        x, xb, sv, gathered = _layer_fwd(x, xb, memb, w, tabs, seg, l, gather)
        saved.append(sv)
        weights.append(w)
    dy, loss = _loss_head(x, target, name="loss_head")
    grads = [None] * DEPTH
    recvs = [jax.ShapeDtypeStruct((DEPTH, N_DEV) + s.shape[1:], BF16) for s in shards[:len(BIG)]]
    sends = None
    for l in reversed(range(DEPTH)):
        scatter = (sends, recvs, l + 1) if sends is not None else None
        dy, grads[l], big, got = _layer_bwd(dy, memb, weights[l], saved[l], tabs, seg, l, scatter)
        if scatter is not None:
            recvs = got
        sends = [(_reshard(big[k], BIG_AXIS[k]), None) for k in BIG]
    recvs = _scatter_call(sends, recvs, 0, name="scatter_L0")
    return loss, dy, grads, recvs


PACK_W = 1024


def _gather_call(xs, l, *, name):
    na = len(xs)

    def body(*refs):
        start, finish = _gather_plan([(refs[a], l, refs[na + a]) for a in range(na)], *refs[2 * na:])
        start()
        finish()

    return list(pl.pallas_call(
        body, name=name,
        out_shape=[_gathered_shape(x) for x in xs],
        in_specs=[_ANY] * na, out_specs=[_ANY] * na,
        scratch_shapes=_comm_scratch(na),
    )(*xs))


def _scatter_io(sends, recvs):
    na = len(sends)
    held = [a for a in range(na) if not isinstance(recvs[a], jax.ShapeDtypeStruct)]
    return ([s for s, _ in sends] + [recvs[a] for a in held],
            [jax.ShapeDtypeStruct(r.shape, r.dtype) for r in recvs], held)


def _scatter_call(sends, recvs, lr, *, name):
    na = len(sends)
    operands, out_shape, held = _scatter_io(sends, recvs)
    n_in = len(operands)

    def body(*refs):
        start, finish = _scatter_plan([(refs[a], sends[a][1], refs[n_in + a], lr) for a in range(na)],
                                      *refs[n_in + na:])
        start()
        finish()

    return list(pl.pallas_call(
        body, name=name,
        out_shape=out_shape,
        in_specs=[_ANY] * n_in, out_specs=[_ANY] * na,
        scratch_shapes=_comm_scratch(na),
        input_output_aliases={na + i: a for i, a in enumerate(held)},
    )(*operands))


def _sum_parts(parts, *, name):
    P, R, C = parts.shape
    tr = _tile(R, 56, align=8)

    def body(p_ref, o_ref):
        g = p_ref[0].astype(F32)
        for s in range(1, P):
            g = g + p_ref[s].astype(F32)
        o_ref[...] = g

    return pl.pallas_call(
        body, name=name, out_shape=jax.ShapeDtypeStruct((R, C), F32), grid=(R // tr,),
        in_specs=[pl.BlockSpec((P, tr, C), lambda i: (0, i, 0))], out_specs=pl.BlockSpec((tr, C), lambda i: (i, 0)),
        compiler_params=_cp(("parallel",)),
    )(parts)


ADAM_BLOCK_ELEMS = 512 * 1024


def _adamw(parts, w, m, v, *, name):
    L, P, R, C = parts.shape
    assert w.shape == (L, R, C), (parts.shape, w.shape)
    tr = _tile(R, max(16, ADAM_BLOCK_ELEMS // C), align=16)

    def body(p_ref, w_ref, m_ref, v_ref, g_ref, d_ref, nm_ref, nv_ref):
        g = p_ref[0].astype(F32)
        for s in range(1, P):
            g = g + p_ref[s].astype(F32)
        nm = ADAM_B1 * m_ref[...] + (1.0 - ADAM_B1) * g
        nv = ADAM_B2 * v_ref[...] + (1.0 - ADAM_B2) * (g * g)
        m_hat = nm / (1.0 - ADAM_B1 ** ADAM_STEP)
        v_hat = nv / (1.0 - ADAM_B2 ** ADAM_STEP)
        g_ref[...] = g
        d_ref[...] = -ADAM_LR * (m_hat / (jnp.sqrt(v_hat) + ADAM_EPS) + ADAM_WD * w_ref[...])
        nm_ref[...] = nm
        nv_ref[...] = nv

    blk = pl.BlockSpec((None, tr, C), lambda l, i: (l, i, 0))
    shp = jax.ShapeDtypeStruct((L, R, C), F32)
    return pl.pallas_call(
        body, name=name, out_shape=(shp, shp, shp, shp), grid=(L, R // tr),
        in_specs=[pl.BlockSpec((None, P, tr, C), lambda l, i: (l, 0, i, 0)), blk, blk, blk],
        out_specs=(blk, blk, blk, blk),
        compiler_params=_cp(("parallel", "parallel")),
    )(parts, w, m, v)


def _pad_rows(vec, width, row_align):
    n = vec.shape[0]
    rows = -(-n // width)
    rows = -(-rows // row_align) * row_align
    return jnp.pad(vec, (0, rows * width - n)).reshape(rows, width)


def kernel(x, mem, w_in, b_gate, a_sink, b_q_gain, b_k_gain, c_ln_g, c_ln_b, c_ws, c_bs, w_branch, w_mix_out, ln1_g, ln1_b, x_wq, x_wkv, x_wo, ln2_g, ln2_b, f_w_up, f_conv_k, f_conv_b, f_w_down, ln3_g, ln3_b, loss_target, m_w_in, m_b_gate, m_a_sink, m_b_q_gain, m_b_k_gain, m_c_ln_g, m_c_ln_b, m_c_ws, m_c_bs, m_w_branch, m_w_mix_out, m_ln1_g, m_ln1_b, m_x_wq, m_x_wkv, m_x_wo, m_ln2_g, m_ln2_b, m_f_w_up, m_f_conv_k, m_f_conv_b, m_f_w_down, m_ln3_g, m_ln3_b, v_w_in, v_b_gate, v_a_sink, v_b_q_gain, v_b_k_gain, v_c_ln_g, v_c_ln_b, v_c_ws, v_c_bs, v_w_branch, v_w_mix_out, v_ln1_g, v_ln1_b, v_x_wq, v_x_wkv, v_x_wo, v_ln2_g, v_ln2_b, v_f_w_up, v_f_conv_k, v_f_conv_b, v_f_w_down, v_ln3_g, v_ln3_b):
    w = dict(w_in=w_in, b_gate=b_gate, a_sink=a_sink, b_q_gain=b_q_gain, b_k_gain=b_k_gain, c_ln_g=c_ln_g,
             c_ln_b=c_ln_b, c_ws=c_ws, c_bs=c_bs, w_branch=w_branch, w_mix_out=w_mix_out, ln1_g=ln1_g, ln1_b=ln1_b,
             x_wq=x_wq, x_wkv=x_wkv, x_wo=x_wo, ln2_g=ln2_g, ln2_b=ln2_b, f_w_up=f_w_up, f_conv_k=f_conv_k,
             f_conv_b=f_conv_b, f_w_down=f_w_down, ln3_g=ln3_g, ln3_b=ln3_b)
    m = dict(w_in=m_w_in, b_gate=m_b_gate, a_sink=m_a_sink, b_q_gain=m_b_q_gain, b_k_gain=m_b_k_gain,
             c_ln_g=m_c_ln_g, c_ln_b=m_c_ln_b, c_ws=m_c_ws, c_bs=m_c_bs, w_branch=m_w_branch, w_mix_out=m_w_mix_out,
             ln1_g=m_ln1_g, ln1_b=m_ln1_b, x_wq=m_x_wq, x_wkv=m_x_wkv, x_wo=m_x_wo, ln2_g=m_ln2_g, ln2_b=m_ln2_b,
             f_w_up=m_f_w_up, f_conv_k=m_f_conv_k, f_conv_b=m_f_conv_b, f_w_down=m_f_w_down, ln3_g=m_ln3_g,
             ln3_b=m_ln3_b)
    v = dict(w_in=v_w_in, b_gate=v_b_gate, a_sink=v_a_sink, b_q_gain=v_b_q_gain, b_k_gain=v_b_k_gain,
             c_ln_g=v_c_ln_g, c_ln_b=v_c_ln_b, c_ws=v_c_ws, c_bs=v_c_bs, w_branch=v_w_branch, w_mix_out=v_w_mix_out,
             ln1_g=v_ln1_g, ln1_b=v_ln1_b, x_wq=v_x_wq, x_wkv=v_x_wkv, x_wo=v_x_wo, ln2_g=v_ln2_g, ln2_b=v_ln2_b,
             f_w_up=v_f_w_up, f_conv_k=v_f_conv_k, f_conv_b=v_f_conv_b, f_w_down=v_f_w_down, ln3_g=v_ln3_g,
             ln3_b=v_ln3_b)
    me = 4 * lax.axis_index("x") + 2 * lax.axis_index("y") + lax.axis_index("c")

    shards = [w[k].astype(BF16) for k in BIG] + [w["f_conv_k"]]
    loss, grad_x, grads, recvs = _local_step(x[0], mem[0], loss_target[0], {k: w[k] for k in SMALL}, shards)
    loss = lax.psum(loss[0, 0], ("x", "y", "c"))

    out_g, out_d, out_m, out_v = {}, {}, {}, {}
    for k, recv in zip(BIG, recvs):
        shp = w[k].shape
        rc = (DEPTH, math.prod(shp[1:-1]), shp[-1])
        parts = recv.reshape((DEPTH, N_DEV) + rc[1:])
        g_, d_, m_, v_ = _adamw(parts, w[k].reshape(rc), m[k].reshape(rc), v[k].reshape(rc), name=f"adamw_{k}")
        out_g[k], out_d[k], out_m[k], out_v[k] = (t.reshape(shp) for t in (g_, d_, m_, v_))

    small_all = SMALL + ("f_conv_k",)
    gfull = {k: jnp.stack([grads[l][k] for l in range(DEPTH)]) for k in small_all}
    svec = _pad_rows(jnp.concatenate([gfull[k].reshape(-1) for k in small_all]), PACK_W, 8)
    ssum = _sum_parts(_gather_call([svec[None]], 0, name="gather_small_grads")[0], name="sum_small_grads").reshape(-1)
    sg, acc = {}, 0
    for k in small_all:
        sg[k] = ssum[acc:acc + gfull[k].size].reshape(gfull[k].shape)
        acc += gfull[k].size
    width = w["f_conv_k"].shape[2]
    sg["f_conv_k"] = lax.dynamic_slice_in_dim(sg["f_conv_k"], me * width, width, axis=2)
    pack = lambda d: _pad_rows(jnp.concatenate([d[k].reshape(-1) for k in small_all]), PACK_W, 8)[None]
    g_, d_, m_, v_ = _adamw(pack(sg)[None], pack(w), pack(m), pack(v), name="adamw_small")
    acc = 0
    for k in small_all:
        n, shp = w[k].size, w[k].shape
        out_g[k] = sg[k]
        out_d[k], out_m[k], out_v[k] = (t.reshape(-1)[acc:acc + n].reshape(shp) for t in (d_, m_, v_))
        acc += n

    return (loss, grad_x[None], *[out_g[k] for k in WEIGHTS], *[out_d[k] for k in WEIGHTS],
            *[out_m[k] for k in WEIGHTS], *[out_v[k] for k in WEIGHTS])
```

```python
import functools
import math

import jax
import jax.numpy as jnp
from jax import lax
from jax.experimental import pallas as pl
from jax.experimental.pallas import tpu as pltpu

F32 = jnp.float32
BF16 = jnp.bfloat16

DEPTH = 4
HEAD_DIM = 64
BLOCK = 128
WINDOW = 128
GRID_W = 64
C_WIDTH = 512
C_GROUPS = 4
CHUNK = 128
N_BRANCH = 3
BRANCH_WIDTH = 512
ROPE_THETA = 10000.0
X_HEADS = 4
X_HEAD_DIM = 128
ALPHA = (2 * DEPTH) ** 0.25
LN_EPS = 1e-5
RMS_EPS = 1e-6
ADAM_LR = 0.001
ADAM_B1 = 0.9
ADAM_B2 = 0.999
ADAM_EPS = 1e-08
ADAM_WD = 0.01
ADAM_STEP = 10
N_DEV = 8

COL_A = 0
COL_B = 768
COL_C = 1536
COL_GATE = 2560
QKV_W = 768

LANES = 128
V7X_VMEM_BYTES = 64 * 1024 * 1024
VMEM_LIMIT = V7X_VMEM_BYTES - 8 * 1024 * 1024
NEG_BIG = -1e30

_NT = (((1,), (1,)), ((), ()))
_TN = (((0,), (0,)), ((), ()))
_NN = (((1,), (0,)), ((), ()))


def _cp(sem=None):
    return pltpu.CompilerParams(dimension_semantics=sem, vmem_limit_bytes=VMEM_LIMIT)


def _tile(n, target, align=LANES):
    if n <= target:
        return n
    best = None
    for t in range(align, target + 1, align):
        if n % t == 0:
            best = t
    assert best is not None, (n, target)
    return best


def _dot(a, b, dims=_NN):
    return lax.dot_general(a, b, dims, preferred_element_type=F32)


def _gelu(x):
    return 0.5 * x * (1.0 + lax.erf(x * 0.7071067811865476))


def _gelu_and_grad(x):
    cdf = 0.5 * (1.0 + lax.erf(x * 0.7071067811865476))
    return x * cdf, cdf + x * jnp.exp(-0.5 * x * x) * 0.3989422804014327


def _sigmoid(x):
    return 1.0 / (1.0 + jnp.exp(-x))


MESH_ID = pl.DeviceIdType.MESH
_ANY = pl.BlockSpec(memory_space=pl.ANY)
COPIES_PER_ARRAY = N_DEV - 1


def _comm_scratch(n_arrays):
    return [pltpu.SemaphoreType.DMA((COPIES_PER_ARRAY * n_arrays,)),
            pltpu.SemaphoreType.DMA((COPIES_PER_ARRAY * n_arrays,)), pltpu.SemaphoreType.DMA((n_arrays,))]


def _gathered_shape(x):
    return jax.ShapeDtypeStruct((N_DEV,) + x.shape[1:], x.dtype)


def _gather_plan(entries, send_sems, recv_sems, local_sems):
    mx, my, mc = lax.axis_index("x"), lax.axis_index("y"), lax.axis_index("c")
    me, sibling = (mx, my, mc), (mx, my, 1 - mc)
    chips = [(1 - mx, my), (mx, 1 - my), (1 - mx, 1 - my)]

    def copy(a, k, block, to, from_shard=False):
        x_ref, l, out_ref = entries[a]
        dst = out_ref.at[4 * block[0] + 2 * block[1] + block[2]]
        return pltpu.make_async_remote_copy(
            src_ref=x_ref.at[l] if from_shard else dst, dst_ref=dst,
            send_sem=send_sems.at[COPIES_PER_ARRAY * a + k], recv_sem=recv_sems.at[COPIES_PER_ARRAY * a + k],
            device_id=to, device_id_type=MESH_ID)

    def own(a):
        x_ref, l, out_ref = entries[a]
        return pltpu.make_async_copy(x_ref.at[l], out_ref.at[4 * mx + 2 * my + mc], local_sems.at[a])

    def first(a):
        return [copy(a, 0, me, sibling, True)] + [copy(a, 1 + j, me, (*chip, mc), True) for j, chip in enumerate(chips)]

    def passed(a):
        return [copy(a, 4 + j, (*chip, mc), sibling) for j, chip in enumerate(chips)]

    def start():
        for a in range(len(entries)):
            own(a).start()
            for cp in first(a):
                cp.start()

    def finish():
        for a in range(len(entries)):
            fwd = passed(a)
            for j, chip in enumerate(chips):
                copy(a, 1 + j, (*chip, mc), me).wait_recv()
                fwd[j].start()
        for a in range(len(entries)):
            copy(a, 0, sibling, me).wait_recv()
            for j, chip in enumerate(chips):
                copy(a, 4 + j, (*chip, 1 - mc), me).wait_recv()
            for cp in first(a) + passed(a):
                cp.wait_send()
            own(a).wait()

    return start, finish


def _scatter_plan(entries, send_sems, recv_sems, local_sems):
    mx, my, mc = lax.axis_index("x"), lax.axis_index("y"), lax.axis_index("c")
    me = 4 * mx + 2 * my + mc

    def src(a, dev):
        send_ref, ls, _, _ = entries[a]
        return send_ref.at[dev] if ls is None else send_ref.at[ls, dev]

    def copies(a):
        _, _, recv_ref, lr = entries[a]
        out = []
        for k in range(1, N_DEV):
            px = 1 - mx if k & 4 else mx
            py = 1 - my if k & 2 else my
            pc = 1 - mc if k & 1 else mc
            peer = 4 * px + 2 * py + pc
            sems = dict(send_sem=send_sems.at[COPIES_PER_ARRAY * a + k - 1],
                        recv_sem=recv_sems.at[COPIES_PER_ARRAY * a + k - 1],
                        device_id=(px, py, pc), device_id_type=MESH_ID)
            sends = pltpu.make_async_remote_copy(src_ref=src(a, peer), dst_ref=recv_ref.at[lr, me], **sems)
            lands = pltpu.make_async_remote_copy(src_ref=src(a, me), dst_ref=recv_ref.at[lr, peer], **sems)
            out.append((sends, lands))
        return out

    def own(a):
        _, _, recv_ref, lr = entries[a]
        return pltpu.make_async_copy(src(a, me), recv_ref.at[lr, me], local_sems.at[a])

    def start():
        for a in range(len(entries)):
            own(a).start()
            for sends, _ in copies(a):
                sends.start()

    def finish():
        for a in range(len(entries)):
            for _, lands in copies(a):
                lands.wait_recv()
        for a in range(len(entries)):
            for sends, _ in copies(a):
                sends.wait_send()
            own(a).wait()

    return start, finish


MM_TM, MM_TN, MM_TK = 1024, 1536, 2048


def _mm(a, b, *, ta=False, tb=False, out_dtype=F32, res=None, res_scale=1.0, into=None, name):
    segs = list(a) if isinstance(a, (list, tuple)) else [a]
    if ta:
        (K, M), seg_k = segs[0].shape, [segs[0].shape[0]]
        assert len(segs) == 1
    else:
        M, seg_k = segs[0].shape[0], [s.shape[1] for s in segs]
        K = sum(seg_k)
    if tb:
        N, Kb = b.shape
    else:
        Kb, N = b.shape
    assert K == Kb, ([s.shape for s in segs], b.shape, ta, tb)
    row_off = into[1] if into is not None else 0
    tm = _tile(math.gcd(M, row_off), MM_TN) if into is not None else _tile(M, MM_TM)
    tn = _tile(N, MM_TN)
    tk = _tile(K, MM_TK) if len(segs) == 1 else _tile(math.gcd(*seg_k), MM_TN)
    nk = K // tk
    seg_chunks = [ks // tk for ks in seg_k]
    seg_first = [sum(seg_chunks[:s]) for s in range(len(segs))]
    dims = (((0 if ta else 1,), (1 if tb else 0,)), ((), ()))
    ns = len(segs)
    n_in = ns + 1 + (res is not None) + (into is not None)

    def body(*refs):
        a_refs, b_ref = refs[:ns], refs[ns]
        r_ref = refs[ns + 1] if res is not None else None
        o_ref = refs[n_in]

        def finish(out):
            if r_ref is not None:
                out = out + res_scale * r_ref[...]
            o_ref[...] = out.astype(o_ref.dtype)

        def prod(s):
            return _dot(a_refs[s][...].astype(BF16), b_ref[...].astype(BF16), dims)

        if nk == 1:
            finish(prod(0))
            return
        acc = refs[n_in + 1]
        k = pl.program_id(2)

        @pl.when(k == 0)
        def _():
            acc[...] = jnp.zeros_like(acc)

        for s in range(ns):
            def add(s=s):
                acc[...] += prod(s)
            pl.when((k >= seg_first[s]) & (k < seg_first[s] + seg_chunks[s]))(add)

        @pl.when(k == nk - 1)
        def _():
            finish(acc[...])

    if ta:
        a_specs = [pl.BlockSpec((tk, tm), lambda i, j, k: (k, i))]
    else:
        a_specs = [pl.BlockSpec((tm, tk), functools.partial(
            lambda i, j, k, first, n: (i, jnp.clip(k - first, 0, n - 1)), first=seg_first[s], n=seg_chunks[s]))
            for s in range(ns)]
    b_spec = pl.BlockSpec((tn, tk), lambda i, j, k: (j, k)) if tb else pl.BlockSpec((tk, tn), lambda i, j, k: (k, j))
    in_specs = a_specs + [b_spec]
    args = segs + [b]
    if res is not None:
        in_specs.append(pl.BlockSpec((tm, tn), lambda i, j, k: (i, j)))
        args.append(res)
    if into is None:
        out_shape = jax.ShapeDtypeStruct((M, N), out_dtype)
        blk_off, aliases = 0, {}
    else:
        buf = into[0]
        assert buf.shape[1] == N and row_off % tm == 0 and row_off + M <= buf.shape[0], (buf.shape, M, N, row_off)
        out_shape = jax.ShapeDtypeStruct(buf.shape, buf.dtype)
        blk_off, aliases = row_off // tm, {n_in - 1: 0}
        in_specs.append(_ANY)
        args.append(buf)
    return pl.pallas_call(
        body, name=name,
        out_shape=out_shape,
        grid=(M // tm, N // tn, nk),
        in_specs=in_specs,
        out_specs=pl.BlockSpec((tm, tn), lambda i, j, k: (i + blk_off, j)),
        scratch_shapes=[pltpu.VMEM((tm, tn), F32)] if nk > 1 else [],
        input_output_aliases=aliases,
        compiler_params=_cp(("parallel", "parallel", "arbitrary")),
    )(*args)


def _mm_res_ln(a, w, x, g, b, *, name):
    S, K = a.shape
    D = w.shape[1]
    tm = _tile(S, 256)

    def body(a_ref, w_ref, x_ref, g_ref, b_ref, y_ref, yb_ref, xh_ref, rs_ref):
        h = _dot(a_ref[...], w_ref[...])
        z = ALPHA * x_ref[...] + h
        mu = jnp.mean(z, axis=-1, keepdims=True)
        zc = z - mu
        var = jnp.mean(zc * zc, axis=-1, keepdims=True)
        r = lax.rsqrt(var + LN_EPS)
        xh = zc * r
        y = xh * g_ref[...] + b_ref[...]
        y_ref[...] = y
        yb_ref[...] = y.astype(BF16)
        xh_ref[...] = xh
        rs_ref[...] = r

    row = lambda i: (i, 0)
    full = lambda i: (0, 0)
    return pl.pallas_call(
        body, name=name,
        out_shape=(jax.ShapeDtypeStruct((S, D), F32), jax.ShapeDtypeStruct((S, D), BF16),
                   jax.ShapeDtypeStruct((S, D), F32), jax.ShapeDtypeStruct((S, 1), F32)),
        grid=(S // tm,),
        in_specs=[pl.BlockSpec((tm, K), row), pl.BlockSpec((K, D), full), pl.BlockSpec((tm, D), row),
                  pl.BlockSpec((1, D), full), pl.BlockSpec((1, D), full)],
        out_specs=(pl.BlockSpec((tm, D), row), pl.BlockSpec((tm, D), row), pl.BlockSpec((tm, D), row),
                   pl.BlockSpec((tm, 1), row)),
        compiler_params=_cp(("parallel",)),
    )(a, w, x, g, b)


def _ln_bwd(dy, xh, rs, g, *, name):
    S, D = dy.shape
    tm = _tile(S, 256)

    def body(dy_ref, xh_ref, rs_ref, g_ref, dz_ref, dzb_ref, dg_ref, db_ref):
        @pl.when(pl.program_id(0) == 0)
        def _():
            dg_ref[...] = jnp.zeros_like(dg_ref)
            db_ref[...] = jnp.zeros_like(db_ref)

        dy = dy_ref[...]
        xh = xh_ref[...]
        dxh = dy * g_ref[...]
        m1 = jnp.mean(dxh, axis=-1, keepdims=True)
        m2 = jnp.mean(dxh * xh, axis=-1, keepdims=True)
        dz = rs_ref[...] * (dxh - m1 - xh * m2)
        dz_ref[...] = dz
        dzb_ref[...] = dz.astype(BF16)
        dg_ref[...] += jnp.sum(dy * xh, axis=0, keepdims=True)
        db_ref[...] += jnp.sum(dy, axis=0, keepdims=True)

    row = lambda i: (i, 0)
    full = lambda i: (0, 0)
    return pl.pallas_call(
        body, name=name,
        out_shape=(jax.ShapeDtypeStruct((S, D), F32), jax.ShapeDtypeStruct((S, D), BF16),
                   jax.ShapeDtypeStruct((1, D), F32), jax.ShapeDtypeStruct((1, D), F32)),
        grid=(S // tm,),
        in_specs=[pl.BlockSpec((tm, D), row), pl.BlockSpec((tm, D), row), pl.BlockSpec((tm, 1), row),
                  pl.BlockSpec((1, D), full)],
        out_specs=(pl.BlockSpec((tm, D), row), pl.BlockSpec((tm, D), row), pl.BlockSpec((1, D), full),
                   pl.BlockSpec((1, D), full)),
        compiler_params=_cp(("arbitrary",)),
    )(dy, xh, rs, g)


def _loss_head(y, t, *, name):
    S, D = y.shape
    tm = _tile(S, 512)

    def body(y_ref, t_ref, dy_ref, l_ref):
        @pl.when(pl.program_id(0) == 0)
        def _():
            l_ref[...] = jnp.zeros_like(l_ref)

        e = y_ref[...] - t_ref[...]
        dy_ref[...] = e / D
        l_ref[...] += 0.5 * jnp.sum(jnp.mean(e * e, axis=-1, keepdims=True), axis=0, keepdims=True)

    row = lambda i: (i, 0)
    return pl.pallas_call(
        body, name=name,
        out_shape=(jax.ShapeDtypeStruct((S, D), F32), jax.ShapeDtypeStruct((1, 1), F32)),
        grid=(S // tm,),
        in_specs=[pl.BlockSpec((tm, D), row), pl.BlockSpec((tm, D), row)],
        out_specs=(pl.BlockSpec((tm, D), row), pl.BlockSpec((1, 1), lambda i: (0, 0))),
        compiler_params=_cp(("arbitrary",)),
    )(y, t)


def _rope_tables(S):
    pos = jnp.arange(S, dtype=jnp.int32)
    row = pos // GRID_W
    col = pos % GRID_W

    def cs(p, d):
        half = d // 2
        inv = ROPE_THETA ** (-jnp.arange(half, dtype=F32) * (2.0 / d))
        ang = p.astype(F32)[:, None] * inv[None, :]
        c, s = jnp.cos(ang), jnp.sin(ang)
        return jnp.concatenate([c, c], -1), jnp.concatenate([-s, s], -1)

    ca, sa = cs(pos, HEAD_DIM)
    cr, sr = cs(row, HEAD_DIM // 2)
    cc, sc = cs(col, HEAD_DIM // 2)
    cb, sb = jnp.concatenate([cr, cc], -1), jnp.concatenate([sr, sc], -1)
    two = lambda t: jnp.concatenate([t, t], -1)
    return two(ca), two(sa), two(cb), two(sb)


def _partner(x, lane, width):
    h = width // 2
    return jnp.where(lane % width < h, pltpu.roll(x, LANES - h, 1), pltpu.roll(x, h, 1))


def _rope_fwd(x, c, s, lane, width):
    return x * c + _partner(x, lane, width) * s


def _rope_bwd(dy, c, s, lane, width):
    return dy * c + _partner(dy * s, lane, width)


def _head_sum(x, seg):
    return lax.dot_general(x, seg, _NN, precision=lax.Precision.HIGHEST, preferred_element_type=F32)


def _split_heads(x, lane):
    lo = lane < HEAD_DIM
    r = pltpu.roll(x, HEAD_DIM, 1)
    z = jnp.zeros_like(x)
    return jnp.where(lo, x, z), jnp.where(lo, z, r), jnp.where(lo, r, z), jnp.where(lo, z, x)


def _fold_heads(d0, d1, lane):
    t0 = d0 + pltpu.roll(d0, HEAD_DIM, 1)
    t1 = d1 + pltpu.roll(d1, HEAD_DIM, 1)
    return jnp.where(lane < HEAD_DIM, t0, t1)


def _seg_matrix():
    i = jnp.arange(LANES)
    return (i[:, None] // HEAD_DIM == i[None, :] // HEAD_DIM).astype(F32)


def _prep(proj, tabs, qg2, kg2, seg, *, name):
    S = proj.shape[0]
    ts = _tile(S, 256)
    ca, sa, cb, sb = tabs

    def body(pa_ref, pb_ref, ca_ref, sa_ref, cb_ref, sb_ref, qg_ref, kg_ref, seg_ref,
             aq_ref, ak_ref, av_ref, bq_ref, bk_ref, bv_ref):
        lane = lax.broadcasted_iota(jnp.int32, (ts, LANES), 1)
        ca, sa, cb, sb = ca_ref[...], sa_ref[...], cb_ref[...], sb_ref[...]
        seg = seg_ref[...]

        def norm(x, gain):
            r = lax.rsqrt(_head_sum(x * x, seg) * (1.0 / HEAD_DIM) + RMS_EPS)
            return x * r * gain

        def put(ref, x):
            for i, part in enumerate(_split_heads(x, lane)):
                ref[i] = part.astype(BF16)

        for gidx in range(4):
            cols = slice(gidx * LANES, (gidx + 1) * LANES)
            aq_ref[:, cols] = (_rope_fwd(pa_ref[:, cols], ca, sa, lane, HEAD_DIM) * 0.125).astype(BF16)
            bq = norm(pb_ref[:, cols], qg_ref[...])
            bq_ref[:, cols] = (_rope_fwd(bq, cb, sb, lane, HEAD_DIM // 2) * 0.125).astype(BF16)
        put(ak_ref, _rope_fwd(pa_ref[:, 512:640], ca, sa, lane, HEAD_DIM))
        put(av_ref, pa_ref[:, 640:768])
        bk = norm(pb_ref[:, 512:640], kg_ref[...])
        put(bk_ref, _rope_fwd(bk, cb, sb, lane, HEAD_DIM // 2))
        put(bv_ref, pb_ref[:, 640:768])

    row = lambda i: (i, 0)
    full = lambda i: (0, 0)
    tab = pl.BlockSpec((ts, LANES), row)
    kv_shape = jax.ShapeDtypeStruct((4, S, LANES), BF16)
    kv_spec = pl.BlockSpec((4, ts, LANES), lambda i: (0, i, 0))
    q_shape = jax.ShapeDtypeStruct((S, 512), BF16)
    q_spec = pl.BlockSpec((ts, 512), row)
    return pl.pallas_call(
        body, name=name,
        out_shape=(q_shape, kv_shape, kv_shape, q_shape, kv_shape, kv_shape),
        grid=(S // ts,),
        in_specs=[pl.BlockSpec((ts, QKV_W), lambda i: (i, 0)), pl.BlockSpec((ts, QKV_W), lambda i: (i, 1)),
                  tab, tab, tab, tab, pl.BlockSpec((1, LANES), full), pl.BlockSpec((1, LANES), full),
                  pl.BlockSpec((LANES, LANES), full)],
        out_specs=(q_spec, kv_spec, kv_spec, q_spec, kv_spec, kv_spec),
        compiler_params=_cp(("parallel",)),
    )(proj, proj, ca, sa, cb, sb, qg2, kg2, seg)


def _unprep(dqa, dka, dva, dqb, dkb, dvb, proj, tabs, qg2, kg2, seg, *, name):
    S = proj.shape[0]
    ts = _tile(S, 256)
    ca, sa, cb, sb = tabs

    def body(dqa_ref, dka_ref, dva_ref, dqb_ref, dkb_ref, dvb_ref, pb_ref, ca_ref, sa_ref, cb_ref, sb_ref,
             qg_ref, kg_ref, seg_ref, dp_ref, dqg_ref, dkg_ref):
        @pl.when(pl.program_id(0) == 0)
        def _():
            dqg_ref[...] = jnp.zeros_like(dqg_ref)
            dkg_ref[...] = jnp.zeros_like(dkg_ref)

        lane = lax.broadcasted_iota(jnp.int32, (ts, LANES), 1)
        ca, sa, cb, sb = ca_ref[...], sa_ref[...], cb_ref[...], sb_ref[...]
        seg = seg_ref[...]

        def norm_bwd(dy, x, gain):
            r = lax.rsqrt(_head_sum(x * x, seg) * (1.0 / HEAD_DIM) + RMS_EPS)
            gdy = gain * dy
            dot = _head_sum(gdy * x, seg) * (1.0 / HEAD_DIM)
            dx = r * gdy - x * (r * r * r) * dot
            return dx, jnp.sum(dy * x * r, axis=0, keepdims=True)

        for gidx in range(4):
            cols = slice(gidx * LANES, (gidx + 1) * LANES)
            dp_ref[:, cols] = _rope_bwd(dqa_ref[:, cols] * 0.125, ca, sa, lane, HEAD_DIM).astype(BF16)
            dbq = _rope_bwd(dqb_ref[:, cols] * 0.125, cb, sb, lane, HEAD_DIM // 2)
            dx, dg = norm_bwd(dbq, pb_ref[:, cols], qg_ref[...])
            dp_ref[:, COL_B + gidx * LANES:COL_B + (gidx + 1) * LANES] = dx.astype(BF16)
            dqg_ref[...] += dg
        dak = _fold_heads(dka_ref[0] + dka_ref[1], dka_ref[2] + dka_ref[3], lane)
        dp_ref[:, 512:640] = _rope_bwd(dak, ca, sa, lane, HEAD_DIM).astype(BF16)
        dp_ref[:, 640:768] = _fold_heads(dva_ref[0] + dva_ref[1], dva_ref[2] + dva_ref[3], lane).astype(BF16)
        dbk = _fold_heads(dkb_ref[0] + dkb_ref[1], dkb_ref[2] + dkb_ref[3], lane)
        dbk = _rope_bwd(dbk, cb, sb, lane, HEAD_DIM // 2)
        dx, dg = norm_bwd(dbk, pb_ref[:, 512:640], kg_ref[...])
        dp_ref[:, COL_B + 512:COL_B + 640] = dx.astype(BF16)
        dkg_ref[...] += dg
        dp_ref[:, COL_B + 640:COL_B + 768] = _fold_heads(dvb_ref[0] + dvb_ref[1], dvb_ref[2] + dvb_ref[3],
                                                         lane).astype(BF16)

    row = lambda i: (i, 0)
    full = lambda i: (0, 0)
    tab = pl.BlockSpec((ts, LANES), row)
    q_spec = pl.BlockSpec((ts, 512), row)
    kv_spec = pl.BlockSpec((4, ts, LANES), lambda i: (0, i, 0))
    return pl.pallas_call(
        body, name=name,
        out_shape=(jax.ShapeDtypeStruct((S, 2 * QKV_W), BF16), jax.ShapeDtypeStruct((1, LANES), F32),
                   jax.ShapeDtypeStruct((1, LANES), F32)),
        grid=(S // ts,),
        in_specs=[q_spec, kv_spec, kv_spec, q_spec, kv_spec, kv_spec,
                  pl.BlockSpec((ts, QKV_W), lambda i: (i, 1)), tab, tab, tab, tab,
                  pl.BlockSpec((1, LANES), full), pl.BlockSpec((1, LANES), full), pl.BlockSpec((LANES, LANES), full)],
        out_specs=(pl.BlockSpec((ts, 2 * QKV_W), row), pl.BlockSpec((1, LANES), full),
                   pl.BlockSpec((1, LANES), full)),
        compiler_params=_cp(("arbitrary",)),
    )(dqa, dka, dva, dqb, dkb, dvb, proj, ca, sa, cb, sb, qg2, kg2, seg)


def _attn_dense_fwd(q, k4, v4, *, gather=None, name):
    S = q.shape[0]
    tq = _tile(S, 256)
    xs, gl = gather if gather is not None else ([], None)
    na = len(xs)

    def body(q_ref, k_ref, v_ref, *rest):
        o_ref, lse_ref = rest[na], rest[na + 1]
        if na:
            x_refs, out_refs, sems = rest[:na], rest[na + 2:2 * na + 2], rest[2 * na + 2:]
            start, finish = _gather_plan([(x_refs[a], gl, out_refs[a]) for a in range(na)], *sems)
            pl.when((pl.program_id(0) == 0) & (pl.program_id(1) == 0))(start)
        for pr in range(2):
            qp = q_ref[:, pr * LANES:(pr + 1) * LANES]
            acc = None
            for half in range(2):
                s = _dot(qp, k_ref[half], _NT)
                m = jnp.max(s, axis=-1, keepdims=True)
                e = jnp.exp(s - m)
                l = jnp.sum(e, axis=-1, keepdims=True)
                pv = _dot(e.astype(BF16), v_ref[half]) * (1.0 / l)
                acc = pv if acc is None else acc + pv
                lse_ref[pr * 2 + half] = m + jnp.log(l)
            o_ref[:, pr * LANES:(pr + 1) * LANES] = acc.astype(BF16)
        if na:
            pl.when((pl.program_id(0) == 1) & (pl.program_id(1) == S // tq - 1))(finish)

    kv_spec = pl.BlockSpec((2, S, LANES), lambda kv, i: (kv, 0, 0))
    res = pl.pallas_call(
        body, name=name,
        out_shape=(jax.ShapeDtypeStruct((S, 512), BF16), jax.ShapeDtypeStruct((8, S, 1), F32),
                   *[_gathered_shape(x) for x in xs]),
        grid=(2, S // tq),
        in_specs=[pl.BlockSpec((tq, 256), lambda kv, i: (i, kv)), kv_spec, kv_spec] + [_ANY] * na,
        out_specs=(pl.BlockSpec((tq, 256), lambda kv, i: (i, kv)),
                   pl.BlockSpec((4, tq, 1), lambda kv, i: (kv, i, 0)), *([_ANY] * na)),
        scratch_shapes=_comm_scratch(na) if na else [],
        compiler_params=_cp(("arbitrary", "arbitrary") if na else ("parallel", "parallel")),
    )(q, k4, v4, *xs)
    return res[0], res[1], list(res[2:])


def _attn_dense_bwd(q, k4, v4, lse, do, *, scatter=None, name):
    S = q.shape[0]
    tq = _tile(S, 256)
    sends, recvs, lr = scatter if scatter is not None else ([], [], None)
    na = len(sends)
    comm_in, comm_out, held = _scatter_io(sends, recvs)
    n_in = len(comm_in)

    def body(q_ref, k_ref, v_ref, lse_ref, do_ref, *rest):
        dq_ref, dk_ref, dv_ref = rest[n_in:n_in + 3]
        if na:
            s_refs, r_refs, sems = rest[:na], rest[n_in + 3:n_in + 3 + na], rest[n_in + 3 + na:]
            start, finish = _scatter_plan([(s_refs[a], sends[a][1], r_refs[a], lr) for a in range(na)], *sems)
            pl.when((pl.program_id(0) == 0) & (pl.program_id(1) == 0))(start)

        @pl.when(pl.program_id(1) == 0)
        def _():
            dk_ref[...] = jnp.zeros_like(dk_ref)
            dv_ref[...] = jnp.zeros_like(dv_ref)

        lane = lax.broadcasted_iota(jnp.int32, (tq, LANES), 1)
        for pr in range(2):
            qp = q_ref[:, pr * LANES:(pr + 1) * LANES]
            dop = do_ref[:, pr * LANES:(pr + 1) * LANES].astype(BF16)
            dq = None
            for half in range(2):
                mine = (lane < HEAD_DIM) if half == 0 else (lane >= HEAD_DIM)
                s = _dot(qp, k_ref[half], _NT)
                p = jnp.exp(s - lse_ref[pr * 2 + half])
                dp = _dot(dop, v_ref[half], _NT)
                delta = jnp.sum(p * dp, axis=-1, keepdims=True)
                ds = (p * (dp - delta)).astype(BF16)
                pb = p.astype(BF16)
                d = _dot(ds, k_ref[half])
                dq = d if dq is None else dq + d
                dk_ref[half] += _dot(ds, jnp.where(mine, qp, jnp.zeros_like(qp)), _TN)
                dv_ref[half] += _dot(pb, jnp.where(mine, dop, jnp.zeros_like(dop)), _TN)
            dq_ref[:, pr * LANES:(pr + 1) * LANES] = dq
        if na:
            pl.when((pl.program_id(0) == 1) & (pl.program_id(1) == S // tq - 1))(finish)

    kv_spec = pl.BlockSpec((2, S, LANES), lambda kv, i: (kv, 0, 0))
    q_spec = pl.BlockSpec((tq, 256), lambda kv, i: (i, kv))
    res = pl.pallas_call(
        body, name=name,
        out_shape=(jax.ShapeDtypeStruct((S, 512), F32), jax.ShapeDtypeStruct((4, S, LANES), F32),
                   jax.ShapeDtypeStruct((4, S, LANES), F32), *comm_out),
        grid=(2, S // tq),
        in_specs=[q_spec, kv_spec, kv_spec, pl.BlockSpec((4, tq, 1), lambda kv, i: (kv, i, 0)), q_spec]
                 + [_ANY] * n_in,
        out_specs=(q_spec, kv_spec, kv_spec, *([_ANY] * na)),
        scratch_shapes=_comm_scratch(na) if na else [],
        input_output_aliases={5 + na + i: 3 + a for i, a in enumerate(held)},
        compiler_params=_cp(("arbitrary", "arbitrary") if na else ("parallel", "arbitrary")),
    )(q, k4, v4, lse, do, *comm_in)
    return res[0], res[1], res[2], list(res[3:])


WIN_KEYS = 3 * BLOCK


def _win_start(n, S):
    return pl.multiple_of(jnp.clip((n - 1) * BLOCK, 0, S - WIN_KEYS), BLOCK)


def _win_valid(n, start):
    qpos = n * BLOCK + lax.broadcasted_iota(jnp.int32, (BLOCK, WIN_KEYS), 0)
    kpos = start + lax.broadcasted_iota(jnp.int32, (BLOCK, WIN_KEYS), 1)
    return jnp.abs(qpos - kpos) <= WINDOW


def _attn_win_fwd(q, k4, v4, sink, *, name):
    S = q.shape[0]
    assert S >= WIN_KEYS

    def body(sink_ref, q_ref, k_ref, v_ref, o_ref, lse_ref):
        n = pl.program_id(0)
        start = _win_start(n, S)
        valid = _win_valid(n, start)
        for kv in range(2):
            for pr in range(2):
                cols = slice((kv * 2 + pr) * LANES, (kv * 2 + pr + 1) * LANES)
                qp = q_ref[:, cols]
                acc = None
                for half in range(2):
                    h = kv * 4 + pr * 2 + half
                    kk = k_ref[kv * 2 + half, pl.ds(start, WIN_KEYS), :]
                    vv = v_ref[kv * 2 + half, pl.ds(start, WIN_KEYS), :]
                    s = jnp.where(valid, _dot(qp, kk, _NT), NEG_BIG)
                    snk = sink_ref[h]
                    m = jnp.maximum(jnp.max(s, axis=-1, keepdims=True), snk)
                    e = jnp.exp(s - m)
                    l = jnp.sum(e, axis=-1, keepdims=True) + jnp.exp(snk - m)
                    p = (e * (1.0 / l)).astype(BF16)
                    pv = _dot(p, vv)
                    acc = pv if acc is None else acc + pv
                    lse_ref[h] = m + jnp.log(l)
                o_ref[:, cols] = acc.astype(BF16)

    kv_spec = pl.BlockSpec((4, S, LANES), lambda n: (0, 0, 0))
    return pl.pallas_call(
        body, name=name,
        out_shape=(jax.ShapeDtypeStruct((S, 512), BF16), jax.ShapeDtypeStruct((8, S, 1), F32)),
        grid=(S // BLOCK,),
        in_specs=[pl.BlockSpec(memory_space=pltpu.SMEM), pl.BlockSpec((BLOCK, 512), lambda n: (n, 0)),
                  kv_spec, kv_spec],
        out_specs=(pl.BlockSpec((BLOCK, 512), lambda n: (n, 0)), pl.BlockSpec((8, BLOCK, 1), lambda n: (0, n, 0))),
        compiler_params=_cp(("parallel",)),
    )(sink, q, k4, v4)


def _attn_win_bwd(q, k4, v4, sink, lse, do, *, name):
    S = q.shape[0]

    def body(sink_ref, q_ref, k_ref, v_ref, lse_ref, do_ref, dq_ref, dk_ref, dv_ref, dsink_ref):
        n = pl.program_id(0)

        @pl.when(n == 0)
        def _():
            dk_ref[...] = jnp.zeros_like(dk_ref)
            dv_ref[...] = jnp.zeros_like(dv_ref)
            dsink_ref[...] = jnp.zeros_like(dsink_ref)

        start = _win_start(n, S)
        valid = _win_valid(n, start)
        lane = lax.broadcasted_iota(jnp.int32, (BLOCK, LANES), 1)
        for kv in range(2):
            for pr in range(2):
                cols = slice((kv * 2 + pr) * LANES, (kv * 2 + pr + 1) * LANES)
                qp = q_ref[:, cols]
                dop = do_ref[:, cols].astype(BF16)
                dq = None
                for half in range(2):
                    h = kv * 4 + pr * 2 + half
                    slot = kv * 2 + half
                    mine = (lane < HEAD_DIM) if half == 0 else (lane >= HEAD_DIM)
                    win = pl.ds(start, WIN_KEYS)
                    kk = k_ref[slot, win, :]
                    vv = v_ref[slot, win, :]
                    lse_h = lse_ref[h]
                    s = jnp.where(valid, _dot(qp, kk, _NT), NEG_BIG)
                    p = jnp.exp(s - lse_h)
                    dp = _dot(dop, vv, _NT)
                    delta = jnp.sum(p * dp, axis=-1, keepdims=True)
                    ds = (p * (dp - delta)).astype(BF16)
                    pb = p.astype(BF16)
                    d = _dot(ds, kk)
                    dq = d if dq is None else dq + d
                    dk_ref[slot, win, :] += _dot(ds, jnp.where(mine, qp, jnp.zeros_like(qp)), _TN)
                    dv_ref[slot, win, :] += _dot(pb, jnp.where(mine, dop, jnp.zeros_like(dop)), _TN)
                    p_sink = jnp.exp(sink_ref[h] - lse_h)
                    dsink_ref[h:h + 1, :] += jnp.broadcast_to(-jnp.sum(p_sink * delta, axis=0, keepdims=True),
                                                              (1, LANES))
                dq_ref[:, cols] = dq

    kv_spec = pl.BlockSpec((4, S, LANES), lambda n: (0, 0, 0))
    q_spec = pl.BlockSpec((BLOCK, 512), lambda n: (n, 0))
    return pl.pallas_call(
        body, name=name,
        out_shape=(jax.ShapeDtypeStruct((S, 512), F32), jax.ShapeDtypeStruct((4, S, LANES), F32),
                   jax.ShapeDtypeStruct((4, S, LANES), F32), jax.ShapeDtypeStruct((8, LANES), F32)),
        grid=(S // BLOCK,),
        in_specs=[pl.BlockSpec(memory_space=pltpu.SMEM), q_spec, kv_spec, kv_spec,
                  pl.BlockSpec((8, BLOCK, 1), lambda n: (0, n, 0)), q_spec],
        out_specs=(q_spec, kv_spec, kv_spec, pl.BlockSpec((8, LANES), lambda n: (0, 0))),
        compiler_params=_cp(("arbitrary",)),
    )(sink, q, k4, v4, lse, do)


def _c_ln(v, g, b):
    mu = jnp.mean(v, axis=-1, keepdims=True)
    vc = v - mu
    r = lax.rsqrt(jnp.mean(vc * vc, axis=-1, keepdims=True) + LN_EPS)
    vh = vc * r
    return vh, r, vh * g + b


def _gmlp_fwd(proj, ws, bs3, lg, lb, *, name):
    S = proj.shape[0]

    def body(u_ref, v_ref, ws_ref, bs_ref, lg_ref, lb_ref, o_ref):
        u = _gelu(u_ref[...])
        _, _, vn = _c_ln(_gelu(v_ref[...]), lg_ref[...], lb_ref[...])
        vn = vn.astype(BF16)
        for gi in range(C_GROUPS):
            cols = slice(gi * LANES, (gi + 1) * LANES)
            mixed = _dot(ws_ref[gi], vn[:, cols]) + bs_ref[gi]
            o_ref[:, cols] = (u[:, cols] * mixed).astype(BF16)

    full2 = lambda n: (0, 0)
    full3 = lambda n: (0, 0, 0)
    return pl.pallas_call(
        body, name=name,
        out_shape=jax.ShapeDtypeStruct((S, C_WIDTH), BF16),
        grid=(S // CHUNK,),
        in_specs=[pl.BlockSpec((CHUNK, C_WIDTH), lambda n: (n, COL_C // C_WIDTH)),
                  pl.BlockSpec((CHUNK, C_WIDTH), lambda n: (n, COL_C // C_WIDTH + 1)),
                  pl.BlockSpec((C_GROUPS, CHUNK, CHUNK), full3), pl.BlockSpec((C_GROUPS, CHUNK, 1), full3),
                  pl.BlockSpec((1, C_WIDTH), full2), pl.BlockSpec((1, C_WIDTH), full2)],
        out_specs=pl.BlockSpec((CHUNK, C_WIDTH), lambda n: (n, 0)),
        compiler_params=_cp(("parallel",)),
    )(proj, proj, ws, bs3, lg, lb)


def _gmlp_bwd(proj, dout, ws, bs3, lg, lb, *, name):
    S = proj.shape[0]

    def body(u_ref, v_ref, d_ref, ws_ref, bs_ref, lg_ref, lb_ref, dz_ref, dws_ref, dbs_ref, dlg_ref, dlb_ref):
        @pl.when(pl.program_id(0) == 0)
        def _():
            dws_ref[...] = jnp.zeros_like(dws_ref)
            dbs_ref[...] = jnp.zeros_like(dbs_ref)
            dlg_ref[...] = jnp.zeros_like(dlg_ref)
            dlb_ref[...] = jnp.zeros_like(dlb_ref)

        u_pre, v_pre, d = u_ref[...], v_ref[...], d_ref[...]
        u, u_grad = _gelu_and_grad(u_pre)
        v, v_grad = _gelu_and_grad(v_pre)
        vh, r, vn = _c_ln(v, lg_ref[...], lb_ref[...])
        vnb = vn.astype(BF16)
        du_parts, dvn_parts = [], []
        for gi in range(C_GROUPS):
            cols = slice(gi * LANES, (gi + 1) * LANES)
            mixed = _dot(ws_ref[gi], vnb[:, cols]) + bs_ref[gi]
            du_parts.append(d[:, cols] * mixed)
            dm = d[:, cols] * u[:, cols]
            dbs_ref[gi] += jnp.sum(dm, axis=-1, keepdims=True)
            dmb = dm.astype(BF16)
            dws_ref[gi] += _dot(dmb, vnb[:, cols], _NT)
            dvn_parts.append(_dot(ws_ref[gi], dmb, _TN))
        du = jnp.concatenate(du_parts, axis=-1)
        dvn = jnp.concatenate(dvn_parts, axis=-1)
        dlg_ref[...] += jnp.sum(dvn * vh, axis=0, keepdims=True)
        dlb_ref[...] += jnp.sum(dvn, axis=0, keepdims=True)
        dvh = dvn * lg_ref[...]
        m1 = jnp.mean(dvh, axis=-1, keepdims=True)
        m2 = jnp.mean(dvh * vh, axis=-1, keepdims=True)
        dv = r * (dvh - m1 - vh * m2)
        dz_ref[:, :C_WIDTH] = (du * u_grad).astype(BF16)
        dz_ref[:, C_WIDTH:] = (dv * v_grad).astype(BF16)

    full2 = lambda n: (0, 0)
    full3 = lambda n: (0, 0, 0)
    return pl.pallas_call(
        body, name=name,
        out_shape=(jax.ShapeDtypeStruct((S, 2 * C_WIDTH), BF16), jax.ShapeDtypeStruct((C_GROUPS, CHUNK, CHUNK), F32),
                   jax.ShapeDtypeStruct((C_GROUPS, CHUNK, 1), F32), jax.ShapeDtypeStruct((1, C_WIDTH), F32),
                   jax.ShapeDtypeStruct((1, C_WIDTH), F32)),
        grid=(S // CHUNK,),
        in_specs=[pl.BlockSpec((CHUNK, C_WIDTH), lambda n: (n, COL_C // C_WIDTH)),
                  pl.BlockSpec((CHUNK, C_WIDTH), lambda n: (n, COL_C // C_WIDTH + 1)),
                  pl.BlockSpec((CHUNK, C_WIDTH), lambda n: (n, 0)),
                  pl.BlockSpec((C_GROUPS, CHUNK, CHUNK), full3), pl.BlockSpec((C_GROUPS, CHUNK, 1), full3),
                  pl.BlockSpec((1, C_WIDTH), full2), pl.BlockSpec((1, C_WIDTH), full2)],
        out_specs=(pl.BlockSpec((CHUNK, 2 * C_WIDTH), lambda n: (n, 0)), pl.BlockSpec((C_GROUPS, CHUNK, CHUNK), full3),
                   pl.BlockSpec((C_GROUPS, CHUNK, 1), full3), pl.BlockSpec((1, C_WIDTH), full2),
                   pl.BlockSpec((1, C_WIDTH), full2)),
        compiler_params=_cp(("arbitrary",)),
    )(proj, proj, dout, ws, bs3, lg, lb)


GATE_BLK = 512


def _gate_specs(tm, D):
    nh = D // GATE_BLK
    first = COL_GATE // GATE_BLK
    return [pl.BlockSpec((tm, GATE_BLK), functools.partial(lambda i, c: (i, c), c=first + b))
            for b in range(N_BRANCH * nh)]


def _merge_fwd(oa, ob, oc, wb, proj, bg, *, name):
    S = oa.shape[0]
    D = wb.shape[2]
    assert D % GATE_BLK == 0
    nh = D // GATE_BLK
    tm = _tile(S, 256)

    def body(oa_ref, ob_ref, oc_ref, wb_ref, *rest):
        gate_refs, bg_ref, o_ref = rest[:N_BRANCH * nh], rest[N_BRANCH * nh], rest[N_BRANCH * nh + 1]
        brs = (oa_ref[...], ob_ref[...], oc_ref[...])
        for j in range(nh):
            cols = slice(j * GATE_BLK, (j + 1) * GATE_BLK)
            acc = None
            for n in range(N_BRANCH):
                b = n * nh + j
                t = _dot(brs[n], wb_ref[n, :, cols])
                g = _sigmoid(gate_refs[b][...] + bg_ref[:, b * GATE_BLK:(b + 1) * GATE_BLK])
                acc = t * g if acc is None else acc + t * g
            o_ref[:, cols] = acc.astype(BF16)

    row = lambda i: (i, 0)
    br = pl.BlockSpec((tm, BRANCH_WIDTH), row)
    return pl.pallas_call(
        body, name=name,
        out_shape=jax.ShapeDtypeStruct((S, D), BF16),
        grid=(S // tm,),
        in_specs=[br, br, br, pl.BlockSpec((N_BRANCH, BRANCH_WIDTH, D), lambda i: (0, 0, 0))]
                 + _gate_specs(tm, D) + [pl.BlockSpec((1, N_BRANCH * D), lambda i: (0, 0))],
        out_specs=pl.BlockSpec((tm, D), row),
        compiler_params=_cp(("parallel",)),
    )(oa, ob, oc, wb, *([proj] * (N_BRANCH * nh)), bg)


def _merge_bwd(oa, ob, oc, wb, proj, bg, dmerged, *, name):
    S = oa.shape[0]
    D = wb.shape[2]
    nh = D // GATE_BLK
    tm = _tile(S, 256)

    def body(oa_ref, ob_ref, oc_ref, wb_ref, *rest):
        gate_refs = rest[:N_BRANCH * nh]
        bg_ref, dm_ref, dgl_ref, dt_ref, dbr_ref, dbg_ref = rest[N_BRANCH * nh:]

        @pl.when(pl.program_id(0) == 0)
        def _():
            dbg_ref[...] = jnp.zeros_like(dbg_ref)

        brs = (oa_ref[...], ob_ref[...], oc_ref[...])
        for n in range(N_BRANCH):
            dbr = None
            for j in range(nh):
                cols = slice(j * GATE_BLK, (j + 1) * GATE_BLK)
                b = n * nh + j
                gcols = slice(b * GATE_BLK, (b + 1) * GATE_BLK)
                w = wb_ref[n, :, cols]
                t = _dot(brs[n], w)
                g = _sigmoid(gate_refs[b][...] + bg_ref[:, gcols])
                dm = dm_ref[:, cols]
                dt = (dm * g).astype(BF16)
                dgl = dm * t * g * (1.0 - g)
                dt_ref[n, :, cols] = dt
                dgl_ref[:, gcols] = dgl.astype(BF16)
                dbg_ref[:, gcols] += jnp.sum(dgl, axis=0, keepdims=True)
                d = _dot(dt, w, _NT)
                dbr = d if dbr is None else dbr + d
            dbr_ref[n] = dbr

    row = lambda i: (i, 0)
    br = pl.BlockSpec((tm, BRANCH_WIDTH), row)
    return pl.pallas_call(
        body, name=name,
        out_shape=(jax.ShapeDtypeStruct((S, N_BRANCH * D), BF16), jax.ShapeDtypeStruct((N_BRANCH, S, D), BF16),
                   jax.ShapeDtypeStruct((N_BRANCH, S, BRANCH_WIDTH), F32), jax.ShapeDtypeStruct((1, N_BRANCH * D), F32)),
        grid=(S // tm,),
        in_specs=[br, br, br, pl.BlockSpec((N_BRANCH, BRANCH_WIDTH, D), lambda i: (0, 0, 0))]
                 + _gate_specs(tm, D)
                 + [pl.BlockSpec((1, N_BRANCH * D), lambda i: (0, 0)), pl.BlockSpec((tm, D), row)],
        out_specs=(pl.BlockSpec((tm, N_BRANCH * D), row), pl.BlockSpec((N_BRANCH, tm, D), lambda i: (0, i, 0)),
                   pl.BlockSpec((N_BRANCH, tm, BRANCH_WIDTH), lambda i: (0, i, 0)),
                   pl.BlockSpec((1, N_BRANCH * D), lambda i: (0, 0))),
        compiler_params=_cp(("arbitrary",)),
    )(oa, ob, oc, wb, *([proj] * (N_BRANCH * nh)), bg, dmerged)


X_SCALE = 1.0 / math.sqrt(X_HEAD_DIM)
X_W = X_HEADS * X_HEAD_DIM


def _xattn_fwd(q, kv, *, name):
    S = q.shape[0]
    M = kv.shape[0]
    tq = _tile(S, 512)

    def body(q_ref, kv_ref, o_ref, lse_ref):
        for h in range(X_HEADS):
            cols = slice(h * LANES, (h + 1) * LANES)
            s = _dot(q_ref[:, cols], kv_ref[:, cols], _NT) * X_SCALE
            m = jnp.max(s, axis=-1, keepdims=True)
            e = jnp.exp(s - m)
            l = jnp.sum(e, axis=-1, keepdims=True)
            p = (e * (1.0 / l)).astype(BF16)
            o_ref[:, cols] = _dot(p, kv_ref[:, X_W + h * LANES:X_W + (h + 1) * LANES]).astype(BF16)
            lse_ref[h] = m + jnp.log(l)

    return pl.pallas_call(
        body, name=name,
        out_shape=(jax.ShapeDtypeStruct((S, X_W), BF16), jax.ShapeDtypeStruct((X_HEADS, S, 1), F32)),
        grid=(S // tq,),
        in_specs=[pl.BlockSpec((tq, X_W), lambda i: (i, 0)), pl.BlockSpec((M, 2 * X_W), lambda i: (0, 0))],
        out_specs=(pl.BlockSpec((tq, X_W), lambda i: (i, 0)), pl.BlockSpec((X_HEADS, tq, 1), lambda i: (0, i, 0))),
        compiler_params=_cp(("parallel",)),
    )(q, kv)


def _xattn_bwd(q, kv, lse, do, *, name):
    S = q.shape[0]
    M = kv.shape[0]
    tq = _tile(S, 512)

    def body(q_ref, kv_ref, lse_ref, do_ref, dq_ref, dkv_ref):
        @pl.when(pl.program_id(0) == 0)
        def _():
            dkv_ref[...] = jnp.zeros_like(dkv_ref)

        for h in range(X_HEADS):
            cols = slice(h * LANES, (h + 1) * LANES)
            vcols = slice(X_W + h * LANES, X_W + (h + 1) * LANES)
            qh, kh, vh = q_ref[:, cols], kv_ref[:, cols], kv_ref[:, vcols]
            doh = do_ref[:, cols].astype(BF16)
            p = jnp.exp(_dot(qh, kh, _NT) * X_SCALE - lse_ref[h])
            dp = _dot(doh, vh, _NT)
            delta = jnp.sum(p * dp, axis=-1, keepdims=True)
            ds = (p * (dp - delta) * X_SCALE).astype(BF16)
            dq_ref[:, cols] = _dot(ds, kh).astype(BF16)
            dkv_ref[:, cols] += _dot(ds, qh, _TN)
            dkv_ref[:, vcols] += _dot(p.astype(BF16), doh, _TN)

    q_spec = pl.BlockSpec((tq, X_W), lambda i: (i, 0))
    return pl.pallas_call(
        body, name=name,
        out_shape=(jax.ShapeDtypeStruct((S, X_W), BF16), jax.ShapeDtypeStruct((M, 2 * X_W), F32)),
        grid=(S // tq,),
        in_specs=[q_spec, pl.BlockSpec((M, 2 * X_W), lambda i: (0, 0)),
                  pl.BlockSpec((X_HEADS, tq, 1), lambda i: (0, i, 0)), q_spec],
        out_specs=(q_spec, pl.BlockSpec((M, 2 * X_W), lambda i: (0, 0))),
        compiler_params=_cp(("arbitrary",)),
    )(q, kv, lse, do)


def _shift_down(h, row):
    return jnp.where(row == 0, 0.0, pltpu.roll(h, 1, 0))


def _shift_up(h, row, S):
    return jnp.where(row == S - 1, 0.0, pltpu.roll(h, S - 1, 0))


def _conv3(h, ck, cb, row, S):
    return _shift_down(h, row) * ck[0:1] + h * ck[1:2] + _shift_up(h, row, S) * ck[2:3] + cb


def _conv_act_fwd(h, ck, cb, *, name):
    S, F2 = h.shape
    F = F2 // 2
    nt = F // LANES

    def body(ha_ref, hb_ref, cka_ref, ckb_ref, cba_ref, cbb_ref, o_ref):
        row = lax.broadcasted_iota(jnp.int32, (S, LANES), 0)
        a = _conv3(ha_ref[...], cka_ref[...], cba_ref[...], row, S)
        b = _conv3(hb_ref[...], ckb_ref[...], cbb_ref[...], row, S)
        o_ref[...] = (_gelu(a) * b).astype(BF16)

    ca = lambda j: (0, j)
    cbi = lambda j: (0, j + nt)
    return pl.pallas_call(
        body, name=name,
        out_shape=jax.ShapeDtypeStruct((S, F), BF16),
        grid=(nt,),
        in_specs=[pl.BlockSpec((S, LANES), ca), pl.BlockSpec((S, LANES), cbi), pl.BlockSpec((3, LANES), ca),
                  pl.BlockSpec((3, LANES), cbi), pl.BlockSpec((1, LANES), ca), pl.BlockSpec((1, LANES), cbi)],
        out_specs=pl.BlockSpec((S, LANES), ca),
        compiler_params=_cp(("parallel",)),
    )(h, h, ck, ck, cb, cb)


def _conv_act_bwd(h, ck, cb, dact, *, name):
    S, F2 = h.shape
    F = F2 // 2
    nt = F // LANES

    def body(ha_ref, hb_ref, cka_ref, ckb_ref, cba_ref, cbb_ref, d_ref,
             dha_ref, dhb_ref, dcka_ref, dckb_ref, dcba_ref, dcbb_ref):
        row = lax.broadcasted_iota(jnp.int32, (S, LANES), 0)
        ha, hb = ha_ref[...], hb_ref[...]
        cka, ckb = cka_ref[...], ckb_ref[...]
        a = _conv3(ha, cka, cba_ref[...], row, S)
        b = _conv3(hb, ckb, cbb_ref[...], row, S)
        d = d_ref[...]
        ga, ga_grad = _gelu_and_grad(a)
        da = d * b * ga_grad
        db = d * ga
        for dd, hh, ck_, dh_ref, dck_ref, dcb_ref in ((da, ha, cka, dha_ref, dcka_ref, dcba_ref),
                                                      (db, hb, ckb, dhb_ref, dckb_ref, dcbb_ref)):
            dcb_ref[...] = jnp.sum(dd, axis=0, keepdims=True)
            dck_ref[0:1, :] = jnp.sum(dd * _shift_down(hh, row), axis=0, keepdims=True)
            dck_ref[1:2, :] = jnp.sum(dd * hh, axis=0, keepdims=True)
            dck_ref[2:3, :] = jnp.sum(dd * _shift_up(hh, row, S), axis=0, keepdims=True)
            dh = _shift_up(dd, row, S) * ck_[0:1] + dd * ck_[1:2] + _shift_down(dd, row) * ck_[2:3]
            dh_ref[...] = dh.astype(BF16)

    ca = lambda j: (0, j)
    cbi = lambda j: (0, j + nt)
    col = pl.BlockSpec((S, LANES), ca)
    return pl.pallas_call(
        body, name=name,
        out_shape=(jax.ShapeDtypeStruct((S, F), BF16), jax.ShapeDtypeStruct((S, F), BF16),
                   jax.ShapeDtypeStruct((3, F), F32), jax.ShapeDtypeStruct((3, F), F32),
                   jax.ShapeDtypeStruct((1, F), F32), jax.ShapeDtypeStruct((1, F), F32)),
        grid=(nt,),
        in_specs=[col, pl.BlockSpec((S, LANES), cbi), pl.BlockSpec((3, LANES), ca), pl.BlockSpec((3, LANES), cbi),
                  pl.BlockSpec((1, LANES), ca), pl.BlockSpec((1, LANES), cbi), col],
        out_specs=(col, col, pl.BlockSpec((3, LANES), ca), pl.BlockSpec((3, LANES), ca),
                   pl.BlockSpec((1, LANES), ca), pl.BlockSpec((1, LANES), ca)),
        compiler_params=_cp(("parallel",)),
    )(h, h, ck, ck, cb, cb, dact)


def _layer_fwd(x, xb, memb, w, tabs, seg, l, gather):
    n = lambda s: f"L{l}_{s}"
    qg2 = jnp.tile(w["b_q_gain"], 2)[None, :]
    kg2 = jnp.tile(w["b_k_gain"], 2)[None, :]
    proj = _mm(xb, w["w_in"], tb=True, name=n("proj"))
    aq, ak4, av4, bq, bk4, bv4 = _prep(proj, tabs, qg2, kg2, seg, name=n("prep"))
    oa, lse_a = _attn_win_fwd(aq, ak4, av4, w["a_sink"], name=n("attn_win"))
    ob, lse_b, gathered = _attn_dense_fwd(bq, bk4, bv4, gather=gather, name=n("attn_dense"))
    oc = _gmlp_fwd(proj, w["c_ws"], w["c_bs3"], w["c_ln_g"], w["c_ln_b"], name=n("gmlp"))
    merged = _merge_fwd(oa, ob, oc, w["w_branch"], proj, w["b_gate"], name=n("merge"))
    x1, x1b, xh1, rs1 = _mm_res_ln(merged, w["w_mix_out"], x, w["ln1_g"], w["ln1_b"], name=n("mix_ln1"))
    xq = _mm(x1b, w["x_wq"], out_dtype=BF16, name=n("xq"))
    xkv = _mm(memb, w["x_wkv"], out_dtype=BF16, name=n("xkv"))
    xo, lse_x = _xattn_fwd(xq, xkv, name=n("xattn"))
    x2, x2b, xh2, rs2 = _mm_res_ln(xo, w["x_wo"], x1, w["ln2_g"], w["ln2_b"], name=n("xo_ln2"))
    h = _mm(x2b, w["f_w_up"], tb=True, name=n("ffn_up"))
    act = _conv_act_fwd(h, w["f_conv_k"], w["f_conv_b"], name=n("conv_act"))
    x3, x3b, xh3, rs3 = _mm_res_ln(act, w["f_w_down"], x2, w["ln3_g"], w["ln3_b"], name=n("down_ln3"))
    saved = dict(xb=xb, proj=proj, aq=aq, ak4=ak4, av4=av4, bq=bq, bk4=bk4, bv4=bv4, lse_a=lse_a, lse_b=lse_b,
                 oa=oa, ob=ob, oc=oc, merged=merged, xh1=xh1, rs1=rs1, x1b=x1b, xq=xq, xkv=xkv, xo=xo, lse_x=lse_x,
                 xh2=xh2, rs2=rs2, x2b=x2b, h=h, act=act, xh3=xh3, rs3=rs3, qg2=qg2, kg2=kg2)
    return x3, x3b, saved, gathered


def _layer_bwd(dy, memb, w, sv, tabs, seg, l, scatter):
    n = lambda s: f"L{l}_{s}"
    g = {}
    big = {}

    def dw(key, a, b, tag):
        big[key] = _mm(a, b, ta=True, out_dtype=BF16, name=n(tag))

    def dw_t(key, segments, x, tag):
        buf = jnp.zeros((sum(s.shape[1] for s in segments), x.shape[1]), BF16)
        row = 0
        for i, s in enumerate(segments):
            buf = _mm(s, x, ta=True, into=(buf, row), name=n(f"{tag}{i}"))
            row += s.shape[1]
        big[key] = buf

    dz3, dz3b, g["ln3_g"], g["ln3_b"] = _ln_bwd(dy, sv["xh3"], sv["rs3"], w["ln3_g"], name=n("ln3_bwd"))
    dw("f_w_down", sv["act"], dz3b, "dw_down")
    dact = _mm(dz3b, w["f_w_down"], tb=True, name=n("dact"))
    dha, dhb, dcka, dckb, dcba, dcbb = _conv_act_bwd(sv["h"], w["f_conv_k"], w["f_conv_b"], dact, name=n("conv_act_bwd"))
    g["f_conv_k"] = jnp.concatenate([dcka, dckb], axis=1)
    g["f_conv_b"] = jnp.concatenate([dcba, dcbb], axis=1)[0]
    dw_t("f_w_up", [dha, dhb], sv["x2b"], "dw_up")
    dx2 = _mm([dha, dhb], w["f_w_up"], res=dz3, res_scale=ALPHA, name=n("dx2"))
    dz2, dz2b, g["ln2_g"], g["ln2_b"] = _ln_bwd(dx2, sv["xh2"], sv["rs2"], w["ln2_g"], name=n("ln2_bwd"))
    dw("x_wo", sv["xo"], dz2b, "dw_xo")
    dxo = _mm(dz2b, w["x_wo"], tb=True, out_dtype=BF16, name=n("dxo"))
    dxq, dxkv = _xattn_bwd(sv["xq"], sv["xkv"], sv["lse_x"], dxo, name=n("xattn_bwd"))
    dw("x_wq", sv["x1b"], dxq, "dw_xq")
    dw("x_wkv", memb, dxkv, "dw_xkv")
    dx1 = _mm(dxq, w["x_wq"], tb=True, res=dz2, res_scale=ALPHA, name=n("dx1"))
    dz1, dz1b, g["ln1_g"], g["ln1_b"] = _ln_bwd(dx1, sv["xh1"], sv["rs1"], w["ln1_g"], name=n("ln1_bwd"))
    dw("w_mix_out", sv["merged"], dz1b, "dw_mix")
    dmerged = _mm(dz1b, w["w_mix_out"], tb=True, name=n("dmerged"))
    dgl, dt, dbr, dbg = _merge_bwd(sv["oa"], sv["ob"], sv["oc"], w["w_branch"], sv["proj"], w["b_gate"], dmerged,
                                   name=n("merge_bwd"))
    g["b_gate"] = dbg[0]
    for i, k in enumerate(("oa", "ob", "oc")):
        dw(f"w_branch{i}", sv[k], dt[i], f"dw_branch{i}")
    big["w_branch"] = jnp.stack([big.pop(f"w_branch{i}") for i in range(N_BRANCH)])
    dqa, dka, dva, dsink = _attn_win_bwd(sv["aq"], sv["ak4"], sv["av4"], w["a_sink"], sv["lse_a"], dbr[0],
                                         name=n("attn_win_bwd"))
    g["a_sink"] = dsink[:, 0]
    dqb, dkb, dvb, recvs = _attn_dense_bwd(sv["bq"], sv["bk4"], sv["bv4"], sv["lse_b"], dbr[1], scatter=scatter,
                                           name=n("attn_dense_bwd"))
    dcz, g["c_ws"], dbs3, dlg, dlb = _gmlp_bwd(sv["proj"], dbr[2], w["c_ws"], w["c_bs3"], w["c_ln_g"], w["c_ln_b"],
                                               name=n("gmlp_bwd"))
    g["c_bs"] = dbs3[:, :, 0]
    g["c_ln_g"], g["c_ln_b"] = dlg[0], dlb[0]
    dqkv, dqg, dkg = _unprep(dqa, dka, dva, dqb, dkb, dvb, sv["proj"], tabs, sv["qg2"], sv["kg2"], seg, name=n("unprep"))
    g["b_q_gain"] = dqg[0, :HEAD_DIM] + dqg[0, HEAD_DIM:]
    g["b_k_gain"] = dkg[0, :HEAD_DIM] + dkg[0, HEAD_DIM:]
    dw_t("w_in", [dqkv, dcz, dgl], sv["xb"], "dw_in")
    dx0 = _mm([dqkv, dcz, dgl], w["w_in"], res=dz1, res_scale=ALPHA, name=n("dx0"))
    for k in ("ln1_g", "ln1_b", "ln2_g", "ln2_b", "ln3_g", "ln3_b"):
        g[k] = g[k][0]
    return dx0, g, big, recvs


WEIGHTS = ("w_in", "b_gate", "a_sink", "b_q_gain", "b_k_gain", "c_ln_g", "c_ln_b", "c_ws", "c_bs", "w_branch",
           "w_mix_out", "ln1_g", "ln1_b", "x_wq", "x_wkv", "x_wo", "ln2_g", "ln2_b", "f_w_up", "f_conv_k",
           "f_conv_b", "f_w_down", "ln3_g", "ln3_b")
TRANSPOSED = ("w_in", "f_w_up")
BIG_AXIS = {"w_in": 0, "w_branch": 2, "w_mix_out": 0, "x_wq": 0, "x_wkv": 0, "x_wo": 1, "f_w_up": 0, "f_w_down": 0}
BIG = tuple(BIG_AXIS)
GATHERED = BIG + ("f_conv_k",)
GATHER_AXIS = dict(BIG_AXIS, f_conv_k=1)
SMALL = tuple(k for k in WEIGHTS if k not in GATHERED)


def _unshard(g, axis):
    t = jnp.moveaxis(g, 0, axis)
    return t.reshape(t.shape[:axis] + (t.shape[axis] * t.shape[axis + 1],) + t.shape[axis + 2:])


def _reshard(full, axis):
    t = full.reshape(full.shape[:axis] + (N_DEV, full.shape[axis] // N_DEV) + full.shape[axis + 1:])
    return jnp.moveaxis(t, axis, 0)


def _layer_weights(small, gathered, l):
    w = {k: v[l] for k, v in small.items()}
    for k in ("c_ln_g", "c_ln_b", "ln1_g", "ln1_b", "ln2_g", "ln2_b", "ln3_g", "ln3_b", "b_gate", "f_conv_b"):
        w[k] = w[k][None, :]
    w["c_bs3"] = w["c_bs"][:, :, None]
    w["c_ws"] = w["c_ws"].astype(BF16)
    for k, g in zip(GATHERED, gathered):
        w[k] = _unshard(g, GATHER_AXIS[k])
    return w


def _local_step(x, mem, target, small, shards):
    S = x.shape[0]
    tabs = _rope_tables(S)
    seg = _seg_matrix()
    memb = mem.astype(BF16)
    xb = x.astype(BF16)
    saved, weights = [], []
    gathered = _gather_call(shards, 0, name="gather_L0")
    for l in range(DEPTH):
        w = _layer_weights(small, gathered, l)
        gather = (shards, l + 1) if l + 1 < DEPTH else None
        x, xb, sv, gathered = _layer_fwd(x, xb, memb, w, tabs, seg, l, gather)
        saved.append(sv)
        weights.append(w)
    dy, loss = _loss_head(x, target, name="loss_head")
    grads = [None] * DEPTH
    recvs = [jax.ShapeDtypeStruct((DEPTH, N_DEV) + s.shape[1:], BF16) for s in shards[:len(BIG)]]
    sends = None
    for l in reversed(range(DEPTH)):
        scatter = (sends, recvs, l + 1) if sends is not None else None
        dy, grads[l], big, got = _layer_bwd(dy, memb, weights[l], saved[l], tabs, seg, l, scatter)
        if scatter is not None:
            recvs = got
        sends = [(_reshard(big[k], BIG_AXIS[k]), None) for k in BIG]
    recvs = _scatter_call(sends, recvs, 0, name="scatter_L0")
    return loss, dy, grads, recvs


PACK_W = 1024


def _gather_call(xs, l, *, name):
    na = len(xs)

    def body(*refs):
        start, finish = _gather_plan([(refs[a], l, refs[na + a]) for a in range(na)], *refs[2 * na:])
        start()
        finish()

    return list(pl.pallas_call(
        body, name=name,
        out_shape=[_gathered_shape(x) for x in xs],
        in_specs=[_ANY] * na, out_specs=[_ANY] * na,
        scratch_shapes=_comm_scratch(na),
    )(*xs))


def _scatter_io(sends, recvs):
    na = len(sends)
    held = [a for a in range(na) if not isinstance(recvs[a], jax.ShapeDtypeStruct)]
    return ([s for s, _ in sends] + [recvs[a] for a in held],
            [jax.ShapeDtypeStruct(r.shape, r.dtype) for r in recvs], held)


def _scatter_call(sends, recvs, lr, *, name):
    na = len(sends)
    operands, out_shape, held = _scatter_io(sends, recvs)
    n_in = len(operands)

    def body(*refs):
        start, finish = _scatter_plan([(refs[a], sends[a][1], refs[n_in + a], lr) for a in range(na)],
                                      *refs[n_in + na:])
        start()
        finish()

    return list(pl.pallas_call(
        body, name=name,
        out_shape=out_shape,
        in_specs=[_ANY] * n_in, out_specs=[_ANY] * na,
        scratch_shapes=_comm_scratch(na),
        input_output_aliases={na + i: a for i, a in enumerate(held)},
    )(*operands))


def _sum_parts(parts, *, name):
    P, R, C = parts.shape
    tr = _tile(R, 64, align=8)

    def body(p_ref, o_ref):
        g = p_ref[0].astype(F32)
        for s in range(1, P):
            g = g + p_ref[s].astype(F32)
        o_ref[...] = g

    return pl.pallas_call(
        body, name=name, out_shape=jax.ShapeDtypeStruct((R, C), F32), grid=(R // tr,),
        in_specs=[pl.BlockSpec((P, tr, C), lambda i: (0, i, 0))], out_specs=pl.BlockSpec((tr, C), lambda i: (i, 0)),
        compiler_params=_cp(("parallel",)),
    )(parts)


ADAM_BLOCK_ELEMS = 512 * 1024


def _adamw(parts, w, m, v, *, name):
    L, P, R, C = parts.shape
    assert w.shape == (L, R, C), (parts.shape, w.shape)
    tr = _tile(R, max(16, ADAM_BLOCK_ELEMS // C), align=16)

    def body(p_ref, w_ref, m_ref, v_ref, g_ref, d_ref, nm_ref, nv_ref):
        g = p_ref[0].astype(F32)
        for s in range(1, P):
            g = g + p_ref[s].astype(F32)
        nm = ADAM_B1 * m_ref[...] + (1.0 - ADAM_B1) * g
        nv = ADAM_B2 * v_ref[...] + (1.0 - ADAM_B2) * (g * g)
        m_hat = nm / (1.0 - ADAM_B1 ** ADAM_STEP)
        v_hat = nv / (1.0 - ADAM_B2 ** ADAM_STEP)
        g_ref[...] = g
        d_ref[...] = -ADAM_LR * (m_hat / (jnp.sqrt(v_hat) + ADAM_EPS) + ADAM_WD * w_ref[...])
        nm_ref[...] = nm
        nv_ref[...] = nv

    blk = pl.BlockSpec((None, tr, C), lambda l, i: (l, i, 0))
    shp = jax.ShapeDtypeStruct((L, R, C), F32)
    return pl.pallas_call(
        body, name=name, out_shape=(shp, shp, shp, shp), grid=(L, R // tr),
        in_specs=[pl.BlockSpec((None, P, tr, C), lambda l, i: (l, 0, i, 0)), blk, blk, blk],
        out_specs=(blk, blk, blk, blk),
        compiler_params=_cp(("parallel", "parallel")),
    )(parts, w, m, v)


def _pad_rows(vec, width, row_align):
    n = vec.shape[0]
    rows = -(-n // width)
    rows = -(-rows // row_align) * row_align
    return jnp.pad(vec, (0, rows * width - n)).reshape(rows, width)


def kernel(x, mem, w_in, b_gate, a_sink, b_q_gain, b_k_gain, c_ln_g, c_ln_b, c_ws, c_bs, w_branch, w_mix_out, ln1_g, ln1_b, x_wq, x_wkv, x_wo, ln2_g, ln2_b, f_w_up, f_conv_k, f_conv_b, f_w_down, ln3_g, ln3_b, loss_target, m_w_in, m_b_gate, m_a_sink, m_b_q_gain, m_b_k_gain, m_c_ln_g, m_c_ln_b, m_c_ws, m_c_bs, m_w_branch, m_w_mix_out, m_ln1_g, m_ln1_b, m_x_wq, m_x_wkv, m_x_wo, m_ln2_g, m_ln2_b, m_f_w_up, m_f_conv_k, m_f_conv_b, m_f_w_down, m_ln3_g, m_ln3_b, v_w_in, v_b_gate, v_a_sink, v_b_q_gain, v_b_k_gain, v_c_ln_g, v_c_ln_b, v_c_ws, v_c_bs, v_w_branch, v_w_mix_out, v_ln1_g, v_ln1_b, v_x_wq, v_x_wkv, v_x_wo, v_ln2_g, v_ln2_b, v_f_w_up, v_f_conv_k, v_f_conv_b, v_f_w_down, v_ln3_g, v_ln3_b):
    w = dict(w_in=w_in, b_gate=b_gate, a_sink=a_sink, b_q_gain=b_q_gain, b_k_gain=b_k_gain, c_ln_g=c_ln_g,
             c_ln_b=c_ln_b, c_ws=c_ws, c_bs=c_bs, w_branch=w_branch, w_mix_out=w_mix_out, ln1_g=ln1_g, ln1_b=ln1_b,
             x_wq=x_wq, x_wkv=x_wkv, x_wo=x_wo, ln2_g=ln2_g, ln2_b=ln2_b, f_w_up=f_w_up, f_conv_k=f_conv_k,
             f_conv_b=f_conv_b, f_w_down=f_w_down, ln3_g=ln3_g, ln3_b=ln3_b)
    m = dict(w_in=m_w_in, b_gate=m_b_gate, a_sink=m_a_sink, b_q_gain=m_b_q_gain, b_k_gain=m_b_k_gain,
             c_ln_g=m_c_ln_g, c_ln_b=m_c_ln_b, c_ws=m_c_ws, c_bs=m_c_bs, w_branch=m_w_branch, w_mix_out=m_w_mix_out,
             ln1_g=m_ln1_g, ln1_b=m_ln1_b, x_wq=m_x_wq, x_wkv=m_x_wkv, x_wo=m_x_wo, ln2_g=m_ln2_g, ln2_b=m_ln2_b,
             f_w_up=m_f_w_up, f_conv_k=m_f_conv_k, f_conv_b=m_f_conv_b, f_w_down=m_f_w_down, ln3_g=m_ln3_g,
             ln3_b=m_ln3_b)
    v = dict(w_in=v_w_in, b_gate=v_b_gate, a_sink=v_a_sink, b_q_gain=v_b_q_gain, b_k_gain=v_b_k_gain,
             c_ln_g=v_c_ln_g, c_ln_b=v_c_ln_b, c_ws=v_c_ws, c_bs=v_c_bs, w_branch=v_w_branch, w_mix_out=v_w_mix_out,
             ln1_g=v_ln1_g, ln1_b=v_ln1_b, x_wq=v_x_wq, x_wkv=v_x_wkv, x_wo=v_x_wo, ln2_g=v_ln2_g, ln2_b=v_ln2_b,
             f_w_up=v_f_w_up, f_conv_k=v_f_conv_k, f_conv_b=v_f_conv_b, f_w_down=v_f_w_down, ln3_g=v_ln3_g,
             ln3_b=v_ln3_b)
    me = 4 * lax.axis_index("x") + 2 * lax.axis_index("y") + lax.axis_index("c")

    def held(k, t):
        return jnp.swapaxes(t, 1, 2) if k in TRANSPOSED else t

    shards = [held(k, w[k]).astype(BF16) for k in BIG] + [w["f_conv_k"]]
    loss, grad_x, grads, recvs = _local_step(x[0], mem[0], loss_target[0], {k: w[k] for k in SMALL}, shards)
    loss = lax.psum(loss[0, 0], ("x", "y", "c"))

    out_g, out_d, out_m, out_v = {}, {}, {}, {}
    for k, recv in zip(BIG, recvs):
        shp = held(k, w[k]).shape
        rc = (DEPTH, math.prod(shp[1:-1]), shp[-1])
        parts = recv.reshape((DEPTH, N_DEV) + rc[1:])
        g_, d_, m_, v_ = _adamw(parts, held(k, w[k]).reshape(rc), held(k, m[k]).reshape(rc),
                                held(k, v[k]).reshape(rc), name=f"adamw_{k}")
        out_g[k], out_d[k], out_m[k], out_v[k] = (held(k, t.reshape(shp)) for t in (g_, d_, m_, v_))

    small_all = SMALL + ("f_conv_k",)
    gfull = {k: jnp.stack([grads[l][k] for l in range(DEPTH)]) for k in small_all}

    def pack(d):
        rows = jnp.concatenate([_pad_rows(d[k].reshape(-1), PACK_W, 1) for k in small_all])
        return jnp.pad(rows, ((0, -rows.shape[0] % 8), (0, 0)))

    def unpack(rows, like):
        out, r = {}, 0
        for k in small_all:
            nr = -(-like[k].size // PACK_W)
            out[k] = rows[r:r + nr].reshape(-1)[:like[k].size].reshape(like[k].shape)
            r += nr
        return out

    gathered = _gather_call([pack(gfull)[None]], 0, name="gather_small_grads")[0]
    sg = unpack(_sum_parts(gathered, name="sum_small_grads"), gfull)
    width = w["f_conv_k"].shape[2]
    sg["f_conv_k"] = lax.dynamic_slice_in_dim(sg["f_conv_k"], me * width, width, axis=2)
    g_, d_, m_, v_ = _adamw(pack(sg)[None, None], pack(w)[None], pack(m)[None], pack(v)[None], name="adamw_small")
    ud, um, uv = (unpack(t[0], w) for t in (d_, m_, v_))
    for k in small_all:
        out_g[k], out_d[k], out_m[k], out_v[k] = sg[k], ud[k], um[k], uv[k]

    return (loss, grad_x[None], *[out_g[k] for k in WEIGHTS], *[out_d[k] for k in WEIGHTS],
            *[out_m[k] for k in WEIGHTS], *[out_v[k] for k in WEIGHTS])
```

```python
import functools
import math

import jax
import jax.numpy as jnp
from jax import lax
from jax.experimental import pallas as pl
from jax.experimental.pallas import tpu as pltpu

F32 = jnp.float32
BF16 = jnp.bfloat16

DEPTH = 4
HEAD_DIM = 64
BLOCK = 128
WINDOW = 128
GRID_W = 64
C_WIDTH = 512
C_GROUPS = 4
CHUNK = 128
N_BRANCH = 3
BRANCH_WIDTH = 512
ROPE_THETA = 10000.0
X_HEADS = 4
X_HEAD_DIM = 128
ALPHA = (2 * DEPTH) ** 0.25
LN_EPS = 1e-5
RMS_EPS = 1e-6
ADAM_LR = 0.001
ADAM_B1 = 0.9
ADAM_B2 = 0.999
ADAM_EPS = 1e-08
ADAM_WD = 0.01
ADAM_STEP = 10
N_DEV = 8

COL_A = 0
COL_B = 768
COL_C = 1536
COL_GATE = 2560
QKV_W = 768

LANES = 128
V7X_VMEM_BYTES = 64 * 1024 * 1024
VMEM_LIMIT = V7X_VMEM_BYTES - 8 * 1024 * 1024
NEG_BIG = -1e30

_NT = (((1,), (1,)), ((), ()))
_TN = (((0,), (0,)), ((), ()))
_NN = (((1,), (0,)), ((), ()))


def _cp(sem=None):
    return pltpu.CompilerParams(dimension_semantics=sem, vmem_limit_bytes=VMEM_LIMIT)


def _tile(n, target, align=LANES):
    if n <= target:
        return n
    best = None
    for t in range(align, target + 1, align):
        if n % t == 0:
            best = t
    assert best is not None, (n, target)
    return best


def _dot(a, b, dims=_NN):
    return lax.dot_general(a, b, dims, preferred_element_type=F32)


def _gelu(x):
    return 0.5 * x * (1.0 + lax.erf(x * 0.7071067811865476))


def _gelu_and_grad(x):
    cdf = 0.5 * (1.0 + lax.erf(x * 0.7071067811865476))
    return x * cdf, cdf + x * jnp.exp(-0.5 * x * x) * 0.3989422804014327


def _sigmoid(x):
    return 1.0 / (1.0 + jnp.exp(-x))


MESH_ID = pl.DeviceIdType.MESH
_ANY = pl.BlockSpec(memory_space=pl.ANY)
COPIES_PER_ARRAY = N_DEV - 1


def _comm_scratch(n_arrays):
    return [pltpu.SemaphoreType.DMA((COPIES_PER_ARRAY * n_arrays,)),
            pltpu.SemaphoreType.DMA((COPIES_PER_ARRAY * n_arrays,)), pltpu.SemaphoreType.DMA((n_arrays,))]


def _gathered_shape(x):
    return jax.ShapeDtypeStruct((N_DEV,) + x.shape[1:], x.dtype)


def _gather_plan(entries, send_sems, recv_sems, local_sems):
    mx, my, mc = lax.axis_index("x"), lax.axis_index("y"), lax.axis_index("c")
    me, sibling = (mx, my, mc), (mx, my, 1 - mc)
    chips = [(1 - mx, my), (mx, 1 - my), (1 - mx, 1 - my)]

    def copy(a, k, block, to, from_shard=False):
        x_ref, l, out_ref = entries[a]
        dst = out_ref.at[4 * block[0] + 2 * block[1] + block[2]]
        return pltpu.make_async_remote_copy(
            src_ref=x_ref.at[l] if from_shard else dst, dst_ref=dst,
            send_sem=send_sems.at[COPIES_PER_ARRAY * a + k], recv_sem=recv_sems.at[COPIES_PER_ARRAY * a + k],
            device_id=to, device_id_type=MESH_ID)

    def own(a):
        x_ref, l, out_ref = entries[a]
        return pltpu.make_async_copy(x_ref.at[l], out_ref.at[4 * mx + 2 * my + mc], local_sems.at[a])

    def first(a):
        return [copy(a, 0, me, sibling, True)] + [copy(a, 1 + j, me, (*chip, mc), True) for j, chip in enumerate(chips)]

    def passed(a):
        return [copy(a, 4 + j, (*chip, mc), sibling) for j, chip in enumerate(chips)]

    def start():
        for a in range(len(entries)):
            own(a).start()
            for cp in first(a):
                cp.start()

    def finish():
        for a in range(len(entries)):
            fwd = passed(a)
            for j, chip in enumerate(chips):
                copy(a, 1 + j, (*chip, mc), me).wait_recv()
                fwd[j].start()
        for a in range(len(entries)):
            copy(a, 0, sibling, me).wait_recv()
            for j, chip in enumerate(chips):
                copy(a, 4 + j, (*chip, 1 - mc), me).wait_recv()
            for cp in first(a) + passed(a):
                cp.wait_send()
            own(a).wait()

    return start, finish


def _scatter_plan(entries, send_sems, recv_sems, local_sems):
    mx, my, mc = lax.axis_index("x"), lax.axis_index("y"), lax.axis_index("c")
    me = 4 * mx + 2 * my + mc

    def src(a, dev):
        return entries[a][0].at[dev]

    def copies(a):
        _, recv_ref, lr = entries[a]
        out = []
        for k in range(1, N_DEV):
            px = 1 - mx if k & 4 else mx
            py = 1 - my if k & 2 else my
            pc = 1 - mc if k & 1 else mc
            peer = 4 * px + 2 * py + pc
            sems = dict(send_sem=send_sems.at[COPIES_PER_ARRAY * a + k - 1],
                        recv_sem=recv_sems.at[COPIES_PER_ARRAY * a + k - 1],
                        device_id=(px, py, pc), device_id_type=MESH_ID)
            sends = pltpu.make_async_remote_copy(src_ref=src(a, peer), dst_ref=recv_ref.at[lr, me], **sems)
            lands = pltpu.make_async_remote_copy(src_ref=src(a, me), dst_ref=recv_ref.at[lr, peer], **sems)
            out.append((sends, lands))
        return out

    def own(a):
        _, recv_ref, lr = entries[a]
        return pltpu.make_async_copy(src(a, me), recv_ref.at[lr, me], local_sems.at[a])

    def start():
        for a in range(len(entries)):
            own(a).start()
            for sends, _ in copies(a):
                sends.start()

    def finish():
        for a in range(len(entries)):
            for _, lands in copies(a):
                lands.wait_recv()
        for a in range(len(entries)):
            for sends, _ in copies(a):
                sends.wait_send()
            own(a).wait()

    return start, finish


MM_TM, MM_TN, MM_TK = 1024, 1536, 2048


def _mm(a, b, *, ta=False, tb=False, out_dtype=F32, res=None, res_scale=1.0, into=None, name):
    segs = list(a) if isinstance(a, (list, tuple)) else [a]
    if ta:
        (K, M), seg_k = segs[0].shape, [segs[0].shape[0]]
        assert len(segs) == 1
    else:
        M, seg_k = segs[0].shape[0], [s.shape[1] for s in segs]
        K = sum(seg_k)
    if tb:
        N, Kb = b.shape
    else:
        Kb, N = b.shape
    assert K == Kb, ([s.shape for s in segs], b.shape, ta, tb)
    row_off = into[1] if into is not None else 0
    tm = _tile(math.gcd(M, row_off), MM_TN) if into is not None else _tile(M, MM_TM)
    tn = _tile(N, MM_TN)
    tk = _tile(K, MM_TK) if len(segs) == 1 else _tile(math.gcd(*seg_k), MM_TN)
    nk = K // tk
    seg_chunks = [ks // tk for ks in seg_k]
    seg_first = [sum(seg_chunks[:s]) for s in range(len(segs))]
    dims = (((0 if ta else 1,), (1 if tb else 0,)), ((), ()))
    ns = len(segs)
    n_in = ns + 1 + (res is not None) + (into is not None)

    def body(*refs):
        a_refs, b_ref = refs[:ns], refs[ns]
        r_ref = refs[ns + 1] if res is not None else None
        o_ref = refs[n_in]

        def finish(out):
            if r_ref is not None:
                out = out + res_scale * r_ref[...]
            o_ref[...] = out.astype(o_ref.dtype)

        def prod(s):
            return _dot(a_refs[s][...].astype(BF16), b_ref[...].astype(BF16), dims)

        if nk == 1:
            finish(prod(0))
            return
        acc = refs[n_in + 1]
        k = pl.program_id(2)

        @pl.when(k == 0)
        def _():
            acc[...] = jnp.zeros_like(acc)

        for s in range(ns):
            def add(s=s):
                acc[...] += prod(s)
            pl.when((k >= seg_first[s]) & (k < seg_first[s] + seg_chunks[s]))(add)

        @pl.when(k == nk - 1)
        def _():
            finish(acc[...])

    if ta:
        a_specs = [pl.BlockSpec((tk, tm), lambda i, j, k: (k, i))]
    else:
        a_specs = [pl.BlockSpec((tm, tk), functools.partial(
            lambda i, j, k, first, n: (i, jnp.clip(k - first, 0, n - 1)), first=seg_first[s], n=seg_chunks[s]))
            for s in range(ns)]
    b_spec = pl.BlockSpec((tn, tk), lambda i, j, k: (j, k)) if tb else pl.BlockSpec((tk, tn), lambda i, j, k: (k, j))
    in_specs = a_specs + [b_spec]
    args = segs + [b]
    if res is not None:
        in_specs.append(pl.BlockSpec((tm, tn), lambda i, j, k: (i, j)))
        args.append(res)
    if into is None:
        out_shape = jax.ShapeDtypeStruct((M, N), out_dtype)
        blk_off, aliases = 0, {}
    else:
        buf = into[0]
        assert buf.shape[1] == N and row_off % tm == 0 and row_off + M <= buf.shape[0], (buf.shape, M, N, row_off)
        out_shape = jax.ShapeDtypeStruct(buf.shape, buf.dtype)
        blk_off, aliases = row_off // tm, {n_in - 1: 0}
        in_specs.append(_ANY)
        args.append(buf)
    return pl.pallas_call(
        body, name=name,
        out_shape=out_shape,
        grid=(M // tm, N // tn, nk),
        in_specs=in_specs,
        out_specs=pl.BlockSpec((tm, tn), lambda i, j, k: (i + blk_off, j)),
        scratch_shapes=[pltpu.VMEM((tm, tn), F32)] if nk > 1 else [],
        input_output_aliases=aliases,
        compiler_params=_cp(("parallel", "parallel", "arbitrary")),
    )(*args)


def _mm_res_ln(a, w, x, g, b, *, name):
    S, K = a.shape
    D = w.shape[1]
    tm = _tile(S, 256)

    def body(a_ref, w_ref, x_ref, g_ref, b_ref, y_ref, yb_ref, xh_ref, rs_ref):
        h = _dot(a_ref[...], w_ref[...])
        z = ALPHA * x_ref[...] + h
        mu = jnp.mean(z, axis=-1, keepdims=True)
        zc = z - mu
        var = jnp.mean(zc * zc, axis=-1, keepdims=True)
        r = lax.rsqrt(var + LN_EPS)
        xh = zc * r
        y = xh * g_ref[...] + b_ref[...]
        y_ref[...] = y
        yb_ref[...] = y.astype(BF16)
        xh_ref[...] = xh
        rs_ref[...] = r

    row = lambda i: (i, 0)
    full = lambda i: (0, 0)
    return pl.pallas_call(
        body, name=name,
        out_shape=(jax.ShapeDtypeStruct((S, D), F32), jax.ShapeDtypeStruct((S, D), BF16),
                   jax.ShapeDtypeStruct((S, D), F32), jax.ShapeDtypeStruct((S, 1), F32)),
        grid=(S // tm,),
        in_specs=[pl.BlockSpec((tm, K), row), pl.BlockSpec((K, D), full), pl.BlockSpec((tm, D), row),
                  pl.BlockSpec((1, D), full), pl.BlockSpec((1, D), full)],
        out_specs=(pl.BlockSpec((tm, D), row), pl.BlockSpec((tm, D), row), pl.BlockSpec((tm, D), row),
                   pl.BlockSpec((tm, 1), row)),
        compiler_params=_cp(("parallel",)),
    )(a, w, x, g, b)


def _ln_bwd(dy, xh, rs, g, *, name):
    S, D = dy.shape
    tm = _tile(S, 256)

    def body(dy_ref, xh_ref, rs_ref, g_ref, dz_ref, dzb_ref, dg_ref, db_ref):
        @pl.when(pl.program_id(0) == 0)
        def _():
            dg_ref[...] = jnp.zeros_like(dg_ref)
            db_ref[...] = jnp.zeros_like(db_ref)

        dy = dy_ref[...]
        xh = xh_ref[...]
        dxh = dy * g_ref[...]
        m1 = jnp.mean(dxh, axis=-1, keepdims=True)
        m2 = jnp.mean(dxh * xh, axis=-1, keepdims=True)
        dz = rs_ref[...] * (dxh - m1 - xh * m2)
        dz_ref[...] = dz
        dzb_ref[...] = dz.astype(BF16)
        dg_ref[...] += jnp.sum(dy * xh, axis=0, keepdims=True)
        db_ref[...] += jnp.sum(dy, axis=0, keepdims=True)

    row = lambda i: (i, 0)
    full = lambda i: (0, 0)
    return pl.pallas_call(
        body, name=name,
        out_shape=(jax.ShapeDtypeStruct((S, D), F32), jax.ShapeDtypeStruct((S, D), BF16),
                   jax.ShapeDtypeStruct((1, D), F32), jax.ShapeDtypeStruct((1, D), F32)),
        grid=(S // tm,),
        in_specs=[pl.BlockSpec((tm, D), row), pl.BlockSpec((tm, D), row), pl.BlockSpec((tm, 1), row),
                  pl.BlockSpec((1, D), full)],
        out_specs=(pl.BlockSpec((tm, D), row), pl.BlockSpec((tm, D), row), pl.BlockSpec((1, D), full),
                   pl.BlockSpec((1, D), full)),
        compiler_params=_cp(("arbitrary",)),
    )(dy, xh, rs, g)


def _loss_head(y, t, *, name):
    S, D = y.shape
    tm = _tile(S, 512)

    def body(y_ref, t_ref, dy_ref, l_ref):
        @pl.when(pl.program_id(0) == 0)
        def _():
            l_ref[...] = jnp.zeros_like(l_ref)

        e = y_ref[...] - t_ref[...]
        dy_ref[...] = e / D
        l_ref[...] += 0.5 * jnp.sum(jnp.mean(e * e, axis=-1, keepdims=True), axis=0, keepdims=True)

    row = lambda i: (i, 0)
    return pl.pallas_call(
        body, name=name,
        out_shape=(jax.ShapeDtypeStruct((S, D), F32), jax.ShapeDtypeStruct((1, 1), F32)),
        grid=(S // tm,),
        in_specs=[pl.BlockSpec((tm, D), row), pl.BlockSpec((tm, D), row)],
        out_specs=(pl.BlockSpec((tm, D), row), pl.BlockSpec((1, 1), lambda i: (0, 0))),
        compiler_params=_cp(("arbitrary",)),
    )(y, t)


def _rope_tables(S):
    pos = jnp.arange(S, dtype=jnp.int32)
    row = pos // GRID_W
    col = pos % GRID_W

    def cs(p, d):
        half = d // 2
        inv = ROPE_THETA ** (-jnp.arange(half, dtype=F32) * (2.0 / d))
        ang = p.astype(F32)[:, None] * inv[None, :]
        c, s = jnp.cos(ang), jnp.sin(ang)
        return jnp.concatenate([c, c], -1), jnp.concatenate([-s, s], -1)

    ca, sa = cs(pos, HEAD_DIM)
    cr, sr = cs(row, HEAD_DIM // 2)
    cc, sc = cs(col, HEAD_DIM // 2)
    cb, sb = jnp.concatenate([cr, cc], -1), jnp.concatenate([sr, sc], -1)
    two = lambda t: jnp.concatenate([t, t], -1)
    return two(ca), two(sa), two(cb), two(sb)


def _partner(x, lane, width):
    h = width // 2
    return jnp.where(lane % width < h, pltpu.roll(x, LANES - h, 1), pltpu.roll(x, h, 1))


def _rope_fwd(x, c, s, lane, width):
    return x * c + _partner(x, lane, width) * s


def _rope_bwd(dy, c, s, lane, width):
    return dy * c + _partner(dy * s, lane, width)


def _head_sum(x, seg):
    return lax.dot_general(x, seg, _NN, precision=lax.Precision.HIGHEST, preferred_element_type=F32)


def _split_heads(x, lane):
    lo = lane < HEAD_DIM
    r = pltpu.roll(x, HEAD_DIM, 1)
    z = jnp.zeros_like(x)
    return jnp.where(lo, x, z), jnp.where(lo, z, r), jnp.where(lo, r, z), jnp.where(lo, z, x)


def _fold_heads(d0, d1, lane):
    t0 = d0 + pltpu.roll(d0, HEAD_DIM, 1)
    t1 = d1 + pltpu.roll(d1, HEAD_DIM, 1)
    return jnp.where(lane < HEAD_DIM, t0, t1)


def _seg_matrix():
    i = jnp.arange(LANES)
    return (i[:, None] // HEAD_DIM == i[None, :] // HEAD_DIM).astype(F32)


def _prep(proj, tabs, qg2, kg2, seg, *, name):
    S = proj.shape[0]
    ts = _tile(S, 256)
    ca, sa, cb, sb = tabs

    def body(pa_ref, pb_ref, ca_ref, sa_ref, cb_ref, sb_ref, qg_ref, kg_ref, seg_ref,
             aq_ref, ak_ref, av_ref, bq_ref, bk_ref, bv_ref):
        lane = lax.broadcasted_iota(jnp.int32, (ts, LANES), 1)
        ca, sa, cb, sb = ca_ref[...], sa_ref[...], cb_ref[...], sb_ref[...]
        seg = seg_ref[...]

        def norm(x, gain):
            r = lax.rsqrt(_head_sum(x * x, seg) * (1.0 / HEAD_DIM) + RMS_EPS)
            return x * r * gain

        def put(ref, x):
            for i, part in enumerate(_split_heads(x, lane)):
                ref[i] = part.astype(BF16)

        for gidx in range(4):
            cols = slice(gidx * LANES, (gidx + 1) * LANES)
            aq_ref[:, cols] = (_rope_fwd(pa_ref[:, cols], ca, sa, lane, HEAD_DIM) * 0.125).astype(BF16)
            bq = norm(pb_ref[:, cols], qg_ref[...])
            bq_ref[:, cols] = (_rope_fwd(bq, cb, sb, lane, HEAD_DIM // 2) * 0.125).astype(BF16)
        put(ak_ref, _rope_fwd(pa_ref[:, 512:640], ca, sa, lane, HEAD_DIM))
        put(av_ref, pa_ref[:, 640:768])
        bk = norm(pb_ref[:, 512:640], kg_ref[...])
        put(bk_ref, _rope_fwd(bk, cb, sb, lane, HEAD_DIM // 2))
        put(bv_ref, pb_ref[:, 640:768])

    row = lambda i: (i, 0)
    full = lambda i: (0, 0)
    tab = pl.BlockSpec((ts, LANES), row)
    kv_shape = jax.ShapeDtypeStruct((4, S, LANES), BF16)
    kv_spec = pl.BlockSpec((4, ts, LANES), lambda i: (0, i, 0))
    q_shape = jax.ShapeDtypeStruct((S, 512), BF16)
    q_spec = pl.BlockSpec((ts, 512), row)
    return pl.pallas_call(
        body, name=name,
        out_shape=(q_shape, kv_shape, kv_shape, q_shape, kv_shape, kv_shape),
        grid=(S // ts,),
        in_specs=[pl.BlockSpec((ts, QKV_W), lambda i: (i, 0)), pl.BlockSpec((ts, QKV_W), lambda i: (i, 1)),
                  tab, tab, tab, tab, pl.BlockSpec((1, LANES), full), pl.BlockSpec((1, LANES), full),
                  pl.BlockSpec((LANES, LANES), full)],
        out_specs=(q_spec, kv_spec, kv_spec, q_spec, kv_spec, kv_spec),
        compiler_params=_cp(("parallel",)),
    )(proj, proj, ca, sa, cb, sb, qg2, kg2, seg)


def _unprep(dqa, dka, dva, dqb, dkb, dvb, proj, tabs, qg2, kg2, seg, *, name):
    S = proj.shape[0]
    ts = _tile(S, 256)
    ca, sa, cb, sb = tabs

    def body(dqa_ref, dka_ref, dva_ref, dqb_ref, dkb_ref, dvb_ref, pb_ref, ca_ref, sa_ref, cb_ref, sb_ref,
             qg_ref, kg_ref, seg_ref, dp_ref, dqg_ref, dkg_ref):
        @pl.when(pl.program_id(0) == 0)
        def _():
            dqg_ref[...] = jnp.zeros_like(dqg_ref)
            dkg_ref[...] = jnp.zeros_like(dkg_ref)

        lane = lax.broadcasted_iota(jnp.int32, (ts, LANES), 1)
        ca, sa, cb, sb = ca_ref[...], sa_ref[...], cb_ref[...], sb_ref[...]
        seg = seg_ref[...]

        def norm_bwd(dy, x, gain):
            r = lax.rsqrt(_head_sum(x * x, seg) * (1.0 / HEAD_DIM) + RMS_EPS)
            gdy = gain * dy
            dot = _head_sum(gdy * x, seg) * (1.0 / HEAD_DIM)
            dx = r * gdy - x * (r * r * r) * dot
            return dx, jnp.sum(dy * x * r, axis=0, keepdims=True)

        for gidx in range(4):
            cols = slice(gidx * LANES, (gidx + 1) * LANES)
            dp_ref[:, cols] = _rope_bwd(dqa_ref[:, cols] * 0.125, ca, sa, lane, HEAD_DIM).astype(BF16)
            dbq = _rope_bwd(dqb_ref[:, cols] * 0.125, cb, sb, lane, HEAD_DIM // 2)
            dx, dg = norm_bwd(dbq, pb_ref[:, cols], qg_ref[...])
            dp_ref[:, COL_B + gidx * LANES:COL_B + (gidx + 1) * LANES] = dx.astype(BF16)
            dqg_ref[...] += dg
        dak = _fold_heads(dka_ref[0] + dka_ref[1], dka_ref[2] + dka_ref[3], lane)
        dp_ref[:, 512:640] = _rope_bwd(dak, ca, sa, lane, HEAD_DIM).astype(BF16)
        dp_ref[:, 640:768] = _fold_heads(dva_ref[0] + dva_ref[1], dva_ref[2] + dva_ref[3], lane).astype(BF16)
        dbk = _fold_heads(dkb_ref[0] + dkb_ref[1], dkb_ref[2] + dkb_ref[3], lane)
        dbk = _rope_bwd(dbk, cb, sb, lane, HEAD_DIM // 2)
        dx, dg = norm_bwd(dbk, pb_ref[:, 512:640], kg_ref[...])
        dp_ref[:, COL_B + 512:COL_B + 640] = dx.astype(BF16)
        dkg_ref[...] += dg
        dp_ref[:, COL_B + 640:COL_B + 768] = _fold_heads(dvb_ref[0] + dvb_ref[1], dvb_ref[2] + dvb_ref[3],
                                                         lane).astype(BF16)

    row = lambda i: (i, 0)
    full = lambda i: (0, 0)
    tab = pl.BlockSpec((ts, LANES), row)
    q_spec = pl.BlockSpec((ts, 512), row)
    kv_spec = pl.BlockSpec((4, ts, LANES), lambda i: (0, i, 0))
    return pl.pallas_call(
        body, name=name,
        out_shape=(jax.ShapeDtypeStruct((S, 2 * QKV_W), BF16), jax.ShapeDtypeStruct((1, LANES), F32),
                   jax.ShapeDtypeStruct((1, LANES), F32)),
        grid=(S // ts,),
        in_specs=[q_spec, kv_spec, kv_spec, q_spec, kv_spec, kv_spec,
                  pl.BlockSpec((ts, QKV_W), lambda i: (i, 1)), tab, tab, tab, tab,
                  pl.BlockSpec((1, LANES), full), pl.BlockSpec((1, LANES), full), pl.BlockSpec((LANES, LANES), full)],
        out_specs=(pl.BlockSpec((ts, 2 * QKV_W), row), pl.BlockSpec((1, LANES), full),
                   pl.BlockSpec((1, LANES), full)),
        compiler_params=_cp(("arbitrary",)),
    )(dqa, dka, dva, dqb, dkb, dvb, proj, ca, sa, cb, sb, qg2, kg2, seg)


def _attn_dense_fwd(q, k4, v4, *, gather=(), name):
    S = q.shape[0]
    tq = _tile(S, 256)
    xs = [x for x, _ in gather]
    na = len(xs)

    def body(q_ref, k_ref, v_ref, *rest):
        o_ref, lse_ref = rest[na], rest[na + 1]
        if na:
            x_refs, out_refs, sems = rest[:na], rest[na + 2:2 * na + 2], rest[2 * na + 2:]
            start, finish = _gather_plan([(x_refs[a], gather[a][1], out_refs[a]) for a in range(na)], *sems)
            pl.when((pl.program_id(0) == 0) & (pl.program_id(1) == 0))(start)
        for pr in range(2):
            qp = q_ref[:, pr * LANES:(pr + 1) * LANES]
            acc = None
            for half in range(2):
                s = _dot(qp, k_ref[half], _NT)
                m = jnp.max(s, axis=-1, keepdims=True)
                e = jnp.exp(s - m)
                l = jnp.sum(e, axis=-1, keepdims=True)
                pv = _dot(e.astype(BF16), v_ref[half]) * (1.0 / l)
                acc = pv if acc is None else acc + pv
                lse_ref[pr * 2 + half] = m + jnp.log(l)
            o_ref[:, pr * LANES:(pr + 1) * LANES] = acc.astype(BF16)
        if na:
            pl.when((pl.program_id(0) == 1) & (pl.program_id(1) == S // tq - 1))(finish)

    kv_spec = pl.BlockSpec((2, S, LANES), lambda kv, i: (kv, 0, 0))
    res = pl.pallas_call(
        body, name=name,
        out_shape=(jax.ShapeDtypeStruct((S, 512), BF16), jax.ShapeDtypeStruct((8, S, 1), F32),
                   *[_gathered_shape(x) for x in xs]),
        grid=(2, S // tq),
        in_specs=[pl.BlockSpec((tq, 256), lambda kv, i: (i, kv)), kv_spec, kv_spec] + [_ANY] * na,
        out_specs=(pl.BlockSpec((tq, 256), lambda kv, i: (i, kv)),
                   pl.BlockSpec((4, tq, 1), lambda kv, i: (kv, i, 0)), *([_ANY] * na)),
        scratch_shapes=_comm_scratch(na) if na else [],
        compiler_params=_cp(("arbitrary", "arbitrary") if na else ("parallel", "parallel")),
    )(q, k4, v4, *xs)
    return res[0], res[1], list(res[2:])


def _attn_dense_bwd(q, k4, v4, lse, do, *, scatter=(), name):
    S = q.shape[0]
    tq = _tile(S, 256)
    na = len(scatter)
    comm_in, comm_out, held = _scatter_io(scatter)
    n_in = len(comm_in)

    def body(q_ref, k_ref, v_ref, lse_ref, do_ref, *rest):
        dq_ref, dk_ref, dv_ref = rest[n_in:n_in + 3]
        if na:
            s_refs, r_refs, sems = rest[:na], rest[n_in + 3:n_in + 3 + na], rest[n_in + 3 + na:]
            start, finish = _scatter_plan([(s_refs[a], r_refs[a], scatter[a][2]) for a in range(na)], *sems)
            pl.when((pl.program_id(0) == 0) & (pl.program_id(1) == 0))(start)

        @pl.when(pl.program_id(1) == 0)
        def _():
            dk_ref[...] = jnp.zeros_like(dk_ref)
            dv_ref[...] = jnp.zeros_like(dv_ref)

        lane = lax.broadcasted_iota(jnp.int32, (tq, LANES), 1)
        for pr in range(2):
            qp = q_ref[:, pr * LANES:(pr + 1) * LANES]
            dop = do_ref[:, pr * LANES:(pr + 1) * LANES].astype(BF16)
            dq = None
            for half in range(2):
                mine = (lane < HEAD_DIM) if half == 0 else (lane >= HEAD_DIM)
                s = _dot(qp, k_ref[half], _NT)
                p = jnp.exp(s - lse_ref[pr * 2 + half])
                dp = _dot(dop, v_ref[half], _NT)
                delta = jnp.sum(p * dp, axis=-1, keepdims=True)
                ds = (p * (dp - delta)).astype(BF16)
                pb = p.astype(BF16)
                d = _dot(ds, k_ref[half])
                dq = d if dq is None else dq + d
                dk_ref[half] += _dot(ds, jnp.where(mine, qp, jnp.zeros_like(qp)), _TN)
                dv_ref[half] += _dot(pb, jnp.where(mine, dop, jnp.zeros_like(dop)), _TN)
            dq_ref[:, pr * LANES:(pr + 1) * LANES] = dq
        if na:
            pl.when((pl.program_id(0) == 1) & (pl.program_id(1) == S // tq - 1))(finish)

    kv_spec = pl.BlockSpec((2, S, LANES), lambda kv, i: (kv, 0, 0))
    q_spec = pl.BlockSpec((tq, 256), lambda kv, i: (i, kv))
    res = pl.pallas_call(
        body, name=name,
        out_shape=(jax.ShapeDtypeStruct((S, 512), F32), jax.ShapeDtypeStruct((4, S, LANES), F32),
                   jax.ShapeDtypeStruct((4, S, LANES), F32), *comm_out),
        grid=(2, S // tq),
        in_specs=[q_spec, kv_spec, kv_spec, pl.BlockSpec((4, tq, 1), lambda kv, i: (kv, i, 0)), q_spec]
                 + [_ANY] * n_in,
        out_specs=(q_spec, kv_spec, kv_spec, *([_ANY] * na)),
        scratch_shapes=_comm_scratch(na) if na else [],
        input_output_aliases={5 + na + i: 3 + a for i, a in enumerate(held)},
        compiler_params=_cp(("arbitrary", "arbitrary") if na else ("parallel", "arbitrary")),
    )(q, k4, v4, lse, do, *comm_in)
    return res[0], res[1], res[2], list(res[3:])


WIN_KEYS = 3 * BLOCK


def _win_start(n, S):
    return pl.multiple_of(jnp.clip((n - 1) * BLOCK, 0, S - WIN_KEYS), BLOCK)


def _win_valid(n, start):
    qpos = n * BLOCK + lax.broadcasted_iota(jnp.int32, (BLOCK, WIN_KEYS), 0)
    kpos = start + lax.broadcasted_iota(jnp.int32, (BLOCK, WIN_KEYS), 1)
    return jnp.abs(qpos - kpos) <= WINDOW


def _attn_win_fwd(q, k4, v4, sink, *, name):
    S = q.shape[0]
    assert S >= WIN_KEYS

    def body(sink_ref, q_ref, k_ref, v_ref, o_ref, lse_ref):
        n = pl.program_id(0)
        start = _win_start(n, S)
        valid = _win_valid(n, start)
        for kv in range(2):
            for pr in range(2):
                cols = slice((kv * 2 + pr) * LANES, (kv * 2 + pr + 1) * LANES)
                qp = q_ref[:, cols]
                acc = None
                for half in range(2):
                    h = kv * 4 + pr * 2 + half
                    kk = k_ref[kv * 2 + half, pl.ds(start, WIN_KEYS), :]
                    vv = v_ref[kv * 2 + half, pl.ds(start, WIN_KEYS), :]
                    s = jnp.where(valid, _dot(qp, kk, _NT), NEG_BIG)
                    snk = sink_ref[h]
                    m = jnp.maximum(jnp.max(s, axis=-1, keepdims=True), snk)
                    e = jnp.exp(s - m)
                    l = jnp.sum(e, axis=-1, keepdims=True) + jnp.exp(snk - m)
                    p = (e * (1.0 / l)).astype(BF16)
                    pv = _dot(p, vv)
                    acc = pv if acc is None else acc + pv
                    lse_ref[h] = m + jnp.log(l)
                o_ref[:, cols] = acc.astype(BF16)

    kv_spec = pl.BlockSpec((4, S, LANES), lambda n: (0, 0, 0))
    return pl.pallas_call(
        body, name=name,
        out_shape=(jax.ShapeDtypeStruct((S, 512), BF16), jax.ShapeDtypeStruct((8, S, 1), F32)),
        grid=(S // BLOCK,),
        in_specs=[pl.BlockSpec(memory_space=pltpu.SMEM), pl.BlockSpec((BLOCK, 512), lambda n: (n, 0)),
                  kv_spec, kv_spec],
        out_specs=(pl.BlockSpec((BLOCK, 512), lambda n: (n, 0)), pl.BlockSpec((8, BLOCK, 1), lambda n: (0, n, 0))),
        compiler_params=_cp(("parallel",)),
    )(sink, q, k4, v4)


def _attn_win_bwd(q, k4, v4, sink, lse, do, *, name):
    S = q.shape[0]

    def body(sink_ref, q_ref, k_ref, v_ref, lse_ref, do_ref, dq_ref, dk_ref, dv_ref, dsink_ref):
        n = pl.program_id(0)

        @pl.when(n == 0)
        def _():
            dk_ref[...] = jnp.zeros_like(dk_ref)
            dv_ref[...] = jnp.zeros_like(dv_ref)
            dsink_ref[...] = jnp.zeros_like(dsink_ref)

        start = _win_start(n, S)
        valid = _win_valid(n, start)
        lane = lax.broadcasted_iota(jnp.int32, (BLOCK, LANES), 1)
        for kv in range(2):
            for pr in range(2):
                cols = slice((kv * 2 + pr) * LANES, (kv * 2 + pr + 1) * LANES)
                qp = q_ref[:, cols]
                dop = do_ref[:, cols].astype(BF16)
                dq = None
                for half in range(2):
                    h = kv * 4 + pr * 2 + half
                    slot = kv * 2 + half
                    mine = (lane < HEAD_DIM) if half == 0 else (lane >= HEAD_DIM)
                    win = pl.ds(start, WIN_KEYS)
                    kk = k_ref[slot, win, :]
                    vv = v_ref[slot, win, :]
                    lse_h = lse_ref[h]
                    s = jnp.where(valid, _dot(qp, kk, _NT), NEG_BIG)
                    p = jnp.exp(s - lse_h)
                    dp = _dot(dop, vv, _NT)
                    delta = jnp.sum(p * dp, axis=-1, keepdims=True)
                    ds = (p * (dp - delta)).astype(BF16)
                    pb = p.astype(BF16)
                    d = _dot(ds, kk)
                    dq = d if dq is None else dq + d
                    dk_ref[slot, win, :] += _dot(ds, jnp.where(mine, qp, jnp.zeros_like(qp)), _TN)
                    dv_ref[slot, win, :] += _dot(pb, jnp.where(mine, dop, jnp.zeros_like(dop)), _TN)
                    p_sink = jnp.exp(sink_ref[h] - lse_h)
                    dsink_ref[h:h + 1, :] += jnp.broadcast_to(-jnp.sum(p_sink * delta, axis=0, keepdims=True),
                                                              (1, LANES))
                dq_ref[:, cols] = dq

    kv_spec = pl.BlockSpec((4, S, LANES), lambda n: (0, 0, 0))
    q_spec = pl.BlockSpec((BLOCK, 512), lambda n: (n, 0))
    return pl.pallas_call(
        body, name=name,
        out_shape=(jax.ShapeDtypeStruct((S, 512), F32), jax.ShapeDtypeStruct((4, S, LANES), F32),
                   jax.ShapeDtypeStruct((4, S, LANES), F32), jax.ShapeDtypeStruct((8, LANES), F32)),
        grid=(S // BLOCK,),
        in_specs=[pl.BlockSpec(memory_space=pltpu.SMEM), q_spec, kv_spec, kv_spec,
                  pl.BlockSpec((8, BLOCK, 1), lambda n: (0, n, 0)), q_spec],
        out_specs=(q_spec, kv_spec, kv_spec, pl.BlockSpec((8, LANES), lambda n: (0, 0))),
        compiler_params=_cp(("arbitrary",)),
    )(sink, q, k4, v4, lse, do)


def _c_ln(v, g, b):
    mu = jnp.mean(v, axis=-1, keepdims=True)
    vc = v - mu
    r = lax.rsqrt(jnp.mean(vc * vc, axis=-1, keepdims=True) + LN_EPS)
    vh = vc * r
    return vh, r, vh * g + b


def _gmlp_fwd(proj, ws, bs3, lg, lb, *, name):
    S = proj.shape[0]

    def body(u_ref, v_ref, ws_ref, bs_ref, lg_ref, lb_ref, o_ref):
        u = _gelu(u_ref[...])
        _, _, vn = _c_ln(_gelu(v_ref[...]), lg_ref[...], lb_ref[...])
        vn = vn.astype(BF16)
        for gi in range(C_GROUPS):
            cols = slice(gi * LANES, (gi + 1) * LANES)
            mixed = _dot(ws_ref[gi], vn[:, cols]) + bs_ref[gi]
            o_ref[:, cols] = (u[:, cols] * mixed).astype(BF16)

    full2 = lambda n: (0, 0)
    full3 = lambda n: (0, 0, 0)
    return pl.pallas_call(
        body, name=name,
        out_shape=jax.ShapeDtypeStruct((S, C_WIDTH), BF16),
        grid=(S // CHUNK,),
        in_specs=[pl.BlockSpec((CHUNK, C_WIDTH), lambda n: (n, COL_C // C_WIDTH)),
                  pl.BlockSpec((CHUNK, C_WIDTH), lambda n: (n, COL_C // C_WIDTH + 1)),
                  pl.BlockSpec((C_GROUPS, CHUNK, CHUNK), full3), pl.BlockSpec((C_GROUPS, CHUNK, 1), full3),
                  pl.BlockSpec((1, C_WIDTH), full2), pl.BlockSpec((1, C_WIDTH), full2)],
        out_specs=pl.BlockSpec((CHUNK, C_WIDTH), lambda n: (n, 0)),
        compiler_params=_cp(("parallel",)),
    )(proj, proj, ws, bs3, lg, lb)


def _gmlp_bwd(proj, dout, ws, bs3, lg, lb, *, name):
    S = proj.shape[0]

    def body(u_ref, v_ref, d_ref, ws_ref, bs_ref, lg_ref, lb_ref, dz_ref, dws_ref, dbs_ref, dlg_ref, dlb_ref):
        @pl.when(pl.program_id(0) == 0)
        def _():
            dws_ref[...] = jnp.zeros_like(dws_ref)
            dbs_ref[...] = jnp.zeros_like(dbs_ref)
            dlg_ref[...] = jnp.zeros_like(dlg_ref)
            dlb_ref[...] = jnp.zeros_like(dlb_ref)

        u_pre, v_pre, d = u_ref[...], v_ref[...], d_ref[...]
        u, u_grad = _gelu_and_grad(u_pre)
        v, v_grad = _gelu_and_grad(v_pre)
        vh, r, vn = _c_ln(v, lg_ref[...], lb_ref[...])
        vnb = vn.astype(BF16)
        du_parts, dvn_parts = [], []
        for gi in range(C_GROUPS):
            cols = slice(gi * LANES, (gi + 1) * LANES)
            mixed = _dot(ws_ref[gi], vnb[:, cols]) + bs_ref[gi]
            du_parts.append(d[:, cols] * mixed)
            dm = d[:, cols] * u[:, cols]
            dbs_ref[gi] += jnp.sum(dm, axis=-1, keepdims=True)
            dmb = dm.astype(BF16)
            dws_ref[gi] += _dot(dmb, vnb[:, cols], _NT)
            dvn_parts.append(_dot(ws_ref[gi], dmb, _TN))
        du = jnp.concatenate(du_parts, axis=-1)
        dvn = jnp.concatenate(dvn_parts, axis=-1)
        dlg_ref[...] += jnp.sum(dvn * vh, axis=0, keepdims=True)
        dlb_ref[...] += jnp.sum(dvn, axis=0, keepdims=True)
        dvh = dvn * lg_ref[...]
        m1 = jnp.mean(dvh, axis=-1, keepdims=True)
        m2 = jnp.mean(dvh * vh, axis=-1, keepdims=True)
        dv = r * (dvh - m1 - vh * m2)
        dz_ref[:, :C_WIDTH] = (du * u_grad).astype(BF16)
        dz_ref[:, C_WIDTH:] = (dv * v_grad).astype(BF16)

    full2 = lambda n: (0, 0)
    full3 = lambda n: (0, 0, 0)
    return pl.pallas_call(
        body, name=name,
        out_shape=(jax.ShapeDtypeStruct((S, 2 * C_WIDTH), BF16), jax.ShapeDtypeStruct((C_GROUPS, CHUNK, CHUNK), F32),
                   jax.ShapeDtypeStruct((C_GROUPS, CHUNK, 1), F32), jax.ShapeDtypeStruct((1, C_WIDTH), F32),
                   jax.ShapeDtypeStruct((1, C_WIDTH), F32)),
        grid=(S // CHUNK,),
        in_specs=[pl.BlockSpec((CHUNK, C_WIDTH), lambda n: (n, COL_C // C_WIDTH)),
                  pl.BlockSpec((CHUNK, C_WIDTH), lambda n: (n, COL_C // C_WIDTH + 1)),
                  pl.BlockSpec((CHUNK, C_WIDTH), lambda n: (n, 0)),
                  pl.BlockSpec((C_GROUPS, CHUNK, CHUNK), full3), pl.BlockSpec((C_GROUPS, CHUNK, 1), full3),
                  pl.BlockSpec((1, C_WIDTH), full2), pl.BlockSpec((1, C_WIDTH), full2)],
        out_specs=(pl.BlockSpec((CHUNK, 2 * C_WIDTH), lambda n: (n, 0)), pl.BlockSpec((C_GROUPS, CHUNK, CHUNK), full3),
                   pl.BlockSpec((C_GROUPS, CHUNK, 1), full3), pl.BlockSpec((1, C_WIDTH), full2),
                   pl.BlockSpec((1, C_WIDTH), full2)),
        compiler_params=_cp(("arbitrary",)),
    )(proj, proj, dout, ws, bs3, lg, lb)


GATE_BLK = 512


def _gate_specs(tm, D):
    nh = D // GATE_BLK
    first = COL_GATE // GATE_BLK
    return [pl.BlockSpec((tm, GATE_BLK), functools.partial(lambda i, c: (i, c), c=first + b))
            for b in range(N_BRANCH * nh)]


def _merge_fwd(oa, ob, oc, wb, proj, bg, *, name):
    S = oa.shape[0]
    D = wb.shape[2]
    assert D % GATE_BLK == 0
    nh = D // GATE_BLK
    tm = _tile(S, 256)

    def body(oa_ref, ob_ref, oc_ref, wb_ref, *rest):
        gate_refs, bg_ref, o_ref = rest[:N_BRANCH * nh], rest[N_BRANCH * nh], rest[N_BRANCH * nh + 1]
        brs = (oa_ref[...], ob_ref[...], oc_ref[...])
        for j in range(nh):
            cols = slice(j * GATE_BLK, (j + 1) * GATE_BLK)
            acc = None
            for n in range(N_BRANCH):
                b = n * nh + j
                t = _dot(brs[n], wb_ref[n, :, cols])
                g = _sigmoid(gate_refs[b][...] + bg_ref[:, b * GATE_BLK:(b + 1) * GATE_BLK])
                acc = t * g if acc is None else acc + t * g
            o_ref[:, cols] = acc.astype(BF16)

    row = lambda i: (i, 0)
    br = pl.BlockSpec((tm, BRANCH_WIDTH), row)
    return pl.pallas_call(
        body, name=name,
        out_shape=jax.ShapeDtypeStruct((S, D), BF16),
        grid=(S // tm,),
        in_specs=[br, br, br, pl.BlockSpec((N_BRANCH, BRANCH_WIDTH, D), lambda i: (0, 0, 0))]
                 + _gate_specs(tm, D) + [pl.BlockSpec((1, N_BRANCH * D), lambda i: (0, 0))],
        out_specs=pl.BlockSpec((tm, D), row),
        compiler_params=_cp(("parallel",)),
    )(oa, ob, oc, wb, *([proj] * (N_BRANCH * nh)), bg)


def _merge_bwd(oa, ob, oc, wb, proj, bg, dmerged, *, name):
    S = oa.shape[0]
    D = wb.shape[2]
    nh = D // GATE_BLK
    tm = _tile(S, 256)

    def body(oa_ref, ob_ref, oc_ref, wb_ref, *rest):
        gate_refs = rest[:N_BRANCH * nh]
        bg_ref, dm_ref, dgl_ref, dbg_ref = rest[N_BRANCH * nh:N_BRANCH * nh + 4]
        dt_refs = rest[N_BRANCH * nh + 4:N_BRANCH * nh + 4 + N_BRANCH]
        dbr_refs = rest[N_BRANCH * nh + 4 + N_BRANCH:]

        @pl.when(pl.program_id(0) == 0)
        def _():
            dbg_ref[...] = jnp.zeros_like(dbg_ref)

        brs = (oa_ref[...], ob_ref[...], oc_ref[...])
        for n in range(N_BRANCH):
            dbr = None
            for j in range(nh):
                cols = slice(j * GATE_BLK, (j + 1) * GATE_BLK)
                b = n * nh + j
                gcols = slice(b * GATE_BLK, (b + 1) * GATE_BLK)
                w = wb_ref[n, :, cols]
                t = _dot(brs[n], w)
                g = _sigmoid(gate_refs[b][...] + bg_ref[:, gcols])
                dm = dm_ref[:, cols]
                dt = (dm * g).astype(BF16)
                dgl = dm * t * g * (1.0 - g)
                dt_refs[n][:, cols] = dt
                dgl_ref[:, gcols] = dgl.astype(BF16)
                dbg_ref[:, gcols] += jnp.sum(dgl, axis=0, keepdims=True)
                d = _dot(dt, w, _NT)
                dbr = d if dbr is None else dbr + d
            dbr_refs[n][...] = dbr

    row = lambda i: (i, 0)
    br = pl.BlockSpec((tm, BRANCH_WIDTH), row)
    res = pl.pallas_call(
        body, name=name,
        out_shape=(jax.ShapeDtypeStruct((S, N_BRANCH * D), BF16), jax.ShapeDtypeStruct((1, N_BRANCH * D), F32),
                   *([jax.ShapeDtypeStruct((S, D), BF16)] * N_BRANCH),
                   *([jax.ShapeDtypeStruct((S, BRANCH_WIDTH), F32)] * N_BRANCH)),
        grid=(S // tm,),
        in_specs=[br, br, br, pl.BlockSpec((N_BRANCH, BRANCH_WIDTH, D), lambda i: (0, 0, 0))]
                 + _gate_specs(tm, D)
                 + [pl.BlockSpec((1, N_BRANCH * D), lambda i: (0, 0)), pl.BlockSpec((tm, D), row)],
        out_specs=(pl.BlockSpec((tm, N_BRANCH * D), row), pl.BlockSpec((1, N_BRANCH * D), lambda i: (0, 0)),
                   *([pl.BlockSpec((tm, D), row)] * N_BRANCH), *([br] * N_BRANCH)),
        compiler_params=_cp(("arbitrary",)),
    )(oa, ob, oc, wb, *([proj] * (N_BRANCH * nh)), bg, dmerged)
    return res[0], res[1], list(res[2:2 + N_BRANCH]), list(res[2 + N_BRANCH:])


X_SCALE = 1.0 / math.sqrt(X_HEAD_DIM)
X_W = X_HEADS * X_HEAD_DIM


def _xattn_fwd(q, kv, *, name):
    S = q.shape[0]
    M = kv.shape[0]
    tq = _tile(S, 512)

    def body(q_ref, kv_ref, o_ref, lse_ref):
        for h in range(X_HEADS):
            cols = slice(h * LANES, (h + 1) * LANES)
            s = _dot(q_ref[:, cols], kv_ref[:, cols], _NT) * X_SCALE
            m = jnp.max(s, axis=-1, keepdims=True)
            e = jnp.exp(s - m)
            l = jnp.sum(e, axis=-1, keepdims=True)
            p = (e * (1.0 / l)).astype(BF16)
            o_ref[:, cols] = _dot(p, kv_ref[:, X_W + h * LANES:X_W + (h + 1) * LANES]).astype(BF16)
            lse_ref[h] = m + jnp.log(l)

    return pl.pallas_call(
        body, name=name,
        out_shape=(jax.ShapeDtypeStruct((S, X_W), BF16), jax.ShapeDtypeStruct((X_HEADS, S, 1), F32)),
        grid=(S // tq,),
        in_specs=[pl.BlockSpec((tq, X_W), lambda i: (i, 0)), pl.BlockSpec((M, 2 * X_W), lambda i: (0, 0))],
        out_specs=(pl.BlockSpec((tq, X_W), lambda i: (i, 0)), pl.BlockSpec((X_HEADS, tq, 1), lambda i: (0, i, 0))),
        compiler_params=_cp(("parallel",)),
    )(q, kv)


def _xattn_bwd(q, kv, lse, do, *, name):
    S = q.shape[0]
    M = kv.shape[0]
    tq = _tile(S, 512)

    def body(q_ref, kv_ref, lse_ref, do_ref, dq_ref, dkv_ref):
        @pl.when(pl.program_id(0) == 0)
        def _():
            dkv_ref[...] = jnp.zeros_like(dkv_ref)

        for h in range(X_HEADS):
            cols = slice(h * LANES, (h + 1) * LANES)
            vcols = slice(X_W + h * LANES, X_W + (h + 1) * LANES)
            qh, kh, vh = q_ref[:, cols], kv_ref[:, cols], kv_ref[:, vcols]
            doh = do_ref[:, cols].astype(BF16)
            p = jnp.exp(_dot(qh, kh, _NT) * X_SCALE - lse_ref[h])
            dp = _dot(doh, vh, _NT)
            delta = jnp.sum(p * dp, axis=-1, keepdims=True)
            ds = (p * (dp - delta) * X_SCALE).astype(BF16)
            dq_ref[:, cols] = _dot(ds, kh).astype(BF16)
            dkv_ref[:, cols] += _dot(ds, qh, _TN)
            dkv_ref[:, vcols] += _dot(p.astype(BF16), doh, _TN)

    q_spec = pl.BlockSpec((tq, X_W), lambda i: (i, 0))
    return pl.pallas_call(
        body, name=name,
        out_shape=(jax.ShapeDtypeStruct((S, X_W), BF16), jax.ShapeDtypeStruct((M, 2 * X_W), F32)),
        grid=(S // tq,),
        in_specs=[q_spec, pl.BlockSpec((M, 2 * X_W), lambda i: (0, 0)),
                  pl.BlockSpec((X_HEADS, tq, 1), lambda i: (0, i, 0)), q_spec],
        out_specs=(q_spec, pl.BlockSpec((M, 2 * X_W), lambda i: (0, 0))),
        compiler_params=_cp(("arbitrary",)),
    )(q, kv, lse, do)


def _shift_down(h, row):
    return jnp.where(row == 0, 0.0, pltpu.roll(h, 1, 0))


def _shift_up(h, row, S):
    return jnp.where(row == S - 1, 0.0, pltpu.roll(h, S - 1, 0))


def _conv3(h, ck, cb, row, S):
    return _shift_down(h, row) * ck[0:1] + h * ck[1:2] + _shift_up(h, row, S) * ck[2:3] + cb


def _conv_act_fwd(h, ck, cb, *, name):
    S, F2 = h.shape
    F = F2 // 2
    nt = F // LANES

    def body(ha_ref, hb_ref, cka_ref, ckb_ref, cba_ref, cbb_ref, o_ref):
        row = lax.broadcasted_iota(jnp.int32, (S, LANES), 0)
        a = _conv3(ha_ref[...], cka_ref[...], cba_ref[...], row, S)
        b = _conv3(hb_ref[...], ckb_ref[...], cbb_ref[...], row, S)
        o_ref[...] = (_gelu(a) * b).astype(BF16)

    ca = lambda j: (0, j)
    cbi = lambda j: (0, j + nt)
    return pl.pallas_call(
        body, name=name,
        out_shape=jax.ShapeDtypeStruct((S, F), BF16),
        grid=(nt,),
        in_specs=[pl.BlockSpec((S, LANES), ca), pl.BlockSpec((S, LANES), cbi), pl.BlockSpec((3, LANES), ca),
                  pl.BlockSpec((3, LANES), cbi), pl.BlockSpec((1, LANES), ca), pl.BlockSpec((1, LANES), cbi)],
        out_specs=pl.BlockSpec((S, LANES), ca),
        compiler_params=_cp(("parallel",)),
    )(h, h, ck, ck, cb, cb)


def _conv_act_bwd(h, ck, cb, dact, *, name):
    S, F2 = h.shape
    F = F2 // 2
    nt = F // LANES

    def body(ha_ref, hb_ref, cka_ref, ckb_ref, cba_ref, cbb_ref, d_ref,
             dha_ref, dhb_ref, dcka_ref, dckb_ref, dcba_ref, dcbb_ref):
        row = lax.broadcasted_iota(jnp.int32, (S, LANES), 0)
        ha, hb = ha_ref[...], hb_ref[...]
        cka, ckb = cka_ref[...], ckb_ref[...]
        a = _conv3(ha, cka, cba_ref[...], row, S)
        b = _conv3(hb, ckb, cbb_ref[...], row, S)
        d = d_ref[...]
        ga, ga_grad = _gelu_and_grad(a)
        da = d * b * ga_grad
        db = d * ga
        for dd, hh, ck_, dh_ref, dck_ref, dcb_ref in ((da, ha, cka, dha_ref, dcka_ref, dcba_ref),
                                                      (db, hb, ckb, dhb_ref, dckb_ref, dcbb_ref)):
            dcb_ref[...] = jnp.sum(dd, axis=0, keepdims=True)
            dck_ref[0:1, :] = jnp.sum(dd * _shift_down(hh, row), axis=0, keepdims=True)
            dck_ref[1:2, :] = jnp.sum(dd * hh, axis=0, keepdims=True)
            dck_ref[2:3, :] = jnp.sum(dd * _shift_up(hh, row, S), axis=0, keepdims=True)
            dh = _shift_up(dd, row, S) * ck_[0:1] + dd * ck_[1:2] + _shift_down(dd, row) * ck_[2:3]
            dh_ref[...] = dh.astype(BF16)

    ca = lambda j: (0, j)
    cbi = lambda j: (0, j + nt)
    col = pl.BlockSpec((S, LANES), ca)
    return pl.pallas_call(
        body, name=name,
        out_shape=(jax.ShapeDtypeStruct((S, F), BF16), jax.ShapeDtypeStruct((S, F), BF16),
                   jax.ShapeDtypeStruct((3, F), F32), jax.ShapeDtypeStruct((3, F), F32),
                   jax.ShapeDtypeStruct((1, F), F32), jax.ShapeDtypeStruct((1, F), F32)),
        grid=(nt,),
        in_specs=[col, pl.BlockSpec((S, LANES), cbi), pl.BlockSpec((3, LANES), ca), pl.BlockSpec((3, LANES), cbi),
                  pl.BlockSpec((1, LANES), ca), pl.BlockSpec((1, LANES), cbi), col],
        out_specs=(col, col, pl.BlockSpec((3, LANES), ca), pl.BlockSpec((3, LANES), ca),
                   pl.BlockSpec((1, LANES), ca), pl.BlockSpec((1, LANES), ca)),
        compiler_params=_cp(("parallel",)),
    )(h, h, ck, ck, cb, cb, dact)


def _layer_fwd(x, xb, memb, w, shards, tabs, seg, l):
    n = lambda s: f"L{l}_{s}"
    w = dict(w)
    qg2 = jnp.tile(w["b_q_gain"], 2)[None, :]
    kg2 = jnp.tile(w["b_k_gain"], 2)[None, :]
    proj = _mm(xb, w["w_in"], tb=True, name=n("proj"))
    aq, ak4, av4, bq, bk4, bv4 = _prep(proj, tabs, qg2, kg2, seg, name=n("prep"))
    oa, lse_a = _attn_win_fwd(aq, ak4, av4, w["a_sink"], name=n("attn_win"))
    gather = [(shards[k], l) for k in GATHERED_LATE] + ([(shards["w_in"], l + 1)] if l + 1 < DEPTH else [])
    ob, lse_b, gathered = _attn_dense_fwd(bq, bk4, bv4, gather=gather, name=n("attn_dense"))
    for k, g in zip(GATHERED_LATE, gathered):
        w[k] = _unshard(g, GATHER_AXIS[k])
    w_in_next = gathered[len(GATHERED_LATE)] if l + 1 < DEPTH else None
    oc = _gmlp_fwd(proj, w["c_ws"], w["c_bs3"], w["c_ln_g"], w["c_ln_b"], name=n("gmlp"))
    merged = _merge_fwd(oa, ob, oc, w["w_branch"], proj, w["b_gate"], name=n("merge"))
    x1, x1b, xh1, rs1 = _mm_res_ln(merged, w["w_mix_out"], x, w["ln1_g"], w["ln1_b"], name=n("mix_ln1"))
    xq = _mm(x1b, w["x_wq"], out_dtype=BF16, name=n("xq"))
    xkv = _mm(memb, w["x_wkv"], out_dtype=BF16, name=n("xkv"))
    xo, lse_x = _xattn_fwd(xq, xkv, name=n("xattn"))
    x2, x2b, xh2, rs2 = _mm_res_ln(xo, w["x_wo"], x1, w["ln2_g"], w["ln2_b"], name=n("xo_ln2"))
    h = _mm(x2b, w["f_w_up"], tb=True, name=n("ffn_up"))
    act = _conv_act_fwd(h, w["f_conv_k"], w["f_conv_b"], name=n("conv_act"))
    x3, x3b, xh3, rs3 = _mm_res_ln(act, w["f_w_down"], x2, w["ln3_g"], w["ln3_b"], name=n("down_ln3"))
    saved = dict(xb=xb, proj=proj, aq=aq, ak4=ak4, av4=av4, bq=bq, bk4=bk4, bv4=bv4, lse_a=lse_a, lse_b=lse_b,
                 oa=oa, ob=ob, oc=oc, merged=merged, xh1=xh1, rs1=rs1, x1b=x1b, xq=xq, xkv=xkv, xo=xo, lse_x=lse_x,
                 xh2=xh2, rs2=rs2, x2b=x2b, h=h, act=act, xh3=xh3, rs3=rs3, qg2=qg2, kg2=kg2)
    return x3, x3b, saved, w, w_in_next


def _layer_bwd(dy, memb, w, sv, tabs, seg, l, dw_in_above, recv):
    n = lambda s: f"L{l}_{s}"
    g = {}
    big = {}
    recv = dict(recv)

    def dw(key, a, b, tag):
        big[key] = _mm(a, b, ta=True, out_dtype=BF16, name=n(tag))

    def dw_t(key, segments, x, tag):
        buf = jnp.zeros((sum(s.shape[1] for s in segments), x.shape[1]), BF16)
        row = 0
        for i, s in enumerate(segments):
            buf = _mm(s, x, ta=True, into=(buf, row), name=n(f"{tag}{i}"))
            row += s.shape[1]
        big[key] = buf

    dz3, dz3b, g["ln3_g"], g["ln3_b"] = _ln_bwd(dy, sv["xh3"], sv["rs3"], w["ln3_g"], name=n("ln3_bwd"))
    dw("f_w_down", sv["act"], dz3b, "dw_down")
    dact = _mm(dz3b, w["f_w_down"], tb=True, name=n("dact"))
    dha, dhb, dcka, dckb, dcba, dcbb = _conv_act_bwd(sv["h"], w["f_conv_k"], w["f_conv_b"], dact, name=n("conv_act_bwd"))
    g["f_conv_k"] = jnp.concatenate([dcka, dckb], axis=1)
    g["f_conv_b"] = jnp.concatenate([dcba, dcbb], axis=1)[0]
    dw_t("f_w_up", [dha, dhb], sv["x2b"], "dw_up")
    dx2 = _mm([dha, dhb], w["f_w_up"], res=dz3, res_scale=ALPHA, name=n("dx2"))
    dz2, dz2b, g["ln2_g"], g["ln2_b"] = _ln_bwd(dx2, sv["xh2"], sv["rs2"], w["ln2_g"], name=n("ln2_bwd"))
    dw("x_wo", sv["xo"], dz2b, "dw_xo")
    dxo = _mm(dz2b, w["x_wo"], tb=True, out_dtype=BF16, name=n("dxo"))
    dxq, dxkv = _xattn_bwd(sv["xq"], sv["xkv"], sv["lse_x"], dxo, name=n("xattn_bwd"))
    dw("x_wq", sv["x1b"], dxq, "dw_xq")
    dw("x_wkv", memb, dxkv, "dw_xkv")
    dx1 = _mm(dxq, w["x_wq"], tb=True, res=dz2, res_scale=ALPHA, name=n("dx1"))
    dz1, dz1b, g["ln1_g"], g["ln1_b"] = _ln_bwd(dx1, sv["xh1"], sv["rs1"], w["ln1_g"], name=n("ln1_bwd"))
    dw("w_mix_out", sv["merged"], dz1b, "dw_mix")
    dmerged = _mm(dz1b, w["w_mix_out"], tb=True, name=n("dmerged"))
    dgl, dbg, dt, dbr = _merge_bwd(sv["oa"], sv["ob"], sv["oc"], w["w_branch"], sv["proj"], w["b_gate"], dmerged,
                                   name=n("merge_bwd"))
    g["b_gate"] = dbg[0]
    for i, k in enumerate(("oa", "ob", "oc")):
        dw(f"w_branch{i}", sv[k], dt[i], f"dw_branch{i}")
    big["w_branch"] = jnp.stack([big.pop(f"w_branch{i}") for i in range(N_BRANCH)])
    dqa, dka, dva, dsink = _attn_win_bwd(sv["aq"], sv["ak4"], sv["av4"], w["a_sink"], sv["lse_a"], dbr[0],
                                         name=n("attn_win_bwd"))
    g["a_sink"] = dsink[:, 0]
    sent = [k for k in BIG if k != "w_in"]
    scatter = [(_reshard(big[k], BIG_AXIS[k]), recv[k], l) for k in sent]
    if dw_in_above is not None:
        sent.append("w_in")
        scatter.append((_reshard(dw_in_above, BIG_AXIS["w_in"]), recv["w_in"], l + 1))
    dqb, dkb, dvb, got = _attn_dense_bwd(sv["bq"], sv["bk4"], sv["bv4"], sv["lse_b"], dbr[1], scatter=scatter,
                                         name=n("attn_dense_bwd"))
    recv.update(zip(sent, got))
    dcz, g["c_ws"], dbs3, dlg, dlb = _gmlp_bwd(sv["proj"], dbr[2], w["c_ws"], w["c_bs3"], w["c_ln_g"], w["c_ln_b"],
                                               name=n("gmlp_bwd"))
    g["c_bs"] = dbs3[:, :, 0]
    g["c_ln_g"], g["c_ln_b"] = dlg[0], dlb[0]
    dqkv, dqg, dkg = _unprep(dqa, dka, dva, dqb, dkb, dvb, sv["proj"], tabs, sv["qg2"], sv["kg2"], seg, name=n("unprep"))
    g["b_q_gain"] = dqg[0, :HEAD_DIM] + dqg[0, HEAD_DIM:]
    g["b_k_gain"] = dkg[0, :HEAD_DIM] + dkg[0, HEAD_DIM:]
    dw_t("w_in", [dqkv, dcz, dgl], sv["xb"], "dw_in")
    dx0 = _mm([dqkv, dcz, dgl], w["w_in"], res=dz1, res_scale=ALPHA, name=n("dx0"))
    for k in ("ln1_g", "ln1_b", "ln2_g", "ln2_b", "ln3_g", "ln3_b"):
        g[k] = g[k][0]
    return dx0, g, big["w_in"], recv


WEIGHTS = ("w_in", "b_gate", "a_sink", "b_q_gain", "b_k_gain", "c_ln_g", "c_ln_b", "c_ws", "c_bs", "w_branch",
           "w_mix_out", "ln1_g", "ln1_b", "x_wq", "x_wkv", "x_wo", "ln2_g", "ln2_b", "f_w_up", "f_conv_k",
           "f_conv_b", "f_w_down", "ln3_g", "ln3_b")
TRANSPOSED = ("w_in", "f_w_up")
BIG_AXIS = {"w_in": 0, "w_branch": 2, "w_mix_out": 0, "x_wq": 0, "x_wkv": 0, "x_wo": 1, "f_w_up": 0, "f_w_down": 0}
BIG = tuple(BIG_AXIS)
GATHERED = BIG + ("f_conv_k",)
GATHERED_LATE = tuple(k for k in GATHERED if k != "w_in")
GATHER_AXIS = dict(BIG_AXIS, f_conv_k=1)
SMALL = tuple(k for k in WEIGHTS if k not in GATHERED)


def _unshard(g, axis):
    t = jnp.moveaxis(g, 0, axis)
    return t.reshape(t.shape[:axis] + (t.shape[axis] * t.shape[axis + 1],) + t.shape[axis + 2:])


def _reshard(full, axis):
    t = full.reshape(full.shape[:axis] + (N_DEV, full.shape[axis] // N_DEV) + full.shape[axis + 1:])
    return jnp.moveaxis(t, axis, 0)


def _small_weights(small, l):
    w = {k: v[l] for k, v in small.items()}
    for k in ("c_ln_g", "c_ln_b", "ln1_g", "ln1_b", "ln2_g", "ln2_b", "ln3_g", "ln3_b", "b_gate", "f_conv_b"):
        w[k] = w[k][None, :]
    w["c_bs3"] = w["c_bs"][:, :, None]
    w["c_ws"] = w["c_ws"].astype(BF16)
    return w


def _local_step(x, mem, target, small, shards):
    S = x.shape[0]
    tabs = _rope_tables(S)
    seg = _seg_matrix()
    memb = mem.astype(BF16)
    xb = x.astype(BF16)
    saved, weights = [], []
    w_in_g = _gather_call([(shards["w_in"], 0)], name="gather_w_in_L0")[0]
    for l in range(DEPTH):
        w = dict(_small_weights(small, l), w_in=_unshard(w_in_g, GATHER_AXIS["w_in"]))
        x, xb, sv, w, w_in_g = _layer_fwd(x, xb, memb, w, shards, tabs, seg, l)
        saved.append(sv)
        weights.append(w)
    dy, loss = _loss_head(x, target, name="loss_head")
    grads = [None] * DEPTH
    recv = {k: jax.ShapeDtypeStruct((DEPTH, N_DEV) + shards[k].shape[1:], BF16) for k in BIG}
    dw_in = None
    for l in reversed(range(DEPTH)):
        dy, grads[l], dw_in, recv = _layer_bwd(dy, memb, weights[l], saved[l], tabs, seg, l, dw_in, recv)
    recv["w_in"] = _scatter_call([(_reshard(dw_in, BIG_AXIS["w_in"]), recv["w_in"], 0)], name="scatter_w_in_L0")[0]
    return loss, dy, grads, [recv[k] for k in BIG]


PACK_W = 1024


def _gather_call(gather, *, name):
    na = len(gather)

    def body(*refs):
        start, finish = _gather_plan([(refs[a], gather[a][1], refs[na + a]) for a in range(na)], *refs[2 * na:])
        start()
        finish()

    return list(pl.pallas_call(
        body, name=name,
        out_shape=[_gathered_shape(x) for x, _ in gather],
        in_specs=[_ANY] * na, out_specs=[_ANY] * na,
        scratch_shapes=_comm_scratch(na),
    )(*[x for x, _ in gather]))


def _scatter_io(scatter):
    held = [a for a, (_, r, _) in enumerate(scatter) if not isinstance(r, jax.ShapeDtypeStruct)]
    return ([s for s, _, _ in scatter] + [scatter[a][1] for a in held],
            [jax.ShapeDtypeStruct(r.shape, r.dtype) for _, r, _ in scatter], held)


def _scatter_call(scatter, *, name):
    na = len(scatter)
    operands, out_shape, held = _scatter_io(scatter)
    n_in = len(operands)

    def body(*refs):
        start, finish = _scatter_plan([(refs[a], refs[n_in + a], scatter[a][2]) for a in range(na)],
                                      *refs[n_in + na:])
        start()
        finish()

    return list(pl.pallas_call(
        body, name=name,
        out_shape=out_shape,
        in_specs=[_ANY] * n_in, out_specs=[_ANY] * na,
        scratch_shapes=_comm_scratch(na),
        input_output_aliases={na + i: a for i, a in enumerate(held)},
    )(*operands))


def _sum_parts(parts, *, name):
    P, R, C = parts.shape
    tr = _tile(R, 64, align=8)

    def body(p_ref, o_ref):
        g = p_ref[0].astype(F32)
        for s in range(1, P):
            g = g + p_ref[s].astype(F32)
        o_ref[...] = g

    return pl.pallas_call(
        body, name=name, out_shape=jax.ShapeDtypeStruct((R, C), F32), grid=(R // tr,),
        in_specs=[pl.BlockSpec((P, tr, C), lambda i: (0, i, 0))], out_specs=pl.BlockSpec((tr, C), lambda i: (i, 0)),
        compiler_params=_cp(("parallel",)),
    )(parts)


ADAM_BLOCK_ELEMS = 512 * 1024


def _adamw(parts, w, m, v, *, name):
    L, P, R, C = parts.shape
    assert w.shape == (L, R, C), (parts.shape, w.shape)
    tr = _tile(R, max(16, ADAM_BLOCK_ELEMS // C), align=16)

    def body(p_ref, w_ref, m_ref, v_ref, g_ref, d_ref, nm_ref, nv_ref):
        g = p_ref[0].astype(F32)
        for s in range(1, P):
            g = g + p_ref[s].astype(F32)
        nm = ADAM_B1 * m_ref[...] + (1.0 - ADAM_B1) * g
        nv = ADAM_B2 * v_ref[...] + (1.0 - ADAM_B2) * (g * g)
        m_hat = nm / (1.0 - ADAM_B1 ** ADAM_STEP)
        v_hat = nv / (1.0 - ADAM_B2 ** ADAM_STEP)
        g_ref[...] = g
        d_ref[...] = -ADAM_LR * (m_hat / (jnp.sqrt(v_hat) + ADAM_EPS) + ADAM_WD * w_ref[...])
        nm_ref[...] = nm
        nv_ref[...] = nv

    blk = pl.BlockSpec((None, tr, C), lambda l, i: (l, i, 0))
    shp = jax.ShapeDtypeStruct((L, R, C), F32)
    return pl.pallas_call(
        body, name=name, out_shape=(shp, shp, shp, shp), grid=(L, R // tr),
        in_specs=[pl.BlockSpec((None, P, tr, C), lambda l, i: (l, 0, i, 0)), blk, blk, blk],
        out_specs=(blk, blk, blk, blk),
        compiler_params=_cp(("parallel", "parallel")),
    )(parts, w, m, v)


def _pad_rows(vec, width, row_align):
    n = vec.shape[0]
    rows = -(-n // width)
    rows = -(-rows // row_align) * row_align
    return jnp.pad(vec, (0, rows * width - n)).reshape(rows, width)


def kernel(x, mem, w_in, b_gate, a_sink, b_q_gain, b_k_gain, c_ln_g, c_ln_b, c_ws, c_bs, w_branch, w_mix_out, ln1_g, ln1_b, x_wq, x_wkv, x_wo, ln2_g, ln2_b, f_w_up, f_conv_k, f_conv_b, f_w_down, ln3_g, ln3_b, loss_target, m_w_in, m_b_gate, m_a_sink, m_b_q_gain, m_b_k_gain, m_c_ln_g, m_c_ln_b, m_c_ws, m_c_bs, m_w_branch, m_w_mix_out, m_ln1_g, m_ln1_b, m_x_wq, m_x_wkv, m_x_wo, m_ln2_g, m_ln2_b, m_f_w_up, m_f_conv_k, m_f_conv_b, m_f_w_down, m_ln3_g, m_ln3_b, v_w_in, v_b_gate, v_a_sink, v_b_q_gain, v_b_k_gain, v_c_ln_g, v_c_ln_b, v_c_ws, v_c_bs, v_w_branch, v_w_mix_out, v_ln1_g, v_ln1_b, v_x_wq, v_x_wkv, v_x_wo, v_ln2_g, v_ln2_b, v_f_w_up, v_f_conv_k, v_f_conv_b, v_f_w_down, v_ln3_g, v_ln3_b):
    w = dict(w_in=w_in, b_gate=b_gate, a_sink=a_sink, b_q_gain=b_q_gain, b_k_gain=b_k_gain, c_ln_g=c_ln_g,
             c_ln_b=c_ln_b, c_ws=c_ws, c_bs=c_bs, w_branch=w_branch, w_mix_out=w_mix_out, ln1_g=ln1_g, ln1_b=ln1_b,
             x_wq=x_wq, x_wkv=x_wkv, x_wo=x_wo, ln2_g=ln2_g, ln2_b=ln2_b, f_w_up=f_w_up, f_conv_k=f_conv_k,
             f_conv_b=f_conv_b, f_w_down=f_w_down, ln3_g=ln3_g, ln3_b=ln3_b)
    m = dict(w_in=m_w_in, b_gate=m_b_gate, a_sink=m_a_sink, b_q_gain=m_b_q_gain, b_k_gain=m_b_k_gain,
             c_ln_g=m_c_ln_g, c_ln_b=m_c_ln_b, c_ws=m_c_ws, c_bs=m_c_bs, w_branch=m_w_branch, w_mix_out=m_w_mix_out,
             ln1_g=m_ln1_g, ln1_b=m_ln1_b, x_wq=m_x_wq, x_wkv=m_x_wkv, x_wo=m_x_wo, ln2_g=m_ln2_g, ln2_b=m_ln2_b,
             f_w_up=m_f_w_up, f_conv_k=m_f_conv_k, f_conv_b=m_f_conv_b, f_w_down=m_f_w_down, ln3_g=m_ln3_g,
             ln3_b=m_ln3_b)
    v = dict(w_in=v_w_in, b_gate=v_b_gate, a_sink=v_a_sink, b_q_gain=v_b_q_gain, b_k_gain=v_b_k_gain,
             c_ln_g=v_c_ln_g, c_ln_b=v_c_ln_b, c_ws=v_c_ws, c_bs=v_c_bs, w_branch=v_w_branch, w_mix_out=v_w_mix_out,
             ln1_g=v_ln1_g, ln1_b=v_ln1_b, x_wq=v_x_wq, x_wkv=v_x_wkv, x_wo=v_x_wo, ln2_g=v_ln2_g, ln2_b=v_ln2_b,
             f_w_up=v_f_w_up, f_conv_k=v_f_conv_k, f_conv_b=v_f_conv_b, f_w_down=v_f_w_down, ln3_g=v_ln3_g,
             ln3_b=v_ln3_b)
    me = 4 * lax.axis_index("x") + 2 * lax.axis_index("y") + lax.axis_index("c")

    def held(k, t):
        return jnp.swapaxes(t, 1, 2) if k in TRANSPOSED else t

    shards = dict({k: held(k, w[k]).astype(BF16) for k in BIG}, f_conv_k=w["f_conv_k"])
    loss, grad_x, grads, recvs = _local_step(x[0], mem[0], loss_target[0], {k: w[k] for k in SMALL}, shards)
    loss = lax.psum(loss[0, 0], ("x", "y", "c"))

    out_g, out_d, out_m, out_v = {}, {}, {}, {}
    for k, recv in zip(BIG, recvs):
        shp = held(k, w[k]).shape
        rc = (DEPTH, math.prod(shp[1:-1]), shp[-1])
        parts = recv.reshape((DEPTH, N_DEV) + rc[1:])
        g_, d_, m_, v_ = _adamw(parts, held(k, w[k]).reshape(rc), held(k, m[k]).reshape(rc),
                                held(k, v[k]).reshape(rc), name=f"adamw_{k}")
        out_g[k], out_d[k], out_m[k], out_v[k] = (held(k, t.reshape(shp)) for t in (g_, d_, m_, v_))

    small_all = SMALL + ("f_conv_k",)
    gfull = {k: jnp.stack([grads[l][k] for l in range(DEPTH)]) for k in small_all}

    def pack(d):
        rows = jnp.concatenate([_pad_rows(d[k].reshape(-1), PACK_W, 1) for k in small_all])
        return jnp.pad(rows, ((0, -rows.shape[0] % 8), (0, 0)))

    def unpack(rows, like):
        out, r = {}, 0
        for k in small_all:
            nr = -(-like[k].size // PACK_W)
            out[k] = rows[r:r + nr].reshape(-1)[:like[k].size].reshape(like[k].shape)
            r += nr
        return out

    gathered = _gather_call([(pack(gfull)[None], 0)], name="gather_small_grads")[0]
    sg = unpack(_sum_parts(gathered, name="sum_small_grads"), gfull)
    width = w["f_conv_k"].shape[2]
    sg["f_conv_k"] = lax.dynamic_slice_in_dim(sg["f_conv_k"], me * width, width, axis=2)
    g_, d_, m_, v_ = _adamw(pack(sg)[None, None], pack(w)[None], pack(m)[None], pack(v)[None], name="adamw_small")
    ud, um, uv = (unpack(t[0], w) for t in (d_, m_, v_))
    for k in small_all:
        out_g[k], out_d[k], out_m[k], out_v[k] = sg[k], ud[k], um[k], uv[k]

    return (loss, grad_x[None], *[out_g[k] for k in WEIGHTS], *[out_d[k] for k in WEIGHTS],
            *[out_m[k] for k in WEIGHTS], *[out_v[k] for k in WEIGHTS])
```

```python
import functools
import math

import jax
import jax.numpy as jnp
from jax import lax
from jax.experimental import pallas as pl
from jax.experimental.pallas import tpu as pltpu

F32 = jnp.float32
BF16 = jnp.bfloat16

DEPTH = 4
HEAD_DIM = 64
BLOCK = 128
WINDOW = 128
GRID_W = 64
C_WIDTH = 512
C_GROUPS = 4
CHUNK = 128
N_BRANCH = 3
BRANCH_WIDTH = 512
ROPE_THETA = 10000.0
X_HEADS = 4
X_HEAD_DIM = 128
ALPHA = (2 * DEPTH) ** 0.25
LN_EPS = 1e-5
RMS_EPS = 1e-6
ADAM_LR = 0.001
ADAM_B1 = 0.9
ADAM_B2 = 0.999
ADAM_EPS = 1e-08
ADAM_WD = 0.01
ADAM_STEP = 10
N_DEV = 8

COL_A = 0
COL_B = 768
COL_C = 1536
COL_GATE = 2560
QKV_W = 768

LANES = 128
V7X_VMEM_BYTES = 64 * 1024 * 1024
VMEM_LIMIT = V7X_VMEM_BYTES - 8 * 1024 * 1024
NEG_BIG = -1e30

_NT = (((1,), (1,)), ((), ()))
_TN = (((0,), (0,)), ((), ()))
_NN = (((1,), (0,)), ((), ()))


def _cp(sem=None):
    return pltpu.CompilerParams(dimension_semantics=sem, vmem_limit_bytes=VMEM_LIMIT)


def _tile(n, target, align=LANES):
    if n <= target:
        return n
    best = None
    for t in range(align, target + 1, align):
        if n % t == 0:
            best = t
    assert best is not None, (n, target)
    return best


def _dot(a, b, dims=_NN):
    return lax.dot_general(a, b, dims, preferred_element_type=F32)


def _gelu(x):
    return 0.5 * x * (1.0 + lax.erf(x * 0.7071067811865476))


def _gelu_and_grad(x):
    cdf = 0.5 * (1.0 + lax.erf(x * 0.7071067811865476))
    return x * cdf, cdf + x * jnp.exp(-0.5 * x * x) * 0.3989422804014327


def _sigmoid(x):
    return 1.0 / (1.0 + jnp.exp(-x))


MESH_ID = pl.DeviceIdType.MESH
_ANY = pl.BlockSpec(memory_space=pl.ANY)
COPIES_PER_ARRAY = N_DEV - 1


def _comm_scratch(n_arrays):
    return [pltpu.SemaphoreType.DMA((COPIES_PER_ARRAY * n_arrays,)),
            pltpu.SemaphoreType.DMA((COPIES_PER_ARRAY * n_arrays,)), pltpu.SemaphoreType.DMA((n_arrays,))]


def _gathered_shape(x):
    return jax.ShapeDtypeStruct((N_DEV,) + x.shape[1:], x.dtype)


def _gather_plan(entries, send_sems, recv_sems, local_sems):
    mx, my, mc = lax.axis_index("x"), lax.axis_index("y"), lax.axis_index("c")
    me, sibling = (mx, my, mc), (mx, my, 1 - mc)
    chips = [(1 - mx, my), (mx, 1 - my), (1 - mx, 1 - my)]

    def copy(a, k, block, to, from_shard=False):
        x_ref, l, out_ref = entries[a]
        dst = out_ref.at[4 * block[0] + 2 * block[1] + block[2]]
        return pltpu.make_async_remote_copy(
            src_ref=x_ref.at[l] if from_shard else dst, dst_ref=dst,
            send_sem=send_sems.at[COPIES_PER_ARRAY * a + k], recv_sem=recv_sems.at[COPIES_PER_ARRAY * a + k],
            device_id=to, device_id_type=MESH_ID)

    def own(a):
        x_ref, l, out_ref = entries[a]
        return pltpu.make_async_copy(x_ref.at[l], out_ref.at[4 * mx + 2 * my + mc], local_sems.at[a])

    def first(a):
        return [copy(a, 0, me, sibling, True)] + [copy(a, 1 + j, me, (*chip, mc), True) for j, chip in enumerate(chips)]

    def passed(a):
        return [copy(a, 4 + j, (*chip, mc), sibling) for j, chip in enumerate(chips)]

    def start():
        for a in range(len(entries)):
            own(a).start()
            for cp in first(a):
                cp.start()

    def finish():
        for a in range(len(entries)):
            fwd = passed(a)
            for j, chip in enumerate(chips):
                copy(a, 1 + j, (*chip, mc), me).wait_recv()
                fwd[j].start()
        for a in range(len(entries)):
            copy(a, 0, sibling, me).wait_recv()
            for j, chip in enumerate(chips):
                copy(a, 4 + j, (*chip, 1 - mc), me).wait_recv()
            for cp in first(a) + passed(a):
                cp.wait_send()
            own(a).wait()

    return start, finish


def _scatter_plan(entries, send_sems, recv_sems, local_sems):
    mx, my, mc = lax.axis_index("x"), lax.axis_index("y"), lax.axis_index("c")
    me = 4 * mx + 2 * my + mc

    def src(a, dev):
        return entries[a][0].at[dev]

    def copies(a):
        _, recv_ref, lr = entries[a]
        out = []
        for k in range(1, N_DEV):
            px = 1 - mx if k & 4 else mx
            py = 1 - my if k & 2 else my
            pc = 1 - mc if k & 1 else mc
            peer = 4 * px + 2 * py + pc
            sems = dict(send_sem=send_sems.at[COPIES_PER_ARRAY * a + k - 1],
                        recv_sem=recv_sems.at[COPIES_PER_ARRAY * a + k - 1],
                        device_id=(px, py, pc), device_id_type=MESH_ID)
            sends = pltpu.make_async_remote_copy(src_ref=src(a, peer), dst_ref=recv_ref.at[lr, me], **sems)
            lands = pltpu.make_async_remote_copy(src_ref=src(a, me), dst_ref=recv_ref.at[lr, peer], **sems)
            out.append((sends, lands))
        return out

    def own(a):
        _, recv_ref, lr = entries[a]
        return pltpu.make_async_copy(src(a, me), recv_ref.at[lr, me], local_sems.at[a])

    def start():
        for a in range(len(entries)):
            own(a).start()
            for sends, _ in copies(a):
                sends.start()

    def finish():
        for a in range(len(entries)):
            for _, lands in copies(a):
                lands.wait_recv()
        for a in range(len(entries)):
            for sends, _ in copies(a):
                sends.wait_send()
            own(a).wait()

    return start, finish


MM_TILE, MM_TK = 1536, 2048
MM_TILE_LN = 512


def _mm(a, b, *, ta=False, tb=False, out_dtype=F32, res=None, res_scale=1.0, into=None, ln_bwd=None, name):
    segs = list(a) if isinstance(a, (list, tuple)) else [a]
    if ta:
        (K, M), seg_k = segs[0].shape, [segs[0].shape[0]]
        assert len(segs) == 1
    else:
        M, seg_k = segs[0].shape[0], [s.shape[1] for s in segs]
        K = sum(seg_k)
    if tb:
        N, Kb = b.shape
    else:
        Kb, N = b.shape
    assert K == Kb, ([s.shape for s in segs], b.shape, ta, tb)
    row_off = into[1] if into is not None else 0
    tm, tn = _tile(math.gcd(M, row_off), MM_TILE if ln_bwd is None else MM_TILE_LN), _tile(N, MM_TILE)
    tk = _tile(K, MM_TK) if len(segs) == 1 else _tile(math.gcd(*seg_k), MM_TILE)
    nk = K // tk
    seg_chunks = [ks // tk for ks in seg_k]
    seg_first = [sum(seg_chunks[:s]) for s in range(len(segs))]
    dims = (((0 if ta else 1,), (1 if tb else 0,)), ((), ()))
    ns = len(segs)
    n_res = ns + 1
    n_ln = n_res + (res is not None)
    n_in = n_ln + (3 if ln_bwd is not None else 0) + (into is not None)
    assert ln_bwd is None or (tn == N and into is None)

    def body(*refs):
        a_refs, b_ref = refs[:ns], refs[ns]
        r_ref = refs[n_res] if res is not None else None
        o_ref = refs[n_in]
        first_row_tile = pl.program_id(0) == 0

        def finish(out):
            if r_ref is not None:
                out = out + res_scale * r_ref[...]
            if ln_bwd is None:
                o_ref[...] = out.astype(o_ref.dtype)
                return
            xh_ref, rs_ref, g_ref = refs[n_ln:n_ln + 3]
            ob_ref, dg_ref, db_ref = refs[n_in + 1:n_in + 4]

            @pl.when(first_row_tile)
            def _():
                dg_ref[...] = jnp.zeros_like(dg_ref)
                db_ref[...] = jnp.zeros_like(db_ref)

            xh = xh_ref[...]
            dxh = out * g_ref[...]
            m1 = jnp.mean(dxh, axis=-1, keepdims=True)
            m2 = jnp.mean(dxh * xh, axis=-1, keepdims=True)
            dz = rs_ref[...] * (dxh - m1 - xh * m2)
            o_ref[...] = dz
            ob_ref[...] = dz.astype(BF16)
            dg_ref[...] += jnp.sum(out * xh, axis=0, keepdims=True)
            db_ref[...] += jnp.sum(out, axis=0, keepdims=True)

        def prod(s):
            return _dot(a_refs[s][...].astype(BF16), b_ref[...].astype(BF16), dims)

        if nk == 1:
            finish(prod(0))
            return
        acc = refs[n_in + (4 if ln_bwd is not None else 1)]
        k = pl.program_id(2)

        @pl.when(k == 0)
        def _():
            acc[...] = jnp.zeros_like(acc)

        for s in range(ns):
            def add(s=s):
                acc[...] += prod(s)
            pl.when((k >= seg_first[s]) & (k < seg_first[s] + seg_chunks[s]))(add)

        @pl.when(k == nk - 1)
        def _():
            finish(acc[...])

    if ta:
        a_specs = [pl.BlockSpec((tk, tm), lambda i, j, k: (k, i))]
    else:
        a_specs = [pl.BlockSpec((tm, tk), functools.partial(
            lambda i, j, k, first, n: (i, jnp.clip(k - first, 0, n - 1)), first=seg_first[s], n=seg_chunks[s]))
            for s in range(ns)]
    b_spec = pl.BlockSpec((tn, tk), lambda i, j, k: (j, k)) if tb else pl.BlockSpec((tk, tn), lambda i, j, k: (k, j))
    in_specs = a_specs + [b_spec]
    args = segs + [b]
    if res is not None:
        in_specs.append(pl.BlockSpec((tm, tn), lambda i, j, k: (i, j)))
        args.append(res)
    out_spec = pl.BlockSpec((tm, tn), lambda i, j, k: (i, j))
    if ln_bwd is not None:
        xh, rs, g = ln_bwd
        in_specs += [out_spec, pl.BlockSpec((tm, 1), lambda i, j, k: (i, 0)), pl.BlockSpec((1, tn), lambda i, j, k: (0, j))]
        args += [xh, rs, g]
        vec = pl.BlockSpec((1, tn), lambda i, j, k: (0, j))
        return pl.pallas_call(
            body, name=name,
            out_shape=(jax.ShapeDtypeStruct((M, N), F32), jax.ShapeDtypeStruct((M, N), BF16),
                       jax.ShapeDtypeStruct((1, N), F32), jax.ShapeDtypeStruct((1, N), F32)),
            grid=(M // tm, N // tn, nk),
            in_specs=in_specs,
            out_specs=(out_spec, out_spec, vec, vec),
            scratch_shapes=[pltpu.VMEM((tm, tn), F32)] if nk > 1 else [],
            compiler_params=_cp(("arbitrary", "arbitrary", "arbitrary")),
        )(*args)
    if into is None:
        out_shape = jax.ShapeDtypeStruct((M, N), out_dtype)
        blk_off, aliases = 0, {}
    else:
        buf = into[0]
        assert buf.shape[1] == N and row_off % tm == 0 and row_off + M <= buf.shape[0], (buf.shape, M, N, row_off)
        out_shape = jax.ShapeDtypeStruct(buf.shape, buf.dtype)
        blk_off, aliases = row_off // tm, {n_in - 1: 0}
        in_specs.append(_ANY)
        args.append(buf)
    return pl.pallas_call(
        body, name=name,
        out_shape=out_shape,
        grid=(M // tm, N // tn, nk),
        in_specs=in_specs,
        out_specs=pl.BlockSpec((tm, tn), lambda i, j, k: (i + blk_off, j)),
        scratch_shapes=[pltpu.VMEM((tm, tn), F32)] if nk > 1 else [],
        input_output_aliases=aliases,
        compiler_params=_cp(("parallel", "parallel", "arbitrary")),
    )(*args)


def _mm_res_ln(a, w, x, g, b, *, name):
    S, K = a.shape
    D = w.shape[1]
    tm = _tile(S, 256)

    def body(a_ref, w_ref, x_ref, g_ref, b_ref, y_ref, yb_ref, xh_ref, rs_ref):
        h = _dot(a_ref[...], w_ref[...])
        z = ALPHA * x_ref[...] + h
        mu = jnp.mean(z, axis=-1, keepdims=True)
        zc = z - mu
        var = jnp.mean(zc * zc, axis=-1, keepdims=True)
        r = lax.rsqrt(var + LN_EPS)
        xh = zc * r
        y = xh * g_ref[...] + b_ref[...]
        y_ref[...] = y
        yb_ref[...] = y.astype(BF16)
        xh_ref[...] = xh
        rs_ref[...] = r

    row = lambda i: (i, 0)
    full = lambda i: (0, 0)
    return pl.pallas_call(
        body, name=name,
        out_shape=(jax.ShapeDtypeStruct((S, D), F32), jax.ShapeDtypeStruct((S, D), BF16),
                   jax.ShapeDtypeStruct((S, D), F32), jax.ShapeDtypeStruct((S, 1), F32)),
        grid=(S // tm,),
        in_specs=[pl.BlockSpec((tm, K), row), pl.BlockSpec((K, D), full), pl.BlockSpec((tm, D), row),
                  pl.BlockSpec((1, D), full), pl.BlockSpec((1, D), full)],
        out_specs=(pl.BlockSpec((tm, D), row), pl.BlockSpec((tm, D), row), pl.BlockSpec((tm, D), row),
                   pl.BlockSpec((tm, 1), row)),
        compiler_params=_cp(("parallel",)),
    )(a, w, x, g, b)


def _ln_bwd(dy, xh, rs, g, *, name):
    S, D = dy.shape
    tm = _tile(S, 256)

    def body(dy_ref, xh_ref, rs_ref, g_ref, dz_ref, dzb_ref, dg_ref, db_ref):
        @pl.when(pl.program_id(0) == 0)
        def _():
            dg_ref[...] = jnp.zeros_like(dg_ref)
            db_ref[...] = jnp.zeros_like(db_ref)

        dy = dy_ref[...]
        xh = xh_ref[...]
        dxh = dy * g_ref[...]
        m1 = jnp.mean(dxh, axis=-1, keepdims=True)
        m2 = jnp.mean(dxh * xh, axis=-1, keepdims=True)
        dz = rs_ref[...] * (dxh - m1 - xh * m2)
        dz_ref[...] = dz
        dzb_ref[...] = dz.astype(BF16)
        dg_ref[...] += jnp.sum(dy * xh, axis=0, keepdims=True)
        db_ref[...] += jnp.sum(dy, axis=0, keepdims=True)

    row = lambda i: (i, 0)
    full = lambda i: (0, 0)
    return pl.pallas_call(
        body, name=name,
        out_shape=(jax.ShapeDtypeStruct((S, D), F32), jax.ShapeDtypeStruct((S, D), BF16),
                   jax.ShapeDtypeStruct((1, D), F32), jax.ShapeDtypeStruct((1, D), F32)),
        grid=(S // tm,),
        in_specs=[pl.BlockSpec((tm, D), row), pl.BlockSpec((tm, D), row), pl.BlockSpec((tm, 1), row),
                  pl.BlockSpec((1, D), full)],
        out_specs=(pl.BlockSpec((tm, D), row), pl.BlockSpec((tm, D), row), pl.BlockSpec((1, D), full),
                   pl.BlockSpec((1, D), full)),
        compiler_params=_cp(("arbitrary",)),
    )(dy, xh, rs, g)


def _loss_head(y, t, *, name):
    S, D = y.shape
    tm = _tile(S, 512)

    def body(y_ref, t_ref, dy_ref, l_ref):
        @pl.when(pl.program_id(0) == 0)
        def _():
            l_ref[...] = jnp.zeros_like(l_ref)

        e = y_ref[...] - t_ref[...]
        dy_ref[...] = e / D
        l_ref[...] += 0.5 * jnp.sum(jnp.mean(e * e, axis=-1, keepdims=True), axis=0, keepdims=True)

    row = lambda i: (i, 0)
    return pl.pallas_call(
        body, name=name,
        out_shape=(jax.ShapeDtypeStruct((S, D), F32), jax.ShapeDtypeStruct((1, 1), F32)),
        grid=(S // tm,),
        in_specs=[pl.BlockSpec((tm, D), row), pl.BlockSpec((tm, D), row)],
        out_specs=(pl.BlockSpec((tm, D), row), pl.BlockSpec((1, 1), lambda i: (0, 0))),
        compiler_params=_cp(("arbitrary",)),
    )(y, t)


def _rope_tables(S):
    pos = jnp.arange(S, dtype=jnp.int32)
    row = pos // GRID_W
    col = pos % GRID_W

    def cs(p, d):
        half = d // 2
        inv = ROPE_THETA ** (-jnp.arange(half, dtype=F32) * (2.0 / d))
        ang = p.astype(F32)[:, None] * inv[None, :]
        c, s = jnp.cos(ang), jnp.sin(ang)
        return jnp.concatenate([c, c], -1), jnp.concatenate([-s, s], -1)

    ca, sa = cs(pos, HEAD_DIM)
    cr, sr = cs(row, HEAD_DIM // 2)
    cc, sc = cs(col, HEAD_DIM // 2)
    cb, sb = jnp.concatenate([cr, cc], -1), jnp.concatenate([sr, sc], -1)
    two = lambda t: jnp.concatenate([t, t], -1)
    return two(ca), two(sa), two(cb), two(sb)


def _partner(x, lane, width):
    h = width // 2
    return jnp.where(lane % width < h, pltpu.roll(x, LANES - h, 1), pltpu.roll(x, h, 1))


def _rope_fwd(x, c, s, lane, width):
    return x * c + _partner(x, lane, width) * s


def _rope_bwd(dy, c, s, lane, width):
    return dy * c + _partner(dy * s, lane, width)


def _head_sum(x, seg):
    return lax.dot_general(x, seg, _NN, precision=lax.Precision.HIGHEST, preferred_element_type=F32)


def _split_heads(x, lane):
    lo = lane < HEAD_DIM
    r = pltpu.roll(x, HEAD_DIM, 1)
    z = jnp.zeros_like(x)
    return jnp.where(lo, x, z), jnp.where(lo, z, r), jnp.where(lo, r, z), jnp.where(lo, z, x)


def _fold_heads(d0, d1, lane):
    t0 = d0 + pltpu.roll(d0, HEAD_DIM, 1)
    t1 = d1 + pltpu.roll(d1, HEAD_DIM, 1)
    return jnp.where(lane < HEAD_DIM, t0, t1)


def _seg_matrix():
    i = jnp.arange(LANES)
    return (i[:, None] // HEAD_DIM == i[None, :] // HEAD_DIM).astype(F32)


def _prep(proj, tabs, qg2, kg2, seg, *, name):
    S = proj.shape[0]
    ts = _tile(S, 256)
    ca, sa, cb, sb = tabs

    def body(pa_ref, pb_ref, ca_ref, sa_ref, cb_ref, sb_ref, qg_ref, kg_ref, seg_ref,
             aq_ref, ak_ref, av_ref, bq_ref, bk_ref, bv_ref):
        lane = lax.broadcasted_iota(jnp.int32, (ts, LANES), 1)
        ca, sa, cb, sb = ca_ref[...], sa_ref[...], cb_ref[...], sb_ref[...]
        seg = seg_ref[...]

        def norm(x, gain):
            r = lax.rsqrt(_head_sum(x * x, seg) * (1.0 / HEAD_DIM) + RMS_EPS)
            return x * r * gain

        def put(ref, x):
            for i, part in enumerate(_split_heads(x, lane)):
                ref[i] = part.astype(BF16)

        for gidx in range(4):
            cols = slice(gidx * LANES, (gidx + 1) * LANES)
            aq_ref[:, cols] = (_rope_fwd(pa_ref[:, cols], ca, sa, lane, HEAD_DIM) * 0.125).astype(BF16)
            bq = norm(pb_ref[:, cols], qg_ref[...])
            bq_ref[:, cols] = (_rope_fwd(bq, cb, sb, lane, HEAD_DIM // 2) * 0.125).astype(BF16)
        put(ak_ref, _rope_fwd(pa_ref[:, 512:640], ca, sa, lane, HEAD_DIM))
        put(av_ref, pa_ref[:, 640:768])
        bk = norm(pb_ref[:, 512:640], kg_ref[...])
        put(bk_ref, _rope_fwd(bk, cb, sb, lane, HEAD_DIM // 2))
        put(bv_ref, pb_ref[:, 640:768])

    row = lambda i: (i, 0)
    full = lambda i: (0, 0)
    tab = pl.BlockSpec((ts, LANES), row)
    kv_shape = jax.ShapeDtypeStruct((4, S, LANES), BF16)
    kv_spec = pl.BlockSpec((4, ts, LANES), lambda i: (0, i, 0))
    q_shape = jax.ShapeDtypeStruct((S, 512), BF16)
    q_spec = pl.BlockSpec((ts, 512), row)
    return pl.pallas_call(
        body, name=name,
        out_shape=(q_shape, kv_shape, kv_shape, q_shape, kv_shape, kv_shape),
        grid=(S // ts,),
        in_specs=[pl.BlockSpec((ts, QKV_W), lambda i: (i, 0)), pl.BlockSpec((ts, QKV_W), lambda i: (i, 1)),
                  tab, tab, tab, tab, pl.BlockSpec((1, LANES), full), pl.BlockSpec((1, LANES), full),
                  pl.BlockSpec((LANES, LANES), full)],
        out_specs=(q_spec, kv_spec, kv_spec, q_spec, kv_spec, kv_spec),
        compiler_params=_cp(("parallel",)),
    )(proj, proj, ca, sa, cb, sb, qg2, kg2, seg)


def _unprep(dqa, dka, dva, dqb, dkb, dvb, proj, tabs, qg2, kg2, seg, *, name):
    S = proj.shape[0]
    ts = _tile(S, 256)
    ca, sa, cb, sb = tabs

    def body(dqa_ref, dka_ref, dva_ref, dqb_ref, dkb_ref, dvb_ref, pb_ref, ca_ref, sa_ref, cb_ref, sb_ref,
             qg_ref, kg_ref, seg_ref, dp_ref, dqg_ref, dkg_ref):
        @pl.when(pl.program_id(0) == 0)
        def _():
            dqg_ref[...] = jnp.zeros_like(dqg_ref)
            dkg_ref[...] = jnp.zeros_like(dkg_ref)

        lane = lax.broadcasted_iota(jnp.int32, (ts, LANES), 1)
        ca, sa, cb, sb = ca_ref[...], sa_ref[...], cb_ref[...], sb_ref[...]
        seg = seg_ref[...]

        def norm_bwd(dy, x, gain):
            r = lax.rsqrt(_head_sum(x * x, seg) * (1.0 / HEAD_DIM) + RMS_EPS)
            gdy = gain * dy
            dot = _head_sum(gdy * x, seg) * (1.0 / HEAD_DIM)
            dx = r * gdy - x * (r * r * r) * dot
            return dx, jnp.sum(dy * x * r, axis=0, keepdims=True)

        for gidx in range(4):
            cols = slice(gidx * LANES, (gidx + 1) * LANES)
            dp_ref[:, cols] = _rope_bwd(dqa_ref[:, cols] * 0.125, ca, sa, lane, HEAD_DIM).astype(BF16)
            dbq = _rope_bwd(dqb_ref[:, cols] * 0.125, cb, sb, lane, HEAD_DIM // 2)
            dx, dg = norm_bwd(dbq, pb_ref[:, cols], qg_ref[...])
            dp_ref[:, COL_B + gidx * LANES:COL_B + (gidx + 1) * LANES] = dx.astype(BF16)
            dqg_ref[...] += dg
        dak = _fold_heads(dka_ref[0] + dka_ref[1], dka_ref[2] + dka_ref[3], lane)
        dp_ref[:, 512:640] = _rope_bwd(dak, ca, sa, lane, HEAD_DIM).astype(BF16)
        dp_ref[:, 640:768] = _fold_heads(dva_ref[0] + dva_ref[1], dva_ref[2] + dva_ref[3], lane).astype(BF16)
        dbk = _fold_heads(dkb_ref[0] + dkb_ref[1], dkb_ref[2] + dkb_ref[3], lane)
        dbk = _rope_bwd(dbk, cb, sb, lane, HEAD_DIM // 2)
        dx, dg = norm_bwd(dbk, pb_ref[:, 512:640], kg_ref[...])
        dp_ref[:, COL_B + 512:COL_B + 640] = dx.astype(BF16)
        dkg_ref[...] += dg
        dp_ref[:, COL_B + 640:COL_B + 768] = _fold_heads(dvb_ref[0] + dvb_ref[1], dvb_ref[2] + dvb_ref[3],
                                                         lane).astype(BF16)

    row = lambda i: (i, 0)
    full = lambda i: (0, 0)
    tab = pl.BlockSpec((ts, LANES), row)
    q_spec = pl.BlockSpec((ts, 512), row)
    kv_spec = pl.BlockSpec((4, ts, LANES), lambda i: (0, i, 0))
    return pl.pallas_call(
        body, name=name,
        out_shape=(jax.ShapeDtypeStruct((S, 2 * QKV_W), BF16), jax.ShapeDtypeStruct((1, LANES), F32),
                   jax.ShapeDtypeStruct((1, LANES), F32)),
        grid=(S // ts,),
        in_specs=[q_spec, kv_spec, kv_spec, q_spec, kv_spec, kv_spec,
                  pl.BlockSpec((ts, QKV_W), lambda i: (i, 1)), tab, tab, tab, tab,
                  pl.BlockSpec((1, LANES), full), pl.BlockSpec((1, LANES), full), pl.BlockSpec((LANES, LANES), full)],
        out_specs=(pl.BlockSpec((ts, 2 * QKV_W), row), pl.BlockSpec((1, LANES), full),
                   pl.BlockSpec((1, LANES), full)),
        compiler_params=_cp(("arbitrary",)),
    )(dqa, dka, dva, dqb, dkb, dvb, proj, ca, sa, cb, sb, qg2, kg2, seg)


def _attn_dense_fwd(q, k4, v4, *, gather=(), name):
    S = q.shape[0]
    tq = _tile(S, 256)
    xs = [x for x, _ in gather]
    na = len(xs)

    def body(q_ref, k_ref, v_ref, *rest):
        o_ref, lse_ref = rest[na], rest[na + 1]
        if na:
            x_refs, out_refs, sems = rest[:na], rest[na + 2:2 * na + 2], rest[2 * na + 2:]
            start, finish = _gather_plan([(x_refs[a], gather[a][1], out_refs[a]) for a in range(na)], *sems)
            pl.when((pl.program_id(0) == 0) & (pl.program_id(1) == 0))(start)
        for pr in range(2):
            qp = q_ref[:, pr * LANES:(pr + 1) * LANES]
            acc = None
            for half in range(2):
                s = _dot(qp, k_ref[half], _NT)
                m = jnp.max(s, axis=-1, keepdims=True)
                e = jnp.exp(s - m)
                l = jnp.sum(e, axis=-1, keepdims=True)
                pv = _dot(e.astype(BF16), v_ref[half]) * (1.0 / l)
                acc = pv if acc is None else acc + pv
                lse_ref[pr * 2 + half] = m + jnp.log(l)
            o_ref[:, pr * LANES:(pr + 1) * LANES] = acc.astype(BF16)
        if na:
            pl.when((pl.program_id(0) == 1) & (pl.program_id(1) == S // tq - 1))(finish)

    kv_spec = pl.BlockSpec((2, S, LANES), lambda kv, i: (kv, 0, 0))
    res = pl.pallas_call(
        body, name=name,
        out_shape=(jax.ShapeDtypeStruct((S, 512), BF16), jax.ShapeDtypeStruct((8, S, 1), F32),
                   *[_gathered_shape(x) for x in xs]),
        grid=(2, S // tq),
        in_specs=[pl.BlockSpec((tq, 256), lambda kv, i: (i, kv)), kv_spec, kv_spec] + [_ANY] * na,
        out_specs=(pl.BlockSpec((tq, 256), lambda kv, i: (i, kv)),
                   pl.BlockSpec((4, tq, 1), lambda kv, i: (kv, i, 0)), *([_ANY] * na)),
        scratch_shapes=_comm_scratch(na) if na else [],
        compiler_params=_cp(("arbitrary", "arbitrary") if na else ("parallel", "parallel")),
    )(q, k4, v4, *xs)
    return res[0], res[1], list(res[2:])


def _attn_dense_bwd(q, k4, v4, lse, do, *, scatter=(), name):
    S = q.shape[0]
    tq = _tile(S, 256)
    na = len(scatter)
    comm_in, comm_out, held = _scatter_io(scatter)
    n_in = len(comm_in)

    def body(q_ref, k_ref, v_ref, lse_ref, do_ref, *rest):
        dq_ref, dk_ref, dv_ref = rest[n_in:n_in + 3]
        if na:
            s_refs, r_refs, sems = rest[:na], rest[n_in + 3:n_in + 3 + na], rest[n_in + 3 + na:]
            start, finish = _scatter_plan([(s_refs[a], r_refs[a], scatter[a][2]) for a in range(na)], *sems)
            pl.when((pl.program_id(0) == 0) & (pl.program_id(1) == 0))(start)

        @pl.when(pl.program_id(1) == 0)
        def _():
            dk_ref[...] = jnp.zeros_like(dk_ref)
            dv_ref[...] = jnp.zeros_like(dv_ref)

        lane = lax.broadcasted_iota(jnp.int32, (tq, LANES), 1)
        for pr in range(2):
            qp = q_ref[:, pr * LANES:(pr + 1) * LANES]
            dop = do_ref[:, pr * LANES:(pr + 1) * LANES].astype(BF16)
            dq = None
            for half in range(2):
                mine = (lane < HEAD_DIM) if half == 0 else (lane >= HEAD_DIM)
                s = _dot(qp, k_ref[half], _NT)
                p = jnp.exp(s - lse_ref[pr * 2 + half])
                dp = _dot(dop, v_ref[half], _NT)
                delta = jnp.sum(p * dp, axis=-1, keepdims=True)
                ds = (p * (dp - delta)).astype(BF16)
                pb = p.astype(BF16)
                d = _dot(ds, k_ref[half])
                dq = d if dq is None else dq + d
                dk_ref[half] += _dot(ds, jnp.where(mine, qp, jnp.zeros_like(qp)), _TN)
                dv_ref[half] += _dot(pb, jnp.where(mine, dop, jnp.zeros_like(dop)), _TN)
            dq_ref[:, pr * LANES:(pr + 1) * LANES] = dq
        if na:
            pl.when((pl.program_id(0) == 1) & (pl.program_id(1) == S // tq - 1))(finish)

    kv_spec = pl.BlockSpec((2, S, LANES), lambda kv, i: (kv, 0, 0))
    q_spec = pl.BlockSpec((tq, 256), lambda kv, i: (i, kv))
    res = pl.pallas_call(
        body, name=name,
        out_shape=(jax.ShapeDtypeStruct((S, 512), F32), jax.ShapeDtypeStruct((4, S, LANES), F32),
                   jax.ShapeDtypeStruct((4, S, LANES), F32), *comm_out),
        grid=(2, S // tq),
        in_specs=[q_spec, kv_spec, kv_spec, pl.BlockSpec((4, tq, 1), lambda kv, i: (kv, i, 0)), q_spec]
                 + [_ANY] * n_in,
        out_specs=(q_spec, kv_spec, kv_spec, *([_ANY] * na)),
        scratch_shapes=_comm_scratch(na) if na else [],
        input_output_aliases={5 + na + i: 3 + a for i, a in enumerate(held)},
        compiler_params=_cp(("arbitrary", "arbitrary") if na else ("parallel", "arbitrary")),
    )(q, k4, v4, lse, do, *comm_in)
    return res[0], res[1], res[2], list(res[3:])


WIN_Q = 2 * BLOCK
WIN_KEYS = WIN_Q + 2 * WINDOW


def _win_start(n, S):
    return pl.multiple_of(jnp.clip(n * WIN_Q - WINDOW, 0, S - WIN_KEYS), BLOCK)


def _win_valid(n, start):
    qpos = n * WIN_Q + lax.broadcasted_iota(jnp.int32, (WIN_Q, WIN_KEYS), 0)
    kpos = start + lax.broadcasted_iota(jnp.int32, (WIN_Q, WIN_KEYS), 1)
    return jnp.abs(qpos - kpos) <= WINDOW


def _attn_win_fwd(q, k4, v4, sink, *, name):
    S = q.shape[0]
    assert S >= WIN_KEYS

    def body(sink_ref, q_ref, k_ref, v_ref, o_ref, lse_ref):
        n = pl.program_id(0)
        start = _win_start(n, S)
        valid = _win_valid(n, start)
        for kv in range(2):
            for pr in range(2):
                cols = slice((kv * 2 + pr) * LANES, (kv * 2 + pr + 1) * LANES)
                qp = q_ref[:, cols]
                acc = None
                for half in range(2):
                    h = kv * 4 + pr * 2 + half
                    kk = k_ref[kv * 2 + half, pl.ds(start, WIN_KEYS), :]
                    vv = v_ref[kv * 2 + half, pl.ds(start, WIN_KEYS), :]
                    s = jnp.where(valid, _dot(qp, kk, _NT), NEG_BIG)
                    snk = sink_ref[h]
                    m = jnp.maximum(jnp.max(s, axis=-1, keepdims=True), snk)
                    e = jnp.exp(s - m)
                    l = jnp.sum(e, axis=-1, keepdims=True) + jnp.exp(snk - m)
                    pv = _dot(e.astype(BF16), vv) * (1.0 / l)
                    acc = pv if acc is None else acc + pv
                    lse_ref[h] = m + jnp.log(l)
                o_ref[:, cols] = acc.astype(BF16)

    kv_spec = pl.BlockSpec((4, S, LANES), lambda n: (0, 0, 0))
    return pl.pallas_call(
        body, name=name,
        out_shape=(jax.ShapeDtypeStruct((S, 512), BF16), jax.ShapeDtypeStruct((8, S, 1), F32)),
        grid=(S // WIN_Q,),
        in_specs=[pl.BlockSpec(memory_space=pltpu.SMEM), pl.BlockSpec((WIN_Q, 512), lambda n: (n, 0)),
                  kv_spec, kv_spec],
        out_specs=(pl.BlockSpec((WIN_Q, 512), lambda n: (n, 0)), pl.BlockSpec((8, WIN_Q, 1), lambda n: (0, n, 0))),
        compiler_params=_cp(("parallel",)),
    )(sink, q, k4, v4)


def _attn_win_bwd(q, k4, v4, sink, lse, do, *, name):
    S = q.shape[0]

    def body(sink_ref, q_ref, k_ref, v_ref, lse_ref, do_ref, dq_ref, dk_ref, dv_ref, dsink_ref):
        n = pl.program_id(0)

        @pl.when(n == 0)
        def _():
            dk_ref[...] = jnp.zeros_like(dk_ref)
            dv_ref[...] = jnp.zeros_like(dv_ref)
            dsink_ref[...] = jnp.zeros_like(dsink_ref)

        start = _win_start(n, S)
        valid = _win_valid(n, start)
        lane = lax.broadcasted_iota(jnp.int32, (WIN_Q, LANES), 1)
        for kv in range(2):
            for pr in range(2):
                cols = slice((kv * 2 + pr) * LANES, (kv * 2 + pr + 1) * LANES)
                qp = q_ref[:, cols]
                dop = do_ref[:, cols].astype(BF16)
                dq = None
                for half in range(2):
                    h = kv * 4 + pr * 2 + half
                    slot = kv * 2 + half
                    mine = (lane < HEAD_DIM) if half == 0 else (lane >= HEAD_DIM)
                    win = pl.ds(start, WIN_KEYS)
                    kk = k_ref[slot, win, :]
                    vv = v_ref[slot, win, :]
                    lse_h = lse_ref[h]
                    s = jnp.where(valid, _dot(qp, kk, _NT), NEG_BIG)
                    p = jnp.exp(s - lse_h)
                    dp = _dot(dop, vv, _NT)
                    delta = jnp.sum(p * dp, axis=-1, keepdims=True)
                    ds = (p * (dp - delta)).astype(BF16)
                    pb = p.astype(BF16)
                    d = _dot(ds, kk)
                    dq = d if dq is None else dq + d
                    dk_ref[slot, win, :] += _dot(ds, jnp.where(mine, qp, jnp.zeros_like(qp)), _TN)
                    dv_ref[slot, win, :] += _dot(pb, jnp.where(mine, dop, jnp.zeros_like(dop)), _TN)
                    p_sink = jnp.exp(sink_ref[h] - lse_h)
                    dsink_ref[h:h + 1, :] += jnp.broadcast_to(-jnp.sum(p_sink * delta, axis=0, keepdims=True),
                                                              (1, LANES))
                dq_ref[:, cols] = dq

    kv_spec = pl.BlockSpec((4, S, LANES), lambda n: (0, 0, 0))
    q_spec = pl.BlockSpec((WIN_Q, 512), lambda n: (n, 0))
    return pl.pallas_call(
        body, name=name,
        out_shape=(jax.ShapeDtypeStruct((S, 512), F32), jax.ShapeDtypeStruct((4, S, LANES), F32),
                   jax.ShapeDtypeStruct((4, S, LANES), F32), jax.ShapeDtypeStruct((8, LANES), F32)),
        grid=(S // WIN_Q,),
        in_specs=[pl.BlockSpec(memory_space=pltpu.SMEM), q_spec, kv_spec, kv_spec,
                  pl.BlockSpec((8, WIN_Q, 1), lambda n: (0, n, 0)), q_spec],
        out_specs=(q_spec, kv_spec, kv_spec, pl.BlockSpec((8, LANES), lambda n: (0, 0))),
        compiler_params=_cp(("arbitrary",)),
    )(sink, q, k4, v4, lse, do)


def _c_ln(v, g, b):
    mu = jnp.mean(v, axis=-1, keepdims=True)
    vc = v - mu
    r = lax.rsqrt(jnp.mean(vc * vc, axis=-1, keepdims=True) + LN_EPS)
    vh = vc * r
    return vh, r, vh * g + b


def _gmlp_fwd(proj, ws, bs3, lg, lb, *, name):
    S = proj.shape[0]

    def body(u_ref, v_ref, ws_ref, bs_ref, lg_ref, lb_ref, o_ref):
        u = _gelu(u_ref[...])
        _, _, vn = _c_ln(_gelu(v_ref[...]), lg_ref[...], lb_ref[...])
        vn = vn.astype(BF16)
        for gi in range(C_GROUPS):
            cols = slice(gi * LANES, (gi + 1) * LANES)
            mixed = _dot(ws_ref[gi], vn[:, cols]) + bs_ref[gi]
            o_ref[:, cols] = (u[:, cols] * mixed).astype(BF16)

    full2 = lambda n: (0, 0)
    full3 = lambda n: (0, 0, 0)
    return pl.pallas_call(
        body, name=name,
        out_shape=jax.ShapeDtypeStruct((S, C_WIDTH), BF16),
        grid=(S // CHUNK,),
        in_specs=[pl.BlockSpec((CHUNK, C_WIDTH), lambda n: (n, COL_C // C_WIDTH)),
                  pl.BlockSpec((CHUNK, C_WIDTH), lambda n: (n, COL_C // C_WIDTH + 1)),
                  pl.BlockSpec((C_GROUPS, CHUNK, CHUNK), full3), pl.BlockSpec((C_GROUPS, CHUNK, 1), full3),
                  pl.BlockSpec((1, C_WIDTH), full2), pl.BlockSpec((1, C_WIDTH), full2)],
        out_specs=pl.BlockSpec((CHUNK, C_WIDTH), lambda n: (n, 0)),
        compiler_params=_cp(("parallel",)),
    )(proj, proj, ws, bs3, lg, lb)


def _gmlp_bwd(proj, dout, ws, bs3, lg, lb, *, name):
    S = proj.shape[0]

    def body(u_ref, v_ref, d_ref, ws_ref, bs_ref, lg_ref, lb_ref, dz_ref, dws_ref, dbs_ref, dlg_ref, dlb_ref):
        @pl.when(pl.program_id(0) == 0)
        def _():
            dws_ref[...] = jnp.zeros_like(dws_ref)
            dbs_ref[...] = jnp.zeros_like(dbs_ref)
            dlg_ref[...] = jnp.zeros_like(dlg_ref)
            dlb_ref[...] = jnp.zeros_like(dlb_ref)

        u_pre, v_pre, d = u_ref[...], v_ref[...], d_ref[...]
        u, u_grad = _gelu_and_grad(u_pre)
        v, v_grad = _gelu_and_grad(v_pre)
        vh, r, vn = _c_ln(v, lg_ref[...], lb_ref[...])
        vnb = vn.astype(BF16)
        du_parts, dvn_parts = [], []
        for gi in range(C_GROUPS):
            cols = slice(gi * LANES, (gi + 1) * LANES)
            mixed = _dot(ws_ref[gi], vnb[:, cols]) + bs_ref[gi]
            du_parts.append(d[:, cols] * mixed)
            dm = d[:, cols] * u[:, cols]
            dbs_ref[gi] += jnp.sum(dm, axis=-1, keepdims=True)
            dmb = dm.astype(BF16)
            dws_ref[gi] += _dot(dmb, vnb[:, cols], _NT)
            dvn_parts.append(_dot(ws_ref[gi], dmb, _TN))
        du = jnp.concatenate(du_parts, axis=-1)
        dvn = jnp.concatenate(dvn_parts, axis=-1)
        dlg_ref[...] += jnp.sum(dvn * vh, axis=0, keepdims=True)
        dlb_ref[...] += jnp.sum(dvn, axis=0, keepdims=True)
        dvh = dvn * lg_ref[...]
        m1 = jnp.mean(dvh, axis=-1, keepdims=True)
        m2 = jnp.mean(dvh * vh, axis=-1, keepdims=True)
        dv = r * (dvh - m1 - vh * m2)
        dz_ref[:, :C_WIDTH] = (du * u_grad).astype(BF16)
        dz_ref[:, C_WIDTH:] = (dv * v_grad).astype(BF16)

    full2 = lambda n: (0, 0)
    full3 = lambda n: (0, 0, 0)
    return pl.pallas_call(
        body, name=name,
        out_shape=(jax.ShapeDtypeStruct((S, 2 * C_WIDTH), BF16), jax.ShapeDtypeStruct((C_GROUPS, CHUNK, CHUNK), F32),
                   jax.ShapeDtypeStruct((C_GROUPS, CHUNK, 1), F32), jax.ShapeDtypeStruct((1, C_WIDTH), F32),
                   jax.ShapeDtypeStruct((1, C_WIDTH), F32)),
        grid=(S // CHUNK,),
        in_specs=[pl.BlockSpec((CHUNK, C_WIDTH), lambda n: (n, COL_C // C_WIDTH)),
                  pl.BlockSpec((CHUNK, C_WIDTH), lambda n: (n, COL_C // C_WIDTH + 1)),
                  pl.BlockSpec((CHUNK, C_WIDTH), lambda n: (n, 0)),
                  pl.BlockSpec((C_GROUPS, CHUNK, CHUNK), full3), pl.BlockSpec((C_GROUPS, CHUNK, 1), full3),
                  pl.BlockSpec((1, C_WIDTH), full2), pl.BlockSpec((1, C_WIDTH), full2)],
        out_specs=(pl.BlockSpec((CHUNK, 2 * C_WIDTH), lambda n: (n, 0)), pl.BlockSpec((C_GROUPS, CHUNK, CHUNK), full3),
                   pl.BlockSpec((C_GROUPS, CHUNK, 1), full3), pl.BlockSpec((1, C_WIDTH), full2),
                   pl.BlockSpec((1, C_WIDTH), full2)),
        compiler_params=_cp(("arbitrary",)),
    )(proj, proj, dout, ws, bs3, lg, lb)


GATE_BLK = 512


def _gate_specs(tm, D):
    nh = D // GATE_BLK
    first = COL_GATE // GATE_BLK
    return [pl.BlockSpec((tm, GATE_BLK), functools.partial(lambda i, c: (i, c), c=first + b))
            for b in range(N_BRANCH * nh)]


def _merge_fwd(oa, ob, oc, wb, proj, bg, *, name):
    S = oa.shape[0]
    D = wb.shape[2]
    assert D % GATE_BLK == 0
    nh = D // GATE_BLK
    tm = _tile(S, 256)

    def body(oa_ref, ob_ref, oc_ref, wb_ref, *rest):
        gate_refs, bg_ref, o_ref = rest[:N_BRANCH * nh], rest[N_BRANCH * nh], rest[N_BRANCH * nh + 1]
        brs = (oa_ref[...], ob_ref[...], oc_ref[...])
        for j in range(nh):
            cols = slice(j * GATE_BLK, (j + 1) * GATE_BLK)
            acc = None
            for n in range(N_BRANCH):
                b = n * nh + j
                t = _dot(brs[n], wb_ref[n, :, cols])
                g = _sigmoid(gate_refs[b][...] + bg_ref[:, b * GATE_BLK:(b + 1) * GATE_BLK])
                acc = t * g if acc is None else acc + t * g
            o_ref[:, cols] = acc.astype(BF16)

    row = lambda i: (i, 0)
    br = pl.BlockSpec((tm, BRANCH_WIDTH), row)
    return pl.pallas_call(
        body, name=name,
        out_shape=jax.ShapeDtypeStruct((S, D), BF16),
        grid=(S // tm,),
        in_specs=[br, br, br, pl.BlockSpec((N_BRANCH, BRANCH_WIDTH, D), lambda i: (0, 0, 0))]
                 + _gate_specs(tm, D) + [pl.BlockSpec((1, N_BRANCH * D), lambda i: (0, 0))],
        out_specs=pl.BlockSpec((tm, D), row),
        compiler_params=_cp(("parallel",)),
    )(oa, ob, oc, wb, *([proj] * (N_BRANCH * nh)), bg)


def _merge_bwd(oa, ob, oc, wb, proj, bg, dmerged, *, name):
    S = oa.shape[0]
    D = wb.shape[2]
    nh = D // GATE_BLK
    tm = _tile(S, 256)

    def body(oa_ref, ob_ref, oc_ref, wb_ref, *rest):
        gate_refs = rest[:N_BRANCH * nh]
        bg_ref, dm_ref, dgl_ref, dbg_ref = rest[N_BRANCH * nh:N_BRANCH * nh + 4]
        dt_refs = rest[N_BRANCH * nh + 4:N_BRANCH * nh + 4 + N_BRANCH]
        dbr_refs = rest[N_BRANCH * nh + 4 + N_BRANCH:]

        @pl.when(pl.program_id(0) == 0)
        def _():
            dbg_ref[...] = jnp.zeros_like(dbg_ref)

        brs = (oa_ref[...], ob_ref[...], oc_ref[...])
        for n in range(N_BRANCH):
            dbr = None
            for j in range(nh):
                cols = slice(j * GATE_BLK, (j + 1) * GATE_BLK)
                b = n * nh + j
                gcols = slice(b * GATE_BLK, (b + 1) * GATE_BLK)
                w = wb_ref[n, :, cols]
                t = _dot(brs[n], w)
                g = _sigmoid(gate_refs[b][...] + bg_ref[:, gcols])
                dm = dm_ref[:, cols]
                dt = (dm * g).astype(BF16)
                dgl = dm * t * g * (1.0 - g)
                dt_refs[n][:, cols] = dt
                dgl_ref[:, gcols] = dgl.astype(BF16)
                dbg_ref[:, gcols] += jnp.sum(dgl, axis=0, keepdims=True)
                d = _dot(dt, w, _NT)
                dbr = d if dbr is None else dbr + d
            dbr_refs[n][...] = dbr

    row = lambda i: (i, 0)
    br = pl.BlockSpec((tm, BRANCH_WIDTH), row)
    res = pl.pallas_call(
        body, name=name,
        out_shape=(jax.ShapeDtypeStruct((S, N_BRANCH * D), BF16), jax.ShapeDtypeStruct((1, N_BRANCH * D), F32),
                   *([jax.ShapeDtypeStruct((S, D), BF16)] * N_BRANCH),
                   *([jax.ShapeDtypeStruct((S, BRANCH_WIDTH), F32)] * N_BRANCH)),
        grid=(S // tm,),
        in_specs=[br, br, br, pl.BlockSpec((N_BRANCH, BRANCH_WIDTH, D), lambda i: (0, 0, 0))]
                 + _gate_specs(tm, D)
                 + [pl.BlockSpec((1, N_BRANCH * D), lambda i: (0, 0)), pl.BlockSpec((tm, D), row)],
        out_specs=(pl.BlockSpec((tm, N_BRANCH * D), row), pl.BlockSpec((1, N_BRANCH * D), lambda i: (0, 0)),
                   *([pl.BlockSpec((tm, D), row)] * N_BRANCH), *([br] * N_BRANCH)),
        compiler_params=_cp(("arbitrary",)),
    )(oa, ob, oc, wb, *([proj] * (N_BRANCH * nh)), bg, dmerged)
    return res[0], res[1], list(res[2:2 + N_BRANCH]), list(res[2 + N_BRANCH:])


X_SCALE = 1.0 / math.sqrt(X_HEAD_DIM)
X_W = X_HEADS * X_HEAD_DIM


def _xattn_fwd(q, kv, *, name):
    S = q.shape[0]
    M = kv.shape[0]
    tq = _tile(S, 512)

    def body(q_ref, kv_ref, o_ref, lse_ref):
        for h in range(X_HEADS):
            cols = slice(h * LANES, (h + 1) * LANES)
            s = _dot(q_ref[:, cols], kv_ref[:, cols], _NT) * X_SCALE
            m = jnp.max(s, axis=-1, keepdims=True)
            e = jnp.exp(s - m)
            l = jnp.sum(e, axis=-1, keepdims=True)
            p = (e * (1.0 / l)).astype(BF16)
            o_ref[:, cols] = _dot(p, kv_ref[:, X_W + h * LANES:X_W + (h + 1) * LANES]).astype(BF16)
            lse_ref[h] = m + jnp.log(l)

    return pl.pallas_call(
        body, name=name,
        out_shape=(jax.ShapeDtypeStruct((S, X_W), BF16), jax.ShapeDtypeStruct((X_HEADS, S, 1), F32)),
        grid=(S // tq,),
        in_specs=[pl.BlockSpec((tq, X_W), lambda i: (i, 0)), pl.BlockSpec((M, 2 * X_W), lambda i: (0, 0))],
        out_specs=(pl.BlockSpec((tq, X_W), lambda i: (i, 0)), pl.BlockSpec((X_HEADS, tq, 1), lambda i: (0, i, 0))),
        compiler_params=_cp(("parallel",)),
    )(q, kv)


def _xattn_bwd(q, kv, lse, do, *, name):
    S = q.shape[0]
    M = kv.shape[0]
    tq = _tile(S, 512)

    def body(q_ref, kv_ref, lse_ref, do_ref, dq_ref, dkv_ref):
        @pl.when(pl.program_id(0) == 0)
        def _():
            dkv_ref[...] = jnp.zeros_like(dkv_ref)

        for h in range(X_HEADS):
            cols = slice(h * LANES, (h + 1) * LANES)
            vcols = slice(X_W + h * LANES, X_W + (h + 1) * LANES)
            qh, kh, vh = q_ref[:, cols], kv_ref[:, cols], kv_ref[:, vcols]
            doh = do_ref[:, cols].astype(BF16)
            p = jnp.exp(_dot(qh, kh, _NT) * X_SCALE - lse_ref[h])
            dp = _dot(doh, vh, _NT)
            delta = jnp.sum(p * dp, axis=-1, keepdims=True)
            ds = (p * (dp - delta) * X_SCALE).astype(BF16)
            dq_ref[:, cols] = _dot(ds, kh).astype(BF16)
            dkv_ref[:, cols] += _dot(ds, qh, _TN)
            dkv_ref[:, vcols] += _dot(p.astype(BF16), doh, _TN)

    q_spec = pl.BlockSpec((tq, X_W), lambda i: (i, 0))
    return pl.pallas_call(
        body, name=name,
        out_shape=(jax.ShapeDtypeStruct((S, X_W), BF16), jax.ShapeDtypeStruct((M, 2 * X_W), F32)),
        grid=(S // tq,),
        in_specs=[q_spec, pl.BlockSpec((M, 2 * X_W), lambda i: (0, 0)),
                  pl.BlockSpec((X_HEADS, tq, 1), lambda i: (0, i, 0)), q_spec],
        out_specs=(q_spec, pl.BlockSpec((M, 2 * X_W), lambda i: (0, 0))),
        compiler_params=_cp(("arbitrary",)),
    )(q, kv, lse, do)


def _shift_down(h, row):
    return jnp.where(row == 0, 0.0, pltpu.roll(h, 1, 0))


def _shift_up(h, row, S):
    return jnp.where(row == S - 1, 0.0, pltpu.roll(h, S - 1, 0))


def _conv3(h, ck, cb, row, S):
    return _shift_down(h, row) * ck[0:1] + h * ck[1:2] + _shift_up(h, row, S) * ck[2:3] + cb


def _conv_act_fwd(h, ck, cb, *, name):
    S, F2 = h.shape
    F = F2 // 2
    nt = F // LANES

    def body(ha_ref, hb_ref, cka_ref, ckb_ref, cba_ref, cbb_ref, o_ref):
        row = lax.broadcasted_iota(jnp.int32, (S, LANES), 0)
        a = _conv3(ha_ref[...], cka_ref[...], cba_ref[...], row, S)
        b = _conv3(hb_ref[...], ckb_ref[...], cbb_ref[...], row, S)
        o_ref[...] = (_gelu(a) * b).astype(BF16)

    ca = lambda j: (0, j)
    cbi = lambda j: (0, j + nt)
    return pl.pallas_call(
        body, name=name,
        out_shape=jax.ShapeDtypeStruct((S, F), BF16),
        grid=(nt,),
        in_specs=[pl.BlockSpec((S, LANES), ca), pl.BlockSpec((S, LANES), cbi), pl.BlockSpec((3, LANES), ca),
                  pl.BlockSpec((3, LANES), cbi), pl.BlockSpec((1, LANES), ca), pl.BlockSpec((1, LANES), cbi)],
        out_specs=pl.BlockSpec((S, LANES), ca),
        compiler_params=_cp(("parallel",)),
    )(h, h, ck, ck, cb, cb)


def _conv_act_bwd(h, ck, cb, dact, *, name):
    S, F2 = h.shape
    F = F2 // 2
    nt = F // LANES

    def body(ha_ref, hb_ref, cka_ref, ckb_ref, cba_ref, cbb_ref, d_ref,
             dha_ref, dhb_ref, dcka_ref, dckb_ref, dcba_ref, dcbb_ref):
        row = lax.broadcasted_iota(jnp.int32, (S, LANES), 0)
        ha, hb = ha_ref[...], hb_ref[...]
        cka, ckb = cka_ref[...], ckb_ref[...]
        a = _conv3(ha, cka, cba_ref[...], row, S)
        b = _conv3(hb, ckb, cbb_ref[...], row, S)
        d = d_ref[...]
        ga, ga_grad = _gelu_and_grad(a)
        da = d * b * ga_grad
        db = d * ga
        for dd, hh, ck_, dh_ref, dck_ref, dcb_ref in ((da, ha, cka, dha_ref, dcka_ref, dcba_ref),
                                                      (db, hb, ckb, dhb_ref, dckb_ref, dcbb_ref)):
            dcb_ref[...] = jnp.sum(dd, axis=0, keepdims=True)
            dck_ref[0:1, :] = jnp.sum(dd * _shift_down(hh, row), axis=0, keepdims=True)
            dck_ref[1:2, :] = jnp.sum(dd * hh, axis=0, keepdims=True)
            dck_ref[2:3, :] = jnp.sum(dd * _shift_up(hh, row, S), axis=0, keepdims=True)
            dh = _shift_up(dd, row, S) * ck_[0:1] + dd * ck_[1:2] + _shift_down(dd, row) * ck_[2:3]
            dh_ref[...] = dh.astype(BF16)

    ca = lambda j: (0, j)
    cbi = lambda j: (0, j + nt)
    col = pl.BlockSpec((S, LANES), ca)
    return pl.pallas_call(
        body, name=name,
        out_shape=(jax.ShapeDtypeStruct((S, F), BF16), jax.ShapeDtypeStruct((S, F), BF16),
                   jax.ShapeDtypeStruct((3, F), F32), jax.ShapeDtypeStruct((3, F), F32),
                   jax.ShapeDtypeStruct((1, F), F32), jax.ShapeDtypeStruct((1, F), F32)),
        grid=(nt,),
        in_specs=[col, pl.BlockSpec((S, LANES), cbi), pl.BlockSpec((3, LANES), ca), pl.BlockSpec((3, LANES), cbi),
                  pl.BlockSpec((1, LANES), ca), pl.BlockSpec((1, LANES), cbi), col],
        out_specs=(col, col, pl.BlockSpec((3, LANES), ca), pl.BlockSpec((3, LANES), ca),
                   pl.BlockSpec((1, LANES), ca), pl.BlockSpec((1, LANES), ca)),
        compiler_params=_cp(("parallel",)),
    )(h, h, ck, ck, cb, cb, dact)


def _layer_fwd(x, xb, memb, w, shards, tabs, seg, l):
    n = lambda s: f"L{l}_{s}"
    w = dict(w)
    qg2 = jnp.tile(w["b_q_gain"], 2)[None, :]
    kg2 = jnp.tile(w["b_k_gain"], 2)[None, :]
    proj = _mm(xb, w["w_in"], tb=True, name=n("proj"))
    aq, ak4, av4, bq, bk4, bv4 = _prep(proj, tabs, qg2, kg2, seg, name=n("prep"))
    oa, lse_a = _attn_win_fwd(aq, ak4, av4, w["a_sink"], name=n("attn_win"))
    gather = [(shards[k], l) for k in GATHERED_LATE] + ([(shards["w_in"], l + 1)] if l + 1 < DEPTH else [])
    ob, lse_b, gathered = _attn_dense_fwd(bq, bk4, bv4, gather=gather, name=n("attn_dense"))
    for k, g in zip(GATHERED_LATE, gathered):
        w[k] = _unshard(g, GATHER_AXIS[k])
    w_in_next = gathered[len(GATHERED_LATE)] if l + 1 < DEPTH else None
    oc = _gmlp_fwd(proj, w["c_ws"], w["c_bs3"], w["c_ln_g"], w["c_ln_b"], name=n("gmlp"))
    merged = _merge_fwd(oa, ob, oc, w["w_branch"], proj, w["b_gate"], name=n("merge"))
    x1, x1b, xh1, rs1 = _mm_res_ln(merged, w["w_mix_out"], x, w["ln1_g"], w["ln1_b"], name=n("mix_ln1"))
    xq = _mm(x1b, w["x_wq"], out_dtype=BF16, name=n("xq"))
    xkv = _mm(memb, w["x_wkv"], out_dtype=BF16, name=n("xkv"))
    xo, lse_x = _xattn_fwd(xq, xkv, name=n("xattn"))
    x2, x2b, xh2, rs2 = _mm_res_ln(xo, w["x_wo"], x1, w["ln2_g"], w["ln2_b"], name=n("xo_ln2"))
    h = _mm(x2b, w["f_w_up"], tb=True, name=n("ffn_up"))
    act = _conv_act_fwd(h, w["f_conv_k"], w["f_conv_b"], name=n("conv_act"))
    x3, x3b, xh3, rs3 = _mm_res_ln(act, w["f_w_down"], x2, w["ln3_g"], w["ln3_b"], name=n("down_ln3"))
    saved = dict(xb=xb, proj=proj, aq=aq, ak4=ak4, av4=av4, bq=bq, bk4=bk4, bv4=bv4, lse_a=lse_a, lse_b=lse_b,
                 oa=oa, ob=ob, oc=oc, merged=merged, xh1=xh1, rs1=rs1, x1b=x1b, xq=xq, xkv=xkv, xo=xo, lse_x=lse_x,
                 xh2=xh2, rs2=rs2, x2b=x2b, h=h, act=act, xh3=xh3, rs3=rs3, qg2=qg2, kg2=kg2)
    return x3, x3b, saved, w, w_in_next


def _layer_bwd(top, memb, w, sv, tabs, seg, l, ln_below, dw_in_above, recv):
    n = lambda s: f"L{l}_{s}"
    g = {}
    big = {}
    recv = dict(recv)

    def dw(key, a, b, tag):
        big[key] = _mm(a, b, ta=True, out_dtype=BF16, name=n(tag))

    def dw_t(key, segments, x, tag):
        buf = jnp.zeros((sum(s.shape[1] for s in segments), x.shape[1]), BF16)
        row = 0
        for i, s in enumerate(segments):
            buf = _mm(s, x, ta=True, into=(buf, row), name=n(f"{tag}{i}"))
            row += s.shape[1]
        big[key] = buf

    dz3, dz3b, g["ln3_g"], g["ln3_b"] = top
    dw("f_w_down", sv["act"], dz3b, "dw_down")
    dact = _mm(dz3b, w["f_w_down"], tb=True, name=n("dact"))
    dha, dhb, dcka, dckb, dcba, dcbb = _conv_act_bwd(sv["h"], w["f_conv_k"], w["f_conv_b"], dact, name=n("conv_act_bwd"))
    g["f_conv_k"] = jnp.concatenate([dcka, dckb], axis=1)
    g["f_conv_b"] = jnp.concatenate([dcba, dcbb], axis=1)[0]
    dw_t("f_w_up", [dha, dhb], sv["x2b"], "dw_up")
    dz2, dz2b, g["ln2_g"], g["ln2_b"] = _mm([dha, dhb], w["f_w_up"], res=dz3, res_scale=ALPHA,
                                            ln_bwd=(sv["xh2"], sv["rs2"], w["ln2_g"]), name=n("dx2_ln2"))
    dw("x_wo", sv["xo"], dz2b, "dw_xo")
    dxo = _mm(dz2b, w["x_wo"], tb=True, out_dtype=BF16, name=n("dxo"))
    dxq, dxkv = _xattn_bwd(sv["xq"], sv["xkv"], sv["lse_x"], dxo, name=n("xattn_bwd"))
    dw("x_wq", sv["x1b"], dxq, "dw_xq")
    dw("x_wkv", memb, dxkv, "dw_xkv")
    dz1, dz1b, g["ln1_g"], g["ln1_b"] = _mm(dxq, w["x_wq"], tb=True, res=dz2, res_scale=ALPHA,
                                            ln_bwd=(sv["xh1"], sv["rs1"], w["ln1_g"]), name=n("dx1_ln1"))
    dw("w_mix_out", sv["merged"], dz1b, "dw_mix")
    dmerged = _mm(dz1b, w["w_mix_out"], tb=True, name=n("dmerged"))
    dgl, dbg, dt, dbr = _merge_bwd(sv["oa"], sv["ob"], sv["oc"], w["w_branch"], sv["proj"], w["b_gate"], dmerged,
                                   name=n("merge_bwd"))
    g["b_gate"] = dbg[0]
    for i, k in enumerate(("oa", "ob", "oc")):
        dw(f"w_branch{i}", sv[k], dt[i], f"dw_branch{i}")
    big["w_branch"] = jnp.stack([big.pop(f"w_branch{i}") for i in range(N_BRANCH)])
    dqa, dka, dva, dsink = _attn_win_bwd(sv["aq"], sv["ak4"], sv["av4"], w["a_sink"], sv["lse_a"], dbr[0],
                                         name=n("attn_win_bwd"))
    g["a_sink"] = dsink[:, 0]
    sent = [k for k in BIG if k != "w_in"]
    scatter = [(_reshard(big[k], BIG_AXIS[k]), recv[k], l) for k in sent]
    if dw_in_above is not None:
        sent.append("w_in")
        scatter.append((_reshard(dw_in_above, BIG_AXIS["w_in"]), recv["w_in"], l + 1))
    dqb, dkb, dvb, got = _attn_dense_bwd(sv["bq"], sv["bk4"], sv["bv4"], sv["lse_b"], dbr[1], scatter=scatter,
                                         name=n("attn_dense_bwd"))
    recv.update(zip(sent, got))
    dcz, g["c_ws"], dbs3, dlg, dlb = _gmlp_bwd(sv["proj"], dbr[2], w["c_ws"], w["c_bs3"], w["c_ln_g"], w["c_ln_b"],
                                               name=n("gmlp_bwd"))
    g["c_bs"] = dbs3[:, :, 0]
    g["c_ln_g"], g["c_ln_b"] = dlg[0], dlb[0]
    dqkv, dqg, dkg = _unprep(dqa, dka, dva, dqb, dkb, dvb, sv["proj"], tabs, sv["qg2"], sv["kg2"], seg, name=n("unprep"))
    g["b_q_gain"] = dqg[0, :HEAD_DIM] + dqg[0, HEAD_DIM:]
    g["b_k_gain"] = dkg[0, :HEAD_DIM] + dkg[0, HEAD_DIM:]
    dw_t("w_in", [dqkv, dcz, dgl], sv["xb"], "dw_in")
    dx0 = _mm([dqkv, dcz, dgl], w["w_in"], res=dz1, res_scale=ALPHA, ln_bwd=ln_below,
              name=n("dx0" if ln_below is None else "dx0_ln3"))
    for k in ("ln1_g", "ln1_b", "ln2_g", "ln2_b", "ln3_g", "ln3_b"):
        g[k] = g[k][0]
    return dx0, g, big["w_in"], recv


WEIGHTS = ("w_in", "b_gate", "a_sink", "b_q_gain", "b_k_gain", "c_ln_g", "c_ln_b", "c_ws", "c_bs", "w_branch",
           "w_mix_out", "ln1_g", "ln1_b", "x_wq", "x_wkv", "x_wo", "ln2_g", "ln2_b", "f_w_up", "f_conv_k",
           "f_conv_b", "f_w_down", "ln3_g", "ln3_b")
TRANSPOSED = ("w_in", "f_w_up")
BIG_AXIS = {"w_in": 0, "w_branch": 2, "w_mix_out": 0, "x_wq": 0, "x_wkv": 0, "x_wo": 1, "f_w_up": 0, "f_w_down": 0}
BIG = tuple(BIG_AXIS)
GATHERED = BIG + ("f_conv_k",)
GATHERED_LATE = tuple(k for k in GATHERED if k != "w_in")
GATHER_AXIS = dict(BIG_AXIS, f_conv_k=1)
SMALL = tuple(k for k in WEIGHTS if k not in GATHERED)


def _unshard(g, axis):
    t = jnp.moveaxis(g, 0, axis)
    return t.reshape(t.shape[:axis] + (t.shape[axis] * t.shape[axis + 1],) + t.shape[axis + 2:])


def _reshard(full, axis):
    t = full.reshape(full.shape[:axis] + (N_DEV, full.shape[axis] // N_DEV) + full.shape[axis + 1:])
    return jnp.moveaxis(t, axis, 0)


def _small_weights(small, l):
    w = {k: v[l] for k, v in small.items()}
    for k in ("c_ln_g", "c_ln_b", "ln1_g", "ln1_b", "ln2_g", "ln2_b", "ln3_g", "ln3_b", "b_gate", "f_conv_b"):
        w[k] = w[k][None, :]
    w["c_bs3"] = w["c_bs"][:, :, None]
    w["c_ws"] = w["c_ws"].astype(BF16)
    return w


def _local_step(x, mem, target, small, shards):
    S = x.shape[0]
    tabs = _rope_tables(S)
    seg = _seg_matrix()
    memb = mem.astype(BF16)
    xb = x.astype(BF16)
    saved, weights = [], []
    w_in_g = _gather_call([(shards["w_in"], 0)], name="gather_w_in_L0")[0]
    for l in range(DEPTH):
        w = dict(_small_weights(small, l), w_in=_unshard(w_in_g, GATHER_AXIS["w_in"]))
        x, xb, sv, w, w_in_g = _layer_fwd(x, xb, memb, w, shards, tabs, seg, l)
        saved.append(sv)
        weights.append(w)
    dy, loss = _loss_head(x, target, name="loss_head")
    grads = [None] * DEPTH
    recv = {k: jax.ShapeDtypeStruct((DEPTH, N_DEV) + shards[k].shape[1:], BF16) for k in BIG}
    dw_in = None
    last_ln = lambda l: (saved[l]["xh3"], saved[l]["rs3"], weights[l]["ln3_g"])
    top = _ln_bwd(dy, *last_ln(DEPTH - 1), name="ln_bwd_top")
    for l in reversed(range(DEPTH)):
        top, grads[l], dw_in, recv = _layer_bwd(top, memb, weights[l], saved[l], tabs, seg, l,
                                                last_ln(l - 1) if l > 0 else None, dw_in, recv)
    recv["w_in"] = _scatter_call([(_reshard(dw_in, BIG_AXIS["w_in"]), recv["w_in"], 0)], name="scatter_w_in_L0")[0]
    return loss, top, grads, [recv[k] for k in BIG]


PACK_W = 1024


def _gather_call(gather, *, name):
    na = len(gather)

    def body(*refs):
        start, finish = _gather_plan([(refs[a], gather[a][1], refs[na + a]) for a in range(na)], *refs[2 * na:])
        start()
        finish()

    return list(pl.pallas_call(
        body, name=name,
        out_shape=[_gathered_shape(x) for x, _ in gather],
        in_specs=[_ANY] * na, out_specs=[_ANY] * na,
        scratch_shapes=_comm_scratch(na),
    )(*[x for x, _ in gather]))


def _scatter_io(scatter):
    held = [a for a, (_, r, _) in enumerate(scatter) if not isinstance(r, jax.ShapeDtypeStruct)]
    return ([s for s, _, _ in scatter] + [scatter[a][1] for a in held],
            [jax.ShapeDtypeStruct(r.shape, r.dtype) for _, r, _ in scatter], held)


def _scatter_call(scatter, *, name):
    na = len(scatter)
    operands, out_shape, held = _scatter_io(scatter)
    n_in = len(operands)

    def body(*refs):
        start, finish = _scatter_plan([(refs[a], refs[n_in + a], scatter[a][2]) for a in range(na)],
                                      *refs[n_in + na:])
        start()
        finish()

    return list(pl.pallas_call(
        body, name=name,
        out_shape=out_shape,
        in_specs=[_ANY] * n_in, out_specs=[_ANY] * na,
        scratch_shapes=_comm_scratch(na),
        input_output_aliases={na + i: a for i, a in enumerate(held)},
    )(*operands))


def _sum_parts(parts, *, name):
    P, R, C = parts.shape
    tr = _tile(R, 64, align=8)

    def body(p_ref, o_ref):
        g = p_ref[0].astype(F32)
        for s in range(1, P):
            g = g + p_ref[s].astype(F32)
        o_ref[...] = g

    return pl.pallas_call(
        body, name=name, out_shape=jax.ShapeDtypeStruct((R, C), F32), grid=(R // tr,),
        in_specs=[pl.BlockSpec((P, tr, C), lambda i: (0, i, 0))], out_specs=pl.BlockSpec((tr, C), lambda i: (i, 0)),
        compiler_params=_cp(("parallel",)),
    )(parts)


ADAM_BLOCK_ELEMS = 512 * 1024


def _adamw(parts, w, m, v, *, name):
    L, P, R, C = parts.shape
    assert w.shape == (L, R, C), (parts.shape, w.shape)
    tr = _tile(R, max(16, ADAM_BLOCK_ELEMS // C), align=16)

    def body(p_ref, w_ref, m_ref, v_ref, g_ref, d_ref, nm_ref, nv_ref):
        g = p_ref[0].astype(F32)
        for s in range(1, P):
            g = g + p_ref[s].astype(F32)
        nm = ADAM_B1 * m_ref[...] + (1.0 - ADAM_B1) * g
        nv = ADAM_B2 * v_ref[...] + (1.0 - ADAM_B2) * (g * g)
        m_hat = nm / (1.0 - ADAM_B1 ** ADAM_STEP)
        v_hat = nv / (1.0 - ADAM_B2 ** ADAM_STEP)
        g_ref[...] = g
        d_ref[...] = -ADAM_LR * (m_hat / (jnp.sqrt(v_hat) + ADAM_EPS) + ADAM_WD * w_ref[...])
        nm_ref[...] = nm
        nv_ref[...] = nv

    blk = pl.BlockSpec((None, tr, C), lambda l, i: (l, i, 0))
    shp = jax.ShapeDtypeStruct((L, R, C), F32)
    return pl.pallas_call(
        body, name=name, out_shape=(shp, shp, shp, shp), grid=(L, R // tr),
        in_specs=[pl.BlockSpec((None, P, tr, C), lambda l, i: (l, 0, i, 0)), blk, blk, blk],
        out_specs=(blk, blk, blk, blk),
        compiler_params=_cp(("parallel", "parallel")),
    )(parts, w, m, v)


def _pad_rows(vec, width, row_align):
    n = vec.shape[0]
    rows = -(-n // width)
    rows = -(-rows // row_align) * row_align
    return jnp.pad(vec, (0, rows * width - n)).reshape(rows, width)


def kernel(x, mem, w_in, b_gate, a_sink, b_q_gain, b_k_gain, c_ln_g, c_ln_b, c_ws, c_bs, w_branch, w_mix_out, ln1_g, ln1_b, x_wq, x_wkv, x_wo, ln2_g, ln2_b, f_w_up, f_conv_k, f_conv_b, f_w_down, ln3_g, ln3_b, loss_target, m_w_in, m_b_gate, m_a_sink, m_b_q_gain, m_b_k_gain, m_c_ln_g, m_c_ln_b, m_c_ws, m_c_bs, m_w_branch, m_w_mix_out, m_ln1_g, m_ln1_b, m_x_wq, m_x_wkv, m_x_wo, m_ln2_g, m_ln2_b, m_f_w_up, m_f_conv_k, m_f_conv_b, m_f_w_down, m_ln3_g, m_ln3_b, v_w_in, v_b_gate, v_a_sink, v_b_q_gain, v_b_k_gain, v_c_ln_g, v_c_ln_b, v_c_ws, v_c_bs, v_w_branch, v_w_mix_out, v_ln1_g, v_ln1_b, v_x_wq, v_x_wkv, v_x_wo, v_ln2_g, v_ln2_b, v_f_w_up, v_f_conv_k, v_f_conv_b, v_f_w_down, v_ln3_g, v_ln3_b):
    w = dict(w_in=w_in, b_gate=b_gate, a_sink=a_sink, b_q_gain=b_q_gain, b_k_gain=b_k_gain, c_ln_g=c_ln_g,
             c_ln_b=c_ln_b, c_ws=c_ws, c_bs=c_bs, w_branch=w_branch, w_mix_out=w_mix_out, ln1_g=ln1_g, ln1_b=ln1_b,
             x_wq=x_wq, x_wkv=x_wkv, x_wo=x_wo, ln2_g=ln2_g, ln2_b=ln2_b, f_w_up=f_w_up, f_conv_k=f_conv_k,
             f_conv_b=f_conv_b, f_w_down=f_w_down, ln3_g=ln3_g, ln3_b=ln3_b)
    m = dict(w_in=m_w_in, b_gate=m_b_gate, a_sink=m_a_sink, b_q_gain=m_b_q_gain, b_k_gain=m_b_k_gain,
             c_ln_g=m_c_ln_g, c_ln_b=m_c_ln_b, c_ws=m_c_ws, c_bs=m_c_bs, w_branch=m_w_branch, w_mix_out=m_w_mix_out,
             ln1_g=m_ln1_g, ln1_b=m_ln1_b, x_wq=m_x_wq, x_wkv=m_x_wkv, x_wo=m_x_wo, ln2_g=m_ln2_g, ln2_b=m_ln2_b,
             f_w_up=m_f_w_up, f_conv_k=m_f_conv_k, f_conv_b=m_f_conv_b, f_w_down=m_f_w_down, ln3_g=m_ln3_g,
             ln3_b=m_ln3_b)
    v = dict(w_in=v_w_in, b_gate=v_b_gate, a_sink=v_a_sink, b_q_gain=v_b_q_gain, b_k_gain=v_b_k_gain,
             c_ln_g=v_c_ln_g, c_ln_b=v_c_ln_b, c_ws=v_c_ws, c_bs=v_c_bs, w_branch=v_w_branch, w_mix_out=v_w_mix_out,
             ln1_g=v_ln1_g, ln1_b=v_ln1_b, x_wq=v_x_wq, x_wkv=v_x_wkv, x_wo=v_x_wo, ln2_g=v_ln2_g, ln2_b=v_ln2_b,
             f_w_up=v_f_w_up, f_conv_k=v_f_conv_k, f_conv_b=v_f_conv_b, f_w_down=v_f_w_down, ln3_g=v_ln3_g,
             ln3_b=v_ln3_b)
    me = 4 * lax.axis_index("x") + 2 * lax.axis_index("y") + lax.axis_index("c")

    def held(k, t):
        return jnp.swapaxes(t, 1, 2) if k in TRANSPOSED else t

    shards = dict({k: held(k, w[k]).astype(BF16) for k in BIG}, f_conv_k=w["f_conv_k"])
    loss, grad_x, grads, recvs = _local_step(x[0], mem[0], loss_target[0], {k: w[k] for k in SMALL}, shards)
    loss = lax.psum(loss[0, 0], ("x", "y", "c"))

    out_g, out_d, out_m, out_v = {}, {}, {}, {}
    for k, recv in zip(BIG, recvs):
        shp = held(k, w[k]).shape
        rc = (DEPTH, math.prod(shp[1:-1]), shp[-1])
        parts = recv.reshape((DEPTH, N_DEV) + rc[1:])
        g_, d_, m_, v_ = _adamw(parts, held(k, w[k]).reshape(rc), held(k, m[k]).reshape(rc),
                                held(k, v[k]).reshape(rc), name=f"adamw_{k}")
        out_g[k], out_d[k], out_m[k], out_v[k] = (held(k, t.reshape(shp)) for t in (g_, d_, m_, v_))

    small_all = SMALL + ("f_conv_k",)
    gfull = {k: jnp.stack([grads[l][k] for l in range(DEPTH)]) for k in small_all}

    def pack(d):
        rows = jnp.concatenate([_pad_rows(d[k].reshape(-1), PACK_W, 1) for k in small_all])
        return jnp.pad(rows, ((0, -rows.shape[0] % 8), (0, 0)))

    def unpack(rows, like):
        out, r = {}, 0
        for k in small_all:
            nr = -(-like[k].size // PACK_W)
            out[k] = rows[r:r + nr].reshape(-1)[:like[k].size].reshape(like[k].shape)
            r += nr
        return out

    gathered = _gather_call([(pack(gfull)[None], 0)], name="gather_small_grads")[0]
    sg = unpack(_sum_parts(gathered, name="sum_small_grads"), gfull)
    width = w["f_conv_k"].shape[2]
    sg["f_conv_k"] = lax.dynamic_slice_in_dim(sg["f_conv_k"], me * width, width, axis=2)
    g_, d_, m_, v_ = _adamw(pack(sg)[None, None], pack(w)[None], pack(m)[None], pack(v)[None], name="adamw_small")
    ud, um, uv = (unpack(t[0], w) for t in (d_, m_, v_))
    for k in small_all:
        out_g[k], out_d[k], out_m[k], out_v[k] = sg[k], ud[k], um[k], uv[k]

    return (loss, grad_x[None], *[out_g[k] for k in WEIGHTS], *[out_d[k] for k in WEIGHTS],
            *[out_m[k] for k in WEIGHTS], *[out_v[k] for k in WEIGHTS])
```

```python
import functools
import math

import jax
import jax.numpy as jnp
from jax import lax
from jax.experimental import pallas as pl
from jax.experimental.pallas import tpu as pltpu

F32 = jnp.float32
BF16 = jnp.bfloat16

DEPTH = 4
HEAD_DIM = 64
BLOCK = 128
WINDOW = 128
GRID_W = 64
C_WIDTH = 512
C_GROUPS = 4
CHUNK = 128
N_BRANCH = 3
BRANCH_WIDTH = 512
ROPE_THETA = 10000.0
X_HEADS = 4
X_HEAD_DIM = 128
ALPHA = (2 * DEPTH) ** 0.25
LN_EPS = 1e-5
RMS_EPS = 1e-6
ADAM_LR = 0.001
ADAM_B1 = 0.9
ADAM_B2 = 0.999
ADAM_EPS = 1e-08
ADAM_WD = 0.01
ADAM_STEP = 10
N_DEV = 8

COL_A = 0
COL_B = 768
COL_C = 1536
COL_GATE = 2560
QKV_W = 768

LANES = 128
V7X_VMEM_BYTES = 64 * 1024 * 1024
VMEM_LIMIT = V7X_VMEM_BYTES - 8 * 1024 * 1024
NEG_BIG = -1e30

_NT = (((1,), (1,)), ((), ()))
_TN = (((0,), (0,)), ((), ()))
_NN = (((1,), (0,)), ((), ()))


def _cp(sem=None):
    return pltpu.CompilerParams(dimension_semantics=sem, vmem_limit_bytes=VMEM_LIMIT)


def _tile(n, target, align=LANES):
    if n <= target:
        return n
    best = None
    for t in range(align, target + 1, align):
        if n % t == 0:
            best = t
    assert best is not None, (n, target)
    return best


def _dot(a, b, dims=_NN):
    return lax.dot_general(a, b, dims, preferred_element_type=F32)


def _normal_cdf(x):
    z = jnp.abs(x) * 0.7071067811865476
    t = 1.0 / (1.0 + 0.3275911 * z)
    poly = t * (0.254829592 + t * (-0.284496736 + t * (1.421413741 + t * (-1.453152027 + t * 1.061405429))))
    ex = jnp.exp(-0.5 * x * x)
    half = 0.5 * poly * ex
    return jnp.where(x >= 0, 1.0 - half, half), ex


def _gelu(x):
    return x * _normal_cdf(x)[0]


def _gelu_and_grad(x):
    cdf, ex = _normal_cdf(x)
    return x * cdf, cdf + x * ex * 0.3989422804014327


def _sigmoid(x):
    return 1.0 / (1.0 + jnp.exp(-x))


MESH_ID = pl.DeviceIdType.MESH
_ANY = pl.BlockSpec(memory_space=pl.ANY)
COPIES_PER_ARRAY = N_DEV - 1


def _comm_scratch(n_arrays):
    return [pltpu.SemaphoreType.DMA((COPIES_PER_ARRAY * n_arrays,)),
            pltpu.SemaphoreType.DMA((COPIES_PER_ARRAY * n_arrays,)), pltpu.SemaphoreType.DMA((n_arrays,))]


def _gathered_shape(x):
    return jax.ShapeDtypeStruct((N_DEV,) + x.shape[1:], x.dtype)


def _gather_plan(entries, send_sems, recv_sems, local_sems):
    mx, my, mc = lax.axis_index("x"), lax.axis_index("y"), lax.axis_index("c")
    me, sibling = (mx, my, mc), (mx, my, 1 - mc)
    chips = [(1 - mx, my), (mx, 1 - my), (1 - mx, 1 - my)]

    def copy(a, k, block, to, from_shard=False):
        x_ref, l, out_ref = entries[a]
        dst = out_ref.at[4 * block[0] + 2 * block[1] + block[2]]
        return pltpu.make_async_remote_copy(
            src_ref=x_ref.at[l] if from_shard else dst, dst_ref=dst,
            send_sem=send_sems.at[COPIES_PER_ARRAY * a + k], recv_sem=recv_sems.at[COPIES_PER_ARRAY * a + k],
            device_id=to, device_id_type=MESH_ID)

    def own(a):
        x_ref, l, out_ref = entries[a]
        return pltpu.make_async_copy(x_ref.at[l], out_ref.at[4 * mx + 2 * my + mc], local_sems.at[a])

    def first(a):
        return [copy(a, 0, me, sibling, True)] + [copy(a, 1 + j, me, (*chip, mc), True) for j, chip in enumerate(chips)]

    def passed(a):
        return [copy(a, 4 + j, (*chip, mc), sibling) for j, chip in enumerate(chips)]

    def start():
        for a in range(len(entries)):
            own(a).start()
            for cp in first(a):
                cp.start()

    def finish():
        for a in range(len(entries)):
            fwd = passed(a)
            for j, chip in enumerate(chips):
                copy(a, 1 + j, (*chip, mc), me).wait_recv()
                fwd[j].start()
        for a in range(len(entries)):
            copy(a, 0, sibling, me).wait_recv()
            for j, chip in enumerate(chips):
                copy(a, 4 + j, (*chip, 1 - mc), me).wait_recv()
            for cp in first(a) + passed(a):
                cp.wait_send()
            own(a).wait()

    return start, finish


def _scatter_plan(entries, send_sems, recv_sems, local_sems):
    mx, my, mc = lax.axis_index("x"), lax.axis_index("y"), lax.axis_index("c")
    me = 4 * mx + 2 * my + mc

    def src(a, dev):
        return entries[a][0].at[dev]

    def copies(a):
        _, recv_ref, lr = entries[a]
        out = []
        for k in range(1, N_DEV):
            px = 1 - mx if k & 4 else mx
            py = 1 - my if k & 2 else my
            pc = 1 - mc if k & 1 else mc
            peer = 4 * px + 2 * py + pc
            sems = dict(send_sem=send_sems.at[COPIES_PER_ARRAY * a + k - 1],
                        recv_sem=recv_sems.at[COPIES_PER_ARRAY * a + k - 1],
                        device_id=(px, py, pc), device_id_type=MESH_ID)
            sends = pltpu.make_async_remote_copy(src_ref=src(a, peer), dst_ref=recv_ref.at[lr, me], **sems)
            lands = pltpu.make_async_remote_copy(src_ref=src(a, me), dst_ref=recv_ref.at[lr, peer], **sems)
            out.append((sends, lands))
        return out

    def own(a):
        _, recv_ref, lr = entries[a]
        return pltpu.make_async_copy(src(a, me), recv_ref.at[lr, me], local_sems.at[a])

    def start():
        for a in range(len(entries)):
            own(a).start()
            for sends, _ in copies(a):
                sends.start()

    def finish():
        for a in range(len(entries)):
            for _, lands in copies(a):
                lands.wait_recv()
        for a in range(len(entries)):
            for sends, _ in copies(a):
                sends.wait_send()
            own(a).wait()

    return start, finish


MM_TILE, MM_TK = 1536, 2048
MM_TILE_LN = 512


def _mm(a, b, *, ta=False, tb=False, out_dtype=F32, res=None, res_scale=1.0, into=None, ln_bwd=None, name):
    segs = list(a) if isinstance(a, (list, tuple)) else [a]
    if ta:
        (K, M), seg_k = segs[0].shape, [segs[0].shape[0]]
        assert len(segs) == 1
    else:
        M, seg_k = segs[0].shape[0], [s.shape[1] for s in segs]
        K = sum(seg_k)
    if tb:
        N, Kb = b.shape
    else:
        Kb, N = b.shape
    assert K == Kb, ([s.shape for s in segs], b.shape, ta, tb)
    row_off = into[1] if into is not None else 0
    tm, tn = _tile(math.gcd(M, row_off), MM_TILE if ln_bwd is None else MM_TILE_LN), _tile(N, MM_TILE)
    tk = _tile(K, MM_TK) if len(segs) == 1 else _tile(math.gcd(*seg_k), MM_TILE)
    nk = K // tk
    seg_chunks = [ks // tk for ks in seg_k]
    seg_first = [sum(seg_chunks[:s]) for s in range(len(segs))]
    dims = (((0 if ta else 1,), (1 if tb else 0,)), ((), ()))
    ns = len(segs)
    n_res = ns + 1
    n_ln = n_res + (res is not None)
    into_held = into is not None and not isinstance(into[0], jax.ShapeDtypeStruct)
    n_in = n_ln + (3 if ln_bwd is not None else 0) + into_held
    assert ln_bwd is None or (tn == N and into is None)

    def body(*refs):
        a_refs, b_ref = refs[:ns], refs[ns]
        r_ref = refs[n_res] if res is not None else None
        o_ref = refs[n_in]
        first_row_tile = pl.program_id(0) == 0

        def finish(out):
            if r_ref is not None:
                out = out + res_scale * r_ref[...]
            if ln_bwd is None:
                o_ref[...] = out.astype(o_ref.dtype)
                return
            xh_ref, rs_ref, g_ref = refs[n_ln:n_ln + 3]
            ob_ref, dg_ref, db_ref = refs[n_in + 1:n_in + 4]

            @pl.when(first_row_tile)
            def _():
                dg_ref[...] = jnp.zeros_like(dg_ref)
                db_ref[...] = jnp.zeros_like(db_ref)

            xh = xh_ref[...]
            dxh = out * g_ref[...]
            m1 = jnp.mean(dxh, axis=-1, keepdims=True)
            m2 = jnp.mean(dxh * xh, axis=-1, keepdims=True)
            dz = rs_ref[...] * (dxh - m1 - xh * m2)
            o_ref[...] = dz
            ob_ref[...] = dz.astype(BF16)
            dg_ref[...] += jnp.sum(out * xh, axis=0, keepdims=True)
            db_ref[...] += jnp.sum(out, axis=0, keepdims=True)

        def prod(s):
            return _dot(a_refs[s][...].astype(BF16), b_ref[...].astype(BF16), dims)

        if nk == 1:
            finish(prod(0))
            return
        acc = refs[n_in + (4 if ln_bwd is not None else 1)]
        k = pl.program_id(2)

        @pl.when(k == 0)
        def _():
            acc[...] = jnp.zeros_like(acc)

        for s in range(ns):
            def add(s=s):
                acc[...] += prod(s)
            pl.when((k >= seg_first[s]) & (k < seg_first[s] + seg_chunks[s]))(add)

        @pl.when(k == nk - 1)
        def _():
            finish(acc[...])

    if ta:
        a_specs = [pl.BlockSpec((tk, tm), lambda i, j, k: (k, i))]
    else:
        a_specs = [pl.BlockSpec((tm, tk), functools.partial(
            lambda i, j, k, first, n: (i, jnp.clip(k - first, 0, n - 1)), first=seg_first[s], n=seg_chunks[s]))
            for s in range(ns)]
    b_spec = pl.BlockSpec((tn, tk), lambda i, j, k: (j, k)) if tb else pl.BlockSpec((tk, tn), lambda i, j, k: (k, j))
    in_specs = a_specs + [b_spec]
    args = segs + [b]
    if res is not None:
        in_specs.append(pl.BlockSpec((tm, tn), lambda i, j, k: (i, j)))
        args.append(res)
    out_spec = pl.BlockSpec((tm, tn), lambda i, j, k: (i, j))
    if ln_bwd is not None:
        xh, rs, g = ln_bwd
        in_specs += [out_spec, pl.BlockSpec((tm, 1), lambda i, j, k: (i, 0)), pl.BlockSpec((1, tn), lambda i, j, k: (0, j))]
        args += [xh, rs, g]
        vec = pl.BlockSpec((1, tn), lambda i, j, k: (0, j))
        return pl.pallas_call(
            body, name=name,
            out_shape=(jax.ShapeDtypeStruct((M, N), F32), jax.ShapeDtypeStruct((M, N), BF16),
                       jax.ShapeDtypeStruct((1, N), F32), jax.ShapeDtypeStruct((1, N), F32)),
            grid=(M // tm, N // tn, nk),
            in_specs=in_specs,
            out_specs=(out_spec, out_spec, vec, vec),
            scratch_shapes=[pltpu.VMEM((tm, tn), F32)] if nk > 1 else [],
            compiler_params=_cp(("arbitrary", "arbitrary", "arbitrary")),
        )(*args)
    if into is None:
        out_shape = jax.ShapeDtypeStruct((M, N), out_dtype)
        blk_off, aliases = 0, {}
    else:
        buf = into[0]
        assert buf.shape[1] == N and row_off % tm == 0 and row_off + M <= buf.shape[0], (buf.shape, M, N, row_off)
        out_shape = jax.ShapeDtypeStruct(buf.shape, buf.dtype)
        blk_off, aliases = row_off // tm, {}
        if into_held:
            aliases = {n_in - 1: 0}
            in_specs.append(_ANY)
            args.append(buf)
    return pl.pallas_call(
        body, name=name,
        out_shape=out_shape,
        grid=(M // tm, N // tn, nk),
        in_specs=in_specs,
        out_specs=pl.BlockSpec((tm, tn), lambda i, j, k: (i + blk_off, j)),
        scratch_shapes=[pltpu.VMEM((tm, tn), F32)] if nk > 1 else [],
        input_output_aliases=aliases,
        compiler_params=_cp(("parallel", "parallel", "arbitrary")),
    )(*args)


def _mm_res_ln(a, w, x, g, b, *, name):
    S, K = a.shape
    D = w.shape[1]
    tm = _tile(S, 256)

    def body(a_ref, w_ref, x_ref, g_ref, b_ref, y_ref, yb_ref, xh_ref, rs_ref):
        h = _dot(a_ref[...], w_ref[...])
        z = ALPHA * x_ref[...] + h
        mu = jnp.mean(z, axis=-1, keepdims=True)
        zc = z - mu
        var = jnp.mean(zc * zc, axis=-1, keepdims=True)
        r = lax.rsqrt(var + LN_EPS)
        xh = zc * r
        y = xh * g_ref[...] + b_ref[...]
        y_ref[...] = y
        yb_ref[...] = y.astype(BF16)
        xh_ref[...] = xh
        rs_ref[...] = r

    row = lambda i: (i, 0)
    full = lambda i: (0, 0)
    return pl.pallas_call(
        body, name=name,
        out_shape=(jax.ShapeDtypeStruct((S, D), F32), jax.ShapeDtypeStruct((S, D), BF16),
                   jax.ShapeDtypeStruct((S, D), F32), jax.ShapeDtypeStruct((S, 1), F32)),
        grid=(S // tm,),
        in_specs=[pl.BlockSpec((tm, K), row), pl.BlockSpec((K, D), full), pl.BlockSpec((tm, D), row),
                  pl.BlockSpec((1, D), full), pl.BlockSpec((1, D), full)],
        out_specs=(pl.BlockSpec((tm, D), row), pl.BlockSpec((tm, D), row), pl.BlockSpec((tm, D), row),
                   pl.BlockSpec((tm, 1), row)),
        compiler_params=_cp(("parallel",)),
    )(a, w, x, g, b)


def _ln_bwd(dy, xh, rs, g, *, name):
    S, D = dy.shape
    tm = _tile(S, 256)

    def body(dy_ref, xh_ref, rs_ref, g_ref, dz_ref, dzb_ref, dg_ref, db_ref):
        @pl.when(pl.program_id(0) == 0)
        def _():
            dg_ref[...] = jnp.zeros_like(dg_ref)
            db_ref[...] = jnp.zeros_like(db_ref)

        dy = dy_ref[...]
        xh = xh_ref[...]
        dxh = dy * g_ref[...]
        m1 = jnp.mean(dxh, axis=-1, keepdims=True)
        m2 = jnp.mean(dxh * xh, axis=-1, keepdims=True)
        dz = rs_ref[...] * (dxh - m1 - xh * m2)
        dz_ref[...] = dz
        dzb_ref[...] = dz.astype(BF16)
        dg_ref[...] += jnp.sum(dy * xh, axis=0, keepdims=True)
        db_ref[...] += jnp.sum(dy, axis=0, keepdims=True)

    row = lambda i: (i, 0)
    full = lambda i: (0, 0)
    return pl.pallas_call(
        body, name=name,
        out_shape=(jax.ShapeDtypeStruct((S, D), F32), jax.ShapeDtypeStruct((S, D), BF16),
                   jax.ShapeDtypeStruct((1, D), F32), jax.ShapeDtypeStruct((1, D), F32)),
        grid=(S // tm,),
        in_specs=[pl.BlockSpec((tm, D), row), pl.BlockSpec((tm, D), row), pl.BlockSpec((tm, 1), row),
                  pl.BlockSpec((1, D), full)],
        out_specs=(pl.BlockSpec((tm, D), row), pl.BlockSpec((tm, D), row), pl.BlockSpec((1, D), full),
                   pl.BlockSpec((1, D), full)),
        compiler_params=_cp(("arbitrary",)),
    )(dy, xh, rs, g)


def _loss_head(y, t, *, name):
    S, D = y.shape
    tm = _tile(S, 512)

    def body(y_ref, t_ref, dy_ref, l_ref):
        @pl.when(pl.program_id(0) == 0)
        def _():
            l_ref[...] = jnp.zeros_like(l_ref)

        e = y_ref[...] - t_ref[...]
        dy_ref[...] = e / D
        l_ref[...] += 0.5 * jnp.sum(jnp.mean(e * e, axis=-1, keepdims=True), axis=0, keepdims=True)

    row = lambda i: (i, 0)
    return pl.pallas_call(
        body, name=name,
        out_shape=(jax.ShapeDtypeStruct((S, D), F32), jax.ShapeDtypeStruct((1, 1), F32)),
        grid=(S // tm,),
        in_specs=[pl.BlockSpec((tm, D), row), pl.BlockSpec((tm, D), row)],
        out_specs=(pl.BlockSpec((tm, D), row), pl.BlockSpec((1, 1), lambda i: (0, 0))),
        compiler_params=_cp(("arbitrary",)),
    )(y, t)


def _rope_tables(S):
    pos = jnp.arange(S, dtype=jnp.int32)
    row = pos // GRID_W
    col = pos % GRID_W

    def cs(p, d):
        half = d // 2
        inv = ROPE_THETA ** (-jnp.arange(half, dtype=F32) * (2.0 / d))
        ang = p.astype(F32)[:, None] * inv[None, :]
        c, s = jnp.cos(ang), jnp.sin(ang)
        return jnp.concatenate([c, c], -1), jnp.concatenate([-s, s], -1)

    ca, sa = cs(pos, HEAD_DIM)
    cr, sr = cs(row, HEAD_DIM // 2)
    cc, sc = cs(col, HEAD_DIM // 2)
    cb, sb = jnp.concatenate([cr, cc], -1), jnp.concatenate([sr, sc], -1)
    two = lambda t: jnp.concatenate([t, t], -1)
    return two(ca), two(sa), two(cb), two(sb)


def _partner(x, lane, width):
    h = width // 2
    return jnp.where(lane % width < h, pltpu.roll(x, LANES - h, 1), pltpu.roll(x, h, 1))


def _rope_fwd(x, c, s, lane, width):
    return x * c + _partner(x, lane, width) * s


def _rope_bwd(dy, c, s, lane, width):
    return dy * c + _partner(dy * s, lane, width)


def _head_sum(x, seg):
    return lax.dot_general(x, seg, _NN, precision=lax.Precision.HIGHEST, preferred_element_type=F32)


def _split_heads(x, lane):
    lo = lane < HEAD_DIM
    r = pltpu.roll(x, HEAD_DIM, 1)
    z = jnp.zeros_like(x)
    return jnp.where(lo, x, z), jnp.where(lo, z, r), jnp.where(lo, r, z), jnp.where(lo, z, x)


def _fold_heads(d0, d1, lane):
    t0 = d0 + pltpu.roll(d0, HEAD_DIM, 1)
    t1 = d1 + pltpu.roll(d1, HEAD_DIM, 1)
    return jnp.where(lane < HEAD_DIM, t0, t1)


def _seg_matrix():
    i = jnp.arange(LANES)
    return (i[:, None] // HEAD_DIM == i[None, :] // HEAD_DIM).astype(F32)


def _prep(proj, tabs, qg2, kg2, seg, *, name):
    S = proj.shape[0]
    ts = _tile(S, 256)
    ca, sa, cb, sb = tabs

    def body(pa_ref, pb_ref, ca_ref, sa_ref, cb_ref, sb_ref, qg_ref, kg_ref, seg_ref,
             aq_ref, ak_ref, av_ref, bq_ref, bk_ref, bv_ref):
        lane = lax.broadcasted_iota(jnp.int32, (ts, LANES), 1)
        ca, sa, cb, sb = ca_ref[...], sa_ref[...], cb_ref[...], sb_ref[...]
        seg = seg_ref[...]

        def norm(x, gain):
            r = lax.rsqrt(_head_sum(x * x, seg) * (1.0 / HEAD_DIM) + RMS_EPS)
            return x * r * gain

        def put(ref, x):
            for i, part in enumerate(_split_heads(x, lane)):
                ref[i] = part.astype(BF16)

        for gidx in range(4):
            cols = slice(gidx * LANES, (gidx + 1) * LANES)
            aq_ref[:, cols] = (_rope_fwd(pa_ref[:, cols], ca, sa, lane, HEAD_DIM) * 0.125).astype(BF16)
            bq = norm(pb_ref[:, cols], qg_ref[...])
            bq_ref[:, cols] = (_rope_fwd(bq, cb, sb, lane, HEAD_DIM // 2) * 0.125).astype(BF16)
        put(ak_ref, _rope_fwd(pa_ref[:, 512:640], ca, sa, lane, HEAD_DIM))
        put(av_ref, pa_ref[:, 640:768])
        bk = norm(pb_ref[:, 512:640], kg_ref[...])
        put(bk_ref, _rope_fwd(bk, cb, sb, lane, HEAD_DIM // 2))
        put(bv_ref, pb_ref[:, 640:768])

    row = lambda i: (i, 0)
    full = lambda i: (0, 0)
    tab = pl.BlockSpec((ts, LANES), row)
    kv_shape = jax.ShapeDtypeStruct((4, S, LANES), BF16)
    kv_spec = pl.BlockSpec((4, ts, LANES), lambda i: (0, i, 0))
    q_shape = jax.ShapeDtypeStruct((S, 512), BF16)
    q_spec = pl.BlockSpec((ts, 512), row)
    return pl.pallas_call(
        body, name=name,
        out_shape=(q_shape, kv_shape, kv_shape, q_shape, kv_shape, kv_shape),
        grid=(S // ts,),
        in_specs=[pl.BlockSpec((ts, QKV_W), lambda i: (i, 0)), pl.BlockSpec((ts, QKV_W), lambda i: (i, 1)),
                  tab, tab, tab, tab, pl.BlockSpec((1, LANES), full), pl.BlockSpec((1, LANES), full),
                  pl.BlockSpec((LANES, LANES), full)],
        out_specs=(q_spec, kv_spec, kv_spec, q_spec, kv_spec, kv_spec),
        compiler_params=_cp(("parallel",)),
    )(proj, proj, ca, sa, cb, sb, qg2, kg2, seg)


def _unprep(dqa, dka, dva, dqb, dkb, dvb, proj, tabs, qg2, kg2, seg, *, name):
    S = proj.shape[0]
    ts = _tile(S, 256)
    ca, sa, cb, sb = tabs

    def body(dqa_ref, dka_ref, dva_ref, dqb_ref, dkb_ref, dvb_ref, pb_ref, ca_ref, sa_ref, cb_ref, sb_ref,
             qg_ref, kg_ref, seg_ref, dp_ref, dqg_ref, dkg_ref):
        @pl.when(pl.program_id(0) == 0)
        def _():
            dqg_ref[...] = jnp.zeros_like(dqg_ref)
            dkg_ref[...] = jnp.zeros_like(dkg_ref)

        lane = lax.broadcasted_iota(jnp.int32, (ts, LANES), 1)
        ca, sa, cb, sb = ca_ref[...], sa_ref[...], cb_ref[...], sb_ref[...]
        seg = seg_ref[...]

        def norm_bwd(dy, x, gain):
            r = lax.rsqrt(_head_sum(x * x, seg) * (1.0 / HEAD_DIM) + RMS_EPS)
            gdy = gain * dy
            dot = _head_sum(gdy * x, seg) * (1.0 / HEAD_DIM)
            dx = r * gdy - x * (r * r * r) * dot
            return dx, jnp.sum(dy * x * r, axis=0, keepdims=True)

        for gidx in range(4):
            cols = slice(gidx * LANES, (gidx + 1) * LANES)
            dp_ref[:, cols] = _rope_bwd(dqa_ref[:, cols] * 0.125, ca, sa, lane, HEAD_DIM).astype(BF16)
            dbq = _rope_bwd(dqb_ref[:, cols] * 0.125, cb, sb, lane, HEAD_DIM // 2)
            dx, dg = norm_bwd(dbq, pb_ref[:, cols], qg_ref[...])
            dp_ref[:, COL_B + gidx * LANES:COL_B + (gidx + 1) * LANES] = dx.astype(BF16)
            dqg_ref[...] += dg
        dak = _fold_heads(dka_ref[0] + dka_ref[1], dka_ref[2] + dka_ref[3], lane)
        dp_ref[:, 512:640] = _rope_bwd(dak, ca, sa, lane, HEAD_DIM).astype(BF16)
        dp_ref[:, 640:768] = _fold_heads(dva_ref[0] + dva_ref[1], dva_ref[2] + dva_ref[3], lane).astype(BF16)
        dbk = _fold_heads(dkb_ref[0] + dkb_ref[1], dkb_ref[2] + dkb_ref[3], lane)
        dbk = _rope_bwd(dbk, cb, sb, lane, HEAD_DIM // 2)
        dx, dg = norm_bwd(dbk, pb_ref[:, 512:640], kg_ref[...])
        dp_ref[:, COL_B + 512:COL_B + 640] = dx.astype(BF16)
        dkg_ref[...] += dg
        dp_ref[:, COL_B + 640:COL_B + 768] = _fold_heads(dvb_ref[0] + dvb_ref[1], dvb_ref[2] + dvb_ref[3],
                                                         lane).astype(BF16)

    row = lambda i: (i, 0)
    full = lambda i: (0, 0)
    tab = pl.BlockSpec((ts, LANES), row)
    q_spec = pl.BlockSpec((ts, 512), row)
    kv_spec = pl.BlockSpec((4, ts, LANES), lambda i: (0, i, 0))
    return pl.pallas_call(
        body, name=name,
        out_shape=(jax.ShapeDtypeStruct((S, 2 * QKV_W), BF16), jax.ShapeDtypeStruct((1, LANES), F32),
                   jax.ShapeDtypeStruct((1, LANES), F32)),
        grid=(S // ts,),
        in_specs=[q_spec, kv_spec, kv_spec, q_spec, kv_spec, kv_spec,
                  pl.BlockSpec((ts, QKV_W), lambda i: (i, 1)), tab, tab, tab, tab,
                  pl.BlockSpec((1, LANES), full), pl.BlockSpec((1, LANES), full), pl.BlockSpec((LANES, LANES), full)],
        out_specs=(pl.BlockSpec((ts, 2 * QKV_W), row), pl.BlockSpec((1, LANES), full),
                   pl.BlockSpec((1, LANES), full)),
        compiler_params=_cp(("arbitrary",)),
    )(dqa, dka, dva, dqb, dkb, dvb, proj, ca, sa, cb, sb, qg2, kg2, seg)


def _attn_dense_fwd(q, k4, v4, *, gather=(), name):
    S = q.shape[0]
    tq = _tile(S, 256)
    xs = [x for x, _ in gather]
    na = len(xs)

    def body(q_ref, k_ref, v_ref, *rest):
        o_ref, lse_ref = rest[na], rest[na + 1]
        if na:
            x_refs, out_refs, sems = rest[:na], rest[na + 2:2 * na + 2], rest[2 * na + 2:]
            start, finish = _gather_plan([(x_refs[a], gather[a][1], out_refs[a]) for a in range(na)], *sems)
            pl.when((pl.program_id(0) == 0) & (pl.program_id(1) == 0))(start)
        for pr in range(2):
            qp = q_ref[:, pr * LANES:(pr + 1) * LANES]
            acc = None
            for half in range(2):
                s = _dot(qp, k_ref[half], _NT)
                m = jnp.max(s, axis=-1, keepdims=True)
                e = jnp.exp(s - m)
                l = jnp.sum(e, axis=-1, keepdims=True)
                pv = _dot(e.astype(BF16), v_ref[half]) * (1.0 / l)
                acc = pv if acc is None else acc + pv
                lse_ref[pr * 2 + half] = m + jnp.log(l)
            o_ref[:, pr * LANES:(pr + 1) * LANES] = acc.astype(BF16)
        if na:
            pl.when((pl.program_id(0) == 1) & (pl.program_id(1) == S // tq - 1))(finish)

    kv_spec = pl.BlockSpec((2, S, LANES), lambda kv, i: (kv, 0, 0))
    res = pl.pallas_call(
        body, name=name,
        out_shape=(jax.ShapeDtypeStruct((S, 512), BF16), jax.ShapeDtypeStruct((8, S, 1), F32),
                   *[_gathered_shape(x) for x in xs]),
        grid=(2, S // tq),
        in_specs=[pl.BlockSpec((tq, 256), lambda kv, i: (i, kv)), kv_spec, kv_spec] + [_ANY] * na,
        out_specs=(pl.BlockSpec((tq, 256), lambda kv, i: (i, kv)),
                   pl.BlockSpec((4, tq, 1), lambda kv, i: (kv, i, 0)), *([_ANY] * na)),
        scratch_shapes=_comm_scratch(na) if na else [],
        compiler_params=_cp(("arbitrary", "arbitrary") if na else ("parallel", "parallel")),
    )(q, k4, v4, *xs)
    return res[0], res[1], list(res[2:])


def _attn_dense_bwd(q, k4, v4, lse, do, *, scatter=(), name):
    S = q.shape[0]
    tq = _tile(S, 256)
    na = len(scatter)
    comm_in, comm_out, held = _scatter_io(scatter)
    n_in = len(comm_in)

    def body(q_ref, k_ref, v_ref, lse_ref, do_ref, *rest):
        dq_ref, dk_ref, dv_ref = rest[n_in:n_in + 3]
        if na:
            s_refs, r_refs, sems = rest[:na], rest[n_in + 3:n_in + 3 + na], rest[n_in + 3 + na:]
            start, finish = _scatter_plan([(s_refs[a], r_refs[a], scatter[a][2]) for a in range(na)], *sems)
            pl.when((pl.program_id(0) == 0) & (pl.program_id(1) == 0))(start)

        @pl.when(pl.program_id(1) == 0)
        def _():
            dk_ref[...] = jnp.zeros_like(dk_ref)
            dv_ref[...] = jnp.zeros_like(dv_ref)

        lane = lax.broadcasted_iota(jnp.int32, (tq, LANES), 1)
        for pr in range(2):
            qp = q_ref[:, pr * LANES:(pr + 1) * LANES]
            dop = do_ref[:, pr * LANES:(pr + 1) * LANES].astype(BF16)
            dq = None
            for half in range(2):
                mine = (lane < HEAD_DIM) if half == 0 else (lane >= HEAD_DIM)
                s = _dot(qp, k_ref[half], _NT)
                p = jnp.exp(s - lse_ref[pr * 2 + half])
                dp = _dot(dop, v_ref[half], _NT)
                delta = jnp.sum(p * dp, axis=-1, keepdims=True)
                ds = (p * (dp - delta)).astype(BF16)
                pb = p.astype(BF16)
                d = _dot(ds, k_ref[half])
                dq = d if dq is None else dq + d
                dk_ref[half] += _dot(ds, jnp.where(mine, qp, jnp.zeros_like(qp)), _TN)
                dv_ref[half] += _dot(pb, jnp.where(mine, dop, jnp.zeros_like(dop)), _TN)
            dq_ref[:, pr * LANES:(pr + 1) * LANES] = dq
        if na:
            pl.when((pl.program_id(0) == 1) & (pl.program_id(1) == S // tq - 1))(finish)

    kv_spec = pl.BlockSpec((2, S, LANES), lambda kv, i: (kv, 0, 0))
    q_spec = pl.BlockSpec((tq, 256), lambda kv, i: (i, kv))
    res = pl.pallas_call(
        body, name=name,
        out_shape=(jax.ShapeDtypeStruct((S, 512), F32), jax.ShapeDtypeStruct((4, S, LANES), F32),
                   jax.ShapeDtypeStruct((4, S, LANES), F32), *comm_out),
        grid=(2, S // tq),
        in_specs=[q_spec, kv_spec, kv_spec, pl.BlockSpec((4, tq, 1), lambda kv, i: (kv, i, 0)), q_spec]
                 + [_ANY] * n_in,
        out_specs=(q_spec, kv_spec, kv_spec, *([_ANY] * na)),
        scratch_shapes=_comm_scratch(na) if na else [],
        input_output_aliases={5 + na + i: 3 + a for i, a in enumerate(held)},
        compiler_params=_cp(("arbitrary", "arbitrary") if na else ("parallel", "arbitrary")),
    )(q, k4, v4, lse, do, *comm_in)
    return res[0], res[1], res[2], list(res[3:])


WIN_Q = 2 * BLOCK
WIN_KEYS = WIN_Q + 2 * WINDOW


def _win_start(n, S):
    return pl.multiple_of(jnp.clip(n * WIN_Q - WINDOW, 0, S - WIN_KEYS), BLOCK)


def _win_valid(n, start):
    qpos = n * WIN_Q + lax.broadcasted_iota(jnp.int32, (WIN_Q, WIN_KEYS), 0)
    kpos = start + lax.broadcasted_iota(jnp.int32, (WIN_Q, WIN_KEYS), 1)
    return jnp.abs(qpos - kpos) <= WINDOW


def _attn_win_fwd(q, k4, v4, sink, *, name):
    S = q.shape[0]
    assert S >= WIN_KEYS

    def body(sink_ref, q_ref, k_ref, v_ref, o_ref, lse_ref):
        n = pl.program_id(0)
        start = _win_start(n, S)
        valid = _win_valid(n, start)
        for kv in range(2):
            for pr in range(2):
                cols = slice((kv * 2 + pr) * LANES, (kv * 2 + pr + 1) * LANES)
                qp = q_ref[:, cols]
                acc = None
                for half in range(2):
                    h = kv * 4 + pr * 2 + half
                    kk = k_ref[kv * 2 + half, pl.ds(start, WIN_KEYS), :]
                    vv = v_ref[kv * 2 + half, pl.ds(start, WIN_KEYS), :]
                    s = jnp.where(valid, _dot(qp, kk, _NT), NEG_BIG)
                    snk = sink_ref[h]
                    m = jnp.maximum(jnp.max(s, axis=-1, keepdims=True), snk)
                    e = jnp.exp(s - m)
                    l = jnp.sum(e, axis=-1, keepdims=True) + jnp.exp(snk - m)
                    pv = _dot(e.astype(BF16), vv) * (1.0 / l)
                    acc = pv if acc is None else acc + pv
                    lse_ref[h] = m + jnp.log(l)
                o_ref[:, cols] = acc.astype(BF16)

    kv_spec = pl.BlockSpec((4, S, LANES), lambda n: (0, 0, 0))
    return pl.pallas_call(
        body, name=name,
        out_shape=(jax.ShapeDtypeStruct((S, 512), BF16), jax.ShapeDtypeStruct((8, S, 1), F32)),
        grid=(S // WIN_Q,),
        in_specs=[pl.BlockSpec(memory_space=pltpu.SMEM), pl.BlockSpec((WIN_Q, 512), lambda n: (n, 0)),
                  kv_spec, kv_spec],
        out_specs=(pl.BlockSpec((WIN_Q, 512), lambda n: (n, 0)), pl.BlockSpec((8, WIN_Q, 1), lambda n: (0, n, 0))),
        compiler_params=_cp(("parallel",)),
    )(sink, q, k4, v4)


def _attn_win_bwd(q, k4, v4, sink, lse, do, *, name):
    S = q.shape[0]

    def body(sink_ref, q_ref, k_ref, v_ref, lse_ref, do_ref, dq_ref, dk_ref, dv_ref, dsink_ref):
        n = pl.program_id(0)

        @pl.when(n == 0)
        def _():
            dk_ref[...] = jnp.zeros_like(dk_ref)
            dv_ref[...] = jnp.zeros_like(dv_ref)
            dsink_ref[...] = jnp.zeros_like(dsink_ref)

        start = _win_start(n, S)
        valid = _win_valid(n, start)
        lane = lax.broadcasted_iota(jnp.int32, (WIN_Q, LANES), 1)
        for kv in range(2):
            for pr in range(2):
                cols = slice((kv * 2 + pr) * LANES, (kv * 2 + pr + 1) * LANES)
                qp = q_ref[:, cols]
                dop = do_ref[:, cols].astype(BF16)
                dq = None
                for half in range(2):
                    h = kv * 4 + pr * 2 + half
                    slot = kv * 2 + half
                    mine = (lane < HEAD_DIM) if half == 0 else (lane >= HEAD_DIM)
                    win = pl.ds(start, WIN_KEYS)
                    kk = k_ref[slot, win, :]
                    vv = v_ref[slot, win, :]
                    lse_h = lse_ref[h]
                    s = jnp.where(valid, _dot(qp, kk, _NT), NEG_BIG)
                    p = jnp.exp(s - lse_h)
                    dp = _dot(dop, vv, _NT)
                    delta = jnp.sum(p * dp, axis=-1, keepdims=True)
                    ds = (p * (dp - delta)).astype(BF16)
                    pb = p.astype(BF16)
                    d = _dot(ds, kk)
                    dq = d if dq is None else dq + d
                    dk_ref[slot, win, :] += _dot(ds, jnp.where(mine, qp, jnp.zeros_like(qp)), _TN)
                    dv_ref[slot, win, :] += _dot(pb, jnp.where(mine, dop, jnp.zeros_like(dop)), _TN)
                    p_sink = jnp.exp(sink_ref[h] - lse_h)
                    dsink_ref[h:h + 1, :] += jnp.broadcast_to(-jnp.sum(p_sink * delta, axis=0, keepdims=True),
                                                              (1, LANES))
                dq_ref[:, cols] = dq

    kv_spec = pl.BlockSpec((4, S, LANES), lambda n: (0, 0, 0))
    q_spec = pl.BlockSpec((WIN_Q, 512), lambda n: (n, 0))
    return pl.pallas_call(
        body, name=name,
        out_shape=(jax.ShapeDtypeStruct((S, 512), F32), jax.ShapeDtypeStruct((4, S, LANES), F32),
                   jax.ShapeDtypeStruct((4, S, LANES), F32), jax.ShapeDtypeStruct((8, LANES), F32)),
        grid=(S // WIN_Q,),
        in_specs=[pl.BlockSpec(memory_space=pltpu.SMEM), q_spec, kv_spec, kv_spec,
                  pl.BlockSpec((8, WIN_Q, 1), lambda n: (0, n, 0)), q_spec],
        out_specs=(q_spec, kv_spec, kv_spec, pl.BlockSpec((8, LANES), lambda n: (0, 0))),
        compiler_params=_cp(("arbitrary",)),
    )(sink, q, k4, v4, lse, do)


def _c_ln(v, g, b):
    mu = jnp.mean(v, axis=-1, keepdims=True)
    vc = v - mu
    r = lax.rsqrt(jnp.mean(vc * vc, axis=-1, keepdims=True) + LN_EPS)
    vh = vc * r
    return vh, r, vh * g + b


def _gmlp_fwd(proj, ws, bs3, lg, lb, *, name):
    S = proj.shape[0]

    def body(u_ref, v_ref, ws_ref, bs_ref, lg_ref, lb_ref, o_ref):
        u = _gelu(u_ref[...])
        _, _, vn = _c_ln(_gelu(v_ref[...]), lg_ref[...], lb_ref[...])
        vn = vn.astype(BF16)
        for gi in range(C_GROUPS):
            cols = slice(gi * LANES, (gi + 1) * LANES)
            mixed = _dot(ws_ref[gi], vn[:, cols]) + bs_ref[gi]
            o_ref[:, cols] = (u[:, cols] * mixed).astype(BF16)

    full2 = lambda n: (0, 0)
    full3 = lambda n: (0, 0, 0)
    return pl.pallas_call(
        body, name=name,
        out_shape=jax.ShapeDtypeStruct((S, C_WIDTH), BF16),
        grid=(S // CHUNK,),
        in_specs=[pl.BlockSpec((CHUNK, C_WIDTH), lambda n: (n, COL_C // C_WIDTH)),
                  pl.BlockSpec((CHUNK, C_WIDTH), lambda n: (n, COL_C // C_WIDTH + 1)),
                  pl.BlockSpec((C_GROUPS, CHUNK, CHUNK), full3), pl.BlockSpec((C_GROUPS, CHUNK, 1), full3),
                  pl.BlockSpec((1, C_WIDTH), full2), pl.BlockSpec((1, C_WIDTH), full2)],
        out_specs=pl.BlockSpec((CHUNK, C_WIDTH), lambda n: (n, 0)),
        compiler_params=_cp(("parallel",)),
    )(proj, proj, ws, bs3, lg, lb)


def _gmlp_bwd(proj, dout, ws, bs3, lg, lb, *, name):
    S = proj.shape[0]

    def body(u_ref, v_ref, d_ref, ws_ref, bs_ref, lg_ref, lb_ref, dz_ref, dws_ref, dbs_ref, dlg_ref, dlb_ref):
        @pl.when(pl.program_id(0) == 0)
        def _():
            dws_ref[...] = jnp.zeros_like(dws_ref)
            dbs_ref[...] = jnp.zeros_like(dbs_ref)
            dlg_ref[...] = jnp.zeros_like(dlg_ref)
            dlb_ref[...] = jnp.zeros_like(dlb_ref)

        u_pre, v_pre, d = u_ref[...], v_ref[...], d_ref[...]
        u, u_grad = _gelu_and_grad(u_pre)
        v, v_grad = _gelu_and_grad(v_pre)
        vh, r, vn = _c_ln(v, lg_ref[...], lb_ref[...])
        vnb = vn.astype(BF16)
        du_parts, dvn_parts = [], []
        for gi in range(C_GROUPS):
            cols = slice(gi * LANES, (gi + 1) * LANES)
            mixed = _dot(ws_ref[gi], vnb[:, cols]) + bs_ref[gi]
            du_parts.append(d[:, cols] * mixed)
            dm = d[:, cols] * u[:, cols]
            dbs_ref[gi] += jnp.sum(dm, axis=-1, keepdims=True)
            dmb = dm.astype(BF16)
            dws_ref[gi] += _dot(dmb, vnb[:, cols], _NT)
            dvn_parts.append(_dot(ws_ref[gi], dmb, _TN))
        du = jnp.concatenate(du_parts, axis=-1)
        dvn = jnp.concatenate(dvn_parts, axis=-1)
        dlg_ref[...] += jnp.sum(dvn * vh, axis=0, keepdims=True)
        dlb_ref[...] += jnp.sum(dvn, axis=0, keepdims=True)
        dvh = dvn * lg_ref[...]
        m1 = jnp.mean(dvh, axis=-1, keepdims=True)
        m2 = jnp.mean(dvh * vh, axis=-1, keepdims=True)
        dv = r * (dvh - m1 - vh * m2)
        dz_ref[:, :C_WIDTH] = (du * u_grad).astype(BF16)
        dz_ref[:, C_WIDTH:] = (dv * v_grad).astype(BF16)

    full2 = lambda n: (0, 0)
    full3 = lambda n: (0, 0, 0)
    return pl.pallas_call(
        body, name=name,
        out_shape=(jax.ShapeDtypeStruct((S, 2 * C_WIDTH), BF16), jax.ShapeDtypeStruct((C_GROUPS, CHUNK, CHUNK), F32),
                   jax.ShapeDtypeStruct((C_GROUPS, CHUNK, 1), F32), jax.ShapeDtypeStruct((1, C_WIDTH), F32),
                   jax.ShapeDtypeStruct((1, C_WIDTH), F32)),
        grid=(S // CHUNK,),
        in_specs=[pl.BlockSpec((CHUNK, C_WIDTH), lambda n: (n, COL_C // C_WIDTH)),
                  pl.BlockSpec((CHUNK, C_WIDTH), lambda n: (n, COL_C // C_WIDTH + 1)),
                  pl.BlockSpec((CHUNK, C_WIDTH), lambda n: (n, 0)),
                  pl.BlockSpec((C_GROUPS, CHUNK, CHUNK), full3), pl.BlockSpec((C_GROUPS, CHUNK, 1), full3),
                  pl.BlockSpec((1, C_WIDTH), full2), pl.BlockSpec((1, C_WIDTH), full2)],
        out_specs=(pl.BlockSpec((CHUNK, 2 * C_WIDTH), lambda n: (n, 0)), pl.BlockSpec((C_GROUPS, CHUNK, CHUNK), full3),
                   pl.BlockSpec((C_GROUPS, CHUNK, 1), full3), pl.BlockSpec((1, C_WIDTH), full2),
                   pl.BlockSpec((1, C_WIDTH), full2)),
        compiler_params=_cp(("arbitrary",)),
    )(proj, proj, dout, ws, bs3, lg, lb)


GATE_BLK = 512


def _gate_specs(tm, D):
    nh = D // GATE_BLK
    first = COL_GATE // GATE_BLK
    return [pl.BlockSpec((tm, GATE_BLK), functools.partial(lambda i, c: (i, c), c=first + b))
            for b in range(N_BRANCH * nh)]


def _merge_fwd(oa, ob, oc, wb, proj, bg, *, name):
    S = oa.shape[0]
    D = wb.shape[2]
    assert D % GATE_BLK == 0
    nh = D // GATE_BLK
    tm = _tile(S, 256)

    def body(oa_ref, ob_ref, oc_ref, wb_ref, *rest):
        gate_refs, bg_ref, o_ref = rest[:N_BRANCH * nh], rest[N_BRANCH * nh], rest[N_BRANCH * nh + 1]
        brs = (oa_ref[...], ob_ref[...], oc_ref[...])
        for j in range(nh):
            cols = slice(j * GATE_BLK, (j + 1) * GATE_BLK)
            acc = None
            for n in range(N_BRANCH):
                b = n * nh + j
                t = _dot(brs[n], wb_ref[n, :, cols])
                g = _sigmoid(gate_refs[b][...] + bg_ref[:, b * GATE_BLK:(b + 1) * GATE_BLK])
                acc = t * g if acc is None else acc + t * g
            o_ref[:, cols] = acc.astype(BF16)

    row = lambda i: (i, 0)
    br = pl.BlockSpec((tm, BRANCH_WIDTH), row)
    return pl.pallas_call(
        body, name=name,
        out_shape=jax.ShapeDtypeStruct((S, D), BF16),
        grid=(S // tm,),
        in_specs=[br, br, br, pl.BlockSpec((N_BRANCH, BRANCH_WIDTH, D), lambda i: (0, 0, 0))]
                 + _gate_specs(tm, D) + [pl.BlockSpec((1, N_BRANCH * D), lambda i: (0, 0))],
        out_specs=pl.BlockSpec((tm, D), row),
        compiler_params=_cp(("parallel",)),
    )(oa, ob, oc, wb, *([proj] * (N_BRANCH * nh)), bg)


def _merge_bwd(oa, ob, oc, wb, proj, bg, dmerged, *, name):
    S = oa.shape[0]
    D = wb.shape[2]
    nh = D // GATE_BLK
    tm = _tile(S, 256)

    def body(oa_ref, ob_ref, oc_ref, wb_ref, *rest):
        gate_refs = rest[:N_BRANCH * nh]
        bg_ref, dm_ref, dgl_ref, dbg_ref = rest[N_BRANCH * nh:N_BRANCH * nh + 4]
        dt_refs = rest[N_BRANCH * nh + 4:N_BRANCH * nh + 4 + N_BRANCH]
        dbr_refs = rest[N_BRANCH * nh + 4 + N_BRANCH:]

        @pl.when(pl.program_id(0) == 0)
        def _():
            dbg_ref[...] = jnp.zeros_like(dbg_ref)

        brs = (oa_ref[...], ob_ref[...], oc_ref[...])
        for n in range(N_BRANCH):
            dbr = None
            for j in range(nh):
                cols = slice(j * GATE_BLK, (j + 1) * GATE_BLK)
                b = n * nh + j
                gcols = slice(b * GATE_BLK, (b + 1) * GATE_BLK)
                w = wb_ref[n, :, cols]
                t = _dot(brs[n], w)
                g = _sigmoid(gate_refs[b][...] + bg_ref[:, gcols])
                dm = dm_ref[:, cols]
                dt = (dm * g).astype(BF16)
                dgl = dm * t * g * (1.0 - g)
                dt_refs[n][:, cols] = dt
                dgl_ref[:, gcols] = dgl.astype(BF16)
                dbg_ref[:, gcols] += jnp.sum(dgl, axis=0, keepdims=True)
                d = _dot(dt, w, _NT)
                dbr = d if dbr is None else dbr + d
            dbr_refs[n][...] = dbr

    row = lambda i: (i, 0)
    br = pl.BlockSpec((tm, BRANCH_WIDTH), row)
    res = pl.pallas_call(
        body, name=name,
        out_shape=(jax.ShapeDtypeStruct((S, N_BRANCH * D), BF16), jax.ShapeDtypeStruct((1, N_BRANCH * D), F32),
                   *([jax.ShapeDtypeStruct((S, D), BF16)] * N_BRANCH),
                   *([jax.ShapeDtypeStruct((S, BRANCH_WIDTH), F32)] * N_BRANCH)),
        grid=(S // tm,),
        in_specs=[br, br, br, pl.BlockSpec((N_BRANCH, BRANCH_WIDTH, D), lambda i: (0, 0, 0))]
                 + _gate_specs(tm, D)
                 + [pl.BlockSpec((1, N_BRANCH * D), lambda i: (0, 0)), pl.BlockSpec((tm, D), row)],
        out_specs=(pl.BlockSpec((tm, N_BRANCH * D), row), pl.BlockSpec((1, N_BRANCH * D), lambda i: (0, 0)),
                   *([pl.BlockSpec((tm, D), row)] * N_BRANCH), *([br] * N_BRANCH)),
        compiler_params=_cp(("arbitrary",)),
    )(oa, ob, oc, wb, *([proj] * (N_BRANCH * nh)), bg, dmerged)
    return res[0], res[1], list(res[2:2 + N_BRANCH]), list(res[2 + N_BRANCH:])


X_SCALE = 1.0 / math.sqrt(X_HEAD_DIM)
X_W = X_HEADS * X_HEAD_DIM


def _xattn_fwd(q, kv, *, name):
    S = q.shape[0]
    M = kv.shape[0]
    tq = _tile(S, 512)

    def body(q_ref, kv_ref, o_ref, lse_ref):
        for h in range(X_HEADS):
            cols = slice(h * LANES, (h + 1) * LANES)
            s = _dot(q_ref[:, cols], kv_ref[:, cols], _NT) * X_SCALE
            m = jnp.max(s, axis=-1, keepdims=True)
            e = jnp.exp(s - m)
            l = jnp.sum(e, axis=-1, keepdims=True)
            p = (e * (1.0 / l)).astype(BF16)
            o_ref[:, cols] = _dot(p, kv_ref[:, X_W + h * LANES:X_W + (h + 1) * LANES]).astype(BF16)
            lse_ref[h] = m + jnp.log(l)

    return pl.pallas_call(
        body, name=name,
        out_shape=(jax.ShapeDtypeStruct((S, X_W), BF16), jax.ShapeDtypeStruct((X_HEADS, S, 1), F32)),
        grid=(S // tq,),
        in_specs=[pl.BlockSpec((tq, X_W), lambda i: (i, 0)), pl.BlockSpec((M, 2 * X_W), lambda i: (0, 0))],
        out_specs=(pl.BlockSpec((tq, X_W), lambda i: (i, 0)), pl.BlockSpec((X_HEADS, tq, 1), lambda i: (0, i, 0))),
        compiler_params=_cp(("parallel",)),
    )(q, kv)


def _xattn_bwd(q, kv, lse, do, *, name):
    S = q.shape[0]
    M = kv.shape[0]
    tq = _tile(S, 512)

    def body(q_ref, kv_ref, lse_ref, do_ref, dq_ref, dkv_ref):
        @pl.when(pl.program_id(0) == 0)
        def _():
            dkv_ref[...] = jnp.zeros_like(dkv_ref)

        for h in range(X_HEADS):
            cols = slice(h * LANES, (h + 1) * LANES)
            vcols = slice(X_W + h * LANES, X_W + (h + 1) * LANES)
            qh, kh, vh = q_ref[:, cols], kv_ref[:, cols], kv_ref[:, vcols]
            doh = do_ref[:, cols].astype(BF16)
            p = jnp.exp(_dot(qh, kh, _NT) * X_SCALE - lse_ref[h])
            dp = _dot(doh, vh, _NT)
            delta = jnp.sum(p * dp, axis=-1, keepdims=True)
            ds = (p * (dp - delta) * X_SCALE).astype(BF16)
            dq_ref[:, cols] = _dot(ds, kh).astype(BF16)
            dkv_ref[:, cols] += _dot(ds, qh, _TN)
            dkv_ref[:, vcols] += _dot(p.astype(BF16), doh, _TN)

    q_spec = pl.BlockSpec((tq, X_W), lambda i: (i, 0))
    return pl.pallas_call(
        body, name=name,
        out_shape=(jax.ShapeDtypeStruct((S, X_W), BF16), jax.ShapeDtypeStruct((M, 2 * X_W), F32)),
        grid=(S // tq,),
        in_specs=[q_spec, pl.BlockSpec((M, 2 * X_W), lambda i: (0, 0)),
                  pl.BlockSpec((X_HEADS, tq, 1), lambda i: (0, i, 0)), q_spec],
        out_specs=(q_spec, pl.BlockSpec((M, 2 * X_W), lambda i: (0, 0))),
        compiler_params=_cp(("arbitrary",)),
    )(q, kv, lse, do)


def _shift_down(h, row):
    return jnp.where(row == 0, 0.0, pltpu.roll(h, 1, 0))


def _shift_up(h, row, S):
    return jnp.where(row == S - 1, 0.0, pltpu.roll(h, S - 1, 0))


def _conv3(h, ck, cb, row, S):
    return _shift_down(h, row) * ck[0:1] + h * ck[1:2] + _shift_up(h, row, S) * ck[2:3] + cb


def _conv_act_fwd(h, ck, cb, *, name):
    S, F2 = h.shape
    F = F2 // 2
    nt = F // LANES

    def body(ha_ref, hb_ref, cka_ref, ckb_ref, cba_ref, cbb_ref, o_ref):
        row = lax.broadcasted_iota(jnp.int32, (S, LANES), 0)
        a = _conv3(ha_ref[...], cka_ref[...], cba_ref[...], row, S)
        b = _conv3(hb_ref[...], ckb_ref[...], cbb_ref[...], row, S)
        o_ref[...] = (_gelu(a) * b).astype(BF16)

    ca = lambda j: (0, j)
    cbi = lambda j: (0, j + nt)
    return pl.pallas_call(
        body, name=name,
        out_shape=jax.ShapeDtypeStruct((S, F), BF16),
        grid=(nt,),
        in_specs=[pl.BlockSpec((S, LANES), ca), pl.BlockSpec((S, LANES), cbi), pl.BlockSpec((3, LANES), ca),
                  pl.BlockSpec((3, LANES), cbi), pl.BlockSpec((1, LANES), ca), pl.BlockSpec((1, LANES), cbi)],
        out_specs=pl.BlockSpec((S, LANES), ca),
        compiler_params=_cp(("parallel",)),
    )(h, h, ck, ck, cb, cb)


def _conv_act_bwd(h, ck, cb, dact, *, name):
    S, F2 = h.shape
    F = F2 // 2
    nt = F // LANES

    def body(ha_ref, hb_ref, cka_ref, ckb_ref, cba_ref, cbb_ref, d_ref,
             dha_ref, dhb_ref, dcka_ref, dckb_ref, dcba_ref, dcbb_ref):
        row = lax.broadcasted_iota(jnp.int32, (S, LANES), 0)
        ha, hb = ha_ref[...], hb_ref[...]
        cka, ckb = cka_ref[...], ckb_ref[...]
        a = _conv3(ha, cka, cba_ref[...], row, S)
        b = _conv3(hb, ckb, cbb_ref[...], row, S)
        d = d_ref[...]
        ga, ga_grad = _gelu_and_grad(a)
        da = d * b * ga_grad
        db = d * ga
        for dd, hh, ck_, dh_ref, dck_ref, dcb_ref in ((da, ha, cka, dha_ref, dcka_ref, dcba_ref),
                                                      (db, hb, ckb, dhb_ref, dckb_ref, dcbb_ref)):
            dcb_ref[...] = jnp.sum(dd, axis=0, keepdims=True)
            dck_ref[0:1, :] = jnp.sum(dd * _shift_down(hh, row), axis=0, keepdims=True)
            dck_ref[1:2, :] = jnp.sum(dd * hh, axis=0, keepdims=True)
            dck_ref[2:3, :] = jnp.sum(dd * _shift_up(hh, row, S), axis=0, keepdims=True)
            dh = _shift_up(dd, row, S) * ck_[0:1] + dd * ck_[1:2] + _shift_down(dd, row) * ck_[2:3]
            dh_ref[...] = dh.astype(BF16)

    ca = lambda j: (0, j)
    cbi = lambda j: (0, j + nt)
    col = pl.BlockSpec((S, LANES), ca)
    return pl.pallas_call(
        body, name=name,
        out_shape=(jax.ShapeDtypeStruct((S, F), BF16), jax.ShapeDtypeStruct((S, F), BF16),
                   jax.ShapeDtypeStruct((3, F), F32), jax.ShapeDtypeStruct((3, F), F32),
                   jax.ShapeDtypeStruct((1, F), F32), jax.ShapeDtypeStruct((1, F), F32)),
        grid=(nt,),
        in_specs=[col, pl.BlockSpec((S, LANES), cbi), pl.BlockSpec((3, LANES), ca), pl.BlockSpec((3, LANES), cbi),
                  pl.BlockSpec((1, LANES), ca), pl.BlockSpec((1, LANES), cbi), col],
        out_specs=(col, col, pl.BlockSpec((3, LANES), ca), pl.BlockSpec((3, LANES), ca),
                   pl.BlockSpec((1, LANES), ca), pl.BlockSpec((1, LANES), ca)),
        compiler_params=_cp(("parallel",)),
    )(h, h, ck, ck, cb, cb, dact)


def _layer_fwd(x, xb, memb, w, shards, tabs, seg, l):
    n = lambda s: f"L{l}_{s}"
    w = dict(w)
    qg2 = jnp.tile(w["b_q_gain"], 2)[None, :]
    kg2 = jnp.tile(w["b_k_gain"], 2)[None, :]
    proj = _mm(xb, w["w_in"], tb=True, name=n("proj"))
    aq, ak4, av4, bq, bk4, bv4 = _prep(proj, tabs, qg2, kg2, seg, name=n("prep"))
    oa, lse_a = _attn_win_fwd(aq, ak4, av4, w["a_sink"], name=n("attn_win"))
    gather = [(shards[k], l) for k in GATHERED_LATE] + ([(shards["w_in"], l + 1)] if l + 1 < DEPTH else [])
    ob, lse_b, gathered = _attn_dense_fwd(bq, bk4, bv4, gather=gather, name=n("attn_dense"))
    for k, g in zip(GATHERED_LATE, gathered):
        w[k] = _unshard(g, GATHER_AXIS[k])
    w_in_next = gathered[len(GATHERED_LATE)] if l + 1 < DEPTH else None
    oc = _gmlp_fwd(proj, w["c_ws"], w["c_bs3"], w["c_ln_g"], w["c_ln_b"], name=n("gmlp"))
    merged = _merge_fwd(oa, ob, oc, w["w_branch"], proj, w["b_gate"], name=n("merge"))
    x1, x1b, xh1, rs1 = _mm_res_ln(merged, w["w_mix_out"], x, w["ln1_g"], w["ln1_b"], name=n("mix_ln1"))
    xq = _mm(x1b, w["x_wq"], out_dtype=BF16, name=n("xq"))
    xkv = _mm(memb, w["x_wkv"], out_dtype=BF16, name=n("xkv"))
    xo, lse_x = _xattn_fwd(xq, xkv, name=n("xattn"))
    x2, x2b, xh2, rs2 = _mm_res_ln(xo, w["x_wo"], x1, w["ln2_g"], w["ln2_b"], name=n("xo_ln2"))
    h = _mm(x2b, w["f_w_up"], tb=True, name=n("ffn_up"))
    act = _conv_act_fwd(h, w["f_conv_k"], w["f_conv_b"], name=n("conv_act"))
    x3, x3b, xh3, rs3 = _mm_res_ln(act, w["f_w_down"], x2, w["ln3_g"], w["ln3_b"], name=n("down_ln3"))
    saved = dict(xb=xb, proj=proj, aq=aq, ak4=ak4, av4=av4, bq=bq, bk4=bk4, bv4=bv4, lse_a=lse_a, lse_b=lse_b,
                 oa=oa, ob=ob, oc=oc, merged=merged, xh1=xh1, rs1=rs1, x1b=x1b, xq=xq, xkv=xkv, xo=xo, lse_x=lse_x,
                 xh2=xh2, rs2=rs2, x2b=x2b, h=h, act=act, xh3=xh3, rs3=rs3, qg2=qg2, kg2=kg2)
    return x3, x3b, saved, w, w_in_next


def _layer_bwd(top, memb, w, sv, tabs, seg, l, ln_below, dw_in_above, recv):
    n = lambda s: f"L{l}_{s}"
    g = {}
    big = {}
    recv = dict(recv)

    def dw(key, a, b, tag):
        big[key] = _mm(a, b, ta=True, out_dtype=BF16, name=n(tag))

    def dw_t(key, segments, x, tag):
        buf = jax.ShapeDtypeStruct((sum(s.shape[1] for s in segments), x.shape[1]), BF16)
        row = 0
        for i, s in enumerate(segments):
            buf = _mm(s, x, ta=True, into=(buf, row), name=n(f"{tag}{i}"))
            row += s.shape[1]
        big[key] = buf

    dz3, dz3b, g["ln3_g"], g["ln3_b"] = top
    dw("f_w_down", sv["act"], dz3b, "dw_down")
    dact = _mm(dz3b, w["f_w_down"], tb=True, name=n("dact"))
    dha, dhb, dcka, dckb, dcba, dcbb = _conv_act_bwd(sv["h"], w["f_conv_k"], w["f_conv_b"], dact, name=n("conv_act_bwd"))
    g["f_conv_k"] = jnp.concatenate([dcka, dckb], axis=1)
    g["f_conv_b"] = jnp.concatenate([dcba, dcbb], axis=1)[0]
    dw_t("f_w_up", [dha, dhb], sv["x2b"], "dw_up")
    dz2, dz2b, g["ln2_g"], g["ln2_b"] = _mm([dha, dhb], w["f_w_up"], res=dz3, res_scale=ALPHA,
                                            ln_bwd=(sv["xh2"], sv["rs2"], w["ln2_g"]), name=n("dx2_ln2"))
    dw("x_wo", sv["xo"], dz2b, "dw_xo")
    dxo = _mm(dz2b, w["x_wo"], tb=True, out_dtype=BF16, name=n("dxo"))
    dxq, dxkv = _xattn_bwd(sv["xq"], sv["xkv"], sv["lse_x"], dxo, name=n("xattn_bwd"))
    dw("x_wq", sv["x1b"], dxq, "dw_xq")
    dw("x_wkv", memb, dxkv, "dw_xkv")
    dz1, dz1b, g["ln1_g"], g["ln1_b"] = _mm(dxq, w["x_wq"], tb=True, res=dz2, res_scale=ALPHA,
                                            ln_bwd=(sv["xh1"], sv["rs1"], w["ln1_g"]), name=n("dx1_ln1"))
    dw("w_mix_out", sv["merged"], dz1b, "dw_mix")
    dmerged = _mm(dz1b, w["w_mix_out"], tb=True, name=n("dmerged"))
    dgl, dbg, dt, dbr = _merge_bwd(sv["oa"], sv["ob"], sv["oc"], w["w_branch"], sv["proj"], w["b_gate"], dmerged,
                                   name=n("merge_bwd"))
    g["b_gate"] = dbg[0]
    for i, k in enumerate(("oa", "ob", "oc")):
        dw(f"w_branch{i}", sv[k], dt[i], f"dw_branch{i}")
    big["w_branch"] = jnp.stack([big.pop(f"w_branch{i}") for i in range(N_BRANCH)])
    dqa, dka, dva, dsink = _attn_win_bwd(sv["aq"], sv["ak4"], sv["av4"], w["a_sink"], sv["lse_a"], dbr[0],
                                         name=n("attn_win_bwd"))
    g["a_sink"] = dsink[:, 0]
    sent = [k for k in BIG if k != "w_in"]
    scatter = [(_reshard(big[k], BIG_AXIS[k]), recv[k], l) for k in sent]
    if dw_in_above is not None:
        sent.append("w_in")
        scatter.append((_reshard(dw_in_above, BIG_AXIS["w_in"]), recv["w_in"], l + 1))
    dqb, dkb, dvb, got = _attn_dense_bwd(sv["bq"], sv["bk4"], sv["bv4"], sv["lse_b"], dbr[1], scatter=scatter,
                                         name=n("attn_dense_bwd"))
    recv.update(zip(sent, got))
    dcz, g["c_ws"], dbs3, dlg, dlb = _gmlp_bwd(sv["proj"], dbr[2], w["c_ws"], w["c_bs3"], w["c_ln_g"], w["c_ln_b"],
                                               name=n("gmlp_bwd"))
    g["c_bs"] = dbs3[:, :, 0]
    g["c_ln_g"], g["c_ln_b"] = dlg[0], dlb[0]
    dqkv, dqg, dkg = _unprep(dqa, dka, dva, dqb, dkb, dvb, sv["proj"], tabs, sv["qg2"], sv["kg2"], seg, name=n("unprep"))
    g["b_q_gain"] = dqg[0, :HEAD_DIM] + dqg[0, HEAD_DIM:]
    g["b_k_gain"] = dkg[0, :HEAD_DIM] + dkg[0, HEAD_DIM:]
    dw_t("w_in", [dqkv, dcz, dgl], sv["xb"], "dw_in")
    dx0 = _mm([dqkv, dcz, dgl], w["w_in"], res=dz1, res_scale=ALPHA, ln_bwd=ln_below,
              name=n("dx0" if ln_below is None else "dx0_ln3"))
    for k in ("ln1_g", "ln1_b", "ln2_g", "ln2_b", "ln3_g", "ln3_b"):
        g[k] = g[k][0]
    return dx0, g, big["w_in"], recv


WEIGHTS = ("w_in", "b_gate", "a_sink", "b_q_gain", "b_k_gain", "c_ln_g", "c_ln_b", "c_ws", "c_bs", "w_branch",
           "w_mix_out", "ln1_g", "ln1_b", "x_wq", "x_wkv", "x_wo", "ln2_g", "ln2_b", "f_w_up", "f_conv_k",
           "f_conv_b", "f_w_down", "ln3_g", "ln3_b")
TRANSPOSED = ("w_in", "f_w_up")
BIG_AXIS = {"w_in": 0, "w_branch": 2, "w_mix_out": 0, "x_wq": 0, "x_wkv": 0, "x_wo": 1, "f_w_up": 0, "f_w_down": 0}
BIG = tuple(BIG_AXIS)
GATHERED = BIG + ("f_conv_k",)
GATHERED_LATE = tuple(k for k in GATHERED if k != "w_in")
GATHER_AXIS = dict(BIG_AXIS, f_conv_k=1)
SMALL = tuple(k for k in WEIGHTS if k not in GATHERED)


def _unshard(g, axis):
    t = jnp.moveaxis(g, 0, axis)
    return t.reshape(t.shape[:axis] + (t.shape[axis] * t.shape[axis + 1],) + t.shape[axis + 2:])


def _reshard(full, axis):
    t = full.reshape(full.shape[:axis] + (N_DEV, full.shape[axis] // N_DEV) + full.shape[axis + 1:])
    return jnp.moveaxis(t, axis, 0)


def _small_weights(small, l):
    w = {k: v[l] for k, v in small.items()}
    for k in ("c_ln_g", "c_ln_b", "ln1_g", "ln1_b", "ln2_g", "ln2_b", "ln3_g", "ln3_b", "b_gate", "f_conv_b"):
        w[k] = w[k][None, :]
    w["c_bs3"] = w["c_bs"][:, :, None]
    w["c_ws"] = w["c_ws"].astype(BF16)
    return w


def _local_step(x, mem, target, small, shards):
    S = x.shape[0]
    tabs = _rope_tables(S)
    seg = _seg_matrix()
    memb = mem.astype(BF16)
    xb = x.astype(BF16)
    saved, weights = [], []
    w_in_g = _gather_call([(shards["w_in"], 0)], name="gather_w_in_L0")[0]
    for l in range(DEPTH):
        w = dict(_small_weights(small, l), w_in=_unshard(w_in_g, GATHER_AXIS["w_in"]))
        x, xb, sv, w, w_in_g = _layer_fwd(x, xb, memb, w, shards, tabs, seg, l)
        saved.append(sv)
        weights.append(w)
    dy, loss = _loss_head(x, target, name="loss_head")
    grads = [None] * DEPTH
    recv = {k: jax.ShapeDtypeStruct((DEPTH, N_DEV) + shards[k].shape[1:], BF16) for k in BIG}
    dw_in = None
    last_ln = lambda l: (saved[l]["xh3"], saved[l]["rs3"], weights[l]["ln3_g"])
    top = _ln_bwd(dy, *last_ln(DEPTH - 1), name="ln_bwd_top")
    for l in reversed(range(DEPTH)):
        top, grads[l], dw_in, recv = _layer_bwd(top, memb, weights[l], saved[l], tabs, seg, l,
                                                last_ln(l - 1) if l > 0 else None, dw_in, recv)
    recv["w_in"] = _scatter_call([(_reshard(dw_in, BIG_AXIS["w_in"]), recv["w_in"], 0)], name="scatter_w_in_L0")[0]
    return loss, top, grads, [recv[k] for k in BIG]


PACK_W = 1024


def _gather_call(gather, *, name):
    na = len(gather)

    def body(*refs):
        start, finish = _gather_plan([(refs[a], gather[a][1], refs[na + a]) for a in range(na)], *refs[2 * na:])
        start()
        finish()

    return list(pl.pallas_call(
        body, name=name,
        out_shape=[_gathered_shape(x) for x, _ in gather],
        in_specs=[_ANY] * na, out_specs=[_ANY] * na,
        scratch_shapes=_comm_scratch(na),
    )(*[x for x, _ in gather]))


def _scatter_io(scatter):
    held = [a for a, (_, r, _) in enumerate(scatter) if not isinstance(r, jax.ShapeDtypeStruct)]
    return ([s for s, _, _ in scatter] + [scatter[a][1] for a in held],
            [jax.ShapeDtypeStruct(r.shape, r.dtype) for _, r, _ in scatter], held)


def _scatter_call(scatter, *, name):
    na = len(scatter)
    operands, out_shape, held = _scatter_io(scatter)
    n_in = len(operands)

    def body(*refs):
        start, finish = _scatter_plan([(refs[a], refs[n_in + a], scatter[a][2]) for a in range(na)],
                                      *refs[n_in + na:])
        start()
        finish()

    return list(pl.pallas_call(
        body, name=name,
        out_shape=out_shape,
        in_specs=[_ANY] * n_in, out_specs=[_ANY] * na,
        scratch_shapes=_comm_scratch(na),
        input_output_aliases={na + i: a for i, a in enumerate(held)},
    )(*operands))


def _sum_parts(parts, *, name):
    P, R, C = parts.shape
    tr = _tile(R, 64, align=8)

    def body(p_ref, o_ref):
        g = p_ref[0].astype(F32)
        for s in range(1, P):
            g = g + p_ref[s].astype(F32)
        o_ref[...] = g

    return pl.pallas_call(
        body, name=name, out_shape=jax.ShapeDtypeStruct((R, C), F32), grid=(R // tr,),
        in_specs=[pl.BlockSpec((P, tr, C), lambda i: (0, i, 0))], out_specs=pl.BlockSpec((tr, C), lambda i: (i, 0)),
        compiler_params=_cp(("parallel",)),
    )(parts)


ADAM_BLOCK_ELEMS = 512 * 1024


def _adamw(parts, w, m, v, *, name):
    L, P, R, C = parts.shape
    assert w.shape == (L, R, C), (parts.shape, w.shape)
    tr = _tile(R, max(16, ADAM_BLOCK_ELEMS // C), align=16)

    def body(p_ref, w_ref, m_ref, v_ref, g_ref, d_ref, nm_ref, nv_ref):
        g = p_ref[0].astype(F32)
        for s in range(1, P):
            g = g + p_ref[s].astype(F32)
        nm = ADAM_B1 * m_ref[...] + (1.0 - ADAM_B1) * g
        nv = ADAM_B2 * v_ref[...] + (1.0 - ADAM_B2) * (g * g)
        m_hat = nm / (1.0 - ADAM_B1 ** ADAM_STEP)
        v_hat = nv / (1.0 - ADAM_B2 ** ADAM_STEP)
        g_ref[...] = g
        d_ref[...] = -ADAM_LR * (m_hat / (jnp.sqrt(v_hat) + ADAM_EPS) + ADAM_WD * w_ref[...])
        nm_ref[...] = nm
        nv_ref[...] = nv

    blk = pl.BlockSpec((None, tr, C), lambda l, i: (l, i, 0))
    shp = jax.ShapeDtypeStruct((L, R, C), F32)
    return pl.pallas_call(
        body, name=name, out_shape=(shp, shp, shp, shp), grid=(L, R // tr),
        in_specs=[pl.BlockSpec((None, P, tr, C), lambda l, i: (l, 0, i, 0)), blk, blk, blk],
        out_specs=(blk, blk, blk, blk),
        compiler_params=_cp(("parallel", "parallel")),
    )(parts, w, m, v)


def _pad_rows(vec, width, row_align):
    n = vec.shape[0]
    rows = -(-n // width)
    rows = -(-rows // row_align) * row_align
    return jnp.pad(vec, (0, rows * width - n)).reshape(rows, width)


def kernel(x, mem, w_in, b_gate, a_sink, b_q_gain, b_k_gain, c_ln_g, c_ln_b, c_ws, c_bs, w_branch, w_mix_out, ln1_g, ln1_b, x_wq, x_wkv, x_wo, ln2_g, ln2_b, f_w_up, f_conv_k, f_conv_b, f_w_down, ln3_g, ln3_b, loss_target, m_w_in, m_b_gate, m_a_sink, m_b_q_gain, m_b_k_gain, m_c_ln_g, m_c_ln_b, m_c_ws, m_c_bs, m_w_branch, m_w_mix_out, m_ln1_g, m_ln1_b, m_x_wq, m_x_wkv, m_x_wo, m_ln2_g, m_ln2_b, m_f_w_up, m_f_conv_k, m_f_conv_b, m_f_w_down, m_ln3_g, m_ln3_b, v_w_in, v_b_gate, v_a_sink, v_b_q_gain, v_b_k_gain, v_c_ln_g, v_c_ln_b, v_c_ws, v_c_bs, v_w_branch, v_w_mix_out, v_ln1_g, v_ln1_b, v_x_wq, v_x_wkv, v_x_wo, v_ln2_g, v_ln2_b, v_f_w_up, v_f_conv_k, v_f_conv_b, v_f_w_down, v_ln3_g, v_ln3_b):
    w = dict(w_in=w_in, b_gate=b_gate, a_sink=a_sink, b_q_gain=b_q_gain, b_k_gain=b_k_gain, c_ln_g=c_ln_g,
             c_ln_b=c_ln_b, c_ws=c_ws, c_bs=c_bs, w_branch=w_branch, w_mix_out=w_mix_out, ln1_g=ln1_g, ln1_b=ln1_b,
             x_wq=x_wq, x_wkv=x_wkv, x_wo=x_wo, ln2_g=ln2_g, ln2_b=ln2_b, f_w_up=f_w_up, f_conv_k=f_conv_k,
             f_conv_b=f_conv_b, f_w_down=f_w_down, ln3_g=ln3_g, ln3_b=ln3_b)
    m = dict(w_in=m_w_in, b_gate=m_b_gate, a_sink=m_a_sink, b_q_gain=m_b_q_gain, b_k_gain=m_b_k_gain,
             c_ln_g=m_c_ln_g, c_ln_b=m_c_ln_b, c_ws=m_c_ws, c_bs=m_c_bs, w_branch=m_w_branch, w_mix_out=m_w_mix_out,
             ln1_g=m_ln1_g, ln1_b=m_ln1_b, x_wq=m_x_wq, x_wkv=m_x_wkv, x_wo=m_x_wo, ln2_g=m_ln2_g, ln2_b=m_ln2_b,
             f_w_up=m_f_w_up, f_conv_k=m_f_conv_k, f_conv_b=m_f_conv_b, f_w_down=m_f_w_down, ln3_g=m_ln3_g,
             ln3_b=m_ln3_b)
    v = dict(w_in=v_w_in, b_gate=v_b_gate, a_sink=v_a_sink, b_q_gain=v_b_q_gain, b_k_gain=v_b_k_gain,
             c_ln_g=v_c_ln_g, c_ln_b=v_c_ln_b, c_ws=v_c_ws, c_bs=v_c_bs, w_branch=v_w_branch, w_mix_out=v_w_mix_out,
             ln1_g=v_ln1_g, ln1_b=v_ln1_b, x_wq=v_x_wq, x_wkv=v_x_wkv, x_wo=v_x_wo, ln2_g=v_ln2_g, ln2_b=v_ln2_b,
             f_w_up=v_f_w_up, f_conv_k=v_f_conv_k, f_conv_b=v_f_conv_b, f_w_down=v_f_w_down, ln3_g=v_ln3_g,
             ln3_b=v_ln3_b)
    me = 4 * lax.axis_index("x") + 2 * lax.axis_index("y") + lax.axis_index("c")

    def held(k, t):
        return jnp.swapaxes(t, 1, 2) if k in TRANSPOSED else t

    shards = dict({k: held(k, w[k]).astype(BF16) for k in BIG}, f_conv_k=w["f_conv_k"])
    loss, grad_x, grads, recvs = _local_step(x[0], mem[0], loss_target[0], {k: w[k] for k in SMALL}, shards)
    loss = lax.psum(loss[0, 0], ("x", "y", "c"))

    out_g, out_d, out_m, out_v = {}, {}, {}, {}
    for k, recv in zip(BIG, recvs):
        shp = held(k, w[k]).shape
        rc = (DEPTH, math.prod(shp[1:-1]), shp[-1])
        parts = recv.reshape((DEPTH, N_DEV) + rc[1:])
        g_, d_, m_, v_ = _adamw(parts, held(k, w[k]).reshape(rc), held(k, m[k]).reshape(rc),
                                held(k, v[k]).reshape(rc), name=f"adamw_{k}")
        out_g[k], out_d[k], out_m[k], out_v[k] = (held(k, t.reshape(shp)) for t in (g_, d_, m_, v_))

    small_all = SMALL + ("f_conv_k",)
    gfull = {k: jnp.stack([grads[l][k] for l in range(DEPTH)]) for k in small_all}

    def pack(d):
        return jnp.concatenate([_pad_rows(d[k].reshape(-1), PACK_W, 8) for k in small_all])

    def unpack(rows, like):
        out, r = {}, 0
        for k in small_all:
            nr = -(-like[k].size // (8 * PACK_W)) * 8
            out[k] = rows[r:r + nr].reshape(-1)[:like[k].size].reshape(like[k].shape)
            r += nr
        return out

    gathered = _gather_call([(pack(gfull)[None], 0)], name="gather_small_grads")[0]
    sg = unpack(_sum_parts(gathered, name="sum_small_grads"), gfull)
    width = w["f_conv_k"].shape[2]
    sg["f_conv_k"] = lax.dynamic_slice_in_dim(sg["f_conv_k"], me * width, width, axis=2)
    g_, d_, m_, v_ = _adamw(pack(sg)[None, None], pack(w)[None], pack(m)[None], pack(v)[None], name="adamw_small")
    ud, um, uv = (unpack(t[0], w) for t in (d_, m_, v_))
    for k in small_all:
        out_g[k], out_d[k], out_m[k], out_v[k] = sg[k], ud[k], um[k], uv[k]

    return (loss, grad_x[None], *[out_g[k] for k in WEIGHTS], *[out_d[k] for k in WEIGHTS],
            *[out_m[k] for k in WEIGHTS], *[out_v[k] for k in WEIGHTS])
```

```python
import functools
import math

import jax
import jax.numpy as jnp
from jax import lax
from jax.experimental import pallas as pl
from jax.experimental.pallas import tpu as pltpu

F32 = jnp.float32
BF16 = jnp.bfloat16

DEPTH = 4
HEAD_DIM = 64
BLOCK = 128
WINDOW = 128
GRID_W = 64
C_WIDTH = 512
C_GROUPS = 4
CHUNK = 128
N_BRANCH = 3
BRANCH_WIDTH = 512
ROPE_THETA = 10000.0
X_HEADS = 4
X_HEAD_DIM = 128
ALPHA = (2 * DEPTH) ** 0.25
LN_EPS = 1e-5
RMS_EPS = 1e-6
ADAM_LR = 0.001
ADAM_B1 = 0.9
ADAM_B2 = 0.999
ADAM_EPS = 1e-08
ADAM_WD = 0.01
ADAM_STEP = 10
N_DEV = 8

COL_A = 0
COL_B = 768
COL_C = 1536
COL_GATE = 2560
QKV_W = 768

LANES = 128
V7X_VMEM_BYTES = 64 * 1024 * 1024
VMEM_LIMIT = V7X_VMEM_BYTES - 8 * 1024 * 1024
NEG_BIG = -1e30

_NT = (((1,), (1,)), ((), ()))
_TN = (((0,), (0,)), ((), ()))
_NN = (((1,), (0,)), ((), ()))


def _cp(sem=None):
    return pltpu.CompilerParams(dimension_semantics=sem, vmem_limit_bytes=VMEM_LIMIT)


def _tile(n, target, align=LANES):
    if n <= target:
        return n
    best = None
    for t in range(align, target + 1, align):
        if n % t == 0:
            best = t
    assert best is not None, (n, target)
    return best


def _dot(a, b, dims=_NN):
    return lax.dot_general(a, b, dims, preferred_element_type=F32)


def _gelu(x):
    return 0.5 * x * (1.0 + lax.erf(x * 0.7071067811865476))


def _gelu_and_grad(x):
    cdf = 0.5 * (1.0 + lax.erf(x * 0.7071067811865476))
    return x * cdf, cdf + x * jnp.exp(-0.5 * x * x) * 0.3989422804014327


def _sigmoid(x):
    return 1.0 / (1.0 + jnp.exp(-x))


MESH_ID = pl.DeviceIdType.MESH
_ANY = pl.BlockSpec(memory_space=pl.ANY)
COPIES_PER_ARRAY = N_DEV - 1


def _comm_scratch(n_arrays):
    return [pltpu.SemaphoreType.DMA((COPIES_PER_ARRAY * n_arrays,)),
            pltpu.SemaphoreType.DMA((COPIES_PER_ARRAY * n_arrays,)), pltpu.SemaphoreType.DMA((n_arrays,))]


def _gathered_shape(x):
    return jax.ShapeDtypeStruct((N_DEV,) + x.shape[1:], x.dtype)


def _gather_plan(entries, send_sems, recv_sems, local_sems):
    mx, my, mc = lax.axis_index("x"), lax.axis_index("y"), lax.axis_index("c")
    me, sibling = (mx, my, mc), (mx, my, 1 - mc)
    chips = [(1 - mx, my), (mx, 1 - my), (1 - mx, 1 - my)]

    def copy(a, k, block, to, from_shard=False):
        x_ref, l, out_ref = entries[a]
        dst = out_ref.at[4 * block[0] + 2 * block[1] + block[2]]
        return pltpu.make_async_remote_copy(
            src_ref=x_ref.at[l] if from_shard else dst, dst_ref=dst,
            send_sem=send_sems.at[COPIES_PER_ARRAY * a + k], recv_sem=recv_sems.at[COPIES_PER_ARRAY * a + k],
            device_id=to, device_id_type=MESH_ID)

    def own(a):
        x_ref, l, out_ref = entries[a]
        return pltpu.make_async_copy(x_ref.at[l], out_ref.at[4 * mx + 2 * my + mc], local_sems.at[a])

    def first(a):
        return [copy(a, 0, me, sibling, True)] + [copy(a, 1 + j, me, (*chip, mc), True) for j, chip in enumerate(chips)]

    def passed(a):
        return [copy(a, 4 + j, (*chip, mc), sibling) for j, chip in enumerate(chips)]

    def start():
        for a in range(len(entries)):
            own(a).start()
            for cp in first(a):
                cp.start()

    def finish():
        for a in range(len(entries)):
            fwd = passed(a)
            for j, chip in enumerate(chips):
                copy(a, 1 + j, (*chip, mc), me).wait_recv()
                fwd[j].start()
        for a in range(len(entries)):
            copy(a, 0, sibling, me).wait_recv()
            for j, chip in enumerate(chips):
                copy(a, 4 + j, (*chip, 1 - mc), me).wait_recv()
            for cp in first(a) + passed(a):
                cp.wait_send()
            own(a).wait()

    return start, finish


def _scatter_plan(entries, send_sems, recv_sems, local_sems):
    mx, my, mc = lax.axis_index("x"), lax.axis_index("y"), lax.axis_index("c")
    me = 4 * mx + 2 * my + mc

    def src(a, dev):
        return entries[a][0].at[dev]

    def copies(a):
        _, recv_ref, lr = entries[a]
        out = []
        for k in range(1, N_DEV):
            px = 1 - mx if k & 4 else mx
            py = 1 - my if k & 2 else my
            pc = 1 - mc if k & 1 else mc
            peer = 4 * px + 2 * py + pc
            sems = dict(send_sem=send_sems.at[COPIES_PER_ARRAY * a + k - 1],
                        recv_sem=recv_sems.at[COPIES_PER_ARRAY * a + k - 1],
                        device_id=(px, py, pc), device_id_type=MESH_ID)
            sends = pltpu.make_async_remote_copy(src_ref=src(a, peer), dst_ref=recv_ref.at[lr, me], **sems)
            lands = pltpu.make_async_remote_copy(src_ref=src(a, me), dst_ref=recv_ref.at[lr, peer], **sems)
            out.append((sends, lands))
        return out

    def own(a):
        _, recv_ref, lr = entries[a]
        return pltpu.make_async_copy(src(a, me), recv_ref.at[lr, me], local_sems.at[a])

    def start():
        for a in range(len(entries)):
            own(a).start()
            for sends, _ in copies(a):
                sends.start()

    def finish():
        for a in range(len(entries)):
            for _, lands in copies(a):
                lands.wait_recv()
        for a in range(len(entries)):
            for sends, _ in copies(a):
                sends.wait_send()
            own(a).wait()

    return start, finish


MM_TILE, MM_TK = 1536, 2048
MM_TILE_LN = 512


def _mm(a, b, *, ta=False, tb=False, out_dtype=F32, res=None, res_scale=1.0, into=None, ln_bwd=None, name):
    segs = list(a) if isinstance(a, (list, tuple)) else [a]
    if ta:
        (K, M), seg_k = segs[0].shape, [segs[0].shape[0]]
        assert len(segs) == 1
    else:
        M, seg_k = segs[0].shape[0], [s.shape[1] for s in segs]
        K = sum(seg_k)
    if tb:
        N, Kb = b.shape
    else:
        Kb, N = b.shape
    assert K == Kb, ([s.shape for s in segs], b.shape, ta, tb)
    row_off = into[1] if into is not None else 0
    tm, tn = _tile(math.gcd(M, row_off), MM_TILE if ln_bwd is None else MM_TILE_LN), _tile(N, MM_TILE)
    tk = _tile(K, MM_TK) if len(segs) == 1 else _tile(math.gcd(*seg_k), MM_TILE)
    nk = K // tk
    seg_chunks = [ks // tk for ks in seg_k]
    seg_first = [sum(seg_chunks[:s]) for s in range(len(segs))]
    dims = (((0 if ta else 1,), (1 if tb else 0,)), ((), ()))
    ns = len(segs)
    n_res = ns + 1
    n_ln = n_res + (res is not None)
    into_held = into is not None and not isinstance(into[0], jax.ShapeDtypeStruct)
    n_in = n_ln + (3 if ln_bwd is not None else 0) + into_held
    assert ln_bwd is None or (tn == N and into is None)

    def body(*refs):
        a_refs, b_ref = refs[:ns], refs[ns]
        r_ref = refs[n_res] if res is not None else None
        o_ref = refs[n_in]
        first_row_tile = pl.program_id(0) == 0

        def finish(out):
            if r_ref is not None:
                out = out + res_scale * r_ref[...]
            if ln_bwd is None:
                o_ref[...] = out.astype(o_ref.dtype)
                return
            xh_ref, rs_ref, g_ref = refs[n_ln:n_ln + 3]
            ob_ref, dg_ref, db_ref = refs[n_in + 1:n_in + 4]

            @pl.when(first_row_tile)
            def _():
                dg_ref[...] = jnp.zeros_like(dg_ref)
                db_ref[...] = jnp.zeros_like(db_ref)

            xh = xh_ref[...]
            dxh = out * g_ref[...]
            m1 = jnp.mean(dxh, axis=-1, keepdims=True)
            m2 = jnp.mean(dxh * xh, axis=-1, keepdims=True)
            dz = rs_ref[...] * (dxh - m1 - xh * m2)
            o_ref[...] = dz
            ob_ref[...] = dz.astype(BF16)
            dg_ref[...] += jnp.sum(out * xh, axis=0, keepdims=True)
            db_ref[...] += jnp.sum(out, axis=0, keepdims=True)

        def prod(s):
            return _dot(a_refs[s][...].astype(BF16), b_ref[...].astype(BF16), dims)

        if nk == 1:
            finish(prod(0))
            return
        acc = refs[n_in + (4 if ln_bwd is not None else 1)]
        k = pl.program_id(2)

        @pl.when(k == 0)
        def _():
            acc[...] = jnp.zeros_like(acc)

        for s in range(ns):
            def add(s=s):
                acc[...] += prod(s)
            pl.when((k >= seg_first[s]) & (k < seg_first[s] + seg_chunks[s]))(add)

        @pl.when(k == nk - 1)
        def _():
            finish(acc[...])

    if ta:
        a_specs = [pl.BlockSpec((tk, tm), lambda i, j, k: (k, i))]
    else:
        a_specs = [pl.BlockSpec((tm, tk), functools.partial(
            lambda i, j, k, first, n: (i, jnp.clip(k - first, 0, n - 1)), first=seg_first[s], n=seg_chunks[s]))
            for s in range(ns)]
    b_spec = pl.BlockSpec((tn, tk), lambda i, j, k: (j, k)) if tb else pl.BlockSpec((tk, tn), lambda i, j, k: (k, j))
    in_specs = a_specs + [b_spec]
    args = segs + [b]
    if res is not None:
        in_specs.append(pl.BlockSpec((tm, tn), lambda i, j, k: (i, j)))
        args.append(res)
    out_spec = pl.BlockSpec((tm, tn), lambda i, j, k: (i, j))
    if ln_bwd is not None:
        xh, rs, g = ln_bwd
        in_specs += [out_spec, pl.BlockSpec((tm, 1), lambda i, j, k: (i, 0)), pl.BlockSpec((1, tn), lambda i, j, k: (0, j))]
        args += [xh, rs, g]
        vec = pl.BlockSpec((1, tn), lambda i, j, k: (0, j))
        return pl.pallas_call(
            body, name=name,
            out_shape=(jax.ShapeDtypeStruct((M, N), F32), jax.ShapeDtypeStruct((M, N), BF16),
                       jax.ShapeDtypeStruct((1, N), F32), jax.ShapeDtypeStruct((1, N), F32)),
            grid=(M // tm, N // tn, nk),
            in_specs=in_specs,
            out_specs=(out_spec, out_spec, vec, vec),
            scratch_shapes=[pltpu.VMEM((tm, tn), F32)] if nk > 1 else [],
            compiler_params=_cp(("arbitrary", "arbitrary", "arbitrary")),
        )(*args)
    if into is None:
        out_shape = jax.ShapeDtypeStruct((M, N), out_dtype)
        blk_off, aliases = 0, {}
    else:
        buf = into[0]
        assert buf.shape[1] == N and row_off % tm == 0 and row_off + M <= buf.shape[0], (buf.shape, M, N, row_off)
        out_shape = jax.ShapeDtypeStruct(buf.shape, buf.dtype)
        blk_off, aliases = row_off // tm, {}
        if into_held:
            aliases = {n_in - 1: 0}
            in_specs.append(_ANY)
            args.append(buf)
    return pl.pallas_call(
        body, name=name,
        out_shape=out_shape,
        grid=(M // tm, N // tn, nk),
        in_specs=in_specs,
        out_specs=pl.BlockSpec((tm, tn), lambda i, j, k: (i + blk_off, j)),
        scratch_shapes=[pltpu.VMEM((tm, tn), F32)] if nk > 1 else [],
        input_output_aliases=aliases,
        compiler_params=_cp(("parallel", "parallel", "arbitrary")),
    )(*args)


def _mm_res_ln(a, w, x, g, b, *, name):
    S, K = a.shape
    D = w.shape[1]
    tm = _tile(S, 256)

    def body(a_ref, w_ref, x_ref, g_ref, b_ref, y_ref, yb_ref, xh_ref, rs_ref):
        h = _dot(a_ref[...], w_ref[...])
        z = ALPHA * x_ref[...] + h
        mu = jnp.mean(z, axis=-1, keepdims=True)
        zc = z - mu
        var = jnp.mean(zc * zc, axis=-1, keepdims=True)
        r = lax.rsqrt(var + LN_EPS)
        xh = zc * r
        y = xh * g_ref[...] + b_ref[...]
        y_ref[...] = y
        yb_ref[...] = y.astype(BF16)
        xh_ref[...] = xh
        rs_ref[...] = r

    row = lambda i: (i, 0)
    full = lambda i: (0, 0)
    return pl.pallas_call(
        body, name=name,
        out_shape=(jax.ShapeDtypeStruct((S, D), F32), jax.ShapeDtypeStruct((S, D), BF16),
                   jax.ShapeDtypeStruct((S, D), F32), jax.ShapeDtypeStruct((S, 1), F32)),
        grid=(S // tm,),
        in_specs=[pl.BlockSpec((tm, K), row), pl.BlockSpec((K, D), full), pl.BlockSpec((tm, D), row),
                  pl.BlockSpec((1, D), full), pl.BlockSpec((1, D), full)],
        out_specs=(pl.BlockSpec((tm, D), row), pl.BlockSpec((tm, D), row), pl.BlockSpec((tm, D), row),
                   pl.BlockSpec((tm, 1), row)),
        compiler_params=_cp(("parallel",)),
    )(a, w, x, g, b)


def _ln_bwd(dy, xh, rs, g, *, name):
    S, D = dy.shape
    tm = _tile(S, 256)

    def body(dy_ref, xh_ref, rs_ref, g_ref, dz_ref, dzb_ref, dg_ref, db_ref):
        @pl.when(pl.program_id(0) == 0)
        def _():
            dg_ref[...] = jnp.zeros_like(dg_ref)
            db_ref[...] = jnp.zeros_like(db_ref)

        dy = dy_ref[...]
        xh = xh_ref[...]
        dxh = dy * g_ref[...]
        m1 = jnp.mean(dxh, axis=-1, keepdims=True)
        m2 = jnp.mean(dxh * xh, axis=-1, keepdims=True)
        dz = rs_ref[...] * (dxh - m1 - xh * m2)
        dz_ref[...] = dz
        dzb_ref[...] = dz.astype(BF16)
        dg_ref[...] += jnp.sum(dy * xh, axis=0, keepdims=True)
        db_ref[...] += jnp.sum(dy, axis=0, keepdims=True)

    row = lambda i: (i, 0)
    full = lambda i: (0, 0)
    return pl.pallas_call(
        body, name=name,
        out_shape=(jax.ShapeDtypeStruct((S, D), F32), jax.ShapeDtypeStruct((S, D), BF16),
                   jax.ShapeDtypeStruct((1, D), F32), jax.ShapeDtypeStruct((1, D), F32)),
        grid=(S // tm,),
        in_specs=[pl.BlockSpec((tm, D), row), pl.BlockSpec((tm, D), row), pl.BlockSpec((tm, 1), row),
                  pl.BlockSpec((1, D), full)],
        out_specs=(pl.BlockSpec((tm, D), row), pl.BlockSpec((tm, D), row), pl.BlockSpec((1, D), full),
                   pl.BlockSpec((1, D), full)),
        compiler_params=_cp(("arbitrary",)),
    )(dy, xh, rs, g)


def _loss_head(y, t, *, name):
    S, D = y.shape
    tm = _tile(S, 512)

    def body(y_ref, t_ref, dy_ref, l_ref):
        @pl.when(pl.program_id(0) == 0)
        def _():
            l_ref[...] = jnp.zeros_like(l_ref)

        e = y_ref[...] - t_ref[...]
        dy_ref[...] = e / D
        l_ref[...] += 0.5 * jnp.sum(jnp.mean(e * e, axis=-1, keepdims=True), axis=0, keepdims=True)

    row = lambda i: (i, 0)
    return pl.pallas_call(
        body, name=name,
        out_shape=(jax.ShapeDtypeStruct((S, D), F32), jax.ShapeDtypeStruct((1, 1), F32)),
        grid=(S // tm,),
        in_specs=[pl.BlockSpec((tm, D), row), pl.BlockSpec((tm, D), row)],
        out_specs=(pl.BlockSpec((tm, D), row), pl.BlockSpec((1, 1), lambda i: (0, 0))),
        compiler_params=_cp(("arbitrary",)),
    )(y, t)


def _rope_tables(S):
    pos = jnp.arange(S, dtype=jnp.int32)
    row = pos // GRID_W
    col = pos % GRID_W

    def cs(p, d):
        half = d // 2
        inv = ROPE_THETA ** (-jnp.arange(half, dtype=F32) * (2.0 / d))
        ang = p.astype(F32)[:, None] * inv[None, :]
        c, s = jnp.cos(ang), jnp.sin(ang)
        return jnp.concatenate([c, c], -1), jnp.concatenate([-s, s], -1)

    ca, sa = cs(pos, HEAD_DIM)
    cr, sr = cs(row, HEAD_DIM // 2)
    cc, sc = cs(col, HEAD_DIM // 2)
    cb, sb = jnp.concatenate([cr, cc], -1), jnp.concatenate([sr, sc], -1)
    two = lambda t: jnp.concatenate([t, t], -1)
    return two(ca), two(sa), two(cb), two(sb)


def _partner(x, lane, width):
    h = width // 2
    return jnp.where(lane % width < h, pltpu.roll(x, LANES - h, 1), pltpu.roll(x, h, 1))


def _rope_fwd(x, c, s, lane, width):
    return x * c + _partner(x, lane, width) * s


def _rope_bwd(dy, c, s, lane, width):
    return dy * c + _partner(dy * s, lane, width)


def _head_sum(x, seg):
    return lax.dot_general(x, seg, _NN, precision=lax.Precision.HIGHEST, preferred_element_type=F32)


def _split_heads(x, lane):
    lo = lane < HEAD_DIM
    r = pltpu.roll(x, HEAD_DIM, 1)
    z = jnp.zeros_like(x)
    return jnp.where(lo, x, z), jnp.where(lo, z, r), jnp.where(lo, r, z), jnp.where(lo, z, x)


def _fold_heads(d0, d1, lane):
    t0 = d0 + pltpu.roll(d0, HEAD_DIM, 1)
    t1 = d1 + pltpu.roll(d1, HEAD_DIM, 1)
    return jnp.where(lane < HEAD_DIM, t0, t1)


def _seg_matrix():
    i = jnp.arange(LANES)
    return (i[:, None] // HEAD_DIM == i[None, :] // HEAD_DIM).astype(F32)


def _prep(proj, tabs, qg2, kg2, seg, *, name):
    S = proj.shape[0]
    ts = _tile(S, 256)
    ca, sa, cb, sb = tabs

    def body(pa_ref, pb_ref, ca_ref, sa_ref, cb_ref, sb_ref, qg_ref, kg_ref, seg_ref,
             aq_ref, ak_ref, av_ref, bq_ref, bk_ref, bv_ref):
        lane = lax.broadcasted_iota(jnp.int32, (ts, LANES), 1)
        ca, sa, cb, sb = ca_ref[...], sa_ref[...], cb_ref[...], sb_ref[...]
        seg = seg_ref[...]

        def norm(x, gain):
            r = lax.rsqrt(_head_sum(x * x, seg) * (1.0 / HEAD_DIM) + RMS_EPS)
            return x * r * gain

        def put(ref, x):
            for i, part in enumerate(_split_heads(x, lane)):
                ref[i] = part.astype(BF16)

        for gidx in range(4):
            cols = slice(gidx * LANES, (gidx + 1) * LANES)
            aq_ref[:, cols] = (_rope_fwd(pa_ref[:, cols], ca, sa, lane, HEAD_DIM) * 0.125).astype(BF16)
            bq = norm(pb_ref[:, cols], qg_ref[...])
            bq_ref[:, cols] = (_rope_fwd(bq, cb, sb, lane, HEAD_DIM // 2) * 0.125).astype(BF16)
        put(ak_ref, _rope_fwd(pa_ref[:, 512:640], ca, sa, lane, HEAD_DIM))
        put(av_ref, pa_ref[:, 640:768])
        bk = norm(pb_ref[:, 512:640], kg_ref[...])
        put(bk_ref, _rope_fwd(bk, cb, sb, lane, HEAD_DIM // 2))
        put(bv_ref, pb_ref[:, 640:768])

    row = lambda i: (i, 0)
    full = lambda i: (0, 0)
    tab = pl.BlockSpec((ts, LANES), row)
    kv_shape = jax.ShapeDtypeStruct((4, S, LANES), BF16)
    kv_spec = pl.BlockSpec((4, ts, LANES), lambda i: (0, i, 0))
    q_shape = jax.ShapeDtypeStruct((S, 512), BF16)
    q_spec = pl.BlockSpec((ts, 512), row)
    return pl.pallas_call(
        body, name=name,
        out_shape=(q_shape, kv_shape, kv_shape, q_shape, kv_shape, kv_shape),
        grid=(S // ts,),
        in_specs=[pl.BlockSpec((ts, QKV_W), lambda i: (i, 0)), pl.BlockSpec((ts, QKV_W), lambda i: (i, 1)),
                  tab, tab, tab, tab, pl.BlockSpec((1, LANES), full), pl.BlockSpec((1, LANES), full),
                  pl.BlockSpec((LANES, LANES), full)],
        out_specs=(q_spec, kv_spec, kv_spec, q_spec, kv_spec, kv_spec),
        compiler_params=_cp(("parallel",)),
    )(proj, proj, ca, sa, cb, sb, qg2, kg2, seg)


def _unprep(dqa, dka, dva, dqb, dkb, dvb, proj, tabs, qg2, kg2, seg, *, name):
    S = proj.shape[0]
    ts = _tile(S, 256)
    ca, sa, cb, sb = tabs

    def body(dqa_ref, dka_ref, dva_ref, dqb_ref, dkb_ref, dvb_ref, pb_ref, ca_ref, sa_ref, cb_ref, sb_ref,
             qg_ref, kg_ref, seg_ref, dp_ref, dqg_ref, dkg_ref):
        @pl.when(pl.program_id(0) == 0)
        def _():
            dqg_ref[...] = jnp.zeros_like(dqg_ref)
            dkg_ref[...] = jnp.zeros_like(dkg_ref)

        lane = lax.broadcasted_iota(jnp.int32, (ts, LANES), 1)
        ca, sa, cb, sb = ca_ref[...], sa_ref[...], cb_ref[...], sb_ref[...]
        seg = seg_ref[...]

        def norm_bwd(dy, x, gain):
            r = lax.rsqrt(_head_sum(x * x, seg) * (1.0 / HEAD_DIM) + RMS_EPS)
            gdy = gain * dy
            dot = _head_sum(gdy * x, seg) * (1.0 / HEAD_DIM)
            dx = r * gdy - x * (r * r * r) * dot
            return dx, jnp.sum(dy * x * r, axis=0, keepdims=True)

        for gidx in range(4):
            cols = slice(gidx * LANES, (gidx + 1) * LANES)
            dp_ref[:, cols] = _rope_bwd(dqa_ref[:, cols] * 0.125, ca, sa, lane, HEAD_DIM).astype(BF16)
            dbq = _rope_bwd(dqb_ref[:, cols] * 0.125, cb, sb, lane, HEAD_DIM // 2)
            dx, dg = norm_bwd(dbq, pb_ref[:, cols], qg_ref[...])
            dp_ref[:, COL_B + gidx * LANES:COL_B + (gidx + 1) * LANES] = dx.astype(BF16)
            dqg_ref[...] += dg
        dak = _fold_heads(dka_ref[0] + dka_ref[1], dka_ref[2] + dka_ref[3], lane)
        dp_ref[:, 512:640] = _rope_bwd(dak, ca, sa, lane, HEAD_DIM).astype(BF16)
        dp_ref[:, 640:768] = _fold_heads(dva_ref[0] + dva_ref[1], dva_ref[2] + dva_ref[3], lane).astype(BF16)
        dbk = _fold_heads(dkb_ref[0] + dkb_ref[1], dkb_ref[2] + dkb_ref[3], lane)
        dbk = _rope_bwd(dbk, cb, sb, lane, HEAD_DIM // 2)
        dx, dg = norm_bwd(dbk, pb_ref[:, 512:640], kg_ref[...])
        dp_ref[:, COL_B + 512:COL_B + 640] = dx.astype(BF16)
        dkg_ref[...] += dg
        dp_ref[:, COL_B + 640:COL_B + 768] = _fold_heads(dvb_ref[0] + dvb_ref[1], dvb_ref[2] + dvb_ref[3],
                                                         lane).astype(BF16)

    row = lambda i: (i, 0)
    full = lambda i: (0, 0)
    tab = pl.BlockSpec((ts, LANES), row)
    q_spec = pl.BlockSpec((ts, 512), row)
    kv_spec = pl.BlockSpec((4, ts, LANES), lambda i: (0, i, 0))
    return pl.pallas_call(
        body, name=name,
        out_shape=(jax.ShapeDtypeStruct((S, 2 * QKV_W), BF16), jax.ShapeDtypeStruct((1, LANES), F32),
                   jax.ShapeDtypeStruct((1, LANES), F32)),
        grid=(S // ts,),
        in_specs=[q_spec, kv_spec, kv_spec, q_spec, kv_spec, kv_spec,
                  pl.BlockSpec((ts, QKV_W), lambda i: (i, 1)), tab, tab, tab, tab,
                  pl.BlockSpec((1, LANES), full), pl.BlockSpec((1, LANES), full), pl.BlockSpec((LANES, LANES), full)],
        out_specs=(pl.BlockSpec((ts, 2 * QKV_W), row), pl.BlockSpec((1, LANES), full),
                   pl.BlockSpec((1, LANES), full)),
        compiler_params=_cp(("arbitrary",)),
    )(dqa, dka, dva, dqb, dkb, dvb, proj, ca, sa, cb, sb, qg2, kg2, seg)


def _attn_dense_fwd(q, k4, v4, *, gather=(), name):
    S = q.shape[0]
    tq = _tile(S, 256)
    xs = [x for x, _ in gather]
    na = len(xs)

    def body(q_ref, k_ref, v_ref, *rest):
        o_ref, lse_ref = rest[na], rest[na + 1]
        if na:
            x_refs, out_refs, sems = rest[:na], rest[na + 2:2 * na + 2], rest[2 * na + 2:]
            start, finish = _gather_plan([(x_refs[a], gather[a][1], out_refs[a]) for a in range(na)], *sems)
            pl.when((pl.program_id(0) == 0) & (pl.program_id(1) == 0))(start)
        for pr in range(2):
            qp = q_ref[:, pr * LANES:(pr + 1) * LANES]
            acc = None
            for half in range(2):
                s = _dot(qp, k_ref[half], _NT)
                m = jnp.max(s, axis=-1, keepdims=True)
                e = jnp.exp(s - m)
                l = jnp.sum(e, axis=-1, keepdims=True)
                pv = _dot(e.astype(BF16), v_ref[half]) * (1.0 / l)
                acc = pv if acc is None else acc + pv
                lse_ref[pr * 2 + half] = m + jnp.log(l)
            o_ref[:, pr * LANES:(pr + 1) * LANES] = acc.astype(BF16)
        if na:
            pl.when((pl.program_id(0) == 1) & (pl.program_id(1) == S // tq - 1))(finish)

    kv_spec = pl.BlockSpec((2, S, LANES), lambda kv, i: (kv, 0, 0))
    res = pl.pallas_call(
        body, name=name,
        out_shape=(jax.ShapeDtypeStruct((S, 512), BF16), jax.ShapeDtypeStruct((8, S, 1), F32),
                   *[_gathered_shape(x) for x in xs]),
        grid=(2, S // tq),
        in_specs=[pl.BlockSpec((tq, 256), lambda kv, i: (i, kv)), kv_spec, kv_spec] + [_ANY] * na,
        out_specs=(pl.BlockSpec((tq, 256), lambda kv, i: (i, kv)),
                   pl.BlockSpec((4, tq, 1), lambda kv, i: (kv, i, 0)), *([_ANY] * na)),
        scratch_shapes=_comm_scratch(na) if na else [],
        compiler_params=_cp(("arbitrary", "arbitrary") if na else ("parallel", "parallel")),
    )(q, k4, v4, *xs)
    return res[0], res[1], list(res[2:])


def _attn_dense_bwd(q, k4, v4, lse, do, *, scatter=(), name):
    S = q.shape[0]
    tq = _tile(S, 256)
    na = len(scatter)
    comm_in, comm_out, held = _scatter_io(scatter)
    n_in = len(comm_in)

    def body(q_ref, k_ref, v_ref, lse_ref, do_ref, *rest):
        dq_ref, dk_ref, dv_ref = rest[n_in:n_in + 3]
        if na:
            s_refs, r_refs, sems = rest[:na], rest[n_in + 3:n_in + 3 + na], rest[n_in + 3 + na:]
            start, finish = _scatter_plan([(s_refs[a], r_refs[a], scatter[a][2]) for a in range(na)], *sems)
            pl.when((pl.program_id(0) == 0) & (pl.program_id(1) == 0))(start)

        @pl.when(pl.program_id(1) == 0)
        def _():
            dk_ref[...] = jnp.zeros_like(dk_ref)
            dv_ref[...] = jnp.zeros_like(dv_ref)

        lane = lax.broadcasted_iota(jnp.int32, (tq, LANES), 1)
        for pr in range(2):
            qp = q_ref[:, pr * LANES:(pr + 1) * LANES]
            dop = do_ref[:, pr * LANES:(pr + 1) * LANES].astype(BF16)
            dq = None
            for half in range(2):
                mine = (lane < HEAD_DIM) if half == 0 else (lane >= HEAD_DIM)
                s = _dot(qp, k_ref[half], _NT)
                p = jnp.exp(s - lse_ref[pr * 2 + half])
                dp = _dot(dop, v_ref[half], _NT)
                delta = jnp.sum(p * dp, axis=-1, keepdims=True)
                ds = (p * (dp - delta)).astype(BF16)
                pb = p.astype(BF16)
                d = _dot(ds, k_ref[half])
                dq = d if dq is None else dq + d
                dk_ref[half] += _dot(ds, jnp.where(mine, qp, jnp.zeros_like(qp)), _TN)
                dv_ref[half] += _dot(pb, jnp.where(mine, dop, jnp.zeros_like(dop)), _TN)
            dq_ref[:, pr * LANES:(pr + 1) * LANES] = dq
        if na:
            pl.when((pl.program_id(0) == 1) & (pl.program_id(1) == S // tq - 1))(finish)

    kv_spec = pl.BlockSpec((2, S, LANES), lambda kv, i: (kv, 0, 0))
    q_spec = pl.BlockSpec((tq, 256), lambda kv, i: (i, kv))
    res = pl.pallas_call(
        body, name=name,
        out_shape=(jax.ShapeDtypeStruct((S, 512), F32), jax.ShapeDtypeStruct((4, S, LANES), F32),
                   jax.ShapeDtypeStruct((4, S, LANES), F32), *comm_out),
        grid=(2, S // tq),
        in_specs=[q_spec, kv_spec, kv_spec, pl.BlockSpec((4, tq, 1), lambda kv, i: (kv, i, 0)), q_spec]
                 + [_ANY] * n_in,
        out_specs=(q_spec, kv_spec, kv_spec, *([_ANY] * na)),
        scratch_shapes=_comm_scratch(na) if na else [],
        input_output_aliases={5 + na + i: 3 + a for i, a in enumerate(held)},
        compiler_params=_cp(("arbitrary", "arbitrary") if na else ("parallel", "arbitrary")),
    )(q, k4, v4, lse, do, *comm_in)
    return res[0], res[1], res[2], list(res[3:])


WIN_Q = 2 * BLOCK
WIN_KEYS = WIN_Q + 2 * WINDOW


def _win_start(n, S):
    return pl.multiple_of(jnp.clip(n * WIN_Q - WINDOW, 0, S - WIN_KEYS), BLOCK)


def _win_valid(n, start):
    qpos = n * WIN_Q + lax.broadcasted_iota(jnp.int32, (WIN_Q, WIN_KEYS), 0)
    kpos = start + lax.broadcasted_iota(jnp.int32, (WIN_Q, WIN_KEYS), 1)
    return jnp.abs(qpos - kpos) <= WINDOW


def _attn_win_fwd(q, k4, v4, sink, *, name):
    S = q.shape[0]
    assert S >= WIN_KEYS

    def body(sink_ref, q_ref, k_ref, v_ref, o_ref, lse_ref):
        n = pl.program_id(0)
        start = _win_start(n, S)
        valid = _win_valid(n, start)
        for kv in range(2):
            for pr in range(2):
                cols = slice((kv * 2 + pr) * LANES, (kv * 2 + pr + 1) * LANES)
                qp = q_ref[:, cols]
                acc = None
                for half in range(2):
                    h = kv * 4 + pr * 2 + half
                    kk = k_ref[kv * 2 + half, pl.ds(start, WIN_KEYS), :]
                    vv = v_ref[kv * 2 + half, pl.ds(start, WIN_KEYS), :]
                    s = jnp.where(valid, _dot(qp, kk, _NT), NEG_BIG)
                    snk = sink_ref[h]
                    m = jnp.maximum(jnp.max(s, axis=-1, keepdims=True), snk)
                    e = jnp.exp(s - m)
                    l = jnp.sum(e, axis=-1, keepdims=True) + jnp.exp(snk - m)
                    pv = _dot(e.astype(BF16), vv) * (1.0 / l)
                    acc = pv if acc is None else acc + pv
                    lse_ref[h] = m + jnp.log(l)
                o_ref[:, cols] = acc.astype(BF16)

    kv_spec = pl.BlockSpec((4, S, LANES), lambda n: (0, 0, 0))
    return pl.pallas_call(
        body, name=name,
        out_shape=(jax.ShapeDtypeStruct((S, 512), BF16), jax.ShapeDtypeStruct((8, S, 1), F32)),
        grid=(S // WIN_Q,),
        in_specs=[pl.BlockSpec(memory_space=pltpu.SMEM), pl.BlockSpec((WIN_Q, 512), lambda n: (n, 0)),
                  kv_spec, kv_spec],
        out_specs=(pl.BlockSpec((WIN_Q, 512), lambda n: (n, 0)), pl.BlockSpec((8, WIN_Q, 1), lambda n: (0, n, 0))),
        compiler_params=_cp(("parallel",)),
    )(sink, q, k4, v4)


def _attn_win_bwd(q, k4, v4, sink, lse, do, *, name):
    S = q.shape[0]

    def body(sink_ref, q_ref, k_ref, v_ref, lse_ref, do_ref, dq_ref, dk_ref, dv_ref, dsink_ref):
        n = pl.program_id(0)

        @pl.when(n == 0)
        def _():
            dk_ref[...] = jnp.zeros_like(dk_ref)
            dv_ref[...] = jnp.zeros_like(dv_ref)
            dsink_ref[...] = jnp.zeros_like(dsink_ref)

        start = _win_start(n, S)
        valid = _win_valid(n, start)
        lane = lax.broadcasted_iota(jnp.int32, (WIN_Q, LANES), 1)
        for kv in range(2):
            for pr in range(2):
                cols = slice((kv * 2 + pr) * LANES, (kv * 2 + pr + 1) * LANES)
                qp = q_ref[:, cols]
                dop = do_ref[:, cols].astype(BF16)
                dq = None
                for half in range(2):
                    h = kv * 4 + pr * 2 + half
                    slot = kv * 2 + half
                    mine = (lane < HEAD_DIM) if half == 0 else (lane >= HEAD_DIM)
                    win = pl.ds(start, WIN_KEYS)
                    kk = k_ref[slot, win, :]
                    vv = v_ref[slot, win, :]
                    lse_h = lse_ref[h]
                    s = jnp.where(valid, _dot(qp, kk, _NT), NEG_BIG)
                    p = jnp.exp(s - lse_h)
                    dp = _dot(dop, vv, _NT)
                    delta = jnp.sum(p * dp, axis=-1, keepdims=True)
                    ds = (p * (dp - delta)).astype(BF16)
                    pb = p.astype(BF16)
                    d = _dot(ds, kk)
                    dq = d if dq is None else dq + d
                    dk_ref[slot, win, :] += _dot(ds, jnp.where(mine, qp, jnp.zeros_like(qp)), _TN)
                    dv_ref[slot, win, :] += _dot(pb, jnp.where(mine, dop, jnp.zeros_like(dop)), _TN)
                    p_sink = jnp.exp(sink_ref[h] - lse_h)
                    dsink_ref[h:h + 1, :] += jnp.broadcast_to(-jnp.sum(p_sink * delta, axis=0, keepdims=True),
                                                              (1, LANES))
                dq_ref[:, cols] = dq

    kv_spec = pl.BlockSpec((4, S, LANES), lambda n: (0, 0, 0))
    q_spec = pl.BlockSpec((WIN_Q, 512), lambda n: (n, 0))
    return pl.pallas_call(
        body, name=name,
        out_shape=(jax.ShapeDtypeStruct((S, 512), F32), jax.ShapeDtypeStruct((4, S, LANES), F32),
                   jax.ShapeDtypeStruct((4, S, LANES), F32), jax.ShapeDtypeStruct((8, LANES), F32)),
        grid=(S // WIN_Q,),
        in_specs=[pl.BlockSpec(memory_space=pltpu.SMEM), q_spec, kv_spec, kv_spec,
                  pl.BlockSpec((8, WIN_Q, 1), lambda n: (0, n, 0)), q_spec],
        out_specs=(q_spec, kv_spec, kv_spec, pl.BlockSpec((8, LANES), lambda n: (0, 0))),
        compiler_params=_cp(("arbitrary",)),
    )(sink, q, k4, v4, lse, do)


def _c_ln(v, g, b):
    mu = jnp.mean(v, axis=-1, keepdims=True)
    vc = v - mu
    r = lax.rsqrt(jnp.mean(vc * vc, axis=-1, keepdims=True) + LN_EPS)
    vh = vc * r
    return vh, r, vh * g + b


def _gmlp_fwd(proj, ws, bs3, lg, lb, *, name):
    S = proj.shape[0]

    def body(u_ref, v_ref, ws_ref, bs_ref, lg_ref, lb_ref, o_ref):
        u = _gelu(u_ref[...])
        _, _, vn = _c_ln(_gelu(v_ref[...]), lg_ref[...], lb_ref[...])
        vn = vn.astype(BF16)
        for gi in range(C_GROUPS):
            cols = slice(gi * LANES, (gi + 1) * LANES)
            mixed = _dot(ws_ref[gi], vn[:, cols]) + bs_ref[gi]
            o_ref[:, cols] = (u[:, cols] * mixed).astype(BF16)

    full2 = lambda n: (0, 0)
    full3 = lambda n: (0, 0, 0)
    return pl.pallas_call(
        body, name=name,
        out_shape=jax.ShapeDtypeStruct((S, C_WIDTH), BF16),
        grid=(S // CHUNK,),
        in_specs=[pl.BlockSpec((CHUNK, C_WIDTH), lambda n: (n, COL_C // C_WIDTH)),
                  pl.BlockSpec((CHUNK, C_WIDTH), lambda n: (n, COL_C // C_WIDTH + 1)),
                  pl.BlockSpec((C_GROUPS, CHUNK, CHUNK), full3), pl.BlockSpec((C_GROUPS, CHUNK, 1), full3),
                  pl.BlockSpec((1, C_WIDTH), full2), pl.BlockSpec((1, C_WIDTH), full2)],
        out_specs=pl.BlockSpec((CHUNK, C_WIDTH), lambda n: (n, 0)),
        compiler_params=_cp(("parallel",)),
    )(proj, proj, ws, bs3, lg, lb)


def _gmlp_bwd(proj, dout, ws, bs3, lg, lb, *, name):
    S = proj.shape[0]

    def body(u_ref, v_ref, d_ref, ws_ref, bs_ref, lg_ref, lb_ref, dz_ref, dws_ref, dbs_ref, dlg_ref, dlb_ref):
        @pl.when(pl.program_id(0) == 0)
        def _():
            dws_ref[...] = jnp.zeros_like(dws_ref)
            dbs_ref[...] = jnp.zeros_like(dbs_ref)
            dlg_ref[...] = jnp.zeros_like(dlg_ref)
            dlb_ref[...] = jnp.zeros_like(dlb_ref)

        u_pre, v_pre, d = u_ref[...], v_ref[...], d_ref[...]
        u, u_grad = _gelu_and_grad(u_pre)
        v, v_grad = _gelu_and_grad(v_pre)
        vh, r, vn = _c_ln(v, lg_ref[...], lb_ref[...])
        vnb = vn.astype(BF16)
        du_parts, dvn_parts = [], []
        for gi in range(C_GROUPS):
            cols = slice(gi * LANES, (gi + 1) * LANES)
            mixed = _dot(ws_ref[gi], vnb[:, cols]) + bs_ref[gi]
            du_parts.append(d[:, cols] * mixed)
            dm = d[:, cols] * u[:, cols]
            dbs_ref[gi] += jnp.sum(dm, axis=-1, keepdims=True)
            dmb = dm.astype(BF16)
            dws_ref[gi] += _dot(dmb, vnb[:, cols], _NT)
            dvn_parts.append(_dot(ws_ref[gi], dmb, _TN))
        du = jnp.concatenate(du_parts, axis=-1)
        dvn = jnp.concatenate(dvn_parts, axis=-1)
        dlg_ref[...] += jnp.sum(dvn * vh, axis=0, keepdims=True)
        dlb_ref[...] += jnp.sum(dvn, axis=0, keepdims=True)
        dvh = dvn * lg_ref[...]
        m1 = jnp.mean(dvh, axis=-1, keepdims=True)
        m2 = jnp.mean(dvh * vh, axis=-1, keepdims=True)
        dv = r * (dvh - m1 - vh * m2)
        dz_ref[:, :C_WIDTH] = (du * u_grad).astype(BF16)
        dz_ref[:, C_WIDTH:] = (dv * v_grad).astype(BF16)

    full2 = lambda n: (0, 0)
    full3 = lambda n: (0, 0, 0)
    return pl.pallas_call(
        body, name=name,
        out_shape=(jax.ShapeDtypeStruct((S, 2 * C_WIDTH), BF16), jax.ShapeDtypeStruct((C_GROUPS, CHUNK, CHUNK), F32),
                   jax.ShapeDtypeStruct((C_GROUPS, CHUNK, 1), F32), jax.ShapeDtypeStruct((1, C_WIDTH), F32),
                   jax.ShapeDtypeStruct((1, C_WIDTH), F32)),
        grid=(S // CHUNK,),
        in_specs=[pl.BlockSpec((CHUNK, C_WIDTH), lambda n: (n, COL_C // C_WIDTH)),
                  pl.BlockSpec((CHUNK, C_WIDTH), lambda n: (n, COL_C // C_WIDTH + 1)),
                  pl.BlockSpec((CHUNK, C_WIDTH), lambda n: (n, 0)),
                  pl.BlockSpec((C_GROUPS, CHUNK, CHUNK), full3), pl.BlockSpec((C_GROUPS, CHUNK, 1), full3),
                  pl.BlockSpec((1, C_WIDTH), full2), pl.BlockSpec((1, C_WIDTH), full2)],
        out_specs=(pl.BlockSpec((CHUNK, 2 * C_WIDTH), lambda n: (n, 0)), pl.BlockSpec((C_GROUPS, CHUNK, CHUNK), full3),
                   pl.BlockSpec((C_GROUPS, CHUNK, 1), full3), pl.BlockSpec((1, C_WIDTH), full2),
                   pl.BlockSpec((1, C_WIDTH), full2)),
        compiler_params=_cp(("arbitrary",)),
    )(proj, proj, dout, ws, bs3, lg, lb)


GATE_BLK = 512


def _gate_specs(tm, D):
    nh = D // GATE_BLK
    first = COL_GATE // GATE_BLK
    return [pl.BlockSpec((tm, GATE_BLK), functools.partial(lambda i, c: (i, c), c=first + b))
            for b in range(N_BRANCH * nh)]


def _merge_fwd(oa, ob, oc, wb, proj, bg, *, name):
    S = oa.shape[0]
    D = wb.shape[2]
    assert D % GATE_BLK == 0
    nh = D // GATE_BLK
    tm = _tile(S, 256)

    def body(oa_ref, ob_ref, oc_ref, wb_ref, *rest):
        gate_refs, bg_ref, o_ref = rest[:N_BRANCH * nh], rest[N_BRANCH * nh], rest[N_BRANCH * nh + 1]
        brs = (oa_ref[...], ob_ref[...], oc_ref[...])
        for j in range(nh):
            cols = slice(j * GATE_BLK, (j + 1) * GATE_BLK)
            acc = None
            for n in range(N_BRANCH):
                b = n * nh + j
                t = _dot(brs[n], wb_ref[n, :, cols])
                g = _sigmoid(gate_refs[b][...] + bg_ref[:, b * GATE_BLK:(b + 1) * GATE_BLK])
                acc = t * g if acc is None else acc + t * g
            o_ref[:, cols] = acc.astype(BF16)

    row = lambda i: (i, 0)
    br = pl.BlockSpec((tm, BRANCH_WIDTH), row)
    return pl.pallas_call(
        body, name=name,
        out_shape=jax.ShapeDtypeStruct((S, D), BF16),
        grid=(S // tm,),
        in_specs=[br, br, br, pl.BlockSpec((N_BRANCH, BRANCH_WIDTH, D), lambda i: (0, 0, 0))]
                 + _gate_specs(tm, D) + [pl.BlockSpec((1, N_BRANCH * D), lambda i: (0, 0))],
        out_specs=pl.BlockSpec((tm, D), row),
        compiler_params=_cp(("parallel",)),
    )(oa, ob, oc, wb, *([proj] * (N_BRANCH * nh)), bg)


def _merge_bwd(oa, ob, oc, wb, proj, bg, dmerged, *, name):
    S = oa.shape[0]
    D = wb.shape[2]
    nh = D // GATE_BLK
    tm = _tile(S, 256)

    def body(oa_ref, ob_ref, oc_ref, wb_ref, *rest):
        gate_refs = rest[:N_BRANCH * nh]
        bg_ref, dm_ref, dgl_ref, dbg_ref = rest[N_BRANCH * nh:N_BRANCH * nh + 4]
        dt_refs = rest[N_BRANCH * nh + 4:N_BRANCH * nh + 4 + N_BRANCH]
        dbr_refs = rest[N_BRANCH * nh + 4 + N_BRANCH:]

        @pl.when(pl.program_id(0) == 0)
        def _():
            dbg_ref[...] = jnp.zeros_like(dbg_ref)

        brs = (oa_ref[...], ob_ref[...], oc_ref[...])
        for n in range(N_BRANCH):
            dbr = None
            for j in range(nh):
                cols = slice(j * GATE_BLK, (j + 1) * GATE_BLK)
                b = n * nh + j
                gcols = slice(b * GATE_BLK, (b + 1) * GATE_BLK)
                w = wb_ref[n, :, cols]
                t = _dot(brs[n], w)
                g = _sigmoid(gate_refs[b][...] + bg_ref[:, gcols])
                dm = dm_ref[:, cols]
                dt = (dm * g).astype(BF16)
                dgl = dm * t * g * (1.0 - g)
                dt_refs[n][:, cols] = dt
                dgl_ref[:, gcols] = dgl.astype(BF16)
                dbg_ref[:, gcols] += jnp.sum(dgl, axis=0, keepdims=True)
                d = _dot(dt, w, _NT)
                dbr = d if dbr is None else dbr + d
            dbr_refs[n][...] = dbr

    row = lambda i: (i, 0)
    br = pl.BlockSpec((tm, BRANCH_WIDTH), row)
    res = pl.pallas_call(
        body, name=name,
        out_shape=(jax.ShapeDtypeStruct((S, N_BRANCH * D), BF16), jax.ShapeDtypeStruct((1, N_BRANCH * D), F32),
                   *([jax.ShapeDtypeStruct((S, D), BF16)] * N_BRANCH),
                   *([jax.ShapeDtypeStruct((S, BRANCH_WIDTH), F32)] * N_BRANCH)),
        grid=(S // tm,),
        in_specs=[br, br, br, pl.BlockSpec((N_BRANCH, BRANCH_WIDTH, D), lambda i: (0, 0, 0))]
                 + _gate_specs(tm, D)
                 + [pl.BlockSpec((1, N_BRANCH * D), lambda i: (0, 0)), pl.BlockSpec((tm, D), row)],
        out_specs=(pl.BlockSpec((tm, N_BRANCH * D), row), pl.BlockSpec((1, N_BRANCH * D), lambda i: (0, 0)),
                   *([pl.BlockSpec((tm, D), row)] * N_BRANCH), *([br] * N_BRANCH)),
        compiler_params=_cp(("arbitrary",)),
    )(oa, ob, oc, wb, *([proj] * (N_BRANCH * nh)), bg, dmerged)
    return res[0], res[1], list(res[2:2 + N_BRANCH]), list(res[2 + N_BRANCH:])


X_SCALE = 1.0 / math.sqrt(X_HEAD_DIM)
X_W = X_HEADS * X_HEAD_DIM


def _xattn_fwd(q, kv, *, name):
    S = q.shape[0]
    M = kv.shape[0]
    tq = _tile(S, 512)

    def body(q_ref, kv_ref, o_ref, lse_ref):
        for h in range(X_HEADS):
            cols = slice(h * LANES, (h + 1) * LANES)
            s = _dot(q_ref[:, cols], kv_ref[:, cols], _NT) * X_SCALE
            m = jnp.max(s, axis=-1, keepdims=True)
            e = jnp.exp(s - m)
            l = jnp.sum(e, axis=-1, keepdims=True)
            p = (e * (1.0 / l)).astype(BF16)
            o_ref[:, cols] = _dot(p, kv_ref[:, X_W + h * LANES:X_W + (h + 1) * LANES]).astype(BF16)
            lse_ref[h] = m + jnp.log(l)

    return pl.pallas_call(
        body, name=name,
        out_shape=(jax.ShapeDtypeStruct((S, X_W), BF16), jax.ShapeDtypeStruct((X_HEADS, S, 1), F32)),
        grid=(S // tq,),
        in_specs=[pl.BlockSpec((tq, X_W), lambda i: (i, 0)), pl.BlockSpec((M, 2 * X_W), lambda i: (0, 0))],
        out_specs=(pl.BlockSpec((tq, X_W), lambda i: (i, 0)), pl.BlockSpec((X_HEADS, tq, 1), lambda i: (0, i, 0))),
        compiler_params=_cp(("parallel",)),
    )(q, kv)


def _xattn_bwd(q, kv, lse, do, *, name):
    S = q.shape[0]
    M = kv.shape[0]
    tq = _tile(S, 512)

    def body(q_ref, kv_ref, lse_ref, do_ref, dq_ref, dkv_ref):
        @pl.when(pl.program_id(0) == 0)
        def _():
            dkv_ref[...] = jnp.zeros_like(dkv_ref)

        for h in range(X_HEADS):
            cols = slice(h * LANES, (h + 1) * LANES)
            vcols = slice(X_W + h * LANES, X_W + (h + 1) * LANES)
            qh, kh, vh = q_ref[:, cols], kv_ref[:, cols], kv_ref[:, vcols]
            doh = do_ref[:, cols].astype(BF16)
            p = jnp.exp(_dot(qh, kh, _NT) * X_SCALE - lse_ref[h])
            dp = _dot(doh, vh, _NT)
            delta = jnp.sum(p * dp, axis=-1, keepdims=True)
            ds = (p * (dp - delta) * X_SCALE).astype(BF16)
            dq_ref[:, cols] = _dot(ds, kh).astype(BF16)
            dkv_ref[:, cols] += _dot(ds, qh, _TN)
            dkv_ref[:, vcols] += _dot(p.astype(BF16), doh, _TN)

    q_spec = pl.BlockSpec((tq, X_W), lambda i: (i, 0))
    return pl.pallas_call(
        body, name=name,
        out_shape=(jax.ShapeDtypeStruct((S, X_W), BF16), jax.ShapeDtypeStruct((M, 2 * X_W), F32)),
        grid=(S // tq,),
        in_specs=[q_spec, pl.BlockSpec((M, 2 * X_W), lambda i: (0, 0)),
                  pl.BlockSpec((X_HEADS, tq, 1), lambda i: (0, i, 0)), q_spec],
        out_specs=(q_spec, pl.BlockSpec((M, 2 * X_W), lambda i: (0, 0))),
        compiler_params=_cp(("arbitrary",)),
    )(q, kv, lse, do)


def _shift_down(h, row):
    return jnp.where(row == 0, 0.0, pltpu.roll(h, 1, 0))


def _shift_up(h, row, S):
    return jnp.where(row == S - 1, 0.0, pltpu.roll(h, S - 1, 0))


def _conv3(h, ck, cb, row, S):
    return _shift_down(h, row) * ck[0:1] + h * ck[1:2] + _shift_up(h, row, S) * ck[2:3] + cb


def _conv_act_fwd(h, ck, cb, *, name):
    S, F2 = h.shape
    F = F2 // 2
    nt = F // LANES

    def body(ha_ref, hb_ref, cka_ref, ckb_ref, cba_ref, cbb_ref, o_ref):
        row = lax.broadcasted_iota(jnp.int32, (S, LANES), 0)
        a = _conv3(ha_ref[...], cka_ref[...], cba_ref[...], row, S)
        b = _conv3(hb_ref[...], ckb_ref[...], cbb_ref[...], row, S)
        o_ref[...] = (_gelu(a) * b).astype(BF16)

    ca = lambda j: (0, j)
    cbi = lambda j: (0, j + nt)
    return pl.pallas_call(
        body, name=name,
        out_shape=jax.ShapeDtypeStruct((S, F), BF16),
        grid=(nt,),
        in_specs=[pl.BlockSpec((S, LANES), ca), pl.BlockSpec((S, LANES), cbi), pl.BlockSpec((3, LANES), ca),
                  pl.BlockSpec((3, LANES), cbi), pl.BlockSpec((1, LANES), ca), pl.BlockSpec((1, LANES), cbi)],
        out_specs=pl.BlockSpec((S, LANES), ca),
        compiler_params=_cp(("parallel",)),
    )(h, h, ck, ck, cb, cb)


def _conv_act_bwd(h, ck, cb, dact, *, name):
    S, F2 = h.shape
    F = F2 // 2
    nt = F // LANES

    def body(ha_ref, hb_ref, cka_ref, ckb_ref, cba_ref, cbb_ref, d_ref,
             dha_ref, dhb_ref, dcka_ref, dckb_ref, dcba_ref, dcbb_ref):
        row = lax.broadcasted_iota(jnp.int32, (S, LANES), 0)
        ha, hb = ha_ref[...], hb_ref[...]
        cka, ckb = cka_ref[...], ckb_ref[...]
        a = _conv3(ha, cka, cba_ref[...], row, S)
        b = _conv3(hb, ckb, cbb_ref[...], row, S)
        d = d_ref[...]
        ga, ga_grad = _gelu_and_grad(a)
        da = d * b * ga_grad
        db = d * ga
        for dd, hh, ck_, dh_ref, dck_ref, dcb_ref in ((da, ha, cka, dha_ref, dcka_ref, dcba_ref),
                                                      (db, hb, ckb, dhb_ref, dckb_ref, dcbb_ref)):
            dcb_ref[...] = jnp.sum(dd, axis=0, keepdims=True)
            dck_ref[0:1, :] = jnp.sum(dd * _shift_down(hh, row), axis=0, keepdims=True)
            dck_ref[1:2, :] = jnp.sum(dd * hh, axis=0, keepdims=True)
            dck_ref[2:3, :] = jnp.sum(dd * _shift_up(hh, row, S), axis=0, keepdims=True)
            dh = _shift_up(dd, row, S) * ck_[0:1] + dd * ck_[1:2] + _shift_down(dd, row) * ck_[2:3]
            dh_ref[...] = dh.astype(BF16)

    ca = lambda j: (0, j)
    cbi = lambda j: (0, j + nt)
    col = pl.BlockSpec((S, LANES), ca)
    return pl.pallas_call(
        body, name=name,
        out_shape=(jax.ShapeDtypeStruct((S, F), BF16), jax.ShapeDtypeStruct((S, F), BF16),
                   jax.ShapeDtypeStruct((3, F), F32), jax.ShapeDtypeStruct((3, F), F32),
                   jax.ShapeDtypeStruct((1, F), F32), jax.ShapeDtypeStruct((1, F), F32)),
        grid=(nt,),
        in_specs=[col, pl.BlockSpec((S, LANES), cbi), pl.BlockSpec((3, LANES), ca), pl.BlockSpec((3, LANES), cbi),
                  pl.BlockSpec((1, LANES), ca), pl.BlockSpec((1, LANES), cbi), col],
        out_specs=(col, col, pl.BlockSpec((3, LANES), ca), pl.BlockSpec((3, LANES), ca),
                   pl.BlockSpec((1, LANES), ca), pl.BlockSpec((1, LANES), ca)),
        compiler_params=_cp(("parallel",)),
    )(h, h, ck, ck, cb, cb, dact)


def _layer_fwd(x, xb, memb, w, shards, tabs, seg, l):
    n = lambda s: f"L{l}_{s}"
    w = dict(w)
    qg2 = jnp.tile(w["b_q_gain"], 2)[None, :]
    kg2 = jnp.tile(w["b_k_gain"], 2)[None, :]
    proj = _mm(xb, w["w_in"], tb=True, name=n("proj"))
    aq, ak4, av4, bq, bk4, bv4 = _prep(proj, tabs, qg2, kg2, seg, name=n("prep"))
    oa, lse_a = _attn_win_fwd(aq, ak4, av4, w["a_sink"], name=n("attn_win"))
    gather = [(shards[k], l) for k in GATHERED_LATE] + ([(shards["w_in"], l + 1)] if l + 1 < DEPTH else [])
    ob, lse_b, gathered = _attn_dense_fwd(bq, bk4, bv4, gather=gather, name=n("attn_dense"))
    for k, g in zip(GATHERED_LATE, gathered):
        w[k] = _unshard(g, GATHER_AXIS[k])
    w_in_next = gathered[len(GATHERED_LATE)] if l + 1 < DEPTH else None
    oc = _gmlp_fwd(proj, w["c_ws"], w["c_bs3"], w["c_ln_g"], w["c_ln_b"], name=n("gmlp"))
    merged = _merge_fwd(oa, ob, oc, w["w_branch"], proj, w["b_gate"], name=n("merge"))
    x1, x1b, xh1, rs1 = _mm_res_ln(merged, w["w_mix_out"], x, w["ln1_g"], w["ln1_b"], name=n("mix_ln1"))
    xq = _mm(x1b, w["x_wq"], out_dtype=BF16, name=n("xq"))
    xkv = _mm(memb, w["x_wkv"], out_dtype=BF16, name=n("xkv"))
    xo, lse_x = _xattn_fwd(xq, xkv, name=n("xattn"))
    x2, x2b, xh2, rs2 = _mm_res_ln(xo, w["x_wo"], x1, w["ln2_g"], w["ln2_b"], name=n("xo_ln2"))
    h = _mm(x2b, w["f_w_up"], tb=True, name=n("ffn_up"))
    act = _conv_act_fwd(h, w["f_conv_k"], w["f_conv_b"], name=n("conv_act"))
    x3, x3b, xh3, rs3 = _mm_res_ln(act, w["f_w_down"], x2, w["ln3_g"], w["ln3_b"], name=n("down_ln3"))
    saved = dict(xb=xb, proj=proj, aq=aq, ak4=ak4, av4=av4, bq=bq, bk4=bk4, bv4=bv4, lse_a=lse_a, lse_b=lse_b,
                 oa=oa, ob=ob, oc=oc, merged=merged, xh1=xh1, rs1=rs1, x1b=x1b, xq=xq, xkv=xkv, xo=xo, lse_x=lse_x,
                 xh2=xh2, rs2=rs2, x2b=x2b, h=h, act=act, xh3=xh3, rs3=rs3, qg2=qg2, kg2=kg2)
    return x3, x3b, saved, w, w_in_next


def _layer_bwd(top, memb, w, sv, tabs, seg, l, ln_below, dw_in_above, recv):
    n = lambda s: f"L{l}_{s}"
    g = {}
    big = {}
    recv = dict(recv)

    def dw(key, a, b, tag):
        big[key] = _mm(a, b, ta=True, out_dtype=BF16, name=n(tag))

    def dw_t(key, segments, x, tag):
        buf = jax.ShapeDtypeStruct((sum(s.shape[1] for s in segments), x.shape[1]), BF16)
        row = 0
        for i, s in enumerate(segments):
            buf = _mm(s, x, ta=True, into=(buf, row), name=n(f"{tag}{i}"))
            row += s.shape[1]
        big[key] = buf

    dz3, dz3b, g["ln3_g"], g["ln3_b"] = top
    dw("f_w_down", sv["act"], dz3b, "dw_down")
    dact = _mm(dz3b, w["f_w_down"], tb=True, name=n("dact"))
    dha, dhb, dcka, dckb, dcba, dcbb = _conv_act_bwd(sv["h"], w["f_conv_k"], w["f_conv_b"], dact, name=n("conv_act_bwd"))
    g["f_conv_k"] = jnp.concatenate([dcka, dckb], axis=1)
    g["f_conv_b"] = jnp.concatenate([dcba, dcbb], axis=1)[0]
    dw_t("f_w_up", [dha, dhb], sv["x2b"], "dw_up")
    dz2, dz2b, g["ln2_g"], g["ln2_b"] = _mm([dha, dhb], w["f_w_up"], res=dz3, res_scale=ALPHA,
                                            ln_bwd=(sv["xh2"], sv["rs2"], w["ln2_g"]), name=n("dx2_ln2"))
    dw("x_wo", sv["xo"], dz2b, "dw_xo")
    dxo = _mm(dz2b, w["x_wo"], tb=True, out_dtype=BF16, name=n("dxo"))
    dxq, dxkv = _xattn_bwd(sv["xq"], sv["xkv"], sv["lse_x"], dxo, name=n("xattn_bwd"))
    dw("x_wq", sv["x1b"], dxq, "dw_xq")
    dw("x_wkv", memb, dxkv, "dw_xkv")
    dz1, dz1b, g["ln1_g"], g["ln1_b"] = _mm(dxq, w["x_wq"], tb=True, res=dz2, res_scale=ALPHA,
                                            ln_bwd=(sv["xh1"], sv["rs1"], w["ln1_g"]), name=n("dx1_ln1"))
    dw("w_mix_out", sv["merged"], dz1b, "dw_mix")
    dmerged = _mm(dz1b, w["w_mix_out"], tb=True, name=n("dmerged"))
    dgl, dbg, dt, dbr = _merge_bwd(sv["oa"], sv["ob"], sv["oc"], w["w_branch"], sv["proj"], w["b_gate"], dmerged,
                                   name=n("merge_bwd"))
    g["b_gate"] = dbg[0]
    for i, k in enumerate(("oa", "ob", "oc")):
        dw(f"w_branch{i}", sv[k], dt[i], f"dw_branch{i}")
    big["w_branch"] = jnp.stack([big.pop(f"w_branch{i}") for i in range(N_BRANCH)])
    dqa, dka, dva, dsink = _attn_win_bwd(sv["aq"], sv["ak4"], sv["av4"], w["a_sink"], sv["lse_a"], dbr[0],
                                         name=n("attn_win_bwd"))
    g["a_sink"] = dsink[:, 0]
    sent = [k for k in BIG if k != "w_in"]
    scatter = [(_reshard(big[k], BIG_AXIS[k]), recv[k], l) for k in sent]
    if dw_in_above is not None:
        sent.append("w_in")
        scatter.append((_reshard(dw_in_above, BIG_AXIS["w_in"]), recv["w_in"], l + 1))
    dqb, dkb, dvb, got = _attn_dense_bwd(sv["bq"], sv["bk4"], sv["bv4"], sv["lse_b"], dbr[1], scatter=scatter,
                                         name=n("attn_dense_bwd"))
    recv.update(zip(sent, got))
    dcz, g["c_ws"], dbs3, dlg, dlb = _gmlp_bwd(sv["proj"], dbr[2], w["c_ws"], w["c_bs3"], w["c_ln_g"], w["c_ln_b"],
                                               name=n("gmlp_bwd"))
    g["c_bs"] = dbs3[:, :, 0]
    g["c_ln_g"], g["c_ln_b"] = dlg[0], dlb[0]
    dqkv, dqg, dkg = _unprep(dqa, dka, dva, dqb, dkb, dvb, sv["proj"], tabs, sv["qg2"], sv["kg2"], seg, name=n("unprep"))
    g["b_q_gain"] = dqg[0, :HEAD_DIM] + dqg[0, HEAD_DIM:]
    g["b_k_gain"] = dkg[0, :HEAD_DIM] + dkg[0, HEAD_DIM:]
    dw_t("w_in", [dqkv, dcz, dgl], sv["xb"], "dw_in")
    dx0 = _mm([dqkv, dcz, dgl], w["w_in"], res=dz1, res_scale=ALPHA, name=n("dx0"))
    if ln_below is not None:
        dx0 = _ln_bwd(dx0, *ln_below, name=n("ln_bwd_below"))
    for k in ("ln1_g", "ln1_b", "ln2_g", "ln2_b", "ln3_g", "ln3_b"):
        g[k] = g[k][0]
    return dx0, g, big["w_in"], recv


WEIGHTS = ("w_in", "b_gate", "a_sink", "b_q_gain", "b_k_gain", "c_ln_g", "c_ln_b", "c_ws", "c_bs", "w_branch",
           "w_mix_out", "ln1_g", "ln1_b", "x_wq", "x_wkv", "x_wo", "ln2_g", "ln2_b", "f_w_up", "f_conv_k",
           "f_conv_b", "f_w_down", "ln3_g", "ln3_b")
TRANSPOSED = ("w_in", "f_w_up")
BIG_AXIS = {"w_in": 0, "w_branch": 2, "w_mix_out": 0, "x_wq": 0, "x_wkv": 0, "x_wo": 1, "f_w_up": 0, "f_w_down": 0}
BIG = tuple(BIG_AXIS)
GATHERED = BIG + ("f_conv_k",)
GATHERED_LATE = tuple(k for k in GATHERED if k != "w_in")
GATHER_AXIS = dict(BIG_AXIS, f_conv_k=1)
SMALL = tuple(k for k in WEIGHTS if k not in GATHERED)


def _unshard(g, axis):
    t = jnp.moveaxis(g, 0, axis)
    return t.reshape(t.shape[:axis] + (t.shape[axis] * t.shape[axis + 1],) + t.shape[axis + 2:])


def _reshard(full, axis):
    t = full.reshape(full.shape[:axis] + (N_DEV, full.shape[axis] // N_DEV) + full.shape[axis + 1:])
    return jnp.moveaxis(t, axis, 0)


def _small_weights(small, l):
    w = {k: v[l] for k, v in small.items()}
    for k in ("c_ln_g", "c_ln_b", "ln1_g", "ln1_b", "ln2_g", "ln2_b", "ln3_g", "ln3_b", "b_gate", "f_conv_b"):
        w[k] = w[k][None, :]
    w["c_bs3"] = w["c_bs"][:, :, None]
    w["c_ws"] = w["c_ws"].astype(BF16)
    return w


def _local_step(x, mem, target, small, shards):
    S = x.shape[0]
    tabs = _rope_tables(S)
    seg = _seg_matrix()
    memb = mem.astype(BF16)
    xb = x.astype(BF16)
    saved, weights = [], []
    w_in_g = _gather_call([(shards["w_in"], 0)], name="gather_w_in_L0")[0]
    for l in range(DEPTH):
        w = dict(_small_weights(small, l), w_in=_unshard(w_in_g, GATHER_AXIS["w_in"]))
        x, xb, sv, w, w_in_g = _layer_fwd(x, xb, memb, w, shards, tabs, seg, l)
        saved.append(sv)
        weights.append(w)
    dy, loss = _loss_head(x, target, name="loss_head")
    grads = [None] * DEPTH
    recv = {k: jax.ShapeDtypeStruct((DEPTH, N_DEV) + shards[k].shape[1:], BF16) for k in BIG}
    dw_in = None
    last_ln = lambda l: (saved[l]["xh3"], saved[l]["rs3"], weights[l]["ln3_g"])
    top = _ln_bwd(dy, *last_ln(DEPTH - 1), name="ln_bwd_top")
    for l in reversed(range(DEPTH)):
        top, grads[l], dw_in, recv = _layer_bwd(top, memb, weights[l], saved[l], tabs, seg, l,
                                                last_ln(l - 1) if l > 0 else None, dw_in, recv)
    recv["w_in"] = _scatter_call([(_reshard(dw_in, BIG_AXIS["w_in"]), recv["w_in"], 0)], name="scatter_w_in_L0")[0]
    return loss, top, grads, [recv[k] for k in BIG]


PACK_W = 1024


def _gather_call(gather, *, name):
    na = len(gather)

    def body(*refs):
        start, finish = _gather_plan([(refs[a], gather[a][1], refs[na + a]) for a in range(na)], *refs[2 * na:])
        start()
        finish()

    return list(pl.pallas_call(
        body, name=name,
        out_shape=[_gathered_shape(x) for x, _ in gather],
        in_specs=[_ANY] * na, out_specs=[_ANY] * na,
        scratch_shapes=_comm_scratch(na),
    )(*[x for x, _ in gather]))


def _scatter_io(scatter):
    held = [a for a, (_, r, _) in enumerate(scatter) if not isinstance(r, jax.ShapeDtypeStruct)]
    return ([s for s, _, _ in scatter] + [scatter[a][1] for a in held],
            [jax.ShapeDtypeStruct(r.shape, r.dtype) for _, r, _ in scatter], held)


def _scatter_call(scatter, *, name):
    na = len(scatter)
    operands, out_shape, held = _scatter_io(scatter)
    n_in = len(operands)

    def body(*refs):
        start, finish = _scatter_plan([(refs[a], refs[n_in + a], scatter[a][2]) for a in range(na)],
                                      *refs[n_in + na:])
        start()
        finish()

    return list(pl.pallas_call(
        body, name=name,
        out_shape=out_shape,
        in_specs=[_ANY] * n_in, out_specs=[_ANY] * na,
        scratch_shapes=_comm_scratch(na),
        input_output_aliases={na + i: a for i, a in enumerate(held)},
    )(*operands))


def _sum_parts(parts, *, name):
    P, R, C = parts.shape
    tr = _tile(R, 64, align=8)

    def body(p_ref, o_ref):
        g = p_ref[0].astype(F32)
        for s in range(1, P):
            g = g + p_ref[s].astype(F32)
        o_ref[...] = g

    return pl.pallas_call(
        body, name=name, out_shape=jax.ShapeDtypeStruct((R, C), F32), grid=(R // tr,),
        in_specs=[pl.BlockSpec((P, tr, C), lambda i: (0, i, 0))], out_specs=pl.BlockSpec((tr, C), lambda i: (i, 0)),
        compiler_params=_cp(("parallel",)),
    )(parts)


ADAM_BLOCK_ELEMS = 512 * 1024


def _adamw(parts, w, m, v, *, name):
    L, P, R, C = parts.shape
    assert w.shape == (L, R, C), (parts.shape, w.shape)
    tr = _tile(R, max(16, ADAM_BLOCK_ELEMS // C), align=16)

    def body(p_ref, w_ref, m_ref, v_ref, g_ref, d_ref, nm_ref, nv_ref):
        g = p_ref[0].astype(F32)
        for s in range(1, P):
            g = g + p_ref[s].astype(F32)
        nm = ADAM_B1 * m_ref[...] + (1.0 - ADAM_B1) * g
        nv = ADAM_B2 * v_ref[...] + (1.0 - ADAM_B2) * (g * g)
        m_hat = nm / (1.0 - ADAM_B1 ** ADAM_STEP)
        v_hat = nv / (1.0 - ADAM_B2 ** ADAM_STEP)
        g_ref[...] = g
        d_ref[...] = -ADAM_LR * (m_hat / (jnp.sqrt(v_hat) + ADAM_EPS) + ADAM_WD * w_ref[...])
        nm_ref[...] = nm
        nv_ref[...] = nv

    blk = pl.BlockSpec((None, tr, C), lambda l, i: (l, i, 0))
    shp = jax.ShapeDtypeStruct((L, R, C), F32)
    return pl.pallas_call(
        body, name=name, out_shape=(shp, shp, shp, shp), grid=(L, R // tr),
        in_specs=[pl.BlockSpec((None, P, tr, C), lambda l, i: (l, 0, i, 0)), blk, blk, blk],
        out_specs=(blk, blk, blk, blk),
        compiler_params=_cp(("parallel", "parallel")),
    )(parts, w, m, v)


def _pad_rows(vec, width, row_align):
    n = vec.shape[0]
    rows = -(-n // width)
    rows = -(-rows // row_align) * row_align
    return jnp.pad(vec, (0, rows * width - n)).reshape(rows, width)


def kernel(x, mem, w_in, b_gate, a_sink, b_q_gain, b_k_gain, c_ln_g, c_ln_b, c_ws, c_bs, w_branch, w_mix_out, ln1_g, ln1_b, x_wq, x_wkv, x_wo, ln2_g, ln2_b, f_w_up, f_conv_k, f_conv_b, f_w_down, ln3_g, ln3_b, loss_target, m_w_in, m_b_gate, m_a_sink, m_b_q_gain, m_b_k_gain, m_c_ln_g, m_c_ln_b, m_c_ws, m_c_bs, m_w_branch, m_w_mix_out, m_ln1_g, m_ln1_b, m_x_wq, m_x_wkv, m_x_wo, m_ln2_g, m_ln2_b, m_f_w_up, m_f_conv_k, m_f_conv_b, m_f_w_down, m_ln3_g, m_ln3_b, v_w_in, v_b_gate, v_a_sink, v_b_q_gain, v_b_k_gain, v_c_ln_g, v_c_ln_b, v_c_ws, v_c_bs, v_w_branch, v_w_mix_out, v_ln1_g, v_ln1_b, v_x_wq, v_x_wkv, v_x_wo, v_ln2_g, v_ln2_b, v_f_w_up, v_f_conv_k, v_f_conv_b, v_f_w_down, v_ln3_g, v_ln3_b):
    w = dict(w_in=w_in, b_gate=b_gate, a_sink=a_sink, b_q_gain=b_q_gain, b_k_gain=b_k_gain, c_ln_g=c_ln_g,
             c_ln_b=c_ln_b, c_ws=c_ws, c_bs=c_bs, w_branch=w_branch, w_mix_out=w_mix_out, ln1_g=ln1_g, ln1_b=ln1_b,
             x_wq=x_wq, x_wkv=x_wkv, x_wo=x_wo, ln2_g=ln2_g, ln2_b=ln2_b, f_w_up=f_w_up, f_conv_k=f_conv_k,
             f_conv_b=f_conv_b, f_w_down=f_w_down, ln3_g=ln3_g, ln3_b=ln3_b)
    m = dict(w_in=m_w_in, b_gate=m_b_gate, a_sink=m_a_sink, b_q_gain=m_b_q_gain, b_k_gain=m_b_k_gain,
             c_ln_g=m_c_ln_g, c_ln_b=m_c_ln_b, c_ws=m_c_ws, c_bs=m_c_bs, w_branch=m_w_branch, w_mix_out=m_w_mix_out,
             ln1_g=m_ln1_g, ln1_b=m_ln1_b, x_wq=m_x_wq, x_wkv=m_x_wkv, x_wo=m_x_wo, ln2_g=m_ln2_g, ln2_b=m_ln2_b,
             f_w_up=m_f_w_up, f_conv_k=m_f_conv_k, f_conv_b=m_f_conv_b, f_w_down=m_f_w_down, ln3_g=m_ln3_g,
             ln3_b=m_ln3_b)
    v = dict(w_in=v_w_in, b_gate=v_b_gate, a_sink=v_a_sink, b_q_gain=v_b_q_gain, b_k_gain=v_b_k_gain,
             c_ln_g=v_c_ln_g, c_ln_b=v_c_ln_b, c_ws=v_c_ws, c_bs=v_c_bs, w_branch=v_w_branch, w_mix_out=v_w_mix_out,
             ln1_g=v_ln1_g, ln1_b=v_ln1_b, x_wq=v_x_wq, x_wkv=v_x_wkv, x_wo=v_x_wo, ln2_g=v_ln2_g, ln2_b=v_ln2_b,
             f_w_up=v_f_w_up, f_conv_k=v_f_conv_k, f_conv_b=v_f_conv_b, f_w_down=v_f_w_down, ln3_g=v_ln3_g,
             ln3_b=v_ln3_b)
    me = 4 * lax.axis_index("x") + 2 * lax.axis_index("y") + lax.axis_index("c")

    def held(k, t):
        return jnp.swapaxes(t, 1, 2) if k in TRANSPOSED else t

    shards = dict({k: held(k, w[k]).astype(BF16) for k in BIG}, f_conv_k=w["f_conv_k"])
    loss, grad_x, grads, recvs = _local_step(x[0], mem[0], loss_target[0], {k: w[k] for k in SMALL}, shards)
    loss = lax.psum(loss[0, 0], ("x", "y", "c"))

    out_g, out_d, out_m, out_v = {}, {}, {}, {}
    for k, recv in zip(BIG, recvs):
        shp = held(k, w[k]).shape
        rc = (DEPTH, math.prod(shp[1:-1]), shp[-1])
        parts = recv.reshape((DEPTH, N_DEV) + rc[1:])
        g_, d_, m_, v_ = _adamw(parts, held(k, w[k]).reshape(rc), held(k, m[k]).reshape(rc),
                                held(k, v[k]).reshape(rc), name=f"adamw_{k}")
        out_g[k], out_d[k], out_m[k], out_v[k] = (held(k, t.reshape(shp)) for t in (g_, d_, m_, v_))

    small_all = SMALL + ("f_conv_k",)
    gfull = {k: jnp.stack([grads[l][k] for l in range(DEPTH)]) for k in small_all}

    def pack(d):
        return jnp.concatenate([_pad_rows(d[k].reshape(-1), PACK_W, 8) for k in small_all])

    def unpack(rows, like):
        out, r = {}, 0
        for k in small_all:
            nr = -(-like[k].size // (8 * PACK_W)) * 8
            out[k] = rows[r:r + nr].reshape(-1)[:like[k].size].reshape(like[k].shape)
            r += nr
        return out

    gathered = _gather_call([(pack(gfull)[None], 0)], name="gather_small_grads")[0]
    sg = unpack(_sum_parts(gathered, name="sum_small_grads"), gfull)
    width = w["f_conv_k"].shape[2]
    sg["f_conv_k"] = lax.dynamic_slice_in_dim(sg["f_conv_k"], me * width, width, axis=2)
    g_, d_, m_, v_ = _adamw(pack(sg)[None, None], pack(w)[None], pack(m)[None], pack(v)[None], name="adamw_small")
    ud, um, uv = (unpack(t[0], w) for t in (d_, m_, v_))
    for k in small_all:
        out_g[k], out_d[k], out_m[k], out_v[k] = sg[k], ud[k], um[k], uv[k]

    return (loss, grad_x[None], *[out_g[k] for k in WEIGHTS], *[out_d[k] for k in WEIGHTS],
            *[out_m[k] for k in WEIGHTS], *[out_v[k] for k in WEIGHTS])
```

```python
import functools
import math

import jax
import jax.numpy as jnp
from jax import lax
from jax.experimental import pallas as pl
from jax.experimental.pallas import tpu as pltpu

F32 = jnp.float32
BF16 = jnp.bfloat16

DEPTH = 4
HEAD_DIM = 64
BLOCK = 128
WINDOW = 128
GRID_W = 64
C_WIDTH = 512
C_GROUPS = 4
CHUNK = 128
N_BRANCH = 3
BRANCH_WIDTH = 512
ROPE_THETA = 10000.0
X_HEADS = 4
X_HEAD_DIM = 128
ALPHA = (2 * DEPTH) ** 0.25
LN_EPS = 1e-5
RMS_EPS = 1e-6
ADAM_LR = 0.001
ADAM_B1 = 0.9
ADAM_B2 = 0.999
ADAM_EPS = 1e-08
ADAM_WD = 0.01
ADAM_STEP = 10
N_DEV = 8

COL_A = 0
COL_B = 768
COL_C = 1536
COL_GATE = 2560
QKV_W = 768

LANES = 128
V7X_VMEM_BYTES = 64 * 1024 * 1024
VMEM_LIMIT = V7X_VMEM_BYTES - 8 * 1024 * 1024
NEG_BIG = -1e30
ROW_TILE = 512

_NT = (((1,), (1,)), ((), ()))
_TN = (((0,), (0,)), ((), ()))
_NN = (((1,), (0,)), ((), ()))


def _cp(sem=None):
    return pltpu.CompilerParams(dimension_semantics=sem, vmem_limit_bytes=VMEM_LIMIT)


def _tile(n, target, align=LANES):
    if n <= target:
        return n
    best = None
    for t in range(align, target + 1, align):
        if n % t == 0:
            best = t
    assert best is not None, (n, target)
    return best


def _dot(a, b, dims=_NN):
    return lax.dot_general(a, b, dims, preferred_element_type=F32)


def _gelu(x):
    return 0.5 * x * (1.0 + lax.erf(x * 0.7071067811865476))


def _gelu_and_grad(x):
    cdf = 0.5 * (1.0 + lax.erf(x * 0.7071067811865476))
    return x * cdf, cdf + x * jnp.exp(-0.5 * x * x) * 0.3989422804014327


def _sigmoid(x):
    return 1.0 / (1.0 + jnp.exp(-x))


MESH_ID = pl.DeviceIdType.MESH
_ANY = pl.BlockSpec(memory_space=pl.ANY)
COPIES_PER_ARRAY = N_DEV - 1


def _comm_scratch(n_arrays):
    return [pltpu.SemaphoreType.DMA((COPIES_PER_ARRAY * n_arrays,)),
            pltpu.SemaphoreType.DMA((COPIES_PER_ARRAY * n_arrays,)), pltpu.SemaphoreType.DMA((n_arrays,))]


def _gathered_shape(x):
    return jax.ShapeDtypeStruct((N_DEV,) + x.shape[1:], x.dtype)


def _gather_plan(entries, send_sems, recv_sems, local_sems):
    mx, my, mc = lax.axis_index("x"), lax.axis_index("y"), lax.axis_index("c")
    me, sibling = (mx, my, mc), (mx, my, 1 - mc)
    chips = [(1 - mx, my), (mx, 1 - my), (1 - mx, 1 - my)]

    def copy(a, k, block, to, from_shard=False):
        x_ref, l, out_ref = entries[a]
        dst = out_ref.at[4 * block[0] + 2 * block[1] + block[2]]
        return pltpu.make_async_remote_copy(
            src_ref=x_ref.at[l] if from_shard else dst, dst_ref=dst,
            send_sem=send_sems.at[COPIES_PER_ARRAY * a + k], recv_sem=recv_sems.at[COPIES_PER_ARRAY * a + k],
            device_id=to, device_id_type=MESH_ID)

    def own(a):
        x_ref, l, out_ref = entries[a]
        return pltpu.make_async_copy(x_ref.at[l], out_ref.at[4 * mx + 2 * my + mc], local_sems.at[a])

    def first(a):
        return [copy(a, 0, me, sibling, True)] + [copy(a, 1 + j, me, (*chip, mc), True) for j, chip in enumerate(chips)]

    def passed(a):
        return [copy(a, 4 + j, (*chip, mc), sibling) for j, chip in enumerate(chips)]

    def start():
        for a in range(len(entries)):
            own(a).start()
            for cp in first(a):
                cp.start()

    def finish():
        for a in range(len(entries)):
            fwd = passed(a)
            for j, chip in enumerate(chips):
                copy(a, 1 + j, (*chip, mc), me).wait_recv()
                fwd[j].start()
        for a in range(len(entries)):
            copy(a, 0, sibling, me).wait_recv()
            for j, chip in enumerate(chips):
                copy(a, 4 + j, (*chip, 1 - mc), me).wait_recv()
            for cp in first(a) + passed(a):
                cp.wait_send()
            own(a).wait()

    return start, finish


def _scatter_plan(entries, send_sems, recv_sems, local_sems):
    mx, my, mc = lax.axis_index("x"), lax.axis_index("y"), lax.axis_index("c")
    me = 4 * mx + 2 * my + mc

    def src(a, dev):
        return entries[a][0].at[dev]

    def copies(a):
        _, recv_ref, lr = entries[a]
        out = []
        for k in range(1, N_DEV):
            px = 1 - mx if k & 4 else mx
            py = 1 - my if k & 2 else my
            pc = 1 - mc if k & 1 else mc
            peer = 4 * px + 2 * py + pc
            sems = dict(send_sem=send_sems.at[COPIES_PER_ARRAY * a + k - 1],
                        recv_sem=recv_sems.at[COPIES_PER_ARRAY * a + k - 1],
                        device_id=(px, py, pc), device_id_type=MESH_ID)
            sends = pltpu.make_async_remote_copy(src_ref=src(a, peer), dst_ref=recv_ref.at[lr, me], **sems)
            lands = pltpu.make_async_remote_copy(src_ref=src(a, me), dst_ref=recv_ref.at[lr, peer], **sems)
            out.append((sends, lands))
        return out

    def own(a):
        _, recv_ref, lr = entries[a]
        return pltpu.make_async_copy(src(a, me), recv_ref.at[lr, me], local_sems.at[a])

    def start():
        for a in range(len(entries)):
            own(a).start()
            for sends, _ in copies(a):
                sends.start()

    def finish():
        for a in range(len(entries)):
            for _, lands in copies(a):
                lands.wait_recv()
        for a in range(len(entries)):
            for sends, _ in copies(a):
                sends.wait_send()
            own(a).wait()

    return start, finish


MM_TILE, MM_TK = 1536, 2048
MM_TILE_LN = 512


def _mm(a, b, *, ta=False, tb=False, out_dtype=F32, res=None, res_scale=1.0, into=None, ln_bwd=None, name):
    segs = list(a) if isinstance(a, (list, tuple)) else [a]
    if ta:
        (K, M), seg_k = segs[0].shape, [segs[0].shape[0]]
        assert len(segs) == 1
    else:
        M, seg_k = segs[0].shape[0], [s.shape[1] for s in segs]
        K = sum(seg_k)
    if tb:
        N, Kb = b.shape
    else:
        Kb, N = b.shape
    assert K == Kb, ([s.shape for s in segs], b.shape, ta, tb)
    row_off = into[1] if into is not None else 0
    tm, tn = _tile(math.gcd(M, row_off), MM_TILE if ln_bwd is None else MM_TILE_LN), _tile(N, MM_TILE)
    tk = _tile(K, MM_TK) if len(segs) == 1 else _tile(math.gcd(*seg_k), MM_TILE)
    nk = K // tk
    seg_chunks = [ks // tk for ks in seg_k]
    seg_first = [sum(seg_chunks[:s]) for s in range(len(segs))]
    dims = (((0 if ta else 1,), (1 if tb else 0,)), ((), ()))
    ns = len(segs)
    n_res = ns + 1
    n_ln = n_res + (res is not None)
    into_held = into is not None and not isinstance(into[0], jax.ShapeDtypeStruct)
    n_in = n_ln + (3 if ln_bwd is not None else 0) + into_held
    assert ln_bwd is None or (tn == N and into is None)

    def body(*refs):
        a_refs, b_ref = refs[:ns], refs[ns]
        r_ref = refs[n_res] if res is not None else None
        o_ref = refs[n_in]
        first_row_tile = pl.program_id(0) == 0

        def finish(out):
            if r_ref is not None:
                out = out + res_scale * r_ref[...]
            if ln_bwd is None:
                o_ref[...] = out.astype(o_ref.dtype)
                return
            xh_ref, rs_ref, g_ref = refs[n_ln:n_ln + 3]
            ob_ref, dg_ref, db_ref = refs[n_in + 1:n_in + 4]

            @pl.when(first_row_tile)
            def _():
                dg_ref[...] = jnp.zeros_like(dg_ref)
                db_ref[...] = jnp.zeros_like(db_ref)

            xh = xh_ref[...]
            dxh = out * g_ref[...]
            m1 = jnp.mean(dxh, axis=-1, keepdims=True)
            m2 = jnp.mean(dxh * xh, axis=-1, keepdims=True)
            dz = rs_ref[...] * (dxh - m1 - xh * m2)
            o_ref[...] = dz
            ob_ref[...] = dz.astype(BF16)
            dg_ref[...] += jnp.sum(out * xh, axis=0, keepdims=True)
            db_ref[...] += jnp.sum(out, axis=0, keepdims=True)

        def prod(s):
            return _dot(a_refs[s][...].astype(BF16), b_ref[...].astype(BF16), dims)

        if nk == 1:
            finish(prod(0))
            return
        acc = refs[n_in + (4 if ln_bwd is not None else 1)]
        k = pl.program_id(2)

        @pl.when(k == 0)
        def _():
            acc[...] = jnp.zeros_like(acc)

        for s in range(ns):
            def add(s=s):
                acc[...] += prod(s)
            pl.when((k >= seg_first[s]) & (k < seg_first[s] + seg_chunks[s]))(add)

        @pl.when(k == nk - 1)
        def _():
            finish(acc[...])

    if ta:
        a_specs = [pl.BlockSpec((tk, tm), lambda i, j, k: (k, i))]
    else:
        a_specs = [pl.BlockSpec((tm, tk), functools.partial(
            lambda i, j, k, first, n: (i, jnp.clip(k - first, 0, n - 1)), first=seg_first[s], n=seg_chunks[s]))
            for s in range(ns)]
    b_spec = pl.BlockSpec((tn, tk), lambda i, j, k: (j, k)) if tb else pl.BlockSpec((tk, tn), lambda i, j, k: (k, j))
    in_specs = a_specs + [b_spec]
    args = segs + [b]
    if res is not None:
        in_specs.append(pl.BlockSpec((tm, tn), lambda i, j, k: (i, j)))
        args.append(res)
    out_spec = pl.BlockSpec((tm, tn), lambda i, j, k: (i, j))
    if ln_bwd is not None:
        xh, rs, g = ln_bwd
        in_specs += [out_spec, pl.BlockSpec((tm, 1), lambda i, j, k: (i, 0)), pl.BlockSpec((1, tn), lambda i, j, k: (0, j))]
        args += [xh, rs, g]
        vec = pl.BlockSpec((1, tn), lambda i, j, k: (0, j))
        return pl.pallas_call(
            body, name=name,
            out_shape=(jax.ShapeDtypeStruct((M, N), F32), jax.ShapeDtypeStruct((M, N), BF16),
                       jax.ShapeDtypeStruct((1, N), F32), jax.ShapeDtypeStruct((1, N), F32)),
            grid=(M // tm, N // tn, nk),
            in_specs=in_specs,
            out_specs=(out_spec, out_spec, vec, vec),
            scratch_shapes=[pltpu.VMEM((tm, tn), F32)] if nk > 1 else [],
            compiler_params=_cp(("arbitrary", "arbitrary", "arbitrary")),
        )(*args)
    if into is None:
        out_shape = jax.ShapeDtypeStruct((M, N), out_dtype)
        blk_off, aliases = 0, {}
    else:
        buf = into[0]
        assert buf.shape[1] == N and row_off % tm == 0 and row_off + M <= buf.shape[0], (buf.shape, M, N, row_off)
        out_shape = jax.ShapeDtypeStruct(buf.shape, buf.dtype)
        blk_off, aliases = row_off // tm, {}
        if into_held:
            aliases = {n_in - 1: 0}
            in_specs.append(_ANY)
            args.append(buf)
    return pl.pallas_call(
        body, name=name,
        out_shape=out_shape,
        grid=(M // tm, N // tn, nk),
        in_specs=in_specs,
        out_specs=pl.BlockSpec((tm, tn), lambda i, j, k: (i + blk_off, j)),
        scratch_shapes=[pltpu.VMEM((tm, tn), F32)] if nk > 1 else [],
        input_output_aliases=aliases,
        compiler_params=_cp(("parallel", "parallel", "arbitrary")),
    )(*args)


def _mm_res_ln(a, w, x, g, b, *, name):
    S, K = a.shape
    D = w.shape[1]
    tm = _tile(S, ROW_TILE)

    def body(a_ref, w_ref, x_ref, g_ref, b_ref, y_ref, yb_ref, xh_ref, rs_ref):
        h = _dot(a_ref[...], w_ref[...])
        z = ALPHA * x_ref[...] + h
        mu = jnp.mean(z, axis=-1, keepdims=True)
        zc = z - mu
        var = jnp.mean(zc * zc, axis=-1, keepdims=True)
        r = lax.rsqrt(var + LN_EPS)
        xh = zc * r
        y = xh * g_ref[...] + b_ref[...]
        y_ref[...] = y
        yb_ref[...] = y.astype(BF16)
        xh_ref[...] = xh
        rs_ref[...] = r

    row = lambda i: (i, 0)
    full = lambda i: (0, 0)
    return pl.pallas_call(
        body, name=name,
        out_shape=(jax.ShapeDtypeStruct((S, D), F32), jax.ShapeDtypeStruct((S, D), BF16),
                   jax.ShapeDtypeStruct((S, D), F32), jax.ShapeDtypeStruct((S, 1), F32)),
        grid=(S // tm,),
        in_specs=[pl.BlockSpec((tm, K), row), pl.BlockSpec((K, D), full), pl.BlockSpec((tm, D), row),
                  pl.BlockSpec((1, D), full), pl.BlockSpec((1, D), full)],
        out_specs=(pl.BlockSpec((tm, D), row), pl.BlockSpec((tm, D), row), pl.BlockSpec((tm, D), row),
                   pl.BlockSpec((tm, 1), row)),
        compiler_params=_cp(("parallel",)),
    )(a, w, x, g, b)


def _ln_bwd(dy, xh, rs, g, *, name):
    S, D = dy.shape
    tm = _tile(S, ROW_TILE)

    def body(dy_ref, xh_ref, rs_ref, g_ref, dz_ref, dzb_ref, dg_ref, db_ref):
        @pl.when(pl.program_id(0) == 0)
        def _():
            dg_ref[...] = jnp.zeros_like(dg_ref)
            db_ref[...] = jnp.zeros_like(db_ref)

        dy = dy_ref[...]
        xh = xh_ref[...]
        dxh = dy * g_ref[...]
        m1 = jnp.mean(dxh, axis=-1, keepdims=True)
        m2 = jnp.mean(dxh * xh, axis=-1, keepdims=True)
        dz = rs_ref[...] * (dxh - m1 - xh * m2)
        dz_ref[...] = dz
        dzb_ref[...] = dz.astype(BF16)
        dg_ref[...] += jnp.sum(dy * xh, axis=0, keepdims=True)
        db_ref[...] += jnp.sum(dy, axis=0, keepdims=True)

    row = lambda i: (i, 0)
    full = lambda i: (0, 0)
    return pl.pallas_call(
        body, name=name,
        out_shape=(jax.ShapeDtypeStruct((S, D), F32), jax.ShapeDtypeStruct((S, D), BF16),
                   jax.ShapeDtypeStruct((1, D), F32), jax.ShapeDtypeStruct((1, D), F32)),
        grid=(S // tm,),
        in_specs=[pl.BlockSpec((tm, D), row), pl.BlockSpec((tm, D), row), pl.BlockSpec((tm, 1), row),
                  pl.BlockSpec((1, D), full)],
        out_specs=(pl.BlockSpec((tm, D), row), pl.BlockSpec((tm, D), row), pl.BlockSpec((1, D), full),
                   pl.BlockSpec((1, D), full)),
        compiler_params=_cp(("arbitrary",)),
    )(dy, xh, rs, g)


def _loss_head(y, t, *, name):
    S, D = y.shape
    tm = _tile(S, 512)

    def body(y_ref, t_ref, dy_ref, l_ref):
        @pl.when(pl.program_id(0) == 0)
        def _():
            l_ref[...] = jnp.zeros_like(l_ref)

        e = y_ref[...] - t_ref[...]
        dy_ref[...] = e / D
        l_ref[...] += 0.5 * jnp.sum(jnp.mean(e * e, axis=-1, keepdims=True), axis=0, keepdims=True)

    row = lambda i: (i, 0)
    return pl.pallas_call(
        body, name=name,
        out_shape=(jax.ShapeDtypeStruct((S, D), F32), jax.ShapeDtypeStruct((1, 1), F32)),
        grid=(S // tm,),
        in_specs=[pl.BlockSpec((tm, D), row), pl.BlockSpec((tm, D), row)],
        out_specs=(pl.BlockSpec((tm, D), row), pl.BlockSpec((1, 1), lambda i: (0, 0))),
        compiler_params=_cp(("arbitrary",)),
    )(y, t)


def _rope_tables(S):
    pos = jnp.arange(S, dtype=jnp.int32)
    row = pos // GRID_W
    col = pos % GRID_W

    def cs(p, d):
        half = d // 2
        inv = ROPE_THETA ** (-jnp.arange(half, dtype=F32) * (2.0 / d))
        ang = p.astype(F32)[:, None] * inv[None, :]
        c, s = jnp.cos(ang), jnp.sin(ang)
        return jnp.concatenate([c, c], -1), jnp.concatenate([-s, s], -1)

    ca, sa = cs(pos, HEAD_DIM)
    cr, sr = cs(row, HEAD_DIM // 2)
    cc, sc = cs(col, HEAD_DIM // 2)
    cb, sb = jnp.concatenate([cr, cc], -1), jnp.concatenate([sr, sc], -1)
    two = lambda t: jnp.concatenate([t, t], -1)
    return two(ca), two(sa), two(cb), two(sb)


def _partner(x, lane, width):
    h = width // 2
    return jnp.where(lane % width < h, pltpu.roll(x, LANES - h, 1), pltpu.roll(x, h, 1))


def _rope_fwd(x, c, s, lane, width):
    return x * c + _partner(x, lane, width) * s


def _rope_bwd(dy, c, s, lane, width):
    return dy * c + _partner(dy * s, lane, width)


def _head_sum(x, seg):
    return lax.dot_general(x, seg, _NN, precision=lax.Precision.HIGHEST, preferred_element_type=F32)


def _split_heads(x, lane):
    lo = lane < HEAD_DIM
    r = pltpu.roll(x, HEAD_DIM, 1)
    z = jnp.zeros_like(x)
    return jnp.where(lo, x, z), jnp.where(lo, z, r), jnp.where(lo, r, z), jnp.where(lo, z, x)


def _fold_heads(d0, d1, lane):
    t0 = d0 + pltpu.roll(d0, HEAD_DIM, 1)
    t1 = d1 + pltpu.roll(d1, HEAD_DIM, 1)
    return jnp.where(lane < HEAD_DIM, t0, t1)


def _seg_matrix():
    i = jnp.arange(LANES)
    return (i[:, None] // HEAD_DIM == i[None, :] // HEAD_DIM).astype(F32)


def _prep(proj, tabs, qg2, kg2, seg, *, name):
    S = proj.shape[0]
    ts = _tile(S, ROW_TILE)
    ca, sa, cb, sb = tabs

    def body(pa_ref, pb_ref, ca_ref, sa_ref, cb_ref, sb_ref, qg_ref, kg_ref, seg_ref,
             aq_ref, ak_ref, av_ref, bq_ref, bk_ref, bv_ref):
        lane = lax.broadcasted_iota(jnp.int32, (ts, LANES), 1)
        ca, sa, cb, sb = ca_ref[...], sa_ref[...], cb_ref[...], sb_ref[...]
        seg = seg_ref[...]

        def norm(x, gain):
            r = lax.rsqrt(_head_sum(x * x, seg) * (1.0 / HEAD_DIM) + RMS_EPS)
            return x * r * gain

        def put(ref, x):
            for i, part in enumerate(_split_heads(x, lane)):
                ref[i] = part.astype(BF16)

        for gidx in range(4):
            cols = slice(gidx * LANES, (gidx + 1) * LANES)
            aq_ref[:, cols] = (_rope_fwd(pa_ref[:, cols], ca, sa, lane, HEAD_DIM) * 0.125).astype(BF16)
            bq = norm(pb_ref[:, cols], qg_ref[...])
            bq_ref[:, cols] = (_rope_fwd(bq, cb, sb, lane, HEAD_DIM // 2) * 0.125).astype(BF16)
        put(ak_ref, _rope_fwd(pa_ref[:, 512:640], ca, sa, lane, HEAD_DIM))
        put(av_ref, pa_ref[:, 640:768])
        bk = norm(pb_ref[:, 512:640], kg_ref[...])
        put(bk_ref, _rope_fwd(bk, cb, sb, lane, HEAD_DIM // 2))
        put(bv_ref, pb_ref[:, 640:768])

    row = lambda i: (i, 0)
    full = lambda i: (0, 0)
    tab = pl.BlockSpec((ts, LANES), row)
    kv_shape = jax.ShapeDtypeStruct((4, S, LANES), BF16)
    kv_spec = pl.BlockSpec((4, ts, LANES), lambda i: (0, i, 0))
    q_shape = jax.ShapeDtypeStruct((S, 512), BF16)
    q_spec = pl.BlockSpec((ts, 512), row)
    return pl.pallas_call(
        body, name=name,
        out_shape=(q_shape, kv_shape, kv_shape, q_shape, kv_shape, kv_shape),
        grid=(S // ts,),
        in_specs=[pl.BlockSpec((ts, QKV_W), lambda i: (i, 0)), pl.BlockSpec((ts, QKV_W), lambda i: (i, 1)),
                  tab, tab, tab, tab, pl.BlockSpec((1, LANES), full), pl.BlockSpec((1, LANES), full),
                  pl.BlockSpec((LANES, LANES), full)],
        out_specs=(q_spec, kv_spec, kv_spec, q_spec, kv_spec, kv_spec),
        compiler_params=_cp(("parallel",)),
    )(proj, proj, ca, sa, cb, sb, qg2, kg2, seg)


def _unprep(dqa, dka, dva, dqb, dkb, dvb, proj, tabs, qg2, kg2, seg, *, name):
    S = proj.shape[0]
    ts = _tile(S, ROW_TILE)
    ca, sa, cb, sb = tabs

    def body(dqa_ref, dka_ref, dva_ref, dqb_ref, dkb_ref, dvb_ref, pb_ref, ca_ref, sa_ref, cb_ref, sb_ref,
             qg_ref, kg_ref, seg_ref, dp_ref, dqg_ref, dkg_ref):
        @pl.when(pl.program_id(0) == 0)
        def _():
            dqg_ref[...] = jnp.zeros_like(dqg_ref)
            dkg_ref[...] = jnp.zeros_like(dkg_ref)

        lane = lax.broadcasted_iota(jnp.int32, (ts, LANES), 1)
        ca, sa, cb, sb = ca_ref[...], sa_ref[...], cb_ref[...], sb_ref[...]
        seg = seg_ref[...]

        def norm_bwd(dy, x, gain):
            r = lax.rsqrt(_head_sum(x * x, seg) * (1.0 / HEAD_DIM) + RMS_EPS)
            gdy = gain * dy
            dot = _head_sum(gdy * x, seg) * (1.0 / HEAD_DIM)
            dx = r * gdy - x * (r * r * r) * dot
            return dx, jnp.sum(dy * x * r, axis=0, keepdims=True)

        for gidx in range(4):
            cols = slice(gidx * LANES, (gidx + 1) * LANES)
            dp_ref[:, cols] = _rope_bwd(dqa_ref[:, cols] * 0.125, ca, sa, lane, HEAD_DIM).astype(BF16)
            dbq = _rope_bwd(dqb_ref[:, cols] * 0.125, cb, sb, lane, HEAD_DIM // 2)
            dx, dg = norm_bwd(dbq, pb_ref[:, cols], qg_ref[...])
            dp_ref[:, COL_B + gidx * LANES:COL_B + (gidx + 1) * LANES] = dx.astype(BF16)
            dqg_ref[...] += dg
        dak = _fold_heads(dka_ref[0] + dka_ref[1], dka_ref[2] + dka_ref[3], lane)
        dp_ref[:, 512:640] = _rope_bwd(dak, ca, sa, lane, HEAD_DIM).astype(BF16)
        dp_ref[:, 640:768] = _fold_heads(dva_ref[0] + dva_ref[1], dva_ref[2] + dva_ref[3], lane).astype(BF16)
        dbk = _fold_heads(dkb_ref[0] + dkb_ref[1], dkb_ref[2] + dkb_ref[3], lane)
        dbk = _rope_bwd(dbk, cb, sb, lane, HEAD_DIM // 2)
        dx, dg = norm_bwd(dbk, pb_ref[:, 512:640], kg_ref[...])
        dp_ref[:, COL_B + 512:COL_B + 640] = dx.astype(BF16)
        dkg_ref[...] += dg
        dp_ref[:, COL_B + 640:COL_B + 768] = _fold_heads(dvb_ref[0] + dvb_ref[1], dvb_ref[2] + dvb_ref[3],
                                                         lane).astype(BF16)

    row = lambda i: (i, 0)
    full = lambda i: (0, 0)
    tab = pl.BlockSpec((ts, LANES), row)
    q_spec = pl.BlockSpec((ts, 512), row)
    kv_spec = pl.BlockSpec((4, ts, LANES), lambda i: (0, i, 0))
    return pl.pallas_call(
        body, name=name,
        out_shape=(jax.ShapeDtypeStruct((S, 2 * QKV_W), BF16), jax.ShapeDtypeStruct((1, LANES), F32),
                   jax.ShapeDtypeStruct((1, LANES), F32)),
        grid=(S // ts,),
        in_specs=[q_spec, kv_spec, kv_spec, q_spec, kv_spec, kv_spec,
                  pl.BlockSpec((ts, QKV_W), lambda i: (i, 1)), tab, tab, tab, tab,
                  pl.BlockSpec((1, LANES), full), pl.BlockSpec((1, LANES), full), pl.BlockSpec((LANES, LANES), full)],
        out_specs=(pl.BlockSpec((ts, 2 * QKV_W), row), pl.BlockSpec((1, LANES), full),
                   pl.BlockSpec((1, LANES), full)),
        compiler_params=_cp(("arbitrary",)),
    )(dqa, dka, dva, dqb, dkb, dvb, proj, ca, sa, cb, sb, qg2, kg2, seg)


def _attn_dense_fwd(q, k4, v4, *, gather=(), name):
    S = q.shape[0]
    tq = _tile(S, 256)
    xs = [x for x, _ in gather]
    na = len(xs)

    def body(q_ref, k_ref, v_ref, *rest):
        o_ref, lse_ref = rest[na], rest[na + 1]
        if na:
            x_refs, out_refs, sems = rest[:na], rest[na + 2:2 * na + 2], rest[2 * na + 2:]
            start, finish = _gather_plan([(x_refs[a], gather[a][1], out_refs[a]) for a in range(na)], *sems)
            pl.when((pl.program_id(0) == 0) & (pl.program_id(1) == 0))(start)
        for pr in range(2):
            qp = q_ref[:, pr * LANES:(pr + 1) * LANES]
            acc = None
            for half in range(2):
                s = _dot(qp, k_ref[half], _NT)
                m = jnp.max(s, axis=-1, keepdims=True)
                e = jnp.exp(s - m)
                l = jnp.sum(e, axis=-1, keepdims=True)
                pv = _dot(e.astype(BF16), v_ref[half]) * (1.0 / l)
                acc = pv if acc is None else acc + pv
                lse_ref[pr * 2 + half] = m + jnp.log(l)
            o_ref[:, pr * LANES:(pr + 1) * LANES] = acc.astype(BF16)
        if na:
            pl.when((pl.program_id(0) == 1) & (pl.program_id(1) == S // tq - 1))(finish)

    kv_spec = pl.BlockSpec((2, S, LANES), lambda kv, i: (kv, 0, 0))
    res = pl.pallas_call(
        body, name=name,
        out_shape=(jax.ShapeDtypeStruct((S, 512), BF16), jax.ShapeDtypeStruct((8, S, 1), F32),
                   *[_gathered_shape(x) for x in xs]),
        grid=(2, S // tq),
        in_specs=[pl.BlockSpec((tq, 256), lambda kv, i: (i, kv)), kv_spec, kv_spec] + [_ANY] * na,
        out_specs=(pl.BlockSpec((tq, 256), lambda kv, i: (i, kv)),
                   pl.BlockSpec((4, tq, 1), lambda kv, i: (kv, i, 0)), *([_ANY] * na)),
        scratch_shapes=_comm_scratch(na) if na else [],
        compiler_params=_cp(("arbitrary", "arbitrary") if na else ("parallel", "parallel")),
    )(q, k4, v4, *xs)
    return res[0], res[1], list(res[2:])


def _attn_dense_bwd(q, k4, v4, lse, do, *, scatter=(), name):
    S = q.shape[0]
    tq = _tile(S, 256)
    na = len(scatter)
    comm_in, comm_out, held = _scatter_io(scatter)
    n_in = len(comm_in)

    def body(q_ref, k_ref, v_ref, lse_ref, do_ref, *rest):
        dq_ref, dk_ref, dv_ref = rest[n_in:n_in + 3]
        if na:
            s_refs, r_refs, sems = rest[:na], rest[n_in + 3:n_in + 3 + na], rest[n_in + 3 + na:]
            start, finish = _scatter_plan([(s_refs[a], r_refs[a], scatter[a][2]) for a in range(na)], *sems)
            pl.when((pl.program_id(0) == 0) & (pl.program_id(1) == 0))(start)

        @pl.when(pl.program_id(1) == 0)
        def _():
            dk_ref[...] = jnp.zeros_like(dk_ref)
            dv_ref[...] = jnp.zeros_like(dv_ref)

        lane = lax.broadcasted_iota(jnp.int32, (tq, LANES), 1)
        for pr in range(2):
            qp = q_ref[:, pr * LANES:(pr + 1) * LANES]
            dop = do_ref[:, pr * LANES:(pr + 1) * LANES].astype(BF16)
            dq = None
            for half in range(2):
                mine = (lane < HEAD_DIM) if half == 0 else (lane >= HEAD_DIM)
                s = _dot(qp, k_ref[half], _NT)
                p = jnp.exp(s - lse_ref[pr * 2 + half])
                dp = _dot(dop, v_ref[half], _NT)
                delta = jnp.sum(p * dp, axis=-1, keepdims=True)
                ds = (p * (dp - delta)).astype(BF16)
                pb = p.astype(BF16)
                d = _dot(ds, k_ref[half])
                dq = d if dq is None else dq + d
                dk_ref[half] += _dot(ds, jnp.where(mine, qp, jnp.zeros_like(qp)), _TN)
                dv_ref[half] += _dot(pb, jnp.where(mine, dop, jnp.zeros_like(dop)), _TN)
            dq_ref[:, pr * LANES:(pr + 1) * LANES] = dq
        if na:
            pl.when((pl.program_id(0) == 1) & (pl.program_id(1) == S // tq - 1))(finish)

    kv_spec = pl.BlockSpec((2, S, LANES), lambda kv, i: (kv, 0, 0))
    q_spec = pl.BlockSpec((tq, 256), lambda kv, i: (i, kv))
    res = pl.pallas_call(
        body, name=name,
        out_shape=(jax.ShapeDtypeStruct((S, 512), F32), jax.ShapeDtypeStruct((4, S, LANES), F32),
                   jax.ShapeDtypeStruct((4, S, LANES), F32), *comm_out),
        grid=(2, S // tq),
        in_specs=[q_spec, kv_spec, kv_spec, pl.BlockSpec((4, tq, 1), lambda kv, i: (kv, i, 0)), q_spec]
                 + [_ANY] * n_in,
        out_specs=(q_spec, kv_spec, kv_spec, *([_ANY] * na)),
        scratch_shapes=_comm_scratch(na) if na else [],
        input_output_aliases={5 + na + i: 3 + a for i, a in enumerate(held)},
        compiler_params=_cp(("arbitrary", "arbitrary") if na else ("parallel", "arbitrary")),
    )(q, k4, v4, lse, do, *comm_in)
    return res[0], res[1], res[2], list(res[3:])


WIN_Q = 2 * BLOCK
WIN_KEYS = WIN_Q + 2 * WINDOW


def _win_start(n, S):
    return pl.multiple_of(jnp.clip(n * WIN_Q - WINDOW, 0, S - WIN_KEYS), BLOCK)


def _win_valid(n, start):
    qpos = n * WIN_Q + lax.broadcasted_iota(jnp.int32, (WIN_Q, WIN_KEYS), 0)
    kpos = start + lax.broadcasted_iota(jnp.int32, (WIN_Q, WIN_KEYS), 1)
    return jnp.abs(qpos - kpos) <= WINDOW


def _attn_win_fwd(q, k4, v4, sink, *, name):
    S = q.shape[0]
    assert S >= WIN_KEYS

    def body(sink_ref, q_ref, k_ref, v_ref, o_ref, lse_ref):
        n = pl.program_id(0)
        start = _win_start(n, S)
        valid = _win_valid(n, start)
        for kv in range(2):
            for pr in range(2):
                cols = slice((kv * 2 + pr) * LANES, (kv * 2 + pr + 1) * LANES)
                qp = q_ref[:, cols]
                acc = None
                for half in range(2):
                    h = kv * 4 + pr * 2 + half
                    kk = k_ref[kv * 2 + half, pl.ds(start, WIN_KEYS), :]
                    vv = v_ref[kv * 2 + half, pl.ds(start, WIN_KEYS), :]
                    s = jnp.where(valid, _dot(qp, kk, _NT), NEG_BIG)
                    snk = sink_ref[h]
                    m = jnp.maximum(jnp.max(s, axis=-1, keepdims=True), snk)
                    e = jnp.exp(s - m)
                    l = jnp.sum(e, axis=-1, keepdims=True) + jnp.exp(snk - m)
                    pv = _dot(e.astype(BF16), vv) * (1.0 / l)
                    acc = pv if acc is None else acc + pv
                    lse_ref[h] = m + jnp.log(l)
                o_ref[:, cols] = acc.astype(BF16)

    kv_spec = pl.BlockSpec((4, S, LANES), lambda n: (0, 0, 0))
    return pl.pallas_call(
        body, name=name,
        out_shape=(jax.ShapeDtypeStruct((S, 512), BF16), jax.ShapeDtypeStruct((8, S, 1), F32)),
        grid=(S // WIN_Q,),
        in_specs=[pl.BlockSpec(memory_space=pltpu.SMEM), pl.BlockSpec((WIN_Q, 512), lambda n: (n, 0)),
                  kv_spec, kv_spec],
        out_specs=(pl.BlockSpec((WIN_Q, 512), lambda n: (n, 0)), pl.BlockSpec((8, WIN_Q, 1), lambda n: (0, n, 0))),
        compiler_params=_cp(("parallel",)),
    )(sink, q, k4, v4)


def _attn_win_bwd(q, k4, v4, sink, lse, do, *, name):
    S = q.shape[0]

    def body(sink_ref, q_ref, k_ref, v_ref, lse_ref, do_ref, dq_ref, dk_ref, dv_ref, dsink_ref):
        n = pl.program_id(0)

        @pl.when(n == 0)
        def _():
            dk_ref[...] = jnp.zeros_like(dk_ref)
            dv_ref[...] = jnp.zeros_like(dv_ref)
            dsink_ref[...] = jnp.zeros_like(dsink_ref)

        start = _win_start(n, S)
        valid = _win_valid(n, start)
        lane = lax.broadcasted_iota(jnp.int32, (WIN_Q, LANES), 1)
        for kv in range(2):
            for pr in range(2):
                cols = slice((kv * 2 + pr) * LANES, (kv * 2 + pr + 1) * LANES)
                qp = q_ref[:, cols]
                dop = do_ref[:, cols].astype(BF16)
                dq = None
                for half in range(2):
                    h = kv * 4 + pr * 2 + half
                    slot = kv * 2 + half
                    mine = (lane < HEAD_DIM) if half == 0 else (lane >= HEAD_DIM)
                    win = pl.ds(start, WIN_KEYS)
                    kk = k_ref[slot, win, :]
                    vv = v_ref[slot, win, :]
                    lse_h = lse_ref[h]
                    s = jnp.where(valid, _dot(qp, kk, _NT), NEG_BIG)
                    p = jnp.exp(s - lse_h)
                    dp = _dot(dop, vv, _NT)
                    delta = jnp.sum(p * dp, axis=-1, keepdims=True)
                    ds = (p * (dp - delta)).astype(BF16)
                    pb = p.astype(BF16)
                    d = _dot(ds, kk)
                    dq = d if dq is None else dq + d
                    dk_ref[slot, win, :] += _dot(ds, jnp.where(mine, qp, jnp.zeros_like(qp)), _TN)
                    dv_ref[slot, win, :] += _dot(pb, jnp.where(mine, dop, jnp.zeros_like(dop)), _TN)
                    p_sink = jnp.exp(sink_ref[h] - lse_h)
                    dsink_ref[h:h + 1, :] += jnp.broadcast_to(-jnp.sum(p_sink * delta, axis=0, keepdims=True),
                                                              (1, LANES))
                dq_ref[:, cols] = dq

    kv_spec = pl.BlockSpec((4, S, LANES), lambda n: (0, 0, 0))
    q_spec = pl.BlockSpec((WIN_Q, 512), lambda n: (n, 0))
    return pl.pallas_call(
        body, name=name,
        out_shape=(jax.ShapeDtypeStruct((S, 512), F32), jax.ShapeDtypeStruct((4, S, LANES), F32),
                   jax.ShapeDtypeStruct((4, S, LANES), F32), jax.ShapeDtypeStruct((8, LANES), F32)),
        grid=(S // WIN_Q,),
        in_specs=[pl.BlockSpec(memory_space=pltpu.SMEM), q_spec, kv_spec, kv_spec,
                  pl.BlockSpec((8, WIN_Q, 1), lambda n: (0, n, 0)), q_spec],
        out_specs=(q_spec, kv_spec, kv_spec, pl.BlockSpec((8, LANES), lambda n: (0, 0))),
        compiler_params=_cp(("arbitrary",)),
    )(sink, q, k4, v4, lse, do)


def _c_ln(v, g, b):
    mu = jnp.mean(v, axis=-1, keepdims=True)
    vc = v - mu
    r = lax.rsqrt(jnp.mean(vc * vc, axis=-1, keepdims=True) + LN_EPS)
    vh = vc * r
    return vh, r, vh * g + b


def _gmlp_fwd(proj, ws, bs3, lg, lb, *, name):
    S = proj.shape[0]

    def body(u_ref, v_ref, ws_ref, bs_ref, lg_ref, lb_ref, o_ref):
        u = _gelu(u_ref[...])
        _, _, vn = _c_ln(_gelu(v_ref[...]), lg_ref[...], lb_ref[...])
        vn = vn.astype(BF16)
        for gi in range(C_GROUPS):
            cols = slice(gi * LANES, (gi + 1) * LANES)
            mixed = _dot(ws_ref[gi], vn[:, cols]) + bs_ref[gi]
            o_ref[:, cols] = (u[:, cols] * mixed).astype(BF16)

    full2 = lambda n: (0, 0)
    full3 = lambda n: (0, 0, 0)
    return pl.pallas_call(
        body, name=name,
        out_shape=jax.ShapeDtypeStruct((S, C_WIDTH), BF16),
        grid=(S // CHUNK,),
        in_specs=[pl.BlockSpec((CHUNK, C_WIDTH), lambda n: (n, COL_C // C_WIDTH)),
                  pl.BlockSpec((CHUNK, C_WIDTH), lambda n: (n, COL_C // C_WIDTH + 1)),
                  pl.BlockSpec((C_GROUPS, CHUNK, CHUNK), full3), pl.BlockSpec((C_GROUPS, CHUNK, 1), full3),
                  pl.BlockSpec((1, C_WIDTH), full2), pl.BlockSpec((1, C_WIDTH), full2)],
        out_specs=pl.BlockSpec((CHUNK, C_WIDTH), lambda n: (n, 0)),
        compiler_params=_cp(("parallel",)),
    )(proj, proj, ws, bs3, lg, lb)


def _gmlp_bwd(proj, dout, ws, bs3, lg, lb, *, name):
    S = proj.shape[0]

    def body(u_ref, v_ref, d_ref, ws_ref, bs_ref, lg_ref, lb_ref, dz_ref, dws_ref, dbs_ref, dlg_ref, dlb_ref):
        @pl.when(pl.program_id(0) == 0)
        def _():
            dws_ref[...] = jnp.zeros_like(dws_ref)
            dbs_ref[...] = jnp.zeros_like(dbs_ref)
            dlg_ref[...] = jnp.zeros_like(dlg_ref)
            dlb_ref[...] = jnp.zeros_like(dlb_ref)

        u_pre, v_pre, d = u_ref[...], v_ref[...], d_ref[...]
        u, u_grad = _gelu_and_grad(u_pre)
        v, v_grad = _gelu_and_grad(v_pre)
        vh, r, vn = _c_ln(v, lg_ref[...], lb_ref[...])
        vnb = vn.astype(BF16)
        du_parts, dvn_parts = [], []
        for gi in range(C_GROUPS):
            cols = slice(gi * LANES, (gi + 1) * LANES)
            mixed = _dot(ws_ref[gi], vnb[:, cols]) + bs_ref[gi]
            du_parts.append(d[:, cols] * mixed)
            dm = d[:, cols] * u[:, cols]
            dbs_ref[gi] += jnp.sum(dm, axis=-1, keepdims=True)
            dmb = dm.astype(BF16)
            dws_ref[gi] += _dot(dmb, vnb[:, cols], _NT)
            dvn_parts.append(_dot(ws_ref[gi], dmb, _TN))
        du = jnp.concatenate(du_parts, axis=-1)
        dvn = jnp.concatenate(dvn_parts, axis=-1)
        dlg_ref[...] += jnp.sum(dvn * vh, axis=0, keepdims=True)
        dlb_ref[...] += jnp.sum(dvn, axis=0, keepdims=True)
        dvh = dvn * lg_ref[...]
        m1 = jnp.mean(dvh, axis=-1, keepdims=True)
        m2 = jnp.mean(dvh * vh, axis=-1, keepdims=True)
        dv = r * (dvh - m1 - vh * m2)
        dz_ref[:, :C_WIDTH] = (du * u_grad).astype(BF16)
        dz_ref[:, C_WIDTH:] = (dv * v_grad).astype(BF16)

    full2 = lambda n: (0, 0)
    full3 = lambda n: (0, 0, 0)
    return pl.pallas_call(
        body, name=name,
        out_shape=(jax.ShapeDtypeStruct((S, 2 * C_WIDTH), BF16), jax.ShapeDtypeStruct((C_GROUPS, CHUNK, CHUNK), F32),
                   jax.ShapeDtypeStruct((C_GROUPS, CHUNK, 1), F32), jax.ShapeDtypeStruct((1, C_WIDTH), F32),
                   jax.ShapeDtypeStruct((1, C_WIDTH), F32)),
        grid=(S // CHUNK,),
        in_specs=[pl.BlockSpec((CHUNK, C_WIDTH), lambda n: (n, COL_C // C_WIDTH)),
                  pl.BlockSpec((CHUNK, C_WIDTH), lambda n: (n, COL_C // C_WIDTH + 1)),
                  pl.BlockSpec((CHUNK, C_WIDTH), lambda n: (n, 0)),
                  pl.BlockSpec((C_GROUPS, CHUNK, CHUNK), full3), pl.BlockSpec((C_GROUPS, CHUNK, 1), full3),
                  pl.BlockSpec((1, C_WIDTH), full2), pl.BlockSpec((1, C_WIDTH), full2)],
        out_specs=(pl.BlockSpec((CHUNK, 2 * C_WIDTH), lambda n: (n, 0)), pl.BlockSpec((C_GROUPS, CHUNK, CHUNK), full3),
                   pl.BlockSpec((C_GROUPS, CHUNK, 1), full3), pl.BlockSpec((1, C_WIDTH), full2),
                   pl.BlockSpec((1, C_WIDTH), full2)),
        compiler_params=_cp(("arbitrary",)),
    )(proj, proj, dout, ws, bs3, lg, lb)


GATE_BLK = 512


def _gate_specs(tm, D):
    nh = D // GATE_BLK
    first = COL_GATE // GATE_BLK
    return [pl.BlockSpec((tm, GATE_BLK), functools.partial(lambda i, c: (i, c), c=first + b))
            for b in range(N_BRANCH * nh)]


def _merge_fwd(oa, ob, oc, wb, proj, bg, *, name):
    S = oa.shape[0]
    D = wb.shape[2]
    assert D % GATE_BLK == 0
    nh = D // GATE_BLK
    tm = _tile(S, ROW_TILE)

    def body(oa_ref, ob_ref, oc_ref, wb_ref, *rest):
        gate_refs, bg_ref, o_ref = rest[:N_BRANCH * nh], rest[N_BRANCH * nh], rest[N_BRANCH * nh + 1]
        brs = (oa_ref[...], ob_ref[...], oc_ref[...])
        for j in range(nh):
            cols = slice(j * GATE_BLK, (j + 1) * GATE_BLK)
            acc = None
            for n in range(N_BRANCH):
                b = n * nh + j
                t = _dot(brs[n], wb_ref[n, :, cols])
                g = _sigmoid(gate_refs[b][...] + bg_ref[:, b * GATE_BLK:(b + 1) * GATE_BLK])
                acc = t * g if acc is None else acc + t * g
            o_ref[:, cols] = acc.astype(BF16)

    row = lambda i: (i, 0)
    br = pl.BlockSpec((tm, BRANCH_WIDTH), row)
    return pl.pallas_call(
        body, name=name,
        out_shape=jax.ShapeDtypeStruct((S, D), BF16),
        grid=(S // tm,),
        in_specs=[br, br, br, pl.BlockSpec((N_BRANCH, BRANCH_WIDTH, D), lambda i: (0, 0, 0))]
                 + _gate_specs(tm, D) + [pl.BlockSpec((1, N_BRANCH * D), lambda i: (0, 0))],
        out_specs=pl.BlockSpec((tm, D), row),
        compiler_params=_cp(("parallel",)),
    )(oa, ob, oc, wb, *([proj] * (N_BRANCH * nh)), bg)


def _merge_bwd(oa, ob, oc, wb, proj, bg, dmerged, *, name):
    S = oa.shape[0]
    D = wb.shape[2]
    nh = D // GATE_BLK
    tm = _tile(S, ROW_TILE)

    def body(oa_ref, ob_ref, oc_ref, wb_ref, *rest):
        gate_refs = rest[:N_BRANCH * nh]
        bg_ref, dm_ref, dgl_ref, dbg_ref = rest[N_BRANCH * nh:N_BRANCH * nh + 4]
        dt_refs = rest[N_BRANCH * nh + 4:N_BRANCH * nh + 4 + N_BRANCH]
        dbr_refs = rest[N_BRANCH * nh + 4 + N_BRANCH:]

        @pl.when(pl.program_id(0) == 0)
        def _():
            dbg_ref[...] = jnp.zeros_like(dbg_ref)

        brs = (oa_ref[...], ob_ref[...], oc_ref[...])
        for n in range(N_BRANCH):
            dbr = None
            for j in range(nh):
                cols = slice(j * GATE_BLK, (j + 1) * GATE_BLK)
                b = n * nh + j
                gcols = slice(b * GATE_BLK, (b + 1) * GATE_BLK)
                w = wb_ref[n, :, cols]
                t = _dot(brs[n], w)
                g = _sigmoid(gate_refs[b][...] + bg_ref[:, gcols])
                dm = dm_ref[:, cols]
                dt = (dm * g).astype(BF16)
                dgl = dm * t * g * (1.0 - g)
                dt_refs[n][:, cols] = dt
                dgl_ref[:, gcols] = dgl.astype(BF16)
                dbg_ref[:, gcols] += jnp.sum(dgl, axis=0, keepdims=True)
                d = _dot(dt, w, _NT)
                dbr = d if dbr is None else dbr + d
            dbr_refs[n][...] = dbr

    row = lambda i: (i, 0)
    br = pl.BlockSpec((tm, BRANCH_WIDTH), row)
    res = pl.pallas_call(
        body, name=name,
        out_shape=(jax.ShapeDtypeStruct((S, N_BRANCH * D), BF16), jax.ShapeDtypeStruct((1, N_BRANCH * D), F32),
                   *([jax.ShapeDtypeStruct((S, D), BF16)] * N_BRANCH),
                   *([jax.ShapeDtypeStruct((S, BRANCH_WIDTH), F32)] * N_BRANCH)),
        grid=(S // tm,),
        in_specs=[br, br, br, pl.BlockSpec((N_BRANCH, BRANCH_WIDTH, D), lambda i: (0, 0, 0))]
                 + _gate_specs(tm, D)
                 + [pl.BlockSpec((1, N_BRANCH * D), lambda i: (0, 0)), pl.BlockSpec((tm, D), row)],
        out_specs=(pl.BlockSpec((tm, N_BRANCH * D), row), pl.BlockSpec((1, N_BRANCH * D), lambda i: (0, 0)),
                   *([pl.BlockSpec((tm, D), row)] * N_BRANCH), *([br] * N_BRANCH)),
        compiler_params=_cp(("arbitrary",)),
    )(oa, ob, oc, wb, *([proj] * (N_BRANCH * nh)), bg, dmerged)
    return res[0], res[1], list(res[2:2 + N_BRANCH]), list(res[2 + N_BRANCH:])


X_SCALE = 1.0 / math.sqrt(X_HEAD_DIM)
X_W = X_HEADS * X_HEAD_DIM


def _xattn_fwd(q, kv, *, name):
    S = q.shape[0]
    M = kv.shape[0]
    tq = _tile(S, 512)

    def body(q_ref, kv_ref, o_ref, lse_ref):
        for h in range(X_HEADS):
            cols = slice(h * LANES, (h + 1) * LANES)
            s = _dot(q_ref[:, cols], kv_ref[:, cols], _NT) * X_SCALE
            m = jnp.max(s, axis=-1, keepdims=True)
            e = jnp.exp(s - m)
            l = jnp.sum(e, axis=-1, keepdims=True)
            p = (e * (1.0 / l)).astype(BF16)
            o_ref[:, cols] = _dot(p, kv_ref[:, X_W + h * LANES:X_W + (h + 1) * LANES]).astype(BF16)
            lse_ref[h] = m + jnp.log(l)

    return pl.pallas_call(
        body, name=name,
        out_shape=(jax.ShapeDtypeStruct((S, X_W), BF16), jax.ShapeDtypeStruct((X_HEADS, S, 1), F32)),
        grid=(S // tq,),
        in_specs=[pl.BlockSpec((tq, X_W), lambda i: (i, 0)), pl.BlockSpec((M, 2 * X_W), lambda i: (0, 0))],
        out_specs=(pl.BlockSpec((tq, X_W), lambda i: (i, 0)), pl.BlockSpec((X_HEADS, tq, 1), lambda i: (0, i, 0))),
        compiler_params=_cp(("parallel",)),
    )(q, kv)


def _xattn_bwd(q, kv, lse, do, *, name):
    S = q.shape[0]
    M = kv.shape[0]
    tq = _tile(S, 512)

    def body(q_ref, kv_ref, lse_ref, do_ref, dq_ref, dkv_ref):
        @pl.when(pl.program_id(0) == 0)
        def _():
            dkv_ref[...] = jnp.zeros_like(dkv_ref)

        for h in range(X_HEADS):
            cols = slice(h * LANES, (h + 1) * LANES)
            vcols = slice(X_W + h * LANES, X_W + (h + 1) * LANES)
            qh, kh, vh = q_ref[:, cols], kv_ref[:, cols], kv_ref[:, vcols]
            doh = do_ref[:, cols].astype(BF16)
            p = jnp.exp(_dot(qh, kh, _NT) * X_SCALE - lse_ref[h])
            dp = _dot(doh, vh, _NT)
            delta = jnp.sum(p * dp, axis=-1, keepdims=True)
            ds = (p * (dp - delta) * X_SCALE).astype(BF16)
            dq_ref[:, cols] = _dot(ds, kh).astype(BF16)
            dkv_ref[:, cols] += _dot(ds, qh, _TN)
            dkv_ref[:, vcols] += _dot(p.astype(BF16), doh, _TN)

    q_spec = pl.BlockSpec((tq, X_W), lambda i: (i, 0))
    return pl.pallas_call(
        body, name=name,
        out_shape=(jax.ShapeDtypeStruct((S, X_W), BF16), jax.ShapeDtypeStruct((M, 2 * X_W), F32)),
        grid=(S // tq,),
        in_specs=[q_spec, pl.BlockSpec((M, 2 * X_W), lambda i: (0, 0)),
                  pl.BlockSpec((X_HEADS, tq, 1), lambda i: (0, i, 0)), q_spec],
        out_specs=(q_spec, pl.BlockSpec((M, 2 * X_W), lambda i: (0, 0))),
        compiler_params=_cp(("arbitrary",)),
    )(q, kv, lse, do)


def _shift_down(h, row):
    return jnp.where(row == 0, 0.0, pltpu.roll(h, 1, 0))


def _shift_up(h, row, S):
    return jnp.where(row == S - 1, 0.0, pltpu.roll(h, S - 1, 0))


def _conv3(h, ck, cb, row, S):
    return _shift_down(h, row) * ck[0:1] + h * ck[1:2] + _shift_up(h, row, S) * ck[2:3] + cb


def _conv_act_fwd(h, ck, cb, *, name):
    S, F2 = h.shape
    F = F2 // 2
    nt = F // LANES

    def body(ha_ref, hb_ref, cka_ref, ckb_ref, cba_ref, cbb_ref, o_ref):
        row = lax.broadcasted_iota(jnp.int32, (S, LANES), 0)
        a = _conv3(ha_ref[...], cka_ref[...], cba_ref[...], row, S)
        b = _conv3(hb_ref[...], ckb_ref[...], cbb_ref[...], row, S)
        o_ref[...] = (_gelu(a) * b).astype(BF16)

    ca = lambda j: (0, j)
    cbi = lambda j: (0, j + nt)
    return pl.pallas_call(
        body, name=name,
        out_shape=jax.ShapeDtypeStruct((S, F), BF16),
        grid=(nt,),
        in_specs=[pl.BlockSpec((S, LANES), ca), pl.BlockSpec((S, LANES), cbi), pl.BlockSpec((3, LANES), ca),
                  pl.BlockSpec((3, LANES), cbi), pl.BlockSpec((1, LANES), ca), pl.BlockSpec((1, LANES), cbi)],
        out_specs=pl.BlockSpec((S, LANES), ca),
        compiler_params=_cp(("parallel",)),
    )(h, h, ck, ck, cb, cb)


def _conv_act_bwd(h, ck, cb, dact, *, name):
    S, F2 = h.shape
    F = F2 // 2
    nt = F // LANES

    def body(ha_ref, hb_ref, cka_ref, ckb_ref, cba_ref, cbb_ref, d_ref,
             dha_ref, dhb_ref, dcka_ref, dckb_ref, dcba_ref, dcbb_ref):
        row = lax.broadcasted_iota(jnp.int32, (S, LANES), 0)
        ha, hb = ha_ref[...], hb_ref[...]
        cka, ckb = cka_ref[...], ckb_ref[...]
        a = _conv3(ha, cka, cba_ref[...], row, S)
        b = _conv3(hb, ckb, cbb_ref[...], row, S)
        d = d_ref[...]
        ga, ga_grad = _gelu_and_grad(a)
        da = d * b * ga_grad
        db = d * ga
        for dd, hh, ck_, dh_ref, dck_ref, dcb_ref in ((da, ha, cka, dha_ref, dcka_ref, dcba_ref),
                                                      (db, hb, ckb, dhb_ref, dckb_ref, dcbb_ref)):
            dcb_ref[...] = jnp.sum(dd, axis=0, keepdims=True)
            dck_ref[0:1, :] = jnp.sum(dd * _shift_down(hh, row), axis=0, keepdims=True)
            dck_ref[1:2, :] = jnp.sum(dd * hh, axis=0, keepdims=True)
            dck_ref[2:3, :] = jnp.sum(dd * _shift_up(hh, row, S), axis=0, keepdims=True)
            dh = _shift_up(dd, row, S) * ck_[0:1] + dd * ck_[1:2] + _shift_down(dd, row) * ck_[2:3]
            dh_ref[...] = dh.astype(BF16)

    ca = lambda j: (0, j)
    cbi = lambda j: (0, j + nt)
    col = pl.BlockSpec((S, LANES), ca)
    return pl.pallas_call(
        body, name=name,
        out_shape=(jax.ShapeDtypeStruct((S, F), BF16), jax.ShapeDtypeStruct((S, F), BF16),
                   jax.ShapeDtypeStruct((3, F), F32), jax.ShapeDtypeStruct((3, F), F32),
                   jax.ShapeDtypeStruct((1, F), F32), jax.ShapeDtypeStruct((1, F), F32)),
        grid=(nt,),
        in_specs=[col, pl.BlockSpec((S, LANES), cbi), pl.BlockSpec((3, LANES), ca), pl.BlockSpec((3, LANES), cbi),
                  pl.BlockSpec((1, LANES), ca), pl.BlockSpec((1, LANES), cbi), col],
        out_specs=(col, col, pl.BlockSpec((3, LANES), ca), pl.BlockSpec((3, LANES), ca),
                   pl.BlockSpec((1, LANES), ca), pl.BlockSpec((1, LANES), ca)),
        compiler_params=_cp(("parallel",)),
    )(h, h, ck, ck, cb, cb, dact)


def _layer_fwd(x, xb, memb, w, shards, tabs, seg, l):
    n = lambda s: f"L{l}_{s}"
    w = dict(w)
    qg2 = jnp.tile(w["b_q_gain"], 2)[None, :]
    kg2 = jnp.tile(w["b_k_gain"], 2)[None, :]
    proj = _mm(xb, w["w_in"], tb=True, name=n("proj"))
    aq, ak4, av4, bq, bk4, bv4 = _prep(proj, tabs, qg2, kg2, seg, name=n("prep"))
    oa, lse_a = _attn_win_fwd(aq, ak4, av4, w["a_sink"], name=n("attn_win"))
    gather = [(shards[k], l) for k in GATHERED_LATE] + ([(shards["w_in"], l + 1)] if l + 1 < DEPTH else [])
    ob, lse_b, gathered = _attn_dense_fwd(bq, bk4, bv4, gather=gather, name=n("attn_dense"))
    for k, g in zip(GATHERED_LATE, gathered):
        w[k] = _unshard(g, GATHER_AXIS[k])
    w_in_next = gathered[len(GATHERED_LATE)] if l + 1 < DEPTH else None
    oc = _gmlp_fwd(proj, w["c_ws"], w["c_bs3"], w["c_ln_g"], w["c_ln_b"], name=n("gmlp"))
    merged = _merge_fwd(oa, ob, oc, w["w_branch"], proj, w["b_gate"], name=n("merge"))
    x1, x1b, xh1, rs1 = _mm_res_ln(merged, w["w_mix_out"], x, w["ln1_g"], w["ln1_b"], name=n("mix_ln1"))
    xq = _mm(x1b, w["x_wq"], out_dtype=BF16, name=n("xq"))
    xkv = _mm(memb, w["x_wkv"], out_dtype=BF16, name=n("xkv"))
    xo, lse_x = _xattn_fwd(xq, xkv, name=n("xattn"))
    x2, x2b, xh2, rs2 = _mm_res_ln(xo, w["x_wo"], x1, w["ln2_g"], w["ln2_b"], name=n("xo_ln2"))
    h = _mm(x2b, w["f_w_up"], tb=True, name=n("ffn_up"))
    act = _conv_act_fwd(h, w["f_conv_k"], w["f_conv_b"], name=n("conv_act"))
    x3, x3b, xh3, rs3 = _mm_res_ln(act, w["f_w_down"], x2, w["ln3_g"], w["ln3_b"], name=n("down_ln3"))
    saved = dict(xb=xb, proj=proj, aq=aq, ak4=ak4, av4=av4, bq=bq, bk4=bk4, bv4=bv4, lse_a=lse_a, lse_b=lse_b,
                 oa=oa, ob=ob, oc=oc, merged=merged, xh1=xh1, rs1=rs1, x1b=x1b, xq=xq, xkv=xkv, xo=xo, lse_x=lse_x,
                 xh2=xh2, rs2=rs2, x2b=x2b, h=h, act=act, xh3=xh3, rs3=rs3, qg2=qg2, kg2=kg2)
    return x3, x3b, saved, w, w_in_next


def _layer_bwd(top, memb, w, sv, tabs, seg, l, ln_below, dw_in_above, recv):
    n = lambda s: f"L{l}_{s}"
    g = {}
    big = {}
    recv = dict(recv)

    def dw(key, a, b, tag):
        big[key] = _mm(a, b, ta=True, out_dtype=BF16, name=n(tag))

    def dw_t(key, segments, x, tag):
        buf = jax.ShapeDtypeStruct((sum(s.shape[1] for s in segments), x.shape[1]), BF16)
        row = 0
        for i, s in enumerate(segments):
            buf = _mm(s, x, ta=True, into=(buf, row), name=n(f"{tag}{i}"))
            row += s.shape[1]
        big[key] = buf

    dz3, dz3b, g["ln3_g"], g["ln3_b"] = top
    dw("f_w_down", sv["act"], dz3b, "dw_down")
    dact = _mm(dz3b, w["f_w_down"], tb=True, name=n("dact"))
    dha, dhb, dcka, dckb, dcba, dcbb = _conv_act_bwd(sv["h"], w["f_conv_k"], w["f_conv_b"], dact, name=n("conv_act_bwd"))
    g["f_conv_k"] = jnp.concatenate([dcka, dckb], axis=1)
    g["f_conv_b"] = jnp.concatenate([dcba, dcbb], axis=1)[0]
    dw_t("f_w_up", [dha, dhb], sv["x2b"], "dw_up")
    dz2, dz2b, g["ln2_g"], g["ln2_b"] = _mm([dha, dhb], w["f_w_up"], res=dz3, res_scale=ALPHA,
                                            ln_bwd=(sv["xh2"], sv["rs2"], w["ln2_g"]), name=n("dx2_ln2"))
    dw("x_wo", sv["xo"], dz2b, "dw_xo")
    dxo = _mm(dz2b, w["x_wo"], tb=True, out_dtype=BF16, name=n("dxo"))
    dxq, dxkv = _xattn_bwd(sv["xq"], sv["xkv"], sv["lse_x"], dxo, name=n("xattn_bwd"))
    dw("x_wq", sv["x1b"], dxq, "dw_xq")
    dw("x_wkv", memb, dxkv, "dw_xkv")
    dz1, dz1b, g["ln1_g"], g["ln1_b"] = _mm(dxq, w["x_wq"], tb=True, res=dz2, res_scale=ALPHA,
                                            ln_bwd=(sv["xh1"], sv["rs1"], w["ln1_g"]), name=n("dx1_ln1"))
    dw("w_mix_out", sv["merged"], dz1b, "dw_mix")
    dmerged = _mm(dz1b, w["w_mix_out"], tb=True, name=n("dmerged"))
    dgl, dbg, dt, dbr = _merge_bwd(sv["oa"], sv["ob"], sv["oc"], w["w_branch"], sv["proj"], w["b_gate"], dmerged,
                                   name=n("merge_bwd"))
    g["b_gate"] = dbg[0]
    for i, k in enumerate(("oa", "ob", "oc")):
        dw(f"w_branch{i}", sv[k], dt[i], f"dw_branch{i}")
    big["w_branch"] = jnp.stack([big.pop(f"w_branch{i}") for i in range(N_BRANCH)])
    dqa, dka, dva, dsink = _attn_win_bwd(sv["aq"], sv["ak4"], sv["av4"], w["a_sink"], sv["lse_a"], dbr[0],
                                         name=n("attn_win_bwd"))
    g["a_sink"] = dsink[:, 0]
    sent = [k for k in BIG if k != "w_in"]
    scatter = [(_reshard(big[k], BIG_AXIS[k]), recv[k], l) for k in sent]
    if dw_in_above is not None:
        sent.append("w_in")
        scatter.append((_reshard(dw_in_above, BIG_AXIS["w_in"]), recv["w_in"], l + 1))
    dqb, dkb, dvb, got = _attn_dense_bwd(sv["bq"], sv["bk4"], sv["bv4"], sv["lse_b"], dbr[1], scatter=scatter,
                                         name=n("attn_dense_bwd"))
    recv.update(zip(sent, got))
    dcz, g["c_ws"], dbs3, dlg, dlb = _gmlp_bwd(sv["proj"], dbr[2], w["c_ws"], w["c_bs3"], w["c_ln_g"], w["c_ln_b"],
                                               name=n("gmlp_bwd"))
    g["c_bs"] = dbs3[:, :, 0]
    g["c_ln_g"], g["c_ln_b"] = dlg[0], dlb[0]
    dqkv, dqg, dkg = _unprep(dqa, dka, dva, dqb, dkb, dvb, sv["proj"], tabs, sv["qg2"], sv["kg2"], seg, name=n("unprep"))
    g["b_q_gain"] = dqg[0, :HEAD_DIM] + dqg[0, HEAD_DIM:]
    g["b_k_gain"] = dkg[0, :HEAD_DIM] + dkg[0, HEAD_DIM:]
    dw_t("w_in", [dqkv, dcz, dgl], sv["xb"], "dw_in")
    dx0 = _mm([dqkv, dcz, dgl], w["w_in"], res=dz1, res_scale=ALPHA, name=n("dx0"))
    if ln_below is not None:
        dx0 = _ln_bwd(dx0, *ln_below, name=n("ln_bwd_below"))
    for k in ("ln1_g", "ln1_b", "ln2_g", "ln2_b", "ln3_g", "ln3_b"):
        g[k] = g[k][0]
    return dx0, g, big["w_in"], recv


WEIGHTS = ("w_in", "b_gate", "a_sink", "b_q_gain", "b_k_gain", "c_ln_g", "c_ln_b", "c_ws", "c_bs", "w_branch",
           "w_mix_out", "ln1_g", "ln1_b", "x_wq", "x_wkv", "x_wo", "ln2_g", "ln2_b", "f_w_up", "f_conv_k",
           "f_conv_b", "f_w_down", "ln3_g", "ln3_b")
TRANSPOSED = ("w_in", "f_w_up")
BIG_AXIS = {"w_in": 0, "w_branch": 2, "w_mix_out": 0, "x_wq": 0, "x_wkv": 0, "x_wo": 1, "f_w_up": 0, "f_w_down": 0}
BIG = tuple(BIG_AXIS)
GATHERED = BIG + ("f_conv_k",)
GATHERED_LATE = tuple(k for k in GATHERED if k != "w_in")
GATHER_AXIS = dict(BIG_AXIS, f_conv_k=1)
SMALL = tuple(k for k in WEIGHTS if k not in GATHERED)


def _unshard(g, axis):
    t = jnp.moveaxis(g, 0, axis)
    return t.reshape(t.shape[:axis] + (t.shape[axis] * t.shape[axis + 1],) + t.shape[axis + 2:])


def _reshard(full, axis):
    t = full.reshape(full.shape[:axis] + (N_DEV, full.shape[axis] // N_DEV) + full.shape[axis + 1:])
    return jnp.moveaxis(t, axis, 0)


def _small_weights(small, l):
    w = {k: v[l] for k, v in small.items()}
    for k in ("c_ln_g", "c_ln_b", "ln1_g", "ln1_b", "ln2_g", "ln2_b", "ln3_g", "ln3_b", "b_gate", "f_conv_b"):
        w[k] = w[k][None, :]
    w["c_bs3"] = w["c_bs"][:, :, None]
    w["c_ws"] = w["c_ws"].astype(BF16)
    return w


def _local_step(x, mem, target, small, shards):
    S = x.shape[0]
    tabs = _rope_tables(S)
    seg = _seg_matrix()
    memb = mem.astype(BF16)
    xb = x.astype(BF16)
    saved, weights = [], []
    w_in_g = _gather_call([(shards["w_in"], 0)], name="gather_w_in_L0")[0]
    for l in range(DEPTH):
        w = dict(_small_weights(small, l), w_in=_unshard(w_in_g, GATHER_AXIS["w_in"]))
        x, xb, sv, w, w_in_g = _layer_fwd(x, xb, memb, w, shards, tabs, seg, l)
        saved.append(sv)
        weights.append(w)
    dy, loss = _loss_head(x, target, name="loss_head")
    grads = [None] * DEPTH
    recv = {k: jax.ShapeDtypeStruct((DEPTH, N_DEV) + shards[k].shape[1:], BF16) for k in BIG}
    dw_in = None
    last_ln = lambda l: (saved[l]["xh3"], saved[l]["rs3"], weights[l]["ln3_g"])
    top = _ln_bwd(dy, *last_ln(DEPTH - 1), name="ln_bwd_top")
    for l in reversed(range(DEPTH)):
        top, grads[l], dw_in, recv = _layer_bwd(top, memb, weights[l], saved[l], tabs, seg, l,
                                                last_ln(l - 1) if l > 0 else None, dw_in, recv)
    recv["w_in"] = _scatter_call([(_reshard(dw_in, BIG_AXIS["w_in"]), recv["w_in"], 0)], name="scatter_w_in_L0")[0]
    return loss, top, grads, [recv[k] for k in BIG]


PACK_W = 1024


def _gather_call(gather, *, name):
    na = len(gather)

    def body(*refs):
        start, finish = _gather_plan([(refs[a], gather[a][1], refs[na + a]) for a in range(na)], *refs[2 * na:])
        start()
        finish()

    return list(pl.pallas_call(
        body, name=name,
        out_shape=[_gathered_shape(x) for x, _ in gather],
        in_specs=[_ANY] * na, out_specs=[_ANY] * na,
        scratch_shapes=_comm_scratch(na),
    )(*[x for x, _ in gather]))


def _scatter_io(scatter):
    held = [a for a, (_, r, _) in enumerate(scatter) if not isinstance(r, jax.ShapeDtypeStruct)]
    return ([s for s, _, _ in scatter] + [scatter[a][1] for a in held],
            [jax.ShapeDtypeStruct(r.shape, r.dtype) for _, r, _ in scatter], held)


def _scatter_call(scatter, *, name):
    na = len(scatter)
    operands, out_shape, held = _scatter_io(scatter)
    n_in = len(operands)

    def body(*refs):
        start, finish = _scatter_plan([(refs[a], refs[n_in + a], scatter[a][2]) for a in range(na)],
                                      *refs[n_in + na:])
        start()
        finish()

    return list(pl.pallas_call(
        body, name=name,
        out_shape=out_shape,
        in_specs=[_ANY] * n_in, out_specs=[_ANY] * na,
        scratch_shapes=_comm_scratch(na),
        input_output_aliases={na + i: a for i, a in enumerate(held)},
    )(*operands))


def _sum_parts(parts, *, name):
    P, R, C = parts.shape
    tr = _tile(R, 64, align=8)

    def body(p_ref, o_ref):
        g = p_ref[0].astype(F32)
        for s in range(1, P):
            g = g + p_ref[s].astype(F32)
        o_ref[...] = g

    return pl.pallas_call(
        body, name=name, out_shape=jax.ShapeDtypeStruct((R, C), F32), grid=(R // tr,),
        in_specs=[pl.BlockSpec((P, tr, C), lambda i: (0, i, 0))], out_specs=pl.BlockSpec((tr, C), lambda i: (i, 0)),
        compiler_params=_cp(("parallel",)),
    )(parts)


ADAM_BLOCK_ELEMS = 512 * 1024


def _adamw(parts, w, m, v, *, name):
    L, P, R, C = parts.shape
    assert w.shape == (L, R, C), (parts.shape, w.shape)
    tr = _tile(R, max(16, ADAM_BLOCK_ELEMS // C), align=16)

    def body(p_ref, w_ref, m_ref, v_ref, g_ref, d_ref, nm_ref, nv_ref):
        g = p_ref[0].astype(F32)
        for s in range(1, P):
            g = g + p_ref[s].astype(F32)
        nm = ADAM_B1 * m_ref[...] + (1.0 - ADAM_B1) * g
        nv = ADAM_B2 * v_ref[...] + (1.0 - ADAM_B2) * (g * g)
        m_hat = nm / (1.0 - ADAM_B1 ** ADAM_STEP)
        v_hat = nv / (1.0 - ADAM_B2 ** ADAM_STEP)
        g_ref[...] = g
        d_ref[...] = -ADAM_LR * (m_hat / (jnp.sqrt(v_hat) + ADAM_EPS) + ADAM_WD * w_ref[...])
        nm_ref[...] = nm
        nv_ref[...] = nv

    blk = pl.BlockSpec((None, tr, C), lambda l, i: (l, i, 0))
    shp = jax.ShapeDtypeStruct((L, R, C), F32)
    return pl.pallas_call(
        body, name=name, out_shape=(shp, shp, shp, shp), grid=(L, R // tr),
        in_specs=[pl.BlockSpec((None, P, tr, C), lambda l, i: (l, 0, i, 0)), blk, blk, blk],
        out_specs=(blk, blk, blk, blk),
        compiler_params=_cp(("parallel", "parallel")),
    )(parts, w, m, v)


def _pad_rows(vec, width, row_align):
    n = vec.shape[0]
    rows = -(-n // width)
    rows = -(-rows // row_align) * row_align
    return jnp.pad(vec, (0, rows * width - n)).reshape(rows, width)


def kernel(x, mem, w_in, b_gate, a_sink, b_q_gain, b_k_gain, c_ln_g, c_ln_b, c_ws, c_bs, w_branch, w_mix_out, ln1_g, ln1_b, x_wq, x_wkv, x_wo, ln2_g, ln2_b, f_w_up, f_conv_k, f_conv_b, f_w_down, ln3_g, ln3_b, loss_target, m_w_in, m_b_gate, m_a_sink, m_b_q_gain, m_b_k_gain, m_c_ln_g, m_c_ln_b, m_c_ws, m_c_bs, m_w_branch, m_w_mix_out, m_ln1_g, m_ln1_b, m_x_wq, m_x_wkv, m_x_wo, m_ln2_g, m_ln2_b, m_f_w_up, m_f_conv_k, m_f_conv_b, m_f_w_down, m_ln3_g, m_ln3_b, v_w_in, v_b_gate, v_a_sink, v_b_q_gain, v_b_k_gain, v_c_ln_g, v_c_ln_b, v_c_ws, v_c_bs, v_w_branch, v_w_mix_out, v_ln1_g, v_ln1_b, v_x_wq, v_x_wkv, v_x_wo, v_ln2_g, v_ln2_b, v_f_w_up, v_f_conv_k, v_f_conv_b, v_f_w_down, v_ln3_g, v_ln3_b):
    w = dict(w_in=w_in, b_gate=b_gate, a_sink=a_sink, b_q_gain=b_q_gain, b_k_gain=b_k_gain, c_ln_g=c_ln_g,
             c_ln_b=c_ln_b, c_ws=c_ws, c_bs=c_bs, w_branch=w_branch, w_mix_out=w_mix_out, ln1_g=ln1_g, ln1_b=ln1_b,
             x_wq=x_wq, x_wkv=x_wkv, x_wo=x_wo, ln2_g=ln2_g, ln2_b=ln2_b, f_w_up=f_w_up, f_conv_k=f_conv_k,
             f_conv_b=f_conv_b, f_w_down=f_w_down, ln3_g=ln3_g, ln3_b=ln3_b)
    m = dict(w_in=m_w_in, b_gate=m_b_gate, a_sink=m_a_sink, b_q_gain=m_b_q_gain, b_k_gain=m_b_k_gain,
             c_ln_g=m_c_ln_g, c_ln_b=m_c_ln_b, c_ws=m_c_ws, c_bs=m_c_bs, w_branch=m_w_branch, w_mix_out=m_w_mix_out,
             ln1_g=m_ln1_g, ln1_b=m_ln1_b, x_wq=m_x_wq, x_wkv=m_x_wkv, x_wo=m_x_wo, ln2_g=m_ln2_g, ln2_b=m_ln2_b,
             f_w_up=m_f_w_up, f_conv_k=m_f_conv_k, f_conv_b=m_f_conv_b, f_w_down=m_f_w_down, ln3_g=m_ln3_g,
             ln3_b=m_ln3_b)
    v = dict(w_in=v_w_in, b_gate=v_b_gate, a_sink=v_a_sink, b_q_gain=v_b_q_gain, b_k_gain=v_b_k_gain,
             c_ln_g=v_c_ln_g, c_ln_b=v_c_ln_b, c_ws=v_c_ws, c_bs=v_c_bs, w_branch=v_w_branch, w_mix_out=v_w_mix_out,
             ln1_g=v_ln1_g, ln1_b=v_ln1_b, x_wq=v_x_wq, x_wkv=v_x_wkv, x_wo=v_x_wo, ln2_g=v_ln2_g, ln2_b=v_ln2_b,
             f_w_up=v_f_w_up, f_conv_k=v_f_conv_k, f_conv_b=v_f_conv_b, f_w_down=v_f_w_down, ln3_g=v_ln3_g,
             ln3_b=v_ln3_b)
    me = 4 * lax.axis_index("x") + 2 * lax.axis_index("y") + lax.axis_index("c")

    def held(k, t):
        return jnp.swapaxes(t, 1, 2) if k in TRANSPOSED else t

    shards = dict({k: held(k, w[k]).astype(BF16) for k in BIG}, f_conv_k=w["f_conv_k"])
    loss, grad_x, grads, recvs = _local_step(x[0], mem[0], loss_target[0], {k: w[k] for k in SMALL}, shards)
    loss = lax.psum(loss[0, 0], ("x", "y", "c"))

    out_g, out_d, out_m, out_v = {}, {}, {}, {}
    for k, recv in zip(BIG, recvs):
        shp = held(k, w[k]).shape
        rc = (DEPTH, math.prod(shp[1:-1]), shp[-1])
        parts = recv.reshape((DEPTH, N_DEV) + rc[1:])
        g_, d_, m_, v_ = _adamw(parts, held(k, w[k]).reshape(rc), held(k, m[k]).reshape(rc),
                                held(k, v[k]).reshape(rc), name=f"adamw_{k}")
        out_g[k], out_d[k], out_m[k], out_v[k] = (held(k, t.reshape(shp)) for t in (g_, d_, m_, v_))

    small_all = SMALL + ("f_conv_k",)
    gfull = {k: jnp.stack([grads[l][k] for l in range(DEPTH)]) for k in small_all}

    def pack(d):
        return jnp.concatenate([_pad_rows(d[k].reshape(-1), PACK_W, 8) for k in small_all])

    def unpack(rows, like):
        out, r = {}, 0
        for k in small_all:
            nr = -(-like[k].size // (8 * PACK_W)) * 8
            out[k] = rows[r:r + nr].reshape(-1)[:like[k].size].reshape(like[k].shape)
            r += nr
        return out

    gathered = _gather_call([(pack(gfull)[None], 0)], name="gather_small_grads")[0]
    sg = unpack(_sum_parts(gathered, name="sum_small_grads"), gfull)
    width = w["f_conv_k"].shape[2]
    sg["f_conv_k"] = lax.dynamic_slice_in_dim(sg["f_conv_k"], me * width, width, axis=2)
    g_, d_, m_, v_ = _adamw(pack(sg)[None, None], pack(w)[None], pack(m)[None], pack(v)[None], name="adamw_small")
    ud, um, uv = (unpack(t[0], w) for t in (d_, m_, v_))
    for k in small_all:
        out_g[k], out_d[k], out_m[k], out_v[k] = sg[k], ud[k], um[k], uv[k]

    return (loss, grad_x[None], *[out_g[k] for k in WEIGHTS], *[out_d[k] for k in WEIGHTS],
            *[out_m[k] for k in WEIGHTS], *[out_v[k] for k in WEIGHTS])
```

```python
import functools
import math

import jax
import jax.numpy as jnp
from jax import lax
from jax.experimental import pallas as pl
from jax.experimental.pallas import tpu as pltpu

F32 = jnp.float32
BF16 = jnp.bfloat16

DEPTH = 4
HEAD_DIM = 64
BLOCK = 128
WINDOW = 128
GRID_W = 64
C_WIDTH = 512
C_GROUPS = 4
CHUNK = 128
N_BRANCH = 3
BRANCH_WIDTH = 512
ROPE_THETA = 10000.0
X_HEADS = 4
X_HEAD_DIM = 128
ALPHA = (2 * DEPTH) ** 0.25
LN_EPS = 1e-5
RMS_EPS = 1e-6
ADAM_LR = 0.001
ADAM_B1 = 0.9
ADAM_B2 = 0.999
ADAM_EPS = 1e-08
ADAM_WD = 0.01
ADAM_STEP = 10
N_DEV = 8

COL_A = 0
COL_B = 768
COL_C = 1536
COL_GATE = 2560
QKV_W = 768

LANES = 128
V7X_VMEM_BYTES = 64 * 1024 * 1024
VMEM_LIMIT = V7X_VMEM_BYTES - 8 * 1024 * 1024
NEG_BIG = -1e30
ROW_TILE = 512

_NT = (((1,), (1,)), ((), ()))
_TN = (((0,), (0,)), ((), ()))
_NN = (((1,), (0,)), ((), ()))


def _cp(sem=None):
    return pltpu.CompilerParams(dimension_semantics=sem, vmem_limit_bytes=VMEM_LIMIT)


def _tile(n, target, align=LANES):
    if n <= target:
        return n
    best = None
    for t in range(align, target + 1, align):
        if n % t == 0:
            best = t
    assert best is not None, (n, target)
    return best


def _dot(a, b, dims=_NN):
    return lax.dot_general(a, b, dims, preferred_element_type=F32)


def _gelu(x):
    return 0.5 * x * (1.0 + lax.erf(x * 0.7071067811865476))


def _gelu_and_grad(x):
    cdf = 0.5 * (1.0 + lax.erf(x * 0.7071067811865476))
    return x * cdf, cdf + x * jnp.exp(-0.5 * x * x) * 0.3989422804014327


def _sigmoid(x):
    return 1.0 / (1.0 + jnp.exp(-x))


MESH_ID = pl.DeviceIdType.MESH
_ANY = pl.BlockSpec(memory_space=pl.ANY)
COPIES_PER_ARRAY = N_DEV - 1


def _comm_scratch(n_arrays):
    return [pltpu.SemaphoreType.DMA((COPIES_PER_ARRAY * n_arrays,)),
            pltpu.SemaphoreType.DMA((COPIES_PER_ARRAY * n_arrays,)), pltpu.SemaphoreType.DMA((n_arrays,))]


def _gathered_shape(x):
    return jax.ShapeDtypeStruct((N_DEV,) + x.shape[1:], x.dtype)


def _gather_plan(entries, send_sems, recv_sems, local_sems):
    mx, my, mc = lax.axis_index("x"), lax.axis_index("y"), lax.axis_index("c")
    me, sibling = (mx, my, mc), (mx, my, 1 - mc)
    chips = [(1 - mx, my), (mx, 1 - my), (1 - mx, 1 - my)]

    def copy(a, k, block, to, from_shard=False):
        x_ref, l, out_ref = entries[a]
        dst = out_ref.at[4 * block[0] + 2 * block[1] + block[2]]
        return pltpu.make_async_remote_copy(
            src_ref=x_ref.at[l] if from_shard else dst, dst_ref=dst,
            send_sem=send_sems.at[COPIES_PER_ARRAY * a + k], recv_sem=recv_sems.at[COPIES_PER_ARRAY * a + k],
            device_id=to, device_id_type=MESH_ID)

    def own(a):
        x_ref, l, out_ref = entries[a]
        return pltpu.make_async_copy(x_ref.at[l], out_ref.at[4 * mx + 2 * my + mc], local_sems.at[a])

    def first(a):
        return [copy(a, 0, me, sibling, True)] + [copy(a, 1 + j, me, (*chip, mc), True) for j, chip in enumerate(chips)]

    def passed(a):
        return [copy(a, 4 + j, (*chip, mc), sibling) for j, chip in enumerate(chips)]

    def start():
        for a in range(len(entries)):
            own(a).start()
            for cp in first(a):
                cp.start()

    def finish():
        for a in range(len(entries)):
            fwd = passed(a)
            for j, chip in enumerate(chips):
                copy(a, 1 + j, (*chip, mc), me).wait_recv()
                fwd[j].start()
        for a in range(len(entries)):
            copy(a, 0, sibling, me).wait_recv()
            for j, chip in enumerate(chips):
                copy(a, 4 + j, (*chip, 1 - mc), me).wait_recv()
            for cp in first(a) + passed(a):
                cp.wait_send()
            own(a).wait()

    return start, finish


def _scatter_plan(entries, send_sems, recv_sems, local_sems):
    mx, my, mc = lax.axis_index("x"), lax.axis_index("y"), lax.axis_index("c")
    me = 4 * mx + 2 * my + mc

    def src(a, dev):
        return entries[a][0].at[dev]

    def copies(a):
        _, recv_ref, lr = entries[a]
        out = []
        for k in range(1, N_DEV):
            px = 1 - mx if k & 4 else mx
            py = 1 - my if k & 2 else my
            pc = 1 - mc if k & 1 else mc
            peer = 4 * px + 2 * py + pc
            sems = dict(send_sem=send_sems.at[COPIES_PER_ARRAY * a + k - 1],
                        recv_sem=recv_sems.at[COPIES_PER_ARRAY * a + k - 1],
                        device_id=(px, py, pc), device_id_type=MESH_ID)
            sends = pltpu.make_async_remote_copy(src_ref=src(a, peer), dst_ref=recv_ref.at[lr, me], **sems)
            lands = pltpu.make_async_remote_copy(src_ref=src(a, me), dst_ref=recv_ref.at[lr, peer], **sems)
            out.append((sends, lands))
        return out

    def own(a):
        _, recv_ref, lr = entries[a]
        return pltpu.make_async_copy(src(a, me), recv_ref.at[lr, me], local_sems.at[a])

    def start():
        for a in range(len(entries)):
            own(a).start()
            for sends, _ in copies(a):
                sends.start()

    def finish():
        for a in range(len(entries)):
            for _, lands in copies(a):
                lands.wait_recv()
        for a in range(len(entries)):
            for sends, _ in copies(a):
                sends.wait_send()
            own(a).wait()

    return start, finish


MM_TILE, MM_TK = 1536, 2048
MM_TILE_LN = 512


def _mm(a, b, *, ta=False, tb=False, out_dtype=F32, res=None, res_scale=1.0, into=None, ln_bwd=None, name):
    segs = list(a) if isinstance(a, (list, tuple)) else [a]
    if ta:
        (K, M), seg_k = segs[0].shape, [segs[0].shape[0]]
        assert len(segs) == 1
    else:
        M, seg_k = segs[0].shape[0], [s.shape[1] for s in segs]
        K = sum(seg_k)
    if tb:
        N, Kb = b.shape
    else:
        Kb, N = b.shape
    assert K == Kb, ([s.shape for s in segs], b.shape, ta, tb)
    row_off = into[1] if into is not None else 0
    tm, tn = _tile(math.gcd(M, row_off), MM_TILE if ln_bwd is None else MM_TILE_LN), _tile(N, MM_TILE)
    tk = _tile(K, MM_TK) if len(segs) == 1 else _tile(math.gcd(*seg_k), MM_TILE)
    nk = K // tk
    seg_chunks = [ks // tk for ks in seg_k]
    seg_first = [sum(seg_chunks[:s]) for s in range(len(segs))]
    dims = (((0 if ta else 1,), (1 if tb else 0,)), ((), ()))
    ns = len(segs)
    n_res = ns + 1
    n_ln = n_res + (res is not None)
    into_held = into is not None and not isinstance(into[0], jax.ShapeDtypeStruct)
    n_in = n_ln + (3 if ln_bwd is not None else 0) + into_held
    assert ln_bwd is None or (tn == N and into is None)

    def body(*refs):
        a_refs, b_ref = refs[:ns], refs[ns]
        r_ref = refs[n_res] if res is not None else None
        o_ref = refs[n_in]
        first_row_tile = pl.program_id(0) == 0

        def finish(out):
            if r_ref is not None:
                out = out + res_scale * r_ref[...]
            if ln_bwd is None:
                o_ref[...] = out.astype(o_ref.dtype)
                return
            xh_ref, rs_ref, g_ref = refs[n_ln:n_ln + 3]
            ob_ref, dg_ref, db_ref = refs[n_in + 1:n_in + 4]

            @pl.when(first_row_tile)
            def _():
                dg_ref[...] = jnp.zeros_like(dg_ref)
                db_ref[...] = jnp.zeros_like(db_ref)

            xh = xh_ref[...]
            dxh = out * g_ref[...]
            m1 = jnp.mean(dxh, axis=-1, keepdims=True)
            m2 = jnp.mean(dxh * xh, axis=-1, keepdims=True)
            dz = rs_ref[...] * (dxh - m1 - xh * m2)
            o_ref[...] = dz
            ob_ref[...] = dz.astype(BF16)
            dg_ref[...] += jnp.sum(out * xh, axis=0, keepdims=True)
            db_ref[...] += jnp.sum(out, axis=0, keepdims=True)

        def prod(s):
            return _dot(a_refs[s][...].astype(BF16), b_ref[...].astype(BF16), dims)

        if nk == 1:
            finish(prod(0))
            return
        acc = refs[n_in + (4 if ln_bwd is not None else 1)]
        k = pl.program_id(2)

        @pl.when(k == 0)
        def _():
            acc[...] = jnp.zeros_like(acc)

        for s in range(ns):
            def add(s=s):
                acc[...] += prod(s)
            pl.when((k >= seg_first[s]) & (k < seg_first[s] + seg_chunks[s]))(add)

        @pl.when(k == nk - 1)
        def _():
            finish(acc[...])

    if ta:
        a_specs = [pl.BlockSpec((tk, tm), lambda i, j, k: (k, i))]
    else:
        a_specs = [pl.BlockSpec((tm, tk), functools.partial(
            lambda i, j, k, first, n: (i, jnp.clip(k - first, 0, n - 1)), first=seg_first[s], n=seg_chunks[s]))
            for s in range(ns)]
    b_spec = pl.BlockSpec((tn, tk), lambda i, j, k: (j, k)) if tb else pl.BlockSpec((tk, tn), lambda i, j, k: (k, j))
    in_specs = a_specs + [b_spec]
    args = segs + [b]
    if res is not None:
        in_specs.append(pl.BlockSpec((tm, tn), lambda i, j, k: (i, j)))
        args.append(res)
    out_spec = pl.BlockSpec((tm, tn), lambda i, j, k: (i, j))
    if ln_bwd is not None:
        xh, rs, g = ln_bwd
        in_specs += [out_spec, pl.BlockSpec((tm, 1), lambda i, j, k: (i, 0)), pl.BlockSpec((1, tn), lambda i, j, k: (0, j))]
        args += [xh, rs, g]
        vec = pl.BlockSpec((1, tn), lambda i, j, k: (0, j))
        return pl.pallas_call(
            body, name=name,
            out_shape=(jax.ShapeDtypeStruct((M, N), F32), jax.ShapeDtypeStruct((M, N), BF16),
                       jax.ShapeDtypeStruct((1, N), F32), jax.ShapeDtypeStruct((1, N), F32)),
            grid=(M // tm, N // tn, nk),
            in_specs=in_specs,
            out_specs=(out_spec, out_spec, vec, vec),
            scratch_shapes=[pltpu.VMEM((tm, tn), F32)] if nk > 1 else [],
            compiler_params=_cp(("arbitrary", "arbitrary", "arbitrary")),
        )(*args)
    if into is None:
        out_shape = jax.ShapeDtypeStruct((M, N), out_dtype)
        blk_off, aliases = 0, {}
    else:
        buf = into[0]
        assert buf.shape[1] == N and row_off % tm == 0 and row_off + M <= buf.shape[0], (buf.shape, M, N, row_off)
        out_shape = jax.ShapeDtypeStruct(buf.shape, buf.dtype)
        blk_off, aliases = row_off // tm, {}
        if into_held:
            aliases = {n_in - 1: 0}
            in_specs.append(_ANY)
            args.append(buf)
    return pl.pallas_call(
        body, name=name,
        out_shape=out_shape,
        grid=(M // tm, N // tn, nk),
        in_specs=in_specs,
        out_specs=pl.BlockSpec((tm, tn), lambda i, j, k: (i + blk_off, j)),
        scratch_shapes=[pltpu.VMEM((tm, tn), F32)] if nk > 1 else [],
        input_output_aliases=aliases,
        compiler_params=_cp(("parallel", "parallel", "arbitrary")),
    )(*args)


def _mm_res_ln(a, w, x, g, b, *, name):
    S, K = a.shape
    D = w.shape[1]
    tm = _tile(S, ROW_TILE)

    def body(a_ref, w_ref, x_ref, g_ref, b_ref, y_ref, yb_ref, xh_ref, rs_ref):
        h = _dot(a_ref[...], w_ref[...])
        z = ALPHA * x_ref[...] + h
        mu = jnp.mean(z, axis=-1, keepdims=True)
        zc = z - mu
        var = jnp.mean(zc * zc, axis=-1, keepdims=True)
        r = lax.rsqrt(var + LN_EPS)
        xh = zc * r
        y = xh * g_ref[...] + b_ref[...]
        y_ref[...] = y
        yb_ref[...] = y.astype(BF16)
        xh_ref[...] = xh
        rs_ref[...] = r

    row = lambda i: (i, 0)
    full = lambda i: (0, 0)
    return pl.pallas_call(
        body, name=name,
        out_shape=(jax.ShapeDtypeStruct((S, D), F32), jax.ShapeDtypeStruct((S, D), BF16),
                   jax.ShapeDtypeStruct((S, D), F32), jax.ShapeDtypeStruct((S, 1), F32)),
        grid=(S // tm,),
        in_specs=[pl.BlockSpec((tm, K), row), pl.BlockSpec((K, D), full), pl.BlockSpec((tm, D), row),
                  pl.BlockSpec((1, D), full), pl.BlockSpec((1, D), full)],
        out_specs=(pl.BlockSpec((tm, D), row), pl.BlockSpec((tm, D), row), pl.BlockSpec((tm, D), row),
                   pl.BlockSpec((tm, 1), row)),
        compiler_params=_cp(("parallel",)),
    )(a, w, x, g, b)


def _ln_bwd(dy, xh, rs, g, *, name):
    S, D = dy.shape
    tm = _tile(S, ROW_TILE)

    def body(dy_ref, xh_ref, rs_ref, g_ref, dz_ref, dzb_ref, dg_ref, db_ref):
        @pl.when(pl.program_id(0) == 0)
        def _():
            dg_ref[...] = jnp.zeros_like(dg_ref)
            db_ref[...] = jnp.zeros_like(db_ref)

        dy = dy_ref[...]
        xh = xh_ref[...]
        dxh = dy * g_ref[...]
        m1 = jnp.mean(dxh, axis=-1, keepdims=True)
        m2 = jnp.mean(dxh * xh, axis=-1, keepdims=True)
        dz = rs_ref[...] * (dxh - m1 - xh * m2)
        dz_ref[...] = dz
        dzb_ref[...] = dz.astype(BF16)
        dg_ref[...] += jnp.sum(dy * xh, axis=0, keepdims=True)
        db_ref[...] += jnp.sum(dy, axis=0, keepdims=True)

    row = lambda i: (i, 0)
    full = lambda i: (0, 0)
    return pl.pallas_call(
        body, name=name,
        out_shape=(jax.ShapeDtypeStruct((S, D), F32), jax.ShapeDtypeStruct((S, D), BF16),
                   jax.ShapeDtypeStruct((1, D), F32), jax.ShapeDtypeStruct((1, D), F32)),
        grid=(S // tm,),
        in_specs=[pl.BlockSpec((tm, D), row), pl.BlockSpec((tm, D), row), pl.BlockSpec((tm, 1), row),
                  pl.BlockSpec((1, D), full)],
        out_specs=(pl.BlockSpec((tm, D), row), pl.BlockSpec((tm, D), row), pl.BlockSpec((1, D), full),
                   pl.BlockSpec((1, D), full)),
        compiler_params=_cp(("arbitrary",)),
    )(dy, xh, rs, g)


def _loss_head(y, t, *, name):
    S, D = y.shape
    tm = _tile(S, 512)

    def body(y_ref, t_ref, dy_ref, l_ref):
        @pl.when(pl.program_id(0) == 0)
        def _():
            l_ref[...] = jnp.zeros_like(l_ref)

        e = y_ref[...] - t_ref[...]
        dy_ref[...] = e / D
        l_ref[...] += 0.5 * jnp.sum(jnp.mean(e * e, axis=-1, keepdims=True), axis=0, keepdims=True)

    row = lambda i: (i, 0)
    return pl.pallas_call(
        body, name=name,
        out_shape=(jax.ShapeDtypeStruct((S, D), F32), jax.ShapeDtypeStruct((1, 1), F32)),
        grid=(S // tm,),
        in_specs=[pl.BlockSpec((tm, D), row), pl.BlockSpec((tm, D), row)],
        out_specs=(pl.BlockSpec((tm, D), row), pl.BlockSpec((1, 1), lambda i: (0, 0))),
        compiler_params=_cp(("arbitrary",)),
    )(y, t)


def _rope_tables(S):
    pos = jnp.arange(S, dtype=jnp.int32)
    row = pos // GRID_W
    col = pos % GRID_W

    def cs(p, d):
        half = d // 2
        inv = ROPE_THETA ** (-jnp.arange(half, dtype=F32) * (2.0 / d))
        ang = p.astype(F32)[:, None] * inv[None, :]
        c, s = jnp.cos(ang), jnp.sin(ang)
        return jnp.concatenate([c, c], -1), jnp.concatenate([-s, s], -1)

    ca, sa = cs(pos, HEAD_DIM)
    cr, sr = cs(row, HEAD_DIM // 2)
    cc, sc = cs(col, HEAD_DIM // 2)
    cb, sb = jnp.concatenate([cr, cc], -1), jnp.concatenate([sr, sc], -1)
    two = lambda t: jnp.concatenate([t, t], -1)
    return two(ca), two(sa), two(cb), two(sb)


def _partner(x, lane, width):
    h = width // 2
    return jnp.where(lane % width < h, pltpu.roll(x, LANES - h, 1), pltpu.roll(x, h, 1))


def _rope_fwd(x, c, s, lane, width):
    return x * c + _partner(x, lane, width) * s


def _rope_bwd(dy, c, s, lane, width):
    return dy * c + _partner(dy * s, lane, width)


def _head_sum(x, seg):
    return lax.dot_general(x, seg, _NN, precision=lax.Precision.HIGHEST, preferred_element_type=F32)


def _split_heads(x, lane):
    lo = lane < HEAD_DIM
    r = pltpu.roll(x, HEAD_DIM, 1)
    z = jnp.zeros_like(x)
    return jnp.where(lo, x, z), jnp.where(lo, z, r), jnp.where(lo, r, z), jnp.where(lo, z, x)


def _fold_heads(d0, d1, lane):
    t0 = d0 + pltpu.roll(d0, HEAD_DIM, 1)
    t1 = d1 + pltpu.roll(d1, HEAD_DIM, 1)
    return jnp.where(lane < HEAD_DIM, t0, t1)


def _seg_matrix():
    i = jnp.arange(LANES)
    return (i[:, None] // HEAD_DIM == i[None, :] // HEAD_DIM).astype(F32)


def _prep(proj, tabs, qg2, kg2, seg, *, name):
    S = proj.shape[0]
    ts = _tile(S, ROW_TILE)
    ca, sa, cb, sb = tabs

    def body(pa_ref, pb_ref, ca_ref, sa_ref, cb_ref, sb_ref, qg_ref, kg_ref, seg_ref,
             aq_ref, ak_ref, av_ref, bq_ref, bk_ref, bv_ref):
        lane = lax.broadcasted_iota(jnp.int32, (ts, LANES), 1)
        ca, sa, cb, sb = ca_ref[...], sa_ref[...], cb_ref[...], sb_ref[...]
        seg = seg_ref[...]

        def norm(x, gain):
            r = lax.rsqrt(_head_sum(x * x, seg) * (1.0 / HEAD_DIM) + RMS_EPS)
            return x * r * gain

        def put(ref, x):
            for i, part in enumerate(_split_heads(x, lane)):
                ref[i] = part.astype(BF16)

        for gidx in range(4):
            cols = slice(gidx * LANES, (gidx + 1) * LANES)
            aq_ref[:, cols] = (_rope_fwd(pa_ref[:, cols], ca, sa, lane, HEAD_DIM) * 0.125).astype(BF16)
            bq = norm(pb_ref[:, cols], qg_ref[...])
            bq_ref[:, cols] = (_rope_fwd(bq, cb, sb, lane, HEAD_DIM // 2) * 0.125).astype(BF16)
        put(ak_ref, _rope_fwd(pa_ref[:, 512:640], ca, sa, lane, HEAD_DIM))
        put(av_ref, pa_ref[:, 640:768])
        bk = norm(pb_ref[:, 512:640], kg_ref[...])
        put(bk_ref, _rope_fwd(bk, cb, sb, lane, HEAD_DIM // 2))
        put(bv_ref, pb_ref[:, 640:768])

    row = lambda i: (i, 0)
    full = lambda i: (0, 0)
    tab = pl.BlockSpec((ts, LANES), row)
    kv_shape = jax.ShapeDtypeStruct((4, S, LANES), BF16)
    kv_spec = pl.BlockSpec((4, ts, LANES), lambda i: (0, i, 0))
    q_shape = jax.ShapeDtypeStruct((S, 512), BF16)
    q_spec = pl.BlockSpec((ts, 512), row)
    return pl.pallas_call(
        body, name=name,
        out_shape=(q_shape, kv_shape, kv_shape, q_shape, kv_shape, kv_shape),
        grid=(S // ts,),
        in_specs=[pl.BlockSpec((ts, QKV_W), lambda i: (i, 0)), pl.BlockSpec((ts, QKV_W), lambda i: (i, 1)),
                  tab, tab, tab, tab, pl.BlockSpec((1, LANES), full), pl.BlockSpec((1, LANES), full),
                  pl.BlockSpec((LANES, LANES), full)],
        out_specs=(q_spec, kv_spec, kv_spec, q_spec, kv_spec, kv_spec),
        compiler_params=_cp(("parallel",)),
    )(proj, proj, ca, sa, cb, sb, qg2, kg2, seg)


def _unprep(dqa, dka, dva, dqb, dkb, dvb, proj, tabs, qg2, kg2, seg, *, name):
    S = proj.shape[0]
    ts = _tile(S, ROW_TILE)
    ca, sa, cb, sb = tabs

    def body(dqa_ref, dka_ref, dva_ref, dqb_ref, dkb_ref, dvb_ref, pb_ref, ca_ref, sa_ref, cb_ref, sb_ref,
             qg_ref, kg_ref, seg_ref, dp_ref, dqg_ref, dkg_ref):
        @pl.when(pl.program_id(0) == 0)
        def _():
            dqg_ref[...] = jnp.zeros_like(dqg_ref)
            dkg_ref[...] = jnp.zeros_like(dkg_ref)

        lane = lax.broadcasted_iota(jnp.int32, (ts, LANES), 1)
        ca, sa, cb, sb = ca_ref[...], sa_ref[...], cb_ref[...], sb_ref[...]
        seg = seg_ref[...]

        def norm_bwd(dy, x, gain):
            r = lax.rsqrt(_head_sum(x * x, seg) * (1.0 / HEAD_DIM) + RMS_EPS)
            gdy = gain * dy
            dot = _head_sum(gdy * x, seg) * (1.0 / HEAD_DIM)
            dx = r * gdy - x * (r * r * r) * dot
            return dx, jnp.sum(dy * x * r, axis=0, keepdims=True)

        for gidx in range(4):
            cols = slice(gidx * LANES, (gidx + 1) * LANES)
            dp_ref[:, cols] = _rope_bwd(dqa_ref[:, cols] * 0.125, ca, sa, lane, HEAD_DIM).astype(BF16)
            dbq = _rope_bwd(dqb_ref[:, cols] * 0.125, cb, sb, lane, HEAD_DIM // 2)
            dx, dg = norm_bwd(dbq, pb_ref[:, cols], qg_ref[...])
            dp_ref[:, COL_B + gidx * LANES:COL_B + (gidx + 1) * LANES] = dx.astype(BF16)
            dqg_ref[...] += dg
        dak = _fold_heads(dka_ref[0] + dka_ref[1], dka_ref[2] + dka_ref[3], lane)
        dp_ref[:, 512:640] = _rope_bwd(dak, ca, sa, lane, HEAD_DIM).astype(BF16)
        dp_ref[:, 640:768] = _fold_heads(dva_ref[0] + dva_ref[1], dva_ref[2] + dva_ref[3], lane).astype(BF16)
        dbk = _fold_heads(dkb_ref[0] + dkb_ref[1], dkb_ref[2] + dkb_ref[3], lane)
        dbk = _rope_bwd(dbk, cb, sb, lane, HEAD_DIM // 2)
        dx, dg = norm_bwd(dbk, pb_ref[:, 512:640], kg_ref[...])
        dp_ref[:, COL_B + 512:COL_B + 640] = dx.astype(BF16)
        dkg_ref[...] += dg
        dp_ref[:, COL_B + 640:COL_B + 768] = _fold_heads(dvb_ref[0] + dvb_ref[1], dvb_ref[2] + dvb_ref[3],
                                                         lane).astype(BF16)

    row = lambda i: (i, 0)
    full = lambda i: (0, 0)
    tab = pl.BlockSpec((ts, LANES), row)
    q_spec = pl.BlockSpec((ts, 512), row)
    kv_spec = pl.BlockSpec((4, ts, LANES), lambda i: (0, i, 0))
    return pl.pallas_call(
        body, name=name,
        out_shape=(jax.ShapeDtypeStruct((S, 2 * QKV_W), BF16), jax.ShapeDtypeStruct((1, LANES), F32),
                   jax.ShapeDtypeStruct((1, LANES), F32)),
        grid=(S // ts,),
        in_specs=[q_spec, kv_spec, kv_spec, q_spec, kv_spec, kv_spec,
                  pl.BlockSpec((ts, QKV_W), lambda i: (i, 1)), tab, tab, tab, tab,
                  pl.BlockSpec((1, LANES), full), pl.BlockSpec((1, LANES), full), pl.BlockSpec((LANES, LANES), full)],
        out_specs=(pl.BlockSpec((ts, 2 * QKV_W), row), pl.BlockSpec((1, LANES), full),
                   pl.BlockSpec((1, LANES), full)),
        compiler_params=_cp(("arbitrary",)),
    )(dqa, dka, dva, dqb, dkb, dvb, proj, ca, sa, cb, sb, qg2, kg2, seg)


def _attn_dense_fwd(q, k4, v4, *, gather=(), name):
    S = q.shape[0]
    tq = _tile(S, 256)
    xs = [x for x, _ in gather]
    na = len(xs)

    def body(q_ref, k_ref, v_ref, *rest):
        o_ref, lse_ref = rest[na], rest[na + 1]
        if na:
            x_refs, out_refs, sems = rest[:na], rest[na + 2:2 * na + 2], rest[2 * na + 2:]
            start, finish = _gather_plan([(x_refs[a], gather[a][1], out_refs[a]) for a in range(na)], *sems)
            pl.when((pl.program_id(0) == 0) & (pl.program_id(1) == 0))(start)
        for pr in range(2):
            qp = q_ref[:, pr * LANES:(pr + 1) * LANES]
            acc = None
            for half in range(2):
                s = _dot(qp, k_ref[half], _NT)
                m = jnp.max(s, axis=-1, keepdims=True)
                e = jnp.exp(s - m)
                l = jnp.sum(e, axis=-1, keepdims=True)
                pv = _dot(e.astype(BF16), v_ref[half]) * (1.0 / l)
                acc = pv if acc is None else acc + pv
                lse_ref[pr * 2 + half] = m + jnp.log(l)
            o_ref[:, pr * LANES:(pr + 1) * LANES] = acc.astype(BF16)
        if na:
            pl.when((pl.program_id(0) == 1) & (pl.program_id(1) == S // tq - 1))(finish)

    kv_spec = pl.BlockSpec((2, S, LANES), lambda kv, i: (kv, 0, 0))
    res = pl.pallas_call(
        body, name=name,
        out_shape=(jax.ShapeDtypeStruct((S, 512), BF16), jax.ShapeDtypeStruct((8, S, 1), F32),
                   *[_gathered_shape(x) for x in xs]),
        grid=(2, S // tq),
        in_specs=[pl.BlockSpec((tq, 256), lambda kv, i: (i, kv)), kv_spec, kv_spec] + [_ANY] * na,
        out_specs=(pl.BlockSpec((tq, 256), lambda kv, i: (i, kv)),
                   pl.BlockSpec((4, tq, 1), lambda kv, i: (kv, i, 0)), *([_ANY] * na)),
        scratch_shapes=_comm_scratch(na) if na else [],
        compiler_params=_cp(("arbitrary", "arbitrary") if na else ("parallel", "parallel")),
    )(q, k4, v4, *xs)
    return res[0], res[1], list(res[2:])


def _attn_dense_bwd(q, k4, v4, lse, do, *, scatter=(), name):
    S = q.shape[0]
    tq = _tile(S, 256)
    na = len(scatter)
    comm_in, comm_out, held = _scatter_io(scatter)
    n_in = len(comm_in)

    def body(q_ref, k_ref, v_ref, lse_ref, do_ref, *rest):
        dq_ref, dk_ref, dv_ref = rest[n_in:n_in + 3]
        if na:
            s_refs, r_refs, sems = rest[:na], rest[n_in + 3:n_in + 3 + na], rest[n_in + 3 + na:]
            start, finish = _scatter_plan([(s_refs[a], r_refs[a], scatter[a][2]) for a in range(na)], *sems)
            pl.when((pl.program_id(0) == 0) & (pl.program_id(1) == 0))(start)

        @pl.when(pl.program_id(1) == 0)
        def _():
            dk_ref[...] = jnp.zeros_like(dk_ref)
            dv_ref[...] = jnp.zeros_like(dv_ref)

        lane = lax.broadcasted_iota(jnp.int32, (tq, LANES), 1)
        for pr in range(2):
            qp = q_ref[:, pr * LANES:(pr + 1) * LANES]
            dop = do_ref[:, pr * LANES:(pr + 1) * LANES].astype(BF16)
            dq = None
            for half in range(2):
                mine = (lane < HEAD_DIM) if half == 0 else (lane >= HEAD_DIM)
                s = _dot(qp, k_ref[half], _NT)
                p = jnp.exp(s - lse_ref[pr * 2 + half])
                dp = _dot(dop, v_ref[half], _NT)
                delta = jnp.sum(p * dp, axis=-1, keepdims=True)
                ds = (p * (dp - delta)).astype(BF16)
                pb = p.astype(BF16)
                d = _dot(ds, k_ref[half])
                dq = d if dq is None else dq + d
                dk_ref[half] += _dot(ds, jnp.where(mine, qp, jnp.zeros_like(qp)), _TN)
                dv_ref[half] += _dot(pb, jnp.where(mine, dop, jnp.zeros_like(dop)), _TN)
            dq_ref[:, pr * LANES:(pr + 1) * LANES] = dq
        if na:
            pl.when((pl.program_id(0) == 1) & (pl.program_id(1) == S // tq - 1))(finish)

    kv_spec = pl.BlockSpec((2, S, LANES), lambda kv, i: (kv, 0, 0))
    q_spec = pl.BlockSpec((tq, 256), lambda kv, i: (i, kv))
    res = pl.pallas_call(
        body, name=name,
        out_shape=(jax.ShapeDtypeStruct((S, 512), F32), jax.ShapeDtypeStruct((4, S, LANES), F32),
                   jax.ShapeDtypeStruct((4, S, LANES), F32), *comm_out),
        grid=(2, S // tq),
        in_specs=[q_spec, kv_spec, kv_spec, pl.BlockSpec((4, tq, 1), lambda kv, i: (kv, i, 0)), q_spec]
                 + [_ANY] * n_in,
        out_specs=(q_spec, kv_spec, kv_spec, *([_ANY] * na)),
        scratch_shapes=_comm_scratch(na) if na else [],
        input_output_aliases={5 + na + i: 3 + a for i, a in enumerate(held)},
        compiler_params=_cp(("arbitrary", "arbitrary") if na else ("parallel", "arbitrary")),
    )(q, k4, v4, lse, do, *comm_in)
    return res[0], res[1], res[2], list(res[3:])


WIN_Q = 2 * BLOCK
WIN_KEYS = WIN_Q + 2 * WINDOW


def _win_start(n, S):
    return pl.multiple_of(jnp.clip(n * WIN_Q - WINDOW, 0, S - WIN_KEYS), BLOCK)


def _win_valid(n, start):
    qpos = n * WIN_Q + lax.broadcasted_iota(jnp.int32, (WIN_Q, WIN_KEYS), 0)
    kpos = start + lax.broadcasted_iota(jnp.int32, (WIN_Q, WIN_KEYS), 1)
    return jnp.abs(qpos - kpos) <= WINDOW


def _attn_win_fwd(q, k4, v4, sink, *, name):
    S = q.shape[0]
    assert S >= WIN_KEYS

    def body(sink_ref, q_ref, k_ref, v_ref, o_ref, lse_ref):
        n = pl.program_id(0)
        start = _win_start(n, S)
        valid = _win_valid(n, start)
        for kv in range(2):
            for pr in range(2):
                cols = slice((kv * 2 + pr) * LANES, (kv * 2 + pr + 1) * LANES)
                qp = q_ref[:, cols]
                acc = None
                for half in range(2):
                    h = kv * 4 + pr * 2 + half
                    kk = k_ref[kv * 2 + half, pl.ds(start, WIN_KEYS), :]
                    vv = v_ref[kv * 2 + half, pl.ds(start, WIN_KEYS), :]
                    s = jnp.where(valid, _dot(qp, kk, _NT), NEG_BIG)
                    snk = sink_ref[h]
                    m = jnp.maximum(jnp.max(s, axis=-1, keepdims=True), snk)
                    e = jnp.exp(s - m)
                    l = jnp.sum(e, axis=-1, keepdims=True) + jnp.exp(snk - m)
                    pv = _dot(e.astype(BF16), vv) * (1.0 / l)
                    acc = pv if acc is None else acc + pv
                    lse_ref[h] = m + jnp.log(l)
                o_ref[:, cols] = acc.astype(BF16)

    kv_spec = pl.BlockSpec((4, S, LANES), lambda n: (0, 0, 0))
    return pl.pallas_call(
        body, name=name,
        out_shape=(jax.ShapeDtypeStruct((S, 512), BF16), jax.ShapeDtypeStruct((8, S, 1), F32)),
        grid=(S // WIN_Q,),
        in_specs=[pl.BlockSpec(memory_space=pltpu.SMEM), pl.BlockSpec((WIN_Q, 512), lambda n: (n, 0)),
                  kv_spec, kv_spec],
        out_specs=(pl.BlockSpec((WIN_Q, 512), lambda n: (n, 0)), pl.BlockSpec((8, WIN_Q, 1), lambda n: (0, n, 0))),
        compiler_params=_cp(("parallel",)),
    )(sink, q, k4, v4)


def _attn_win_bwd(q, k4, v4, sink, lse, do, *, name):
    S = q.shape[0]

    def body(sink_ref, q_ref, k_ref, v_ref, lse_ref, do_ref, dq_ref, dk_ref, dv_ref, dsink_ref):
        n = pl.program_id(0)

        @pl.when(n == 0)
        def _():
            dk_ref[...] = jnp.zeros_like(dk_ref)
            dv_ref[...] = jnp.zeros_like(dv_ref)
            dsink_ref[...] = jnp.zeros_like(dsink_ref)

        start = _win_start(n, S)
        valid = _win_valid(n, start)
        lane = lax.broadcasted_iota(jnp.int32, (WIN_Q, LANES), 1)
        for kv in range(2):
            for pr in range(2):
                cols = slice((kv * 2 + pr) * LANES, (kv * 2 + pr + 1) * LANES)
                qp = q_ref[:, cols]
                dop = do_ref[:, cols].astype(BF16)
                dq = None
                for half in range(2):
                    h = kv * 4 + pr * 2 + half
                    slot = kv * 2 + half
                    mine = (lane < HEAD_DIM) if half == 0 else (lane >= HEAD_DIM)
                    win = pl.ds(start, WIN_KEYS)
                    kk = k_ref[slot, win, :]
                    vv = v_ref[slot, win, :]
                    lse_h = lse_ref[h]
                    s = jnp.where(valid, _dot(qp, kk, _NT), NEG_BIG)
                    p = jnp.exp(s - lse_h)
                    dp = _dot(dop, vv, _NT)
                    delta = jnp.sum(p * dp, axis=-1, keepdims=True)
                    ds = (p * (dp - delta)).astype(BF16)
                    pb = p.astype(BF16)
                    d = _dot(ds, kk)
                    dq = d if dq is None else dq + d
                    dk_ref[slot, win, :] += _dot(ds, jnp.where(mine, qp, jnp.zeros_like(qp)), _TN)
                    dv_ref[slot, win, :] += _dot(pb, jnp.where(mine, dop, jnp.zeros_like(dop)), _TN)
                    p_sink = jnp.exp(sink_ref[h] - lse_h)
                    dsink_ref[h:h + 1, :] += jnp.broadcast_to(-jnp.sum(p_sink * delta, axis=0, keepdims=True),
                                                              (1, LANES))
                dq_ref[:, cols] = dq

    kv_spec = pl.BlockSpec((4, S, LANES), lambda n: (0, 0, 0))
    q_spec = pl.BlockSpec((WIN_Q, 512), lambda n: (n, 0))
    return pl.pallas_call(
        body, name=name,
        out_shape=(jax.ShapeDtypeStruct((S, 512), F32), jax.ShapeDtypeStruct((4, S, LANES), F32),
                   jax.ShapeDtypeStruct((4, S, LANES), F32), jax.ShapeDtypeStruct((8, LANES), F32)),
        grid=(S // WIN_Q,),
        in_specs=[pl.BlockSpec(memory_space=pltpu.SMEM), q_spec, kv_spec, kv_spec,
                  pl.BlockSpec((8, WIN_Q, 1), lambda n: (0, n, 0)), q_spec],
        out_specs=(q_spec, kv_spec, kv_spec, pl.BlockSpec((8, LANES), lambda n: (0, 0))),
        compiler_params=_cp(("arbitrary",)),
    )(sink, q, k4, v4, lse, do)


def _c_ln(v, g, b):
    mu = jnp.mean(v, axis=-1, keepdims=True)
    vc = v - mu
    r = lax.rsqrt(jnp.mean(vc * vc, axis=-1, keepdims=True) + LN_EPS)
    vh = vc * r
    return vh, r, vh * g + b


def _gmlp_blocks(rows):
    return [(slice(c * CHUNK, (c + 1) * CHUNK), slice(gi * LANES, (gi + 1) * LANES), gi)
            for c in range(rows // CHUNK) for gi in range(C_GROUPS)]


def _gmlp_fwd(proj, ws, bs3, lg, lb, *, name):
    S = proj.shape[0]
    rows = _tile(S, ROW_TILE)

    def body(u_ref, v_ref, ws_ref, bs_ref, lg_ref, lb_ref, o_ref):
        u = _gelu(u_ref[...])
        _, _, vn = _c_ln(_gelu(v_ref[...]), lg_ref[...], lb_ref[...])
        vn = vn.astype(BF16)
        for r, cols, gi in _gmlp_blocks(rows):
            mixed = _dot(ws_ref[gi], vn[r, cols]) + bs_ref[gi]
            o_ref[r, cols] = (u[r, cols] * mixed).astype(BF16)

    full2 = lambda n: (0, 0)
    full3 = lambda n: (0, 0, 0)
    return pl.pallas_call(
        body, name=name,
        out_shape=jax.ShapeDtypeStruct((S, C_WIDTH), BF16),
        grid=(S // rows,),
        in_specs=[pl.BlockSpec((rows, C_WIDTH), lambda n: (n, COL_C // C_WIDTH)),
                  pl.BlockSpec((rows, C_WIDTH), lambda n: (n, COL_C // C_WIDTH + 1)),
                  pl.BlockSpec((C_GROUPS, CHUNK, CHUNK), full3), pl.BlockSpec((C_GROUPS, CHUNK, 1), full3),
                  pl.BlockSpec((1, C_WIDTH), full2), pl.BlockSpec((1, C_WIDTH), full2)],
        out_specs=pl.BlockSpec((rows, C_WIDTH), lambda n: (n, 0)),
        compiler_params=_cp(("parallel",)),
    )(proj, proj, ws, bs3, lg, lb)


def _gmlp_bwd(proj, dout, ws, bs3, lg, lb, *, name):
    S = proj.shape[0]
    rows = _tile(S, ROW_TILE)
    nch = rows // CHUNK

    def body(u_ref, v_ref, d_ref, ws_ref, bs_ref, lg_ref, lb_ref, dz_ref, dws_ref, dbs_ref, dlg_ref, dlb_ref):
        @pl.when(pl.program_id(0) == 0)
        def _():
            dws_ref[...] = jnp.zeros_like(dws_ref)
            dbs_ref[...] = jnp.zeros_like(dbs_ref)
            dlg_ref[...] = jnp.zeros_like(dlg_ref)
            dlb_ref[...] = jnp.zeros_like(dlb_ref)

        u_pre, v_pre, d = u_ref[...], v_ref[...], d_ref[...]
        u, u_grad = _gelu_and_grad(u_pre)
        v, v_grad = _gelu_and_grad(v_pre)
        vh, r, vn = _c_ln(v, lg_ref[...], lb_ref[...])
        vnb = vn.astype(BF16)
        dm = d * u
        dvn_parts = []
        dws = [None] * C_GROUPS
        dbs = [None] * C_GROUPS
        for rs, cols, gi in _gmlp_blocks(rows):
            mixed = _dot(ws_ref[gi], vnb[rs, cols]) + bs_ref[gi]
            dz_ref[rs, cols] = (d[rs, cols] * mixed * u_grad[rs, cols]).astype(BF16)
            dmb = dm[rs, cols].astype(BF16)
            t, b = _dot(dmb, vnb[rs, cols], _NT), jnp.sum(dm[rs, cols], axis=-1, keepdims=True)
            dws[gi] = t if dws[gi] is None else dws[gi] + t
            dbs[gi] = b if dbs[gi] is None else dbs[gi] + b
            dvn_parts.append(_dot(ws_ref[gi], dmb, _TN))
        for gi in range(C_GROUPS):
            dws_ref[gi] += dws[gi]
            dbs_ref[gi] += dbs[gi]
        dvn = jnp.concatenate([jnp.concatenate(dvn_parts[c * C_GROUPS:(c + 1) * C_GROUPS], axis=-1)
                               for c in range(nch)], axis=0)
        dlg_ref[...] += jnp.sum(dvn * vh, axis=0, keepdims=True)
        dlb_ref[...] += jnp.sum(dvn, axis=0, keepdims=True)
        dvh = dvn * lg_ref[...]
        m1 = jnp.mean(dvh, axis=-1, keepdims=True)
        m2 = jnp.mean(dvh * vh, axis=-1, keepdims=True)
        dv = r * (dvh - m1 - vh * m2)
        dz_ref[:, C_WIDTH:] = (dv * v_grad).astype(BF16)

    full2 = lambda n: (0, 0)
    full3 = lambda n: (0, 0, 0)
    return pl.pallas_call(
        body, name=name,
        out_shape=(jax.ShapeDtypeStruct((S, 2 * C_WIDTH), BF16), jax.ShapeDtypeStruct((C_GROUPS, CHUNK, CHUNK), F32),
                   jax.ShapeDtypeStruct((C_GROUPS, CHUNK, 1), F32), jax.ShapeDtypeStruct((1, C_WIDTH), F32),
                   jax.ShapeDtypeStruct((1, C_WIDTH), F32)),
        grid=(S // rows,),
        in_specs=[pl.BlockSpec((rows, C_WIDTH), lambda n: (n, COL_C // C_WIDTH)),
                  pl.BlockSpec((rows, C_WIDTH), lambda n: (n, COL_C // C_WIDTH + 1)),
                  pl.BlockSpec((rows, C_WIDTH), lambda n: (n, 0)),
                  pl.BlockSpec((C_GROUPS, CHUNK, CHUNK), full3), pl.BlockSpec((C_GROUPS, CHUNK, 1), full3),
                  pl.BlockSpec((1, C_WIDTH), full2), pl.BlockSpec((1, C_WIDTH), full2)],
        out_specs=(pl.BlockSpec((rows, 2 * C_WIDTH), lambda n: (n, 0)), pl.BlockSpec((C_GROUPS, CHUNK, CHUNK), full3),
                   pl.BlockSpec((C_GROUPS, CHUNK, 1), full3), pl.BlockSpec((1, C_WIDTH), full2),
                   pl.BlockSpec((1, C_WIDTH), full2)),
        compiler_params=_cp(("arbitrary",)),
    )(proj, proj, dout, ws, bs3, lg, lb)


GATE_BLK = 512


def _gate_specs(tm, D):
    nh = D // GATE_BLK
    first = COL_GATE // GATE_BLK
    return [pl.BlockSpec((tm, GATE_BLK), functools.partial(lambda i, c: (i, c), c=first + b))
            for b in range(N_BRANCH * nh)]


def _merge_fwd(oa, ob, oc, wb, proj, bg, *, name):
    S = oa.shape[0]
    D = wb.shape[2]
    assert D % GATE_BLK == 0
    nh = D // GATE_BLK
    tm = _tile(S, ROW_TILE)

    def body(oa_ref, ob_ref, oc_ref, wb_ref, *rest):
        gate_refs, bg_ref, o_ref = rest[:N_BRANCH * nh], rest[N_BRANCH * nh], rest[N_BRANCH * nh + 1]
        brs = (oa_ref[...], ob_ref[...], oc_ref[...])
        for j in range(nh):
            cols = slice(j * GATE_BLK, (j + 1) * GATE_BLK)
            acc = None
            for n in range(N_BRANCH):
                b = n * nh + j
                t = _dot(brs[n], wb_ref[n, :, cols])
                g = _sigmoid(gate_refs[b][...] + bg_ref[:, b * GATE_BLK:(b + 1) * GATE_BLK])
                acc = t * g if acc is None else acc + t * g
            o_ref[:, cols] = acc.astype(BF16)

    row = lambda i: (i, 0)
    br = pl.BlockSpec((tm, BRANCH_WIDTH), row)
    return pl.pallas_call(
        body, name=name,
        out_shape=jax.ShapeDtypeStruct((S, D), BF16),
        grid=(S // tm,),
        in_specs=[br, br, br, pl.BlockSpec((N_BRANCH, BRANCH_WIDTH, D), lambda i: (0, 0, 0))]
                 + _gate_specs(tm, D) + [pl.BlockSpec((1, N_BRANCH * D), lambda i: (0, 0))],
        out_specs=pl.BlockSpec((tm, D), row),
        compiler_params=_cp(("parallel",)),
    )(oa, ob, oc, wb, *([proj] * (N_BRANCH * nh)), bg)


def _merge_bwd(oa, ob, oc, wb, proj, bg, dmerged, *, name):
    S = oa.shape[0]
    D = wb.shape[2]
    nh = D // GATE_BLK
    tm = _tile(S, ROW_TILE)

    def body(oa_ref, ob_ref, oc_ref, wb_ref, *rest):
        gate_refs = rest[:N_BRANCH * nh]
        bg_ref, dm_ref, dgl_ref, dbg_ref = rest[N_BRANCH * nh:N_BRANCH * nh + 4]
        dt_refs = rest[N_BRANCH * nh + 4:N_BRANCH * nh + 4 + N_BRANCH]
        dbr_refs = rest[N_BRANCH * nh + 4 + N_BRANCH:]

        @pl.when(pl.program_id(0) == 0)
        def _():
            dbg_ref[...] = jnp.zeros_like(dbg_ref)

        brs = (oa_ref[...], ob_ref[...], oc_ref[...])
        for n in range(N_BRANCH):
            dbr = None
            for j in range(nh):
                cols = slice(j * GATE_BLK, (j + 1) * GATE_BLK)
                b = n * nh + j
                gcols = slice(b * GATE_BLK, (b + 1) * GATE_BLK)
                w = wb_ref[n, :, cols]
                t = _dot(brs[n], w)
                g = _sigmoid(gate_refs[b][...] + bg_ref[:, gcols])
                dm = dm_ref[:, cols]
                dt = (dm * g).astype(BF16)
                dgl = dm * t * g * (1.0 - g)
                dt_refs[n][:, cols] = dt
                dgl_ref[:, gcols] = dgl.astype(BF16)
                dbg_ref[:, gcols] += jnp.sum(dgl, axis=0, keepdims=True)
                d = _dot(dt, w, _NT)
                dbr = d if dbr is None else dbr + d
            dbr_refs[n][...] = dbr

    row = lambda i: (i, 0)
    br = pl.BlockSpec((tm, BRANCH_WIDTH), row)
    res = pl.pallas_call(
        body, name=name,
        out_shape=(jax.ShapeDtypeStruct((S, N_BRANCH * D), BF16), jax.ShapeDtypeStruct((1, N_BRANCH * D), F32),
                   *([jax.ShapeDtypeStruct((S, D), BF16)] * N_BRANCH),
                   *([jax.ShapeDtypeStruct((S, BRANCH_WIDTH), F32)] * N_BRANCH)),
        grid=(S // tm,),
        in_specs=[br, br, br, pl.BlockSpec((N_BRANCH, BRANCH_WIDTH, D), lambda i: (0, 0, 0))]
                 + _gate_specs(tm, D)
                 + [pl.BlockSpec((1, N_BRANCH * D), lambda i: (0, 0)), pl.BlockSpec((tm, D), row)],
        out_specs=(pl.BlockSpec((tm, N_BRANCH * D), row), pl.BlockSpec((1, N_BRANCH * D), lambda i: (0, 0)),
                   *([pl.BlockSpec((tm, D), row)] * N_BRANCH), *([br] * N_BRANCH)),
        compiler_params=_cp(("arbitrary",)),
    )(oa, ob, oc, wb, *([proj] * (N_BRANCH * nh)), bg, dmerged)
    return res[0], res[1], list(res[2:2 + N_BRANCH]), list(res[2 + N_BRANCH:])


X_SCALE = 1.0 / math.sqrt(X_HEAD_DIM)
X_W = X_HEADS * X_HEAD_DIM


def _xattn_fwd(q, kv, *, name):
    S = q.shape[0]
    M = kv.shape[0]
    tq = _tile(S, 512)

    def body(q_ref, kv_ref, o_ref, lse_ref):
        for h in range(X_HEADS):
            cols = slice(h * LANES, (h + 1) * LANES)
            s = _dot(q_ref[:, cols], kv_ref[:, cols], _NT) * X_SCALE
            m = jnp.max(s, axis=-1, keepdims=True)
            e = jnp.exp(s - m)
            l = jnp.sum(e, axis=-1, keepdims=True)
            p = (e * (1.0 / l)).astype(BF16)
            o_ref[:, cols] = _dot(p, kv_ref[:, X_W + h * LANES:X_W + (h + 1) * LANES]).astype(BF16)
            lse_ref[h] = m + jnp.log(l)

    return pl.pallas_call(
        body, name=name,
        out_shape=(jax.ShapeDtypeStruct((S, X_W), BF16), jax.ShapeDtypeStruct((X_HEADS, S, 1), F32)),
        grid=(S // tq,),
        in_specs=[pl.BlockSpec((tq, X_W), lambda i: (i, 0)), pl.BlockSpec((M, 2 * X_W), lambda i: (0, 0))],
        out_specs=(pl.BlockSpec((tq, X_W), lambda i: (i, 0)), pl.BlockSpec((X_HEADS, tq, 1), lambda i: (0, i, 0))),
        compiler_params=_cp(("parallel",)),
    )(q, kv)


def _xattn_bwd(q, kv, lse, do, *, name):
    S = q.shape[0]
    M = kv.shape[0]
    tq = _tile(S, 512)

    def body(q_ref, kv_ref, lse_ref, do_ref, dq_ref, dkv_ref):
        @pl.when(pl.program_id(0) == 0)
        def _():
            dkv_ref[...] = jnp.zeros_like(dkv_ref)

        for h in range(X_HEADS):
            cols = slice(h * LANES, (h + 1) * LANES)
            vcols = slice(X_W + h * LANES, X_W + (h + 1) * LANES)
            qh, kh, vh = q_ref[:, cols], kv_ref[:, cols], kv_ref[:, vcols]
            doh = do_ref[:, cols].astype(BF16)
            p = jnp.exp(_dot(qh, kh, _NT) * X_SCALE - lse_ref[h])
            dp = _dot(doh, vh, _NT)
            delta = jnp.sum(p * dp, axis=-1, keepdims=True)
            ds = (p * (dp - delta) * X_SCALE).astype(BF16)
            dq_ref[:, cols] = _dot(ds, kh).astype(BF16)
            dkv_ref[:, cols] += _dot(ds, qh, _TN)
            dkv_ref[:, vcols] += _dot(p.astype(BF16), doh, _TN)

    q_spec = pl.BlockSpec((tq, X_W), lambda i: (i, 0))
    return pl.pallas_call(
        body, name=name,
        out_shape=(jax.ShapeDtypeStruct((S, X_W), BF16), jax.ShapeDtypeStruct((M, 2 * X_W), F32)),
        grid=(S // tq,),
        in_specs=[q_spec, pl.BlockSpec((M, 2 * X_W), lambda i: (0, 0)),
                  pl.BlockSpec((X_HEADS, tq, 1), lambda i: (0, i, 0)), q_spec],
        out_specs=(q_spec, pl.BlockSpec((M, 2 * X_W), lambda i: (0, 0))),
        compiler_params=_cp(("arbitrary",)),
    )(q, kv, lse, do)


def _shift_down(h, row):
    return jnp.where(row == 0, 0.0, pltpu.roll(h, 1, 0))


def _shift_up(h, row, S):
    return jnp.where(row == S - 1, 0.0, pltpu.roll(h, S - 1, 0))


def _conv3(h, ck, cb, row, S):
    return _shift_down(h, row) * ck[0:1] + h * ck[1:2] + _shift_up(h, row, S) * ck[2:3] + cb


def _conv_act_fwd(h, ck, cb, *, name):
    S, F2 = h.shape
    F = F2 // 2
    nt = F // LANES

    def body(ha_ref, hb_ref, cka_ref, ckb_ref, cba_ref, cbb_ref, o_ref):
        row = lax.broadcasted_iota(jnp.int32, (S, LANES), 0)
        a = _conv3(ha_ref[...], cka_ref[...], cba_ref[...], row, S)
        b = _conv3(hb_ref[...], ckb_ref[...], cbb_ref[...], row, S)
        o_ref[...] = (_gelu(a) * b).astype(BF16)

    ca = lambda j: (0, j)
    cbi = lambda j: (0, j + nt)
    return pl.pallas_call(
        body, name=name,
        out_shape=jax.ShapeDtypeStruct((S, F), BF16),
        grid=(nt,),
        in_specs=[pl.BlockSpec((S, LANES), ca), pl.BlockSpec((S, LANES), cbi), pl.BlockSpec((3, LANES), ca),
                  pl.BlockSpec((3, LANES), cbi), pl.BlockSpec((1, LANES), ca), pl.BlockSpec((1, LANES), cbi)],
        out_specs=pl.BlockSpec((S, LANES), ca),
        compiler_params=_cp(("parallel",)),
    )(h, h, ck, ck, cb, cb)


def _conv_act_bwd(h, ck, cb, dact, *, name):
    S, F2 = h.shape
    F = F2 // 2
    nt = F // LANES

    def body(ha_ref, hb_ref, cka_ref, ckb_ref, cba_ref, cbb_ref, d_ref,
             dha_ref, dhb_ref, dcka_ref, dckb_ref, dcba_ref, dcbb_ref):
        row = lax.broadcasted_iota(jnp.int32, (S, LANES), 0)
        ha, hb = ha_ref[...], hb_ref[...]
        cka, ckb = cka_ref[...], ckb_ref[...]
        a = _conv3(ha, cka, cba_ref[...], row, S)
        b = _conv3(hb, ckb, cbb_ref[...], row, S)
        d = d_ref[...]
        ga, ga_grad = _gelu_and_grad(a)
        da = d * b * ga_grad
        db = d * ga
        for dd, hh, ck_, dh_ref, dck_ref, dcb_ref in ((da, ha, cka, dha_ref, dcka_ref, dcba_ref),
                                                      (db, hb, ckb, dhb_ref, dckb_ref, dcbb_ref)):
            dcb_ref[...] = jnp.sum(dd, axis=0, keepdims=True)
            dck_ref[0:1, :] = jnp.sum(dd * _shift_down(hh, row), axis=0, keepdims=True)
            dck_ref[1:2, :] = jnp.sum(dd * hh, axis=0, keepdims=True)
            dck_ref[2:3, :] = jnp.sum(dd * _shift_up(hh, row, S), axis=0, keepdims=True)
            dh = _shift_up(dd, row, S) * ck_[0:1] + dd * ck_[1:2] + _shift_down(dd, row) * ck_[2:3]
            dh_ref[...] = dh.astype(BF16)

    ca = lambda j: (0, j)
    cbi = lambda j: (0, j + nt)
    col = pl.BlockSpec((S, LANES), ca)
    return pl.pallas_call(
        body, name=name,
        out_shape=(jax.ShapeDtypeStruct((S, F), BF16), jax.ShapeDtypeStruct((S, F), BF16),
                   jax.ShapeDtypeStruct((3, F), F32), jax.ShapeDtypeStruct((3, F), F32),
                   jax.ShapeDtypeStruct((1, F), F32), jax.ShapeDtypeStruct((1, F), F32)),
        grid=(nt,),
        in_specs=[col, pl.BlockSpec((S, LANES), cbi), pl.BlockSpec((3, LANES), ca), pl.BlockSpec((3, LANES), cbi),
                  pl.BlockSpec((1, LANES), ca), pl.BlockSpec((1, LANES), cbi), col],
        out_specs=(col, col, pl.BlockSpec((3, LANES), ca), pl.BlockSpec((3, LANES), ca),
                   pl.BlockSpec((1, LANES), ca), pl.BlockSpec((1, LANES), ca)),
        compiler_params=_cp(("parallel",)),
    )(h, h, ck, ck, cb, cb, dact)


def _layer_fwd(x, xb, memb, w, shards, tabs, seg, l):
    n = lambda s: f"L{l}_{s}"
    w = dict(w)
    qg2 = jnp.tile(w["b_q_gain"], 2)[None, :]
    kg2 = jnp.tile(w["b_k_gain"], 2)[None, :]
    proj = _mm(xb, w["w_in"], tb=True, name=n("proj"))
    aq, ak4, av4, bq, bk4, bv4 = _prep(proj, tabs, qg2, kg2, seg, name=n("prep"))
    oa, lse_a = _attn_win_fwd(aq, ak4, av4, w["a_sink"], name=n("attn_win"))
    gather = [(shards[k], l) for k in GATHERED_LATE] + ([(shards["w_in"], l + 1)] if l + 1 < DEPTH else [])
    ob, lse_b, gathered = _attn_dense_fwd(bq, bk4, bv4, gather=gather, name=n("attn_dense"))
    for k, g in zip(GATHERED_LATE, gathered):
        w[k] = _unshard(g, GATHER_AXIS[k])
    w_in_next = gathered[len(GATHERED_LATE)] if l + 1 < DEPTH else None
    oc = _gmlp_fwd(proj, w["c_ws"], w["c_bs3"], w["c_ln_g"], w["c_ln_b"], name=n("gmlp"))
    merged = _merge_fwd(oa, ob, oc, w["w_branch"], proj, w["b_gate"], name=n("merge"))
    x1, x1b, xh1, rs1 = _mm_res_ln(merged, w["w_mix_out"], x, w["ln1_g"], w["ln1_b"], name=n("mix_ln1"))
    xq = _mm(x1b, w["x_wq"], out_dtype=BF16, name=n("xq"))
    xkv = _mm(memb, w["x_wkv"], out_dtype=BF16, name=n("xkv"))
    xo, lse_x = _xattn_fwd(xq, xkv, name=n("xattn"))
    x2, x2b, xh2, rs2 = _mm_res_ln(xo, w["x_wo"], x1, w["ln2_g"], w["ln2_b"], name=n("xo_ln2"))
    h = _mm(x2b, w["f_w_up"], tb=True, name=n("ffn_up"))
    act = _conv_act_fwd(h, w["f_conv_k"], w["f_conv_b"], name=n("conv_act"))
    x3, x3b, xh3, rs3 = _mm_res_ln(act, w["f_w_down"], x2, w["ln3_g"], w["ln3_b"], name=n("down_ln3"))
    saved = dict(xb=xb, proj=proj, aq=aq, ak4=ak4, av4=av4, bq=bq, bk4=bk4, bv4=bv4, lse_a=lse_a, lse_b=lse_b,
                 oa=oa, ob=ob, oc=oc, merged=merged, xh1=xh1, rs1=rs1, x1b=x1b, xq=xq, xkv=xkv, xo=xo, lse_x=lse_x,
                 xh2=xh2, rs2=rs2, x2b=x2b, h=h, act=act, xh3=xh3, rs3=rs3, qg2=qg2, kg2=kg2)
    return x3, x3b, saved, w, w_in_next


def _layer_bwd(top, memb, w, sv, tabs, seg, l, ln_below, dw_in_above, recv):
    n = lambda s: f"L{l}_{s}"
    g = {}
    big = {}
    recv = dict(recv)

    def dw(key, a, b, tag):
        big[key] = _mm(a, b, ta=True, out_dtype=BF16, name=n(tag))

    def dw_t(key, segments, x, tag):
        buf = jax.ShapeDtypeStruct((sum(s.shape[1] for s in segments), x.shape[1]), BF16)
        row = 0
        for i, s in enumerate(segments):
            buf = _mm(s, x, ta=True, into=(buf, row), name=n(f"{tag}{i}"))
            row += s.shape[1]
        big[key] = buf

    dz3, dz3b, g["ln3_g"], g["ln3_b"] = top
    dw("f_w_down", sv["act"], dz3b, "dw_down")
    dact = _mm(dz3b, w["f_w_down"], tb=True, name=n("dact"))
    dha, dhb, dcka, dckb, dcba, dcbb = _conv_act_bwd(sv["h"], w["f_conv_k"], w["f_conv_b"], dact, name=n("conv_act_bwd"))
    g["f_conv_k"] = jnp.concatenate([dcka, dckb], axis=1)
    g["f_conv_b"] = jnp.concatenate([dcba, dcbb], axis=1)[0]
    dw_t("f_w_up", [dha, dhb], sv["x2b"], "dw_up")
    dz2, dz2b, g["ln2_g"], g["ln2_b"] = _mm([dha, dhb], w["f_w_up"], res=dz3, res_scale=ALPHA,
                                            ln_bwd=(sv["xh2"], sv["rs2"], w["ln2_g"]), name=n("dx2_ln2"))
    dw("x_wo", sv["xo"], dz2b, "dw_xo")
    dxo = _mm(dz2b, w["x_wo"], tb=True, out_dtype=BF16, name=n("dxo"))
    dxq, dxkv = _xattn_bwd(sv["xq"], sv["xkv"], sv["lse_x"], dxo, name=n("xattn_bwd"))
    dw("x_wq", sv["x1b"], dxq, "dw_xq")
    dw("x_wkv", memb, dxkv, "dw_xkv")
    dz1, dz1b, g["ln1_g"], g["ln1_b"] = _mm(dxq, w["x_wq"], tb=True, res=dz2, res_scale=ALPHA,
                                            ln_bwd=(sv["xh1"], sv["rs1"], w["ln1_g"]), name=n("dx1_ln1"))
    dw("w_mix_out", sv["merged"], dz1b, "dw_mix")
    dmerged = _mm(dz1b, w["w_mix_out"], tb=True, name=n("dmerged"))
    dgl, dbg, dt, dbr = _merge_bwd(sv["oa"], sv["ob"], sv["oc"], w["w_branch"], sv["proj"], w["b_gate"], dmerged,
                                   name=n("merge_bwd"))
    g["b_gate"] = dbg[0]
    for i, k in enumerate(("oa", "ob", "oc")):
        dw(f"w_branch{i}", sv[k], dt[i], f"dw_branch{i}")
    big["w_branch"] = jnp.stack([big.pop(f"w_branch{i}") for i in range(N_BRANCH)])
    dqa, dka, dva, dsink = _attn_win_bwd(sv["aq"], sv["ak4"], sv["av4"], w["a_sink"], sv["lse_a"], dbr[0],
                                         name=n("attn_win_bwd"))
    g["a_sink"] = dsink[:, 0]
    sent = [k for k in BIG if k != "w_in"]
    scatter = [(_reshard(big[k], BIG_AXIS[k]), recv[k], l) for k in sent]
    if dw_in_above is not None:
        sent.append("w_in")
        scatter.append((_reshard(dw_in_above, BIG_AXIS["w_in"]), recv["w_in"], l + 1))
    dqb, dkb, dvb, got = _attn_dense_bwd(sv["bq"], sv["bk4"], sv["bv4"], sv["lse_b"], dbr[1], scatter=scatter,
                                         name=n("attn_dense_bwd"))
    recv.update(zip(sent, got))
    dcz, g["c_ws"], dbs3, dlg, dlb = _gmlp_bwd(sv["proj"], dbr[2], w["c_ws"], w["c_bs3"], w["c_ln_g"], w["c_ln_b"],
                                               name=n("gmlp_bwd"))
    g["c_bs"] = dbs3[:, :, 0]
    g["c_ln_g"], g["c_ln_b"] = dlg[0], dlb[0]
    dqkv, dqg, dkg = _unprep(dqa, dka, dva, dqb, dkb, dvb, sv["proj"], tabs, sv["qg2"], sv["kg2"], seg, name=n("unprep"))
    g["b_q_gain"] = dqg[0, :HEAD_DIM] + dqg[0, HEAD_DIM:]
    g["b_k_gain"] = dkg[0, :HEAD_DIM] + dkg[0, HEAD_DIM:]
    dw_t("w_in", [dqkv, dcz, dgl], sv["xb"], "dw_in")
    dx0 = _mm([dqkv, dcz, dgl], w["w_in"], res=dz1, res_scale=ALPHA, name=n("dx0"))
    if ln_below is not None:
        dx0 = _ln_bwd(dx0, *ln_below, name=n("ln_bwd_below"))
    for k in ("ln1_g", "ln1_b", "ln2_g", "ln2_b", "ln3_g", "ln3_b"):
        g[k] = g[k][0]
    return dx0, g, big["w_in"], recv


WEIGHTS = ("w_in", "b_gate", "a_sink", "b_q_gain", "b_k_gain", "c_ln_g", "c_ln_b", "c_ws", "c_bs", "w_branch",
           "w_mix_out", "ln1_g", "ln1_b", "x_wq", "x_wkv", "x_wo", "ln2_g", "ln2_b", "f_w_up", "f_conv_k",
           "f_conv_b", "f_w_down", "ln3_g", "ln3_b")
TRANSPOSED = ("w_in", "f_w_up")
BIG_AXIS = {"w_in": 0, "w_branch": 2, "w_mix_out": 0, "x_wq": 0, "x_wkv": 0, "x_wo": 1, "f_w_up": 0, "f_w_down": 0}
BIG = tuple(BIG_AXIS)
GATHERED = BIG + ("f_conv_k",)
GATHERED_LATE = tuple(k for k in GATHERED if k != "w_in")
GATHER_AXIS = dict(BIG_AXIS, f_conv_k=1)
SMALL = tuple(k for k in WEIGHTS if k not in GATHERED)


def _unshard(g, axis):
    t = jnp.moveaxis(g, 0, axis)
    return t.reshape(t.shape[:axis] + (t.shape[axis] * t.shape[axis + 1],) + t.shape[axis + 2:])


def _reshard(full, axis):
    t = full.reshape(full.shape[:axis] + (N_DEV, full.shape[axis] // N_DEV) + full.shape[axis + 1:])
    return jnp.moveaxis(t, axis, 0)


def _small_weights(small, l):
    w = {k: v[l] for k, v in small.items()}
    for k in ("c_ln_g", "c_ln_b", "ln1_g", "ln1_b", "ln2_g", "ln2_b", "ln3_g", "ln3_b", "b_gate", "f_conv_b"):
        w[k] = w[k][None, :]
    w["c_bs3"] = w["c_bs"][:, :, None]
    w["c_ws"] = w["c_ws"].astype(BF16)
    return w


def _local_step(x, mem, target, small, shards):
    S = x.shape[0]
    tabs = _rope_tables(S)
    seg = _seg_matrix()
    memb = mem.astype(BF16)
    xb = x.astype(BF16)
    saved, weights = [], []
    w_in_g = _gather_call([(shards["w_in"], 0)], name="gather_w_in_L0")[0]
    for l in range(DEPTH):
        w = dict(_small_weights(small, l), w_in=_unshard(w_in_g, GATHER_AXIS["w_in"]))
        x, xb, sv, w, w_in_g = _layer_fwd(x, xb, memb, w, shards, tabs, seg, l)
        saved.append(sv)
        weights.append(w)
    dy, loss = _loss_head(x, target, name="loss_head")
    grads = [None] * DEPTH
    recv = {k: jax.ShapeDtypeStruct((DEPTH, N_DEV) + shards[k].shape[1:], BF16) for k in BIG}
    dw_in = None
    last_ln = lambda l: (saved[l]["xh3"], saved[l]["rs3"], weights[l]["ln3_g"])
    top = _ln_bwd(dy, *last_ln(DEPTH - 1), name="ln_bwd_top")
    for l in reversed(range(DEPTH)):
        top, grads[l], dw_in, recv = _layer_bwd(top, memb, weights[l], saved[l], tabs, seg, l,
                                                last_ln(l - 1) if l > 0 else None, dw_in, recv)
    recv["w_in"] = _scatter_call([(_reshard(dw_in, BIG_AXIS["w_in"]), recv["w_in"], 0)], name="scatter_w_in_L0")[0]
    return loss, top, grads, [recv[k] for k in BIG]


PACK_W = 1024


def _gather_call(gather, *, name):
    na = len(gather)

    def body(*refs):
        start, finish = _gather_plan([(refs[a], gather[a][1], refs[na + a]) for a in range(na)], *refs[2 * na:])
        start()
        finish()

    return list(pl.pallas_call(
        body, name=name,
        out_shape=[_gathered_shape(x) for x, _ in gather],
        in_specs=[_ANY] * na, out_specs=[_ANY] * na,
        scratch_shapes=_comm_scratch(na),
    )(*[x for x, _ in gather]))


def _scatter_io(scatter):
    held = [a for a, (_, r, _) in enumerate(scatter) if not isinstance(r, jax.ShapeDtypeStruct)]
    return ([s for s, _, _ in scatter] + [scatter[a][1] for a in held],
            [jax.ShapeDtypeStruct(r.shape, r.dtype) for _, r, _ in scatter], held)


def _scatter_call(scatter, *, name):
    na = len(scatter)
    operands, out_shape, held = _scatter_io(scatter)
    n_in = len(operands)

    def body(*refs):
        start, finish = _scatter_plan([(refs[a], refs[n_in + a], scatter[a][2]) for a in range(na)],
                                      *refs[n_in + na:])
        start()
        finish()

    return list(pl.pallas_call(
        body, name=name,
        out_shape=out_shape,
        in_specs=[_ANY] * n_in, out_specs=[_ANY] * na,
        scratch_shapes=_comm_scratch(na),
        input_output_aliases={na + i: a for i, a in enumerate(held)},
    )(*operands))


def _sum_parts(parts, *, name):
    P, R, C = parts.shape
    tr = _tile(R, 64, align=8)

    def body(p_ref, o_ref):
        g = p_ref[0].astype(F32)
        for s in range(1, P):
            g = g + p_ref[s].astype(F32)
        o_ref[...] = g

    return pl.pallas_call(
        body, name=name, out_shape=jax.ShapeDtypeStruct((R, C), F32), grid=(R // tr,),
        in_specs=[pl.BlockSpec((P, tr, C), lambda i: (0, i, 0))], out_specs=pl.BlockSpec((tr, C), lambda i: (i, 0)),
        compiler_params=_cp(("parallel",)),
    )(parts)


ADAM_BLOCK_ELEMS = 512 * 1024


def _adamw(parts, w, m, v, *, name):
    L, P, R, C = parts.shape
    assert w.shape == (L, R, C), (parts.shape, w.shape)
    tr = _tile(R, max(16, ADAM_BLOCK_ELEMS // C), align=16)

    def body(p_ref, w_ref, m_ref, v_ref, g_ref, d_ref, nm_ref, nv_ref):
        g = p_ref[0].astype(F32)
        for s in range(1, P):
            g = g + p_ref[s].astype(F32)
        nm = ADAM_B1 * m_ref[...] + (1.0 - ADAM_B1) * g
        nv = ADAM_B2 * v_ref[...] + (1.0 - ADAM_B2) * (g * g)
        m_hat = nm / (1.0 - ADAM_B1 ** ADAM_STEP)
        v_hat = nv / (1.0 - ADAM_B2 ** ADAM_STEP)
        g_ref[...] = g
        d_ref[...] = -ADAM_LR * (m_hat / (jnp.sqrt(v_hat) + ADAM_EPS) + ADAM_WD * w_ref[...])
        nm_ref[...] = nm
        nv_ref[...] = nv

    blk = pl.BlockSpec((None, tr, C), lambda l, i: (l, i, 0))
    shp = jax.ShapeDtypeStruct((L, R, C), F32)
    return pl.pallas_call(
        body, name=name, out_shape=(shp, shp, shp, shp), grid=(L, R // tr),
        in_specs=[pl.BlockSpec((None, P, tr, C), lambda l, i: (l, 0, i, 0)), blk, blk, blk],
        out_specs=(blk, blk, blk, blk),
        compiler_params=_cp(("parallel", "parallel")),
    )(parts, w, m, v)


def _pad_rows(vec, width, row_align):
    n = vec.shape[0]
    rows = -(-n // width)
    rows = -(-rows // row_align) * row_align
    return jnp.pad(vec, (0, rows * width - n)).reshape(rows, width)


def kernel(x, mem, w_in, b_gate, a_sink, b_q_gain, b_k_gain, c_ln_g, c_ln_b, c_ws, c_bs, w_branch, w_mix_out, ln1_g, ln1_b, x_wq, x_wkv, x_wo, ln2_g, ln2_b, f_w_up, f_conv_k, f_conv_b, f_w_down, ln3_g, ln3_b, loss_target, m_w_in, m_b_gate, m_a_sink, m_b_q_gain, m_b_k_gain, m_c_ln_g, m_c_ln_b, m_c_ws, m_c_bs, m_w_branch, m_w_mix_out, m_ln1_g, m_ln1_b, m_x_wq, m_x_wkv, m_x_wo, m_ln2_g, m_ln2_b, m_f_w_up, m_f_conv_k, m_f_conv_b, m_f_w_down, m_ln3_g, m_ln3_b, v_w_in, v_b_gate, v_a_sink, v_b_q_gain, v_b_k_gain, v_c_ln_g, v_c_ln_b, v_c_ws, v_c_bs, v_w_branch, v_w_mix_out, v_ln1_g, v_ln1_b, v_x_wq, v_x_wkv, v_x_wo, v_ln2_g, v_ln2_b, v_f_w_up, v_f_conv_k, v_f_conv_b, v_f_w_down, v_ln3_g, v_ln3_b):
    w = dict(w_in=w_in, b_gate=b_gate, a_sink=a_sink, b_q_gain=b_q_gain, b_k_gain=b_k_gain, c_ln_g=c_ln_g,
             c_ln_b=c_ln_b, c_ws=c_ws, c_bs=c_bs, w_branch=w_branch, w_mix_out=w_mix_out, ln1_g=ln1_g, ln1_b=ln1_b,
             x_wq=x_wq, x_wkv=x_wkv, x_wo=x_wo, ln2_g=ln2_g, ln2_b=ln2_b, f_w_up=f_w_up, f_conv_k=f_conv_k,
             f_conv_b=f_conv_b, f_w_down=f_w_down, ln3_g=ln3_g, ln3_b=ln3_b)
    m = dict(w_in=m_w_in, b_gate=m_b_gate, a_sink=m_a_sink, b_q_gain=m_b_q_gain, b_k_gain=m_b_k_gain,
             c_ln_g=m_c_ln_g, c_ln_b=m_c_ln_b, c_ws=m_c_ws, c_bs=m_c_bs, w_branch=m_w_branch, w_mix_out=m_w_mix_out,
             ln1_g=m_ln1_g, ln1_b=m_ln1_b, x_wq=m_x_wq, x_wkv=m_x_wkv, x_wo=m_x_wo, ln2_g=m_ln2_g, ln2_b=m_ln2_b,
             f_w_up=m_f_w_up, f_conv_k=m_f_conv_k, f_conv_b=m_f_conv_b, f_w_down=m_f_w_down, ln3_g=m_ln3_g,
             ln3_b=m_ln3_b)
    v = dict(w_in=v_w_in, b_gate=v_b_gate, a_sink=v_a_sink, b_q_gain=v_b_q_gain, b_k_gain=v_b_k_gain,
             c_ln_g=v_c_ln_g, c_ln_b=v_c_ln_b, c_ws=v_c_ws, c_bs=v_c_bs, w_branch=v_w_branch, w_mix_out=v_w_mix_out,
             ln1_g=v_ln1_g, ln1_b=v_ln1_b, x_wq=v_x_wq, x_wkv=v_x_wkv, x_wo=v_x_wo, ln2_g=v_ln2_g, ln2_b=v_ln2_b,
             f_w_up=v_f_w_up, f_conv_k=v_f_conv_k, f_conv_b=v_f_conv_b, f_w_down=v_f_w_down, ln3_g=v_ln3_g,
             ln3_b=v_ln3_b)
    me = 4 * lax.axis_index("x") + 2 * lax.axis_index("y") + lax.axis_index("c")

    def held(k, t):
        return jnp.swapaxes(t, 1, 2) if k in TRANSPOSED else t

    shards = dict({k: held(k, w[k]).astype(BF16) for k in BIG}, f_conv_k=w["f_conv_k"])
    loss, grad_x, grads, recvs = _local_step(x[0], mem[0], loss_target[0], {k: w[k] for k in SMALL}, shards)
    loss = lax.psum(loss[0, 0], ("x", "y", "c"))

    out_g, out_d, out_m, out_v = {}, {}, {}, {}
    for k, recv in zip(BIG, recvs):
        shp = held(k, w[k]).shape
        rc = (DEPTH, math.prod(shp[1:-1]), shp[-1])
        parts = recv.reshape((DEPTH, N_DEV) + rc[1:])
        g_, d_, m_, v_ = _adamw(parts, held(k, w[k]).reshape(rc), held(k, m[k]).reshape(rc),
                                held(k, v[k]).reshape(rc), name=f"adamw_{k}")
        out_g[k], out_d[k], out_m[k], out_v[k] = (held(k, t.reshape(shp)) for t in (g_, d_, m_, v_))

    small_all = SMALL + ("f_conv_k",)
    gfull = {k: jnp.stack([grads[l][k] for l in range(DEPTH)]) for k in small_all}

    def pack(d):
        return jnp.concatenate([_pad_rows(d[k].reshape(-1), PACK_W, 8) for k in small_all])

    def unpack(rows, like):
        out, r = {}, 0
        for k in small_all:
            nr = -(-like[k].size // (8 * PACK_W)) * 8
            out[k] = rows[r:r + nr].reshape(-1)[:like[k].size].reshape(like[k].shape)
            r += nr
        return out

    gathered = _gather_call([(pack(gfull)[None], 0)], name="gather_small_grads")[0]
    sg = unpack(_sum_parts(gathered, name="sum_small_grads"), gfull)
    width = w["f_conv_k"].shape[2]
    sg["f_conv_k"] = lax.dynamic_slice_in_dim(sg["f_conv_k"], me * width, width, axis=2)
    g_, d_, m_, v_ = _adamw(pack(sg)[None, None], pack(w)[None], pack(m)[None], pack(v)[None], name="adamw_small")
    ud, um, uv = (unpack(t[0], w) for t in (d_, m_, v_))
    for k in small_all:
        out_g[k], out_d[k], out_m[k], out_v[k] = sg[k], ud[k], um[k], uv[k]

    return (loss, grad_x[None], *[out_g[k] for k in WEIGHTS], *[out_d[k] for k in WEIGHTS],
            *[out_m[k] for k in WEIGHTS], *[out_v[k] for k in WEIGHTS])
```

```python
import functools
import math

import jax
import jax.numpy as jnp
from jax import lax
from jax.experimental import pallas as pl
from jax.experimental.pallas import tpu as pltpu

F32 = jnp.float32
BF16 = jnp.bfloat16

DEPTH = 4
HEAD_DIM = 64
BLOCK = 128
WINDOW = 128
GRID_W = 64
C_WIDTH = 512
C_GROUPS = 4
CHUNK = 128
N_BRANCH = 3
BRANCH_WIDTH = 512
ROPE_THETA = 10000.0
X_HEADS = 4
X_HEAD_DIM = 128
ALPHA = (2 * DEPTH) ** 0.25
LN_EPS = 1e-5
RMS_EPS = 1e-6
ADAM_LR = 0.001
ADAM_B1 = 0.9
ADAM_B2 = 0.999
ADAM_EPS = 1e-08
ADAM_WD = 0.01
ADAM_STEP = 10
N_DEV = 8

COL_A = 0
COL_B = 768
COL_C = 1536
COL_GATE = 2560
QKV_W = 768

LANES = 128
V7X_VMEM_BYTES = 64 * 1024 * 1024
VMEM_LIMIT = V7X_VMEM_BYTES - 8 * 1024 * 1024
NEG_BIG = -1e30
ROW_TILE = 512

_NT = (((1,), (1,)), ((), ()))
_TN = (((0,), (0,)), ((), ()))
_NN = (((1,), (0,)), ((), ()))


def _cp(sem=None):
    return pltpu.CompilerParams(dimension_semantics=sem, vmem_limit_bytes=VMEM_LIMIT)


def _tile(n, target, align=LANES):
    if n <= target:
        return n
    best = None
    for t in range(align, target + 1, align):
        if n % t == 0:
            best = t
    assert best is not None, (n, target)
    return best


def _dot(a, b, dims=_NN):
    return lax.dot_general(a, b, dims, preferred_element_type=F32)


def _gelu(x):
    return 0.5 * x * (1.0 + lax.erf(x * 0.7071067811865476))


def _gelu_and_grad(x):
    cdf = 0.5 * (1.0 + lax.erf(x * 0.7071067811865476))
    return x * cdf, cdf + x * jnp.exp(-0.5 * x * x) * 0.3989422804014327


def _sigmoid(x):
    return 1.0 / (1.0 + jnp.exp(-x))


MESH_ID = pl.DeviceIdType.MESH
_ANY = pl.BlockSpec(memory_space=pl.ANY)
COPIES_PER_ARRAY = N_DEV - 1


def _comm_scratch(n_arrays):
    return [pltpu.SemaphoreType.DMA((COPIES_PER_ARRAY * n_arrays,)),
            pltpu.SemaphoreType.DMA((COPIES_PER_ARRAY * n_arrays,)), pltpu.SemaphoreType.DMA((n_arrays,))]


def _gathered_shape(x):
    return jax.ShapeDtypeStruct((N_DEV,) + x.shape[1:], x.dtype)


def _gather_plan(entries, send_sems, recv_sems, local_sems):
    mx, my, mc = lax.axis_index("x"), lax.axis_index("y"), lax.axis_index("c")
    me, sibling = (mx, my, mc), (mx, my, 1 - mc)
    chips = [(1 - mx, my), (mx, 1 - my), (1 - mx, 1 - my)]

    def copy(a, k, block, to, from_shard=False):
        x_ref, l, out_ref = entries[a]
        dst = out_ref.at[4 * block[0] + 2 * block[1] + block[2]]
        return pltpu.make_async_remote_copy(
            src_ref=x_ref.at[l] if from_shard else dst, dst_ref=dst,
            send_sem=send_sems.at[COPIES_PER_ARRAY * a + k], recv_sem=recv_sems.at[COPIES_PER_ARRAY * a + k],
            device_id=to, device_id_type=MESH_ID)

    def own(a):
        x_ref, l, out_ref = entries[a]
        return pltpu.make_async_copy(x_ref.at[l], out_ref.at[4 * mx + 2 * my + mc], local_sems.at[a])

    def first(a):
        return [copy(a, 0, me, sibling, True)] + [copy(a, 1 + j, me, (*chip, mc), True) for j, chip in enumerate(chips)]

    def passed(a):
        return [copy(a, 4 + j, (*chip, mc), sibling) for j, chip in enumerate(chips)]

    def start():
        for a in range(len(entries)):
            own(a).start()
            for cp in first(a):
                cp.start()

    def finish():
        for a in range(len(entries)):
            fwd = passed(a)
            for j, chip in enumerate(chips):
                copy(a, 1 + j, (*chip, mc), me).wait_recv()
                fwd[j].start()
        for a in range(len(entries)):
            copy(a, 0, sibling, me).wait_recv()
            for j, chip in enumerate(chips):
                copy(a, 4 + j, (*chip, 1 - mc), me).wait_recv()
            for cp in first(a) + passed(a):
                cp.wait_send()
            own(a).wait()

    return start, finish


def _scatter_plan(entries, send_sems, recv_sems, local_sems):
    mx, my, mc = lax.axis_index("x"), lax.axis_index("y"), lax.axis_index("c")
    me = 4 * mx + 2 * my + mc

    def src(a, dev):
        return entries[a][0].at[dev]

    def copies(a):
        _, recv_ref, lr = entries[a]
        out = []
        for k in range(1, N_DEV):
            px = 1 - mx if k & 4 else mx
            py = 1 - my if k & 2 else my
            pc = 1 - mc if k & 1 else mc
            peer = 4 * px + 2 * py + pc
            sems = dict(send_sem=send_sems.at[COPIES_PER_ARRAY * a + k - 1],
                        recv_sem=recv_sems.at[COPIES_PER_ARRAY * a + k - 1],
                        device_id=(px, py, pc), device_id_type=MESH_ID)
            sends = pltpu.make_async_remote_copy(src_ref=src(a, peer), dst_ref=recv_ref.at[lr, me], **sems)
            lands = pltpu.make_async_remote_copy(src_ref=src(a, me), dst_ref=recv_ref.at[lr, peer], **sems)
            out.append((sends, lands))
        return out

    def own(a):
        _, recv_ref, lr = entries[a]
        return pltpu.make_async_copy(src(a, me), recv_ref.at[lr, me], local_sems.at[a])

    def start():
        for a in range(len(entries)):
            own(a).start()
            for sends, _ in copies(a):
                sends.start()

    def finish():
        for a in range(len(entries)):
            for _, lands in copies(a):
                lands.wait_recv()
        for a in range(len(entries)):
            for sends, _ in copies(a):
                sends.wait_send()
            own(a).wait()

    return start, finish


MM_TILE, MM_TK = 1536, 2048
MM_TILE_LN = 512


def _mm(a, b, *, ta=False, tb=False, out_dtype=F32, res=None, res_scale=1.0, into=None, ln_bwd=None, name):
    segs = list(a) if isinstance(a, (list, tuple)) else [a]
    if ta:
        (K, M), seg_k = segs[0].shape, [segs[0].shape[0]]
        assert len(segs) == 1
    else:
        M, seg_k = segs[0].shape[0], [s.shape[1] for s in segs]
        K = sum(seg_k)
    if tb:
        N, Kb = b.shape
    else:
        Kb, N = b.shape
    assert K == Kb, ([s.shape for s in segs], b.shape, ta, tb)
    row_off = into[1] if into is not None else 0
    tm, tn = _tile(math.gcd(M, row_off), MM_TILE if ln_bwd is None else MM_TILE_LN), _tile(N, MM_TILE)
    tk = _tile(K, MM_TK) if len(segs) == 1 else _tile(math.gcd(*seg_k), MM_TILE)
    nk = K // tk
    seg_chunks = [ks // tk for ks in seg_k]
    seg_first = [sum(seg_chunks[:s]) for s in range(len(segs))]
    dims = (((0 if ta else 1,), (1 if tb else 0,)), ((), ()))
    ns = len(segs)
    n_res = ns + 1
    n_ln = n_res + (res is not None)
    into_held = into is not None and not isinstance(into[0], jax.ShapeDtypeStruct)
    n_in = n_ln + (3 if ln_bwd is not None else 0) + into_held
    assert ln_bwd is None or (tn == N and into is None)

    def body(*refs):
        a_refs, b_ref = refs[:ns], refs[ns]
        r_ref = refs[n_res] if res is not None else None
        o_ref = refs[n_in]
        first_row_tile = pl.program_id(0) == 0

        def finish(out):
            if r_ref is not None:
                out = out + res_scale * r_ref[...]
            if ln_bwd is None:
                o_ref[...] = out.astype(o_ref.dtype)
                return
            xh_ref, rs_ref, g_ref = refs[n_ln:n_ln + 3]
            ob_ref, dg_ref, db_ref = refs[n_in + 1:n_in + 4]

            @pl.when(first_row_tile)
            def _():
                dg_ref[...] = jnp.zeros_like(dg_ref)
                db_ref[...] = jnp.zeros_like(db_ref)

            xh = xh_ref[...]
            dxh = out * g_ref[...]
            m1 = jnp.mean(dxh, axis=-1, keepdims=True)
            m2 = jnp.mean(dxh * xh, axis=-1, keepdims=True)
            dz = rs_ref[...] * (dxh - m1 - xh * m2)
            o_ref[...] = dz
            ob_ref[...] = dz.astype(BF16)
            dg_ref[...] += jnp.sum(out * xh, axis=0, keepdims=True)
            db_ref[...] += jnp.sum(out, axis=0, keepdims=True)

        def prod(s):
            return _dot(a_refs[s][...].astype(BF16), b_ref[...].astype(BF16), dims)

        if nk == 1:
            finish(prod(0))
            return
        acc = refs[n_in + (4 if ln_bwd is not None else 1)]
        k = pl.program_id(2)

        @pl.when(k == 0)
        def _():
            acc[...] = jnp.zeros_like(acc)

        for s in range(ns):
            def add(s=s):
                acc[...] += prod(s)
            pl.when((k >= seg_first[s]) & (k < seg_first[s] + seg_chunks[s]))(add)

        @pl.when(k == nk - 1)
        def _():
            finish(acc[...])

    if ta:
        a_specs = [pl.BlockSpec((tk, tm), lambda i, j, k: (k, i))]
    else:
        a_specs = [pl.BlockSpec((tm, tk), functools.partial(
            lambda i, j, k, first, n: (i, jnp.clip(k - first, 0, n - 1)), first=seg_first[s], n=seg_chunks[s]))
            for s in range(ns)]
    b_spec = pl.BlockSpec((tn, tk), lambda i, j, k: (j, k)) if tb else pl.BlockSpec((tk, tn), lambda i, j, k: (k, j))
    in_specs = a_specs + [b_spec]
    args = segs + [b]
    if res is not None:
        in_specs.append(pl.BlockSpec((tm, tn), lambda i, j, k: (i, j)))
        args.append(res)
    out_spec = pl.BlockSpec((tm, tn), lambda i, j, k: (i, j))
    if ln_bwd is not None:
        xh, rs, g = ln_bwd
        in_specs += [out_spec, pl.BlockSpec((tm, 1), lambda i, j, k: (i, 0)), pl.BlockSpec((1, tn), lambda i, j, k: (0, j))]
        args += [xh, rs, g]
        vec = pl.BlockSpec((1, tn), lambda i, j, k: (0, j))
        return pl.pallas_call(
            body, name=name,
            out_shape=(jax.ShapeDtypeStruct((M, N), F32), jax.ShapeDtypeStruct((M, N), BF16),
                       jax.ShapeDtypeStruct((1, N), F32), jax.ShapeDtypeStruct((1, N), F32)),
            grid=(M // tm, N // tn, nk),
            in_specs=in_specs,
            out_specs=(out_spec, out_spec, vec, vec),
            scratch_shapes=[pltpu.VMEM((tm, tn), F32)] if nk > 1 else [],
            compiler_params=_cp(("arbitrary", "arbitrary", "arbitrary")),
        )(*args)
    if into is None:
        out_shape = jax.ShapeDtypeStruct((M, N), out_dtype)
        blk_off, aliases = 0, {}
    else:
        buf = into[0]
        assert buf.shape[1] == N and row_off % tm == 0 and row_off + M <= buf.shape[0], (buf.shape, M, N, row_off)
        out_shape = jax.ShapeDtypeStruct(buf.shape, buf.dtype)
        blk_off, aliases = row_off // tm, {}
        if into_held:
            aliases = {n_in - 1: 0}
            in_specs.append(_ANY)
            args.append(buf)
    return pl.pallas_call(
        body, name=name,
        out_shape=out_shape,
        grid=(M // tm, N // tn, nk),
        in_specs=in_specs,
        out_specs=pl.BlockSpec((tm, tn), lambda i, j, k: (i + blk_off, j)),
        scratch_shapes=[pltpu.VMEM((tm, tn), F32)] if nk > 1 else [],
        input_output_aliases=aliases,
        compiler_params=_cp(("parallel", "parallel", "arbitrary")),
    )(*args)


def _mm_res_ln(a, w, x, g, b, *, name):
    S, K = a.shape
    D = w.shape[1]
    tm = _tile(S, ROW_TILE)

    def body(a_ref, w_ref, x_ref, g_ref, b_ref, y_ref, yb_ref, xh_ref, rs_ref):
        h = _dot(a_ref[...], w_ref[...])
        z = ALPHA * x_ref[...] + h
        mu = jnp.mean(z, axis=-1, keepdims=True)
        zc = z - mu
        var = jnp.mean(zc * zc, axis=-1, keepdims=True)
        r = lax.rsqrt(var + LN_EPS)
        xh = zc * r
        y = xh * g_ref[...] + b_ref[...]
        y_ref[...] = y
        yb_ref[...] = y.astype(BF16)
        xh_ref[...] = xh
        rs_ref[...] = r

    row = lambda i: (i, 0)
    full = lambda i: (0, 0)
    return pl.pallas_call(
        body, name=name,
        out_shape=(jax.ShapeDtypeStruct((S, D), F32), jax.ShapeDtypeStruct((S, D), BF16),
                   jax.ShapeDtypeStruct((S, D), F32), jax.ShapeDtypeStruct((S, 1), F32)),
        grid=(S // tm,),
        in_specs=[pl.BlockSpec((tm, K), row), pl.BlockSpec((K, D), full), pl.BlockSpec((tm, D), row),
                  pl.BlockSpec((1, D), full), pl.BlockSpec((1, D), full)],
        out_specs=(pl.BlockSpec((tm, D), row), pl.BlockSpec((tm, D), row), pl.BlockSpec((tm, D), row),
                   pl.BlockSpec((tm, 1), row)),
        compiler_params=_cp(("parallel",)),
    )(a, w, x, g, b)


def _ln_bwd(dy, xh, rs, g, *, name):
    S, D = dy.shape
    tm = _tile(S, ROW_TILE)

    def body(dy_ref, xh_ref, rs_ref, g_ref, dz_ref, dzb_ref, dg_ref, db_ref):
        @pl.when(pl.program_id(0) == 0)
        def _():
            dg_ref[...] = jnp.zeros_like(dg_ref)
            db_ref[...] = jnp.zeros_like(db_ref)

        dy = dy_ref[...]
        xh = xh_ref[...]
        dxh = dy * g_ref[...]
        m1 = jnp.mean(dxh, axis=-1, keepdims=True)
        m2 = jnp.mean(dxh * xh, axis=-1, keepdims=True)
        dz = rs_ref[...] * (dxh - m1 - xh * m2)
        dz_ref[...] = dz
        dzb_ref[...] = dz.astype(BF16)
        dg_ref[...] += jnp.sum(dy * xh, axis=0, keepdims=True)
        db_ref[...] += jnp.sum(dy, axis=0, keepdims=True)

    row = lambda i: (i, 0)
    full = lambda i: (0, 0)
    return pl.pallas_call(
        body, name=name,
        out_shape=(jax.ShapeDtypeStruct((S, D), F32), jax.ShapeDtypeStruct((S, D), BF16),
                   jax.ShapeDtypeStruct((1, D), F32), jax.ShapeDtypeStruct((1, D), F32)),
        grid=(S // tm,),
        in_specs=[pl.BlockSpec((tm, D), row), pl.BlockSpec((tm, D), row), pl.BlockSpec((tm, 1), row),
                  pl.BlockSpec((1, D), full)],
        out_specs=(pl.BlockSpec((tm, D), row), pl.BlockSpec((tm, D), row), pl.BlockSpec((1, D), full),
                   pl.BlockSpec((1, D), full)),
        compiler_params=_cp(("arbitrary",)),
    )(dy, xh, rs, g)


def _loss_head(y, t, *, name):
    S, D = y.shape
    tm = _tile(S, 512)

    def body(y_ref, t_ref, dy_ref, l_ref):
        @pl.when(pl.program_id(0) == 0)
        def _():
            l_ref[...] = jnp.zeros_like(l_ref)

        e = y_ref[...] - t_ref[...]
        dy_ref[...] = e / D
        l_ref[...] += 0.5 * jnp.sum(jnp.mean(e * e, axis=-1, keepdims=True), axis=0, keepdims=True)

    row = lambda i: (i, 0)
    return pl.pallas_call(
        body, name=name,
        out_shape=(jax.ShapeDtypeStruct((S, D), F32), jax.ShapeDtypeStruct((1, 1), F32)),
        grid=(S // tm,),
        in_specs=[pl.BlockSpec((tm, D), row), pl.BlockSpec((tm, D), row)],
        out_specs=(pl.BlockSpec((tm, D), row), pl.BlockSpec((1, 1), lambda i: (0, 0))),
        compiler_params=_cp(("arbitrary",)),
    )(y, t)


def _rope_tables(S):
    pos = jnp.arange(S, dtype=jnp.int32)
    row = pos // GRID_W
    col = pos % GRID_W

    def cs(p, d):
        half = d // 2
        inv = ROPE_THETA ** (-jnp.arange(half, dtype=F32) * (2.0 / d))
        ang = p.astype(F32)[:, None] * inv[None, :]
        c, s = jnp.cos(ang), jnp.sin(ang)
        return jnp.concatenate([c, c], -1), jnp.concatenate([-s, s], -1)

    ca, sa = cs(pos, HEAD_DIM)
    cr, sr = cs(row, HEAD_DIM // 2)
    cc, sc = cs(col, HEAD_DIM // 2)
    cb, sb = jnp.concatenate([cr, cc], -1), jnp.concatenate([sr, sc], -1)
    two = lambda t: jnp.concatenate([t, t], -1)
    return two(ca), two(sa), two(cb), two(sb)


def _partner(x, lane, width):
    h = width // 2
    return jnp.where(lane % width < h, pltpu.roll(x, LANES - h, 1), pltpu.roll(x, h, 1))


def _rope_fwd(x, c, s, lane, width):
    return x * c + _partner(x, lane, width) * s


def _rope_bwd(dy, c, s, lane, width):
    return dy * c + _partner(dy * s, lane, width)


def _head_sum(x, seg):
    return lax.dot_general(x, seg, _NN, precision=lax.Precision.HIGHEST, preferred_element_type=F32)


def _split_heads(x, lane):
    lo = lane < HEAD_DIM
    r = pltpu.roll(x, HEAD_DIM, 1)
    z = jnp.zeros_like(x)
    return jnp.where(lo, x, z), jnp.where(lo, z, r), jnp.where(lo, r, z), jnp.where(lo, z, x)


def _fold_heads(d0, d1, lane):
    t0 = d0 + pltpu.roll(d0, HEAD_DIM, 1)
    t1 = d1 + pltpu.roll(d1, HEAD_DIM, 1)
    return jnp.where(lane < HEAD_DIM, t0, t1)


def _seg_matrix():
    i = jnp.arange(LANES)
    return (i[:, None] // HEAD_DIM == i[None, :] // HEAD_DIM).astype(F32)


def _prep(proj, tabs, qg2, kg2, seg, *, name):
    S = proj.shape[0]
    ts = _tile(S, ROW_TILE)
    ca, sa, cb, sb = tabs

    def body(pa_ref, pb_ref, ca_ref, sa_ref, cb_ref, sb_ref, qg_ref, kg_ref, seg_ref,
             aq_ref, ak_ref, av_ref, bq_ref, bk_ref, bv_ref):
        lane = lax.broadcasted_iota(jnp.int32, (ts, LANES), 1)
        ca, sa, cb, sb = ca_ref[...], sa_ref[...], cb_ref[...], sb_ref[...]
        seg = seg_ref[...]

        def norm(x, gain):
            r = lax.rsqrt(_head_sum(x * x, seg) * (1.0 / HEAD_DIM) + RMS_EPS)
            return x * r * gain

        def put(ref, x):
            for i, part in enumerate(_split_heads(x, lane)):
                ref[i] = part.astype(BF16)

        for gidx in range(4):
            cols = slice(gidx * LANES, (gidx + 1) * LANES)
            aq_ref[:, cols] = (_rope_fwd(pa_ref[:, cols], ca, sa, lane, HEAD_DIM) * 0.125).astype(BF16)
            bq = norm(pb_ref[:, cols], qg_ref[...])
            bq_ref[:, cols] = (_rope_fwd(bq, cb, sb, lane, HEAD_DIM // 2) * 0.125).astype(BF16)
        put(ak_ref, _rope_fwd(pa_ref[:, 512:640], ca, sa, lane, HEAD_DIM))
        put(av_ref, pa_ref[:, 640:768])
        bk = norm(pb_ref[:, 512:640], kg_ref[...])
        put(bk_ref, _rope_fwd(bk, cb, sb, lane, HEAD_DIM // 2))
        put(bv_ref, pb_ref[:, 640:768])

    row = lambda i: (i, 0)
    full = lambda i: (0, 0)
    tab = pl.BlockSpec((ts, LANES), row)
    kv_shape = jax.ShapeDtypeStruct((4, S, LANES), BF16)
    kv_spec = pl.BlockSpec((4, ts, LANES), lambda i: (0, i, 0))
    q_shape = jax.ShapeDtypeStruct((S, 512), BF16)
    q_spec = pl.BlockSpec((ts, 512), row)
    return pl.pallas_call(
        body, name=name,
        out_shape=(q_shape, kv_shape, kv_shape, q_shape, kv_shape, kv_shape),
        grid=(S // ts,),
        in_specs=[pl.BlockSpec((ts, QKV_W), lambda i: (i, 0)), pl.BlockSpec((ts, QKV_W), lambda i: (i, 1)),
                  tab, tab, tab, tab, pl.BlockSpec((1, LANES), full), pl.BlockSpec((1, LANES), full),
                  pl.BlockSpec((LANES, LANES), full)],
        out_specs=(q_spec, kv_spec, kv_spec, q_spec, kv_spec, kv_spec),
        compiler_params=_cp(("parallel",)),
    )(proj, proj, ca, sa, cb, sb, qg2, kg2, seg)


def _unprep(dqa, dka, dva, dqb, dkb, dvb, proj, tabs, qg2, kg2, seg, *, name):
    S = proj.shape[0]
    ts = _tile(S, ROW_TILE)
    ca, sa, cb, sb = tabs

    def body(dqa_ref, dka_ref, dva_ref, dqb_ref, dkb_ref, dvb_ref, pb_ref, ca_ref, sa_ref, cb_ref, sb_ref,
             qg_ref, kg_ref, seg_ref, dp_ref, dqg_ref, dkg_ref):
        @pl.when(pl.program_id(0) == 0)
        def _():
            dqg_ref[...] = jnp.zeros_like(dqg_ref)
            dkg_ref[...] = jnp.zeros_like(dkg_ref)

        lane = lax.broadcasted_iota(jnp.int32, (ts, LANES), 1)
        ca, sa, cb, sb = ca_ref[...], sa_ref[...], cb_ref[...], sb_ref[...]
        seg = seg_ref[...]

        def norm_bwd(dy, x, gain):
            r = lax.rsqrt(_head_sum(x * x, seg) * (1.0 / HEAD_DIM) + RMS_EPS)
            gdy = gain * dy
            dot = _head_sum(gdy * x, seg) * (1.0 / HEAD_DIM)
            dx = r * gdy - x * (r * r * r) * dot
            return dx, jnp.sum(dy * x * r, axis=0, keepdims=True)

        for gidx in range(4):
            cols = slice(gidx * LANES, (gidx + 1) * LANES)
            dp_ref[:, cols] = _rope_bwd(dqa_ref[:, cols] * 0.125, ca, sa, lane, HEAD_DIM).astype(BF16)
            dbq = _rope_bwd(dqb_ref[:, cols] * 0.125, cb, sb, lane, HEAD_DIM // 2)
            dx, dg = norm_bwd(dbq, pb_ref[:, cols], qg_ref[...])
            dp_ref[:, COL_B + gidx * LANES:COL_B + (gidx + 1) * LANES] = dx.astype(BF16)
            dqg_ref[...] += dg
        dak = _fold_heads(dka_ref[0] + dka_ref[1], dka_ref[2] + dka_ref[3], lane)
        dp_ref[:, 512:640] = _rope_bwd(dak, ca, sa, lane, HEAD_DIM).astype(BF16)
        dp_ref[:, 640:768] = _fold_heads(dva_ref[0] + dva_ref[1], dva_ref[2] + dva_ref[3], lane).astype(BF16)
        dbk = _fold_heads(dkb_ref[0] + dkb_ref[1], dkb_ref[2] + dkb_ref[3], lane)
        dbk = _rope_bwd(dbk, cb, sb, lane, HEAD_DIM // 2)
        dx, dg = norm_bwd(dbk, pb_ref[:, 512:640], kg_ref[...])
        dp_ref[:, COL_B + 512:COL_B + 640] = dx.astype(BF16)
        dkg_ref[...] += dg
        dp_ref[:, COL_B + 640:COL_B + 768] = _fold_heads(dvb_ref[0] + dvb_ref[1], dvb_ref[2] + dvb_ref[3],
                                                         lane).astype(BF16)

    row = lambda i: (i, 0)
    full = lambda i: (0, 0)
    tab = pl.BlockSpec((ts, LANES), row)
    q_spec = pl.BlockSpec((ts, 512), row)
    kv_spec = pl.BlockSpec((4, ts, LANES), lambda i: (0, i, 0))
    return pl.pallas_call(
        body, name=name,
        out_shape=(jax.ShapeDtypeStruct((S, 2 * QKV_W), BF16), jax.ShapeDtypeStruct((1, LANES), F32),
                   jax.ShapeDtypeStruct((1, LANES), F32)),
        grid=(S // ts,),
        in_specs=[q_spec, kv_spec, kv_spec, q_spec, kv_spec, kv_spec,
                  pl.BlockSpec((ts, QKV_W), lambda i: (i, 1)), tab, tab, tab, tab,
                  pl.BlockSpec((1, LANES), full), pl.BlockSpec((1, LANES), full), pl.BlockSpec((LANES, LANES), full)],
        out_specs=(pl.BlockSpec((ts, 2 * QKV_W), row), pl.BlockSpec((1, LANES), full),
                   pl.BlockSpec((1, LANES), full)),
        compiler_params=_cp(("arbitrary",)),
    )(dqa, dka, dva, dqb, dkb, dvb, proj, ca, sa, cb, sb, qg2, kg2, seg)


def _attn_dense_fwd(q, k4, v4, *, gather=(), name):
    S = q.shape[0]
    tq = _tile(S, 256)
    xs = [x for x, _ in gather]
    na = len(xs)

    def body(q_ref, k_ref, v_ref, *rest):
        o_ref, lse_ref = rest[na], rest[na + 1]
        if na:
            x_refs, out_refs, sems = rest[:na], rest[na + 2:2 * na + 2], rest[2 * na + 2:]
            start, finish = _gather_plan([(x_refs[a], gather[a][1], out_refs[a]) for a in range(na)], *sems)
            pl.when((pl.program_id(0) == 0) & (pl.program_id(1) == 0))(start)
        for pr in range(2):
            qp = q_ref[:, pr * LANES:(pr + 1) * LANES]
            acc = None
            for half in range(2):
                s = _dot(qp, k_ref[half], _NT)
                m = jnp.max(s, axis=-1, keepdims=True)
                e = jnp.exp(s - m)
                l = jnp.sum(e, axis=-1, keepdims=True)
                pv = _dot(e.astype(BF16), v_ref[half]) * (1.0 / l)
                acc = pv if acc is None else acc + pv
                lse_ref[pr * 2 + half] = m + jnp.log(l)
            o_ref[:, pr * LANES:(pr + 1) * LANES] = acc.astype(BF16)
        if na:
            pl.when((pl.program_id(0) == 1) & (pl.program_id(1) == S // tq - 1))(finish)

    kv_spec = pl.BlockSpec((2, S, LANES), lambda kv, i: (kv, 0, 0))
    res = pl.pallas_call(
        body, name=name,
        out_shape=(jax.ShapeDtypeStruct((S, 512), BF16), jax.ShapeDtypeStruct((8, S, 1), F32),
                   *[_gathered_shape(x) for x in xs]),
        grid=(2, S // tq),
        in_specs=[pl.BlockSpec((tq, 256), lambda kv, i: (i, kv)), kv_spec, kv_spec] + [_ANY] * na,
        out_specs=(pl.BlockSpec((tq, 256), lambda kv, i: (i, kv)),
                   pl.BlockSpec((4, tq, 1), lambda kv, i: (kv, i, 0)), *([_ANY] * na)),
        scratch_shapes=_comm_scratch(na) if na else [],
        compiler_params=_cp(("arbitrary", "arbitrary") if na else ("parallel", "parallel")),
    )(q, k4, v4, *xs)
    return res[0], res[1], list(res[2:])


def _attn_dense_bwd(q, k4, v4, lse, do, *, scatter=(), name):
    S = q.shape[0]
    tq = _tile(S, 256)
    na = len(scatter)
    comm_in, comm_out, held = _scatter_io(scatter)
    n_in = len(comm_in)

    def body(q_ref, k_ref, v_ref, lse_ref, do_ref, *rest):
        dq_ref, dk_ref, dv_ref = rest[n_in:n_in + 3]
        if na:
            s_refs, r_refs, sems = rest[:na], rest[n_in + 3:n_in + 3 + na], rest[n_in + 3 + na:]
            start, finish = _scatter_plan([(s_refs[a], r_refs[a], scatter[a][2]) for a in range(na)], *sems)
            pl.when((pl.program_id(0) == 0) & (pl.program_id(1) == 0))(start)

        @pl.when(pl.program_id(1) == 0)
        def _():
            dk_ref[...] = jnp.zeros_like(dk_ref)
            dv_ref[...] = jnp.zeros_like(dv_ref)

        lane = lax.broadcasted_iota(jnp.int32, (tq, LANES), 1)
        for pr in range(2):
            qp = q_ref[:, pr * LANES:(pr + 1) * LANES]
            dop = do_ref[:, pr * LANES:(pr + 1) * LANES].astype(BF16)
            dq = None
            for half in range(2):
                mine = (lane < HEAD_DIM) if half == 0 else (lane >= HEAD_DIM)
                s = _dot(qp, k_ref[half], _NT)
                p = jnp.exp(s - lse_ref[pr * 2 + half])
                dp = _dot(dop, v_ref[half], _NT)
                delta = jnp.sum(p * dp, axis=-1, keepdims=True)
                ds = (p * (dp - delta)).astype(BF16)
                pb = p.astype(BF16)
                d = _dot(ds, k_ref[half])
                dq = d if dq is None else dq + d
                dk_ref[half] += _dot(ds, jnp.where(mine, qp, jnp.zeros_like(qp)), _TN)
                dv_ref[half] += _dot(pb, jnp.where(mine, dop, jnp.zeros_like(dop)), _TN)
            dq_ref[:, pr * LANES:(pr + 1) * LANES] = dq
        if na:
            pl.when((pl.program_id(0) == 1) & (pl.program_id(1) == S // tq - 1))(finish)

    kv_spec = pl.BlockSpec((2, S, LANES), lambda kv, i: (kv, 0, 0))
    q_spec = pl.BlockSpec((tq, 256), lambda kv, i: (i, kv))
    res = pl.pallas_call(
        body, name=name,
        out_shape=(jax.ShapeDtypeStruct((S, 512), F32), jax.ShapeDtypeStruct((4, S, LANES), F32),
                   jax.ShapeDtypeStruct((4, S, LANES), F32), *comm_out),
        grid=(2, S // tq),
        in_specs=[q_spec, kv_spec, kv_spec, pl.BlockSpec((4, tq, 1), lambda kv, i: (kv, i, 0)), q_spec]
                 + [_ANY] * n_in,
        out_specs=(q_spec, kv_spec, kv_spec, *([_ANY] * na)),
        scratch_shapes=_comm_scratch(na) if na else [],
        input_output_aliases={5 + na + i: 3 + a for i, a in enumerate(held)},
        compiler_params=_cp(("arbitrary", "arbitrary") if na else ("parallel", "arbitrary")),
    )(q, k4, v4, lse, do, *comm_in)
    return res[0], res[1], res[2], list(res[3:])


WIN_Q = 2 * BLOCK
WIN_KEYS = WIN_Q + 2 * WINDOW


def _win_start(n, S):
    return pl.multiple_of(jnp.clip(n * WIN_Q - WINDOW, 0, S - WIN_KEYS), BLOCK)


def _win_valid(n, start):
    qpos = n * WIN_Q + lax.broadcasted_iota(jnp.int32, (WIN_Q, WIN_KEYS), 0)
    kpos = start + lax.broadcasted_iota(jnp.int32, (WIN_Q, WIN_KEYS), 1)
    return jnp.abs(qpos - kpos) <= WINDOW


def _attn_win_fwd(q, k4, v4, sink, *, name):
    S = q.shape[0]
    assert S >= WIN_KEYS

    def body(sink_ref, q_ref, k_ref, v_ref, o_ref, lse_ref):
        n = pl.program_id(0)
        start = _win_start(n, S)
        valid = _win_valid(n, start)
        for kv in range(2):
            for pr in range(2):
                cols = slice((kv * 2 + pr) * LANES, (kv * 2 + pr + 1) * LANES)
                qp = q_ref[:, cols]
                acc = None
                for half in range(2):
                    h = kv * 4 + pr * 2 + half
                    kk = k_ref[kv * 2 + half, pl.ds(start, WIN_KEYS), :]
                    vv = v_ref[kv * 2 + half, pl.ds(start, WIN_KEYS), :]
                    s = jnp.where(valid, _dot(qp, kk, _NT), NEG_BIG)
                    snk = sink_ref[h]
                    m = jnp.maximum(jnp.max(s, axis=-1, keepdims=True), snk)
                    e = jnp.exp(s - m)
                    l = jnp.sum(e, axis=-1, keepdims=True) + jnp.exp(snk - m)
                    pv = _dot(e.astype(BF16), vv) * (1.0 / l)
                    acc = pv if acc is None else acc + pv
                    lse_ref[h] = m + jnp.log(l)
                o_ref[:, cols] = acc.astype(BF16)

    kv_spec = pl.BlockSpec((4, S, LANES), lambda n: (0, 0, 0))
    return pl.pallas_call(
        body, name=name,
        out_shape=(jax.ShapeDtypeStruct((S, 512), BF16), jax.ShapeDtypeStruct((8, S, 1), F32)),
        grid=(S // WIN_Q,),
        in_specs=[pl.BlockSpec(memory_space=pltpu.SMEM), pl.BlockSpec((WIN_Q, 512), lambda n: (n, 0)),
                  kv_spec, kv_spec],
        out_specs=(pl.BlockSpec((WIN_Q, 512), lambda n: (n, 0)), pl.BlockSpec((8, WIN_Q, 1), lambda n: (0, n, 0))),
        compiler_params=_cp(("parallel",)),
    )(sink, q, k4, v4)


def _attn_win_bwd(q, k4, v4, sink, lse, do, *, name):
    S = q.shape[0]

    def body(sink_ref, q_ref, k_ref, v_ref, lse_ref, do_ref, dq_ref, dk_ref, dv_ref, dsink_ref):
        n = pl.program_id(0)

        @pl.when(n == 0)
        def _():
            dk_ref[...] = jnp.zeros_like(dk_ref)
            dv_ref[...] = jnp.zeros_like(dv_ref)
            dsink_ref[...] = jnp.zeros_like(dsink_ref)

        start = _win_start(n, S)
        valid = _win_valid(n, start)
        lane = lax.broadcasted_iota(jnp.int32, (WIN_Q, LANES), 1)
        for kv in range(2):
            for pr in range(2):
                cols = slice((kv * 2 + pr) * LANES, (kv * 2 + pr + 1) * LANES)
                qp = q_ref[:, cols]
                dop = do_ref[:, cols].astype(BF16)
                dq = None
                for half in range(2):
                    h = kv * 4 + pr * 2 + half
                    slot = kv * 2 + half
                    mine = (lane < HEAD_DIM) if half == 0 else (lane >= HEAD_DIM)
                    win = pl.ds(start, WIN_KEYS)
                    kk = k_ref[slot, win, :]
                    vv = v_ref[slot, win, :]
                    lse_h = lse_ref[h]
                    s = jnp.where(valid, _dot(qp, kk, _NT), NEG_BIG)
                    p = jnp.exp(s - lse_h)
                    dp = _dot(dop, vv, _NT)
                    delta = jnp.sum(p * dp, axis=-1, keepdims=True)
                    ds = (p * (dp - delta)).astype(BF16)
                    pb = p.astype(BF16)
                    d = _dot(ds, kk)
                    dq = d if dq is None else dq + d
                    dk_ref[slot, win, :] += _dot(ds, jnp.where(mine, qp, jnp.zeros_like(qp)), _TN)
                    dv_ref[slot, win, :] += _dot(pb, jnp.where(mine, dop, jnp.zeros_like(dop)), _TN)
                    p_sink = jnp.exp(sink_ref[h] - lse_h)
                    dsink_ref[h:h + 1, :] += jnp.broadcast_to(-jnp.sum(p_sink * delta, axis=0, keepdims=True),
                                                              (1, LANES))
                dq_ref[:, cols] = dq

    kv_spec = pl.BlockSpec((4, S, LANES), lambda n: (0, 0, 0))
    q_spec = pl.BlockSpec((WIN_Q, 512), lambda n: (n, 0))
    return pl.pallas_call(
        body, name=name,
        out_shape=(jax.ShapeDtypeStruct((S, 512), F32), jax.ShapeDtypeStruct((4, S, LANES), F32),
                   jax.ShapeDtypeStruct((4, S, LANES), F32), jax.ShapeDtypeStruct((8, LANES), F32)),
        grid=(S // WIN_Q,),
        in_specs=[pl.BlockSpec(memory_space=pltpu.SMEM), q_spec, kv_spec, kv_spec,
                  pl.BlockSpec((8, WIN_Q, 1), lambda n: (0, n, 0)), q_spec],
        out_specs=(q_spec, kv_spec, kv_spec, pl.BlockSpec((8, LANES), lambda n: (0, 0))),
        compiler_params=_cp(("arbitrary",)),
    )(sink, q, k4, v4, lse, do)


def _c_ln(v, g, b):
    mu = jnp.mean(v, axis=-1, keepdims=True)
    vc = v - mu
    r = lax.rsqrt(jnp.mean(vc * vc, axis=-1, keepdims=True) + LN_EPS)
    vh = vc * r
    return vh, r, vh * g + b


def _gmlp_blocks(rows):
    return [(slice(c * CHUNK, (c + 1) * CHUNK), slice(gi * LANES, (gi + 1) * LANES), gi)
            for c in range(rows // CHUNK) for gi in range(C_GROUPS)]


def _gmlp_fwd(proj, ws, bs3, lg, lb, *, name):
    S = proj.shape[0]
    rows = _tile(S, ROW_TILE)

    def body(u_ref, v_ref, ws_ref, bs_ref, lg_ref, lb_ref, o_ref):
        u = _gelu(u_ref[...])
        _, _, vn = _c_ln(_gelu(v_ref[...]), lg_ref[...], lb_ref[...])
        vn = vn.astype(BF16)
        for r, cols, gi in _gmlp_blocks(rows):
            mixed = _dot(ws_ref[gi], vn[r, cols]) + bs_ref[gi]
            o_ref[r, cols] = (u[r, cols] * mixed).astype(BF16)

    full2 = lambda n: (0, 0)
    full3 = lambda n: (0, 0, 0)
    return pl.pallas_call(
        body, name=name,
        out_shape=jax.ShapeDtypeStruct((S, C_WIDTH), BF16),
        grid=(S // rows,),
        in_specs=[pl.BlockSpec((rows, C_WIDTH), lambda n: (n, COL_C // C_WIDTH)),
                  pl.BlockSpec((rows, C_WIDTH), lambda n: (n, COL_C // C_WIDTH + 1)),
                  pl.BlockSpec((C_GROUPS, CHUNK, CHUNK), full3), pl.BlockSpec((C_GROUPS, CHUNK, 1), full3),
                  pl.BlockSpec((1, C_WIDTH), full2), pl.BlockSpec((1, C_WIDTH), full2)],
        out_specs=pl.BlockSpec((rows, C_WIDTH), lambda n: (n, 0)),
        compiler_params=_cp(("parallel",)),
    )(proj, proj, ws, bs3, lg, lb)


def _gmlp_bwd(proj, dout, ws, bs3, lg, lb, *, name):
    S = proj.shape[0]
    rows = _tile(S, ROW_TILE)
    nch = rows // CHUNK

    def body(u_ref, v_ref, d_ref, ws_ref, bs_ref, lg_ref, lb_ref, dz_ref, dws_ref, dbs_ref, dlg_ref, dlb_ref):
        @pl.when(pl.program_id(0) == 0)
        def _():
            dws_ref[...] = jnp.zeros_like(dws_ref)
            dbs_ref[...] = jnp.zeros_like(dbs_ref)
            dlg_ref[...] = jnp.zeros_like(dlg_ref)
            dlb_ref[...] = jnp.zeros_like(dlb_ref)

        u_pre, v_pre, d = u_ref[...], v_ref[...], d_ref[...]
        u, u_grad = _gelu_and_grad(u_pre)
        v, v_grad = _gelu_and_grad(v_pre)
        vh, r, vn = _c_ln(v, lg_ref[...], lb_ref[...])
        vnb = vn.astype(BF16)
        dm = d * u
        dvn_parts = []
        dws = [None] * C_GROUPS
        dbs = [None] * C_GROUPS
        for rs, cols, gi in _gmlp_blocks(rows):
            mixed = _dot(ws_ref[gi], vnb[rs, cols]) + bs_ref[gi]
            dz_ref[rs, cols] = (d[rs, cols] * mixed * u_grad[rs, cols]).astype(BF16)
            dmb = dm[rs, cols].astype(BF16)
            t, b = _dot(dmb, vnb[rs, cols], _NT), jnp.sum(dm[rs, cols], axis=-1, keepdims=True)
            dws[gi] = t if dws[gi] is None else dws[gi] + t
            dbs[gi] = b if dbs[gi] is None else dbs[gi] + b
            dvn_parts.append(_dot(ws_ref[gi], dmb, _TN))
        for gi in range(C_GROUPS):
            dws_ref[gi] += dws[gi]
            dbs_ref[gi] += dbs[gi]
        dvn = jnp.concatenate([jnp.concatenate(dvn_parts[c * C_GROUPS:(c + 1) * C_GROUPS], axis=-1)
                               for c in range(nch)], axis=0)
        dlg_ref[...] += jnp.sum(dvn * vh, axis=0, keepdims=True)
        dlb_ref[...] += jnp.sum(dvn, axis=0, keepdims=True)
        dvh = dvn * lg_ref[...]
        m1 = jnp.mean(dvh, axis=-1, keepdims=True)
        m2 = jnp.mean(dvh * vh, axis=-1, keepdims=True)
        dv = r * (dvh - m1 - vh * m2)
        dz_ref[:, C_WIDTH:] = (dv * v_grad).astype(BF16)

    full2 = lambda n: (0, 0)
    full3 = lambda n: (0, 0, 0)
    return pl.pallas_call(
        body, name=name,
        out_shape=(jax.ShapeDtypeStruct((S, 2 * C_WIDTH), BF16), jax.ShapeDtypeStruct((C_GROUPS, CHUNK, CHUNK), F32),
                   jax.ShapeDtypeStruct((C_GROUPS, CHUNK, 1), F32), jax.ShapeDtypeStruct((1, C_WIDTH), F32),
                   jax.ShapeDtypeStruct((1, C_WIDTH), F32)),
        grid=(S // rows,),
        in_specs=[pl.BlockSpec((rows, C_WIDTH), lambda n: (n, COL_C // C_WIDTH)),
                  pl.BlockSpec((rows, C_WIDTH), lambda n: (n, COL_C // C_WIDTH + 1)),
                  pl.BlockSpec((rows, C_WIDTH), lambda n: (n, 0)),
                  pl.BlockSpec((C_GROUPS, CHUNK, CHUNK), full3), pl.BlockSpec((C_GROUPS, CHUNK, 1), full3),
                  pl.BlockSpec((1, C_WIDTH), full2), pl.BlockSpec((1, C_WIDTH), full2)],
        out_specs=(pl.BlockSpec((rows, 2 * C_WIDTH), lambda n: (n, 0)), pl.BlockSpec((C_GROUPS, CHUNK, CHUNK), full3),
                   pl.BlockSpec((C_GROUPS, CHUNK, 1), full3), pl.BlockSpec((1, C_WIDTH), full2),
                   pl.BlockSpec((1, C_WIDTH), full2)),
        compiler_params=_cp(("arbitrary",)),
    )(proj, proj, dout, ws, bs3, lg, lb)


GATE_BLK = 512


def _gate_specs(tm, D):
    nh = D // GATE_BLK
    first = COL_GATE // GATE_BLK
    return [pl.BlockSpec((tm, GATE_BLK), functools.partial(lambda i, c: (i, c), c=first + b))
            for b in range(N_BRANCH * nh)]


def _merge_fwd(oa, ob, oc, wb, proj, bg, *, name):
    S = oa.shape[0]
    D = wb.shape[2]
    assert D % GATE_BLK == 0
    nh = D // GATE_BLK
    tm = _tile(S, ROW_TILE)

    def body(oa_ref, ob_ref, oc_ref, wb_ref, *rest):
        gate_refs, bg_ref, o_ref = rest[:N_BRANCH * nh], rest[N_BRANCH * nh], rest[N_BRANCH * nh + 1]
        brs = (oa_ref[...], ob_ref[...], oc_ref[...])
        for j in range(nh):
            cols = slice(j * GATE_BLK, (j + 1) * GATE_BLK)
            acc = None
            for n in range(N_BRANCH):
                b = n * nh + j
                t = _dot(brs[n], wb_ref[n, :, cols])
                g = _sigmoid(gate_refs[b][...] + bg_ref[:, b * GATE_BLK:(b + 1) * GATE_BLK])
                acc = t * g if acc is None else acc + t * g
            o_ref[:, cols] = acc.astype(BF16)

    row = lambda i: (i, 0)
    br = pl.BlockSpec((tm, BRANCH_WIDTH), row)
    return pl.pallas_call(
        body, name=name,
        out_shape=jax.ShapeDtypeStruct((S, D), BF16),
        grid=(S // tm,),
        in_specs=[br, br, br, pl.BlockSpec((N_BRANCH, BRANCH_WIDTH, D), lambda i: (0, 0, 0))]
                 + _gate_specs(tm, D) + [pl.BlockSpec((1, N_BRANCH * D), lambda i: (0, 0))],
        out_specs=pl.BlockSpec((tm, D), row),
        compiler_params=_cp(("parallel",)),
    )(oa, ob, oc, wb, *([proj] * (N_BRANCH * nh)), bg)


def _merge_bwd(oa, ob, oc, wb, proj, bg, dmerged, *, name):
    S = oa.shape[0]
    D = wb.shape[2]
    nh = D // GATE_BLK
    tm = _tile(S, ROW_TILE)

    def body(oa_ref, ob_ref, oc_ref, wb_ref, *rest):
        gate_refs = rest[:N_BRANCH * nh]
        bg_ref, dm_ref, dgl_ref, dbg_ref = rest[N_BRANCH * nh:N_BRANCH * nh + 4]
        dt_refs = rest[N_BRANCH * nh + 4:N_BRANCH * nh + 4 + N_BRANCH]
        dbr_refs = rest[N_BRANCH * nh + 4 + N_BRANCH:]

        @pl.when(pl.program_id(0) == 0)
        def _():
            dbg_ref[...] = jnp.zeros_like(dbg_ref)

        brs = (oa_ref[...], ob_ref[...], oc_ref[...])
        for n in range(N_BRANCH):
            dbr = None
            for j in range(nh):
                cols = slice(j * GATE_BLK, (j + 1) * GATE_BLK)
                b = n * nh + j
                gcols = slice(b * GATE_BLK, (b + 1) * GATE_BLK)
                w = wb_ref[n, :, cols]
                t = _dot(brs[n], w)
                g = _sigmoid(gate_refs[b][...] + bg_ref[:, gcols])
                dm = dm_ref[:, cols]
                dt = (dm * g).astype(BF16)
                dgl = dm * t * g * (1.0 - g)
                dt_refs[n][:, cols] = dt
                dgl_ref[:, gcols] = dgl.astype(BF16)
                dbg_ref[:, gcols] += jnp.sum(dgl, axis=0, keepdims=True)
                d = _dot(dt, w, _NT)
                dbr = d if dbr is None else dbr + d
            dbr_refs[n][...] = dbr.astype(dbr_refs[n].dtype)

    row = lambda i: (i, 0)
    br = pl.BlockSpec((tm, BRANCH_WIDTH), row)
    res = pl.pallas_call(
        body, name=name,
        out_shape=(jax.ShapeDtypeStruct((S, N_BRANCH * D), BF16), jax.ShapeDtypeStruct((1, N_BRANCH * D), F32),
                   *([jax.ShapeDtypeStruct((S, D), BF16)] * N_BRANCH),
                   *[jax.ShapeDtypeStruct((S, BRANCH_WIDTH), dt) for dt in (BF16, BF16, F32)]),
        grid=(S // tm,),
        in_specs=[br, br, br, pl.BlockSpec((N_BRANCH, BRANCH_WIDTH, D), lambda i: (0, 0, 0))]
                 + _gate_specs(tm, D)
                 + [pl.BlockSpec((1, N_BRANCH * D), lambda i: (0, 0)), pl.BlockSpec((tm, D), row)],
        out_specs=(pl.BlockSpec((tm, N_BRANCH * D), row), pl.BlockSpec((1, N_BRANCH * D), lambda i: (0, 0)),
                   *([pl.BlockSpec((tm, D), row)] * N_BRANCH), *([br] * N_BRANCH)),
        compiler_params=_cp(("arbitrary",)),
    )(oa, ob, oc, wb, *([proj] * (N_BRANCH * nh)), bg, dmerged)
    return res[0], res[1], list(res[2:2 + N_BRANCH]), list(res[2 + N_BRANCH:])


X_SCALE = 1.0 / math.sqrt(X_HEAD_DIM)
X_W = X_HEADS * X_HEAD_DIM


def _xattn_fwd(q, kv, *, name):
    S = q.shape[0]
    M = kv.shape[0]
    tq = _tile(S, 512)

    def body(q_ref, kv_ref, o_ref, lse_ref):
        for h in range(X_HEADS):
            cols = slice(h * LANES, (h + 1) * LANES)
            s = _dot(q_ref[:, cols], kv_ref[:, cols], _NT) * X_SCALE
            m = jnp.max(s, axis=-1, keepdims=True)
            e = jnp.exp(s - m)
            l = jnp.sum(e, axis=-1, keepdims=True)
            p = (e * (1.0 / l)).astype(BF16)
            o_ref[:, cols] = _dot(p, kv_ref[:, X_W + h * LANES:X_W + (h + 1) * LANES]).astype(BF16)
            lse_ref[h] = m + jnp.log(l)

    return pl.pallas_call(
        body, name=name,
        out_shape=(jax.ShapeDtypeStruct((S, X_W), BF16), jax.ShapeDtypeStruct((X_HEADS, S, 1), F32)),
        grid=(S // tq,),
        in_specs=[pl.BlockSpec((tq, X_W), lambda i: (i, 0)), pl.BlockSpec((M, 2 * X_W), lambda i: (0, 0))],
        out_specs=(pl.BlockSpec((tq, X_W), lambda i: (i, 0)), pl.BlockSpec((X_HEADS, tq, 1), lambda i: (0, i, 0))),
        compiler_params=_cp(("parallel",)),
    )(q, kv)


def _xattn_bwd(q, kv, lse, do, *, name):
    S = q.shape[0]
    M = kv.shape[0]
    tq = _tile(S, 512)

    def body(q_ref, kv_ref, lse_ref, do_ref, dq_ref, dkv_ref):
        @pl.when(pl.program_id(0) == 0)
        def _():
            dkv_ref[...] = jnp.zeros_like(dkv_ref)

        for h in range(X_HEADS):
            cols = slice(h * LANES, (h + 1) * LANES)
            vcols = slice(X_W + h * LANES, X_W + (h + 1) * LANES)
            qh, kh, vh = q_ref[:, cols], kv_ref[:, cols], kv_ref[:, vcols]
            doh = do_ref[:, cols].astype(BF16)
            p = jnp.exp(_dot(qh, kh, _NT) * X_SCALE - lse_ref[h])
            dp = _dot(doh, vh, _NT)
            delta = jnp.sum(p * dp, axis=-1, keepdims=True)
            ds = (p * (dp - delta) * X_SCALE).astype(BF16)
            dq_ref[:, cols] = _dot(ds, kh).astype(BF16)
            dkv_ref[:, cols] += _dot(ds, qh, _TN)
            dkv_ref[:, vcols] += _dot(p.astype(BF16), doh, _TN)

    q_spec = pl.BlockSpec((tq, X_W), lambda i: (i, 0))
    return pl.pallas_call(
        body, name=name,
        out_shape=(jax.ShapeDtypeStruct((S, X_W), BF16), jax.ShapeDtypeStruct((M, 2 * X_W), F32)),
        grid=(S // tq,),
        in_specs=[q_spec, pl.BlockSpec((M, 2 * X_W), lambda i: (0, 0)),
                  pl.BlockSpec((X_HEADS, tq, 1), lambda i: (0, i, 0)), q_spec],
        out_specs=(q_spec, pl.BlockSpec((M, 2 * X_W), lambda i: (0, 0))),
        compiler_params=_cp(("arbitrary",)),
    )(q, kv, lse, do)


def _shift_down(h, row):
    return jnp.where(row == 0, 0.0, pltpu.roll(h, 1, 0))


def _shift_up(h, row, S):
    return jnp.where(row == S - 1, 0.0, pltpu.roll(h, S - 1, 0))


def _conv3(h, ck, cb, row, S):
    return _shift_down(h, row) * ck[0:1] + h * ck[1:2] + _shift_up(h, row, S) * ck[2:3] + cb


def _conv_act_fwd(h, ck, cb, *, name):
    S, F2 = h.shape
    F = F2 // 2
    nt = F // LANES

    def body(ha_ref, hb_ref, cka_ref, ckb_ref, cba_ref, cbb_ref, o_ref):
        row = lax.broadcasted_iota(jnp.int32, (S, LANES), 0)
        a = _conv3(ha_ref[...], cka_ref[...], cba_ref[...], row, S)
        b = _conv3(hb_ref[...], ckb_ref[...], cbb_ref[...], row, S)
        o_ref[...] = (_gelu(a) * b).astype(BF16)

    ca = lambda j: (0, j)
    cbi = lambda j: (0, j + nt)
    return pl.pallas_call(
        body, name=name,
        out_shape=jax.ShapeDtypeStruct((S, F), BF16),
        grid=(nt,),
        in_specs=[pl.BlockSpec((S, LANES), ca), pl.BlockSpec((S, LANES), cbi), pl.BlockSpec((3, LANES), ca),
                  pl.BlockSpec((3, LANES), cbi), pl.BlockSpec((1, LANES), ca), pl.BlockSpec((1, LANES), cbi)],
        out_specs=pl.BlockSpec((S, LANES), ca),
        compiler_params=_cp(("parallel",)),
    )(h, h, ck, ck, cb, cb)


def _conv_act_bwd(h, ck, cb, dact, *, name):
    S, F2 = h.shape
    F = F2 // 2
    nt = F // LANES

    def body(ha_ref, hb_ref, cka_ref, ckb_ref, cba_ref, cbb_ref, d_ref,
             dha_ref, dhb_ref, dcka_ref, dckb_ref, dcba_ref, dcbb_ref):
        row = lax.broadcasted_iota(jnp.int32, (S, LANES), 0)
        ha, hb = ha_ref[...], hb_ref[...]
        cka, ckb = cka_ref[...], ckb_ref[...]
        a = _conv3(ha, cka, cba_ref[...], row, S)
        b = _conv3(hb, ckb, cbb_ref[...], row, S)
        d = d_ref[...]
        ga, ga_grad = _gelu_and_grad(a)
        da = d * b * ga_grad
        db = d * ga
        for dd, hh, ck_, dh_ref, dck_ref, dcb_ref in ((da, ha, cka, dha_ref, dcka_ref, dcba_ref),
                                                      (db, hb, ckb, dhb_ref, dckb_ref, dcbb_ref)):
            dcb_ref[...] = jnp.sum(dd, axis=0, keepdims=True)
            dck_ref[0:1, :] = jnp.sum(dd * _shift_down(hh, row), axis=0, keepdims=True)
            dck_ref[1:2, :] = jnp.sum(dd * hh, axis=0, keepdims=True)
            dck_ref[2:3, :] = jnp.sum(dd * _shift_up(hh, row, S), axis=0, keepdims=True)
            dh = _shift_up(dd, row, S) * ck_[0:1] + dd * ck_[1:2] + _shift_down(dd, row) * ck_[2:3]
            dh_ref[...] = dh.astype(BF16)

    ca = lambda j: (0, j)
    cbi = lambda j: (0, j + nt)
    col = pl.BlockSpec((S, LANES), ca)
    return pl.pallas_call(
        body, name=name,
        out_shape=(jax.ShapeDtypeStruct((S, F), BF16), jax.ShapeDtypeStruct((S, F), BF16),
                   jax.ShapeDtypeStruct((3, F), F32), jax.ShapeDtypeStruct((3, F), F32),
                   jax.ShapeDtypeStruct((1, F), F32), jax.ShapeDtypeStruct((1, F), F32)),
        grid=(nt,),
        in_specs=[col, pl.BlockSpec((S, LANES), cbi), pl.BlockSpec((3, LANES), ca), pl.BlockSpec((3, LANES), cbi),
                  pl.BlockSpec((1, LANES), ca), pl.BlockSpec((1, LANES), cbi), col],
        out_specs=(col, col, pl.BlockSpec((3, LANES), ca), pl.BlockSpec((3, LANES), ca),
                   pl.BlockSpec((1, LANES), ca), pl.BlockSpec((1, LANES), ca)),
        compiler_params=_cp(("parallel",)),
    )(h, h, ck, ck, cb, cb, dact)


def _layer_fwd(x, xb, memb, w, shards, tabs, seg, l):
    n = lambda s: f"L{l}_{s}"
    w = dict(w)
    qg2 = jnp.tile(w["b_q_gain"], 2)[None, :]
    kg2 = jnp.tile(w["b_k_gain"], 2)[None, :]
    proj = _mm(xb, w["w_in"], tb=True, name=n("proj"))
    aq, ak4, av4, bq, bk4, bv4 = _prep(proj, tabs, qg2, kg2, seg, name=n("prep"))
    oa, lse_a = _attn_win_fwd(aq, ak4, av4, w["a_sink"], name=n("attn_win"))
    gather = [(shards[k], l) for k in GATHERED_LATE] + ([(shards["w_in"], l + 1)] if l + 1 < DEPTH else [])
    ob, lse_b, gathered = _attn_dense_fwd(bq, bk4, bv4, gather=gather, name=n("attn_dense"))
    for k, g in zip(GATHERED_LATE, gathered):
        w[k] = _unshard(g, GATHER_AXIS[k])
    w_in_next = gathered[len(GATHERED_LATE)] if l + 1 < DEPTH else None
    oc = _gmlp_fwd(proj, w["c_ws"], w["c_bs3"], w["c_ln_g"], w["c_ln_b"], name=n("gmlp"))
    merged = _merge_fwd(oa, ob, oc, w["w_branch"], proj, w["b_gate"], name=n("merge"))
    x1, x1b, xh1, rs1 = _mm_res_ln(merged, w["w_mix_out"], x, w["ln1_g"], w["ln1_b"], name=n("mix_ln1"))
    xq = _mm(x1b, w["x_wq"], out_dtype=BF16, name=n("xq"))
    xkv = _mm(memb, w["x_wkv"], out_dtype=BF16, name=n("xkv"))
    xo, lse_x = _xattn_fwd(xq, xkv, name=n("xattn"))
    x2, x2b, xh2, rs2 = _mm_res_ln(xo, w["x_wo"], x1, w["ln2_g"], w["ln2_b"], name=n("xo_ln2"))
    h = _mm(x2b, w["f_w_up"], tb=True, name=n("ffn_up"))
    act = _conv_act_fwd(h, w["f_conv_k"], w["f_conv_b"], name=n("conv_act"))
    x3, x3b, xh3, rs3 = _mm_res_ln(act, w["f_w_down"], x2, w["ln3_g"], w["ln3_b"], name=n("down_ln3"))
    saved = dict(xb=xb, proj=proj, aq=aq, ak4=ak4, av4=av4, bq=bq, bk4=bk4, bv4=bv4, lse_a=lse_a, lse_b=lse_b,
                 oa=oa, ob=ob, oc=oc, merged=merged, xh1=xh1, rs1=rs1, x1b=x1b, xq=xq, xkv=xkv, xo=xo, lse_x=lse_x,
                 xh2=xh2, rs2=rs2, x2b=x2b, h=h, act=act, xh3=xh3, rs3=rs3, qg2=qg2, kg2=kg2)
    return x3, x3b, saved, w, w_in_next


def _layer_bwd(top, memb, w, sv, tabs, seg, l, ln_below, dw_in_above, recv):
    n = lambda s: f"L{l}_{s}"
    g = {}
    big = {}
    recv = dict(recv)

    def dw(key, a, b, tag):
        big[key] = _mm(a, b, ta=True, out_dtype=BF16, name=n(tag))

    def dw_t(key, segments, x, tag):
        buf = jax.ShapeDtypeStruct((sum(s.shape[1] for s in segments), x.shape[1]), BF16)
        row = 0
        for i, s in enumerate(segments):
            buf = _mm(s, x, ta=True, into=(buf, row), name=n(f"{tag}{i}"))
            row += s.shape[1]
        big[key] = buf

    dz3, dz3b, g["ln3_g"], g["ln3_b"] = top
    dw("f_w_down", sv["act"], dz3b, "dw_down")
    dact = _mm(dz3b, w["f_w_down"], tb=True, name=n("dact"))
    dha, dhb, dcka, dckb, dcba, dcbb = _conv_act_bwd(sv["h"], w["f_conv_k"], w["f_conv_b"], dact, name=n("conv_act_bwd"))
    g["f_conv_k"] = jnp.concatenate([dcka, dckb], axis=1)
    g["f_conv_b"] = jnp.concatenate([dcba, dcbb], axis=1)[0]
    dw_t("f_w_up", [dha, dhb], sv["x2b"], "dw_up")
    dz2, dz2b, g["ln2_g"], g["ln2_b"] = _mm([dha, dhb], w["f_w_up"], res=dz3, res_scale=ALPHA,
                                            ln_bwd=(sv["xh2"], sv["rs2"], w["ln2_g"]), name=n("dx2_ln2"))
    dw("x_wo", sv["xo"], dz2b, "dw_xo")
    dxo = _mm(dz2b, w["x_wo"], tb=True, out_dtype=BF16, name=n("dxo"))
    dxq, dxkv = _xattn_bwd(sv["xq"], sv["xkv"], sv["lse_x"], dxo, name=n("xattn_bwd"))
    dw("x_wq", sv["x1b"], dxq, "dw_xq")
    dw("x_wkv", memb, dxkv, "dw_xkv")
    dz1, dz1b, g["ln1_g"], g["ln1_b"] = _mm(dxq, w["x_wq"], tb=True, res=dz2, res_scale=ALPHA,
                                            ln_bwd=(sv["xh1"], sv["rs1"], w["ln1_g"]), name=n("dx1_ln1"))
    dw("w_mix_out", sv["merged"], dz1b, "dw_mix")
    dmerged = _mm(dz1b, w["w_mix_out"], tb=True, name=n("dmerged"))
    dgl, dbg, dt, dbr = _merge_bwd(sv["oa"], sv["ob"], sv["oc"], w["w_branch"], sv["proj"], w["b_gate"], dmerged,
                                   name=n("merge_bwd"))
    g["b_gate"] = dbg[0]
    for i, k in enumerate(("oa", "ob", "oc")):
        dw(f"w_branch{i}", sv[k], dt[i], f"dw_branch{i}")
    big["w_branch"] = jnp.stack([big.pop(f"w_branch{i}") for i in range(N_BRANCH)])
    dqa, dka, dva, dsink = _attn_win_bwd(sv["aq"], sv["ak4"], sv["av4"], w["a_sink"], sv["lse_a"], dbr[0],
                                         name=n("attn_win_bwd"))
    g["a_sink"] = dsink[:, 0]
    sent = [k for k in BIG if k != "w_in"]
    scatter = [(_reshard(big[k], BIG_AXIS[k]), recv[k], l) for k in sent]
    if dw_in_above is not None:
        sent.append("w_in")
        scatter.append((_reshard(dw_in_above, BIG_AXIS["w_in"]), recv["w_in"], l + 1))
    dqb, dkb, dvb, got = _attn_dense_bwd(sv["bq"], sv["bk4"], sv["bv4"], sv["lse_b"], dbr[1], scatter=scatter,
                                         name=n("attn_dense_bwd"))
    recv.update(zip(sent, got))
    dcz, g["c_ws"], dbs3, dlg, dlb = _gmlp_bwd(sv["proj"], dbr[2], w["c_ws"], w["c_bs3"], w["c_ln_g"], w["c_ln_b"],
                                               name=n("gmlp_bwd"))
    g["c_bs"] = dbs3[:, :, 0]
    g["c_ln_g"], g["c_ln_b"] = dlg[0], dlb[0]
    dqkv, dqg, dkg = _unprep(dqa, dka, dva, dqb, dkb, dvb, sv["proj"], tabs, sv["qg2"], sv["kg2"], seg, name=n("unprep"))
    g["b_q_gain"] = dqg[0, :HEAD_DIM] + dqg[0, HEAD_DIM:]
    g["b_k_gain"] = dkg[0, :HEAD_DIM] + dkg[0, HEAD_DIM:]
    dw_t("w_in", [dqkv, dcz, dgl], sv["xb"], "dw_in")
    dx0 = _mm([dqkv, dcz, dgl], w["w_in"], res=dz1, res_scale=ALPHA, name=n("dx0"))
    if ln_below is not None:
        dx0 = _ln_bwd(dx0, *ln_below, name=n("ln_bwd_below"))
    for k in ("ln1_g", "ln1_b", "ln2_g", "ln2_b", "ln3_g", "ln3_b"):
        g[k] = g[k][0]
    return dx0, g, big["w_in"], recv


WEIGHTS = ("w_in", "b_gate", "a_sink", "b_q_gain", "b_k_gain", "c_ln_g", "c_ln_b", "c_ws", "c_bs", "w_branch",
           "w_mix_out", "ln1_g", "ln1_b", "x_wq", "x_wkv", "x_wo", "ln2_g", "ln2_b", "f_w_up", "f_conv_k",
           "f_conv_b", "f_w_down", "ln3_g", "ln3_b")
TRANSPOSED = ("w_in", "f_w_up")
BIG_AXIS = {"w_in": 0, "w_branch": 2, "w_mix_out": 0, "x_wq": 0, "x_wkv": 0, "x_wo": 1, "f_w_up": 0, "f_w_down": 0}
BIG = tuple(BIG_AXIS)
GATHERED = BIG + ("f_conv_k",)
GATHERED_LATE = tuple(k for k in GATHERED if k != "w_in")
GATHER_AXIS = dict(BIG_AXIS, f_conv_k=1)
SMALL = tuple(k for k in WEIGHTS if k not in GATHERED)


def _unshard(g, axis):
    t = jnp.moveaxis(g, 0, axis)
    return t.reshape(t.shape[:axis] + (t.shape[axis] * t.shape[axis + 1],) + t.shape[axis + 2:])


def _reshard(full, axis):
    t = full.reshape(full.shape[:axis] + (N_DEV, full.shape[axis] // N_DEV) + full.shape[axis + 1:])
    return jnp.moveaxis(t, axis, 0)


def _small_weights(small, l):
    w = {k: v[l] for k, v in small.items()}
    for k in ("c_ln_g", "c_ln_b", "ln1_g", "ln1_b", "ln2_g", "ln2_b", "ln3_g", "ln3_b", "b_gate", "f_conv_b"):
        w[k] = w[k][None, :]
    w["c_bs3"] = w["c_bs"][:, :, None]
    w["c_ws"] = w["c_ws"].astype(BF16)
    return w


def _local_step(x, mem, target, small, shards):
    S = x.shape[0]
    tabs = _rope_tables(S)
    seg = _seg_matrix()
    memb = mem.astype(BF16)
    xb = x.astype(BF16)
    saved, weights = [], []
    w_in_g = _gather_call([(shards["w_in"], 0)], name="gather_w_in_L0")[0]
    for l in range(DEPTH):
        w = dict(_small_weights(small, l), w_in=_unshard(w_in_g, GATHER_AXIS["w_in"]))
        x, xb, sv, w, w_in_g = _layer_fwd(x, xb, memb, w, shards, tabs, seg, l)
        saved.append(sv)
        weights.append(w)
    dy, loss = _loss_head(x, target, name="loss_head")
    grads = [None] * DEPTH
    recv = {k: jax.ShapeDtypeStruct((DEPTH, N_DEV) + shards[k].shape[1:], BF16) for k in BIG}
    dw_in = None
    last_ln = lambda l: (saved[l]["xh3"], saved[l]["rs3"], weights[l]["ln3_g"])
    top = _ln_bwd(dy, *last_ln(DEPTH - 1), name="ln_bwd_top")
    for l in reversed(range(DEPTH)):
        top, grads[l], dw_in, recv = _layer_bwd(top, memb, weights[l], saved[l], tabs, seg, l,
                                                last_ln(l - 1) if l > 0 else None, dw_in, recv)
    recv["w_in"] = _scatter_call([(_reshard(dw_in, BIG_AXIS["w_in"]), recv["w_in"], 0)], name="scatter_w_in_L0")[0]
    return loss, top, grads, [recv[k] for k in BIG]


PACK_W = 1024


def _gather_call(gather, *, name):
    na = len(gather)

    def body(*refs):
        start, finish = _gather_plan([(refs[a], gather[a][1], refs[na + a]) for a in range(na)], *refs[2 * na:])
        start()
        finish()

    return list(pl.pallas_call(
        body, name=name,
        out_shape=[_gathered_shape(x) for x, _ in gather],
        in_specs=[_ANY] * na, out_specs=[_ANY] * na,
        scratch_shapes=_comm_scratch(na),
    )(*[x for x, _ in gather]))


def _scatter_io(scatter):
    held = [a for a, (_, r, _) in enumerate(scatter) if not isinstance(r, jax.ShapeDtypeStruct)]
    return ([s for s, _, _ in scatter] + [scatter[a][1] for a in held],
            [jax.ShapeDtypeStruct(r.shape, r.dtype) for _, r, _ in scatter], held)


def _scatter_call(scatter, *, name):
    na = len(scatter)
    operands, out_shape, held = _scatter_io(scatter)
    n_in = len(operands)

    def body(*refs):
        start, finish = _scatter_plan([(refs[a], refs[n_in + a], scatter[a][2]) for a in range(na)],
                                      *refs[n_in + na:])
        start()
        finish()

    return list(pl.pallas_call(
        body, name=name,
        out_shape=out_shape,
        in_specs=[_ANY] * n_in, out_specs=[_ANY] * na,
        scratch_shapes=_comm_scratch(na),
        input_output_aliases={na + i: a for i, a in enumerate(held)},
    )(*operands))


def _sum_parts(parts, *, name):
    P, R, C = parts.shape
    tr = _tile(R, 64, align=8)

    def body(p_ref, o_ref):
        g = p_ref[0].astype(F32)
        for s in range(1, P):
            g = g + p_ref[s].astype(F32)
        o_ref[...] = g

    return pl.pallas_call(
        body, name=name, out_shape=jax.ShapeDtypeStruct((R, C), F32), grid=(R // tr,),
        in_specs=[pl.BlockSpec((P, tr, C), lambda i: (0, i, 0))], out_specs=pl.BlockSpec((tr, C), lambda i: (i, 0)),
        compiler_params=_cp(("parallel",)),
    )(parts)


ADAM_BLOCK_ELEMS = 512 * 1024


def _adamw(parts, w, m, v, *, name):
    L, P, R, C = parts.shape
    assert w.shape == (L, R, C), (parts.shape, w.shape)
    tr = _tile(R, max(16, ADAM_BLOCK_ELEMS // C), align=16)

    def body(p_ref, w_ref, m_ref, v_ref, g_ref, d_ref, nm_ref, nv_ref):
        g = p_ref[0].astype(F32)
        for s in range(1, P):
            g = g + p_ref[s].astype(F32)
        g_ref[...] = g
        d_ref[...], nm_ref[...], nv_ref[...] = _adam_update(g, w_ref[...], m_ref[...], v_ref[...])

    blk = pl.BlockSpec((None, tr, C), lambda l, i: (l, i, 0))
    shp = jax.ShapeDtypeStruct((L, R, C), F32)
    return pl.pallas_call(
        body, name=name, out_shape=(shp, shp, shp, shp), grid=(L, R // tr),
        in_specs=[pl.BlockSpec((None, P, tr, C), lambda l, i: (l, 0, i, 0)), blk, blk, blk],
        out_specs=(blk, blk, blk, blk),
        compiler_params=_cp(("parallel", "parallel")),
    )(parts, w, m, v)


def _adam_update(g, w, m, v):
    nm = ADAM_B1 * m + (1.0 - ADAM_B1) * g
    nv = ADAM_B2 * v + (1.0 - ADAM_B2) * (g * g)
    m_hat = nm / (1.0 - ADAM_B1 ** ADAM_STEP)
    v_hat = nv / (1.0 - ADAM_B2 ** ADAM_STEP)
    return -ADAM_LR * (m_hat / (jnp.sqrt(v_hat) + ADAM_EPS) + ADAM_WD * w), nm, nv


def _adamw_small(gs, ws, ms, vs, *, name):
    n = len(gs)

    def body(*refs):
        for a in range(n):
            g, w, m, v = (refs[k * n + a][...] for k in range(4))
            d, nm, nv = _adam_update(g, w, m, v)
            refs[4 * n + a][...] = d
            refs[5 * n + a][...] = nm
            refs[6 * n + a][...] = nv

    shapes = [jax.ShapeDtypeStruct(w.shape, F32) for w in ws]
    res = pl.pallas_call(body, name=name, out_shape=shapes * 3, compiler_params=_cp())(*gs, *ws, *ms, *vs)
    return res[:n], res[n:2 * n], res[2 * n:]


def _pad_rows(vec, width, row_align):
    n = vec.shape[0]
    rows = -(-n // width)
    rows = -(-rows // row_align) * row_align
    return jnp.pad(vec, (0, rows * width - n)).reshape(rows, width)


def kernel(x, mem, w_in, b_gate, a_sink, b_q_gain, b_k_gain, c_ln_g, c_ln_b, c_ws, c_bs, w_branch, w_mix_out, ln1_g, ln1_b, x_wq, x_wkv, x_wo, ln2_g, ln2_b, f_w_up, f_conv_k, f_conv_b, f_w_down, ln3_g, ln3_b, loss_target, m_w_in, m_b_gate, m_a_sink, m_b_q_gain, m_b_k_gain, m_c_ln_g, m_c_ln_b, m_c_ws, m_c_bs, m_w_branch, m_w_mix_out, m_ln1_g, m_ln1_b, m_x_wq, m_x_wkv, m_x_wo, m_ln2_g, m_ln2_b, m_f_w_up, m_f_conv_k, m_f_conv_b, m_f_w_down, m_ln3_g, m_ln3_b, v_w_in, v_b_gate, v_a_sink, v_b_q_gain, v_b_k_gain, v_c_ln_g, v_c_ln_b, v_c_ws, v_c_bs, v_w_branch, v_w_mix_out, v_ln1_g, v_ln1_b, v_x_wq, v_x_wkv, v_x_wo, v_ln2_g, v_ln2_b, v_f_w_up, v_f_conv_k, v_f_conv_b, v_f_w_down, v_ln3_g, v_ln3_b):
    w = dict(w_in=w_in, b_gate=b_gate, a_sink=a_sink, b_q_gain=b_q_gain, b_k_gain=b_k_gain, c_ln_g=c_ln_g,
             c_ln_b=c_ln_b, c_ws=c_ws, c_bs=c_bs, w_branch=w_branch, w_mix_out=w_mix_out, ln1_g=ln1_g, ln1_b=ln1_b,
             x_wq=x_wq, x_wkv=x_wkv, x_wo=x_wo, ln2_g=ln2_g, ln2_b=ln2_b, f_w_up=f_w_up, f_conv_k=f_conv_k,
             f_conv_b=f_conv_b, f_w_down=f_w_down, ln3_g=ln3_g, ln3_b=ln3_b)
    m = dict(w_in=m_w_in, b_gate=m_b_gate, a_sink=m_a_sink, b_q_gain=m_b_q_gain, b_k_gain=m_b_k_gain,
             c_ln_g=m_c_ln_g, c_ln_b=m_c_ln_b, c_ws=m_c_ws, c_bs=m_c_bs, w_branch=m_w_branch, w_mix_out=m_w_mix_out,
             ln1_g=m_ln1_g, ln1_b=m_ln1_b, x_wq=m_x_wq, x_wkv=m_x_wkv, x_wo=m_x_wo, ln2_g=m_ln2_g, ln2_b=m_ln2_b,
             f_w_up=m_f_w_up, f_conv_k=m_f_conv_k, f_conv_b=m_f_conv_b, f_w_down=m_f_w_down, ln3_g=m_ln3_g,
             ln3_b=m_ln3_b)
    v = dict(w_in=v_w_in, b_gate=v_b_gate, a_sink=v_a_sink, b_q_gain=v_b_q_gain, b_k_gain=v_b_k_gain,
             c_ln_g=v_c_ln_g, c_ln_b=v_c_ln_b, c_ws=v_c_ws, c_bs=v_c_bs, w_branch=v_w_branch, w_mix_out=v_w_mix_out,
             ln1_g=v_ln1_g, ln1_b=v_ln1_b, x_wq=v_x_wq, x_wkv=v_x_wkv, x_wo=v_x_wo, ln2_g=v_ln2_g, ln2_b=v_ln2_b,
             f_w_up=v_f_w_up, f_conv_k=v_f_conv_k, f_conv_b=v_f_conv_b, f_w_down=v_f_w_down, ln3_g=v_ln3_g,
             ln3_b=v_ln3_b)
    me = 4 * lax.axis_index("x") + 2 * lax.axis_index("y") + lax.axis_index("c")

    def held(k, t):
        return jnp.swapaxes(t, 1, 2) if k in TRANSPOSED else t

    shards = dict({k: held(k, w[k]).astype(BF16) for k in BIG}, f_conv_k=w["f_conv_k"])
    loss, grad_x, grads, recvs = _local_step(x[0], mem[0], loss_target[0], {k: w[k] for k in SMALL}, shards)
    loss = lax.psum(loss[0, 0], ("x", "y", "c"))

    out_g, out_d, out_m, out_v = {}, {}, {}, {}
    for k, recv in zip(BIG, recvs):
        shp = held(k, w[k]).shape
        rc = (DEPTH, math.prod(shp[1:-1]), shp[-1])
        parts = recv.reshape((DEPTH, N_DEV) + rc[1:])
        g_, d_, m_, v_ = _adamw(parts, held(k, w[k]).reshape(rc), held(k, m[k]).reshape(rc),
                                held(k, v[k]).reshape(rc), name=f"adamw_{k}")
        out_g[k], out_d[k], out_m[k], out_v[k] = (held(k, t.reshape(shp)) for t in (g_, d_, m_, v_))

    small_all = SMALL + ("f_conv_k",)
    gfull = {k: jnp.stack([grads[l][k] for l in range(DEPTH)]) for k in small_all}

    def pack(d):
        return jnp.concatenate([_pad_rows(d[k].reshape(-1), PACK_W, 8) for k in small_all])

    def unpack(rows, like):
        out, r = {}, 0
        for k in small_all:
            nr = -(-like[k].size // (8 * PACK_W)) * 8
            out[k] = rows[r:r + nr].reshape(-1)[:like[k].size].reshape(like[k].shape)
            r += nr
        return out

    gathered = _gather_call([(pack(gfull)[None], 0)], name="gather_small_grads")[0]
    sg = unpack(_sum_parts(gathered, name="sum_small_grads"), gfull)
    width = w["f_conv_k"].shape[2]
    sg["f_conv_k"] = lax.dynamic_slice_in_dim(sg["f_conv_k"], me * width, width, axis=2)
    ud, um, uv = _adamw_small(*[[d[k] for k in small_all] for d in (sg, w, m, v)], name="adamw_small")
    for i, k in enumerate(small_all):
        out_g[k], out_d[k], out_m[k], out_v[k] = sg[k], ud[i], um[i], uv[i]

    return (loss, grad_x[None], *[out_g[k] for k in WEIGHTS], *[out_d[k] for k in WEIGHTS],
            *[out_m[k] for k in WEIGHTS], *[out_v[k] for k in WEIGHTS])
```

```python
import functools
import math

import jax
import jax.numpy as jnp
from jax import lax
from jax.experimental import pallas as pl
from jax.experimental.pallas import tpu as pltpu

F32 = jnp.float32
BF16 = jnp.bfloat16

DEPTH = 4
HEAD_DIM = 64
BLOCK = 128
WINDOW = 128
GRID_W = 64
C_WIDTH = 512
C_GROUPS = 4
CHUNK = 128
N_BRANCH = 3
BRANCH_WIDTH = 512
ROPE_THETA = 10000.0
X_HEADS = 4
X_HEAD_DIM = 128
ALPHA = (2 * DEPTH) ** 0.25
LN_EPS = 1e-5
RMS_EPS = 1e-6
ADAM_LR = 0.001
ADAM_B1 = 0.9
ADAM_B2 = 0.999
ADAM_EPS = 1e-08
ADAM_WD = 0.01
ADAM_STEP = 10
N_DEV = 8

COL_A = 0
COL_B = 768
COL_C = 1536
COL_GATE = 2560
QKV_W = 768

LANES = 128
V7X_VMEM_BYTES = 64 * 1024 * 1024
VMEM_LIMIT = V7X_VMEM_BYTES - 8 * 1024 * 1024
NEG_BIG = -1e30
ROW_TILE = 512

_NT = (((1,), (1,)), ((), ()))
_TN = (((0,), (0,)), ((), ()))
_NN = (((1,), (0,)), ((), ()))


def _cp(sem=None):
    return pltpu.CompilerParams(dimension_semantics=sem, vmem_limit_bytes=VMEM_LIMIT)


def _tile(n, target, align=LANES):
    if n <= target:
        return n
    best = None
    for t in range(align, target + 1, align):
        if n % t == 0:
            best = t
    assert best is not None, (n, target)
    return best


def _dot(a, b, dims=_NN):
    return lax.dot_general(a, b, dims, preferred_element_type=F32)


def _gelu(x):
    return 0.5 * x * (1.0 + lax.erf(x * 0.7071067811865476))


def _gelu_and_grad(x):
    cdf = 0.5 * (1.0 + lax.erf(x * 0.7071067811865476))
    return x * cdf, cdf + x * jnp.exp(-0.5 * x * x) * 0.3989422804014327


def _sigmoid(x):
    return 1.0 / (1.0 + jnp.exp(-x))


MESH_ID = pl.DeviceIdType.MESH
_ANY = pl.BlockSpec(memory_space=pl.ANY)
COPIES_PER_ARRAY = N_DEV - 1


def _comm_scratch(n_arrays):
    return [pltpu.SemaphoreType.DMA((COPIES_PER_ARRAY * n_arrays,)),
            pltpu.SemaphoreType.DMA((COPIES_PER_ARRAY * n_arrays,)), pltpu.SemaphoreType.DMA((n_arrays,))]


def _gathered_shape(x):
    return jax.ShapeDtypeStruct((N_DEV,) + x.shape[1:], x.dtype)


def _gather_plan(entries, send_sems, recv_sems, local_sems):
    mx, my, mc = lax.axis_index("x"), lax.axis_index("y"), lax.axis_index("c")
    me, sibling = (mx, my, mc), (mx, my, 1 - mc)
    chips = [(1 - mx, my), (mx, 1 - my), (1 - mx, 1 - my)]

    def copy(a, k, block, to, from_shard=False):
        x_ref, l, out_ref = entries[a]
        dst = out_ref.at[4 * block[0] + 2 * block[1] + block[2]]
        return pltpu.make_async_remote_copy(
            src_ref=x_ref.at[l] if from_shard else dst, dst_ref=dst,
            send_sem=send_sems.at[COPIES_PER_ARRAY * a + k], recv_sem=recv_sems.at[COPIES_PER_ARRAY * a + k],
            device_id=to, device_id_type=MESH_ID)

    def own(a):
        x_ref, l, out_ref = entries[a]
        return pltpu.make_async_copy(x_ref.at[l], out_ref.at[4 * mx + 2 * my + mc], local_sems.at[a])

    def first(a):
        return [copy(a, 0, me, sibling, True)] + [copy(a, 1 + j, me, (*chip, mc), True) for j, chip in enumerate(chips)]

    def passed(a):
        return [copy(a, 4 + j, (*chip, mc), sibling) for j, chip in enumerate(chips)]

    def start():
        for a in range(len(entries)):
            own(a).start()
            for cp in first(a):
                cp.start()

    def finish():
        for a in range(len(entries)):
            fwd = passed(a)
            for j, chip in enumerate(chips):
                copy(a, 1 + j, (*chip, mc), me).wait_recv()
                fwd[j].start()
        for a in range(len(entries)):
            copy(a, 0, sibling, me).wait_recv()
            for j, chip in enumerate(chips):
                copy(a, 4 + j, (*chip, 1 - mc), me).wait_recv()
            for cp in first(a) + passed(a):
                cp.wait_send()
            own(a).wait()

    return start, finish


def _scatter_plan(entries, send_sems, recv_sems, local_sems):
    mx, my, mc = lax.axis_index("x"), lax.axis_index("y"), lax.axis_index("c")
    me = 4 * mx + 2 * my + mc

    def src(a, dev):
        return entries[a][0].at[dev]

    def copies(a):
        _, recv_ref, lr = entries[a]
        out = []
        for k in range(1, N_DEV):
            px = 1 - mx if k & 4 else mx
            py = 1 - my if k & 2 else my
            pc = 1 - mc if k & 1 else mc
            peer = 4 * px + 2 * py + pc
            sems = dict(send_sem=send_sems.at[COPIES_PER_ARRAY * a + k - 1],
                        recv_sem=recv_sems.at[COPIES_PER_ARRAY * a + k - 1],
                        device_id=(px, py, pc), device_id_type=MESH_ID)
            sends = pltpu.make_async_remote_copy(src_ref=src(a, peer), dst_ref=recv_ref.at[lr, me], **sems)
            lands = pltpu.make_async_remote_copy(src_ref=src(a, me), dst_ref=recv_ref.at[lr, peer], **sems)
            out.append((sends, lands))
        return out

    def own(a):
        _, recv_ref, lr = entries[a]
        return pltpu.make_async_copy(src(a, me), recv_ref.at[lr, me], local_sems.at[a])

    def start():
        for a in range(len(entries)):
            own(a).start()
            for sends, _ in copies(a):
                sends.start()

    def finish():
        for a in range(len(entries)):
            for _, lands in copies(a):
                lands.wait_recv()
        for a in range(len(entries)):
            for sends, _ in copies(a):
                sends.wait_send()
            own(a).wait()

    return start, finish


MM_TILE, MM_TK = 1536, 2048
MM_TILE_LN = 512


def _mm(a, b, *, ta=False, tb=False, out_dtype=F32, res=None, res_scale=1.0, into=None, ln_bwd=None, name):
    segs = list(a) if isinstance(a, (list, tuple)) else [a]
    if ta:
        (K, M), seg_k = segs[0].shape, [segs[0].shape[0]]
        assert len(segs) == 1
    else:
        M, seg_k = segs[0].shape[0], [s.shape[1] for s in segs]
        K = sum(seg_k)
    if tb:
        N, Kb = b.shape
    else:
        Kb, N = b.shape
    assert K == Kb, ([s.shape for s in segs], b.shape, ta, tb)
    row_off = into[1] if into is not None else 0
    tm, tn = _tile(math.gcd(M, row_off), MM_TILE if ln_bwd is None else MM_TILE_LN), _tile(N, MM_TILE)
    tk = _tile(K, MM_TK) if len(segs) == 1 else _tile(math.gcd(*seg_k), MM_TILE)
    nk = K // tk
    seg_chunks = [ks // tk for ks in seg_k]
    seg_first = [sum(seg_chunks[:s]) for s in range(len(segs))]
    dims = (((0 if ta else 1,), (1 if tb else 0,)), ((), ()))
    ns = len(segs)
    n_res = ns + 1
    n_ln = n_res + (res is not None)
    into_held = into is not None and not isinstance(into[0], jax.ShapeDtypeStruct)
    n_in = n_ln + (3 if ln_bwd is not None else 0) + into_held
    assert ln_bwd is None or (tn == N and into is None)

    def body(*refs):
        a_refs, b_ref = refs[:ns], refs[ns]
        r_ref = refs[n_res] if res is not None else None
        o_ref = refs[n_in]
        first_row_tile = pl.program_id(0) == 0

        def finish(out):
            if r_ref is not None:
                out = out + res_scale * r_ref[...]
            if ln_bwd is None:
                o_ref[...] = out.astype(o_ref.dtype)
                return
            xh_ref, rs_ref, g_ref = refs[n_ln:n_ln + 3]
            ob_ref, dg_ref, db_ref = refs[n_in + 1:n_in + 4]

            @pl.when(first_row_tile)
            def _():
                dg_ref[...] = jnp.zeros_like(dg_ref)
                db_ref[...] = jnp.zeros_like(db_ref)

            xh = xh_ref[...]
            dxh = out * g_ref[...]
            m1 = jnp.mean(dxh, axis=-1, keepdims=True)
            m2 = jnp.mean(dxh * xh, axis=-1, keepdims=True)
            dz = rs_ref[...] * (dxh - m1 - xh * m2)
            o_ref[...] = dz
            ob_ref[...] = dz.astype(BF16)
            dg_ref[...] += jnp.sum(out * xh, axis=0, keepdims=True)
            db_ref[...] += jnp.sum(out, axis=0, keepdims=True)

        def prod(s):
            return _dot(a_refs[s][...].astype(BF16), b_ref[...].astype(BF16), dims)

        if nk == 1:
            finish(prod(0))
            return
        acc = refs[n_in + (4 if ln_bwd is not None else 1)]
        k = pl.program_id(2)

        @pl.when(k == 0)
        def _():
            acc[...] = jnp.zeros_like(acc)

        for s in range(ns):
            def add(s=s):
                acc[...] += prod(s)
            pl.when((k >= seg_first[s]) & (k < seg_first[s] + seg_chunks[s]))(add)

        @pl.when(k == nk - 1)
        def _():
            finish(acc[...])

    if ta:
        a_specs = [pl.BlockSpec((tk, tm), lambda i, j, k: (k, i))]
    else:
        a_specs = [pl.BlockSpec((tm, tk), functools.partial(
            lambda i, j, k, first, n: (i, jnp.clip(k - first, 0, n - 1)), first=seg_first[s], n=seg_chunks[s]))
            for s in range(ns)]
    b_spec = pl.BlockSpec((tn, tk), lambda i, j, k: (j, k)) if tb else pl.BlockSpec((tk, tn), lambda i, j, k: (k, j))
    in_specs = a_specs + [b_spec]
    args = segs + [b]
    if res is not None:
        in_specs.append(pl.BlockSpec((tm, tn), lambda i, j, k: (i, j)))
        args.append(res)
    out_spec = pl.BlockSpec((tm, tn), lambda i, j, k: (i, j))
    if ln_bwd is not None:
        xh, rs, g = ln_bwd
        in_specs += [out_spec, pl.BlockSpec((tm, 1), lambda i, j, k: (i, 0)), pl.BlockSpec((1, tn), lambda i, j, k: (0, j))]
        args += [xh, rs, g]
        vec = pl.BlockSpec((1, tn), lambda i, j, k: (0, j))
        return pl.pallas_call(
            body, name=name,
            out_shape=(jax.ShapeDtypeStruct((M, N), F32), jax.ShapeDtypeStruct((M, N), BF16),
                       jax.ShapeDtypeStruct((1, N), F32), jax.ShapeDtypeStruct((1, N), F32)),
            grid=(M // tm, N // tn, nk),
            in_specs=in_specs,
            out_specs=(out_spec, out_spec, vec, vec),
            scratch_shapes=[pltpu.VMEM((tm, tn), F32)] if nk > 1 else [],
            compiler_params=_cp(("arbitrary", "arbitrary", "arbitrary")),
        )(*args)
    if into is None:
        out_shape = jax.ShapeDtypeStruct((M, N), out_dtype)
        blk_off, aliases = 0, {}
    else:
        buf = into[0]
        assert buf.shape[1] == N and row_off % tm == 0 and row_off + M <= buf.shape[0], (buf.shape, M, N, row_off)
        out_shape = jax.ShapeDtypeStruct(buf.shape, buf.dtype)
        blk_off, aliases = row_off // tm, {}
        if into_held:
            aliases = {n_in - 1: 0}
            in_specs.append(_ANY)
            args.append(buf)
    return pl.pallas_call(
        body, name=name,
        out_shape=out_shape,
        grid=(M // tm, N // tn, nk),
        in_specs=in_specs,
        out_specs=pl.BlockSpec((tm, tn), lambda i, j, k: (i + blk_off, j)),
        scratch_shapes=[pltpu.VMEM((tm, tn), F32)] if nk > 1 else [],
        input_output_aliases=aliases,
        compiler_params=_cp(("parallel", "parallel", "arbitrary")),
    )(*args)


def _mm_res_ln(a, w, x, g, b, *, name):
    S, K = a.shape
    D = w.shape[1]
    tm = _tile(S, ROW_TILE)

    def body(a_ref, w_ref, x_ref, g_ref, b_ref, y_ref, yb_ref, xh_ref, rs_ref):
        h = _dot(a_ref[...], w_ref[...])
        z = ALPHA * x_ref[...] + h
        mu = jnp.mean(z, axis=-1, keepdims=True)
        zc = z - mu
        var = jnp.mean(zc * zc, axis=-1, keepdims=True)
        r = lax.rsqrt(var + LN_EPS)
        xh = zc * r
        y = xh * g_ref[...] + b_ref[...]
        y_ref[...] = y
        yb_ref[...] = y.astype(BF16)
        xh_ref[...] = xh
        rs_ref[...] = r

    row = lambda i: (i, 0)
    full = lambda i: (0, 0)
    return pl.pallas_call(
        body, name=name,
        out_shape=(jax.ShapeDtypeStruct((S, D), F32), jax.ShapeDtypeStruct((S, D), BF16),
                   jax.ShapeDtypeStruct((S, D), F32), jax.ShapeDtypeStruct((S, 1), F32)),
        grid=(S // tm,),
        in_specs=[pl.BlockSpec((tm, K), row), pl.BlockSpec((K, D), full), pl.BlockSpec((tm, D), row),
                  pl.BlockSpec((1, D), full), pl.BlockSpec((1, D), full)],
        out_specs=(pl.BlockSpec((tm, D), row), pl.BlockSpec((tm, D), row), pl.BlockSpec((tm, D), row),
                   pl.BlockSpec((tm, 1), row)),
        compiler_params=_cp(("parallel",)),
    )(a, w, x, g, b)


def _ln_bwd(dy, xh, rs, g, *, name):
    S, D = dy.shape
    tm = _tile(S, ROW_TILE)

    def body(dy_ref, xh_ref, rs_ref, g_ref, dz_ref, dzb_ref, dg_ref, db_ref):
        @pl.when(pl.program_id(0) == 0)
        def _():
            dg_ref[...] = jnp.zeros_like(dg_ref)
            db_ref[...] = jnp.zeros_like(db_ref)

        dy = dy_ref[...]
        xh = xh_ref[...]
        dxh = dy * g_ref[...]
        m1 = jnp.mean(dxh, axis=-1, keepdims=True)
        m2 = jnp.mean(dxh * xh, axis=-1, keepdims=True)
        dz = rs_ref[...] * (dxh - m1 - xh * m2)
        dz_ref[...] = dz
        dzb_ref[...] = dz.astype(BF16)
        dg_ref[...] += jnp.sum(dy * xh, axis=0, keepdims=True)
        db_ref[...] += jnp.sum(dy, axis=0, keepdims=True)

    row = lambda i: (i, 0)
    full = lambda i: (0, 0)
    return pl.pallas_call(
        body, name=name,
        out_shape=(jax.ShapeDtypeStruct((S, D), F32), jax.ShapeDtypeStruct((S, D), BF16),
                   jax.ShapeDtypeStruct((1, D), F32), jax.ShapeDtypeStruct((1, D), F32)),
        grid=(S // tm,),
        in_specs=[pl.BlockSpec((tm, D), row), pl.BlockSpec((tm, D), row), pl.BlockSpec((tm, 1), row),
                  pl.BlockSpec((1, D), full)],
        out_specs=(pl.BlockSpec((tm, D), row), pl.BlockSpec((tm, D), row), pl.BlockSpec((1, D), full),
                   pl.BlockSpec((1, D), full)),
        compiler_params=_cp(("arbitrary",)),
    )(dy, xh, rs, g)


def _loss_head(y, t, *, name):
    S, D = y.shape
    tm = _tile(S, 512)

    def body(y_ref, t_ref, dy_ref, l_ref):
        @pl.when(pl.program_id(0) == 0)
        def _():
            l_ref[...] = jnp.zeros_like(l_ref)

        e = y_ref[...] - t_ref[...]
        dy_ref[...] = e / D
        l_ref[...] += 0.5 * jnp.sum(jnp.mean(e * e, axis=-1, keepdims=True), axis=0, keepdims=True)

    row = lambda i: (i, 0)
    return pl.pallas_call(
        body, name=name,
        out_shape=(jax.ShapeDtypeStruct((S, D), F32), jax.ShapeDtypeStruct((1, 1), F32)),
        grid=(S // tm,),
        in_specs=[pl.BlockSpec((tm, D), row), pl.BlockSpec((tm, D), row)],
        out_specs=(pl.BlockSpec((tm, D), row), pl.BlockSpec((1, 1), lambda i: (0, 0))),
        compiler_params=_cp(("arbitrary",)),
    )(y, t)


def _rope_tables(S):
    pos = jnp.arange(S, dtype=jnp.int32)[:, None]
    lane = jnp.arange(LANES, dtype=jnp.int32)[None, :] % HEAD_DIM

    def inv(d):
        half = d // 2
        return jnp.tile(ROPE_THETA ** (-jnp.arange(half, dtype=F32) * (2.0 / d)), LANES // half)[None, :]

    def cs(p, d):
        ang = p.astype(F32) * inv(d)
        return jnp.cos(ang), jnp.where(lane % d < d // 2, -jnp.sin(ang), jnp.sin(ang))

    ca, sa = cs(pos, HEAD_DIM)
    axial = jnp.where(lane < HEAD_DIM // 2, pos // GRID_W, pos % GRID_W)
    cb, sb = cs(axial, HEAD_DIM // 2)
    return ca, sa, cb, sb


def _partner(x, lane, width):
    h = width // 2
    return jnp.where(lane % width < h, pltpu.roll(x, LANES - h, 1), pltpu.roll(x, h, 1))


def _rope_fwd(x, c, s, lane, width):
    return x * c + _partner(x, lane, width) * s


def _rope_bwd(dy, c, s, lane, width):
    return dy * c + _partner(dy * s, lane, width)


def _head_sum(x, seg):
    return lax.dot_general(x, seg, _NN, precision=lax.Precision.HIGHEST, preferred_element_type=F32)


def _split_heads(x, lane):
    lo = lane < HEAD_DIM
    r = pltpu.roll(x, HEAD_DIM, 1)
    z = jnp.zeros_like(x)
    return jnp.where(lo, x, z), jnp.where(lo, z, r), jnp.where(lo, r, z), jnp.where(lo, z, x)


def _fold_heads(d0, d1, lane):
    t0 = d0 + pltpu.roll(d0, HEAD_DIM, 1)
    t1 = d1 + pltpu.roll(d1, HEAD_DIM, 1)
    return jnp.where(lane < HEAD_DIM, t0, t1)


def _seg_matrix():
    i = jnp.arange(LANES)
    return (i[:, None] // HEAD_DIM == i[None, :] // HEAD_DIM).astype(F32)


def _prep(proj, tabs, qg2, kg2, seg, *, name):
    S = proj.shape[0]
    ts = _tile(S, ROW_TILE)
    ca, sa, cb, sb = tabs

    def body(pa_ref, pb_ref, ca_ref, sa_ref, cb_ref, sb_ref, qg_ref, kg_ref, seg_ref,
             aq_ref, ak_ref, av_ref, bq_ref, bk_ref, bv_ref):
        lane = lax.broadcasted_iota(jnp.int32, (ts, LANES), 1)
        ca, sa, cb, sb = ca_ref[...], sa_ref[...], cb_ref[...], sb_ref[...]
        seg = seg_ref[...]

        def norm(x, gain):
            r = lax.rsqrt(_head_sum(x * x, seg) * (1.0 / HEAD_DIM) + RMS_EPS)
            return x * r * gain

        def put(ref, x):
            for i, part in enumerate(_split_heads(x, lane)):
                ref[i] = part.astype(BF16)

        for gidx in range(4):
            cols = slice(gidx * LANES, (gidx + 1) * LANES)
            aq_ref[:, cols] = (_rope_fwd(pa_ref[:, cols], ca, sa, lane, HEAD_DIM) * 0.125).astype(BF16)
            bq = norm(pb_ref[:, cols], qg_ref[...])
            bq_ref[:, cols] = (_rope_fwd(bq, cb, sb, lane, HEAD_DIM // 2) * 0.125).astype(BF16)
        put(ak_ref, _rope_fwd(pa_ref[:, 512:640], ca, sa, lane, HEAD_DIM))
        put(av_ref, pa_ref[:, 640:768])
        bk = norm(pb_ref[:, 512:640], kg_ref[...])
        put(bk_ref, _rope_fwd(bk, cb, sb, lane, HEAD_DIM // 2))
        put(bv_ref, pb_ref[:, 640:768])

    row = lambda i: (i, 0)
    full = lambda i: (0, 0)
    tab = pl.BlockSpec((ts, LANES), row)
    kv_shape = jax.ShapeDtypeStruct((4, S, LANES), BF16)
    kv_spec = pl.BlockSpec((4, ts, LANES), lambda i: (0, i, 0))
    q_shape = jax.ShapeDtypeStruct((S, 512), BF16)
    q_spec = pl.BlockSpec((ts, 512), row)
    return pl.pallas_call(
        body, name=name,
        out_shape=(q_shape, kv_shape, kv_shape, q_shape, kv_shape, kv_shape),
        grid=(S // ts,),
        in_specs=[pl.BlockSpec((ts, QKV_W), lambda i: (i, 0)), pl.BlockSpec((ts, QKV_W), lambda i: (i, 1)),
                  tab, tab, tab, tab, pl.BlockSpec((1, LANES), full), pl.BlockSpec((1, LANES), full),
                  pl.BlockSpec((LANES, LANES), full)],
        out_specs=(q_spec, kv_spec, kv_spec, q_spec, kv_spec, kv_spec),
        compiler_params=_cp(("parallel",)),
    )(proj, proj, ca, sa, cb, sb, qg2, kg2, seg)


def _unprep(dqa, dka, dva, dqb, dkb, dvb, proj, tabs, qg2, kg2, seg, *, name):
    S = proj.shape[0]
    ts = _tile(S, ROW_TILE)
    ca, sa, cb, sb = tabs

    def body(dqa_ref, dka_ref, dva_ref, dqb_ref, dkb_ref, dvb_ref, pb_ref, ca_ref, sa_ref, cb_ref, sb_ref,
             qg_ref, kg_ref, seg_ref, dp_ref, dqg_ref, dkg_ref):
        @pl.when(pl.program_id(0) == 0)
        def _():
            dqg_ref[...] = jnp.zeros_like(dqg_ref)
            dkg_ref[...] = jnp.zeros_like(dkg_ref)

        lane = lax.broadcasted_iota(jnp.int32, (ts, LANES), 1)
        ca, sa, cb, sb = ca_ref[...], sa_ref[...], cb_ref[...], sb_ref[...]
        seg = seg_ref[...]

        def norm_bwd(dy, x, gain):
            r = lax.rsqrt(_head_sum(x * x, seg) * (1.0 / HEAD_DIM) + RMS_EPS)
            gdy = gain * dy
            dot = _head_sum(gdy * x, seg) * (1.0 / HEAD_DIM)
            dx = r * gdy - x * (r * r * r) * dot
            return dx, jnp.sum(dy * x * r, axis=0, keepdims=True)

        for gidx in range(4):
            cols = slice(gidx * LANES, (gidx + 1) * LANES)
            dp_ref[:, cols] = _rope_bwd(dqa_ref[:, cols] * 0.125, ca, sa, lane, HEAD_DIM).astype(BF16)
            dbq = _rope_bwd(dqb_ref[:, cols] * 0.125, cb, sb, lane, HEAD_DIM // 2)
            dx, dg = norm_bwd(dbq, pb_ref[:, cols], qg_ref[...])
            dp_ref[:, COL_B + gidx * LANES:COL_B + (gidx + 1) * LANES] = dx.astype(BF16)
            dqg_ref[...] += dg
        dak = _fold_heads(dka_ref[0] + dka_ref[1], dka_ref[2] + dka_ref[3], lane)
        dp_ref[:, 512:640] = _rope_bwd(dak, ca, sa, lane, HEAD_DIM).astype(BF16)
        dp_ref[:, 640:768] = _fold_heads(dva_ref[0] + dva_ref[1], dva_ref[2] + dva_ref[3], lane).astype(BF16)
        dbk = _fold_heads(dkb_ref[0] + dkb_ref[1], dkb_ref[2] + dkb_ref[3], lane)
        dbk = _rope_bwd(dbk, cb, sb, lane, HEAD_DIM // 2)
        dx, dg = norm_bwd(dbk, pb_ref[:, 512:640], kg_ref[...])
        dp_ref[:, COL_B + 512:COL_B + 640] = dx.astype(BF16)
        dkg_ref[...] += dg
        dp_ref[:, COL_B + 640:COL_B + 768] = _fold_heads(dvb_ref[0] + dvb_ref[1], dvb_ref[2] + dvb_ref[3],
                                                         lane).astype(BF16)

    row = lambda i: (i, 0)
    full = lambda i: (0, 0)
    tab = pl.BlockSpec((ts, LANES), row)
    q_spec = pl.BlockSpec((ts, 512), row)
    kv_spec = pl.BlockSpec((4, ts, LANES), lambda i: (0, i, 0))
    return pl.pallas_call(
        body, name=name,
        out_shape=(jax.ShapeDtypeStruct((S, 2 * QKV_W), BF16), jax.ShapeDtypeStruct((1, LANES), F32),
                   jax.ShapeDtypeStruct((1, LANES), F32)),
        grid=(S // ts,),
        in_specs=[q_spec, kv_spec, kv_spec, q_spec, kv_spec, kv_spec,
                  pl.BlockSpec((ts, QKV_W), lambda i: (i, 1)), tab, tab, tab, tab,
                  pl.BlockSpec((1, LANES), full), pl.BlockSpec((1, LANES), full), pl.BlockSpec((LANES, LANES), full)],
        out_specs=(pl.BlockSpec((ts, 2 * QKV_W), row), pl.BlockSpec((1, LANES), full),
                   pl.BlockSpec((1, LANES), full)),
        compiler_params=_cp(("arbitrary",)),
    )(dqa, dka, dva, dqb, dkb, dvb, proj, ca, sa, cb, sb, qg2, kg2, seg)


def _attn_dense_fwd(q, k4, v4, *, gather=(), name):
    S = q.shape[0]
    tq = _tile(S, 256)
    xs = [x for x, _ in gather]
    na = len(xs)

    def body(q_ref, k_ref, v_ref, *rest):
        o_ref, lse_ref = rest[na], rest[na + 1]
        if na:
            x_refs, out_refs, sems = rest[:na], rest[na + 2:2 * na + 2], rest[2 * na + 2:]
            start, finish = _gather_plan([(x_refs[a], gather[a][1], out_refs[a]) for a in range(na)], *sems)
            pl.when((pl.program_id(0) == 0) & (pl.program_id(1) == 0))(start)
        for pr in range(2):
            qp = q_ref[:, pr * LANES:(pr + 1) * LANES]
            acc = None
            for half in range(2):
                s = _dot(qp, k_ref[half], _NT)
                m = jnp.max(s, axis=-1, keepdims=True)
                e = jnp.exp(s - m)
                l = jnp.sum(e, axis=-1, keepdims=True)
                pv = _dot(e.astype(BF16), v_ref[half]) * (1.0 / l)
                acc = pv if acc is None else acc + pv
                lse_ref[pr * 2 + half] = m + jnp.log(l)
            o_ref[:, pr * LANES:(pr + 1) * LANES] = acc.astype(BF16)
        if na:
            pl.when((pl.program_id(0) == 1) & (pl.program_id(1) == S // tq - 1))(finish)

    kv_spec = pl.BlockSpec((2, S, LANES), lambda kv, i: (kv, 0, 0))
    res = pl.pallas_call(
        body, name=name,
        out_shape=(jax.ShapeDtypeStruct((S, 512), BF16), jax.ShapeDtypeStruct((8, S, 1), F32),
                   *[_gathered_shape(x) for x in xs]),
        grid=(2, S // tq),
        in_specs=[pl.BlockSpec((tq, 256), lambda kv, i: (i, kv)), kv_spec, kv_spec] + [_ANY] * na,
        out_specs=(pl.BlockSpec((tq, 256), lambda kv, i: (i, kv)),
                   pl.BlockSpec((4, tq, 1), lambda kv, i: (kv, i, 0)), *([_ANY] * na)),
        scratch_shapes=_comm_scratch(na) if na else [],
        compiler_params=_cp(("arbitrary", "arbitrary") if na else ("parallel", "parallel")),
    )(q, k4, v4, *xs)
    return res[0], res[1], list(res[2:])


def _attn_dense_bwd(q, k4, v4, lse, do, *, scatter=(), name):
    S = q.shape[0]
    tq = _tile(S, 256)
    na = len(scatter)
    comm_in, comm_out, held = _scatter_io(scatter)
    n_in = len(comm_in)

    def body(q_ref, k_ref, v_ref, lse_ref, do_ref, *rest):
        dq_ref, dk_ref, dv_ref = rest[n_in:n_in + 3]
        if na:
            s_refs, r_refs, sems = rest[:na], rest[n_in + 3:n_in + 3 + na], rest[n_in + 3 + na:]
            start, finish = _scatter_plan([(s_refs[a], r_refs[a], scatter[a][2]) for a in range(na)], *sems)
            pl.when((pl.program_id(0) == 0) & (pl.program_id(1) == 0))(start)

        @pl.when(pl.program_id(1) == 0)
        def _():
            dk_ref[...] = jnp.zeros_like(dk_ref)
            dv_ref[...] = jnp.zeros_like(dv_ref)

        lane = lax.broadcasted_iota(jnp.int32, (tq, LANES), 1)
        for pr in range(2):
            qp = q_ref[:, pr * LANES:(pr + 1) * LANES]
            dop = do_ref[:, pr * LANES:(pr + 1) * LANES].astype(BF16)
            dq = None
            for half in range(2):
                mine = (lane < HEAD_DIM) if half == 0 else (lane >= HEAD_DIM)
                s = _dot(qp, k_ref[half], _NT)
                p = jnp.exp(s - lse_ref[pr * 2 + half])
                dp = _dot(dop, v_ref[half], _NT)
                delta = jnp.sum(p * dp, axis=-1, keepdims=True)
                ds = (p * (dp - delta)).astype(BF16)
                pb = p.astype(BF16)
                d = _dot(ds, k_ref[half])
                dq = d if dq is None else dq + d
                dk_ref[half] += _dot(ds, jnp.where(mine, qp, jnp.zeros_like(qp)), _TN)
                dv_ref[half] += _dot(pb, jnp.where(mine, dop, jnp.zeros_like(dop)), _TN)
            dq_ref[:, pr * LANES:(pr + 1) * LANES] = dq
        if na:
            pl.when((pl.program_id(0) == 1) & (pl.program_id(1) == S // tq - 1))(finish)

    kv_spec = pl.BlockSpec((2, S, LANES), lambda kv, i: (kv, 0, 0))
    q_spec = pl.BlockSpec((tq, 256), lambda kv, i: (i, kv))
    res = pl.pallas_call(
        body, name=name,
        out_shape=(jax.ShapeDtypeStruct((S, 512), F32), jax.ShapeDtypeStruct((4, S, LANES), F32),
                   jax.ShapeDtypeStruct((4, S, LANES), F32), *comm_out),
        grid=(2, S // tq),
        in_specs=[q_spec, kv_spec, kv_spec, pl.BlockSpec((4, tq, 1), lambda kv, i: (kv, i, 0)), q_spec]
                 + [_ANY] * n_in,
        out_specs=(q_spec, kv_spec, kv_spec, *([_ANY] * na)),
        scratch_shapes=_comm_scratch(na) if na else [],
        input_output_aliases={5 + na + i: 3 + a for i, a in enumerate(held)},
        compiler_params=_cp(("arbitrary", "arbitrary") if na else ("parallel", "arbitrary")),
    )(q, k4, v4, lse, do, *comm_in)
    return res[0], res[1], res[2], list(res[3:])


WIN_Q = 2 * BLOCK
WIN_KEYS = WIN_Q + 2 * WINDOW


def _win_start(n, S):
    return pl.multiple_of(jnp.clip(n * WIN_Q - WINDOW, 0, S - WIN_KEYS), BLOCK)


def _win_valid(n, start):
    qpos = n * WIN_Q + lax.broadcasted_iota(jnp.int32, (WIN_Q, WIN_KEYS), 0)
    kpos = start + lax.broadcasted_iota(jnp.int32, (WIN_Q, WIN_KEYS), 1)
    return jnp.abs(qpos - kpos) <= WINDOW


def _attn_win_fwd(q, k4, v4, sink, *, name):
    S = q.shape[0]
    assert S >= WIN_KEYS

    def body(sink_ref, q_ref, k_ref, v_ref, o_ref, lse_ref):
        n = pl.program_id(0)
        start = _win_start(n, S)
        valid = _win_valid(n, start)
        for kv in range(2):
            for pr in range(2):
                cols = slice((kv * 2 + pr) * LANES, (kv * 2 + pr + 1) * LANES)
                qp = q_ref[:, cols]
                acc = None
                for half in range(2):
                    h = kv * 4 + pr * 2 + half
                    kk = k_ref[kv * 2 + half, pl.ds(start, WIN_KEYS), :]
                    vv = v_ref[kv * 2 + half, pl.ds(start, WIN_KEYS), :]
                    s = jnp.where(valid, _dot(qp, kk, _NT), NEG_BIG)
                    snk = sink_ref[h]
                    m = jnp.maximum(jnp.max(s, axis=-1, keepdims=True), snk)
                    e = jnp.exp(s - m)
                    l = jnp.sum(e, axis=-1, keepdims=True) + jnp.exp(snk - m)
                    pv = _dot(e.astype(BF16), vv) * (1.0 / l)
                    acc = pv if acc is None else acc + pv
                    lse_ref[h] = m + jnp.log(l)
                o_ref[:, cols] = acc.astype(BF16)

    kv_spec = pl.BlockSpec((4, S, LANES), lambda n: (0, 0, 0))
    return pl.pallas_call(
        body, name=name,
        out_shape=(jax.ShapeDtypeStruct((S, 512), BF16), jax.ShapeDtypeStruct((8, S, 1), F32)),
        grid=(S // WIN_Q,),
        in_specs=[pl.BlockSpec(memory_space=pltpu.SMEM), pl.BlockSpec((WIN_Q, 512), lambda n: (n, 0)),
                  kv_spec, kv_spec],
        out_specs=(pl.BlockSpec((WIN_Q, 512), lambda n: (n, 0)), pl.BlockSpec((8, WIN_Q, 1), lambda n: (0, n, 0))),
        compiler_params=_cp(("parallel",)),
    )(sink, q, k4, v4)


def _attn_win_bwd(q, k4, v4, sink, lse, do, *, name):
    S = q.shape[0]

    def body(sink_ref, q_ref, k_ref, v_ref, lse_ref, do_ref, dq_ref, dk_ref, dv_ref, dsink_ref):
        n = pl.program_id(0)

        @pl.when(n == 0)
        def _():
            dk_ref[...] = jnp.zeros_like(dk_ref)
            dv_ref[...] = jnp.zeros_like(dv_ref)
            dsink_ref[...] = jnp.zeros_like(dsink_ref)

        start = _win_start(n, S)
        valid = _win_valid(n, start)
        lane = lax.broadcasted_iota(jnp.int32, (WIN_Q, LANES), 1)
        for kv in range(2):
            for pr in range(2):
                cols = slice((kv * 2 + pr) * LANES, (kv * 2 + pr + 1) * LANES)
                qp = q_ref[:, cols]
                dop = do_ref[:, cols].astype(BF16)
                dq = None
                for half in range(2):
                    h = kv * 4 + pr * 2 + half
                    slot = kv * 2 + half
                    mine = (lane < HEAD_DIM) if half == 0 else (lane >= HEAD_DIM)
                    win = pl.ds(start, WIN_KEYS)
                    kk = k_ref[slot, win, :]
                    vv = v_ref[slot, win, :]
                    lse_h = lse_ref[h]
                    s = jnp.where(valid, _dot(qp, kk, _NT), NEG_BIG)
                    p = jnp.exp(s - lse_h)
                    dp = _dot(dop, vv, _NT)
                    delta = jnp.sum(p * dp, axis=-1, keepdims=True)
                    ds = (p * (dp - delta)).astype(BF16)
                    pb = p.astype(BF16)
                    d = _dot(ds, kk)
                    dq = d if dq is None else dq + d
                    dk_ref[slot, win, :] += _dot(ds, jnp.where(mine, qp, jnp.zeros_like(qp)), _TN)
                    dv_ref[slot, win, :] += _dot(pb, jnp.where(mine, dop, jnp.zeros_like(dop)), _TN)
                    p_sink = jnp.exp(sink_ref[h] - lse_h)
                    dsink_ref[h:h + 1, :] += jnp.broadcast_to(-jnp.sum(p_sink * delta, axis=0, keepdims=True),
                                                              (1, LANES))
                dq_ref[:, cols] = dq

    kv_spec = pl.BlockSpec((4, S, LANES), lambda n: (0, 0, 0))
    q_spec = pl.BlockSpec((WIN_Q, 512), lambda n: (n, 0))
    return pl.pallas_call(
        body, name=name,
        out_shape=(jax.ShapeDtypeStruct((S, 512), F32), jax.ShapeDtypeStruct((4, S, LANES), F32),
                   jax.ShapeDtypeStruct((4, S, LANES), F32), jax.ShapeDtypeStruct((8, LANES), F32)),
        grid=(S // WIN_Q,),
        in_specs=[pl.BlockSpec(memory_space=pltpu.SMEM), q_spec, kv_spec, kv_spec,
                  pl.BlockSpec((8, WIN_Q, 1), lambda n: (0, n, 0)), q_spec],
        out_specs=(q_spec, kv_spec, kv_spec, pl.BlockSpec((8, LANES), lambda n: (0, 0))),
        compiler_params=_cp(("arbitrary",)),
    )(sink, q, k4, v4, lse, do)


def _c_ln(v, g, b):
    mu = jnp.mean(v, axis=-1, keepdims=True)
    vc = v - mu
    r = lax.rsqrt(jnp.mean(vc * vc, axis=-1, keepdims=True) + LN_EPS)
    vh = vc * r
    return vh, r, vh * g + b


def _gmlp_blocks(rows):
    return [(slice(c * CHUNK, (c + 1) * CHUNK), slice(gi * LANES, (gi + 1) * LANES), gi)
            for c in range(rows // CHUNK) for gi in range(C_GROUPS)]


def _gmlp_fwd(proj, ws, bs3, lg, lb, *, name):
    S = proj.shape[0]
    rows = _tile(S, ROW_TILE)

    def body(u_ref, v_ref, ws_ref, bs_ref, lg_ref, lb_ref, o_ref):
        u = _gelu(u_ref[...])
        _, _, vn = _c_ln(_gelu(v_ref[...]), lg_ref[...], lb_ref[...])
        vn = vn.astype(BF16)
        for r, cols, gi in _gmlp_blocks(rows):
            mixed = _dot(ws_ref[gi], vn[r, cols]) + bs_ref[gi]
            o_ref[r, cols] = (u[r, cols] * mixed).astype(BF16)

    full2 = lambda n: (0, 0)
    full3 = lambda n: (0, 0, 0)
    return pl.pallas_call(
        body, name=name,
        out_shape=jax.ShapeDtypeStruct((S, C_WIDTH), BF16),
        grid=(S // rows,),
        in_specs=[pl.BlockSpec((rows, C_WIDTH), lambda n: (n, COL_C // C_WIDTH)),
                  pl.BlockSpec((rows, C_WIDTH), lambda n: (n, COL_C // C_WIDTH + 1)),
                  pl.BlockSpec((C_GROUPS, CHUNK, CHUNK), full3), pl.BlockSpec((C_GROUPS, CHUNK, 1), full3),
                  pl.BlockSpec((1, C_WIDTH), full2), pl.BlockSpec((1, C_WIDTH), full2)],
        out_specs=pl.BlockSpec((rows, C_WIDTH), lambda n: (n, 0)),
        compiler_params=_cp(("parallel",)),
    )(proj, proj, ws, bs3, lg, lb)


def _gmlp_bwd(proj, dout, ws, bs3, lg, lb, *, name):
    S = proj.shape[0]
    rows = _tile(S, ROW_TILE)
    nch = rows // CHUNK

    def body(u_ref, v_ref, d_ref, ws_ref, bs_ref, lg_ref, lb_ref, dz_ref, dws_ref, dbs_ref, dlg_ref, dlb_ref):
        @pl.when(pl.program_id(0) == 0)
        def _():
            dws_ref[...] = jnp.zeros_like(dws_ref)
            dbs_ref[...] = jnp.zeros_like(dbs_ref)
            dlg_ref[...] = jnp.zeros_like(dlg_ref)
            dlb_ref[...] = jnp.zeros_like(dlb_ref)

        u_pre, v_pre, d = u_ref[...], v_ref[...], d_ref[...]
        u, u_grad = _gelu_and_grad(u_pre)
        v, v_grad = _gelu_and_grad(v_pre)
        vh, r, vn = _c_ln(v, lg_ref[...], lb_ref[...])
        vnb = vn.astype(BF16)
        dm = d * u
        dvn_parts = []
        dws = [None] * C_GROUPS
        dbs = [None] * C_GROUPS
        for rs, cols, gi in _gmlp_blocks(rows):
            mixed = _dot(ws_ref[gi], vnb[rs, cols]) + bs_ref[gi]
            dz_ref[rs, cols] = (d[rs, cols] * mixed * u_grad[rs, cols]).astype(BF16)
            dmb = dm[rs, cols].astype(BF16)
            t, b = _dot(dmb, vnb[rs, cols], _NT), jnp.sum(dm[rs, cols], axis=-1, keepdims=True)
            dws[gi] = t if dws[gi] is None else dws[gi] + t
            dbs[gi] = b if dbs[gi] is None else dbs[gi] + b
            dvn_parts.append(_dot(ws_ref[gi], dmb, _TN))
        for gi in range(C_GROUPS):
            dws_ref[gi] += dws[gi]
            dbs_ref[gi] += dbs[gi]
        dvn = jnp.concatenate([jnp.concatenate(dvn_parts[c * C_GROUPS:(c + 1) * C_GROUPS], axis=-1)
                               for c in range(nch)], axis=0)
        dlg_ref[...] += jnp.sum(dvn * vh, axis=0, keepdims=True)
        dlb_ref[...] += jnp.sum(dvn, axis=0, keepdims=True)
        dvh = dvn * lg_ref[...]
        m1 = jnp.mean(dvh, axis=-1, keepdims=True)
        m2 = jnp.mean(dvh * vh, axis=-1, keepdims=True)
        dv = r * (dvh - m1 - vh * m2)
        dz_ref[:, C_WIDTH:] = (dv * v_grad).astype(BF16)

    full2 = lambda n: (0, 0)
    full3 = lambda n: (0, 0, 0)
    return pl.pallas_call(
        body, name=name,
        out_shape=(jax.ShapeDtypeStruct((S, 2 * C_WIDTH), BF16), jax.ShapeDtypeStruct((C_GROUPS, CHUNK, CHUNK), F32),
                   jax.ShapeDtypeStruct((C_GROUPS, CHUNK, 1), F32), jax.ShapeDtypeStruct((1, C_WIDTH), F32),
                   jax.ShapeDtypeStruct((1, C_WIDTH), F32)),
        grid=(S // rows,),
        in_specs=[pl.BlockSpec((rows, C_WIDTH), lambda n: (n, COL_C // C_WIDTH)),
                  pl.BlockSpec((rows, C_WIDTH), lambda n: (n, COL_C // C_WIDTH + 1)),
                  pl.BlockSpec((rows, C_WIDTH), lambda n: (n, 0)),
                  pl.BlockSpec((C_GROUPS, CHUNK, CHUNK), full3), pl.BlockSpec((C_GROUPS, CHUNK, 1), full3),
                  pl.BlockSpec((1, C_WIDTH), full2), pl.BlockSpec((1, C_WIDTH), full2)],
        out_specs=(pl.BlockSpec((rows, 2 * C_WIDTH), lambda n: (n, 0)), pl.BlockSpec((C_GROUPS, CHUNK, CHUNK), full3),
                   pl.BlockSpec((C_GROUPS, CHUNK, 1), full3), pl.BlockSpec((1, C_WIDTH), full2),
                   pl.BlockSpec((1, C_WIDTH), full2)),
        compiler_params=_cp(("arbitrary",)),
    )(proj, proj, dout, ws, bs3, lg, lb)


GATE_BLK = 512


def _gate_specs(tm, D):
    nh = D // GATE_BLK
    first = COL_GATE // GATE_BLK
    return [pl.BlockSpec((tm, GATE_BLK), functools.partial(lambda i, c: (i, c), c=first + b))
            for b in range(N_BRANCH * nh)]


def _merge_fwd(oa, ob, oc, wb, proj, bg, *, name):
    S = oa.shape[0]
    D = wb.shape[2]
    assert D % GATE_BLK == 0
    nh = D // GATE_BLK
    tm = _tile(S, ROW_TILE)

    def body(oa_ref, ob_ref, oc_ref, wb_ref, *rest):
        gate_refs, bg_ref, o_ref = rest[:N_BRANCH * nh], rest[N_BRANCH * nh], rest[N_BRANCH * nh + 1]
        brs = (oa_ref[...], ob_ref[...], oc_ref[...])
        for j in range(nh):
            cols = slice(j * GATE_BLK, (j + 1) * GATE_BLK)
            acc = None
            for n in range(N_BRANCH):
                b = n * nh + j
                t = _dot(brs[n], wb_ref[n, :, cols])
                g = _sigmoid(gate_refs[b][...] + bg_ref[:, b * GATE_BLK:(b + 1) * GATE_BLK])
                acc = t * g if acc is None else acc + t * g
            o_ref[:, cols] = acc.astype(BF16)

    row = lambda i: (i, 0)
    br = pl.BlockSpec((tm, BRANCH_WIDTH), row)
    return pl.pallas_call(
        body, name=name,
        out_shape=jax.ShapeDtypeStruct((S, D), BF16),
        grid=(S // tm,),
        in_specs=[br, br, br, pl.BlockSpec((N_BRANCH, BRANCH_WIDTH, D), lambda i: (0, 0, 0))]
                 + _gate_specs(tm, D) + [pl.BlockSpec((1, N_BRANCH * D), lambda i: (0, 0))],
        out_specs=pl.BlockSpec((tm, D), row),
        compiler_params=_cp(("parallel",)),
    )(oa, ob, oc, wb, *([proj] * (N_BRANCH * nh)), bg)


def _merge_bwd(oa, ob, oc, wb, proj, bg, dmerged, *, name):
    S = oa.shape[0]
    D = wb.shape[2]
    nh = D // GATE_BLK
    tm = _tile(S, ROW_TILE)

    def body(oa_ref, ob_ref, oc_ref, wb_ref, *rest):
        gate_refs = rest[:N_BRANCH * nh]
        bg_ref, dm_ref, dgl_ref, dbg_ref = rest[N_BRANCH * nh:N_BRANCH * nh + 4]
        dt_refs = rest[N_BRANCH * nh + 4:N_BRANCH * nh + 4 + N_BRANCH]
        dbr_refs = rest[N_BRANCH * nh + 4 + N_BRANCH:]

        @pl.when(pl.program_id(0) == 0)
        def _():
            dbg_ref[...] = jnp.zeros_like(dbg_ref)

        brs = (oa_ref[...], ob_ref[...], oc_ref[...])
        for n in range(N_BRANCH):
            dbr = None
            for j in range(nh):
                cols = slice(j * GATE_BLK, (j + 1) * GATE_BLK)
                b = n * nh + j
                gcols = slice(b * GATE_BLK, (b + 1) * GATE_BLK)
                w = wb_ref[n, :, cols]
                t = _dot(brs[n], w)
                g = _sigmoid(gate_refs[b][...] + bg_ref[:, gcols])
                dm = dm_ref[:, cols]
                dt = (dm * g).astype(BF16)
                dgl = dm * t * g * (1.0 - g)
                dt_refs[n][:, cols] = dt
                dgl_ref[:, gcols] = dgl.astype(BF16)
                dbg_ref[:, gcols] += jnp.sum(dgl, axis=0, keepdims=True)
                d = _dot(dt, w, _NT)
                dbr = d if dbr is None else dbr + d
            dbr_refs[n][...] = dbr.astype(dbr_refs[n].dtype)

    row = lambda i: (i, 0)
    br = pl.BlockSpec((tm, BRANCH_WIDTH), row)
    res = pl.pallas_call(
        body, name=name,
        out_shape=(jax.ShapeDtypeStruct((S, N_BRANCH * D), BF16), jax.ShapeDtypeStruct((1, N_BRANCH * D), F32),
                   *([jax.ShapeDtypeStruct((S, D), BF16)] * N_BRANCH),
                   *[jax.ShapeDtypeStruct((S, BRANCH_WIDTH), dt) for dt in (BF16, BF16, F32)]),
        grid=(S // tm,),
        in_specs=[br, br, br, pl.BlockSpec((N_BRANCH, BRANCH_WIDTH, D), lambda i: (0, 0, 0))]
                 + _gate_specs(tm, D)
                 + [pl.BlockSpec((1, N_BRANCH * D), lambda i: (0, 0)), pl.BlockSpec((tm, D), row)],
        out_specs=(pl.BlockSpec((tm, N_BRANCH * D), row), pl.BlockSpec((1, N_BRANCH * D), lambda i: (0, 0)),
                   *([pl.BlockSpec((tm, D), row)] * N_BRANCH), *([br] * N_BRANCH)),
        compiler_params=_cp(("arbitrary",)),
    )(oa, ob, oc, wb, *([proj] * (N_BRANCH * nh)), bg, dmerged)
    return res[0], res[1], list(res[2:2 + N_BRANCH]), list(res[2 + N_BRANCH:])


X_SCALE = 1.0 / math.sqrt(X_HEAD_DIM)
X_W = X_HEADS * X_HEAD_DIM


def _xattn_fwd(q, kv, *, name):
    S = q.shape[0]
    M = kv.shape[0]
    tq = _tile(S, 512)

    def body(q_ref, kv_ref, o_ref, lse_ref):
        for h in range(X_HEADS):
            cols = slice(h * LANES, (h + 1) * LANES)
            s = _dot(q_ref[:, cols], kv_ref[:, cols], _NT) * X_SCALE
            m = jnp.max(s, axis=-1, keepdims=True)
            e = jnp.exp(s - m)
            l = jnp.sum(e, axis=-1, keepdims=True)
            p = (e * (1.0 / l)).astype(BF16)
            o_ref[:, cols] = _dot(p, kv_ref[:, X_W + h * LANES:X_W + (h + 1) * LANES]).astype(BF16)
            lse_ref[h] = m + jnp.log(l)

    return pl.pallas_call(
        body, name=name,
        out_shape=(jax.ShapeDtypeStruct((S, X_W), BF16), jax.ShapeDtypeStruct((X_HEADS, S, 1), F32)),
        grid=(S // tq,),
        in_specs=[pl.BlockSpec((tq, X_W), lambda i: (i, 0)), pl.BlockSpec((M, 2 * X_W), lambda i: (0, 0))],
        out_specs=(pl.BlockSpec((tq, X_W), lambda i: (i, 0)), pl.BlockSpec((X_HEADS, tq, 1), lambda i: (0, i, 0))),
        compiler_params=_cp(("parallel",)),
    )(q, kv)


def _xattn_bwd(q, kv, lse, do, *, name):
    S = q.shape[0]
    M = kv.shape[0]
    tq = _tile(S, 512)

    def body(q_ref, kv_ref, lse_ref, do_ref, dq_ref, dkv_ref):
        @pl.when(pl.program_id(0) == 0)
        def _():
            dkv_ref[...] = jnp.zeros_like(dkv_ref)

        for h in range(X_HEADS):
            cols = slice(h * LANES, (h + 1) * LANES)
            vcols = slice(X_W + h * LANES, X_W + (h + 1) * LANES)
            qh, kh, vh = q_ref[:, cols], kv_ref[:, cols], kv_ref[:, vcols]
            doh = do_ref[:, cols].astype(BF16)
            p = jnp.exp(_dot(qh, kh, _NT) * X_SCALE - lse_ref[h])
            dp = _dot(doh, vh, _NT)
            delta = jnp.sum(p * dp, axis=-1, keepdims=True)
            ds = (p * (dp - delta) * X_SCALE).astype(BF16)
            dq_ref[:, cols] = _dot(ds, kh).astype(BF16)
            dkv_ref[:, cols] += _dot(ds, qh, _TN)
            dkv_ref[:, vcols] += _dot(p.astype(BF16), doh, _TN)

    q_spec = pl.BlockSpec((tq, X_W), lambda i: (i, 0))
    return pl.pallas_call(
        body, name=name,
        out_shape=(jax.ShapeDtypeStruct((S, X_W), BF16), jax.ShapeDtypeStruct((M, 2 * X_W), F32)),
        grid=(S // tq,),
        in_specs=[q_spec, pl.BlockSpec((M, 2 * X_W), lambda i: (0, 0)),
                  pl.BlockSpec((X_HEADS, tq, 1), lambda i: (0, i, 0)), q_spec],
        out_specs=(q_spec, pl.BlockSpec((M, 2 * X_W), lambda i: (0, 0))),
        compiler_params=_cp(("arbitrary",)),
    )(q, kv, lse, do)


def _shift_down(h, row):
    return jnp.where(row == 0, 0.0, pltpu.roll(h, 1, 0))


def _shift_up(h, row, S):
    return jnp.where(row == S - 1, 0.0, pltpu.roll(h, S - 1, 0))


def _conv3(h, ck, cb, row, S):
    return _shift_down(h, row) * ck[0:1] + h * ck[1:2] + _shift_up(h, row, S) * ck[2:3] + cb


def _conv_act_fwd(h, ck, cb, *, name):
    S, F2 = h.shape
    F = F2 // 2
    nt = F // LANES

    def body(ha_ref, hb_ref, cka_ref, ckb_ref, cba_ref, cbb_ref, o_ref):
        row = lax.broadcasted_iota(jnp.int32, (S, LANES), 0)
        a = _conv3(ha_ref[...], cka_ref[...], cba_ref[...], row, S)
        b = _conv3(hb_ref[...], ckb_ref[...], cbb_ref[...], row, S)
        o_ref[...] = (_gelu(a) * b).astype(BF16)

    ca = lambda j: (0, j)
    cbi = lambda j: (0, j + nt)
    return pl.pallas_call(
        body, name=name,
        out_shape=jax.ShapeDtypeStruct((S, F), BF16),
        grid=(nt,),
        in_specs=[pl.BlockSpec((S, LANES), ca), pl.BlockSpec((S, LANES), cbi), pl.BlockSpec((3, LANES), ca),
                  pl.BlockSpec((3, LANES), cbi), pl.BlockSpec((1, LANES), ca), pl.BlockSpec((1, LANES), cbi)],
        out_specs=pl.BlockSpec((S, LANES), ca),
        compiler_params=_cp(("parallel",)),
    )(h, h, ck, ck, cb, cb)


def _conv_act_bwd(h, ck, cb, dact, *, name):
    S, F2 = h.shape
    F = F2 // 2
    nt = F // LANES

    def body(ha_ref, hb_ref, cka_ref, ckb_ref, cba_ref, cbb_ref, d_ref,
             dha_ref, dhb_ref, dcka_ref, dckb_ref, dcba_ref, dcbb_ref):
        row = lax.broadcasted_iota(jnp.int32, (S, LANES), 0)
        ha, hb = ha_ref[...], hb_ref[...]
        cka, ckb = cka_ref[...], ckb_ref[...]
        a = _conv3(ha, cka, cba_ref[...], row, S)
        b = _conv3(hb, ckb, cbb_ref[...], row, S)
        d = d_ref[...]
        ga, ga_grad = _gelu_and_grad(a)
        da = d * b * ga_grad
        db = d * ga
        for dd, hh, ck_, dh_ref, dck_ref, dcb_ref in ((da, ha, cka, dha_ref, dcka_ref, dcba_ref),
                                                      (db, hb, ckb, dhb_ref, dckb_ref, dcbb_ref)):
            dcb_ref[...] = jnp.sum(dd, axis=0, keepdims=True)
            dck_ref[0:1, :] = jnp.sum(dd * _shift_down(hh, row), axis=0, keepdims=True)
            dck_ref[1:2, :] = jnp.sum(dd * hh, axis=0, keepdims=True)
            dck_ref[2:3, :] = jnp.sum(dd * _shift_up(hh, row, S), axis=0, keepdims=True)
            dh = _shift_up(dd, row, S) * ck_[0:1] + dd * ck_[1:2] + _shift_down(dd, row) * ck_[2:3]
            dh_ref[...] = dh.astype(BF16)

    ca = lambda j: (0, j)
    cbi = lambda j: (0, j + nt)
    col = pl.BlockSpec((S, LANES), ca)
    return pl.pallas_call(
        body, name=name,
        out_shape=(jax.ShapeDtypeStruct((S, F), BF16), jax.ShapeDtypeStruct((S, F), BF16),
                   jax.ShapeDtypeStruct((3, F), F32), jax.ShapeDtypeStruct((3, F), F32),
                   jax.ShapeDtypeStruct((1, F), F32), jax.ShapeDtypeStruct((1, F), F32)),
        grid=(nt,),
        in_specs=[col, pl.BlockSpec((S, LANES), cbi), pl.BlockSpec((3, LANES), ca), pl.BlockSpec((3, LANES), cbi),
                  pl.BlockSpec((1, LANES), ca), pl.BlockSpec((1, LANES), cbi), col],
        out_specs=(col, col, pl.BlockSpec((3, LANES), ca), pl.BlockSpec((3, LANES), ca),
                   pl.BlockSpec((1, LANES), ca), pl.BlockSpec((1, LANES), ca)),
        compiler_params=_cp(("parallel",)),
    )(h, h, ck, ck, cb, cb, dact)


def _layer_fwd(x, xb, memb, w, shards, tabs, seg, l):
    n = lambda s: f"L{l}_{s}"
    w = dict(w)
    qg2 = jnp.tile(w["b_q_gain"], 2)[None, :]
    kg2 = jnp.tile(w["b_k_gain"], 2)[None, :]
    proj = _mm(xb, w["w_in"], tb=True, name=n("proj"))
    aq, ak4, av4, bq, bk4, bv4 = _prep(proj, tabs, qg2, kg2, seg, name=n("prep"))
    oa, lse_a = _attn_win_fwd(aq, ak4, av4, w["a_sink"], name=n("attn_win"))
    gather = [(shards[k], l) for k in GATHERED_LATE] + ([(shards["w_in"], l + 1)] if l + 1 < DEPTH else [])
    ob, lse_b, gathered = _attn_dense_fwd(bq, bk4, bv4, gather=gather, name=n("attn_dense"))
    for k, g in zip(GATHERED_LATE, gathered):
        w[k] = _unshard(g, GATHER_AXIS[k])
    w_in_next = gathered[len(GATHERED_LATE)] if l + 1 < DEPTH else None
    oc = _gmlp_fwd(proj, w["c_ws"], w["c_bs3"], w["c_ln_g"], w["c_ln_b"], name=n("gmlp"))
    merged = _merge_fwd(oa, ob, oc, w["w_branch"], proj, w["b_gate"], name=n("merge"))
    x1, x1b, xh1, rs1 = _mm_res_ln(merged, w["w_mix_out"], x, w["ln1_g"], w["ln1_b"], name=n("mix_ln1"))
    xq = _mm(x1b, w["x_wq"], out_dtype=BF16, name=n("xq"))
    xkv = _mm(memb, w["x_wkv"], out_dtype=BF16, name=n("xkv"))
    xo, lse_x = _xattn_fwd(xq, xkv, name=n("xattn"))
    x2, x2b, xh2, rs2 = _mm_res_ln(xo, w["x_wo"], x1, w["ln2_g"], w["ln2_b"], name=n("xo_ln2"))
    h = _mm(x2b, w["f_w_up"], tb=True, name=n("ffn_up"))
    act = _conv_act_fwd(h, w["f_conv_k"], w["f_conv_b"], name=n("conv_act"))
    x3, x3b, xh3, rs3 = _mm_res_ln(act, w["f_w_down"], x2, w["ln3_g"], w["ln3_b"], name=n("down_ln3"))
    saved = dict(xb=xb, proj=proj, aq=aq, ak4=ak4, av4=av4, bq=bq, bk4=bk4, bv4=bv4, lse_a=lse_a, lse_b=lse_b,
                 oa=oa, ob=ob, oc=oc, merged=merged, xh1=xh1, rs1=rs1, x1b=x1b, xq=xq, xkv=xkv, xo=xo, lse_x=lse_x,
                 xh2=xh2, rs2=rs2, x2b=x2b, h=h, act=act, xh3=xh3, rs3=rs3, qg2=qg2, kg2=kg2)
    return x3, x3b, saved, w, w_in_next


def _layer_bwd(top, memb, w, sv, tabs, seg, l, ln_below, dw_in_above, recv):
    n = lambda s: f"L{l}_{s}"
    g = {}
    big = {}
    recv = dict(recv)

    def dw(key, a, b, tag):
        big[key] = _mm(a, b, ta=True, out_dtype=BF16, name=n(tag))

    def dw_t(key, segments, x, tag):
        buf = jax.ShapeDtypeStruct((sum(s.shape[1] for s in segments), x.shape[1]), BF16)
        row = 0
        for i, s in enumerate(segments):
            buf = _mm(s, x, ta=True, into=(buf, row), name=n(f"{tag}{i}"))
            row += s.shape[1]
        big[key] = buf

    dz3, dz3b, g["ln3_g"], g["ln3_b"] = top
    dw("f_w_down", sv["act"], dz3b, "dw_down")
    dact = _mm(dz3b, w["f_w_down"], tb=True, name=n("dact"))
    dha, dhb, dcka, dckb, dcba, dcbb = _conv_act_bwd(sv["h"], w["f_conv_k"], w["f_conv_b"], dact, name=n("conv_act_bwd"))
    g["f_conv_k"] = jnp.concatenate([dcka, dckb], axis=1)
    g["f_conv_b"] = jnp.concatenate([dcba, dcbb], axis=1)[0]
    dw_t("f_w_up", [dha, dhb], sv["x2b"], "dw_up")
    dz2, dz2b, g["ln2_g"], g["ln2_b"] = _mm([dha, dhb], w["f_w_up"], res=dz3, res_scale=ALPHA,
                                            ln_bwd=(sv["xh2"], sv["rs2"], w["ln2_g"]), name=n("dx2_ln2"))
    dw("x_wo", sv["xo"], dz2b, "dw_xo")
    dxo = _mm(dz2b, w["x_wo"], tb=True, out_dtype=BF16, name=n("dxo"))
    dxq, dxkv = _xattn_bwd(sv["xq"], sv["xkv"], sv["lse_x"], dxo, name=n("xattn_bwd"))
    dw("x_wq", sv["x1b"], dxq, "dw_xq")
    dw("x_wkv", memb, dxkv, "dw_xkv")
    dz1, dz1b, g["ln1_g"], g["ln1_b"] = _mm(dxq, w["x_wq"], tb=True, res=dz2, res_scale=ALPHA,
                                            ln_bwd=(sv["xh1"], sv["rs1"], w["ln1_g"]), name=n("dx1_ln1"))
    dw("w_mix_out", sv["merged"], dz1b, "dw_mix")
    dmerged = _mm(dz1b, w["w_mix_out"], tb=True, name=n("dmerged"))
    dgl, dbg, dt, dbr = _merge_bwd(sv["oa"], sv["ob"], sv["oc"], w["w_branch"], sv["proj"], w["b_gate"], dmerged,
                                   name=n("merge_bwd"))
    g["b_gate"] = dbg[0]
    for i, k in enumerate(("oa", "ob", "oc")):
        dw(f"w_branch{i}", sv[k], dt[i], f"dw_branch{i}")
    big["w_branch"] = jnp.stack([big.pop(f"w_branch{i}") for i in range(N_BRANCH)])
    dqa, dka, dva, dsink = _attn_win_bwd(sv["aq"], sv["ak4"], sv["av4"], w["a_sink"], sv["lse_a"], dbr[0],
                                         name=n("attn_win_bwd"))
    g["a_sink"] = dsink[:, 0]
    sent = [k for k in BIG if k != "w_in"]
    scatter = [(_reshard(big[k], BIG_AXIS[k]), recv[k], l) for k in sent]
    if dw_in_above is not None:
        sent.append("w_in")
        scatter.append((_reshard(dw_in_above, BIG_AXIS["w_in"]), recv["w_in"], l + 1))
    dqb, dkb, dvb, got = _attn_dense_bwd(sv["bq"], sv["bk4"], sv["bv4"], sv["lse_b"], dbr[1], scatter=scatter,
                                         name=n("attn_dense_bwd"))
    recv.update(zip(sent, got))
    dcz, g["c_ws"], dbs3, dlg, dlb = _gmlp_bwd(sv["proj"], dbr[2], w["c_ws"], w["c_bs3"], w["c_ln_g"], w["c_ln_b"],
                                               name=n("gmlp_bwd"))
    g["c_bs"] = dbs3[:, :, 0]
    g["c_ln_g"], g["c_ln_b"] = dlg[0], dlb[0]
    dqkv, dqg, dkg = _unprep(dqa, dka, dva, dqb, dkb, dvb, sv["proj"], tabs, sv["qg2"], sv["kg2"], seg, name=n("unprep"))
    g["b_q_gain"] = dqg[0, :HEAD_DIM] + dqg[0, HEAD_DIM:]
    g["b_k_gain"] = dkg[0, :HEAD_DIM] + dkg[0, HEAD_DIM:]
    dw_t("w_in", [dqkv, dcz, dgl], sv["xb"], "dw_in")
    dx0 = _mm([dqkv, dcz, dgl], w["w_in"], res=dz1, res_scale=ALPHA, name=n("dx0"))
    if ln_below is not None:
        dx0 = _ln_bwd(dx0, *ln_below, name=n("ln_bwd_below"))
    for k in ("ln1_g", "ln1_b", "ln2_g", "ln2_b", "ln3_g", "ln3_b"):
        g[k] = g[k][0]
    return dx0, g, big["w_in"], recv


WEIGHTS = ("w_in", "b_gate", "a_sink", "b_q_gain", "b_k_gain", "c_ln_g", "c_ln_b", "c_ws", "c_bs", "w_branch",
           "w_mix_out", "ln1_g", "ln1_b", "x_wq", "x_wkv", "x_wo", "ln2_g", "ln2_b", "f_w_up", "f_conv_k",
           "f_conv_b", "f_w_down", "ln3_g", "ln3_b")
TRANSPOSED = ("w_in", "f_w_up")
BIG_AXIS = {"w_in": 0, "w_branch": 2, "w_mix_out": 0, "x_wq": 0, "x_wkv": 0, "x_wo": 1, "f_w_up": 0, "f_w_down": 0}
BIG = tuple(BIG_AXIS)
GATHERED = BIG + ("f_conv_k",)
GATHERED_LATE = tuple(k for k in GATHERED if k != "w_in")
GATHER_AXIS = dict(BIG_AXIS, f_conv_k=1)
SMALL = tuple(k for k in WEIGHTS if k not in GATHERED)


def _unshard(g, axis):
    t = jnp.moveaxis(g, 0, axis)
    return t.reshape(t.shape[:axis] + (t.shape[axis] * t.shape[axis + 1],) + t.shape[axis + 2:])


def _reshard(full, axis):
    t = full.reshape(full.shape[:axis] + (N_DEV, full.shape[axis] // N_DEV) + full.shape[axis + 1:])
    return jnp.moveaxis(t, axis, 0)


def _small_weights(small, l):
    w = {k: v[l] for k, v in small.items()}
    for k in ("c_ln_g", "c_ln_b", "ln1_g", "ln1_b", "ln2_g", "ln2_b", "ln3_g", "ln3_b", "b_gate", "f_conv_b"):
        w[k] = w[k][None, :]
    w["c_bs3"] = w["c_bs"][:, :, None]
    w["c_ws"] = w["c_ws"].astype(BF16)
    return w


def _local_step(x, mem, target, small, shards):
    S = x.shape[0]
    tabs = _rope_tables(S)
    seg = _seg_matrix()
    memb = mem.astype(BF16)
    xb = x.astype(BF16)
    saved, weights = [], []
    w_in_g = _gather_call([(shards["w_in"], 0)], name="gather_w_in_L0")[0]
    for l in range(DEPTH):
        w = dict(_small_weights(small, l), w_in=_unshard(w_in_g, GATHER_AXIS["w_in"]))
        x, xb, sv, w, w_in_g = _layer_fwd(x, xb, memb, w, shards, tabs, seg, l)
        saved.append(sv)
        weights.append(w)
    dy, loss = _loss_head(x, target, name="loss_head")
    grads = [None] * DEPTH
    recv = {k: jax.ShapeDtypeStruct((DEPTH, N_DEV) + shards[k].shape[1:], BF16) for k in BIG}
    dw_in = None
    last_ln = lambda l: (saved[l]["xh3"], saved[l]["rs3"], weights[l]["ln3_g"])
    top = _ln_bwd(dy, *last_ln(DEPTH - 1), name="ln_bwd_top")
    for l in reversed(range(DEPTH)):
        top, grads[l], dw_in, recv = _layer_bwd(top, memb, weights[l], saved[l], tabs, seg, l,
                                                last_ln(l - 1) if l > 0 else None, dw_in, recv)
    recv["w_in"] = _scatter_call([(_reshard(dw_in, BIG_AXIS["w_in"]), recv["w_in"], 0)], name="scatter_w_in_L0")[0]
    return loss, top, grads, [recv[k] for k in BIG]


PACK_W = 1024


def _gather_call(gather, *, name):
    na = len(gather)

    def body(*refs):
        start, finish = _gather_plan([(refs[a], gather[a][1], refs[na + a]) for a in range(na)], *refs[2 * na:])
        start()
        finish()

    return list(pl.pallas_call(
        body, name=name,
        out_shape=[_gathered_shape(x) for x, _ in gather],
        in_specs=[_ANY] * na, out_specs=[_ANY] * na,
        scratch_shapes=_comm_scratch(na),
    )(*[x for x, _ in gather]))


def _scatter_io(scatter):
    held = [a for a, (_, r, _) in enumerate(scatter) if not isinstance(r, jax.ShapeDtypeStruct)]
    return ([s for s, _, _ in scatter] + [scatter[a][1] for a in held],
            [jax.ShapeDtypeStruct(r.shape, r.dtype) for _, r, _ in scatter], held)


def _scatter_call(scatter, *, name):
    na = len(scatter)
    operands, out_shape, held = _scatter_io(scatter)
    n_in = len(operands)

    def body(*refs):
        start, finish = _scatter_plan([(refs[a], refs[n_in + a], scatter[a][2]) for a in range(na)],
                                      *refs[n_in + na:])
        start()
        finish()

    return list(pl.pallas_call(
        body, name=name,
        out_shape=out_shape,
        in_specs=[_ANY] * n_in, out_specs=[_ANY] * na,
        scratch_shapes=_comm_scratch(na),
        input_output_aliases={na + i: a for i, a in enumerate(held)},
    )(*operands))


def _sum_parts(parts, *, name):
    P, R, C = parts.shape
    tr = _tile(R, 64, align=8)

    def body(p_ref, o_ref):
        g = p_ref[0].astype(F32)
        for s in range(1, P):
            g = g + p_ref[s].astype(F32)
        o_ref[...] = g

    return pl.pallas_call(
        body, name=name, out_shape=jax.ShapeDtypeStruct((R, C), F32), grid=(R // tr,),
        in_specs=[pl.BlockSpec((P, tr, C), lambda i: (0, i, 0))], out_specs=pl.BlockSpec((tr, C), lambda i: (i, 0)),
        compiler_params=_cp(("parallel",)),
    )(parts)


ADAM_BLOCK_ELEMS = 512 * 1024


def _adamw(parts, w, m, v, *, name):
    L, P, R, C = parts.shape
    assert w.shape == (L, R, C), (parts.shape, w.shape)
    tr = _tile(R, max(16, ADAM_BLOCK_ELEMS // C), align=16)

    def body(p_ref, w_ref, m_ref, v_ref, g_ref, d_ref, nm_ref, nv_ref):
        g = p_ref[0].astype(F32)
        for s in range(1, P):
            g = g + p_ref[s].astype(F32)
        g_ref[...] = g
        d_ref[...], nm_ref[...], nv_ref[...] = _adam_update(g, w_ref[...], m_ref[...], v_ref[...])

    blk = pl.BlockSpec((None, tr, C), lambda l, i: (l, i, 0))
    shp = jax.ShapeDtypeStruct((L, R, C), F32)
    return pl.pallas_call(
        body, name=name, out_shape=(shp, shp, shp, shp), grid=(L, R // tr),
        in_specs=[pl.BlockSpec((None, P, tr, C), lambda l, i: (l, 0, i, 0)), blk, blk, blk],
        out_specs=(blk, blk, blk, blk),
        compiler_params=_cp(("parallel", "parallel")),
    )(parts, w, m, v)


def _adam_update(g, w, m, v):
    nm = ADAM_B1 * m + (1.0 - ADAM_B1) * g
    nv = ADAM_B2 * v + (1.0 - ADAM_B2) * (g * g)
    m_hat = nm / (1.0 - ADAM_B1 ** ADAM_STEP)
    v_hat = nv / (1.0 - ADAM_B2 ** ADAM_STEP)
    return -ADAM_LR * (m_hat / (jnp.sqrt(v_hat) + ADAM_EPS) + ADAM_WD * w), nm, nv


def _adamw_small(gs, ws, ms, vs, *, name):
    n = len(gs)

    def body(*refs):
        for a in range(n):
            g, w, m, v = (refs[k * n + a][...] for k in range(4))
            d, nm, nv = _adam_update(g, w, m, v)
            refs[4 * n + a][...] = d
            refs[5 * n + a][...] = nm
            refs[6 * n + a][...] = nv

    shapes = [jax.ShapeDtypeStruct(w.shape, F32) for w in ws]
    res = pl.pallas_call(body, name=name, out_shape=shapes * 3, compiler_params=_cp())(*gs, *ws, *ms, *vs)
    return res[:n], res[n:2 * n], res[2 * n:]


def _pad_rows(vec, width, row_align):
    n = vec.shape[0]
    rows = -(-n // width)
    rows = -(-rows // row_align) * row_align
    return jnp.pad(vec, (0, rows * width - n)).reshape(rows, width)


def kernel(x, mem, w_in, b_gate, a_sink, b_q_gain, b_k_gain, c_ln_g, c_ln_b, c_ws, c_bs, w_branch, w_mix_out, ln1_g, ln1_b, x_wq, x_wkv, x_wo, ln2_g, ln2_b, f_w_up, f_conv_k, f_conv_b, f_w_down, ln3_g, ln3_b, loss_target, m_w_in, m_b_gate, m_a_sink, m_b_q_gain, m_b_k_gain, m_c_ln_g, m_c_ln_b, m_c_ws, m_c_bs, m_w_branch, m_w_mix_out, m_ln1_g, m_ln1_b, m_x_wq, m_x_wkv, m_x_wo, m_ln2_g, m_ln2_b, m_f_w_up, m_f_conv_k, m_f_conv_b, m_f_w_down, m_ln3_g, m_ln3_b, v_w_in, v_b_gate, v_a_sink, v_b_q_gain, v_b_k_gain, v_c_ln_g, v_c_ln_b, v_c_ws, v_c_bs, v_w_branch, v_w_mix_out, v_ln1_g, v_ln1_b, v_x_wq, v_x_wkv, v_x_wo, v_ln2_g, v_ln2_b, v_f_w_up, v_f_conv_k, v_f_conv_b, v_f_w_down, v_ln3_g, v_ln3_b):
    w = dict(w_in=w_in, b_gate=b_gate, a_sink=a_sink, b_q_gain=b_q_gain, b_k_gain=b_k_gain, c_ln_g=c_ln_g,
             c_ln_b=c_ln_b, c_ws=c_ws, c_bs=c_bs, w_branch=w_branch, w_mix_out=w_mix_out, ln1_g=ln1_g, ln1_b=ln1_b,
             x_wq=x_wq, x_wkv=x_wkv, x_wo=x_wo, ln2_g=ln2_g, ln2_b=ln2_b, f_w_up=f_w_up, f_conv_k=f_conv_k,
             f_conv_b=f_conv_b, f_w_down=f_w_down, ln3_g=ln3_g, ln3_b=ln3_b)
    m = dict(w_in=m_w_in, b_gate=m_b_gate, a_sink=m_a_sink, b_q_gain=m_b_q_gain, b_k_gain=m_b_k_gain,
             c_ln_g=m_c_ln_g, c_ln_b=m_c_ln_b, c_ws=m_c_ws, c_bs=m_c_bs, w_branch=m_w_branch, w_mix_out=m_w_mix_out,
             ln1_g=m_ln1_g, ln1_b=m_ln1_b, x_wq=m_x_wq, x_wkv=m_x_wkv, x_wo=m_x_wo, ln2_g=m_ln2_g, ln2_b=m_ln2_b,
             f_w_up=m_f_w_up, f_conv_k=m_f_conv_k, f_conv_b=m_f_conv_b, f_w_down=m_f_w_down, ln3_g=m_ln3_g,
             ln3_b=m_ln3_b)
    v = dict(w_in=v_w_in, b_gate=v_b_gate, a_sink=v_a_sink, b_q_gain=v_b_q_gain, b_k_gain=v_b_k_gain,
             c_ln_g=v_c_ln_g, c_ln_b=v_c_ln_b, c_ws=v_c_ws, c_bs=v_c_bs, w_branch=v_w_branch, w_mix_out=v_w_mix_out,
             ln1_g=v_ln1_g, ln1_b=v_ln1_b, x_wq=v_x_wq, x_wkv=v_x_wkv, x_wo=v_x_wo, ln2_g=v_ln2_g, ln2_b=v_ln2_b,
             f_w_up=v_f_w_up, f_conv_k=v_f_conv_k, f_conv_b=v_f_conv_b, f_w_down=v_f_w_down, ln3_g=v_ln3_g,
             ln3_b=v_ln3_b)
    me = 4 * lax.axis_index("x") + 2 * lax.axis_index("y") + lax.axis_index("c")

    def held(k, t):
        return jnp.swapaxes(t, 1, 2) if k in TRANSPOSED else t

    shards = dict({k: held(k, w[k]).astype(BF16) for k in BIG}, f_conv_k=w["f_conv_k"])
    loss, grad_x, grads, recvs = _local_step(x[0], mem[0], loss_target[0], {k: w[k] for k in SMALL}, shards)
    loss = lax.psum(loss[0, 0], ("x", "y", "c"))

    out_g, out_d, out_m, out_v = {}, {}, {}, {}
    for k, recv in zip(BIG, recvs):
        shp = held(k, w[k]).shape
        rc = (DEPTH, math.prod(shp[1:-1]), shp[-1])
        parts = recv.reshape((DEPTH, N_DEV) + rc[1:])
        g_, d_, m_, v_ = _adamw(parts, held(k, w[k]).reshape(rc), held(k, m[k]).reshape(rc),
                                held(k, v[k]).reshape(rc), name=f"adamw_{k}")
        out_g[k], out_d[k], out_m[k], out_v[k] = (held(k, t.reshape(shp)) for t in (g_, d_, m_, v_))

    small_all = SMALL + ("f_conv_k",)
    gfull = {k: jnp.stack([grads[l][k] for l in range(DEPTH)]) for k in small_all}

    def pack(d):
        return jnp.concatenate([_pad_rows(d[k].reshape(-1), PACK_W, 8) for k in small_all])

    def unpack(rows, like):
        out, r = {}, 0
        for k in small_all:
            nr = -(-like[k].size // (8 * PACK_W)) * 8
            out[k] = rows[r:r + nr].reshape(-1)[:like[k].size].reshape(like[k].shape)
            r += nr
        return out

    gathered = _gather_call([(pack(gfull)[None], 0)], name="gather_small_grads")[0]
    sg = unpack(_sum_parts(gathered, name="sum_small_grads"), gfull)
    width = w["f_conv_k"].shape[2]
    sg["f_conv_k"] = lax.dynamic_slice_in_dim(sg["f_conv_k"], me * width, width, axis=2)
    ud, um, uv = _adamw_small(*[[d[k] for k in small_all] for d in (sg, w, m, v)], name="adamw_small")
    for i, k in enumerate(small_all):
        out_g[k], out_d[k], out_m[k], out_v[k] = sg[k], ud[i], um[i], uv[i]

    return (loss, grad_x[None], *[out_g[k] for k in WEIGHTS], *[out_d[k] for k in WEIGHTS],
            *[out_m[k] for k in WEIGHTS], *[out_v[k] for k in WEIGHTS])
```

```python
import functools
import math

import jax
import jax.numpy as jnp
from jax import lax
from jax.experimental import pallas as pl
from jax.experimental.pallas import tpu as pltpu

F32 = jnp.float32
BF16 = jnp.bfloat16

DEPTH = 4
HEAD_DIM = 64
BLOCK = 128
WINDOW = 128
GRID_W = 64
C_WIDTH = 512
C_GROUPS = 4
CHUNK = 128
N_BRANCH = 3
BRANCH_WIDTH = 512
ROPE_THETA = 10000.0
X_HEADS = 4
X_HEAD_DIM = 128
ALPHA = (2 * DEPTH) ** 0.25
LN_EPS = 1e-5
RMS_EPS = 1e-6
ADAM_LR = 0.001
ADAM_B1 = 0.9
ADAM_B2 = 0.999
ADAM_EPS = 1e-08
ADAM_WD = 0.01
ADAM_STEP = 10
N_DEV = 8

COL_A = 0
COL_B = 768
COL_C = 1536
COL_GATE = 2560
QKV_W = 768

LANES = 128
V7X_VMEM_BYTES = 64 * 1024 * 1024
VMEM_LIMIT = V7X_VMEM_BYTES - 8 * 1024 * 1024
NEG_BIG = -1e30
ROW_TILE = 512

_NT = (((1,), (1,)), ((), ()))
_TN = (((0,), (0,)), ((), ()))
_NN = (((1,), (0,)), ((), ()))


def _cp(sem=None):
    return pltpu.CompilerParams(dimension_semantics=sem, vmem_limit_bytes=VMEM_LIMIT)


def _tile(n, target, align=LANES):
    if n <= target:
        return n
    best = None
    for t in range(align, target + 1, align):
        if n % t == 0:
            best = t
    assert best is not None, (n, target)
    return best


def _dot(a, b, dims=_NN):
    return lax.dot_general(a, b, dims, preferred_element_type=F32)


def _gelu(x):
    return 0.5 * x * (1.0 + lax.erf(x * 0.7071067811865476))


def _gelu_and_grad(x):
    cdf = 0.5 * (1.0 + lax.erf(x * 0.7071067811865476))
    return x * cdf, cdf + x * jnp.exp(-0.5 * x * x) * 0.3989422804014327


def _sigmoid(x):
    return 1.0 / (1.0 + jnp.exp(-x))


MESH_ID = pl.DeviceIdType.MESH
_ANY = pl.BlockSpec(memory_space=pl.ANY)
COPIES_PER_ARRAY = N_DEV - 1


def _comm_scratch(n_arrays):
    return [pltpu.SemaphoreType.DMA((COPIES_PER_ARRAY * n_arrays,)),
            pltpu.SemaphoreType.DMA((COPIES_PER_ARRAY * n_arrays,)), pltpu.SemaphoreType.DMA((n_arrays,))]


def _gathered_shape(x):
    return jax.ShapeDtypeStruct((N_DEV,) + x.shape[1:], x.dtype)


def _gather_plan(entries, send_sems, recv_sems, local_sems):
    mx, my, mc = lax.axis_index("x"), lax.axis_index("y"), lax.axis_index("c")
    me, sibling = (mx, my, mc), (mx, my, 1 - mc)
    chips = [(1 - mx, my), (mx, 1 - my), (1 - mx, 1 - my)]

    def copy(a, k, block, to, from_shard=False):
        x_ref, l, out_ref = entries[a]
        dst = out_ref.at[4 * block[0] + 2 * block[1] + block[2]]
        return pltpu.make_async_remote_copy(
            src_ref=x_ref.at[l] if from_shard else dst, dst_ref=dst,
            send_sem=send_sems.at[COPIES_PER_ARRAY * a + k], recv_sem=recv_sems.at[COPIES_PER_ARRAY * a + k],
            device_id=to, device_id_type=MESH_ID)

    def own(a):
        x_ref, l, out_ref = entries[a]
        return pltpu.make_async_copy(x_ref.at[l], out_ref.at[4 * mx + 2 * my + mc], local_sems.at[a])

    def first(a):
        return [copy(a, 0, me, sibling, True)] + [copy(a, 1 + j, me, (*chip, mc), True) for j, chip in enumerate(chips)]

    def passed(a):
        return [copy(a, 4 + j, (*chip, mc), sibling) for j, chip in enumerate(chips)]

    def start():
        for a in range(len(entries)):
            own(a).start()
            for cp in first(a):
                cp.start()

    def finish():
        for a in range(len(entries)):
            fwd = passed(a)
            for j, chip in enumerate(chips):
                copy(a, 1 + j, (*chip, mc), me).wait_recv()
                fwd[j].start()
        for a in range(len(entries)):
            copy(a, 0, sibling, me).wait_recv()
            for j, chip in enumerate(chips):
                copy(a, 4 + j, (*chip, 1 - mc), me).wait_recv()
            for cp in first(a) + passed(a):
                cp.wait_send()
            own(a).wait()

    return start, finish


def _scatter_plan(entries, send_sems, recv_sems, local_sems):
    mx, my, mc = lax.axis_index("x"), lax.axis_index("y"), lax.axis_index("c")
    me = 4 * mx + 2 * my + mc

    def src(a, dev):
        return entries[a][0].at[dev]

    def copies(a):
        _, recv_ref, lr = entries[a]
        out = []
        for k in range(1, N_DEV):
            px = 1 - mx if k & 4 else mx
            py = 1 - my if k & 2 else my
            pc = 1 - mc if k & 1 else mc
            peer = 4 * px + 2 * py + pc
            sems = dict(send_sem=send_sems.at[COPIES_PER_ARRAY * a + k - 1],
                        recv_sem=recv_sems.at[COPIES_PER_ARRAY * a + k - 1],
                        device_id=(px, py, pc), device_id_type=MESH_ID)
            sends = pltpu.make_async_remote_copy(src_ref=src(a, peer), dst_ref=recv_ref.at[lr, me], **sems)
            lands = pltpu.make_async_remote_copy(src_ref=src(a, me), dst_ref=recv_ref.at[lr, peer], **sems)
            out.append((sends, lands))
        return out

    def own(a):
        _, recv_ref, lr = entries[a]
        return pltpu.make_async_copy(src(a, me), recv_ref.at[lr, me], local_sems.at[a])

    def start():
        for a in range(len(entries)):
            own(a).start()
            for sends, _ in copies(a):
                sends.start()

    def finish():
        for a in range(len(entries)):
            for _, lands in copies(a):
                lands.wait_recv()
        for a in range(len(entries)):
            for sends, _ in copies(a):
                sends.wait_send()
            own(a).wait()

    return start, finish


MM_TILE, MM_TK = 1536, 2048
MM_TILE_LN = 512


def _mm(a, b, *, ta=False, tb=False, out_dtype=F32, res=None, res_scale=1.0, into=None, ln_bwd=None, scatter=(),
        name):
    segs = list(a) if isinstance(a, (list, tuple)) else [a]
    if ta:
        (K, M), seg_k = segs[0].shape, [segs[0].shape[0]]
        assert len(segs) == 1
    else:
        M, seg_k = segs[0].shape[0], [s.shape[1] for s in segs]
        K = sum(seg_k)
    if tb:
        N, Kb = b.shape
    else:
        Kb, N = b.shape
    assert K == Kb, ([s.shape for s in segs], b.shape, ta, tb)
    row_off = into[1] if into is not None else 0
    tm, tn = _tile(math.gcd(M, row_off), MM_TILE if ln_bwd is None else MM_TILE_LN), _tile(N, MM_TILE)
    tk = _tile(K, MM_TK) if len(segs) == 1 else _tile(math.gcd(*seg_k), MM_TILE)
    nk = K // tk
    seg_chunks = [ks // tk for ks in seg_k]
    seg_first = [sum(seg_chunks[:s]) for s in range(len(segs))]
    dims = (((0 if ta else 1,), (1 if tb else 0,)), ((), ()))
    ns = len(segs)
    n_res = ns + 1
    n_ln = n_res + (res is not None)
    into_held = into is not None and not isinstance(into[0], jax.ShapeDtypeStruct)
    n_own = n_ln + (3 if ln_bwd is not None else 0) + into_held
    assert ln_bwd is None or (tn == N and into is None)
    na = len(scatter)
    assert na == 0 or (ln_bwd is None and into is None)
    comm_in, comm_out, held = _scatter_io(scatter)
    n_in = n_own + len(comm_in)
    grid = (M // tm, N // tn, nk)

    def body(*refs):
        a_refs, b_ref = refs[:ns], refs[ns]
        r_ref = refs[n_res] if res is not None else None
        o_ref = refs[n_in]
        first_row_tile = pl.program_id(0) == 0
        if na:
            ids = [pl.program_id(d) for d in range(3)]
            start, wait_all = _scatter_plan(
                [(refs[n_own + a], refs[n_in + 1 + a], scatter[a][2]) for a in range(na)], *refs[len(refs) - 3:])
            pl.when((ids[0] == 0) & (ids[1] == 0) & (ids[2] == 0))(start)
            last_step = (ids[0] == grid[0] - 1) & (ids[1] == grid[1] - 1) & (ids[2] == grid[2] - 1)

        def finish(out):
            if r_ref is not None:
                out = out + res_scale * r_ref[...]
            if ln_bwd is None:
                o_ref[...] = out.astype(o_ref.dtype)
                return
            xh_ref, rs_ref, g_ref = refs[n_ln:n_ln + 3]
            ob_ref, dg_ref, db_ref = refs[n_in + 1:n_in + 4]

            @pl.when(first_row_tile)
            def _():
                dg_ref[...] = jnp.zeros_like(dg_ref)
                db_ref[...] = jnp.zeros_like(db_ref)

            xh = xh_ref[...]
            dxh = out * g_ref[...]
            m1 = jnp.mean(dxh, axis=-1, keepdims=True)
            m2 = jnp.mean(dxh * xh, axis=-1, keepdims=True)
            dz = rs_ref[...] * (dxh - m1 - xh * m2)
            o_ref[...] = dz
            ob_ref[...] = dz.astype(BF16)
            dg_ref[...] += jnp.sum(out * xh, axis=0, keepdims=True)
            db_ref[...] += jnp.sum(out, axis=0, keepdims=True)

        def prod(s):
            return _dot(a_refs[s][...].astype(BF16), b_ref[...].astype(BF16), dims)

        if nk == 1:
            finish(prod(0))
        else:
            acc = refs[n_in + (4 if ln_bwd is not None else 1) + na]
            k = pl.program_id(2)

            @pl.when(k == 0)
            def _():
                acc[...] = jnp.zeros_like(acc)

            for s in range(ns):
                def add(s=s):
                    acc[...] += prod(s)
                pl.when((k >= seg_first[s]) & (k < seg_first[s] + seg_chunks[s]))(add)

            @pl.when(k == nk - 1)
            def _():
                finish(acc[...])
        if na:
            pl.when(last_step)(wait_all)

    if ta:
        a_specs = [pl.BlockSpec((tk, tm), lambda i, j, k: (k, i))]
    else:
        a_specs = [pl.BlockSpec((tm, tk), functools.partial(
            lambda i, j, k, first, n: (i, jnp.clip(k - first, 0, n - 1)), first=seg_first[s], n=seg_chunks[s]))
            for s in range(ns)]
    b_spec = pl.BlockSpec((tn, tk), lambda i, j, k: (j, k)) if tb else pl.BlockSpec((tk, tn), lambda i, j, k: (k, j))
    in_specs = a_specs + [b_spec]
    args = segs + [b]
    if res is not None:
        in_specs.append(pl.BlockSpec((tm, tn), lambda i, j, k: (i, j)))
        args.append(res)
    out_spec = pl.BlockSpec((tm, tn), lambda i, j, k: (i, j))
    if ln_bwd is not None:
        xh, rs, g = ln_bwd
        in_specs += [out_spec, pl.BlockSpec((tm, 1), lambda i, j, k: (i, 0)), pl.BlockSpec((1, tn), lambda i, j, k: (0, j))]
        args += [xh, rs, g]
        vec = pl.BlockSpec((1, tn), lambda i, j, k: (0, j))
        return pl.pallas_call(
            body, name=name,
            out_shape=(jax.ShapeDtypeStruct((M, N), F32), jax.ShapeDtypeStruct((M, N), BF16),
                       jax.ShapeDtypeStruct((1, N), F32), jax.ShapeDtypeStruct((1, N), F32)),
            grid=(M // tm, N // tn, nk),
            in_specs=in_specs,
            out_specs=(out_spec, out_spec, vec, vec),
            scratch_shapes=[pltpu.VMEM((tm, tn), F32)] if nk > 1 else [],
            compiler_params=_cp(("arbitrary", "arbitrary", "arbitrary")),
        )(*args)
    if na:
        res_all = pl.pallas_call(
            body, name=name,
            out_shape=(jax.ShapeDtypeStruct((M, N), out_dtype), *comm_out),
            grid=grid,
            in_specs=in_specs + [_ANY] * len(comm_in),
            out_specs=(out_spec, *([_ANY] * na)),
            scratch_shapes=([pltpu.VMEM((tm, tn), F32)] if nk > 1 else []) + _comm_scratch(na),
            input_output_aliases={n_own + na + i: 1 + a for i, a in enumerate(held)},
            compiler_params=_cp(("arbitrary", "arbitrary", "arbitrary")),
        )(*args, *comm_in)
        return res_all[0], list(res_all[1:])
    if into is None:
        out_shape = jax.ShapeDtypeStruct((M, N), out_dtype)
        blk_off, aliases = 0, {}
    else:
        buf = into[0]
        assert buf.shape[1] == N and row_off % tm == 0 and row_off + M <= buf.shape[0], (buf.shape, M, N, row_off)
        out_shape = jax.ShapeDtypeStruct(buf.shape, buf.dtype)
        blk_off, aliases = row_off // tm, {}
        if into_held:
            aliases = {n_in - 1: 0}
            in_specs.append(_ANY)
            args.append(buf)
    return pl.pallas_call(
        body, name=name,
        out_shape=out_shape,
        grid=(M // tm, N // tn, nk),
        in_specs=in_specs,
        out_specs=pl.BlockSpec((tm, tn), lambda i, j, k: (i + blk_off, j)),
        scratch_shapes=[pltpu.VMEM((tm, tn), F32)] if nk > 1 else [],
        input_output_aliases=aliases,
        compiler_params=_cp(("parallel", "parallel", "arbitrary")),
    )(*args)


def _mm_res_ln(a, w, x, g, b, *, name):
    S, K = a.shape
    D = w.shape[1]
    tm = _tile(S, ROW_TILE)

    def body(a_ref, w_ref, x_ref, g_ref, b_ref, y_ref, yb_ref, xh_ref, rs_ref):
        h = _dot(a_ref[...], w_ref[...])
        z = ALPHA * x_ref[...] + h
        mu = jnp.mean(z, axis=-1, keepdims=True)
        zc = z - mu
        var = jnp.mean(zc * zc, axis=-1, keepdims=True)
        r = lax.rsqrt(var + LN_EPS)
        xh = zc * r
        y = xh * g_ref[...] + b_ref[...]
        y_ref[...] = y
        yb_ref[...] = y.astype(BF16)
        xh_ref[...] = xh
        rs_ref[...] = r

    row = lambda i: (i, 0)
    full = lambda i: (0, 0)
    return pl.pallas_call(
        body, name=name,
        out_shape=(jax.ShapeDtypeStruct((S, D), F32), jax.ShapeDtypeStruct((S, D), BF16),
                   jax.ShapeDtypeStruct((S, D), F32), jax.ShapeDtypeStruct((S, 1), F32)),
        grid=(S // tm,),
        in_specs=[pl.BlockSpec((tm, K), row), pl.BlockSpec((K, D), full), pl.BlockSpec((tm, D), row),
                  pl.BlockSpec((1, D), full), pl.BlockSpec((1, D), full)],
        out_specs=(pl.BlockSpec((tm, D), row), pl.BlockSpec((tm, D), row), pl.BlockSpec((tm, D), row),
                   pl.BlockSpec((tm, 1), row)),
        compiler_params=_cp(("parallel",)),
    )(a, w, x, g, b)


def _ln_bwd(dy, xh, rs, g, *, name):
    S, D = dy.shape
    tm = _tile(S, ROW_TILE)

    def body(dy_ref, xh_ref, rs_ref, g_ref, dz_ref, dzb_ref, dg_ref, db_ref):
        @pl.when(pl.program_id(0) == 0)
        def _():
            dg_ref[...] = jnp.zeros_like(dg_ref)
            db_ref[...] = jnp.zeros_like(db_ref)

        dy = dy_ref[...]
        xh = xh_ref[...]
        dxh = dy * g_ref[...]
        m1 = jnp.mean(dxh, axis=-1, keepdims=True)
        m2 = jnp.mean(dxh * xh, axis=-1, keepdims=True)
        dz = rs_ref[...] * (dxh - m1 - xh * m2)
        dz_ref[...] = dz
        dzb_ref[...] = dz.astype(BF16)
        dg_ref[...] += jnp.sum(dy * xh, axis=0, keepdims=True)
        db_ref[...] += jnp.sum(dy, axis=0, keepdims=True)

    row = lambda i: (i, 0)
    full = lambda i: (0, 0)
    return pl.pallas_call(
        body, name=name,
        out_shape=(jax.ShapeDtypeStruct((S, D), F32), jax.ShapeDtypeStruct((S, D), BF16),
                   jax.ShapeDtypeStruct((1, D), F32), jax.ShapeDtypeStruct((1, D), F32)),
        grid=(S // tm,),
        in_specs=[pl.BlockSpec((tm, D), row), pl.BlockSpec((tm, D), row), pl.BlockSpec((tm, 1), row),
                  pl.BlockSpec((1, D), full)],
        out_specs=(pl.BlockSpec((tm, D), row), pl.BlockSpec((tm, D), row), pl.BlockSpec((1, D), full),
                   pl.BlockSpec((1, D), full)),
        compiler_params=_cp(("arbitrary",)),
    )(dy, xh, rs, g)


def _loss_head(y, t, *, name):
    S, D = y.shape
    tm = _tile(S, 512)

    def body(y_ref, t_ref, dy_ref, l_ref):
        @pl.when(pl.program_id(0) == 0)
        def _():
            l_ref[...] = jnp.zeros_like(l_ref)

        e = y_ref[...] - t_ref[...]
        dy_ref[...] = e / D
        l_ref[...] += 0.5 * jnp.sum(jnp.mean(e * e, axis=-1, keepdims=True), axis=0, keepdims=True)

    row = lambda i: (i, 0)
    return pl.pallas_call(
        body, name=name,
        out_shape=(jax.ShapeDtypeStruct((S, D), F32), jax.ShapeDtypeStruct((1, 1), F32)),
        grid=(S // tm,),
        in_specs=[pl.BlockSpec((tm, D), row), pl.BlockSpec((tm, D), row)],
        out_specs=(pl.BlockSpec((tm, D), row), pl.BlockSpec((1, 1), lambda i: (0, 0))),
        compiler_params=_cp(("arbitrary",)),
    )(y, t)


def _rope_tables(S):
    pos = jnp.arange(S, dtype=jnp.int32)[:, None]
    lane = jnp.arange(LANES, dtype=jnp.int32)[None, :] % HEAD_DIM

    def inv(d):
        half = d // 2
        return jnp.tile(ROPE_THETA ** (-jnp.arange(half, dtype=F32) * (2.0 / d)), LANES // half)[None, :]

    def cs(p, d):
        ang = p.astype(F32) * inv(d)
        return jnp.cos(ang), jnp.where(lane % d < d // 2, -jnp.sin(ang), jnp.sin(ang))

    ca, sa = cs(pos, HEAD_DIM)
    axial = jnp.where(lane < HEAD_DIM // 2, pos // GRID_W, pos % GRID_W)
    cb, sb = cs(axial, HEAD_DIM // 2)
    return ca, sa, cb, sb


def _partner(x, lane, width):
    h = width // 2
    return jnp.where(lane % width < h, pltpu.roll(x, LANES - h, 1), pltpu.roll(x, h, 1))


def _rope_fwd(x, c, s, lane, width):
    return x * c + _partner(x, lane, width) * s


def _rope_bwd(dy, c, s, lane, width):
    return dy * c + _partner(dy * s, lane, width)


def _head_sum(x, seg):
    return lax.dot_general(x, seg, _NN, precision=lax.Precision.HIGHEST, preferred_element_type=F32)


def _split_heads(x, lane):
    lo = lane < HEAD_DIM
    r = pltpu.roll(x, HEAD_DIM, 1)
    z = jnp.zeros_like(x)
    return jnp.where(lo, x, z), jnp.where(lo, z, r), jnp.where(lo, r, z), jnp.where(lo, z, x)


def _fold_heads(d0, d1, lane):
    t0 = d0 + pltpu.roll(d0, HEAD_DIM, 1)
    t1 = d1 + pltpu.roll(d1, HEAD_DIM, 1)
    return jnp.where(lane < HEAD_DIM, t0, t1)


def _seg_matrix():
    i = jnp.arange(LANES)
    return (i[:, None] // HEAD_DIM == i[None, :] // HEAD_DIM).astype(F32)


def _prep(proj, tabs, qg2, kg2, seg, *, name):
    S = proj.shape[0]
    ts = _tile(S, ROW_TILE)
    ca, sa, cb, sb = tabs

    def body(pa_ref, pb_ref, ca_ref, sa_ref, cb_ref, sb_ref, qg_ref, kg_ref, seg_ref,
             aq_ref, ak_ref, av_ref, bq_ref, bk_ref, bv_ref):
        lane = lax.broadcasted_iota(jnp.int32, (ts, LANES), 1)
        ca, sa, cb, sb = ca_ref[...], sa_ref[...], cb_ref[...], sb_ref[...]
        seg = seg_ref[...]

        def norm(x, gain):
            r = lax.rsqrt(_head_sum(x * x, seg) * (1.0 / HEAD_DIM) + RMS_EPS)
            return x * r * gain

        def put(ref, x):
            for i, part in enumerate(_split_heads(x, lane)):
                ref[i] = part.astype(BF16)

        for gidx in range(4):
            cols = slice(gidx * LANES, (gidx + 1) * LANES)
            aq_ref[:, cols] = (_rope_fwd(pa_ref[:, cols], ca, sa, lane, HEAD_DIM) * 0.125).astype(BF16)
            bq = norm(pb_ref[:, cols], qg_ref[...])
            bq_ref[:, cols] = (_rope_fwd(bq, cb, sb, lane, HEAD_DIM // 2) * 0.125).astype(BF16)
        put(ak_ref, _rope_fwd(pa_ref[:, 512:640], ca, sa, lane, HEAD_DIM))
        put(av_ref, pa_ref[:, 640:768])
        bk = norm(pb_ref[:, 512:640], kg_ref[...])
        put(bk_ref, _rope_fwd(bk, cb, sb, lane, HEAD_DIM // 2))
        put(bv_ref, pb_ref[:, 640:768])

    row = lambda i: (i, 0)
    full = lambda i: (0, 0)
    tab = pl.BlockSpec((ts, LANES), row)
    kv_shape = jax.ShapeDtypeStruct((4, S, LANES), BF16)
    kv_spec = pl.BlockSpec((4, ts, LANES), lambda i: (0, i, 0))
    q_shape = jax.ShapeDtypeStruct((S, 512), BF16)
    q_spec = pl.BlockSpec((ts, 512), row)
    return pl.pallas_call(
        body, name=name,
        out_shape=(q_shape, kv_shape, kv_shape, q_shape, kv_shape, kv_shape),
        grid=(S // ts,),
        in_specs=[pl.BlockSpec((ts, QKV_W), lambda i: (i, 0)), pl.BlockSpec((ts, QKV_W), lambda i: (i, 1)),
                  tab, tab, tab, tab, pl.BlockSpec((1, LANES), full), pl.BlockSpec((1, LANES), full),
                  pl.BlockSpec((LANES, LANES), full)],
        out_specs=(q_spec, kv_spec, kv_spec, q_spec, kv_spec, kv_spec),
        compiler_params=_cp(("parallel",)),
    )(proj, proj, ca, sa, cb, sb, qg2, kg2, seg)


def _unprep(dqa, dka, dva, dqb, dkb, dvb, proj, tabs, qg2, kg2, seg, *, name):
    S = proj.shape[0]
    ts = _tile(S, ROW_TILE)
    ca, sa, cb, sb = tabs

    def body(dqa_ref, dka_ref, dva_ref, dqb_ref, dkb_ref, dvb_ref, pb_ref, ca_ref, sa_ref, cb_ref, sb_ref,
             qg_ref, kg_ref, seg_ref, dp_ref, dqg_ref, dkg_ref):
        @pl.when(pl.program_id(0) == 0)
        def _():
            dqg_ref[...] = jnp.zeros_like(dqg_ref)
            dkg_ref[...] = jnp.zeros_like(dkg_ref)

        lane = lax.broadcasted_iota(jnp.int32, (ts, LANES), 1)
        ca, sa, cb, sb = ca_ref[...], sa_ref[...], cb_ref[...], sb_ref[...]
        seg = seg_ref[...]

        def norm_bwd(dy, x, gain):
            r = lax.rsqrt(_head_sum(x * x, seg) * (1.0 / HEAD_DIM) + RMS_EPS)
            gdy = gain * dy
            dot = _head_sum(gdy * x, seg) * (1.0 / HEAD_DIM)
            dx = r * gdy - x * (r * r * r) * dot
            return dx, jnp.sum(dy * x * r, axis=0, keepdims=True)

        for gidx in range(4):
            cols = slice(gidx * LANES, (gidx + 1) * LANES)
            dp_ref[:, cols] = _rope_bwd(dqa_ref[:, cols] * 0.125, ca, sa, lane, HEAD_DIM).astype(BF16)
            dbq = _rope_bwd(dqb_ref[:, cols] * 0.125, cb, sb, lane, HEAD_DIM // 2)
            dx, dg = norm_bwd(dbq, pb_ref[:, cols], qg_ref[...])
            dp_ref[:, COL_B + gidx * LANES:COL_B + (gidx + 1) * LANES] = dx.astype(BF16)
            dqg_ref[...] += dg
        dak = _fold_heads(dka_ref[0] + dka_ref[1], dka_ref[2] + dka_ref[3], lane)
        dp_ref[:, 512:640] = _rope_bwd(dak, ca, sa, lane, HEAD_DIM).astype(BF16)
        dp_ref[:, 640:768] = _fold_heads(dva_ref[0] + dva_ref[1], dva_ref[2] + dva_ref[3], lane).astype(BF16)
        dbk = _fold_heads(dkb_ref[0] + dkb_ref[1], dkb_ref[2] + dkb_ref[3], lane)
        dbk = _rope_bwd(dbk, cb, sb, lane, HEAD_DIM // 2)
        dx, dg = norm_bwd(dbk, pb_ref[:, 512:640], kg_ref[...])
        dp_ref[:, COL_B + 512:COL_B + 640] = dx.astype(BF16)
        dkg_ref[...] += dg
        dp_ref[:, COL_B + 640:COL_B + 768] = _fold_heads(dvb_ref[0] + dvb_ref[1], dvb_ref[2] + dvb_ref[3],
                                                         lane).astype(BF16)

    row = lambda i: (i, 0)
    full = lambda i: (0, 0)
    tab = pl.BlockSpec((ts, LANES), row)
    q_spec = pl.BlockSpec((ts, 512), row)
    kv_spec = pl.BlockSpec((4, ts, LANES), lambda i: (0, i, 0))
    return pl.pallas_call(
        body, name=name,
        out_shape=(jax.ShapeDtypeStruct((S, 2 * QKV_W), BF16), jax.ShapeDtypeStruct((1, LANES), F32),
                   jax.ShapeDtypeStruct((1, LANES), F32)),
        grid=(S // ts,),
        in_specs=[q_spec, kv_spec, kv_spec, q_spec, kv_spec, kv_spec,
                  pl.BlockSpec((ts, QKV_W), lambda i: (i, 1)), tab, tab, tab, tab,
                  pl.BlockSpec((1, LANES), full), pl.BlockSpec((1, LANES), full), pl.BlockSpec((LANES, LANES), full)],
        out_specs=(pl.BlockSpec((ts, 2 * QKV_W), row), pl.BlockSpec((1, LANES), full),
                   pl.BlockSpec((1, LANES), full)),
        compiler_params=_cp(("arbitrary",)),
    )(dqa, dka, dva, dqb, dkb, dvb, proj, ca, sa, cb, sb, qg2, kg2, seg)


def _attn_dense_fwd(q, k4, v4, *, gather=(), name):
    S = q.shape[0]
    tq = _tile(S, 256)
    xs = [x for x, _ in gather]
    na = len(xs)

    def body(q_ref, k_ref, v_ref, *rest):
        o_ref, lse_ref = rest[na], rest[na + 1]
        if na:
            x_refs, out_refs, sems = rest[:na], rest[na + 2:2 * na + 2], rest[2 * na + 2:]
            start, finish = _gather_plan([(x_refs[a], gather[a][1], out_refs[a]) for a in range(na)], *sems)
            pl.when((pl.program_id(0) == 0) & (pl.program_id(1) == 0))(start)
        for pr in range(2):
            qp = q_ref[:, pr * LANES:(pr + 1) * LANES]
            acc = None
            for half in range(2):
                s = _dot(qp, k_ref[half], _NT)
                m = jnp.max(s, axis=-1, keepdims=True)
                e = jnp.exp(s - m)
                l = jnp.sum(e, axis=-1, keepdims=True)
                pv = _dot(e.astype(BF16), v_ref[half]) * (1.0 / l)
                acc = pv if acc is None else acc + pv
                lse_ref[pr * 2 + half] = m + jnp.log(l)
            o_ref[:, pr * LANES:(pr + 1) * LANES] = acc.astype(BF16)
        if na:
            pl.when((pl.program_id(0) == 1) & (pl.program_id(1) == S // tq - 1))(finish)

    kv_spec = pl.BlockSpec((2, S, LANES), lambda kv, i: (kv, 0, 0))
    res = pl.pallas_call(
        body, name=name,
        out_shape=(jax.ShapeDtypeStruct((S, 512), BF16), jax.ShapeDtypeStruct((8, S, 1), F32),
                   *[_gathered_shape(x) for x in xs]),
        grid=(2, S // tq),
        in_specs=[pl.BlockSpec((tq, 256), lambda kv, i: (i, kv)), kv_spec, kv_spec] + [_ANY] * na,
        out_specs=(pl.BlockSpec((tq, 256), lambda kv, i: (i, kv)),
                   pl.BlockSpec((4, tq, 1), lambda kv, i: (kv, i, 0)), *([_ANY] * na)),
        scratch_shapes=_comm_scratch(na) if na else [],
        compiler_params=_cp(("arbitrary", "arbitrary") if na else ("parallel", "parallel")),
    )(q, k4, v4, *xs)
    return res[0], res[1], list(res[2:])


def _attn_dense_bwd(q, k4, v4, lse, do, *, scatter=(), name):
    S = q.shape[0]
    tq = _tile(S, 256)
    na = len(scatter)
    comm_in, comm_out, held = _scatter_io(scatter)
    n_in = len(comm_in)

    def body(q_ref, k_ref, v_ref, lse_ref, do_ref, *rest):
        dq_ref, dk_ref, dv_ref = rest[n_in:n_in + 3]
        if na:
            s_refs, r_refs, sems = rest[:na], rest[n_in + 3:n_in + 3 + na], rest[n_in + 3 + na:]
            start, finish = _scatter_plan([(s_refs[a], r_refs[a], scatter[a][2]) for a in range(na)], *sems)
            pl.when((pl.program_id(0) == 0) & (pl.program_id(1) == 0))(start)

        @pl.when(pl.program_id(1) == 0)
        def _():
            dk_ref[...] = jnp.zeros_like(dk_ref)
            dv_ref[...] = jnp.zeros_like(dv_ref)

        lane = lax.broadcasted_iota(jnp.int32, (tq, LANES), 1)
        for pr in range(2):
            qp = q_ref[:, pr * LANES:(pr + 1) * LANES]
            dop = do_ref[:, pr * LANES:(pr + 1) * LANES].astype(BF16)
            dq = None
            for half in range(2):
                mine = (lane < HEAD_DIM) if half == 0 else (lane >= HEAD_DIM)
                s = _dot(qp, k_ref[half], _NT)
                p = jnp.exp(s - lse_ref[pr * 2 + half])
                dp = _dot(dop, v_ref[half], _NT)
                delta = jnp.sum(p * dp, axis=-1, keepdims=True)
                ds = (p * (dp - delta)).astype(BF16)
                pb = p.astype(BF16)
                d = _dot(ds, k_ref[half])
                dq = d if dq is None else dq + d
                dk_ref[half] += _dot(ds, jnp.where(mine, qp, jnp.zeros_like(qp)), _TN)
                dv_ref[half] += _dot(pb, jnp.where(mine, dop, jnp.zeros_like(dop)), _TN)
            dq_ref[:, pr * LANES:(pr + 1) * LANES] = dq
        if na:
            pl.when((pl.program_id(0) == 1) & (pl.program_id(1) == S // tq - 1))(finish)

    kv_spec = pl.BlockSpec((2, S, LANES), lambda kv, i: (kv, 0, 0))
    q_spec = pl.BlockSpec((tq, 256), lambda kv, i: (i, kv))
    res = pl.pallas_call(
        body, name=name,
        out_shape=(jax.ShapeDtypeStruct((S, 512), F32), jax.ShapeDtypeStruct((4, S, LANES), F32),
                   jax.ShapeDtypeStruct((4, S, LANES), F32), *comm_out),
        grid=(2, S // tq),
        in_specs=[q_spec, kv_spec, kv_spec, pl.BlockSpec((4, tq, 1), lambda kv, i: (kv, i, 0)), q_spec]
                 + [_ANY] * n_in,
        out_specs=(q_spec, kv_spec, kv_spec, *([_ANY] * na)),
        scratch_shapes=_comm_scratch(na) if na else [],
        input_output_aliases={5 + na + i: 3 + a for i, a in enumerate(held)},
        compiler_params=_cp(("arbitrary", "arbitrary") if na else ("parallel", "arbitrary")),
    )(q, k4, v4, lse, do, *comm_in)
    return res[0], res[1], res[2], list(res[3:])


WIN_Q = 2 * BLOCK
WIN_KEYS = WIN_Q + 2 * WINDOW


def _win_start(n, S):
    return pl.multiple_of(jnp.clip(n * WIN_Q - WINDOW, 0, S - WIN_KEYS), BLOCK)


def _win_valid(n, start):
    qpos = n * WIN_Q + lax.broadcasted_iota(jnp.int32, (WIN_Q, WIN_KEYS), 0)
    kpos = start + lax.broadcasted_iota(jnp.int32, (WIN_Q, WIN_KEYS), 1)
    return jnp.abs(qpos - kpos) <= WINDOW


def _attn_win_fwd(q, k4, v4, sink, *, name):
    S = q.shape[0]
    assert S >= WIN_KEYS

    def body(sink_ref, q_ref, k_ref, v_ref, o_ref, lse_ref):
        n = pl.program_id(0)
        start = _win_start(n, S)
        valid = _win_valid(n, start)
        for kv in range(2):
            for pr in range(2):
                cols = slice((kv * 2 + pr) * LANES, (kv * 2 + pr + 1) * LANES)
                qp = q_ref[:, cols]
                acc = None
                for half in range(2):
                    h = kv * 4 + pr * 2 + half
                    kk = k_ref[kv * 2 + half, pl.ds(start, WIN_KEYS), :]
                    vv = v_ref[kv * 2 + half, pl.ds(start, WIN_KEYS), :]
                    s = jnp.where(valid, _dot(qp, kk, _NT), NEG_BIG)
                    snk = sink_ref[h]
                    m = jnp.maximum(jnp.max(s, axis=-1, keepdims=True), snk)
                    e = jnp.exp(s - m)
                    l = jnp.sum(e, axis=-1, keepdims=True) + jnp.exp(snk - m)
                    pv = _dot(e.astype(BF16), vv) * (1.0 / l)
                    acc = pv if acc is None else acc + pv
                    lse_ref[h] = m + jnp.log(l)
                o_ref[:, cols] = acc.astype(BF16)

    kv_spec = pl.BlockSpec((4, S, LANES), lambda n: (0, 0, 0))
    return pl.pallas_call(
        body, name=name,
        out_shape=(jax.ShapeDtypeStruct((S, 512), BF16), jax.ShapeDtypeStruct((8, S, 1), F32)),
        grid=(S // WIN_Q,),
        in_specs=[pl.BlockSpec(memory_space=pltpu.SMEM), pl.BlockSpec((WIN_Q, 512), lambda n: (n, 0)),
                  kv_spec, kv_spec],
        out_specs=(pl.BlockSpec((WIN_Q, 512), lambda n: (n, 0)), pl.BlockSpec((8, WIN_Q, 1), lambda n: (0, n, 0))),
        compiler_params=_cp(("parallel",)),
    )(sink, q, k4, v4)


def _attn_win_bwd(q, k4, v4, sink, lse, do, *, name):
    S = q.shape[0]

    def body(sink_ref, q_ref, k_ref, v_ref, lse_ref, do_ref, dq_ref, dk_ref, dv_ref, dsink_ref):
        n = pl.program_id(0)

        @pl.when(n == 0)
        def _():
            dk_ref[...] = jnp.zeros_like(dk_ref)
            dv_ref[...] = jnp.zeros_like(dv_ref)
            dsink_ref[...] = jnp.zeros_like(dsink_ref)

        start = _win_start(n, S)
        valid = _win_valid(n, start)
        lane = lax.broadcasted_iota(jnp.int32, (WIN_Q, LANES), 1)
        for kv in range(2):
            for pr in range(2):
                cols = slice((kv * 2 + pr) * LANES, (kv * 2 + pr + 1) * LANES)
                qp = q_ref[:, cols]
                dop = do_ref[:, cols].astype(BF16)
                dq = None
                for half in range(2):
                    h = kv * 4 + pr * 2 + half
                    slot = kv * 2 + half
                    mine = (lane < HEAD_DIM) if half == 0 else (lane >= HEAD_DIM)
                    win = pl.ds(start, WIN_KEYS)
                    kk = k_ref[slot, win, :]
                    vv = v_ref[slot, win, :]
                    lse_h = lse_ref[h]
                    s = jnp.where(valid, _dot(qp, kk, _NT), NEG_BIG)
                    p = jnp.exp(s - lse_h)
                    dp = _dot(dop, vv, _NT)
                    delta = jnp.sum(p * dp, axis=-1, keepdims=True)
                    ds = (p * (dp - delta)).astype(BF16)
                    pb = p.astype(BF16)
                    d = _dot(ds, kk)
                    dq = d if dq is None else dq + d
                    dk_ref[slot, win, :] += _dot(ds, jnp.where(mine, qp, jnp.zeros_like(qp)), _TN)
                    dv_ref[slot, win, :] += _dot(pb, jnp.where(mine, dop, jnp.zeros_like(dop)), _TN)
                    p_sink = jnp.exp(sink_ref[h] - lse_h)
                    dsink_ref[h:h + 1, :] += jnp.broadcast_to(-jnp.sum(p_sink * delta, axis=0, keepdims=True),
                                                              (1, LANES))
                dq_ref[:, cols] = dq

    kv_spec = pl.BlockSpec((4, S, LANES), lambda n: (0, 0, 0))
    q_spec = pl.BlockSpec((WIN_Q, 512), lambda n: (n, 0))
    return pl.pallas_call(
        body, name=name,
        out_shape=(jax.ShapeDtypeStruct((S, 512), F32), jax.ShapeDtypeStruct((4, S, LANES), F32),
                   jax.ShapeDtypeStruct((4, S, LANES), F32), jax.ShapeDtypeStruct((8, LANES), F32)),
        grid=(S // WIN_Q,),
        in_specs=[pl.BlockSpec(memory_space=pltpu.SMEM), q_spec, kv_spec, kv_spec,
                  pl.BlockSpec((8, WIN_Q, 1), lambda n: (0, n, 0)), q_spec],
        out_specs=(q_spec, kv_spec, kv_spec, pl.BlockSpec((8, LANES), lambda n: (0, 0))),
        compiler_params=_cp(("arbitrary",)),
    )(sink, q, k4, v4, lse, do)


def _c_ln(v, g, b):
    mu = jnp.mean(v, axis=-1, keepdims=True)
    vc = v - mu
    r = lax.rsqrt(jnp.mean(vc * vc, axis=-1, keepdims=True) + LN_EPS)
    vh = vc * r
    return vh, r, vh * g + b


def _gmlp_blocks(rows):
    return [(slice(c * CHUNK, (c + 1) * CHUNK), slice(gi * LANES, (gi + 1) * LANES), gi)
            for c in range(rows // CHUNK) for gi in range(C_GROUPS)]


def _gmlp_fwd(proj, ws, bs3, lg, lb, *, name):
    S = proj.shape[0]
    rows = _tile(S, ROW_TILE)

    def body(u_ref, v_ref, ws_ref, bs_ref, lg_ref, lb_ref, o_ref):
        u = _gelu(u_ref[...])
        _, _, vn = _c_ln(_gelu(v_ref[...]), lg_ref[...], lb_ref[...])
        vn = vn.astype(BF16)
        for r, cols, gi in _gmlp_blocks(rows):
            mixed = _dot(ws_ref[gi], vn[r, cols]) + bs_ref[gi]
            o_ref[r, cols] = (u[r, cols] * mixed).astype(BF16)

    full2 = lambda n: (0, 0)
    full3 = lambda n: (0, 0, 0)
    return pl.pallas_call(
        body, name=name,
        out_shape=jax.ShapeDtypeStruct((S, C_WIDTH), BF16),
        grid=(S // rows,),
        in_specs=[pl.BlockSpec((rows, C_WIDTH), lambda n: (n, COL_C // C_WIDTH)),
                  pl.BlockSpec((rows, C_WIDTH), lambda n: (n, COL_C // C_WIDTH + 1)),
                  pl.BlockSpec((C_GROUPS, CHUNK, CHUNK), full3), pl.BlockSpec((C_GROUPS, CHUNK, 1), full3),
                  pl.BlockSpec((1, C_WIDTH), full2), pl.BlockSpec((1, C_WIDTH), full2)],
        out_specs=pl.BlockSpec((rows, C_WIDTH), lambda n: (n, 0)),
        compiler_params=_cp(("parallel",)),
    )(proj, proj, ws, bs3, lg, lb)


def _gmlp_bwd(proj, dout, ws, bs3, lg, lb, *, name):
    S = proj.shape[0]
    rows = _tile(S, ROW_TILE)
    nch = rows // CHUNK

    def body(u_ref, v_ref, d_ref, ws_ref, bs_ref, lg_ref, lb_ref, dz_ref, dws_ref, dbs_ref, dlg_ref, dlb_ref):
        @pl.when(pl.program_id(0) == 0)
        def _():
            dws_ref[...] = jnp.zeros_like(dws_ref)
            dbs_ref[...] = jnp.zeros_like(dbs_ref)
            dlg_ref[...] = jnp.zeros_like(dlg_ref)
            dlb_ref[...] = jnp.zeros_like(dlb_ref)

        u_pre, v_pre, d = u_ref[...], v_ref[...], d_ref[...]
        u, u_grad = _gelu_and_grad(u_pre)
        v, v_grad = _gelu_and_grad(v_pre)
        vh, r, vn = _c_ln(v, lg_ref[...], lb_ref[...])
        vnb = vn.astype(BF16)
        dm = d * u
        dvn_parts = []
        dws = [None] * C_GROUPS
        dbs = [None] * C_GROUPS
        for rs, cols, gi in _gmlp_blocks(rows):
            mixed = _dot(ws_ref[gi], vnb[rs, cols]) + bs_ref[gi]
            dz_ref[rs, cols] = (d[rs, cols] * mixed * u_grad[rs, cols]).astype(BF16)
            dmb = dm[rs, cols].astype(BF16)
            t, b = _dot(dmb, vnb[rs, cols], _NT), jnp.sum(dm[rs, cols], axis=-1, keepdims=True)
            dws[gi] = t if dws[gi] is None else dws[gi] + t
            dbs[gi] = b if dbs[gi] is None else dbs[gi] + b
            dvn_parts.append(_dot(ws_ref[gi], dmb, _TN))
        for gi in range(C_GROUPS):
            dws_ref[gi] += dws[gi]
            dbs_ref[gi] += dbs[gi]
        dvn = jnp.concatenate([jnp.concatenate(dvn_parts[c * C_GROUPS:(c + 1) * C_GROUPS], axis=-1)
                               for c in range(nch)], axis=0)
        dlg_ref[...] += jnp.sum(dvn * vh, axis=0, keepdims=True)
        dlb_ref[...] += jnp.sum(dvn, axis=0, keepdims=True)
        dvh = dvn * lg_ref[...]
        m1 = jnp.mean(dvh, axis=-1, keepdims=True)
        m2 = jnp.mean(dvh * vh, axis=-1, keepdims=True)
        dv = r * (dvh - m1 - vh * m2)
        dz_ref[:, C_WIDTH:] = (dv * v_grad).astype(BF16)

    full2 = lambda n: (0, 0)
    full3 = lambda n: (0, 0, 0)
    return pl.pallas_call(
        body, name=name,
        out_shape=(jax.ShapeDtypeStruct((S, 2 * C_WIDTH), BF16), jax.ShapeDtypeStruct((C_GROUPS, CHUNK, CHUNK), F32),
                   jax.ShapeDtypeStruct((C_GROUPS, CHUNK, 1), F32), jax.ShapeDtypeStruct((1, C_WIDTH), F32),
                   jax.ShapeDtypeStruct((1, C_WIDTH), F32)),
        grid=(S // rows,),
        in_specs=[pl.BlockSpec((rows, C_WIDTH), lambda n: (n, COL_C // C_WIDTH)),
                  pl.BlockSpec((rows, C_WIDTH), lambda n: (n, COL_C // C_WIDTH + 1)),
                  pl.BlockSpec((rows, C_WIDTH), lambda n: (n, 0)),
                  pl.BlockSpec((C_GROUPS, CHUNK, CHUNK), full3), pl.BlockSpec((C_GROUPS, CHUNK, 1), full3),
                  pl.BlockSpec((1, C_WIDTH), full2), pl.BlockSpec((1, C_WIDTH), full2)],
        out_specs=(pl.BlockSpec((rows, 2 * C_WIDTH), lambda n: (n, 0)), pl.BlockSpec((C_GROUPS, CHUNK, CHUNK), full3),
                   pl.BlockSpec((C_GROUPS, CHUNK, 1), full3), pl.BlockSpec((1, C_WIDTH), full2),
                   pl.BlockSpec((1, C_WIDTH), full2)),
        compiler_params=_cp(("arbitrary",)),
    )(proj, proj, dout, ws, bs3, lg, lb)


GATE_BLK = 512


def _gate_specs(tm, D):
    nh = D // GATE_BLK
    first = COL_GATE // GATE_BLK
    return [pl.BlockSpec((tm, GATE_BLK), functools.partial(lambda i, c: (i, c), c=first + b))
            for b in range(N_BRANCH * nh)]


def _merge_fwd(oa, ob, oc, wb, proj, bg, *, name):
    S = oa.shape[0]
    D = wb.shape[2]
    assert D % GATE_BLK == 0
    nh = D // GATE_BLK
    tm = _tile(S, ROW_TILE)

    def body(oa_ref, ob_ref, oc_ref, wb_ref, *rest):
        gate_refs, bg_ref, o_ref = rest[:N_BRANCH * nh], rest[N_BRANCH * nh], rest[N_BRANCH * nh + 1]
        brs = (oa_ref[...], ob_ref[...], oc_ref[...])
        for j in range(nh):
            cols = slice(j * GATE_BLK, (j + 1) * GATE_BLK)
            acc = None
            for n in range(N_BRANCH):
                b = n * nh + j
                t = _dot(brs[n], wb_ref[n, :, cols])
                g = _sigmoid(gate_refs[b][...] + bg_ref[:, b * GATE_BLK:(b + 1) * GATE_BLK])
                acc = t * g if acc is None else acc + t * g
            o_ref[:, cols] = acc.astype(BF16)

    row = lambda i: (i, 0)
    br = pl.BlockSpec((tm, BRANCH_WIDTH), row)
    return pl.pallas_call(
        body, name=name,
        out_shape=jax.ShapeDtypeStruct((S, D), BF16),
        grid=(S // tm,),
        in_specs=[br, br, br, pl.BlockSpec((N_BRANCH, BRANCH_WIDTH, D), lambda i: (0, 0, 0))]
                 + _gate_specs(tm, D) + [pl.BlockSpec((1, N_BRANCH * D), lambda i: (0, 0))],
        out_specs=pl.BlockSpec((tm, D), row),
        compiler_params=_cp(("parallel",)),
    )(oa, ob, oc, wb, *([proj] * (N_BRANCH * nh)), bg)


def _merge_bwd(oa, ob, oc, wb, proj, bg, dmerged, *, name):
    S = oa.shape[0]
    D = wb.shape[2]
    nh = D // GATE_BLK
    tm = _tile(S, ROW_TILE)

    def body(oa_ref, ob_ref, oc_ref, wb_ref, *rest):
        gate_refs = rest[:N_BRANCH * nh]
        bg_ref, dm_ref, dgl_ref, dbg_ref = rest[N_BRANCH * nh:N_BRANCH * nh + 4]
        dt_refs = rest[N_BRANCH * nh + 4:N_BRANCH * nh + 4 + N_BRANCH]
        dbr_refs = rest[N_BRANCH * nh + 4 + N_BRANCH:]

        @pl.when(pl.program_id(0) == 0)
        def _():
            dbg_ref[...] = jnp.zeros_like(dbg_ref)

        brs = (oa_ref[...], ob_ref[...], oc_ref[...])
        for n in range(N_BRANCH):
            dbr = None
            for j in range(nh):
                cols = slice(j * GATE_BLK, (j + 1) * GATE_BLK)
                b = n * nh + j
                gcols = slice(b * GATE_BLK, (b + 1) * GATE_BLK)
                w = wb_ref[n, :, cols]
                t = _dot(brs[n], w)
                g = _sigmoid(gate_refs[b][...] + bg_ref[:, gcols])
                dm = dm_ref[:, cols]
                dt = (dm * g).astype(BF16)
                dgl = dm * t * g * (1.0 - g)
                dt_refs[n][:, cols] = dt
                dgl_ref[:, gcols] = dgl.astype(BF16)
                dbg_ref[:, gcols] += jnp.sum(dgl, axis=0, keepdims=True)
                d = _dot(dt, w, _NT)
                dbr = d if dbr is None else dbr + d
            dbr_refs[n][...] = dbr.astype(dbr_refs[n].dtype)

    row = lambda i: (i, 0)
    br = pl.BlockSpec((tm, BRANCH_WIDTH), row)
    res = pl.pallas_call(
        body, name=name,
        out_shape=(jax.ShapeDtypeStruct((S, N_BRANCH * D), BF16), jax.ShapeDtypeStruct((1, N_BRANCH * D), F32),
                   *([jax.ShapeDtypeStruct((S, D), BF16)] * N_BRANCH),
                   *[jax.ShapeDtypeStruct((S, BRANCH_WIDTH), dt) for dt in (BF16, BF16, F32)]),
        grid=(S // tm,),
        in_specs=[br, br, br, pl.BlockSpec((N_BRANCH, BRANCH_WIDTH, D), lambda i: (0, 0, 0))]
                 + _gate_specs(tm, D)
                 + [pl.BlockSpec((1, N_BRANCH * D), lambda i: (0, 0)), pl.BlockSpec((tm, D), row)],
        out_specs=(pl.BlockSpec((tm, N_BRANCH * D), row), pl.BlockSpec((1, N_BRANCH * D), lambda i: (0, 0)),
                   *([pl.BlockSpec((tm, D), row)] * N_BRANCH), *([br] * N_BRANCH)),
        compiler_params=_cp(("arbitrary",)),
    )(oa, ob, oc, wb, *([proj] * (N_BRANCH * nh)), bg, dmerged)
    return res[0], res[1], list(res[2:2 + N_BRANCH]), list(res[2 + N_BRANCH:])


X_SCALE = 1.0 / math.sqrt(X_HEAD_DIM)
X_W = X_HEADS * X_HEAD_DIM


def _xattn_fwd(q, kv, *, name):
    S = q.shape[0]
    M = kv.shape[0]
    tq = _tile(S, 512)

    def body(q_ref, kv_ref, o_ref, lse_ref):
        for h in range(X_HEADS):
            cols = slice(h * LANES, (h + 1) * LANES)
            s = _dot(q_ref[:, cols], kv_ref[:, cols], _NT) * X_SCALE
            m = jnp.max(s, axis=-1, keepdims=True)
            e = jnp.exp(s - m)
            l = jnp.sum(e, axis=-1, keepdims=True)
            p = (e * (1.0 / l)).astype(BF16)
            o_ref[:, cols] = _dot(p, kv_ref[:, X_W + h * LANES:X_W + (h + 1) * LANES]).astype(BF16)
            lse_ref[h] = m + jnp.log(l)

    return pl.pallas_call(
        body, name=name,
        out_shape=(jax.ShapeDtypeStruct((S, X_W), BF16), jax.ShapeDtypeStruct((X_HEADS, S, 1), F32)),
        grid=(S // tq,),
        in_specs=[pl.BlockSpec((tq, X_W), lambda i: (i, 0)), pl.BlockSpec((M, 2 * X_W), lambda i: (0, 0))],
        out_specs=(pl.BlockSpec((tq, X_W), lambda i: (i, 0)), pl.BlockSpec((X_HEADS, tq, 1), lambda i: (0, i, 0))),
        compiler_params=_cp(("parallel",)),
    )(q, kv)


def _xattn_bwd(q, kv, lse, do, *, name):
    S = q.shape[0]
    M = kv.shape[0]
    tq = _tile(S, 512)

    def body(q_ref, kv_ref, lse_ref, do_ref, dq_ref, dkv_ref):
        @pl.when(pl.program_id(0) == 0)
        def _():
            dkv_ref[...] = jnp.zeros_like(dkv_ref)

        for h in range(X_HEADS):
            cols = slice(h * LANES, (h + 1) * LANES)
            vcols = slice(X_W + h * LANES, X_W + (h + 1) * LANES)
            qh, kh, vh = q_ref[:, cols], kv_ref[:, cols], kv_ref[:, vcols]
            doh = do_ref[:, cols].astype(BF16)
            p = jnp.exp(_dot(qh, kh, _NT) * X_SCALE - lse_ref[h])
            dp = _dot(doh, vh, _NT)
            delta = jnp.sum(p * dp, axis=-1, keepdims=True)
            ds = (p * (dp - delta) * X_SCALE).astype(BF16)
            dq_ref[:, cols] = _dot(ds, kh).astype(BF16)
            dkv_ref[:, cols] += _dot(ds, qh, _TN)
            dkv_ref[:, vcols] += _dot(p.astype(BF16), doh, _TN)

    q_spec = pl.BlockSpec((tq, X_W), lambda i: (i, 0))
    return pl.pallas_call(
        body, name=name,
        out_shape=(jax.ShapeDtypeStruct((S, X_W), BF16), jax.ShapeDtypeStruct((M, 2 * X_W), F32)),
        grid=(S // tq,),
        in_specs=[q_spec, pl.BlockSpec((M, 2 * X_W), lambda i: (0, 0)),
                  pl.BlockSpec((X_HEADS, tq, 1), lambda i: (0, i, 0)), q_spec],
        out_specs=(q_spec, pl.BlockSpec((M, 2 * X_W), lambda i: (0, 0))),
        compiler_params=_cp(("arbitrary",)),
    )(q, kv, lse, do)


def _shift_down(h, row):
    return jnp.where(row == 0, 0.0, pltpu.roll(h, 1, 0))


def _shift_up(h, row, S):
    return jnp.where(row == S - 1, 0.0, pltpu.roll(h, S - 1, 0))


def _conv3(h, ck, cb, row, S):
    return _shift_down(h, row) * ck[0:1] + h * ck[1:2] + _shift_up(h, row, S) * ck[2:3] + cb


def _conv_act_fwd(h, ck, cb, *, name):
    S, F2 = h.shape
    F = F2 // 2
    nt = F // LANES

    def body(ha_ref, hb_ref, cka_ref, ckb_ref, cba_ref, cbb_ref, o_ref):
        row = lax.broadcasted_iota(jnp.int32, (S, LANES), 0)
        a = _conv3(ha_ref[...], cka_ref[...], cba_ref[...], row, S)
        b = _conv3(hb_ref[...], ckb_ref[...], cbb_ref[...], row, S)
        o_ref[...] = (_gelu(a) * b).astype(BF16)

    ca = lambda j: (0, j)
    cbi = lambda j: (0, j + nt)
    return pl.pallas_call(
        body, name=name,
        out_shape=jax.ShapeDtypeStruct((S, F), BF16),
        grid=(nt,),
        in_specs=[pl.BlockSpec((S, LANES), ca), pl.BlockSpec((S, LANES), cbi), pl.BlockSpec((3, LANES), ca),
                  pl.BlockSpec((3, LANES), cbi), pl.BlockSpec((1, LANES), ca), pl.BlockSpec((1, LANES), cbi)],
        out_specs=pl.BlockSpec((S, LANES), ca),
        compiler_params=_cp(("parallel",)),
    )(h, h, ck, ck, cb, cb)


def _conv_act_bwd(h, ck, cb, dact, *, name):
    S, F2 = h.shape
    F = F2 // 2
    nt = F // LANES

    def body(ha_ref, hb_ref, cka_ref, ckb_ref, cba_ref, cbb_ref, d_ref,
             dha_ref, dhb_ref, dcka_ref, dckb_ref, dcba_ref, dcbb_ref):
        row = lax.broadcasted_iota(jnp.int32, (S, LANES), 0)
        ha, hb = ha_ref[...], hb_ref[...]
        cka, ckb = cka_ref[...], ckb_ref[...]
        a = _conv3(ha, cka, cba_ref[...], row, S)
        b = _conv3(hb, ckb, cbb_ref[...], row, S)
        d = d_ref[...]
        ga, ga_grad = _gelu_and_grad(a)
        da = d * b * ga_grad
        db = d * ga
        for dd, hh, ck_, dh_ref, dck_ref, dcb_ref in ((da, ha, cka, dha_ref, dcka_ref, dcba_ref),
                                                      (db, hb, ckb, dhb_ref, dckb_ref, dcbb_ref)):
            dcb_ref[...] = jnp.sum(dd, axis=0, keepdims=True)
            dck_ref[0:1, :] = jnp.sum(dd * _shift_down(hh, row), axis=0, keepdims=True)
            dck_ref[1:2, :] = jnp.sum(dd * hh, axis=0, keepdims=True)
            dck_ref[2:3, :] = jnp.sum(dd * _shift_up(hh, row, S), axis=0, keepdims=True)
            dh = _shift_up(dd, row, S) * ck_[0:1] + dd * ck_[1:2] + _shift_down(dd, row) * ck_[2:3]
            dh_ref[...] = dh.astype(BF16)

    ca = lambda j: (0, j)
    cbi = lambda j: (0, j + nt)
    col = pl.BlockSpec((S, LANES), ca)
    return pl.pallas_call(
        body, name=name,
        out_shape=(jax.ShapeDtypeStruct((S, F), BF16), jax.ShapeDtypeStruct((S, F), BF16),
                   jax.ShapeDtypeStruct((3, F), F32), jax.ShapeDtypeStruct((3, F), F32),
                   jax.ShapeDtypeStruct((1, F), F32), jax.ShapeDtypeStruct((1, F), F32)),
        grid=(nt,),
        in_specs=[col, pl.BlockSpec((S, LANES), cbi), pl.BlockSpec((3, LANES), ca), pl.BlockSpec((3, LANES), cbi),
                  pl.BlockSpec((1, LANES), ca), pl.BlockSpec((1, LANES), cbi), col],
        out_specs=(col, col, pl.BlockSpec((3, LANES), ca), pl.BlockSpec((3, LANES), ca),
                   pl.BlockSpec((1, LANES), ca), pl.BlockSpec((1, LANES), ca)),
        compiler_params=_cp(("parallel",)),
    )(h, h, ck, ck, cb, cb, dact)


def _layer_fwd(x, xb, memb, w, shards, tabs, seg, l):
    n = lambda s: f"L{l}_{s}"
    w = dict(w)
    qg2 = jnp.tile(w["b_q_gain"], 2)[None, :]
    kg2 = jnp.tile(w["b_k_gain"], 2)[None, :]
    proj = _mm(xb, w["w_in"], tb=True, name=n("proj"))
    aq, ak4, av4, bq, bk4, bv4 = _prep(proj, tabs, qg2, kg2, seg, name=n("prep"))
    oa, lse_a = _attn_win_fwd(aq, ak4, av4, w["a_sink"], name=n("attn_win"))
    gather = [(shards[k], l) for k in GATHERED_LATE] + ([(shards["w_in"], l + 1)] if l + 1 < DEPTH else [])
    ob, lse_b, gathered = _attn_dense_fwd(bq, bk4, bv4, gather=gather, name=n("attn_dense"))
    for k, g in zip(GATHERED_LATE, gathered):
        w[k] = _unshard(g, GATHER_AXIS[k])
    w_in_next = gathered[len(GATHERED_LATE)] if l + 1 < DEPTH else None
    oc = _gmlp_fwd(proj, w["c_ws"], w["c_bs3"], w["c_ln_g"], w["c_ln_b"], name=n("gmlp"))
    merged = _merge_fwd(oa, ob, oc, w["w_branch"], proj, w["b_gate"], name=n("merge"))
    x1, x1b, xh1, rs1 = _mm_res_ln(merged, w["w_mix_out"], x, w["ln1_g"], w["ln1_b"], name=n("mix_ln1"))
    xq = _mm(x1b, w["x_wq"], out_dtype=BF16, name=n("xq"))
    xkv = _mm(memb, w["x_wkv"], out_dtype=BF16, name=n("xkv"))
    xo, lse_x = _xattn_fwd(xq, xkv, name=n("xattn"))
    x2, x2b, xh2, rs2 = _mm_res_ln(xo, w["x_wo"], x1, w["ln2_g"], w["ln2_b"], name=n("xo_ln2"))
    h = _mm(x2b, w["f_w_up"], tb=True, name=n("ffn_up"))
    act = _conv_act_fwd(h, w["f_conv_k"], w["f_conv_b"], name=n("conv_act"))
    x3, x3b, xh3, rs3 = _mm_res_ln(act, w["f_w_down"], x2, w["ln3_g"], w["ln3_b"], name=n("down_ln3"))
    saved = dict(xb=xb, proj=proj, aq=aq, ak4=ak4, av4=av4, bq=bq, bk4=bk4, bv4=bv4, lse_a=lse_a, lse_b=lse_b,
                 oa=oa, ob=ob, oc=oc, merged=merged, xh1=xh1, rs1=rs1, x1b=x1b, xq=xq, xkv=xkv, xo=xo, lse_x=lse_x,
                 xh2=xh2, rs2=rs2, x2b=x2b, h=h, act=act, xh3=xh3, rs3=rs3, qg2=qg2, kg2=kg2)
    return x3, x3b, saved, w, w_in_next


def _layer_bwd(top, memb, w, sv, tabs, seg, l, ln_below, dw_in_above, recv):
    n = lambda s: f"L{l}_{s}"
    g = {}
    big = {}
    recv = dict(recv)

    def dw(key, a, b, tag):
        big[key] = _mm(a, b, ta=True, out_dtype=BF16, name=n(tag))

    def dw_t(key, segments, x, tag):
        buf = jax.ShapeDtypeStruct((sum(s.shape[1] for s in segments), x.shape[1]), BF16)
        row = 0
        for i, s in enumerate(segments):
            buf = _mm(s, x, ta=True, into=(buf, row), name=n(f"{tag}{i}"))
            row += s.shape[1]
        big[key] = buf

    dz3, dz3b, g["ln3_g"], g["ln3_b"] = top
    dw("f_w_down", sv["act"], dz3b, "dw_down")
    dact = _mm(dz3b, w["f_w_down"], tb=True, name=n("dact"))
    dha, dhb, dcka, dckb, dcba, dcbb = _conv_act_bwd(sv["h"], w["f_conv_k"], w["f_conv_b"], dact, name=n("conv_act_bwd"))
    g["f_conv_k"] = jnp.concatenate([dcka, dckb], axis=1)
    g["f_conv_b"] = jnp.concatenate([dcba, dcbb], axis=1)[0]
    dw_t("f_w_up", [dha, dhb], sv["x2b"], "dw_up")
    dz2, dz2b, g["ln2_g"], g["ln2_b"] = _mm([dha, dhb], w["f_w_up"], res=dz3, res_scale=ALPHA,
                                            ln_bwd=(sv["xh2"], sv["rs2"], w["ln2_g"]), name=n("dx2_ln2"))
    dw("x_wo", sv["xo"], dz2b, "dw_xo")
    dxo = _mm(dz2b, w["x_wo"], tb=True, out_dtype=BF16, name=n("dxo"))
    dxq, dxkv = _xattn_bwd(sv["xq"], sv["xkv"], sv["lse_x"], dxo, name=n("xattn_bwd"))
    dw("x_wq", sv["x1b"], dxq, "dw_xq")
    dw("x_wkv", memb, dxkv, "dw_xkv")
    dz1, dz1b, g["ln1_g"], g["ln1_b"] = _mm(dxq, w["x_wq"], tb=True, res=dz2, res_scale=ALPHA,
                                            ln_bwd=(sv["xh1"], sv["rs1"], w["ln1_g"]), name=n("dx1_ln1"))
    dw("w_mix_out", sv["merged"], dz1b, "dw_mix")
    dmerged = _mm(dz1b, w["w_mix_out"], tb=True, name=n("dmerged"))
    dgl, dbg, dt, dbr = _merge_bwd(sv["oa"], sv["ob"], sv["oc"], w["w_branch"], sv["proj"], w["b_gate"], dmerged,
                                   name=n("merge_bwd"))
    g["b_gate"] = dbg[0]
    for i, k in enumerate(("oa", "ob", "oc")):
        dw(f"w_branch{i}", sv[k], dt[i], f"dw_branch{i}")
    big["w_branch"] = jnp.stack([big.pop(f"w_branch{i}") for i in range(N_BRANCH)])
    dqa, dka, dva, dsink = _attn_win_bwd(sv["aq"], sv["ak4"], sv["av4"], w["a_sink"], sv["lse_a"], dbr[0],
                                         name=n("attn_win_bwd"))
    g["a_sink"] = dsink[:, 0]
    sent = [k for k in BIG if k != "w_in"]
    scatter = [(_reshard(big[k], BIG_AXIS[k]), recv[k], l) for k in sent]
    if dw_in_above is not None:
        sent.append("w_in")
        scatter.append((_reshard(dw_in_above, BIG_AXIS["w_in"]), recv["w_in"], l + 1))
    dqb, dkb, dvb, got = _attn_dense_bwd(sv["bq"], sv["bk4"], sv["bv4"], sv["lse_b"], dbr[1], scatter=scatter,
                                         name=n("attn_dense_bwd"))
    recv.update(zip(sent, got))
    dcz, g["c_ws"], dbs3, dlg, dlb = _gmlp_bwd(sv["proj"], dbr[2], w["c_ws"], w["c_bs3"], w["c_ln_g"], w["c_ln_b"],
                                               name=n("gmlp_bwd"))
    g["c_bs"] = dbs3[:, :, 0]
    g["c_ln_g"], g["c_ln_b"] = dlg[0], dlb[0]
    dqkv, dqg, dkg = _unprep(dqa, dka, dva, dqb, dkb, dvb, sv["proj"], tabs, sv["qg2"], sv["kg2"], seg, name=n("unprep"))
    g["b_q_gain"] = dqg[0, :HEAD_DIM] + dqg[0, HEAD_DIM:]
    g["b_k_gain"] = dkg[0, :HEAD_DIM] + dkg[0, HEAD_DIM:]
    dw_t("w_in", [dqkv, dcz, dgl], sv["xb"], "dw_in")
    if ln_below is None:
        dx0, (recv["w_in"],) = _mm([dqkv, dcz, dgl], w["w_in"], res=dz1, res_scale=ALPHA, name=n("dx0"),
                                   scatter=[(_reshard(big.pop("w_in"), BIG_AXIS["w_in"]), recv["w_in"], l)])
    else:
        dx0 = _mm([dqkv, dcz, dgl], w["w_in"], res=dz1, res_scale=ALPHA, name=n("dx0"))
        dx0 = _ln_bwd(dx0, *ln_below, name=n("ln_bwd_below"))
    for k in ("ln1_g", "ln1_b", "ln2_g", "ln2_b", "ln3_g", "ln3_b"):
        g[k] = g[k][0]
    return dx0, g, big.get("w_in"), recv


WEIGHTS = ("w_in", "b_gate", "a_sink", "b_q_gain", "b_k_gain", "c_ln_g", "c_ln_b", "c_ws", "c_bs", "w_branch",
           "w_mix_out", "ln1_g", "ln1_b", "x_wq", "x_wkv", "x_wo", "ln2_g", "ln2_b", "f_w_up", "f_conv_k",
           "f_conv_b", "f_w_down", "ln3_g", "ln3_b")
TRANSPOSED = ("w_in", "f_w_up")
BIG_AXIS = {"w_in": 0, "w_branch": 2, "w_mix_out": 0, "x_wq": 0, "x_wkv": 0, "x_wo": 1, "f_w_up": 0, "f_w_down": 0}
BIG = tuple(BIG_AXIS)
GATHERED = BIG + ("f_conv_k",)
GATHERED_LATE = tuple(k for k in GATHERED if k != "w_in")
GATHER_AXIS = dict(BIG_AXIS, f_conv_k=1)
SMALL = tuple(k for k in WEIGHTS if k not in GATHERED)


def _unshard(g, axis):
    t = jnp.moveaxis(g, 0, axis)
    return t.reshape(t.shape[:axis] + (t.shape[axis] * t.shape[axis + 1],) + t.shape[axis + 2:])


def _reshard(full, axis):
    t = full.reshape(full.shape[:axis] + (N_DEV, full.shape[axis] // N_DEV) + full.shape[axis + 1:])
    return jnp.moveaxis(t, axis, 0)


def _small_weights(small, l):
    w = {k: v[l] for k, v in small.items()}
    for k in ("c_ln_g", "c_ln_b", "ln1_g", "ln1_b", "ln2_g", "ln2_b", "ln3_g", "ln3_b", "b_gate", "f_conv_b"):
        w[k] = w[k][None, :]
    w["c_bs3"] = w["c_bs"][:, :, None]
    w["c_ws"] = w["c_ws"].astype(BF16)
    return w


def _local_step(x, mem, target, small, shards):
    S = x.shape[0]
    tabs = _rope_tables(S)
    seg = _seg_matrix()
    memb = mem.astype(BF16)
    xb = x.astype(BF16)
    saved, weights = [], []
    w_in_g = _gather_call([(shards["w_in"], 0)], name="gather_w_in_L0")[0]
    for l in range(DEPTH):
        w = dict(_small_weights(small, l), w_in=_unshard(w_in_g, GATHER_AXIS["w_in"]))
        x, xb, sv, w, w_in_g = _layer_fwd(x, xb, memb, w, shards, tabs, seg, l)
        saved.append(sv)
        weights.append(w)
    dy, loss = _loss_head(x, target, name="loss_head")
    grads = [None] * DEPTH
    recv = {k: jax.ShapeDtypeStruct((DEPTH, N_DEV) + shards[k].shape[1:], BF16) for k in BIG}
    dw_in = None
    last_ln = lambda l: (saved[l]["xh3"], saved[l]["rs3"], weights[l]["ln3_g"])
    top = _ln_bwd(dy, *last_ln(DEPTH - 1), name="ln_bwd_top")
    for l in reversed(range(DEPTH)):
        top, grads[l], dw_in, recv = _layer_bwd(top, memb, weights[l], saved[l], tabs, seg, l,
                                                last_ln(l - 1) if l > 0 else None, dw_in, recv)
    return loss, top, grads, [recv[k] for k in BIG]


PACK_W = 1024


def _gather_call(gather, *, name):
    na = len(gather)

    def body(*refs):
        start, finish = _gather_plan([(refs[a], gather[a][1], refs[na + a]) for a in range(na)], *refs[2 * na:])
        start()
        finish()

    return list(pl.pallas_call(
        body, name=name,
        out_shape=[_gathered_shape(x) for x, _ in gather],
        in_specs=[_ANY] * na, out_specs=[_ANY] * na,
        scratch_shapes=_comm_scratch(na),
    )(*[x for x, _ in gather]))


def _scatter_io(scatter):
    held = [a for a, (_, r, _) in enumerate(scatter) if not isinstance(r, jax.ShapeDtypeStruct)]
    return ([s for s, _, _ in scatter] + [scatter[a][1] for a in held],
            [jax.ShapeDtypeStruct(r.shape, r.dtype) for _, r, _ in scatter], held)


def _sum_parts(parts, *, name):
    P, R, C = parts.shape
    tr = _tile(R, 64, align=8)

    def body(p_ref, o_ref):
        g = p_ref[0].astype(F32)
        for s in range(1, P):
            g = g + p_ref[s].astype(F32)
        o_ref[...] = g

    return pl.pallas_call(
        body, name=name, out_shape=jax.ShapeDtypeStruct((R, C), F32), grid=(R // tr,),
        in_specs=[pl.BlockSpec((P, tr, C), lambda i: (0, i, 0))], out_specs=pl.BlockSpec((tr, C), lambda i: (i, 0)),
        compiler_params=_cp(("parallel",)),
    )(parts)


ADAM_BLOCK_ELEMS = 512 * 1024


def _adamw(parts, w, m, v, *, name):
    L, P, R, C = parts.shape
    assert w.shape == (L, R, C), (parts.shape, w.shape)
    tr = _tile(R, max(16, ADAM_BLOCK_ELEMS // C), align=16)

    def body(p_ref, w_ref, m_ref, v_ref, g_ref, d_ref, nm_ref, nv_ref):
        g = p_ref[0].astype(F32)
        for s in range(1, P):
            g = g + p_ref[s].astype(F32)
        g_ref[...] = g
        d_ref[...], nm_ref[...], nv_ref[...] = _adam_update(g, w_ref[...], m_ref[...], v_ref[...])

    blk = pl.BlockSpec((None, tr, C), lambda l, i: (l, i, 0))
    shp = jax.ShapeDtypeStruct((L, R, C), F32)
    return pl.pallas_call(
        body, name=name, out_shape=(shp, shp, shp, shp), grid=(L, R // tr),
        in_specs=[pl.BlockSpec((None, P, tr, C), lambda l, i: (l, 0, i, 0)), blk, blk, blk],
        out_specs=(blk, blk, blk, blk),
        compiler_params=_cp(("parallel", "parallel")),
    )(parts, w, m, v)


def _adam_update(g, w, m, v):
    nm = ADAM_B1 * m + (1.0 - ADAM_B1) * g
    nv = ADAM_B2 * v + (1.0 - ADAM_B2) * (g * g)
    m_hat = nm / (1.0 - ADAM_B1 ** ADAM_STEP)
    v_hat = nv / (1.0 - ADAM_B2 ** ADAM_STEP)
    return -ADAM_LR * (m_hat / (jnp.sqrt(v_hat) + ADAM_EPS) + ADAM_WD * w), nm, nv


def _adamw_small(gs, ws, ms, vs, *, name):
    n = len(gs)

    def body(*refs):
        for a in range(n):
            g, w, m, v = (refs[k * n + a][...] for k in range(4))
            d, nm, nv = _adam_update(g, w, m, v)
            refs[4 * n + a][...] = d
            refs[5 * n + a][...] = nm
            refs[6 * n + a][...] = nv

    shapes = [jax.ShapeDtypeStruct(w.shape, F32) for w in ws]
    res = pl.pallas_call(body, name=name, out_shape=shapes * 3, compiler_params=_cp())(*gs, *ws, *ms, *vs)
    return res[:n], res[n:2 * n], res[2 * n:]


def _pad_rows(vec, width, row_align):
    n = vec.shape[0]
    rows = -(-n // width)
    rows = -(-rows // row_align) * row_align
    return jnp.pad(vec, (0, rows * width - n)).reshape(rows, width)


def kernel(x, mem, w_in, b_gate, a_sink, b_q_gain, b_k_gain, c_ln_g, c_ln_b, c_ws, c_bs, w_branch, w_mix_out, ln1_g, ln1_b, x_wq, x_wkv, x_wo, ln2_g, ln2_b, f_w_up, f_conv_k, f_conv_b, f_w_down, ln3_g, ln3_b, loss_target, m_w_in, m_b_gate, m_a_sink, m_b_q_gain, m_b_k_gain, m_c_ln_g, m_c_ln_b, m_c_ws, m_c_bs, m_w_branch, m_w_mix_out, m_ln1_g, m_ln1_b, m_x_wq, m_x_wkv, m_x_wo, m_ln2_g, m_ln2_b, m_f_w_up, m_f_conv_k, m_f_conv_b, m_f_w_down, m_ln3_g, m_ln3_b, v_w_in, v_b_gate, v_a_sink, v_b_q_gain, v_b_k_gain, v_c_ln_g, v_c_ln_b, v_c_ws, v_c_bs, v_w_branch, v_w_mix_out, v_ln1_g, v_ln1_b, v_x_wq, v_x_wkv, v_x_wo, v_ln2_g, v_ln2_b, v_f_w_up, v_f_conv_k, v_f_conv_b, v_f_w_down, v_ln3_g, v_ln3_b):
    w = dict(w_in=w_in, b_gate=b_gate, a_sink=a_sink, b_q_gain=b_q_gain, b_k_gain=b_k_gain, c_ln_g=c_ln_g,
             c_ln_b=c_ln_b, c_ws=c_ws, c_bs=c_bs, w_branch=w_branch, w_mix_out=w_mix_out, ln1_g=ln1_g, ln1_b=ln1_b,
             x_wq=x_wq, x_wkv=x_wkv, x_wo=x_wo, ln2_g=ln2_g, ln2_b=ln2_b, f_w_up=f_w_up, f_conv_k=f_conv_k,
             f_conv_b=f_conv_b, f_w_down=f_w_down, ln3_g=ln3_g, ln3_b=ln3_b)
    m = dict(w_in=m_w_in, b_gate=m_b_gate, a_sink=m_a_sink, b_q_gain=m_b_q_gain, b_k_gain=m_b_k_gain,
             c_ln_g=m_c_ln_g, c_ln_b=m_c_ln_b, c_ws=m_c_ws, c_bs=m_c_bs, w_branch=m_w_branch, w_mix_out=m_w_mix_out,
             ln1_g=m_ln1_g, ln1_b=m_ln1_b, x_wq=m_x_wq, x_wkv=m_x_wkv, x_wo=m_x_wo, ln2_g=m_ln2_g, ln2_b=m_ln2_b,
             f_w_up=m_f_w_up, f_conv_k=m_f_conv_k, f_conv_b=m_f_conv_b, f_w_down=m_f_w_down, ln3_g=m_ln3_g,
             ln3_b=m_ln3_b)
    v = dict(w_in=v_w_in, b_gate=v_b_gate, a_sink=v_a_sink, b_q_gain=v_b_q_gain, b_k_gain=v_b_k_gain,
             c_ln_g=v_c_ln_g, c_ln_b=v_c_ln_b, c_ws=v_c_ws, c_bs=v_c_bs, w_branch=v_w_branch, w_mix_out=v_w_mix_out,
             ln1_g=v_ln1_g, ln1_b=v_ln1_b, x_wq=v_x_wq, x_wkv=v_x_wkv, x_wo=v_x_wo, ln2_g=v_ln2_g, ln2_b=v_ln2_b,
             f_w_up=v_f_w_up, f_conv_k=v_f_conv_k, f_conv_b=v_f_conv_b, f_w_down=v_f_w_down, ln3_g=v_ln3_g,
             ln3_b=v_ln3_b)
    me = 4 * lax.axis_index("x") + 2 * lax.axis_index("y") + lax.axis_index("c")

    def held(k, t):
        return jnp.swapaxes(t, 1, 2) if k in TRANSPOSED else t

    shards = dict({k: held(k, w[k]).astype(BF16) for k in BIG}, f_conv_k=w["f_conv_k"])
    loss, grad_x, grads, recvs = _local_step(x[0], mem[0], loss_target[0], {k: w[k] for k in SMALL}, shards)
    loss = lax.psum(loss[0, 0], ("x", "y", "c"))

    out_g, out_d, out_m, out_v = {}, {}, {}, {}
    for k, recv in zip(BIG, recvs):
        shp = held(k, w[k]).shape
        rc = (DEPTH, math.prod(shp[1:-1]), shp[-1])
        parts = recv.reshape((DEPTH, N_DEV) + rc[1:])
        g_, d_, m_, v_ = _adamw(parts, held(k, w[k]).reshape(rc), held(k, m[k]).reshape(rc),
                                held(k, v[k]).reshape(rc), name=f"adamw_{k}")
        out_g[k], out_d[k], out_m[k], out_v[k] = (held(k, t.reshape(shp)) for t in (g_, d_, m_, v_))

    small_all = SMALL + ("f_conv_k",)
    gfull = {k: jnp.stack([grads[l][k] for l in range(DEPTH)]) for k in small_all}

    def pack(d):
        return jnp.concatenate([_pad_rows(d[k].reshape(-1), PACK_W, 8) for k in small_all])

    def unpack(rows, like):
        out, r = {}, 0
        for k in small_all:
            nr = -(-like[k].size // (8 * PACK_W)) * 8
            out[k] = rows[r:r + nr].reshape(-1)[:like[k].size].reshape(like[k].shape)
            r += nr
        return out

    gathered = _gather_call([(pack(gfull)[None], 0)], name="gather_small_grads")[0]
    sg = unpack(_sum_parts(gathered, name="sum_small_grads"), gfull)
    width = w["f_conv_k"].shape[2]
    sg["f_conv_k"] = lax.dynamic_slice_in_dim(sg["f_conv_k"], me * width, width, axis=2)
    ud, um, uv = _adamw_small(*[[d[k] for k in small_all] for d in (sg, w, m, v)], name="adamw_small")
    for i, k in enumerate(small_all):
        out_g[k], out_d[k], out_m[k], out_v[k] = sg[k], ud[i], um[i], uv[i]

    return (loss, grad_x[None], *[out_g[k] for k in WEIGHTS], *[out_d[k] for k in WEIGHTS],
            *[out_m[k] for k in WEIGHTS], *[out_v[k] for k in WEIGHTS])
```

```python
import functools
import math

import jax
import jax.numpy as jnp
from jax import lax
from jax.experimental import pallas as pl
from jax.experimental.pallas import tpu as pltpu

F32 = jnp.float32
BF16 = jnp.bfloat16

DEPTH = 4
HEAD_DIM = 64
BLOCK = 128
WINDOW = 128
GRID_W = 64
C_WIDTH = 512
C_GROUPS = 4
CHUNK = 128
N_BRANCH = 3
BRANCH_WIDTH = 512
ROPE_THETA = 10000.0
X_HEADS = 4
X_HEAD_DIM = 128
ALPHA = (2 * DEPTH) ** 0.25
LN_EPS = 1e-5
RMS_EPS = 1e-6
ADAM_LR = 0.001
ADAM_B1 = 0.9
ADAM_B2 = 0.999
ADAM_EPS = 1e-08
ADAM_WD = 0.01
ADAM_STEP = 10
N_DEV = 8

COL_A = 0
COL_B = 768
COL_C = 1536
COL_GATE = 2560
QKV_W = 768

LANES = 128
V7X_VMEM_BYTES = 64 * 1024 * 1024
VMEM_LIMIT = V7X_VMEM_BYTES - 8 * 1024 * 1024
NEG_BIG = -1e30
ROW_TILE = 512

_NT = (((1,), (1,)), ((), ()))
_TN = (((0,), (0,)), ((), ()))
_NN = (((1,), (0,)), ((), ()))


def _cp(sem=None):
    return pltpu.CompilerParams(dimension_semantics=sem, vmem_limit_bytes=VMEM_LIMIT)


def _tile(n, target, align=LANES):
    if n <= target:
        return n
    best = None
    for t in range(align, target + 1, align):
        if n % t == 0:
            best = t
    assert best is not None, (n, target)
    return best


def _dot(a, b, dims=_NN):
    return lax.dot_general(a, b, dims, preferred_element_type=F32)


def _gelu(x):
    return 0.5 * x * (1.0 + lax.erf(x * 0.7071067811865476))


def _gelu_and_grad(x):
    cdf = 0.5 * (1.0 + lax.erf(x * 0.7071067811865476))
    return x * cdf, cdf + x * jnp.exp(-0.5 * x * x) * 0.3989422804014327


def _sigmoid(x):
    return 1.0 / (1.0 + jnp.exp(-x))


MESH_ID = pl.DeviceIdType.MESH
_ANY = pl.BlockSpec(memory_space=pl.ANY)
COPIES_PER_ARRAY = N_DEV - 1


def _comm_scratch(n_arrays):
    return [pltpu.SemaphoreType.DMA((COPIES_PER_ARRAY * n_arrays,)),
            pltpu.SemaphoreType.DMA((COPIES_PER_ARRAY * n_arrays,)), pltpu.SemaphoreType.DMA((n_arrays,))]


def _gathered_shape(x):
    return jax.ShapeDtypeStruct((N_DEV,) + x.shape[1:], x.dtype)


def _gather_plan(entries, send_sems, recv_sems, local_sems):
    mx, my, mc = lax.axis_index("x"), lax.axis_index("y"), lax.axis_index("c")
    me, sibling = (mx, my, mc), (mx, my, 1 - mc)
    chips = [(1 - mx, my), (mx, 1 - my), (1 - mx, 1 - my)]

    def copy(a, k, block, to, from_shard=False):
        x_ref, l, out_ref = entries[a]
        dst = out_ref.at[4 * block[0] + 2 * block[1] + block[2]]
        return pltpu.make_async_remote_copy(
            src_ref=x_ref.at[l] if from_shard else dst, dst_ref=dst,
            send_sem=send_sems.at[COPIES_PER_ARRAY * a + k], recv_sem=recv_sems.at[COPIES_PER_ARRAY * a + k],
            device_id=to, device_id_type=MESH_ID)

    def own(a):
        x_ref, l, out_ref = entries[a]
        return pltpu.make_async_copy(x_ref.at[l], out_ref.at[4 * mx + 2 * my + mc], local_sems.at[a])

    def first(a):
        return [copy(a, 0, me, sibling, True)] + [copy(a, 1 + j, me, (*chip, mc), True) for j, chip in enumerate(chips)]

    def passed(a):
        return [copy(a, 4 + j, (*chip, mc), sibling) for j, chip in enumerate(chips)]

    def start():
        for a in range(len(entries)):
            own(a).start()
            for cp in first(a):
                cp.start()

    def finish():
        for a in range(len(entries)):
            fwd = passed(a)
            for j, chip in enumerate(chips):
                copy(a, 1 + j, (*chip, mc), me).wait_recv()
                fwd[j].start()
        for a in range(len(entries)):
            copy(a, 0, sibling, me).wait_recv()
            for j, chip in enumerate(chips):
                copy(a, 4 + j, (*chip, 1 - mc), me).wait_recv()
            for cp in first(a) + passed(a):
                cp.wait_send()
            own(a).wait()

    return start, finish


def _scatter_plan(entries, send_sems, recv_sems, local_sems):
    mx, my, mc = lax.axis_index("x"), lax.axis_index("y"), lax.axis_index("c")
    me = 4 * mx + 2 * my + mc

    def src(a, dev):
        return entries[a][0].at[dev]

    def when(a, x_of_dest, fn):
        half = entries[a][3]
        if half is None:
            fn()
        else:
            pl.when(x_of_dest == half)(fn)

    def copies(a):
        _, recv_ref, lr, _ = entries[a]
        out = []
        for k in range(1, N_DEV):
            px = 1 - mx if k & 4 else mx
            py = 1 - my if k & 2 else my
            pc = 1 - mc if k & 1 else mc
            peer = 4 * px + 2 * py + pc
            sems = dict(send_sem=send_sems.at[COPIES_PER_ARRAY * a + k - 1],
                        recv_sem=recv_sems.at[COPIES_PER_ARRAY * a + k - 1],
                        device_id=(px, py, pc), device_id_type=MESH_ID)
            sends = pltpu.make_async_remote_copy(src_ref=src(a, peer), dst_ref=recv_ref.at[lr, me], **sems)
            lands = pltpu.make_async_remote_copy(src_ref=src(a, me), dst_ref=recv_ref.at[lr, peer], **sems)
            out.append((sends, lands, px))
        return out

    def own(a):
        _, recv_ref, lr, _ = entries[a]
        return pltpu.make_async_copy(src(a, me), recv_ref.at[lr, me], local_sems.at[a])

    def start():
        for a in range(len(entries)):
            when(a, mx, own(a).start)
            for sends, _, px in copies(a):
                when(a, px, sends.start)

    def finish():
        for a in range(len(entries)):
            for _, lands, _ in copies(a):
                when(a, mx, lands.wait_recv)
        for a in range(len(entries)):
            for sends, _, px in copies(a):
                when(a, px, sends.wait_send)
            when(a, mx, own(a).wait)

    return start, finish


MM_TILE, MM_TK = 1536, 2048
MM_TILE_LN = 512


def _mm(a, b, *, ta=False, tb=False, out_dtype=F32, res=None, res_scale=1.0, into=None, ln_bwd=None, scatter=(),
        name):
    segs = list(a) if isinstance(a, (list, tuple)) else [a]
    if ta:
        (K, M), seg_k = segs[0].shape, [segs[0].shape[0]]
        assert len(segs) == 1
    else:
        M, seg_k = segs[0].shape[0], [s.shape[1] for s in segs]
        K = sum(seg_k)
    if tb:
        N, Kb = b.shape
    else:
        Kb, N = b.shape
    assert K == Kb, ([s.shape for s in segs], b.shape, ta, tb)
    row_off = into[1] if into is not None else 0
    tm, tn = _tile(math.gcd(M, row_off), MM_TILE if ln_bwd is None else MM_TILE_LN), _tile(N, MM_TILE)
    tk = _tile(K, MM_TK) if len(segs) == 1 else _tile(math.gcd(*seg_k), MM_TILE)
    nk = K // tk
    seg_chunks = [ks // tk for ks in seg_k]
    seg_first = [sum(seg_chunks[:s]) for s in range(len(segs))]
    dims = (((0 if ta else 1,), (1 if tb else 0,)), ((), ()))
    ns = len(segs)
    n_res = ns + 1
    n_ln = n_res + (res is not None)
    into_held = into is not None and not isinstance(into[0], jax.ShapeDtypeStruct)
    n_own = n_ln + (3 if ln_bwd is not None else 0) + into_held
    assert ln_bwd is None or (tn == N and into is None)
    na = len(scatter)
    assert na == 0 or (ln_bwd is None and into is None)
    comm_in, comm_out, held = _scatter_io(scatter)
    n_in = n_own + len(comm_in)
    grid = (M // tm, N // tn, nk)

    def body(*refs):
        a_refs, b_ref = refs[:ns], refs[ns]
        r_ref = refs[n_res] if res is not None else None
        o_ref = refs[n_in]
        first_row_tile = pl.program_id(0) == 0
        if na:
            ids = [pl.program_id(d) for d in range(3)]
            start, wait_all = _scatter_plan(
                [(refs[n_own + a], refs[n_in + 1 + a], *scatter[a][2:]) for a in range(na)], *refs[len(refs) - 3:])
            pl.when((ids[0] == 0) & (ids[1] == 0) & (ids[2] == 0))(start)
            last_step = (ids[0] == grid[0] - 1) & (ids[1] == grid[1] - 1) & (ids[2] == grid[2] - 1)

        def finish(out):
            if r_ref is not None:
                out = out + res_scale * r_ref[...]
            if ln_bwd is None:
                o_ref[...] = out.astype(o_ref.dtype)
                return
            xh_ref, rs_ref, g_ref = refs[n_ln:n_ln + 3]
            ob_ref, dg_ref, db_ref = refs[n_in + 1:n_in + 4]

            @pl.when(first_row_tile)
            def _():
                dg_ref[...] = jnp.zeros_like(dg_ref)
                db_ref[...] = jnp.zeros_like(db_ref)

            xh = xh_ref[...]
            dxh = out * g_ref[...]
            m1 = jnp.mean(dxh, axis=-1, keepdims=True)
            m2 = jnp.mean(dxh * xh, axis=-1, keepdims=True)
            dz = rs_ref[...] * (dxh - m1 - xh * m2)
            o_ref[...] = dz
            ob_ref[...] = dz.astype(BF16)
            dg_ref[...] += jnp.sum(out * xh, axis=0, keepdims=True)
            db_ref[...] += jnp.sum(out, axis=0, keepdims=True)

        def prod(s):
            return _dot(a_refs[s][...].astype(BF16), b_ref[...].astype(BF16), dims)

        if nk == 1:
            finish(prod(0))
        else:
            acc = refs[n_in + (4 if ln_bwd is not None else 1) + na]
            k = pl.program_id(2)

            @pl.when(k == 0)
            def _():
                acc[...] = jnp.zeros_like(acc)

            for s in range(ns):
                def add(s=s):
                    acc[...] += prod(s)
                pl.when((k >= seg_first[s]) & (k < seg_first[s] + seg_chunks[s]))(add)

            @pl.when(k == nk - 1)
            def _():
                finish(acc[...])
        if na:
            pl.when(last_step)(wait_all)

    if ta:
        a_specs = [pl.BlockSpec((tk, tm), lambda i, j, k: (k, i))]
    else:
        a_specs = [pl.BlockSpec((tm, tk), functools.partial(
            lambda i, j, k, first, n: (i, jnp.clip(k - first, 0, n - 1)), first=seg_first[s], n=seg_chunks[s]))
            for s in range(ns)]
    b_spec = pl.BlockSpec((tn, tk), lambda i, j, k: (j, k)) if tb else pl.BlockSpec((tk, tn), lambda i, j, k: (k, j))
    in_specs = a_specs + [b_spec]
    args = segs + [b]
    if res is not None:
        in_specs.append(pl.BlockSpec((tm, tn), lambda i, j, k: (i, j)))
        args.append(res)
    out_spec = pl.BlockSpec((tm, tn), lambda i, j, k: (i, j))
    if ln_bwd is not None:
        xh, rs, g = ln_bwd
        in_specs += [out_spec, pl.BlockSpec((tm, 1), lambda i, j, k: (i, 0)), pl.BlockSpec((1, tn), lambda i, j, k: (0, j))]
        args += [xh, rs, g]
        vec = pl.BlockSpec((1, tn), lambda i, j, k: (0, j))
        return pl.pallas_call(
            body, name=name,
            out_shape=(jax.ShapeDtypeStruct((M, N), F32), jax.ShapeDtypeStruct((M, N), BF16),
                       jax.ShapeDtypeStruct((1, N), F32), jax.ShapeDtypeStruct((1, N), F32)),
            grid=(M // tm, N // tn, nk),
            in_specs=in_specs,
            out_specs=(out_spec, out_spec, vec, vec),
            scratch_shapes=[pltpu.VMEM((tm, tn), F32)] if nk > 1 else [],
            compiler_params=_cp(("arbitrary", "arbitrary", "arbitrary")),
        )(*args)
    if na:
        res_all = pl.pallas_call(
            body, name=name,
            out_shape=(jax.ShapeDtypeStruct((M, N), out_dtype), *comm_out),
            grid=grid,
            in_specs=in_specs + [_ANY] * len(comm_in),
            out_specs=(out_spec, *([_ANY] * na)),
            scratch_shapes=([pltpu.VMEM((tm, tn), F32)] if nk > 1 else []) + _comm_scratch(na),
            input_output_aliases={n_own + na + i: 1 + a for i, a in enumerate(held)},
            compiler_params=_cp(("arbitrary", "arbitrary", "arbitrary")),
        )(*args, *comm_in)
        return res_all[0], list(res_all[1:])
    if into is None:
        out_shape = jax.ShapeDtypeStruct((M, N), out_dtype)
        blk_off, aliases = 0, {}
    else:
        buf = into[0]
        assert buf.shape[1] == N and row_off % tm == 0 and row_off + M <= buf.shape[0], (buf.shape, M, N, row_off)
        out_shape = jax.ShapeDtypeStruct(buf.shape, buf.dtype)
        blk_off, aliases = row_off // tm, {}
        if into_held:
            aliases = {n_in - 1: 0}
            in_specs.append(_ANY)
            args.append(buf)
    return pl.pallas_call(
        body, name=name,
        out_shape=out_shape,
        grid=(M // tm, N // tn, nk),
        in_specs=in_specs,
        out_specs=pl.BlockSpec((tm, tn), lambda i, j, k: (i + blk_off, j)),
        scratch_shapes=[pltpu.VMEM((tm, tn), F32)] if nk > 1 else [],
        input_output_aliases=aliases,
        compiler_params=_cp(("parallel", "parallel", "arbitrary")),
    )(*args)


def _mm_res_ln(a, w, x, g, b, *, name):
    S, K = a.shape
    D = w.shape[1]
    tm = _tile(S, ROW_TILE)

    def body(a_ref, w_ref, x_ref, g_ref, b_ref, y_ref, yb_ref, xh_ref, rs_ref):
        h = _dot(a_ref[...], w_ref[...])
        z = ALPHA * x_ref[...] + h
        mu = jnp.mean(z, axis=-1, keepdims=True)
        zc = z - mu
        var = jnp.mean(zc * zc, axis=-1, keepdims=True)
        r = lax.rsqrt(var + LN_EPS)
        xh = zc * r
        y = xh * g_ref[...] + b_ref[...]
        y_ref[...] = y
        yb_ref[...] = y.astype(BF16)
        xh_ref[...] = xh
        rs_ref[...] = r

    row = lambda i: (i, 0)
    full = lambda i: (0, 0)
    return pl.pallas_call(
        body, name=name,
        out_shape=(jax.ShapeDtypeStruct((S, D), F32), jax.ShapeDtypeStruct((S, D), BF16),
                   jax.ShapeDtypeStruct((S, D), F32), jax.ShapeDtypeStruct((S, 1), F32)),
        grid=(S // tm,),
        in_specs=[pl.BlockSpec((tm, K), row), pl.BlockSpec((K, D), full), pl.BlockSpec((tm, D), row),
                  pl.BlockSpec((1, D), full), pl.BlockSpec((1, D), full)],
        out_specs=(pl.BlockSpec((tm, D), row), pl.BlockSpec((tm, D), row), pl.BlockSpec((tm, D), row),
                   pl.BlockSpec((tm, 1), row)),
        compiler_params=_cp(("parallel",)),
    )(a, w, x, g, b)


def _ln_bwd(dy, xh, rs, g, *, name):
    S, D = dy.shape
    tm = _tile(S, ROW_TILE)

    def body(dy_ref, xh_ref, rs_ref, g_ref, dz_ref, dzb_ref, dg_ref, db_ref):
        @pl.when(pl.program_id(0) == 0)
        def _():
            dg_ref[...] = jnp.zeros_like(dg_ref)
            db_ref[...] = jnp.zeros_like(db_ref)

        dy = dy_ref[...]
        xh = xh_ref[...]
        dxh = dy * g_ref[...]
        m1 = jnp.mean(dxh, axis=-1, keepdims=True)
        m2 = jnp.mean(dxh * xh, axis=-1, keepdims=True)
        dz = rs_ref[...] * (dxh - m1 - xh * m2)
        dz_ref[...] = dz
        dzb_ref[...] = dz.astype(BF16)
        dg_ref[...] += jnp.sum(dy * xh, axis=0, keepdims=True)
        db_ref[...] += jnp.sum(dy, axis=0, keepdims=True)

    row = lambda i: (i, 0)
    full = lambda i: (0, 0)
    return pl.pallas_call(
        body, name=name,
        out_shape=(jax.ShapeDtypeStruct((S, D), F32), jax.ShapeDtypeStruct((S, D), BF16),
                   jax.ShapeDtypeStruct((1, D), F32), jax.ShapeDtypeStruct((1, D), F32)),
        grid=(S // tm,),
        in_specs=[pl.BlockSpec((tm, D), row), pl.BlockSpec((tm, D), row), pl.BlockSpec((tm, 1), row),
                  pl.BlockSpec((1, D), full)],
        out_specs=(pl.BlockSpec((tm, D), row), pl.BlockSpec((tm, D), row), pl.BlockSpec((1, D), full),
                   pl.BlockSpec((1, D), full)),
        compiler_params=_cp(("arbitrary",)),
    )(dy, xh, rs, g)


def _loss_head(y, t, *, name):
    S, D = y.shape
    tm = _tile(S, 512)

    def body(y_ref, t_ref, dy_ref, l_ref):
        @pl.when(pl.program_id(0) == 0)
        def _():
            l_ref[...] = jnp.zeros_like(l_ref)

        e = y_ref[...] - t_ref[...]
        dy_ref[...] = e / D
        l_ref[...] += 0.5 * jnp.sum(jnp.mean(e * e, axis=-1, keepdims=True), axis=0, keepdims=True)

    row = lambda i: (i, 0)
    return pl.pallas_call(
        body, name=name,
        out_shape=(jax.ShapeDtypeStruct((S, D), F32), jax.ShapeDtypeStruct((1, 1), F32)),
        grid=(S // tm,),
        in_specs=[pl.BlockSpec((tm, D), row), pl.BlockSpec((tm, D), row)],
        out_specs=(pl.BlockSpec((tm, D), row), pl.BlockSpec((1, 1), lambda i: (0, 0))),
        compiler_params=_cp(("arbitrary",)),
    )(y, t)


def _rope_tables(S):
    pos = jnp.arange(S, dtype=jnp.int32)[:, None]
    lane = jnp.arange(LANES, dtype=jnp.int32)[None, :] % HEAD_DIM

    def inv(d):
        half = d // 2
        return jnp.tile(ROPE_THETA ** (-jnp.arange(half, dtype=F32) * (2.0 / d)), LANES // half)[None, :]

    def cs(p, d):
        ang = p.astype(F32) * inv(d)
        return jnp.cos(ang), jnp.where(lane % d < d // 2, -jnp.sin(ang), jnp.sin(ang))

    ca, sa = cs(pos, HEAD_DIM)
    axial = jnp.where(lane < HEAD_DIM // 2, pos // GRID_W, pos % GRID_W)
    cb, sb = cs(axial, HEAD_DIM // 2)
    return ca, sa, cb, sb


def _partner(x, lane, width):
    h = width // 2
    return jnp.where(lane % width < h, pltpu.roll(x, LANES - h, 1), pltpu.roll(x, h, 1))


def _rope_fwd(x, c, s, lane, width):
    return x * c + _partner(x, lane, width) * s


def _rope_bwd(dy, c, s, lane, width):
    return dy * c + _partner(dy * s, lane, width)


def _head_sum(x, seg):
    return lax.dot_general(x, seg, _NN, precision=lax.Precision.HIGHEST, preferred_element_type=F32)


def _split_heads(x, lane):
    lo = lane < HEAD_DIM
    r = pltpu.roll(x, HEAD_DIM, 1)
    z = jnp.zeros_like(x)
    return jnp.where(lo, x, z), jnp.where(lo, z, r), jnp.where(lo, r, z), jnp.where(lo, z, x)


def _fold_heads(d0, d1, lane):
    t0 = d0 + pltpu.roll(d0, HEAD_DIM, 1)
    t1 = d1 + pltpu.roll(d1, HEAD_DIM, 1)
    return jnp.where(lane < HEAD_DIM, t0, t1)


def _seg_matrix():
    i = jnp.arange(LANES)
    return (i[:, None] // HEAD_DIM == i[None, :] // HEAD_DIM).astype(F32)


def _prep(proj, tabs, qg2, kg2, seg, *, name):
    S = proj.shape[0]
    ts = _tile(S, ROW_TILE)
    ca, sa, cb, sb = tabs

    def body(pa_ref, pb_ref, ca_ref, sa_ref, cb_ref, sb_ref, qg_ref, kg_ref, seg_ref,
             aq_ref, ak_ref, av_ref, bq_ref, bk_ref, bv_ref):
        lane = lax.broadcasted_iota(jnp.int32, (ts, LANES), 1)
        ca, sa, cb, sb = ca_ref[...], sa_ref[...], cb_ref[...], sb_ref[...]
        seg = seg_ref[...]

        def norm(x, gain):
            r = lax.rsqrt(_head_sum(x * x, seg) * (1.0 / HEAD_DIM) + RMS_EPS)
            return x * r * gain

        def put(ref, x):
            for i, part in enumerate(_split_heads(x, lane)):
                ref[i] = part.astype(BF16)

        for gidx in range(4):
            cols = slice(gidx * LANES, (gidx + 1) * LANES)
            aq_ref[:, cols] = (_rope_fwd(pa_ref[:, cols], ca, sa, lane, HEAD_DIM) * 0.125).astype(BF16)
            bq = norm(pb_ref[:, cols], qg_ref[...])
            bq_ref[:, cols] = (_rope_fwd(bq, cb, sb, lane, HEAD_DIM // 2) * 0.125).astype(BF16)
        put(ak_ref, _rope_fwd(pa_ref[:, 512:640], ca, sa, lane, HEAD_DIM))
        put(av_ref, pa_ref[:, 640:768])
        bk = norm(pb_ref[:, 512:640], kg_ref[...])
        put(bk_ref, _rope_fwd(bk, cb, sb, lane, HEAD_DIM // 2))
        put(bv_ref, pb_ref[:, 640:768])

    row = lambda i: (i, 0)
    full = lambda i: (0, 0)
    tab = pl.BlockSpec((ts, LANES), row)
    kv_shape = jax.ShapeDtypeStruct((4, S, LANES), BF16)
    kv_spec = pl.BlockSpec((4, ts, LANES), lambda i: (0, i, 0))
    q_shape = jax.ShapeDtypeStruct((S, 512), BF16)
    q_spec = pl.BlockSpec((ts, 512), row)
    return pl.pallas_call(
        body, name=name,
        out_shape=(q_shape, kv_shape, kv_shape, q_shape, kv_shape, kv_shape),
        grid=(S // ts,),
        in_specs=[pl.BlockSpec((ts, QKV_W), lambda i: (i, 0)), pl.BlockSpec((ts, QKV_W), lambda i: (i, 1)),
                  tab, tab, tab, tab, pl.BlockSpec((1, LANES), full), pl.BlockSpec((1, LANES), full),
                  pl.BlockSpec((LANES, LANES), full)],
        out_specs=(q_spec, kv_spec, kv_spec, q_spec, kv_spec, kv_spec),
        compiler_params=_cp(("parallel",)),
    )(proj, proj, ca, sa, cb, sb, qg2, kg2, seg)


def _unprep(dqa, dka, dva, dqb, dkb, dvb, proj, tabs, qg2, kg2, seg, *, name):
    S = proj.shape[0]
    ts = _tile(S, ROW_TILE)
    ca, sa, cb, sb = tabs

    def body(dqa_ref, dka_ref, dva_ref, dqb_ref, dkb_ref, dvb_ref, pb_ref, ca_ref, sa_ref, cb_ref, sb_ref,
             qg_ref, kg_ref, seg_ref, dp_ref, dqg_ref, dkg_ref):
        @pl.when(pl.program_id(0) == 0)
        def _():
            dqg_ref[...] = jnp.zeros_like(dqg_ref)
            dkg_ref[...] = jnp.zeros_like(dkg_ref)

        lane = lax.broadcasted_iota(jnp.int32, (ts, LANES), 1)
        ca, sa, cb, sb = ca_ref[...], sa_ref[...], cb_ref[...], sb_ref[...]
        seg = seg_ref[...]

        def norm_bwd(dy, x, gain):
            r = lax.rsqrt(_head_sum(x * x, seg) * (1.0 / HEAD_DIM) + RMS_EPS)
            gdy = gain * dy
            dot = _head_sum(gdy * x, seg) * (1.0 / HEAD_DIM)
            dx = r * gdy - x * (r * r * r) * dot
            return dx, jnp.sum(dy * x * r, axis=0, keepdims=True)

        for gidx in range(4):
            cols = slice(gidx * LANES, (gidx + 1) * LANES)
            dp_ref[:, cols] = _rope_bwd(dqa_ref[:, cols] * 0.125, ca, sa, lane, HEAD_DIM).astype(BF16)
            dbq = _rope_bwd(dqb_ref[:, cols] * 0.125, cb, sb, lane, HEAD_DIM // 2)
            dx, dg = norm_bwd(dbq, pb_ref[:, cols], qg_ref[...])
            dp_ref[:, COL_B + gidx * LANES:COL_B + (gidx + 1) * LANES] = dx.astype(BF16)
            dqg_ref[...] += dg
        dak = _fold_heads(dka_ref[0] + dka_ref[1], dka_ref[2] + dka_ref[3], lane)
        dp_ref[:, 512:640] = _rope_bwd(dak, ca, sa, lane, HEAD_DIM).astype(BF16)
        dp_ref[:, 640:768] = _fold_heads(dva_ref[0] + dva_ref[1], dva_ref[2] + dva_ref[3], lane).astype(BF16)
        dbk = _fold_heads(dkb_ref[0] + dkb_ref[1], dkb_ref[2] + dkb_ref[3], lane)
        dbk = _rope_bwd(dbk, cb, sb, lane, HEAD_DIM // 2)
        dx, dg = norm_bwd(dbk, pb_ref[:, 512:640], kg_ref[...])
        dp_ref[:, COL_B + 512:COL_B + 640] = dx.astype(BF16)
        dkg_ref[...] += dg
        dp_ref[:, COL_B + 640:COL_B + 768] = _fold_heads(dvb_ref[0] + dvb_ref[1], dvb_ref[2] + dvb_ref[3],
                                                         lane).astype(BF16)

    row = lambda i: (i, 0)
    full = lambda i: (0, 0)
    tab = pl.BlockSpec((ts, LANES), row)
    q_spec = pl.BlockSpec((ts, 512), row)
    kv_spec = pl.BlockSpec((4, ts, LANES), lambda i: (0, i, 0))
    return pl.pallas_call(
        body, name=name,
        out_shape=(jax.ShapeDtypeStruct((S, 2 * QKV_W), BF16), jax.ShapeDtypeStruct((1, LANES), F32),
                   jax.ShapeDtypeStruct((1, LANES), F32)),
        grid=(S // ts,),
        in_specs=[q_spec, kv_spec, kv_spec, q_spec, kv_spec, kv_spec,
                  pl.BlockSpec((ts, QKV_W), lambda i: (i, 1)), tab, tab, tab, tab,
                  pl.BlockSpec((1, LANES), full), pl.BlockSpec((1, LANES), full), pl.BlockSpec((LANES, LANES), full)],
        out_specs=(pl.BlockSpec((ts, 2 * QKV_W), row), pl.BlockSpec((1, LANES), full),
                   pl.BlockSpec((1, LANES), full)),
        compiler_params=_cp(("arbitrary",)),
    )(dqa, dka, dva, dqb, dkb, dvb, proj, ca, sa, cb, sb, qg2, kg2, seg)


def _attn_dense_fwd(q, k4, v4, *, gather=(), name):
    S = q.shape[0]
    tq = _tile(S, 256)
    xs = [x for x, _ in gather]
    na = len(xs)

    def body(q_ref, k_ref, v_ref, *rest):
        o_ref, lse_ref = rest[na], rest[na + 1]
        if na:
            x_refs, out_refs, sems = rest[:na], rest[na + 2:2 * na + 2], rest[2 * na + 2:]
            start, finish = _gather_plan([(x_refs[a], gather[a][1], out_refs[a]) for a in range(na)], *sems)
            pl.when((pl.program_id(0) == 0) & (pl.program_id(1) == 0))(start)
        for pr in range(2):
            qp = q_ref[:, pr * LANES:(pr + 1) * LANES]
            acc = None
            for half in range(2):
                s = _dot(qp, k_ref[half], _NT)
                m = jnp.max(s, axis=-1, keepdims=True)
                e = jnp.exp(s - m)
                l = jnp.sum(e, axis=-1, keepdims=True)
                pv = _dot(e.astype(BF16), v_ref[half]) * (1.0 / l)
                acc = pv if acc is None else acc + pv
                lse_ref[pr * 2 + half] = m + jnp.log(l)
            o_ref[:, pr * LANES:(pr + 1) * LANES] = acc.astype(BF16)
        if na:
            pl.when((pl.program_id(0) == 1) & (pl.program_id(1) == S // tq - 1))(finish)

    kv_spec = pl.BlockSpec((2, S, LANES), lambda kv, i: (kv, 0, 0))
    res = pl.pallas_call(
        body, name=name,
        out_shape=(jax.ShapeDtypeStruct((S, 512), BF16), jax.ShapeDtypeStruct((8, S, 1), F32),
                   *[_gathered_shape(x) for x in xs]),
        grid=(2, S // tq),
        in_specs=[pl.BlockSpec((tq, 256), lambda kv, i: (i, kv)), kv_spec, kv_spec] + [_ANY] * na,
        out_specs=(pl.BlockSpec((tq, 256), lambda kv, i: (i, kv)),
                   pl.BlockSpec((4, tq, 1), lambda kv, i: (kv, i, 0)), *([_ANY] * na)),
        scratch_shapes=_comm_scratch(na) if na else [],
        compiler_params=_cp(("arbitrary", "arbitrary") if na else ("parallel", "parallel")),
    )(q, k4, v4, *xs)
    return res[0], res[1], list(res[2:])


def _attn_dense_bwd(q, k4, v4, lse, do, *, scatter=(), name):
    S = q.shape[0]
    tq = _tile(S, 256)
    na = len(scatter)
    comm_in, comm_out, held = _scatter_io(scatter)
    n_in = len(comm_in)

    def body(q_ref, k_ref, v_ref, lse_ref, do_ref, *rest):
        dq_ref, dk_ref, dv_ref = rest[n_in:n_in + 3]
        if na:
            s_refs, r_refs, sems = rest[:na], rest[n_in + 3:n_in + 3 + na], rest[n_in + 3 + na:]
            start, finish = _scatter_plan([(s_refs[a], r_refs[a], *scatter[a][2:]) for a in range(na)], *sems)
            pl.when((pl.program_id(0) == 0) & (pl.program_id(1) == 0))(start)

        @pl.when(pl.program_id(1) == 0)
        def _():
            dk_ref[...] = jnp.zeros_like(dk_ref)
            dv_ref[...] = jnp.zeros_like(dv_ref)

        lane = lax.broadcasted_iota(jnp.int32, (tq, LANES), 1)
        for pr in range(2):
            qp = q_ref[:, pr * LANES:(pr + 1) * LANES]
            dop = do_ref[:, pr * LANES:(pr + 1) * LANES].astype(BF16)
            dq = None
            for half in range(2):
                mine = (lane < HEAD_DIM) if half == 0 else (lane >= HEAD_DIM)
                s = _dot(qp, k_ref[half], _NT)
                p = jnp.exp(s - lse_ref[pr * 2 + half])
                dp = _dot(dop, v_ref[half], _NT)
                delta = jnp.sum(p * dp, axis=-1, keepdims=True)
                ds = (p * (dp - delta)).astype(BF16)
                pb = p.astype(BF16)
                d = _dot(ds, k_ref[half])
                dq = d if dq is None else dq + d
                dk_ref[half] += _dot(ds, jnp.where(mine, qp, jnp.zeros_like(qp)), _TN)
                dv_ref[half] += _dot(pb, jnp.where(mine, dop, jnp.zeros_like(dop)), _TN)
            dq_ref[:, pr * LANES:(pr + 1) * LANES] = dq
        if na:
            pl.when((pl.program_id(0) == 1) & (pl.program_id(1) == S // tq - 1))(finish)

    kv_spec = pl.BlockSpec((2, S, LANES), lambda kv, i: (kv, 0, 0))
    q_spec = pl.BlockSpec((tq, 256), lambda kv, i: (i, kv))
    res = pl.pallas_call(
        body, name=name,
        out_shape=(jax.ShapeDtypeStruct((S, 512), F32), jax.ShapeDtypeStruct((4, S, LANES), F32),
                   jax.ShapeDtypeStruct((4, S, LANES), F32), *comm_out),
        grid=(2, S // tq),
        in_specs=[q_spec, kv_spec, kv_spec, pl.BlockSpec((4, tq, 1), lambda kv, i: (kv, i, 0)), q_spec]
                 + [_ANY] * n_in,
        out_specs=(q_spec, kv_spec, kv_spec, *([_ANY] * na)),
        scratch_shapes=_comm_scratch(na) if na else [],
        input_output_aliases={5 + na + i: 3 + a for i, a in enumerate(held)},
        compiler_params=_cp(("arbitrary", "arbitrary") if na else ("parallel", "arbitrary")),
    )(q, k4, v4, lse, do, *comm_in)
    return res[0], res[1], res[2], list(res[3:])


WIN_Q = 2 * BLOCK
WIN_KEYS = WIN_Q + 2 * WINDOW


def _win_start(n, S):
    return pl.multiple_of(jnp.clip(n * WIN_Q - WINDOW, 0, S - WIN_KEYS), BLOCK)


def _win_valid(n, start):
    qpos = n * WIN_Q + lax.broadcasted_iota(jnp.int32, (WIN_Q, WIN_KEYS), 0)
    kpos = start + lax.broadcasted_iota(jnp.int32, (WIN_Q, WIN_KEYS), 1)
    return jnp.abs(qpos - kpos) <= WINDOW


def _attn_win_fwd(q, k4, v4, sink, *, name):
    S = q.shape[0]
    assert S >= WIN_KEYS

    def body(sink_ref, q_ref, k_ref, v_ref, o_ref, lse_ref):
        n = pl.program_id(0)
        start = _win_start(n, S)
        valid = _win_valid(n, start)
        for kv in range(2):
            for pr in range(2):
                cols = slice((kv * 2 + pr) * LANES, (kv * 2 + pr + 1) * LANES)
                qp = q_ref[:, cols]
                acc = None
                for half in range(2):
                    h = kv * 4 + pr * 2 + half
                    kk = k_ref[kv * 2 + half, pl.ds(start, WIN_KEYS), :]
                    vv = v_ref[kv * 2 + half, pl.ds(start, WIN_KEYS), :]
                    s = jnp.where(valid, _dot(qp, kk, _NT), NEG_BIG)
                    snk = sink_ref[h]
                    m = jnp.maximum(jnp.max(s, axis=-1, keepdims=True), snk)
                    e = jnp.exp(s - m)
                    l = jnp.sum(e, axis=-1, keepdims=True) + jnp.exp(snk - m)
                    pv = _dot(e.astype(BF16), vv) * (1.0 / l)
                    acc = pv if acc is None else acc + pv
                    lse_ref[h] = m + jnp.log(l)
                o_ref[:, cols] = acc.astype(BF16)

    kv_spec = pl.BlockSpec((4, S, LANES), lambda n: (0, 0, 0))
    return pl.pallas_call(
        body, name=name,
        out_shape=(jax.ShapeDtypeStruct((S, 512), BF16), jax.ShapeDtypeStruct((8, S, 1), F32)),
        grid=(S // WIN_Q,),
        in_specs=[pl.BlockSpec(memory_space=pltpu.SMEM), pl.BlockSpec((WIN_Q, 512), lambda n: (n, 0)),
                  kv_spec, kv_spec],
        out_specs=(pl.BlockSpec((WIN_Q, 512), lambda n: (n, 0)), pl.BlockSpec((8, WIN_Q, 1), lambda n: (0, n, 0))),
        compiler_params=_cp(("parallel",)),
    )(sink, q, k4, v4)


def _attn_win_bwd(q, k4, v4, sink, lse, do, *, name):
    S = q.shape[0]

    def body(sink_ref, q_ref, k_ref, v_ref, lse_ref, do_ref, dq_ref, dk_ref, dv_ref, dsink_ref):
        n = pl.program_id(0)

        @pl.when(n == 0)
        def _():
            dk_ref[...] = jnp.zeros_like(dk_ref)
            dv_ref[...] = jnp.zeros_like(dv_ref)
            dsink_ref[...] = jnp.zeros_like(dsink_ref)

        start = _win_start(n, S)
        valid = _win_valid(n, start)
        lane = lax.broadcasted_iota(jnp.int32, (WIN_Q, LANES), 1)
        for kv in range(2):
            for pr in range(2):
                cols = slice((kv * 2 + pr) * LANES, (kv * 2 + pr + 1) * LANES)
                qp = q_ref[:, cols]
                dop = do_ref[:, cols].astype(BF16)
                dq = None
                for half in range(2):
                    h = kv * 4 + pr * 2 + half
                    slot = kv * 2 + half
                    mine = (lane < HEAD_DIM) if half == 0 else (lane >= HEAD_DIM)
                    win = pl.ds(start, WIN_KEYS)
                    kk = k_ref[slot, win, :]
                    vv = v_ref[slot, win, :]
                    lse_h = lse_ref[h]
                    s = jnp.where(valid, _dot(qp, kk, _NT), NEG_BIG)
                    p = jnp.exp(s - lse_h)
                    dp = _dot(dop, vv, _NT)
                    delta = jnp.sum(p * dp, axis=-1, keepdims=True)
                    ds = (p * (dp - delta)).astype(BF16)
                    pb = p.astype(BF16)
                    d = _dot(ds, kk)
                    dq = d if dq is None else dq + d
                    dk_ref[slot, win, :] += _dot(ds, jnp.where(mine, qp, jnp.zeros_like(qp)), _TN)
                    dv_ref[slot, win, :] += _dot(pb, jnp.where(mine, dop, jnp.zeros_like(dop)), _TN)
                    p_sink = jnp.exp(sink_ref[h] - lse_h)
                    dsink_ref[h:h + 1, :] += jnp.broadcast_to(-jnp.sum(p_sink * delta, axis=0, keepdims=True),
                                                              (1, LANES))
                dq_ref[:, cols] = dq

    kv_spec = pl.BlockSpec((4, S, LANES), lambda n: (0, 0, 0))
    q_spec = pl.BlockSpec((WIN_Q, 512), lambda n: (n, 0))
    return pl.pallas_call(
        body, name=name,
        out_shape=(jax.ShapeDtypeStruct((S, 512), F32), jax.ShapeDtypeStruct((4, S, LANES), F32),
                   jax.ShapeDtypeStruct((4, S, LANES), F32), jax.ShapeDtypeStruct((8, LANES), F32)),
        grid=(S // WIN_Q,),
        in_specs=[pl.BlockSpec(memory_space=pltpu.SMEM), q_spec, kv_spec, kv_spec,
                  pl.BlockSpec((8, WIN_Q, 1), lambda n: (0, n, 0)), q_spec],
        out_specs=(q_spec, kv_spec, kv_spec, pl.BlockSpec((8, LANES), lambda n: (0, 0))),
        compiler_params=_cp(("arbitrary",)),
    )(sink, q, k4, v4, lse, do)


def _c_ln(v, g, b):
    mu = jnp.mean(v, axis=-1, keepdims=True)
    vc = v - mu
    r = lax.rsqrt(jnp.mean(vc * vc, axis=-1, keepdims=True) + LN_EPS)
    vh = vc * r
    return vh, r, vh * g + b


def _gmlp_blocks(rows):
    return [(slice(c * CHUNK, (c + 1) * CHUNK), slice(gi * LANES, (gi + 1) * LANES), gi)
            for c in range(rows // CHUNK) for gi in range(C_GROUPS)]


def _gmlp_fwd(proj, ws, bs3, lg, lb, *, name):
    S = proj.shape[0]
    rows = _tile(S, ROW_TILE)

    def body(u_ref, v_ref, ws_ref, bs_ref, lg_ref, lb_ref, o_ref):
        u = _gelu(u_ref[...])
        _, _, vn = _c_ln(_gelu(v_ref[...]), lg_ref[...], lb_ref[...])
        vn = vn.astype(BF16)
        for r, cols, gi in _gmlp_blocks(rows):
            mixed = _dot(ws_ref[gi], vn[r, cols]) + bs_ref[gi]
            o_ref[r, cols] = (u[r, cols] * mixed).astype(BF16)

    full2 = lambda n: (0, 0)
    full3 = lambda n: (0, 0, 0)
    return pl.pallas_call(
        body, name=name,
        out_shape=jax.ShapeDtypeStruct((S, C_WIDTH), BF16),
        grid=(S // rows,),
        in_specs=[pl.BlockSpec((rows, C_WIDTH), lambda n: (n, COL_C // C_WIDTH)),
                  pl.BlockSpec((rows, C_WIDTH), lambda n: (n, COL_C // C_WIDTH + 1)),
                  pl.BlockSpec((C_GROUPS, CHUNK, CHUNK), full3), pl.BlockSpec((C_GROUPS, CHUNK, 1), full3),
                  pl.BlockSpec((1, C_WIDTH), full2), pl.BlockSpec((1, C_WIDTH), full2)],
        out_specs=pl.BlockSpec((rows, C_WIDTH), lambda n: (n, 0)),
        compiler_params=_cp(("parallel",)),
    )(proj, proj, ws, bs3, lg, lb)


def _gmlp_bwd(proj, dout, ws, bs3, lg, lb, *, name):
    S = proj.shape[0]
    rows = _tile(S, ROW_TILE)
    nch = rows // CHUNK

    def body(u_ref, v_ref, d_ref, ws_ref, bs_ref, lg_ref, lb_ref, dz_ref, dws_ref, dbs_ref, dlg_ref, dlb_ref):
        @pl.when(pl.program_id(0) == 0)
        def _():
            dws_ref[...] = jnp.zeros_like(dws_ref)
            dbs_ref[...] = jnp.zeros_like(dbs_ref)
            dlg_ref[...] = jnp.zeros_like(dlg_ref)
            dlb_ref[...] = jnp.zeros_like(dlb_ref)

        u_pre, v_pre, d = u_ref[...], v_ref[...], d_ref[...]
        u, u_grad = _gelu_and_grad(u_pre)
        v, v_grad = _gelu_and_grad(v_pre)
        vh, r, vn = _c_ln(v, lg_ref[...], lb_ref[...])
        vnb = vn.astype(BF16)
        dm = d * u
        dvn_parts = []
        dws = [None] * C_GROUPS
        dbs = [None] * C_GROUPS
        for rs, cols, gi in _gmlp_blocks(rows):
            mixed = _dot(ws_ref[gi], vnb[rs, cols]) + bs_ref[gi]
            dz_ref[rs, cols] = (d[rs, cols] * mixed * u_grad[rs, cols]).astype(BF16)
            dmb = dm[rs, cols].astype(BF16)
            t, b = _dot(dmb, vnb[rs, cols], _NT), jnp.sum(dm[rs, cols], axis=-1, keepdims=True)
            dws[gi] = t if dws[gi] is None else dws[gi] + t
            dbs[gi] = b if dbs[gi] is None else dbs[gi] + b
            dvn_parts.append(_dot(ws_ref[gi], dmb, _TN))
        for gi in range(C_GROUPS):
            dws_ref[gi] += dws[gi]
            dbs_ref[gi] += dbs[gi]
        dvn = jnp.concatenate([jnp.concatenate(dvn_parts[c * C_GROUPS:(c + 1) * C_GROUPS], axis=-1)
                               for c in range(nch)], axis=0)
        dlg_ref[...] += jnp.sum(dvn * vh, axis=0, keepdims=True)
        dlb_ref[...] += jnp.sum(dvn, axis=0, keepdims=True)
        dvh = dvn * lg_ref[...]
        m1 = jnp.mean(dvh, axis=-1, keepdims=True)
        m2 = jnp.mean(dvh * vh, axis=-1, keepdims=True)
        dv = r * (dvh - m1 - vh * m2)
        dz_ref[:, C_WIDTH:] = (dv * v_grad).astype(BF16)

    full2 = lambda n: (0, 0)
    full3 = lambda n: (0, 0, 0)
    return pl.pallas_call(
        body, name=name,
        out_shape=(jax.ShapeDtypeStruct((S, 2 * C_WIDTH), BF16), jax.ShapeDtypeStruct((C_GROUPS, CHUNK, CHUNK), F32),
                   jax.ShapeDtypeStruct((C_GROUPS, CHUNK, 1), F32), jax.ShapeDtypeStruct((1, C_WIDTH), F32),
                   jax.ShapeDtypeStruct((1, C_WIDTH), F32)),
        grid=(S // rows,),
        in_specs=[pl.BlockSpec((rows, C_WIDTH), lambda n: (n, COL_C // C_WIDTH)),
                  pl.BlockSpec((rows, C_WIDTH), lambda n: (n, COL_C // C_WIDTH + 1)),
                  pl.BlockSpec((rows, C_WIDTH), lambda n: (n, 0)),
                  pl.BlockSpec((C_GROUPS, CHUNK, CHUNK), full3), pl.BlockSpec((C_GROUPS, CHUNK, 1), full3),
                  pl.BlockSpec((1, C_WIDTH), full2), pl.BlockSpec((1, C_WIDTH), full2)],
        out_specs=(pl.BlockSpec((rows, 2 * C_WIDTH), lambda n: (n, 0)), pl.BlockSpec((C_GROUPS, CHUNK, CHUNK), full3),
                   pl.BlockSpec((C_GROUPS, CHUNK, 1), full3), pl.BlockSpec((1, C_WIDTH), full2),
                   pl.BlockSpec((1, C_WIDTH), full2)),
        compiler_params=_cp(("arbitrary",)),
    )(proj, proj, dout, ws, bs3, lg, lb)


GATE_BLK = 512


def _gate_specs(tm, D):
    nh = D // GATE_BLK
    first = COL_GATE // GATE_BLK
    return [pl.BlockSpec((tm, GATE_BLK), functools.partial(lambda i, c: (i, c), c=first + b))
            for b in range(N_BRANCH * nh)]


def _merge_fwd(oa, ob, oc, wb, proj, bg, *, name):
    S = oa.shape[0]
    D = wb.shape[2]
    assert D % GATE_BLK == 0
    nh = D // GATE_BLK
    tm = _tile(S, ROW_TILE)

    def body(oa_ref, ob_ref, oc_ref, wb_ref, *rest):
        gate_refs, bg_ref, o_ref = rest[:N_BRANCH * nh], rest[N_BRANCH * nh], rest[N_BRANCH * nh + 1]
        brs = (oa_ref[...], ob_ref[...], oc_ref[...])
        for j in range(nh):
            cols = slice(j * GATE_BLK, (j + 1) * GATE_BLK)
            acc = None
            for n in range(N_BRANCH):
                b = n * nh + j
                t = _dot(brs[n], wb_ref[n, :, cols])
                g = _sigmoid(gate_refs[b][...] + bg_ref[:, b * GATE_BLK:(b + 1) * GATE_BLK])
                acc = t * g if acc is None else acc + t * g
            o_ref[:, cols] = acc.astype(BF16)

    row = lambda i: (i, 0)
    br = pl.BlockSpec((tm, BRANCH_WIDTH), row)
    return pl.pallas_call(
        body, name=name,
        out_shape=jax.ShapeDtypeStruct((S, D), BF16),
        grid=(S // tm,),
        in_specs=[br, br, br, pl.BlockSpec((N_BRANCH, BRANCH_WIDTH, D), lambda i: (0, 0, 0))]
                 + _gate_specs(tm, D) + [pl.BlockSpec((1, N_BRANCH * D), lambda i: (0, 0))],
        out_specs=pl.BlockSpec((tm, D), row),
        compiler_params=_cp(("parallel",)),
    )(oa, ob, oc, wb, *([proj] * (N_BRANCH * nh)), bg)


def _merge_bwd(oa, ob, oc, wb, proj, bg, dmerged, *, name):
    S = oa.shape[0]
    D = wb.shape[2]
    nh = D // GATE_BLK
    tm = _tile(S, ROW_TILE)

    def body(oa_ref, ob_ref, oc_ref, wb_ref, *rest):
        gate_refs = rest[:N_BRANCH * nh]
        bg_ref, dm_ref, dgl_ref, dbg_ref = rest[N_BRANCH * nh:N_BRANCH * nh + 4]
        dt_refs = rest[N_BRANCH * nh + 4:N_BRANCH * nh + 4 + N_BRANCH]
        dbr_refs = rest[N_BRANCH * nh + 4 + N_BRANCH:]

        @pl.when(pl.program_id(0) == 0)
        def _():
            dbg_ref[...] = jnp.zeros_like(dbg_ref)

        brs = (oa_ref[...], ob_ref[...], oc_ref[...])
        for n in range(N_BRANCH):
            dbr = None
            for j in range(nh):
                cols = slice(j * GATE_BLK, (j + 1) * GATE_BLK)
                b = n * nh + j
                gcols = slice(b * GATE_BLK, (b + 1) * GATE_BLK)
                w = wb_ref[n, :, cols]
                t = _dot(brs[n], w)
                g = _sigmoid(gate_refs[b][...] + bg_ref[:, gcols])
                dm = dm_ref[:, cols]
                dt = (dm * g).astype(BF16)
                dgl = dm * t * g * (1.0 - g)
                dt_refs[n][:, cols] = dt
                dgl_ref[:, gcols] = dgl.astype(BF16)
                dbg_ref[:, gcols] += jnp.sum(dgl, axis=0, keepdims=True)
                d = _dot(dt, w, _NT)
                dbr = d if dbr is None else dbr + d
            dbr_refs[n][...] = dbr.astype(dbr_refs[n].dtype)

    row = lambda i: (i, 0)
    br = pl.BlockSpec((tm, BRANCH_WIDTH), row)
    res = pl.pallas_call(
        body, name=name,
        out_shape=(jax.ShapeDtypeStruct((S, N_BRANCH * D), BF16), jax.ShapeDtypeStruct((1, N_BRANCH * D), F32),
                   *([jax.ShapeDtypeStruct((S, D), BF16)] * N_BRANCH),
                   *[jax.ShapeDtypeStruct((S, BRANCH_WIDTH), dt) for dt in (BF16, BF16, F32)]),
        grid=(S // tm,),
        in_specs=[br, br, br, pl.BlockSpec((N_BRANCH, BRANCH_WIDTH, D), lambda i: (0, 0, 0))]
                 + _gate_specs(tm, D)
                 + [pl.BlockSpec((1, N_BRANCH * D), lambda i: (0, 0)), pl.BlockSpec((tm, D), row)],
        out_specs=(pl.BlockSpec((tm, N_BRANCH * D), row), pl.BlockSpec((1, N_BRANCH * D), lambda i: (0, 0)),
                   *([pl.BlockSpec((tm, D), row)] * N_BRANCH), *([br] * N_BRANCH)),
        compiler_params=_cp(("arbitrary",)),
    )(oa, ob, oc, wb, *([proj] * (N_BRANCH * nh)), bg, dmerged)
    return res[0], res[1], list(res[2:2 + N_BRANCH]), list(res[2 + N_BRANCH:])


X_SCALE = 1.0 / math.sqrt(X_HEAD_DIM)
X_W = X_HEADS * X_HEAD_DIM


def _xattn_fwd(q, kv, *, name):
    S = q.shape[0]
    M = kv.shape[0]
    tq = _tile(S, 512)

    def body(q_ref, kv_ref, o_ref, lse_ref):
        for h in range(X_HEADS):
            cols = slice(h * LANES, (h + 1) * LANES)
            s = _dot(q_ref[:, cols], kv_ref[:, cols], _NT) * X_SCALE
            m = jnp.max(s, axis=-1, keepdims=True)
            e = jnp.exp(s - m)
            l = jnp.sum(e, axis=-1, keepdims=True)
            p = (e * (1.0 / l)).astype(BF16)
            o_ref[:, cols] = _dot(p, kv_ref[:, X_W + h * LANES:X_W + (h + 1) * LANES]).astype(BF16)
            lse_ref[h] = m + jnp.log(l)

    return pl.pallas_call(
        body, name=name,
        out_shape=(jax.ShapeDtypeStruct((S, X_W), BF16), jax.ShapeDtypeStruct((X_HEADS, S, 1), F32)),
        grid=(S // tq,),
        in_specs=[pl.BlockSpec((tq, X_W), lambda i: (i, 0)), pl.BlockSpec((M, 2 * X_W), lambda i: (0, 0))],
        out_specs=(pl.BlockSpec((tq, X_W), lambda i: (i, 0)), pl.BlockSpec((X_HEADS, tq, 1), lambda i: (0, i, 0))),
        compiler_params=_cp(("parallel",)),
    )(q, kv)


def _xattn_bwd(q, kv, lse, do, *, name):
    S = q.shape[0]
    M = kv.shape[0]
    tq = _tile(S, 512)

    def body(q_ref, kv_ref, lse_ref, do_ref, dq_ref, dkv_ref):
        @pl.when(pl.program_id(0) == 0)
        def _():
            dkv_ref[...] = jnp.zeros_like(dkv_ref)

        for h in range(X_HEADS):
            cols = slice(h * LANES, (h + 1) * LANES)
            vcols = slice(X_W + h * LANES, X_W + (h + 1) * LANES)
            qh, kh, vh = q_ref[:, cols], kv_ref[:, cols], kv_ref[:, vcols]
            doh = do_ref[:, cols].astype(BF16)
            p = jnp.exp(_dot(qh, kh, _NT) * X_SCALE - lse_ref[h])
            dp = _dot(doh, vh, _NT)
            delta = jnp.sum(p * dp, axis=-1, keepdims=True)
            ds = (p * (dp - delta) * X_SCALE).astype(BF16)
            dq_ref[:, cols] = _dot(ds, kh).astype(BF16)
            dkv_ref[:, cols] += _dot(ds, qh, _TN)
            dkv_ref[:, vcols] += _dot(p.astype(BF16), doh, _TN)

    q_spec = pl.BlockSpec((tq, X_W), lambda i: (i, 0))
    return pl.pallas_call(
        body, name=name,
        out_shape=(jax.ShapeDtypeStruct((S, X_W), BF16), jax.ShapeDtypeStruct((M, 2 * X_W), F32)),
        grid=(S // tq,),
        in_specs=[q_spec, pl.BlockSpec((M, 2 * X_W), lambda i: (0, 0)),
                  pl.BlockSpec((X_HEADS, tq, 1), lambda i: (0, i, 0)), q_spec],
        out_specs=(q_spec, pl.BlockSpec((M, 2 * X_W), lambda i: (0, 0))),
        compiler_params=_cp(("arbitrary",)),
    )(q, kv, lse, do)


def _shift_down(h, row):
    return jnp.where(row == 0, 0.0, pltpu.roll(h, 1, 0))


def _shift_up(h, row, S):
    return jnp.where(row == S - 1, 0.0, pltpu.roll(h, S - 1, 0))


def _conv3(h, ck, cb, row, S):
    return _shift_down(h, row) * ck[0:1] + h * ck[1:2] + _shift_up(h, row, S) * ck[2:3] + cb


def _conv_act_fwd(h, ck, cb, *, name):
    S, F2 = h.shape
    F = F2 // 2
    nt = F // LANES

    def body(ha_ref, hb_ref, cka_ref, ckb_ref, cba_ref, cbb_ref, o_ref):
        row = lax.broadcasted_iota(jnp.int32, (S, LANES), 0)
        a = _conv3(ha_ref[...], cka_ref[...], cba_ref[...], row, S)
        b = _conv3(hb_ref[...], ckb_ref[...], cbb_ref[...], row, S)
        o_ref[...] = (_gelu(a) * b).astype(BF16)

    ca = lambda j: (0, j)
    cbi = lambda j: (0, j + nt)
    return pl.pallas_call(
        body, name=name,
        out_shape=jax.ShapeDtypeStruct((S, F), BF16),
        grid=(nt,),
        in_specs=[pl.BlockSpec((S, LANES), ca), pl.BlockSpec((S, LANES), cbi), pl.BlockSpec((3, LANES), ca),
                  pl.BlockSpec((3, LANES), cbi), pl.BlockSpec((1, LANES), ca), pl.BlockSpec((1, LANES), cbi)],
        out_specs=pl.BlockSpec((S, LANES), ca),
        compiler_params=_cp(("parallel",)),
    )(h, h, ck, ck, cb, cb)


def _conv_act_bwd(h, ck, cb, dact, *, name):
    S, F2 = h.shape
    F = F2 // 2
    nt = F // LANES

    def body(ha_ref, hb_ref, cka_ref, ckb_ref, cba_ref, cbb_ref, d_ref,
             dha_ref, dhb_ref, dcka_ref, dckb_ref, dcba_ref, dcbb_ref):
        row = lax.broadcasted_iota(jnp.int32, (S, LANES), 0)
        ha, hb = ha_ref[...], hb_ref[...]
        cka, ckb = cka_ref[...], ckb_ref[...]
        a = _conv3(ha, cka, cba_ref[...], row, S)
        b = _conv3(hb, ckb, cbb_ref[...], row, S)
        d = d_ref[...]
        ga, ga_grad = _gelu_and_grad(a)
        da = d * b * ga_grad
        db = d * ga
        for dd, hh, ck_, dh_ref, dck_ref, dcb_ref in ((da, ha, cka, dha_ref, dcka_ref, dcba_ref),
                                                      (db, hb, ckb, dhb_ref, dckb_ref, dcbb_ref)):
            dcb_ref[...] = jnp.sum(dd, axis=0, keepdims=True)
            dck_ref[0:1, :] = jnp.sum(dd * _shift_down(hh, row), axis=0, keepdims=True)
            dck_ref[1:2, :] = jnp.sum(dd * hh, axis=0, keepdims=True)
            dck_ref[2:3, :] = jnp.sum(dd * _shift_up(hh, row, S), axis=0, keepdims=True)
            dh = _shift_up(dd, row, S) * ck_[0:1] + dd * ck_[1:2] + _shift_down(dd, row) * ck_[2:3]
            dh_ref[...] = dh.astype(BF16)

    ca = lambda j: (0, j)
    cbi = lambda j: (0, j + nt)
    col = pl.BlockSpec((S, LANES), ca)
    return pl.pallas_call(
        body, name=name,
        out_shape=(jax.ShapeDtypeStruct((S, F), BF16), jax.ShapeDtypeStruct((S, F), BF16),
                   jax.ShapeDtypeStruct((3, F), F32), jax.ShapeDtypeStruct((3, F), F32),
                   jax.ShapeDtypeStruct((1, F), F32), jax.ShapeDtypeStruct((1, F), F32)),
        grid=(nt,),
        in_specs=[col, pl.BlockSpec((S, LANES), cbi), pl.BlockSpec((3, LANES), ca), pl.BlockSpec((3, LANES), cbi),
                  pl.BlockSpec((1, LANES), ca), pl.BlockSpec((1, LANES), cbi), col],
        out_specs=(col, col, pl.BlockSpec((3, LANES), ca), pl.BlockSpec((3, LANES), ca),
                   pl.BlockSpec((1, LANES), ca), pl.BlockSpec((1, LANES), ca)),
        compiler_params=_cp(("parallel",)),
    )(h, h, ck, ck, cb, cb, dact)


def _layer_fwd(x, xb, memb, w, shards, tabs, seg, l):
    n = lambda s: f"L{l}_{s}"
    w = dict(w)
    qg2 = jnp.tile(w["b_q_gain"], 2)[None, :]
    kg2 = jnp.tile(w["b_k_gain"], 2)[None, :]
    proj = _mm(xb, w["w_in"], tb=True, name=n("proj"))
    aq, ak4, av4, bq, bk4, bv4 = _prep(proj, tabs, qg2, kg2, seg, name=n("prep"))
    oa, lse_a = _attn_win_fwd(aq, ak4, av4, w["a_sink"], name=n("attn_win"))
    gather = [(shards[k], l) for k in GATHERED_LATE] + ([(shards["w_in"], l + 1)] if l + 1 < DEPTH else [])
    ob, lse_b, gathered = _attn_dense_fwd(bq, bk4, bv4, gather=gather, name=n("attn_dense"))
    for k, g in zip(GATHERED_LATE, gathered):
        w[k] = _unshard(g, GATHER_AXIS[k])
    w_in_next = gathered[len(GATHERED_LATE)] if l + 1 < DEPTH else None
    oc = _gmlp_fwd(proj, w["c_ws"], w["c_bs3"], w["c_ln_g"], w["c_ln_b"], name=n("gmlp"))
    merged = _merge_fwd(oa, ob, oc, w["w_branch"], proj, w["b_gate"], name=n("merge"))
    x1, x1b, xh1, rs1 = _mm_res_ln(merged, w["w_mix_out"], x, w["ln1_g"], w["ln1_b"], name=n("mix_ln1"))
    xq = _mm(x1b, w["x_wq"], out_dtype=BF16, name=n("xq"))
    xkv = _mm(memb, w["x_wkv"], out_dtype=BF16, name=n("xkv"))
    xo, lse_x = _xattn_fwd(xq, xkv, name=n("xattn"))
    x2, x2b, xh2, rs2 = _mm_res_ln(xo, w["x_wo"], x1, w["ln2_g"], w["ln2_b"], name=n("xo_ln2"))
    h = _mm(x2b, w["f_w_up"], tb=True, name=n("ffn_up"))
    act = _conv_act_fwd(h, w["f_conv_k"], w["f_conv_b"], name=n("conv_act"))
    x3, x3b, xh3, rs3 = _mm_res_ln(act, w["f_w_down"], x2, w["ln3_g"], w["ln3_b"], name=n("down_ln3"))
    saved = dict(xb=xb, proj=proj, aq=aq, ak4=ak4, av4=av4, bq=bq, bk4=bk4, bv4=bv4, lse_a=lse_a, lse_b=lse_b,
                 oa=oa, ob=ob, oc=oc, merged=merged, xh1=xh1, rs1=rs1, x1b=x1b, xq=xq, xkv=xkv, xo=xo, lse_x=lse_x,
                 xh2=xh2, rs2=rs2, x2b=x2b, h=h, act=act, xh3=xh3, rs3=rs3, qg2=qg2, kg2=kg2)
    return x3, x3b, saved, w, w_in_next


def _layer_bwd(top, memb, w, sv, tabs, seg, l, ln_below, dw_in_above, recv):
    n = lambda s: f"L{l}_{s}"
    g = {}
    big = {}
    recv = dict(recv)

    def dw(key, a, b, tag):
        big[key] = _mm(a, b, ta=True, out_dtype=BF16, name=n(tag))

    def dw_t(key, segments, x, tag):
        total, row = sum(s.shape[1] for s in segments), 0
        for i, s in enumerate(segments):
            dw_rows(key, s, x, row, total, f"{tag}{i}")
            row += s.shape[1]

    def dw_rows(key, seg, x, row, total, tag):
        buf = big.get(key, jax.ShapeDtypeStruct((total, x.shape[1]), BF16))
        big[key] = _mm(seg, x, ta=True, into=(buf, row), name=n(tag))

    dz3, dz3b, g["ln3_g"], g["ln3_b"] = top
    dw("f_w_down", sv["act"], dz3b, "dw_down")
    dact = _mm(dz3b, w["f_w_down"], tb=True, name=n("dact"))
    dha, dhb, dcka, dckb, dcba, dcbb = _conv_act_bwd(sv["h"], w["f_conv_k"], w["f_conv_b"], dact, name=n("conv_act_bwd"))
    g["f_conv_k"] = jnp.concatenate([dcka, dckb], axis=1)
    g["f_conv_b"] = jnp.concatenate([dcba, dcbb], axis=1)[0]
    dw_t("f_w_up", [dha, dhb], sv["x2b"], "dw_up")
    dz2, dz2b, g["ln2_g"], g["ln2_b"] = _mm([dha, dhb], w["f_w_up"], res=dz3, res_scale=ALPHA,
                                            ln_bwd=(sv["xh2"], sv["rs2"], w["ln2_g"]), name=n("dx2_ln2"))
    dw("x_wo", sv["xo"], dz2b, "dw_xo")
    dxo = _mm(dz2b, w["x_wo"], tb=True, out_dtype=BF16, name=n("dxo"))
    dxq, dxkv = _xattn_bwd(sv["xq"], sv["xkv"], sv["lse_x"], dxo, name=n("xattn_bwd"))
    dw("x_wq", sv["x1b"], dxq, "dw_xq")
    dw("x_wkv", memb, dxkv, "dw_xkv")
    dz1, dz1b, g["ln1_g"], g["ln1_b"] = _mm(dxq, w["x_wq"], tb=True, res=dz2, res_scale=ALPHA,
                                            ln_bwd=(sv["xh1"], sv["rs1"], w["ln1_g"]), name=n("dx1_ln1"))
    dw("w_mix_out", sv["merged"], dz1b, "dw_mix")
    dmerged = _mm(dz1b, w["w_mix_out"], tb=True, name=n("dmerged"))
    dgl, dbg, dt, dbr = _merge_bwd(sv["oa"], sv["ob"], sv["oc"], w["w_branch"], sv["proj"], w["b_gate"], dmerged,
                                   name=n("merge_bwd"))
    g["b_gate"] = dbg[0]
    for i, k in enumerate(("oa", "ob", "oc")):
        dw(f"w_branch{i}", sv[k], dt[i], f"dw_branch{i}")
    big["w_branch"] = jnp.stack([big.pop(f"w_branch{i}") for i in range(N_BRANCH)])
    dqa, dka, dva, dsink = _attn_win_bwd(sv["aq"], sv["ak4"], sv["av4"], w["a_sink"], sv["lse_a"], dbr[0],
                                         name=n("attn_win_bwd"))
    g["a_sink"] = dsink[:, 0]
    sent = [k for k in BIG if k != "w_in"]
    scatter = [(_reshard(big[k], BIG_AXIS[k]), recv[k], l, None) for k in sent]
    if dw_in_above is not None:
        sent.append("w_in")
        scatter.append((_reshard(dw_in_above, BIG_AXIS["w_in"]), recv["w_in"], l + 1, None))
    d_in = w["w_in"].shape[0]
    early = ln_below is None and (d_in // 2) >= COL_GATE
    if early:
        dw_rows("w_in", dgl, sv["xb"], COL_GATE, d_in, "dw_in2")
        own_recv = jax.ShapeDtypeStruct((1, N_DEV, d_in // N_DEV, w["w_in"].shape[1]), BF16)
        sent.append("own_w_in")
        scatter.append((_reshard(big["w_in"], BIG_AXIS["w_in"]), own_recv, 0, 1))
    dqb, dkb, dvb, got = _attn_dense_bwd(sv["bq"], sv["bk4"], sv["bv4"], sv["lse_b"], dbr[1], scatter=scatter,
                                         name=n("attn_dense_bwd"))
    recv.update(zip(sent, got))
    dcz, g["c_ws"], dbs3, dlg, dlb = _gmlp_bwd(sv["proj"], dbr[2], w["c_ws"], w["c_bs3"], w["c_ln_g"], w["c_ln_b"],
                                               name=n("gmlp_bwd"))
    g["c_bs"] = dbs3[:, :, 0]
    g["c_ln_g"], g["c_ln_b"] = dlg[0], dlb[0]
    dqkv, dqg, dkg = _unprep(dqa, dka, dva, dqb, dkb, dvb, sv["proj"], tabs, sv["qg2"], sv["kg2"], seg, name=n("unprep"))
    g["b_q_gain"] = dqg[0, :HEAD_DIM] + dqg[0, HEAD_DIM:]
    g["b_k_gain"] = dkg[0, :HEAD_DIM] + dkg[0, HEAD_DIM:]
    if early:
        dw_rows("w_in", dqkv, sv["xb"], 0, d_in, "dw_in0")
        dw_rows("w_in", dcz, sv["xb"], dqkv.shape[1], d_in, "dw_in1")
        dx0, (own,) = _mm([dqkv, dcz, dgl], w["w_in"], res=dz1, res_scale=ALPHA, name=n("dx0"),
                          scatter=[(_reshard(big.pop("w_in"), BIG_AXIS["w_in"]), recv.pop("own_w_in"), 0, 0)])
        recv["w_in"] = recv["w_in"].at[l].set(own[0])
    elif ln_below is None:
        dw_t("w_in", [dqkv, dcz, dgl], sv["xb"], "dw_in")
        dx0, (recv["w_in"],) = _mm([dqkv, dcz, dgl], w["w_in"], res=dz1, res_scale=ALPHA, name=n("dx0"),
                                   scatter=[(_reshard(big.pop("w_in"), BIG_AXIS["w_in"]), recv["w_in"], l, None)])
    else:
        dw_t("w_in", [dqkv, dcz, dgl], sv["xb"], "dw_in")
        dx0 = _mm([dqkv, dcz, dgl], w["w_in"], res=dz1, res_scale=ALPHA, name=n("dx0"))
        dx0 = _ln_bwd(dx0, *ln_below, name=n("ln_bwd_below"))
    for k in ("ln1_g", "ln1_b", "ln2_g", "ln2_b", "ln3_g", "ln3_b"):
        g[k] = g[k][0]
    return dx0, g, big.get("w_in"), recv


WEIGHTS = ("w_in", "b_gate", "a_sink", "b_q_gain", "b_k_gain", "c_ln_g", "c_ln_b", "c_ws", "c_bs", "w_branch",
           "w_mix_out", "ln1_g", "ln1_b", "x_wq", "x_wkv", "x_wo", "ln2_g", "ln2_b", "f_w_up", "f_conv_k",
           "f_conv_b", "f_w_down", "ln3_g", "ln3_b")
TRANSPOSED = ("w_in", "f_w_up")
BIG_AXIS = {"w_in": 0, "w_branch": 2, "w_mix_out": 0, "x_wq": 0, "x_wkv": 0, "x_wo": 1, "f_w_up": 0, "f_w_down": 0}
BIG = tuple(BIG_AXIS)
GATHERED = BIG + ("f_conv_k",)
GATHERED_LATE = tuple(k for k in GATHERED if k != "w_in")
GATHER_AXIS = dict(BIG_AXIS, f_conv_k=1)
SMALL = tuple(k for k in WEIGHTS if k not in GATHERED)


def _unshard(g, axis):
    t = jnp.moveaxis(g, 0, axis)
    return t.reshape(t.shape[:axis] + (t.shape[axis] * t.shape[axis + 1],) + t.shape[axis + 2:])


def _reshard(full, axis):
    t = full.reshape(full.shape[:axis] + (N_DEV, full.shape[axis] // N_DEV) + full.shape[axis + 1:])
    return jnp.moveaxis(t, axis, 0)


def _small_weights(small, l):
    w = {k: v[l] for k, v in small.items()}
    for k in ("c_ln_g", "c_ln_b", "ln1_g", "ln1_b", "ln2_g", "ln2_b", "ln3_g", "ln3_b", "b_gate", "f_conv_b"):
        w[k] = w[k][None, :]
    w["c_bs3"] = w["c_bs"][:, :, None]
    w["c_ws"] = w["c_ws"].astype(BF16)
    return w


def _local_step(x, mem, target, small, shards):
    S = x.shape[0]
    tabs = _rope_tables(S)
    seg = _seg_matrix()
    memb = mem.astype(BF16)
    xb = x.astype(BF16)
    saved, weights = [], []
    w_in_g = _gather_call([(shards["w_in"], 0)], name="gather_w_in_L0")[0]
    for l in range(DEPTH):
        w = dict(_small_weights(small, l), w_in=_unshard(w_in_g, GATHER_AXIS["w_in"]))
        x, xb, sv, w, w_in_g = _layer_fwd(x, xb, memb, w, shards, tabs, seg, l)
        saved.append(sv)
        weights.append(w)
    dy, loss = _loss_head(x, target, name="loss_head")
    grads = [None] * DEPTH
    recv = {k: jax.ShapeDtypeStruct((DEPTH, N_DEV) + shards[k].shape[1:], BF16) for k in BIG}
    dw_in = None
    last_ln = lambda l: (saved[l]["xh3"], saved[l]["rs3"], weights[l]["ln3_g"])
    top = _ln_bwd(dy, *last_ln(DEPTH - 1), name="ln_bwd_top")
    for l in reversed(range(DEPTH)):
        top, grads[l], dw_in, recv = _layer_bwd(top, memb, weights[l], saved[l], tabs, seg, l,
                                                last_ln(l - 1) if l > 0 else None, dw_in, recv)
    return loss, top, grads, [recv[k] for k in BIG]


PACK_W = 1024


def _gather_call(gather, *, name):
    na = len(gather)

    def body(*refs):
        start, finish = _gather_plan([(refs[a], gather[a][1], refs[na + a]) for a in range(na)], *refs[2 * na:])
        start()
        finish()

    return list(pl.pallas_call(
        body, name=name,
        out_shape=[_gathered_shape(x) for x, _ in gather],
        in_specs=[_ANY] * na, out_specs=[_ANY] * na,
        scratch_shapes=_comm_scratch(na),
    )(*[x for x, _ in gather]))


def _scatter_io(scatter):
    held = [a for a, e in enumerate(scatter) if not isinstance(e[1], jax.ShapeDtypeStruct)]
    return ([e[0] for e in scatter] + [scatter[a][1] for a in held],
            [jax.ShapeDtypeStruct(e[1].shape, e[1].dtype) for e in scatter], held)


def _sum_parts(parts, *, name):
    P, R, C = parts.shape
    tr = _tile(R, 64, align=8)

    def body(p_ref, o_ref):
        g = p_ref[0].astype(F32)
        for s in range(1, P):
            g = g + p_ref[s].astype(F32)
        o_ref[...] = g

    return pl.pallas_call(
        body, name=name, out_shape=jax.ShapeDtypeStruct((R, C), F32), grid=(R // tr,),
        in_specs=[pl.BlockSpec((P, tr, C), lambda i: (0, i, 0))], out_specs=pl.BlockSpec((tr, C), lambda i: (i, 0)),
        compiler_params=_cp(("parallel",)),
    )(parts)


ADAM_BLOCK_ELEMS = 512 * 1024


def _adamw(parts, w, m, v, *, name):
    L, P, R, C = parts.shape
    assert w.shape == (L, R, C), (parts.shape, w.shape)
    tr = _tile(R, max(16, ADAM_BLOCK_ELEMS // C), align=16)

    def body(p_ref, w_ref, m_ref, v_ref, g_ref, d_ref, nm_ref, nv_ref):
        g = p_ref[0].astype(F32)
        for s in range(1, P):
            g = g + p_ref[s].astype(F32)
        g_ref[...] = g
        d_ref[...], nm_ref[...], nv_ref[...] = _adam_update(g, w_ref[...], m_ref[...], v_ref[...])

    blk = pl.BlockSpec((None, tr, C), lambda l, i: (l, i, 0))
    shp = jax.ShapeDtypeStruct((L, R, C), F32)
    return pl.pallas_call(
        body, name=name, out_shape=(shp, shp, shp, shp), grid=(L, R // tr),
        in_specs=[pl.BlockSpec((None, P, tr, C), lambda l, i: (l, 0, i, 0)), blk, blk, blk],
        out_specs=(blk, blk, blk, blk),
        compiler_params=_cp(("parallel", "parallel")),
    )(parts, w, m, v)


def _adam_update(g, w, m, v):
    nm = ADAM_B1 * m + (1.0 - ADAM_B1) * g
    nv = ADAM_B2 * v + (1.0 - ADAM_B2) * (g * g)
    m_hat = nm / (1.0 - ADAM_B1 ** ADAM_STEP)
    v_hat = nv / (1.0 - ADAM_B2 ** ADAM_STEP)
    return -ADAM_LR * (m_hat / (jnp.sqrt(v_hat) + ADAM_EPS) + ADAM_WD * w), nm, nv


def _adamw_small(gs, ws, ms, vs, *, name):
    n = len(gs)

    def body(*refs):
        for a in range(n):
            g, w, m, v = (refs[k * n + a][...] for k in range(4))
            d, nm, nv = _adam_update(g, w, m, v)
            refs[4 * n + a][...] = d
            refs[5 * n + a][...] = nm
            refs[6 * n + a][...] = nv

    shapes = [jax.ShapeDtypeStruct(w.shape, F32) for w in ws]
    res = pl.pallas_call(body, name=name, out_shape=shapes * 3, compiler_params=_cp())(*gs, *ws, *ms, *vs)
    return res[:n], res[n:2 * n], res[2 * n:]


def _pad_rows(vec, width, row_align):
    n = vec.shape[0]
    rows = -(-n // width)
    rows = -(-rows // row_align) * row_align
    return jnp.pad(vec, (0, rows * width - n)).reshape(rows, width)


def kernel(x, mem, w_in, b_gate, a_sink, b_q_gain, b_k_gain, c_ln_g, c_ln_b, c_ws, c_bs, w_branch, w_mix_out, ln1_g, ln1_b, x_wq, x_wkv, x_wo, ln2_g, ln2_b, f_w_up, f_conv_k, f_conv_b, f_w_down, ln3_g, ln3_b, loss_target, m_w_in, m_b_gate, m_a_sink, m_b_q_gain, m_b_k_gain, m_c_ln_g, m_c_ln_b, m_c_ws, m_c_bs, m_w_branch, m_w_mix_out, m_ln1_g, m_ln1_b, m_x_wq, m_x_wkv, m_x_wo, m_ln2_g, m_ln2_b, m_f_w_up, m_f_conv_k, m_f_conv_b, m_f_w_down, m_ln3_g, m_ln3_b, v_w_in, v_b_gate, v_a_sink, v_b_q_gain, v_b_k_gain, v_c_ln_g, v_c_ln_b, v_c_ws, v_c_bs, v_w_branch, v_w_mix_out, v_ln1_g, v_ln1_b, v_x_wq, v_x_wkv, v_x_wo, v_ln2_g, v_ln2_b, v_f_w_up, v_f_conv_k, v_f_conv_b, v_f_w_down, v_ln3_g, v_ln3_b):
    w = dict(w_in=w_in, b_gate=b_gate, a_sink=a_sink, b_q_gain=b_q_gain, b_k_gain=b_k_gain, c_ln_g=c_ln_g,
             c_ln_b=c_ln_b, c_ws=c_ws, c_bs=c_bs, w_branch=w_branch, w_mix_out=w_mix_out, ln1_g=ln1_g, ln1_b=ln1_b,
             x_wq=x_wq, x_wkv=x_wkv, x_wo=x_wo, ln2_g=ln2_g, ln2_b=ln2_b, f_w_up=f_w_up, f_conv_k=f_conv_k,
             f_conv_b=f_conv_b, f_w_down=f_w_down, ln3_g=ln3_g, ln3_b=ln3_b)
    m = dict(w_in=m_w_in, b_gate=m_b_gate, a_sink=m_a_sink, b_q_gain=m_b_q_gain, b_k_gain=m_b_k_gain,
             c_ln_g=m_c_ln_g, c_ln_b=m_c_ln_b, c_ws=m_c_ws, c_bs=m_c_bs, w_branch=m_w_branch, w_mix_out=m_w_mix_out,
             ln1_g=m_ln1_g, ln1_b=m_ln1_b, x_wq=m_x_wq, x_wkv=m_x_wkv, x_wo=m_x_wo, ln2_g=m_ln2_g, ln2_b=m_ln2_b,
             f_w_up=m_f_w_up, f_conv_k=m_f_conv_k, f_conv_b=m_f_conv_b, f_w_down=m_f_w_down, ln3_g=m_ln3_g,
             ln3_b=m_ln3_b)
    v = dict(w_in=v_w_in, b_gate=v_b_gate, a_sink=v_a_sink, b_q_gain=v_b_q_gain, b_k_gain=v_b_k_gain,
             c_ln_g=v_c_ln_g, c_ln_b=v_c_ln_b, c_ws=v_c_ws, c_bs=v_c_bs, w_branch=v_w_branch, w_mix_out=v_w_mix_out,
             ln1_g=v_ln1_g, ln1_b=v_ln1_b, x_wq=v_x_wq, x_wkv=v_x_wkv, x_wo=v_x_wo, ln2_g=v_ln2_g, ln2_b=v_ln2_b,
             f_w_up=v_f_w_up, f_conv_k=v_f_conv_k, f_conv_b=v_f_conv_b, f_w_down=v_f_w_down, ln3_g=v_ln3_g,
             ln3_b=v_ln3_b)
    me = 4 * lax.axis_index("x") + 2 * lax.axis_index("y") + lax.axis_index("c")

    def held(k, t):
        return jnp.swapaxes(t, 1, 2) if k in TRANSPOSED else t

    shards = dict({k: held(k, w[k]).astype(BF16) for k in BIG}, f_conv_k=w["f_conv_k"])
    loss, grad_x, grads, recvs = _local_step(x[0], mem[0], loss_target[0], {k: w[k] for k in SMALL}, shards)
    loss = lax.psum(loss[0, 0], ("x", "y", "c"))

    out_g, out_d, out_m, out_v = {}, {}, {}, {}
    for k, recv in zip(BIG, recvs):
        shp = held(k, w[k]).shape
        rc = (DEPTH, math.prod(shp[1:-1]), shp[-1])
        parts = recv.reshape((DEPTH, N_DEV) + rc[1:])
        g_, d_, m_, v_ = _adamw(parts, held(k, w[k]).reshape(rc), held(k, m[k]).reshape(rc),
                                held(k, v[k]).reshape(rc), name=f"adamw_{k}")
        out_g[k], out_d[k], out_m[k], out_v[k] = (held(k, t.reshape(shp)) for t in (g_, d_, m_, v_))

    small_all = SMALL + ("f_conv_k",)
    gfull = {k: jnp.stack([grads[l][k] for l in range(DEPTH)]) for k in small_all}

    def pack(d):
        return jnp.concatenate([_pad_rows(d[k].reshape(-1), PACK_W, 8) for k in small_all])

    def unpack(rows, like):
        out, r = {}, 0
        for k in small_all:
            nr = -(-like[k].size // (8 * PACK_W)) * 8
            out[k] = rows[r:r + nr].reshape(-1)[:like[k].size].reshape(like[k].shape)
            r += nr
        return out

    gathered = _gather_call([(pack(gfull)[None], 0)], name="gather_small_grads")[0]
    sg = unpack(_sum_parts(gathered, name="sum_small_grads"), gfull)
    width = w["f_conv_k"].shape[2]
    sg["f_conv_k"] = lax.dynamic_slice_in_dim(sg["f_conv_k"], me * width, width, axis=2)
    ud, um, uv = _adamw_small(*[[d[k] for k in small_all] for d in (sg, w, m, v)], name="adamw_small")
    for i, k in enumerate(small_all):
        out_g[k], out_d[k], out_m[k], out_v[k] = sg[k], ud[i], um[i], uv[i]

    return (loss, grad_x[None], *[out_g[k] for k in WEIGHTS], *[out_d[k] for k in WEIGHTS],
            *[out_m[k] for k in WEIGHTS], *[out_v[k] for k in WEIGHTS])
```
